```python
import jax, jax.numpy as jnp
from jax import lax
import numpy as np

D_MODEL = 1024
BATCH = 16
SEQ = 4096
DEPTH = 4

CHUNK = 128
A_WIDTH = D_MODEL
A_HEADS = 8
A_HEAD_DIM = A_WIDTH // A_HEADS
B_WIDTH = D_MODEL
CONV_WIDTH = 31
FFN_HIDDEN = 4 * D_MODEL
N_BRANCH = 2
N_MOD = 6
IN_COLS = 2 * A_WIDTH + 2 * B_WIDTH + N_BRANCH * D_MODEL
EPS = 1e-6

kernel_name = "hybrid_gmlp_conformer_gated_adaln"


def rmsnorm(x, g):
    xf = x.astype(jnp.float32)
    y = xf * lax.rsqrt(jnp.mean(xf * xf, axis=-1, keepdims=True) + EPS)
    return (y * g.astype(jnp.float32)).astype(x.dtype)


def layernorm(x, g, b):
    xf = x.astype(jnp.float32)
    mu = jnp.mean(xf, axis=-1, keepdims=True)
    xc = xf - mu
    var = jnp.mean(xc * xc, axis=-1, keepdims=True)
    y = xc * lax.rsqrt(var + EPS) * g.astype(jnp.float32) + b.astype(jnp.float32)
    return y.astype(x.dtype)


def chunked_spatial_gating(u, v, ln_g, ln_b, w_s, b_s):
    bsz, t, _ = v.shape
    v = layernorm(v, ln_g, ln_b)
    v = v.reshape(bsz, t // CHUNK, CHUNK, A_HEADS, A_HEAD_DIM)
    causal = jnp.tril(jnp.ones((CHUNK, CHUNK), dtype=bool))
    w = jnp.where(causal[None], w_s, 0).astype(v.dtype)
    s = jnp.einsum('hij,bnjhd->bnihd', w, v) + b_s.T.astype(v.dtype)[None, None, :, :, None]
    return u * s.reshape(bsz, t, A_WIDTH)


def conformer_conv(p, conv_w, conv_b, ln_g, ln_b):
    a, g = jnp.split(p, 2, axis=-1)
    z = a * jax.nn.sigmoid(g)
    z = lax.conv_general_dilated(
        z, conv_w.astype(z.dtype), window_strides=(1,),
        padding=[(CONV_WIDTH - 1, 0)],
        dimension_numbers=('NWC', 'WIO', 'NWC'),
        feature_group_count=B_WIDTH) + conv_b.astype(z.dtype)
    z = layernorm(z, ln_g, ln_b)
    return jax.nn.silu(z)


def _fwd_setup_inputs(seed: int = 0) -> dict:
    key = jax.random.key(seed)
    ks = jax.random.split(key, 24)
    f32 = jnp.float32
    L, D = DEPTH, D_MODEL

    def nrm(k, shape, scale):
        return jax.random.normal(k, shape, f32) * scale

    return {
        "x": nrm(ks[0], (BATCH, SEQ, D), 1.0),
        "c": nrm(ks[1], (BATCH, D), 1.0),
        "w_ada": nrm(ks[2], (L, D, N_MOD * D), 0.5 * D ** -0.5),
        "b_ada": nrm(ks[3], (L, N_MOD * D), 0.02),
        "norm1_g": 1.0 + nrm(ks[4], (L, D), 0.05),
        "w_in": nrm(ks[5], (L, D, IN_COLS), D ** -0.5),
        "a_ln_g": 1.0 + nrm(ks[6], (L, A_WIDTH), 0.05),
        "a_ln_b": nrm(ks[7], (L, A_WIDTH), 0.02),
        "a_ws": nrm(ks[8], (L, A_HEADS, CHUNK, CHUNK), CHUNK ** -0.5),
        "a_bs": 1.0 + nrm(ks[9], (L, A_HEADS, CHUNK), 0.1),
        "w_pa": nrm(ks[10], (L, A_WIDTH, D), A_WIDTH ** -0.5),
        "b_conv_w": nrm(ks[11], (L, CONV_WIDTH, 1, B_WIDTH), CONV_WIDTH ** -0.5),
        "b_conv_b": nrm(ks[12], (L, B_WIDTH), 0.02),
        "b_ln_g": 1.0 + nrm(ks[13], (L, B_WIDTH), 0.05),
        "b_ln_b": nrm(ks[14], (L, B_WIDTH), 0.02),
        "w_pb": nrm(ks[15], (L, B_WIDTH, D), B_WIDTH ** -0.5),
        "w_out": nrm(ks[16], (L, D, D), D ** -0.5),
        "norm2_g": 1.0 + nrm(ks[17], (L, D), 0.05),
        "w_ff1": nrm(ks[18], (L, D, FFN_HIDDEN), D ** -0.5),
        "w_ff2": nrm(ks[19], (L, FFN_HIDDEN, D), FFN_HIDDEN ** -0.5),
        "final_g": 1.0 + nrm(ks[20], (D,), 0.05),
    }


def _fwd_reference(x, c, w_ada, b_ada, norm1_g, w_in, a_ln_g, a_ln_b, a_ws, a_bs, w_pa,
              b_conv_w, b_conv_b, b_ln_g, b_ln_b, w_pb, w_out, norm2_g,
              w_ff1, w_ff2, final_g):
    split_at = [A_WIDTH, 2 * A_WIDTH, 2 * A_WIDTH + 2 * B_WIDTH,
                2 * A_WIDTH + 2 * B_WIDTH + D_MODEL]
    c_act = jax.nn.silu(c)
    for l in range(DEPTH):
        mod = (c_act @ w_ada[l] + b_ada[l])[:, None, :]
        sh1, sc1, gt1, sh2, sc2, gt2 = jnp.split(mod, N_MOD, axis=-1)

        h = rmsnorm(x, norm1_g[l]) * (1 + sc1) + sh1
        proj = h @ w_in[l]
        u, v, p_b, g_a, g_b = jnp.split(proj, split_at, axis=-1)
        y_a = chunked_spatial_gating(u, v, a_ln_g[l], a_ln_b[l], a_ws[l], a_bs[l]) @ w_pa[l]
        y_b = conformer_conv(p_b, b_conv_w[l], b_conv_b[l], b_ln_g[l], b_ln_b[l]) @ w_pb[l]
        merged = jax.nn.sigmoid(g_a) * y_a + jax.nn.sigmoid(g_b) * y_b
        x = x + gt1 * (merged @ w_out[l])

        h = rmsnorm(x, norm2_g[l]) * (1 + sc2) + sh2
        x = x + gt2 * (jnp.square(jax.nn.relu(h @ w_ff1[l])) @ w_ff2[l])

    return rmsnorm(x, final_g)


import jax as _jax
import jax.numpy as _jnp

TWIN_FORMAT = 'train_step'
FWD_PARAMS = ['x', 'c', 'w_ada', 'b_ada', 'norm1_g', 'w_in', 'a_ln_g', 'a_ln_b', 'a_ws', 'a_bs', 'w_pa', 'b_conv_w', 'b_conv_b', 'b_ln_g', 'b_ln_b', 'w_pb', 'w_out', 'norm2_g', 'w_ff1', 'w_ff2', 'final_g']
TWIN_WEIGHTS = ['w_ada', 'b_ada', 'norm1_g', 'w_in', 'a_ln_g', 'a_ln_b', 'a_ws', 'a_bs', 'w_pa', 'b_conv_w', 'b_conv_b', 'b_ln_g', 'b_ln_b', 'w_pb', 'w_out', 'norm2_g', 'w_ff1', 'w_ff2', 'final_g']
TWIN_DIFF_INPUT = 'x'
TWIN_INPUTS = ['x', 'c', 'w_ada', 'b_ada', 'norm1_g', 'w_in', 'a_ln_g', 'a_ln_b', 'a_ws', 'a_bs', 'w_pa', 'b_conv_w', 'b_conv_b', 'b_ln_g', 'b_ln_b', 'w_pb', 'w_out', 'norm2_g', 'w_ff1', 'w_ff2', 'final_g', 'loss_target', 'm_w_ada', 'm_b_ada', 'm_norm1_g', 'm_w_in', 'm_a_ln_g', 'm_a_ln_b', 'm_a_ws', 'm_a_bs', 'm_w_pa', 'm_b_conv_w', 'm_b_conv_b', 'm_b_ln_g', 'm_b_ln_b', 'm_w_pb', 'm_w_out', 'm_norm2_g', 'm_w_ff1', 'm_w_ff2', 'm_final_g', 'v_w_ada', 'v_b_ada', 'v_norm1_g', 'v_w_in', 'v_a_ln_g', 'v_a_ln_b', 'v_a_ws', 'v_a_bs', 'v_w_pa', 'v_b_conv_w', 'v_b_conv_b', 'v_b_ln_g', 'v_b_ln_b', 'v_w_pb', 'v_w_out', 'v_norm2_g', 'v_w_ff1', 'v_w_ff2', 'v_final_g']
TWIN_OUTPUTS = ['loss', 'grad_x', 'grad_w_ada', 'grad_b_ada', 'grad_norm1_g', 'grad_w_in', 'grad_a_ln_g', 'grad_a_ln_b', 'grad_a_ws', 'grad_a_bs', 'grad_w_pa', 'grad_b_conv_w', 'grad_b_conv_b', 'grad_b_ln_g', 'grad_b_ln_b', 'grad_w_pb', 'grad_w_out', 'grad_norm2_g', 'grad_w_ff1', 'grad_w_ff2', 'grad_final_g', 'delta_w_ada', 'delta_b_ada', 'delta_norm1_g', 'delta_w_in', 'delta_a_ln_g', 'delta_a_ln_b', 'delta_a_ws', 'delta_a_bs', 'delta_w_pa', 'delta_b_conv_w', 'delta_b_conv_b', 'delta_b_ln_g', 'delta_b_ln_b', 'delta_w_pb', 'delta_w_out', 'delta_norm2_g', 'delta_w_ff1', 'delta_w_ff2', 'delta_final_g', 'new_m_w_ada', 'new_m_b_ada', 'new_m_norm1_g', 'new_m_w_in', 'new_m_a_ln_g', 'new_m_a_ln_b', 'new_m_a_ws', 'new_m_a_bs', 'new_m_w_pa', 'new_m_b_conv_w', 'new_m_b_conv_b', 'new_m_b_ln_g', 'new_m_b_ln_b', 'new_m_w_pb', 'new_m_w_out', 'new_m_norm2_g', 'new_m_w_ff1', 'new_m_w_ff2', 'new_m_final_g', 'new_v_w_ada', 'new_v_b_ada', 'new_v_norm1_g', 'new_v_w_in', 'new_v_a_ln_g', 'new_v_a_ln_b', 'new_v_a_ws', 'new_v_a_bs', 'new_v_w_pa', 'new_v_b_conv_w', 'new_v_b_conv_b', 'new_v_b_ln_g', 'new_v_b_ln_b', 'new_v_w_pb', 'new_v_w_out', 'new_v_norm2_g', 'new_v_w_ff1', 'new_v_w_ff2', 'new_v_final_g']
TWIN_LEAF_KINDS = {'loss': 'loss', 'grad_x': 'grad_x', 'grad_w_ada': 'grad_w', 'grad_b_ada': 'grad_w', 'grad_norm1_g': 'grad_w', 'grad_w_in': 'grad_w', 'grad_a_ln_g': 'grad_w', 'grad_a_ln_b': 'grad_w', 'grad_a_ws': 'grad_w', 'grad_a_bs': 'grad_w', 'grad_w_pa': 'grad_w', 'grad_b_conv_w': 'grad_w', 'grad_b_conv_b': 'grad_w', 'grad_b_ln_g': 'grad_w', 'grad_b_ln_b': 'grad_w', 'grad_w_pb': 'grad_w', 'grad_w_out': 'grad_w', 'grad_norm2_g': 'grad_w', 'grad_w_ff1': 'grad_w', 'grad_w_ff2': 'grad_w', 'grad_final_g': 'grad_w', 'delta_w_ada': 'delta_w', 'delta_b_ada': 'delta_w', 'delta_norm1_g': 'delta_w', 'delta_w_in': 'delta_w', 'delta_a_ln_g': 'delta_w', 'delta_a_ln_b': 'delta_w', 'delta_a_ws': 'delta_w', 'delta_a_bs': 'delta_w', 'delta_w_pa': 'delta_w', 'delta_b_conv_w': 'delta_w', 'delta_b_conv_b': 'delta_w', 'delta_b_ln_g': 'delta_w', 'delta_b_ln_b': 'delta_w', 'delta_w_pb': 'delta_w', 'delta_w_out': 'delta_w', 'delta_norm2_g': 'delta_w', 'delta_w_ff1': 'delta_w', 'delta_w_ff2': 'delta_w', 'delta_final_g': 'delta_w', 'new_m_w_ada': 'new_m', 'new_m_b_ada': 'new_m', 'new_m_norm1_g': 'new_m', 'new_m_w_in': 'new_m', 'new_m_a_ln_g': 'new_m', 'new_m_a_ln_b': 'new_m', 'new_m_a_ws': 'new_m', 'new_m_a_bs': 'new_m', 'new_m_w_pa': 'new_m', 'new_m_b_conv_w': 'new_m', 'new_m_b_conv_b': 'new_m', 'new_m_b_ln_g': 'new_m', 'new_m_b_ln_b': 'new_m', 'new_m_w_pb': 'new_m', 'new_m_w_out': 'new_m', 'new_m_norm2_g': 'new_m', 'new_m_w_ff1': 'new_m', 'new_m_w_ff2': 'new_m', 'new_m_final_g': 'new_m', 'new_v_w_ada': 'new_v', 'new_v_b_ada': 'new_v', 'new_v_norm1_g': 'new_v', 'new_v_w_in': 'new_v', 'new_v_a_ln_g': 'new_v', 'new_v_a_ln_b': 'new_v', 'new_v_a_ws': 'new_v', 'new_v_a_bs': 'new_v', 'new_v_w_pa': 'new_v', 'new_v_b_conv_w': 'new_v', 'new_v_b_conv_b': 'new_v', 'new_v_b_ln_g': 'new_v', 'new_v_b_ln_b': 'new_v', 'new_v_w_pb': 'new_v', 'new_v_w_out': 'new_v', 'new_v_norm2_g': 'new_v', 'new_v_w_ff1': 'new_v', 'new_v_w_ff2': 'new_v', 'new_v_final_g': 'new_v'}


def _forward(args):
    return _fwd_reference(*[args[k] for k in FWD_PARAMS])


def _output_shape():
    out = _jax.eval_shape(lambda: _forward(_fwd_setup_inputs(0)))
    return out.shape, out.dtype

N_MICROBATCH = 1
ADAM_LR = 0.001
ADAM_B1 = 0.9
ADAM_B2 = 0.999
ADAM_EPS = 1e-08
ADAM_WD = 0.01
ADAM_STEP = 10
PER_EXAMPLE_BATCH_AXIS = {'x': 0, 'c': 0, 'loss_target': 0}
SHARED_INPUTS = []
_WEIGHT_DTYPES = {'w_ada': _jnp.float32, 'b_ada': _jnp.float32, 'norm1_g': _jnp.float32, 'w_in': _jnp.float32, 'a_ln_g': _jnp.float32, 'a_ln_b': _jnp.float32, 'a_ws': _jnp.float32, 'a_bs': _jnp.float32, 'w_pa': _jnp.float32, 'b_conv_w': _jnp.float32, 'b_conv_b': _jnp.float32, 'b_ln_g': _jnp.float32, 'b_ln_b': _jnp.float32, 'w_pb': _jnp.float32, 'w_out': _jnp.float32, 'norm2_g': _jnp.float32, 'w_ff1': _jnp.float32, 'w_ff2': _jnp.float32, 'final_g': _jnp.float32}
MOMENT_SCALE = {'w_ada': 1.731152e-01, 'b_ada': 3.135962e-01, 'norm1_g': 6.518534e-02, 'w_in': 2.916214e-02, 'a_ln_g': 3.070348e-02, 'a_ln_b': 2.944098e-02, 'a_ws': 2.967766e-02, 'a_bs': 4.440198e-02, 'w_pa': 5.426567e-02, 'b_conv_w': 2.520786e-02, 'b_conv_b': 4.884084e-02, 'b_ln_g': 3.398297e-02, 'b_ln_b': 3.470222e-02, 'w_pb': 2.604105e-02, 'w_out': 6.030684e-02, 'norm2_g': 1.073149e-01, 'w_ff1': 5.699780e-02, 'w_ff2': 1.235638e-01, 'final_g': 6.490610e+01}


def _to_microbatches(a, axis):
    t = _jnp.moveaxis(a, axis, 0)
    t = t.reshape((N_MICROBATCH, t.shape[0] // N_MICROBATCH) + t.shape[1:])
    return _jnp.moveaxis(t, 1, axis + 1)


def setup_inputs(seed: int = 0) -> dict:
    inp = _fwd_setup_inputs(seed)
    key = _jax.random.fold_in(_jax.random.key(seed), 7919)
    shape, _ = _output_shape()
    out = dict(inp)
    out["loss_target"] = _jax.random.normal(_jax.random.fold_in(key, 0), shape, _jnp.float32)
    for i, name in enumerate(TWIN_WEIGHTS):
        w = inp[name].astype(_jnp.float32)
        if MOMENT_SCALE is None:
            s = _jnp.sqrt(_jnp.mean(_jnp.square(w)) + 1e-30)
        else:
            s = MOMENT_SCALE[name]
        km, kv = _jax.random.split(_jax.random.fold_in(key, i + 1))
        out[name] = w
        out["m_" + name] = s * _jax.random.normal(km, w.shape, _jnp.float32)
        out["v_" + name] = (s * s) * _jax.random.uniform(kv, w.shape, _jnp.float32, 0.5, 1.5)
    if N_MICROBATCH > 1:
        for name, axis in PER_EXAMPLE_BATCH_AXIS.items():
            out[name] = _to_microbatches(out[name], axis)
    return {'x': out['x'], 'c': out['c'], 'w_ada': out['w_ada'], 'b_ada': out['b_ada'], 'norm1_g': out['norm1_g'], 'w_in': out['w_in'], 'a_ln_g': out['a_ln_g'], 'a_ln_b': out['a_ln_b'], 'a_ws': out['a_ws'], 'a_bs': out['a_bs'], 'w_pa': out['w_pa'], 'b_conv_w': out['b_conv_w'], 'b_conv_b': out['b_conv_b'], 'b_ln_g': out['b_ln_g'], 'b_ln_b': out['b_ln_b'], 'w_pb': out['w_pb'], 'w_out': out['w_out'], 'norm2_g': out['norm2_g'], 'w_ff1': out['w_ff1'], 'w_ff2': out['w_ff2'], 'final_g': out['final_g'], 'loss_target': out['loss_target'], 'm_w_ada': out['m_w_ada'], 'm_b_ada': out['m_b_ada'], 'm_norm1_g': out['m_norm1_g'], 'm_w_in': out['m_w_in'], 'm_a_ln_g': out['m_a_ln_g'], 'm_a_ln_b': out['m_a_ln_b'], 'm_a_ws': out['m_a_ws'], 'm_a_bs': out['m_a_bs'], 'm_w_pa': out['m_w_pa'], 'm_b_conv_w': out['m_b_conv_w'], 'm_b_conv_b': out['m_b_conv_b'], 'm_b_ln_g': out['m_b_ln_g'], 'm_b_ln_b': out['m_b_ln_b'], 'm_w_pb': out['m_w_pb'], 'm_w_out': out['m_w_out'], 'm_norm2_g': out['m_norm2_g'], 'm_w_ff1': out['m_w_ff1'], 'm_w_ff2': out['m_w_ff2'], 'm_final_g': out['m_final_g'], 'v_w_ada': out['v_w_ada'], 'v_b_ada': out['v_b_ada'], 'v_norm1_g': out['v_norm1_g'], 'v_w_in': out['v_w_in'], 'v_a_ln_g': out['v_a_ln_g'], 'v_a_ln_b': out['v_a_ln_b'], 'v_a_ws': out['v_a_ws'], 'v_a_bs': out['v_a_bs'], 'v_w_pa': out['v_w_pa'], 'v_b_conv_w': out['v_b_conv_w'], 'v_b_conv_b': out['v_b_conv_b'], 'v_b_ln_g': out['v_b_ln_g'], 'v_b_ln_b': out['v_b_ln_b'], 'v_w_pb': out['v_w_pb'], 'v_w_out': out['v_w_out'], 'v_norm2_g': out['v_norm2_g'], 'v_w_ff1': out['v_w_ff1'], 'v_w_ff2': out['v_w_ff2'], 'v_final_g': out['v_final_g']}


def _loss(weights, diff, rest, loss_target):
    with _jax.named_scope("forward"):
        args = {**rest, TWIN_DIFF_INPUT: diff, **{k: w.astype(_WEIGHT_DTYPES[k]) for k, w in weights.items()}}
        y = _forward(args)
    with _jax.named_scope("loss_head"):
        err = _jnp.square(y.astype(_jnp.float32) - loss_target)
        return 0.5 * _jnp.sum(_jnp.mean(err, axis=-1)) if err.ndim else 0.5 * err


def _adamw(w, g, m, v):
    m = ADAM_B1 * m + (1.0 - ADAM_B1) * g
    v = ADAM_B2 * v + (1.0 - ADAM_B2) * _jnp.square(g)
    m_hat = m / (1.0 - ADAM_B1 ** ADAM_STEP)
    v_hat = v / (1.0 - ADAM_B2 ** ADAM_STEP)
    delta = -ADAM_LR * (m_hat / (_jnp.sqrt(v_hat) + ADAM_EPS) + ADAM_WD * w)
    return delta, m, v


def reference(x, c, w_ada, b_ada, norm1_g, w_in, a_ln_g, a_ln_b, a_ws, a_bs, w_pa, b_conv_w, b_conv_b, b_ln_g, b_ln_b, w_pb, w_out, norm2_g, w_ff1, w_ff2, final_g, loss_target, m_w_ada, m_b_ada, m_norm1_g, m_w_in, m_a_ln_g, m_a_ln_b, m_a_ws, m_a_bs, m_w_pa, m_b_conv_w, m_b_conv_b, m_b_ln_g, m_b_ln_b, m_w_pb, m_w_out, m_norm2_g, m_w_ff1, m_w_ff2, m_final_g, v_w_ada, v_b_ada, v_norm1_g, v_w_in, v_a_ln_g, v_a_ln_b, v_a_ws, v_a_bs, v_w_pa, v_b_conv_w, v_b_conv_b, v_b_ln_g, v_b_ln_b, v_w_pb, v_w_out, v_norm2_g, v_w_ff1, v_w_ff2, v_final_g):
    given = dict(x=x, c=c, w_ada=w_ada, b_ada=b_ada, norm1_g=norm1_g, w_in=w_in, a_ln_g=a_ln_g, a_ln_b=a_ln_b, a_ws=a_ws, a_bs=a_bs, w_pa=w_pa, b_conv_w=b_conv_w, b_conv_b=b_conv_b, b_ln_g=b_ln_g, b_ln_b=b_ln_b, w_pb=w_pb, w_out=w_out, norm2_g=norm2_g, w_ff1=w_ff1, w_ff2=w_ff2, final_g=final_g, loss_target=loss_target, m_w_ada=m_w_ada, m_b_ada=m_b_ada, m_norm1_g=m_norm1_g, m_w_in=m_w_in, m_a_ln_g=m_a_ln_g, m_a_ln_b=m_a_ln_b, m_a_ws=m_a_ws, m_a_bs=m_a_bs, m_w_pa=m_w_pa, m_b_conv_w=m_b_conv_w, m_b_conv_b=m_b_conv_b, m_b_ln_g=m_b_ln_g, m_b_ln_b=m_b_ln_b, m_w_pb=m_w_pb, m_w_out=m_w_out, m_norm2_g=m_norm2_g, m_w_ff1=m_w_ff1, m_w_ff2=m_w_ff2, m_final_g=m_final_g, v_w_ada=v_w_ada, v_b_ada=v_b_ada, v_norm1_g=v_norm1_g, v_w_in=v_w_in, v_a_ln_g=v_a_ln_g, v_a_ln_b=v_a_ln_b, v_a_ws=v_a_ws, v_a_bs=v_a_bs, v_w_pa=v_w_pa, v_b_conv_w=v_b_conv_w, v_b_conv_b=v_b_conv_b, v_b_ln_g=v_b_ln_g, v_b_ln_b=v_b_ln_b, v_w_pb=v_w_pb, v_w_out=v_w_out, v_norm2_g=v_norm2_g, v_w_ff1=v_w_ff1, v_w_ff2=v_w_ff2, v_final_g=v_final_g)
    weights = {n: given[n] for n in TWIN_WEIGHTS}
    shared = {n: given[n] for n in SHARED_INPUTS}
    per_example = {n: given[n] for n in ['x', 'c']}
    grad_fn = _jax.value_and_grad(_loss, argnums=(0, 1))

    def one_microbatch(ex, loss_target):
        ex = dict(ex)
        diff = ex.pop(TWIN_DIFF_INPUT)
        return grad_fn(weights, diff, {**shared, **ex}, loss_target)

    if N_MICROBATCH == 1:
        loss, (grad_w, grad_x) = one_microbatch(per_example, given["loss_target"])
    else:
        def body(carry, xs):
            loss_sum, grad_sum = carry
            l_k, (gw_k, gx_k) = one_microbatch(xs[0], xs[1])
            with _jax.named_scope("update"):
                return (loss_sum + l_k, _jax.tree.map(_jnp.add, grad_sum, gw_k)), gx_k

        init = (_jnp.zeros((), _jnp.float32), _jax.tree.map(_jnp.zeros_like, weights))
        (loss, grad_w), grad_x = _jax.lax.scan(body, init, (per_example, given["loss_target"]))
    with _jax.named_scope("update"):
        delta_w, new_m, new_v = {}, {}, {}
        for n in TWIN_WEIGHTS:
            delta_w[n], new_m[n], new_v[n] = _adamw(weights[n], grad_w[n], given["m_" + n], given["v_" + n])
    return (loss, grad_x, *[grad_w[n] for n in TWIN_WEIGHTS], *[delta_w[n] for n in TWIN_WEIGHTS],
            *[new_m[n] for n in TWIN_WEIGHTS], *[new_v[n] for n in TWIN_WEIGHTS])
```

```python
import functools

import jax
import jax.numpy as jnp
from jax import lax
from jax.experimental import pallas as pl
from jax.experimental.pallas import tpu as pltpu

F32 = jnp.float32
MXU_DTYPE = jnp.bfloat16
ACT_DTYPE = jnp.bfloat16
WIRE_DTYPE = jnp.bfloat16

EPS = 1e-6
CHUNK = 128
HEADS = 8
CONV_TAPS = 31
HALO = 32
N_DEV = 8
N_CHIP = 4
ADAM_LR, ADAM_B1, ADAM_B2, ADAM_EPS, ADAM_WD, ADAM_STEP = 0.001, 0.9, 0.999, 1e-08, 0.01, 10

V7X_VMEM_BYTES = 64 * 1024 * 1024
VMEM_LIMIT = V7X_VMEM_BYTES * 3 // 4
TOKEN_TILE = 512
CONV_ROWS = 64
LANES = 128
MESH_ID = pl.DeviceIdType.MESH


def _params(sem=None):
    return pltpu.CompilerParams(dimension_semantics=sem, vmem_limit_bytes=VMEM_LIMIT)


def _dot(a, b):
    return jnp.dot(a.astype(MXU_DTYPE), b.astype(MXU_DTYPE), preferred_element_type=F32)


def _dot_nt(a, b):
    return lax.dot_general(a.astype(MXU_DTYPE), b.astype(MXU_DTYPE), (((1,), (1,)), ((), ())),
                           preferred_element_type=F32)


def _dot_tn(a, b):
    return lax.dot_general(a.astype(MXU_DTYPE), b.astype(MXU_DTYPE), (((0,), (0,)), ((), ())),
                           preferred_element_type=F32)


def _colsum(a):
    return jnp.sum(a, axis=0, keepdims=True)


def _rowmean(a):
    return jnp.mean(a, axis=-1, keepdims=True)


def _sigmoid(a):
    return 1.0 / (1.0 + jnp.exp(-a))


def _modnorm_fwd(x, g, sc, sh):
    r = lax.rsqrt(_rowmean(x * x) + EPS)
    return (x * r) * (g * (1.0 + sc)) + sh


def _modnorm_bwd(x, dh, g, sc):
    r = lax.rsqrt(_rowmean(x * x) + EPS)
    xn = x * r
    dxn = dh * (g * (1.0 + sc))
    dx = r * (dxn - xn * _rowmean(dxn * xn))
    return dx, _colsum(dh), _colsum(dh * xn)


def _ln_stats(v):
    mu = _rowmean(v)
    vc = v - mu
    rstd = lax.rsqrt(_rowmean(vc * vc) + EPS)
    return vc * rstd, rstd


def _ln_bwd(dy, vhat, rstd, g):
    dvh = dy * g
    return rstd * (dvh - _rowmean(dvh) - vhat * _rowmean(dvh * vhat))


def _causal_mask():
    row = lax.broadcasted_iota(jnp.int32, (CHUNK, CHUNK), 0)
    col = lax.broadcasted_iota(jnp.int32, (CHUNK, CHUNK), 1)
    return row >= col


def _shifted(win, width):
    return [win if r == 0 else pltpu.roll(win, width - r, 0) for r in range(8)]


def _my_place():
    return lax.axis_index("x"), lax.axis_index("y"), lax.axis_index("c")


def _all_to_all(x, name, reduce=False):
    n, rows, cols = x.shape
    assert n == N_DEV

    def body(x_ref, o_ref, *scratch):
        if reduce:
            land, send_sems, recv_sems = scratch
        else:
            land = o_ref
            send_sems, recv_sems = scratch
        mx, my, mc = _my_place()
        me = 4 * mx + 2 * my + mc
        land[me] = x_ref[me]
        copies = []
        for k in range(1, N_DEV):
            px = (mx + ((k >> 2) & 1)) % 2
            py = (my + ((k >> 1) & 1)) % 2
            pc = (mc + (k & 1)) % 2
            peer = 4 * px + 2 * py + pc
            cp = pltpu.make_async_remote_copy(
                src_ref=x_ref.at[peer], dst_ref=land.at[me],
                send_sem=send_sems.at[k - 1], recv_sem=recv_sems.at[k - 1],
                device_id=(px, py, pc), device_id_type=MESH_ID)
            cp.start()
            copies.append(cp)
        for cp in copies:
            cp.wait()
        if reduce:
            acc = land[0]
            for s in range(1, N_DEV):
                acc = acc + land[s]
            o_ref[...] = acc

    scratch = [pltpu.SemaphoreType.DMA((N_DEV - 1,)), pltpu.SemaphoreType.DMA((N_DEV - 1,))]
    if reduce:
        scratch = [pltpu.VMEM((N_DEV, rows, cols), x.dtype)] + scratch
        out_shape = jax.ShapeDtypeStruct((rows, cols), x.dtype)
    else:
        out_shape = jax.ShapeDtypeStruct(x.shape, x.dtype)
    return pl.pallas_call(
        body, name=name, out_shape=out_shape,
        in_specs=[pl.BlockSpec(memory_space=pltpu.VMEM)],
        out_specs=pl.BlockSpec(memory_space=pltpu.VMEM),
        scratch_shapes=scratch,
        compiler_params=pltpu.CompilerParams(vmem_limit_bytes=VMEM_LIMIT),
    )(x)


def _other_chips(mx, my):
    return [(1 - mx, my), (mx, 1 - my), (1 - mx, 1 - my)]


def _gather_quarters(parts, name):
    n = len(parts)

    def body(*refs):
        ins, outs = refs[:n], refs[n:2 * n]
        send_sems, recv_sems, local_sems = refs[2 * n:]
        mx, my, mc = _my_place()
        myq = 2 * mx + my
        copies = []
        for a in range(n):
            lc = pltpu.make_async_copy(ins[a], outs[a].at[myq], local_sems.at[a])
            lc.start()
            copies.append(lc)
            for k, (px, py) in enumerate(_other_chips(mx, my)):
                cp = pltpu.make_async_remote_copy(
                    src_ref=ins[a], dst_ref=outs[a].at[myq],
                    send_sem=send_sems.at[3 * a + k], recv_sem=recv_sems.at[3 * a + k],
                    device_id=(px, py, mc), device_id_type=MESH_ID)
                cp.start()
                copies.append(cp)
        for cp in copies:
            cp.wait()

    any_spec = pl.BlockSpec(memory_space=pl.ANY)
    return pl.pallas_call(
        body, name=name,
        out_shape=[jax.ShapeDtypeStruct((N_CHIP,) + p.shape, p.dtype) for p in parts],
        in_specs=[any_spec] * n, out_specs=[any_spec] * n,
        scratch_shapes=[pltpu.SemaphoreType.DMA((3 * n,)), pltpu.SemaphoreType.DMA((3 * n,)),
                        pltpu.SemaphoreType.DMA((n,))],
        compiler_params=pltpu.CompilerParams(has_side_effects=True),
    )(*parts)


def _scatter_quarters(grads, name):
    n = len(grads)

    def body(*refs):
        ins, outs = refs[:n], refs[n:2 * n]
        send_sems, recv_sems = refs[2 * n:]
        mx, my, mc = _my_place()
        copies = []
        for a in range(n):
            for k, (px, py) in enumerate(_other_chips(mx, my)):
                cp = pltpu.make_async_remote_copy(
                    src_ref=ins[a].at[2 * px + py], dst_ref=outs[a].at[k],
                    send_sem=send_sems.at[3 * a + k], recv_sem=recv_sems.at[3 * a + k],
                    device_id=(px, py, mc), device_id_type=MESH_ID)
                cp.start()
                copies.append(cp)
        for cp in copies:
            cp.wait()

    any_spec = pl.BlockSpec(memory_space=pl.ANY)
    return pl.pallas_call(
        body, name=name,
        out_shape=[jax.ShapeDtypeStruct((3,) + g.shape[1:], g.dtype) for g in grads],
        in_specs=[any_spec] * n, out_specs=[any_spec] * n,
        scratch_shapes=[pltpu.SemaphoreType.DMA((3 * n,)), pltpu.SemaphoreType.DMA((3 * n,))],
        compiler_params=pltpu.CompilerParams(has_side_effects=True),
    )(*grads)


def _swap_sibling(parts, name):
    n = len(parts)

    def body(*refs):
        ins, outs = refs[:n], refs[n:2 * n]
        send_sems, recv_sems = refs[2 * n:]
        mx, my, mc = _my_place()
        copies = []
        for a in range(n):
            cp = pltpu.make_async_remote_copy(
                src_ref=ins[a], dst_ref=outs[a], send_sem=send_sems.at[a], recv_sem=recv_sems.at[a],
                device_id=(mx, my, 1 - mc), device_id_type=MESH_ID)
            cp.start()
            copies.append(cp)
        for cp in copies:
            cp.wait()

    any_spec = pl.BlockSpec(memory_space=pl.ANY)
    return pl.pallas_call(
        body, name=name,
        out_shape=[jax.ShapeDtypeStruct(p.shape, p.dtype) for p in parts],
        in_specs=[any_spec] * n, out_specs=[any_spec] * n,
        scratch_shapes=[pltpu.SemaphoreType.DMA((n,)), pltpu.SemaphoreType.DMA((n,))],
        compiler_params=pltpu.CompilerParams(has_side_effects=True),
    )(*parts)


def _ada_forward(c_all, w_ada, b_ada3, myq):
    nl, d, cq = w_ada.shape
    nb = c_all.shape[0]

    def body(q_ref, c_ref, w_ref, b_ref, o_ref):
        c = c_ref[...]
        act = c * _sigmoid(c)
        o_ref[...] = _dot(act, w_ref[...]) + b_ref[...]

    return pl.pallas_call(
        body, name="ada_forward",
        out_shape=jax.ShapeDtypeStruct((nl, nb, cq), F32),
        grid_spec=pltpu.PrefetchScalarGridSpec(
            num_scalar_prefetch=1, grid=(nl,),
            in_specs=[pl.BlockSpec((nb, d), lambda l, q: (0, 0)),
                      pl.BlockSpec((None, d, cq), lambda l, q: (l, 0, 0)),
                      pl.BlockSpec((None, 1, cq), lambda l, q: (l, 0, q[0]))],
            out_specs=pl.BlockSpec((None, nb, cq), lambda l, q: (l, 0, 0))),
        compiler_params=_params(("arbitrary",)),
    )(myq, c_all, w_ada, b_ada3)


def _ada_backward(c_all, dmod_all, myq, cq):
    nb, d = c_all.shape
    nl = dmod_all.shape[0]
    full = dmod_all.shape[2]

    def body(q_ref, c_ref, dq_ref, dall_ref, gw_ref, gb_ref):
        c = c_ref[...]
        act = c * _sigmoid(c)
        gw_ref[...] = _dot_tn(act, dq_ref[...])
        gb_ref[...] = _colsum(dall_ref[...])

    return pl.pallas_call(
        body, name="ada_backward",
        out_shape=[jax.ShapeDtypeStruct((nl, d, cq), F32), jax.ShapeDtypeStruct((nl, 1, full), F32)],
        grid_spec=pltpu.PrefetchScalarGridSpec(
            num_scalar_prefetch=1, grid=(nl,),
            in_specs=[pl.BlockSpec((nb, d), lambda l, q: (0, 0)),
                      pl.BlockSpec((None, nb, cq), lambda l, q: (l, 0, q[0])),
                      pl.BlockSpec((None, nb, full), lambda l, q: (l, 0, 0))],
            out_specs=[pl.BlockSpec((None, d, cq), lambda l, q: (l, 0, 0)),
                       pl.BlockSpec((None, 1, full), lambda l, q: (l, 0, 0))]),
        compiler_params=_params(("arbitrary",)),
    )(myq, c_all, dmod_all, dmod_all)


def _in_proj(l, x, mod, g1, wg_in, t_len):
    n, d = x.shape
    tm = min(TOKEN_TILE, t_len)
    tpb = t_len // tm
    qc = wg_in.shape[-1]

    def body(x_ref, mod_ref, g_ref, w_ref, h_ref, proj_ref, h_s):
        @pl.when(pl.program_id(1) == 0)
        def _():
            h = _modnorm_fwd(x_ref[...], g_ref[...], mod_ref[1:2, :], mod_ref[0:1, :])
            h_s[...] = h.astype(MXU_DTYPE)
            h_ref[...] = h.astype(ACT_DTYPE)
        proj_ref[...] = jnp.dot(h_s[...], w_ref[...], preferred_element_type=F32).astype(ACT_DTYPE)

    return pl.pallas_call(
        body, name=f"in_proj_{l}",
        out_shape=[jax.ShapeDtypeStruct((n, d), ACT_DTYPE), jax.ShapeDtypeStruct((n, N_CHIP * qc), ACT_DTYPE)],
        grid=(n // tm, N_CHIP),
        in_specs=[pl.BlockSpec((tm, d), lambda i, j: (i, 0)),
                  pl.BlockSpec((None, None, 8, d), lambda i, j: (l, i // tpb, 0, 0)),
                  pl.BlockSpec((None, 1, d), lambda i, j: (l, 0, 0)),
                  pl.BlockSpec((None, None, d, qc), lambda i, j: (j, l, 0, 0))],
        out_specs=[pl.BlockSpec((tm, d), lambda i, j: (i, 0)),
                   pl.BlockSpec((tm, qc), lambda i, j: (i, j))],
        scratch_shapes=[pltpu.VMEM((tm, d), MXU_DTYPE)],
        compiler_params=_params(("arbitrary", "arbitrary")),
    )(x, mod, g1, wg_in)


def _masked_ws(ws_ref, wm_s):
    mask = _causal_mask()
    for h in range(HEADS):
        wm_s[h] = jnp.where(mask, ws_ref[h], 0.0).astype(MXU_DTYPE)


def _fill_z(i, tpb, a_ref, g_ref, ah_ref, gh_ref, zext):
    ah = ah_ref[...].astype(F32)
    gh = gh_ref[...].astype(F32)
    keep = jnp.where(i % tpb == 0, 0.0, 1.0)
    zext[0:HALO, :] = ah * _sigmoid(gh) * keep


def _conv_taps(src, w_ref, r0, lanes, flip):
    width = CONV_ROWS + HALO
    win = src[pl.ds(r0, width), lanes]
    rot = _shifted(win, width)
    acc = jnp.zeros((CONV_ROWS, LANES), F32)
    for s in range(0, HALO + 1):
        k = (CONV_TAPS - 1 - s) if flip else (s - 2)
        if k < 0 or k >= CONV_TAPS:
            continue
        q, r = divmod(s, 8)
        acc = acc + rot[r][8 * q:8 * q + CONV_ROWS] * w_ref[k:k + 1, lanes]
    return acc


def _branches_fwd(l, proj, lng, lnb, ws, bst, cw, cb, blg, blb, t_len):
    n = proj.shape[0]
    d = lng.shape[-1]
    tm = min(TOKEN_TILE, t_len)
    tpb = t_len // tm
    per = tm // HALO
    nchunk = tm // CHUNK

    def body(u_ref, v_ref, a_ref, g_ref, ah_ref, gh_ref, lng_ref, lnb_ref, ws_ref, bst_ref, cw_ref, cb_ref,
             blg_ref, blb_ref, ya_ref, yb_ref, zc_ref, wm_s, zext):
        i = pl.program_id(0)
        _masked_ws(ws_ref, wm_s)
        _fill_z(i, tpb, a_ref, g_ref, ah_ref, gh_ref, zext)

        def chunk(c, carry):
            r0 = pl.multiple_of(c * CHUNK, CHUNK)
            rows = pl.ds(r0, CHUNK)
            vhat, _ = _ln_stats(v_ref[rows, :].astype(F32))
            vn = (vhat * lng_ref[...] + lnb_ref[...]).astype(MXU_DTYPE)
            u = u_ref[rows, :].astype(F32)
            for h in range(HEADS):
                cols = slice(h * CHUNK, (h + 1) * CHUNK)
                s = jnp.dot(wm_s[h], vn[:, cols], preferred_element_type=F32) + bst_ref[:, h:h + 1]
                ya_ref[rows, cols] = (u[:, cols] * s).astype(ACT_DTYPE)
            a = a_ref[rows, :].astype(F32)
            g = g_ref[rows, :].astype(F32)
            zext[pl.ds(HALO + r0, CHUNK), :] = a * _sigmoid(g)
            return carry

        lax.fori_loop(0, nchunk, chunk, 0)

        for lc in range(d // LANES):
            lanes = slice(lc * LANES, (lc + 1) * LANES)

            def block(rb, carry):
                r0 = pl.multiple_of(rb * CONV_ROWS, CONV_ROWS)
                acc = _conv_taps(zext, cw_ref, r0, lanes, flip=False) + cb_ref[:, lanes]
                zc_ref[pl.ds(r0, CONV_ROWS), lanes] = acc.astype(ACT_DTYPE)
                return carry

            lax.fori_loop(0, tm // CONV_ROWS, block, 0)

        def chunk2(c, carry):
            r0 = pl.multiple_of(c * CHUNK, CHUNK)
            rows = pl.ds(r0, CHUNK)
            zhat, _ = _ln_stats(zc_ref[rows, :].astype(F32))
            zn = zhat * blg_ref[...] + blb_ref[...]
            yb_ref[rows, :] = (zn * _sigmoid(zn)).astype(ACT_DTYPE)
            return carry

        lax.fori_loop(0, nchunk, chunk2, 0)

    col = lambda k: pl.BlockSpec((tm, d), lambda i: (i, k))
    halo = lambda k: pl.BlockSpec((HALO, d), lambda i: (jnp.maximum(i * per - 1, 0), k))
    vec = pl.BlockSpec((None, 1, d), lambda i: (l, 0, 0))
    out = pl.BlockSpec((tm, d), lambda i: (i, 0))
    return pl.pallas_call(
        body, name=f"branches_fwd_{l}",
        out_shape=[jax.ShapeDtypeStruct((n, d), ACT_DTYPE)] * 3,
        grid=(n // tm,),
        in_specs=[col(0), col(1), col(2), col(3), halo(2), halo(3), vec, vec,
                  pl.BlockSpec((None, HEADS, CHUNK, CHUNK), lambda i: (l, 0, 0, 0)),
                  pl.BlockSpec((None, CHUNK, HEADS), lambda i: (l, 0, 0)),
                  pl.BlockSpec((None, HALO, d), lambda i: (l, 0, 0)), vec, vec, vec],
        out_specs=[out, out, out],
        scratch_shapes=[pltpu.VMEM((HEADS, CHUNK, CHUNK), MXU_DTYPE), pltpu.VMEM((HALO + tm, d), F32)],
        compiler_params=_params(("arbitrary",)),
    )(proj, proj, proj, proj, proj, proj, lng, lnb, ws, bst, cw, cb, blg, blb)


def _merge_out(l, x, mod, proj, ya_in, yb_in, wg_pa, wg_pb, wg_out, t_len):
    n, d = x.shape
    tm = min(TOKEN_TILE, t_len)
    tpb = t_len // tm
    rq = d // N_CHIP

    def body(x_ref, mod_ref, ga_ref, gb_ref, yai_ref, ybi_ref, wpa_ref, wpb_ref, wo_ref,
             ya_ref, yb_ref, mg_ref, o_ref, x1_ref):
        wpa = wpa_ref[...].reshape(d, d)
        wpb = wpb_ref[...].reshape(d, d)
        wo = wo_ref[...].reshape(d, d)
        ya = jnp.dot(yai_ref[...].astype(MXU_DTYPE), wpa, preferred_element_type=F32)
        yb = jnp.dot(ybi_ref[...].astype(MXU_DTYPE), wpb, preferred_element_type=F32)
        merged = _sigmoid(ga_ref[...].astype(F32)) * ya + _sigmoid(gb_ref[...].astype(F32)) * yb
        o = _dot(merged, wo)
        ya_ref[...] = ya.astype(ACT_DTYPE)
        yb_ref[...] = yb.astype(ACT_DTYPE)
        mg_ref[...] = merged.astype(ACT_DTYPE)
        o_ref[...] = o.astype(ACT_DTYPE)
        x1_ref[...] = x_ref[...] + mod_ref[2:3, :] * o

    tile = pl.BlockSpec((tm, d), lambda i: (i, 0))
    wspec = pl.BlockSpec((N_CHIP, None, rq, d), lambda i: (0, l, 0, 0))
    return pl.pallas_call(
        body, name=f"merge_out_{l}",
        out_shape=[jax.ShapeDtypeStruct((n, d), ACT_DTYPE)] * 4 + [jax.ShapeDtypeStruct((n, d), F32)],
        grid=(n // tm,),
        in_specs=[tile, pl.BlockSpec((None, None, 8, d), lambda i: (l, i // tpb, 0, 0)),
                  pl.BlockSpec((tm, d), lambda i: (i, 4)), pl.BlockSpec((tm, d), lambda i: (i, 5)),
                  tile, tile, wspec, wspec, wspec],
        out_specs=[tile] * 5,
        compiler_params=_params(("arbitrary",)),
    )(x, mod, proj, proj, ya_in, yb_in, wg_pa, wg_pb, wg_out)


def _ffn_fwd(l, x1, mod, g2, wg_ff1, wg_ff2, t_len):
    n, d = x1.shape
    tm = min(TOKEN_TILE, t_len)
    tpb = t_len // tm
    hq = wg_ff1.shape[-1]

    def body(x_ref, mod_ref, g_ref, w1_ref, w2_ref, h_ref, f_ref, o2_ref, x2_ref, h_s, acc):
        j = pl.program_id(1)

        @pl.when(j == 0)
        def _():
            h = _modnorm_fwd(x_ref[...], g_ref[...], mod_ref[4:5, :], mod_ref[3:4, :])
            h_s[...] = h.astype(MXU_DTYPE)
            h_ref[...] = h.astype(ACT_DTYPE)
            acc[...] = jnp.zeros_like(acc)

        f = jnp.dot(h_s[...], w1_ref[...], preferred_element_type=F32)
        f_ref[...] = f.astype(ACT_DTYPE)
        acc[...] += _dot(jnp.square(jnp.maximum(f, 0.0)), w2_ref[...])

        @pl.when(j == N_CHIP - 1)
        def _():
            o2 = acc[...]
            o2_ref[...] = o2.astype(ACT_DTYPE)
            x2_ref[...] = x_ref[...] + mod_ref[5:6, :] * o2

    tile = pl.BlockSpec((tm, d), lambda i, j: (i, 0))
    return pl.pallas_call(
        body, name=f"ffn_fwd_{l}",
        out_shape=[jax.ShapeDtypeStruct((n, d), ACT_DTYPE), jax.ShapeDtypeStruct((n, N_CHIP * hq), ACT_DTYPE),
                   jax.ShapeDtypeStruct((n, d), ACT_DTYPE), jax.ShapeDtypeStruct((n, d), F32)],
        grid=(n // tm, N_CHIP),
        in_specs=[tile, pl.BlockSpec((None, None, 8, d), lambda i, j: (l, i // tpb, 0, 0)),
                  pl.BlockSpec((None, 1, d), lambda i, j: (l, 0, 0)),
                  pl.BlockSpec((None, None, d, hq), lambda i, j: (j, l, 0, 0)),
                  pl.BlockSpec((None, None, hq, d), lambda i, j: (j, l, 0, 0))],
        out_specs=[tile, pl.BlockSpec((tm, hq), lambda i, j: (i, j)), tile, tile],
        scratch_shapes=[pltpu.VMEM((tm, d), MXU_DTYPE), pltpu.VMEM((tm, d), F32)],
        compiler_params=_params(("arbitrary", "arbitrary")),
    )(x1, mod, g2, wg_ff1, wg_ff2)


def _loss_head(x, final_g, target):
    n, d = x.shape
    tm = min(TOKEN_TILE, n)

    def body(x_ref, g_ref, t_ref, loss_ref, dx_ref, dg_ref):
        @pl.when(pl.program_id(0) == 0)
        def _():
            loss_ref[...] = jnp.zeros_like(loss_ref)
            dg_ref[...] = jnp.zeros_like(dg_ref)

        x_t = x_ref[...]
        g = g_ref[...]
        r = lax.rsqrt(_rowmean(x_t * x_t) + EPS)
        xn = x_t * r
        e = xn * g - t_ref[...]
        loss_ref[...] += jnp.sum(e * e) * (0.5 / d)
        dy = e * (1.0 / d)
        dxn = dy * g
        dx_ref[...] = r * (dxn - xn * _rowmean(dxn * xn))
        dg_ref[0:1, :] += _colsum(dy * xn)

    tile = pl.BlockSpec((tm, d), lambda i: (i, 0))
    return pl.pallas_call(
        body, name="loss_head",
        out_shape=[jax.ShapeDtypeStruct((8, LANES), F32), jax.ShapeDtypeStruct((n, d), F32),
                   jax.ShapeDtypeStruct((8, d), F32)],
        grid=(n // tm,),
        in_specs=[tile, pl.BlockSpec((1, d), lambda i: (0, 0)), tile],
        out_specs=[pl.BlockSpec((8, LANES), lambda i: (0, 0)), tile, pl.BlockSpec((8, d), lambda i: (0, 0))],
        compiler_params=_params(("arbitrary",)),
    )(x, final_g, target)


def _norm_tail(i, tpb, x_ref, dxin_ref, dh, g_ref, sc, dx_ref, dmod_ref, dg_ref, row_sh, row_sc):
    dxm, dsh, q = _modnorm_bwd(x_ref[...], dh, g_ref[...], sc)
    dx_ref[...] = dxin_ref[...] + dxm
    dmod_ref[row_sh:row_sh + 1, :] += dsh
    dmod_ref[row_sc:row_sc + 1, :] += g_ref[...] * q
    dg_ref[0:1, :] += (1.0 + sc) * q


def _ffn_bwd(l, dx2, x1, mod, g2, o2, f, wg_ff1, wg_ff2, t_len, nb):
    n, d = dx2.shape
    tm = min(TOKEN_TILE, t_len)
    tpb = t_len // tm
    hq = wg_ff1.shape[-1]

    def body(dx2_ref, x1_ref, mod_ref, g_ref, o2_ref, f_ref, w1_ref, w2_ref,
             do2_ref, df_ref, dx1_ref, dmod_ref, dg_ref, do_s, acc):
        i, j = pl.program_id(0), pl.program_id(1)

        @pl.when((i == 0) & (j == 0))
        def _():
            dg_ref[...] = jnp.zeros_like(dg_ref)

        @pl.when((i % tpb == 0) & (j == 0))
        def _():
            dmod_ref[...] = jnp.zeros_like(dmod_ref)

        @pl.when(j == 0)
        def _():
            dx2_t = dx2_ref[...]
            dmod_ref[5:6, :] += _colsum(dx2_t * o2_ref[...].astype(F32))
            do2 = dx2_t * mod_ref[5:6, :]
            do_s[...] = do2.astype(MXU_DTYPE)
            do2_ref[...] = do2.astype(ACT_DTYPE)
            acc[...] = jnp.zeros_like(acc)

        da2 = _dot_nt(do_s[...], w2_ref[...])
        df = da2 * (2.0 * jnp.maximum(f_ref[...].astype(F32), 0.0))
        df_ref[...] = df.astype(ACT_DTYPE)
        acc[...] += _dot_nt(df, w1_ref[...])

        @pl.when(j == N_CHIP - 1)
        def _():
            _norm_tail(i, tpb, x1_ref, dx2_ref, acc[...], g_ref, mod_ref[4:5, :], dx1_ref, dmod_ref, dg_ref, 3, 4)

    tile = pl.BlockSpec((tm, d), lambda i, j: (i, 0))
    wide = pl.BlockSpec((tm, hq), lambda i, j: (i, j))
    return pl.pallas_call(
        body, name=f"ffn_bwd_{l}",
        out_shape=[jax.ShapeDtypeStruct((n, d), ACT_DTYPE), jax.ShapeDtypeStruct((n, N_CHIP * hq), ACT_DTYPE),
                   jax.ShapeDtypeStruct((n, d), F32), jax.ShapeDtypeStruct((nb, 8, d), F32),
                   jax.ShapeDtypeStruct((8, d), F32)],
        grid=(n // tm, N_CHIP),
        in_specs=[tile, tile, pl.BlockSpec((None, None, 8, d), lambda i, j: (l, i // tpb, 0, 0)),
                  pl.BlockSpec((None, 1, d), lambda i, j: (l, 0, 0)), tile, wide,
                  pl.BlockSpec((None, None, d, hq), lambda i, j: (j, l, 0, 0)),
                  pl.BlockSpec((None, None, hq, d), lambda i, j: (j, l, 0, 0))],
        out_specs=[tile, wide, tile, pl.BlockSpec((None, 8, d), lambda i, j: (i // tpb, 0, 0)),
                   pl.BlockSpec((8, d), lambda i, j: (0, 0))],
        scratch_shapes=[pltpu.VMEM((tm, d), MXU_DTYPE), pltpu.VMEM((tm, d), F32)],
        compiler_params=_params(("arbitrary", "arbitrary")),
    )(dx2, x1, mod, g2, o2, f, wg_ff1, wg_ff2)


def _merge_bwd(l, dx1, mod, o, ya, yb, proj, wg_pa, wg_pb, wg_out, t_len, nb):
    n, d = dx1.shape
    tm = min(TOKEN_TILE, t_len)
    tpb = t_len // tm
    rq = d // N_CHIP

    def body(dx_ref, mod_ref, o_ref, ya_ref, yb_ref, ga_ref, gb_ref, wpa_ref, wpb_ref, wo_ref,
             do_ref, dya_ref, dyb_ref, dyai_ref, dybi_ref, dproj_ref, dmod_ref):
        i = pl.program_id(0)

        @pl.when(i % tpb == 0)
        def _():
            dmod_ref[...] = jnp.zeros_like(dmod_ref)

        dx = dx_ref[...]
        dmod_ref[2:3, :] += _colsum(dx * o_ref[...].astype(F32))
        do = (dx * mod_ref[2:3, :]).astype(MXU_DTYPE)
        do_ref[...] = do.astype(ACT_DTYPE)
        dm = _dot_nt(do, wo_ref[...].reshape(d, d))
        sa = _sigmoid(ga_ref[...].astype(F32))
        sb = _sigmoid(gb_ref[...].astype(F32))
        dya = (dm * sa).astype(MXU_DTYPE)
        dyb = (dm * sb).astype(MXU_DTYPE)
        dya_ref[...] = dya.astype(ACT_DTYPE)
        dyb_ref[...] = dyb.astype(ACT_DTYPE)
        dproj_ref[:, 0:d] = (dm * ya_ref[...].astype(F32) * sa * (1.0 - sa)).astype(ACT_DTYPE)
        dproj_ref[:, d:2 * d] = (dm * yb_ref[...].astype(F32) * sb * (1.0 - sb)).astype(ACT_DTYPE)
        dyai_ref[...] = _dot_nt(dya, wpa_ref[...].reshape(d, d)).astype(ACT_DTYPE)
        dybi_ref[...] = _dot_nt(dyb, wpb_ref[...].reshape(d, d)).astype(ACT_DTYPE)

    tile = pl.BlockSpec((tm, d), lambda i: (i, 0))
    wspec = pl.BlockSpec((N_CHIP, None, rq, d), lambda i: (0, l, 0, 0))
    return pl.pallas_call(
        body, name=f"merge_bwd_{l}",
        out_shape=[jax.ShapeDtypeStruct((n, d), ACT_DTYPE)] * 5
        + [jax.ShapeDtypeStruct((n, 6 * d), ACT_DTYPE), jax.ShapeDtypeStruct((nb, 8, d), F32)],
        grid=(n // tm,),
        in_specs=[tile, pl.BlockSpec((None, None, 8, d), lambda i: (l, i // tpb, 0, 0)), tile, tile, tile,
                  pl.BlockSpec((tm, d), lambda i: (i, 4)), pl.BlockSpec((tm, d), lambda i: (i, 5)),
                  wspec, wspec, wspec],
        out_specs=[tile] * 5 + [pl.BlockSpec((tm, 2 * d), lambda i: (i, 2)),
                                pl.BlockSpec((None, 8, d), lambda i: (i // tpb, 0, 0))],
        compiler_params=_params(("arbitrary",)),
    )(dx1, mod, o, ya, yb, proj, proj, wg_pa, wg_pb, wg_out)


def _branches_bwd(l, proj, zc, dya_in, dyb_in, dproj, lng, lnb, ws, bst, cw, blg, blb, t_len):
    n = proj.shape[0]
    d = lng.shape[-1]
    tm = min(TOKEN_TILE, t_len)
    tpb = t_len // tm
    per = tm // HALO
    nchunk = tm // CHUNK
    ntile = n // tm
    nblk = tm // CONV_ROWS

    def body(u_ref, v_ref, a_ref, g_ref, ah_ref, gh_ref, zc_ref, zcn_ref, dya_ref, dyb_ref, dybn_ref, dproj_in,
             lng_ref, lnb_ref, ws_ref, bst_ref, cw_ref, blg_ref, blb_ref,
             dproj_ref, dws_ref, dbst_ref, dcw_ref, vec_ref, wm_s, zext, dzext, dvn_s):
        i = pl.program_id(0)

        @pl.when(i == 0)
        def _():
            dws_ref[...] = jnp.zeros_like(dws_ref)
            dbst_ref[...] = jnp.zeros_like(dbst_ref)
            dcw_ref[...] = jnp.zeros_like(dcw_ref)
            vec_ref[...] = jnp.zeros_like(vec_ref)

        _masked_ws(ws_ref, wm_s)
        _fill_z(i, tpb, a_ref, g_ref, ah_ref, gh_ref, zext)

        def conv_ln_bwd(zc_t, dyb_t):
            zhat, rstd = _ln_stats(zc_t)
            zn = zhat * blg_ref[...] + blb_ref[...]
            sg = _sigmoid(zn)
            dzn = dyb_t * (sg * (1.0 + zn * (1.0 - sg)))
            return _ln_bwd(dzn, zhat, rstd, blg_ref[...]), _colsum(dzn * zhat), _colsum(dzn)

        def chunk(c, carry):
            r0 = pl.multiple_of(c * CHUNK, CHUNK)
            rows = pl.ds(r0, CHUNK)
            vhat, rstd = _ln_stats(v_ref[rows, :].astype(F32))
            vn = (vhat * lng_ref[...] + lnb_ref[...]).astype(MXU_DTYPE)
            u = u_ref[rows, :].astype(F32)
            dya = dya_ref[rows, :].astype(F32)
            for h in range(HEADS):
                cols = slice(h * CHUNK, (h + 1) * CHUNK)
                s = jnp.dot(wm_s[h], vn[:, cols], preferred_element_type=F32) + bst_ref[:, h:h + 1]
                dproj_ref[rows, cols] = (dya[:, cols] * s).astype(ACT_DTYPE)
                ds = dya[:, cols] * u[:, cols]
                dvn_s[:, cols] = _dot_tn(wm_s[h], ds)
                dws_ref[h] += _dot_nt(ds, vn[:, cols])
                dbst_ref[:, h:h + 1] += jnp.sum(ds, axis=1, keepdims=True)
            dvn = dvn_s[...]
            dproj_ref[rows, d:2 * d] = _ln_bwd(dvn, vhat, rstd, lng_ref[...]).astype(ACT_DTYPE)
            vec_ref[0:1, :] += _colsum(dvn * vhat)
            vec_ref[1:2, :] += _colsum(dvn)
            a = a_ref[rows, :].astype(F32)
            g = g_ref[rows, :].astype(F32)
            zext[pl.ds(HALO + r0, CHUNK), :] = a * _sigmoid(g)
            dzc, dblg, dblb = conv_ln_bwd(zc_ref[rows, :].astype(F32), dyb_ref[rows, :].astype(F32))
            dzext[rows, :] = dzc
            vec_ref[2:3, :] += _colsum(dzc)
            vec_ref[3:4, :] += dblg
            vec_ref[4:5, :] += dblb
            return carry

        lax.fori_loop(0, nchunk, chunk, 0)

        dzc_next, _, _ = conv_ln_bwd(zcn_ref[...].astype(F32), dybn_ref[...].astype(F32))
        dzext[tm:tm + HALO, :] = dzc_next * jnp.where(i % tpb == tpb - 1, 0.0, 1.0)

        for lc in range(d // LANES):
            lanes = slice(lc * LANES, (lc + 1) * LANES)

            def block(rb, carry):
                r0 = pl.multiple_of(rb * CONV_ROWS, CONV_ROWS)
                rows = pl.ds(r0, CONV_ROWS)
                dz = _conv_taps(dzext, cw_ref, r0, lanes, flip=True)
                a = a_ref[rows, lanes].astype(F32)
                sg = _sigmoid(g_ref[rows, lanes].astype(F32))
                dproj_ref[rows, 2 * d + lc * LANES:2 * d + (lc + 1) * LANES] = (dz * sg).astype(ACT_DTYPE)
                dproj_ref[rows, 3 * d + lc * LANES:3 * d + (lc + 1) * LANES] = (
                    dz * a * sg * (1.0 - sg)).astype(ACT_DTYPE)
                return carry

            lax.fori_loop(0, nblk, block, 0)

            def wblock(rb, accs):
                r0 = pl.multiple_of(rb * CONV_ROWS, CONV_ROWS)
                width = CONV_ROWS + HALO
                rot = _shifted(zext[pl.ds(r0, width), lanes], width)
                dzc = dzext[pl.ds(r0, CONV_ROWS), lanes]
                out = []
                for k in range(CONV_TAPS):
                    q, r = divmod(k + 2, 8)
                    prod = dzc * rot[r][8 * q:8 * q + CONV_ROWS]
                    part = prod[0:8]
                    for e in range(1, CONV_ROWS // 8):
                        part = part + prod[8 * e:8 * e + 8]
                    out.append(accs[k] + part)
                return tuple(out)

            accs = lax.fori_loop(0, nblk, wblock, tuple(jnp.zeros((8, LANES), F32) for _ in range(CONV_TAPS)))
            for k in range(CONV_TAPS):
                dcw_ref[k:k + 1, lanes] += _colsum(accs[k])

        @pl.when(i == ntile - 1)
        def _():
            mask = _causal_mask()
            for h in range(HEADS):
                dws_ref[h] = jnp.where(mask, dws_ref[h], 0.0)

    col = lambda k: pl.BlockSpec((tm, d), lambda i: (i, k))
    tile = pl.BlockSpec((tm, d), lambda i: (i, 0))
    before = lambda k: pl.BlockSpec((HALO, d), lambda i: (jnp.maximum(i * per - 1, 0), k))
    after = pl.BlockSpec((HALO, d), lambda i: (jnp.minimum((i + 1) * per, n // HALO - 1), 0))
    vec = pl.BlockSpec((None, 1, d), lambda i: (l, 0, 0))
    const2 = lambda r, c: pl.BlockSpec((r, c), lambda i: (0, 0))
    return pl.pallas_call(
        body, name=f"branches_bwd_{l}",
        out_shape=[jax.ShapeDtypeStruct((n, 6 * d), ACT_DTYPE), jax.ShapeDtypeStruct((HEADS, CHUNK, CHUNK), F32),
                   jax.ShapeDtypeStruct((CHUNK, HEADS), F32), jax.ShapeDtypeStruct((HALO, d), F32),
                   jax.ShapeDtypeStruct((8, d), F32)],
        grid=(ntile,),
        in_specs=[col(0), col(1), col(2), col(3), before(2), before(3), tile, after, tile, tile, after,
                  pl.BlockSpec(memory_space=pl.ANY), vec, vec,
                  pl.BlockSpec((None, HEADS, CHUNK, CHUNK), lambda i: (l, 0, 0, 0)),
                  pl.BlockSpec((None, CHUNK, HEADS), lambda i: (l, 0, 0)),
                  pl.BlockSpec((None, HALO, d), lambda i: (l, 0, 0)), vec, vec],
        out_specs=[pl.BlockSpec((tm, 4 * d), lambda i: (i, 0)),
                   pl.BlockSpec((HEADS, CHUNK, CHUNK), lambda i: (0, 0, 0)),
                   const2(CHUNK, HEADS), const2(HALO, d), const2(8, d)],
        scratch_shapes=[pltpu.VMEM((HEADS, CHUNK, CHUNK), MXU_DTYPE), pltpu.VMEM((HALO + tm, d), F32),
                        pltpu.VMEM((tm + HALO, d), F32), pltpu.VMEM((CHUNK, d), F32)],
        input_output_aliases={11: 0},
        compiler_params=_params(("arbitrary",)),
    )(proj, proj, proj, proj, proj, proj, zc, zc, dya_in, dyb_in, dyb_in, dproj, lng, lnb, ws, bst, cw, blg, blb)


def _in_proj_bwd(l, dproj, dx1, x, mod, g1, wg_in, t_len, nb):
    n, d = x.shape
    tm = min(TOKEN_TILE, t_len)
    tpb = t_len // tm
    qc = wg_in.shape[-1]

    def body(dp_ref, dx1_ref, x_ref, mod_ref, g_ref, w_ref, dx_ref, dmod_ref, dg_ref, acc):
        i, j = pl.program_id(0), pl.program_id(1)

        @pl.when((i == 0) & (j == 0))
        def _():
            dg_ref[...] = jnp.zeros_like(dg_ref)

        @pl.when((i % tpb == 0) & (j == 0))
        def _():
            dmod_ref[...] = jnp.zeros_like(dmod_ref)

        @pl.when(j == 0)
        def _():
            acc[...] = jnp.zeros_like(acc)

        acc[...] += _dot_nt(dp_ref[...], w_ref[...])

        @pl.when(j == N_CHIP - 1)
        def _():
            _norm_tail(i, tpb, x_ref, dx1_ref, acc[...], g_ref, mod_ref[1:2, :], dx_ref, dmod_ref, dg_ref, 0, 1)

    tile = pl.BlockSpec((tm, d), lambda i, j: (i, 0))
    return pl.pallas_call(
        body, name=f"in_proj_bwd_{l}",
        out_shape=[jax.ShapeDtypeStruct((n, d), F32), jax.ShapeDtypeStruct((nb, 8, d), F32),
                   jax.ShapeDtypeStruct((8, d), F32)],
        grid=(n // tm, N_CHIP),
        in_specs=[pl.BlockSpec((tm, qc), lambda i, j: (i, j)), tile, tile,
                  pl.BlockSpec((None, None, 8, d), lambda i, j: (l, i // tpb, 0, 0)),
                  pl.BlockSpec((None, 1, d), lambda i, j: (l, 0, 0)),
                  pl.BlockSpec((None, None, d, qc), lambda i, j: (j, l, 0, 0))],
        out_specs=[tile, pl.BlockSpec((None, 8, d), lambda i, j: (i // tpb, 0, 0)),
                   pl.BlockSpec((8, d), lambda i, j: (0, 0))],
        scratch_shapes=[pltpu.VMEM((tm, d), F32)],
        compiler_params=_params(("arbitrary", "arbitrary")),
    )(dproj, dx1, x, mod, g1, wg_in)


def _weight_grad(name, l, nl, a, b, a_spec, b_spec, out_block, out_index, grid_ij, prev=None, relu2=False):
    n = a.shape[0]
    tk = min(TOKEN_TILE, n)
    nk = n // tk
    rows, cols = out_block

    def body(*refs):
        if prev is None:
            a_ref, b_ref, o_ref, acc = refs
        else:
            a_ref, b_ref, _, o_ref, acc = refs
        k = pl.program_id(2)

        @pl.when(k == 0)
        def _():
            acc[...] = jnp.zeros_like(acc)

        a_t = a_ref[...]
        if relu2:
            a_t = jnp.square(jnp.maximum(a_t.astype(F32), 0.0))
        acc[...] += _dot_tn(a_t, b_ref[...])

        @pl.when(k == nk - 1)
        def _():
            o_ref[...] = acc[...].astype(WIRE_DTYPE)

    gi, gj = grid_ij
    operands = [a, b] + ([] if prev is None else [prev])
    in_specs = [a_spec(tk), b_spec(tk)] + ([] if prev is None else [pl.BlockSpec(memory_space=pl.ANY)])
    out_shape = jax.ShapeDtypeStruct((N_CHIP, nl, rows, cols), WIRE_DTYPE)
    return pl.pallas_call(
        body, name=f"{name}_{l}", out_shape=out_shape,
        grid=(gi, gj, nk),
        in_specs=in_specs,
        out_specs=pl.BlockSpec((None, None, rows, cols), out_index),
        scratch_shapes=[pltpu.VMEM((rows, cols), F32)],
        input_output_aliases={} if prev is None else {2: 0},
        compiler_params=_params(("arbitrary", "arbitrary", "arbitrary")),
    )(*operands)


def _row_tile(rows, cols, arrays):
    budget = VMEM_LIMIT // 3
    t = budget // (arrays * 2 * cols * 4)
    t = max(8, min(rows, t // 8 * 8))
    while rows % t:
        t -= 8
    return t


def _sum_partials(own, got, myq, name):
    _, rows, cols = own.shape
    tr = _row_tile(rows, cols, 3)

    def body(q_ref, own_ref, got_ref, o_ref):
        acc = own_ref[...].astype(F32)
        for k in range(3):
            acc = acc + got_ref[k].astype(F32)
        o_ref[...] = acc

    return pl.pallas_call(
        body, name=name, out_shape=jax.ShapeDtypeStruct((rows, cols), F32),
        grid_spec=pltpu.PrefetchScalarGridSpec(
            num_scalar_prefetch=1, grid=(rows // tr,),
            in_specs=[pl.BlockSpec((None, tr, cols), lambda i, q: (q[0], i, 0)),
                      pl.BlockSpec((3, tr, cols), lambda i, q: (0, i, 0))],
            out_specs=pl.BlockSpec((tr, cols), lambda i, q: (i, 0))),
        compiler_params=_params(("arbitrary",)),
    )(myq, own, got)


def _adamw(name, w, m, v, g_a, g_b=None):
    rows, cols = w.shape
    tr = _row_tile(rows, cols, 9)
    c1 = 1.0 - ADAM_B1 ** ADAM_STEP
    c2 = 1.0 - ADAM_B2 ** ADAM_STEP

    def body(*refs):
        if g_b is None:
            w_ref, m_ref, v_ref, ga_ref, g_ref, d_ref, m2_ref, v2_ref = refs
            g = ga_ref[...]
        else:
            w_ref, m_ref, v_ref, ga_ref, gb_ref, g_ref, d_ref, m2_ref, v2_ref = refs
            g = ga_ref[...] + gb_ref[...]
        m2 = ADAM_B1 * m_ref[...] + (1.0 - ADAM_B1) * g
        v2 = ADAM_B2 * v_ref[...] + (1.0 - ADAM_B2) * (g * g)
        g_ref[...] = g
        m2_ref[...] = m2
        v2_ref[...] = v2
        d_ref[...] = -ADAM_LR * ((m2 / c1) / (jnp.sqrt(v2 / c2) + ADAM_EPS) + ADAM_WD * w_ref[...])

    tile = pl.BlockSpec((tr, cols), lambda i: (i, 0))
    operands = [w, m, v, g_a] + ([] if g_b is None else [g_b])
    return pl.pallas_call(
        body, name=name, out_shape=[jax.ShapeDtypeStruct((rows, cols), F32)] * 4,
        grid=(rows // tr,), in_specs=[tile] * len(operands), out_specs=[tile] * 4,
        compiler_params=_params(("arbitrary",)),
    )(*operands)


def _pack(parts):
    flat = [p.reshape(-1, LANES) for p in parts]
    for f in flat:
        assert f.shape[0] % 8 == 0
    return jnp.concatenate(flat, axis=0)


def _unpack(packed, shapes):
    out, r = [], 0
    for s in shapes:
        size = 1
        for e in s:
            size *= e
        rows = size // LANES
        out.append(packed[r:r + rows].reshape(s))
        r += rows
    return out


def kernel(x, c, w_ada, b_ada, norm1_g, w_in, a_ln_g, a_ln_b, a_ws, a_bs, w_pa, b_conv_w, b_conv_b, b_ln_g, b_ln_b, w_pb, w_out, norm2_g, w_ff1, w_ff2, final_g, loss_target, m_w_ada, m_b_ada, m_norm1_g, m_w_in, m_a_ln_g, m_a_ln_b, m_a_ws, m_a_bs, m_w_pa, m_b_conv_w, m_b_conv_b, m_b_ln_g, m_b_ln_b, m_w_pb, m_w_out, m_norm2_g, m_w_ff1, m_w_ff2, m_final_g, v_w_ada, v_b_ada, v_norm1_g, v_w_in, v_a_ln_g, v_a_ln_b, v_a_ws, v_a_bs, v_w_pa, v_b_conv_w, v_b_conv_b, v_b_ln_g, v_b_ln_b, v_w_pb, v_w_out, v_norm2_g, v_w_ff1, v_w_ff2, v_final_g):
    nb, t_len, d = x.shape
    nl = w_in.shape[0]
    n = nb * t_len
    cq = w_ada.shape[-1]
    cc = d // N_CHIP
    mx, my, mc = _my_place()
    myq = (2 * mx + my).astype(jnp.int32).reshape(1)

    c_slots = jnp.broadcast_to(c[None], (N_DEV, nb, d))
    c_all = _all_to_all(c_slots, "gather_c").reshape(N_DEV * nb, d)
    mod_part = _ada_forward(c_all, w_ada, b_ada.reshape(nl, 1, N_CHIP * cq), myq)
    mod_slots = mod_part.reshape(nl, N_DEV, nb, cq).transpose(1, 0, 2, 3).reshape(N_DEV, nl * nb, cq)
    mod_got = _all_to_all(mod_slots, "exchange_mod").reshape(N_CHIP, 2, nl, nb, cq)[:, 0]
    mod6 = mod_got.transpose(1, 2, 0, 3).reshape(nl, nb, 6, d)
    mod = jnp.pad(mod6, ((0, 0), (0, 0), (0, 2), (0, 0)))

    cast = lambda w: w.astype(WIRE_DTYPE)
    wg_in, wg_pa, wg_pb, wg_out, wg_ff1, wg_ff2, cwg = _gather_quarters(
        [cast(w_in), cast(w_pa), cast(w_pb), cast(w_out), cast(w_ff1), cast(w_ff2),
         b_conv_w.reshape(nl, CONV_TAPS, cc)], "gather_weights")
    cw = jnp.pad(cwg.transpose(1, 2, 0, 3).reshape(nl, CONV_TAPS, d), ((0, 0), (0, HALO - CONV_TAPS), (0, 0)))

    vec3 = lambda p: p.reshape(nl, 1, d)
    g1, g2 = vec3(norm1_g), vec3(norm2_g)
    lng, lnb, cb, blg, blb = vec3(a_ln_g), vec3(a_ln_b), vec3(b_conv_b), vec3(b_ln_g), vec3(b_ln_b)
    bst = a_bs.transpose(0, 2, 1)

    xs = x.reshape(n, d)
    saved = []
    for l in range(nl):
        h, proj = _in_proj(l, xs, mod, g1, wg_in, t_len)
        ya_in, yb_in, zc = _branches_fwd(l, proj, lng, lnb, a_ws, bst, cw, cb, blg, blb, t_len)
        ya, yb, merged, o, x1 = _merge_out(l, xs, mod, proj, ya_in, yb_in, wg_pa, wg_pb, wg_out, t_len)
        h2, f, o2, x2 = _ffn_fwd(l, x1, mod, g2, wg_ff1, wg_ff2, t_len)
        saved.append((xs, h, proj, ya_in, yb_in, zc, ya, yb, merged, o, x1, h2, f, o2))
        xs = x2

    loss_blk, dx, dfinal = _loss_head(xs, final_g.reshape(1, d), loss_target.reshape(n, d))
    loss = lax.psum(loss_blk[0, 0], ("x", "y", "c"))

    tok = lambda w: (lambda tk: pl.BlockSpec((tk, w), lambda i, j, k: (k, 0)))
    tok_i = lambda w: (lambda tk: pl.BlockSpec((tk, w), lambda i, j, k: (k, i)))
    tok_j = lambda w: (lambda tk: pl.BlockSpec((tk, w), lambda i, j, k: (k, j)))
    qin = wg_in.shape[-1]
    hq = wg_ff1.shape[-1]
    rq = d // N_CHIP
    gw = dict(w_in=None, w_pa=None, w_pb=None, w_out=None, w_ff1=None, w_ff2=None)
    dmods, small = [None] * nl, [None] * nl
    for l in reversed(range(nl)):
        x0, h, proj, ya_in, yb_in, zc, ya, yb, merged, o, x1, h2, f, o2 = saved[l]
        do2, df, dx1, dmod_c, dg2 = _ffn_bwd(l, dx, x1, mod, g2, o2, f, wg_ff1, wg_ff2, t_len, nb)
        gw["w_ff2"] = _weight_grad("grad_w_ff2", l, nl, f, do2, tok_i(hq), tok(d), (hq, d),
                                   lambda i, j, k: (i, l, 0, 0), (N_CHIP, 1), gw["w_ff2"], relu2=True)
        gw["w_ff1"] = _weight_grad("grad_w_ff1", l, nl, h2, df, tok(d), tok_j(hq), (d, hq),
                                   lambda i, j, k: (j, l, 0, 0), (1, N_CHIP), gw["w_ff1"])
        do, dya, dyb, dya_in, dyb_in, dproj, dmod_b = _merge_bwd(l, dx1, mod, o, ya, yb, proj, wg_pa, wg_pb, wg_out,
                                                                 t_len, nb)
        row_q = lambda i, j, k: (i, l, 0, 0)
        gw["w_out"] = _weight_grad("grad_w_out", l, nl, merged, do, tok_i(rq), tok(d), (rq, d), row_q,
                                   (N_CHIP, 1), gw["w_out"])
        gw["w_pa"] = _weight_grad("grad_w_pa", l, nl, ya_in, dya, tok_i(rq), tok(d), (rq, d), row_q,
                                  (N_CHIP, 1), gw["w_pa"])
        gw["w_pb"] = _weight_grad("grad_w_pb", l, nl, yb_in, dyb, tok_i(rq), tok(d), (rq, d), row_q,
                                  (N_CHIP, 1), gw["w_pb"])
        dproj, dws, dbst, dcw, vecs = _branches_bwd(l, proj, zc, dya_in, dyb_in, dproj, lng, lnb, a_ws, bst, cw,
                                                    blg, blb, t_len)
        gw["w_in"] = _weight_grad("grad_w_in", l, nl, h, dproj, tok(d), tok_j(qin), (d, qin),
                                  lambda i, j, k: (j, l, 0, 0), (1, N_CHIP), gw["w_in"])
        dx, dmod_a, dg1 = _in_proj_bwd(l, dproj, dx1, x0, mod, g1, wg_in, t_len, nb)
        dmods[l] = jnp.concatenate([dmod_a[:, 0:2], dmod_b[:, 2:3], dmod_c[:, 3:6]], axis=1)
        small[l] = (dg1[0], vecs[0], vecs[1], dws, dbst.T, dcw[:CONV_TAPS], vecs[2], vecs[3], vecs[4], dg2[0])
    grad_x = dx.reshape(nb, t_len, d)

    dmod_mine = jnp.stack(dmods).reshape(nl * nb, 6 * d)
    dmod_all = _all_to_all(jnp.broadcast_to(dmod_mine[None], (N_DEV,) + dmod_mine.shape), "gather_dmod")
    dmod_all = dmod_all.reshape(N_DEV, nl, nb, 6 * d).transpose(1, 0, 2, 3).reshape(nl, N_DEV * nb, 6 * d)
    g_w_ada, g_b_ada = _ada_backward(c_all, dmod_all, myq, cq)

    names = ["norm1_g", "a_ln_g", "a_ln_b", "a_ws", "a_bs", "b_conv_w", "b_conv_b", "b_ln_g", "b_ln_b", "norm2_g"]
    stacked = [jnp.stack([small[l][k] for l in range(nl)]) for k in range(len(names))]
    stacked[5] = jnp.pad(stacked[5], ((0, 0), (0, HALO - CONV_TAPS), (0, 0)))
    stacked.append(dfinal)
    part_shapes = [s.shape for s in stacked]
    packed = _pack(stacked)
    prow = packed.shape[0]
    pad_rows = (-prow) % (8 * N_DEV)
    packed = jnp.pad(packed, ((0, pad_rows), (0, 0)))
    srow = packed.shape[0] // N_DEV
    mine = _all_to_all(packed.reshape(N_DEV, srow, LANES), "reduce_small", reduce=True)
    total = _all_to_all(jnp.broadcast_to(mine[None], (N_DEV, srow, LANES)), "gather_small")
    total = total.reshape(N_DEV * srow, LANES)[:prow]
    sg = dict(zip(names + ["final_g"], _unpack(total, part_shapes)))
    sg["b_conv_w"] = lax.dynamic_slice_in_dim(sg["b_conv_w"][:, :CONV_TAPS], myq[0] * cc, cc, axis=2).reshape(
        nl, CONV_TAPS, 1, cc)
    sg["final_g"] = sg["final_g"][0]
    sg["b_ada"] = g_b_ada.reshape(nl, N_CHIP * cq)
    small_names = ["b_ada", "norm1_g", "a_ln_g", "a_ln_b", "a_ws", "a_bs", "b_conv_w", "b_conv_b", "b_ln_g",
                   "b_ln_b", "norm2_g", "final_g"]
    given = dict(b_ada=(b_ada, m_b_ada, v_b_ada), norm1_g=(norm1_g, m_norm1_g, v_norm1_g),
                 a_ln_g=(a_ln_g, m_a_ln_g, v_a_ln_g), a_ln_b=(a_ln_b, m_a_ln_b, v_a_ln_b),
                 a_ws=(a_ws, m_a_ws, v_a_ws), a_bs=(a_bs, m_a_bs, v_a_bs),
                 b_conv_w=(b_conv_w, m_b_conv_w, v_b_conv_w), b_conv_b=(b_conv_b, m_b_conv_b, v_b_conv_b),
                 b_ln_g=(b_ln_g, m_b_ln_g, v_b_ln_g), b_ln_b=(b_ln_b, m_b_ln_b, v_b_ln_b),
                 norm2_g=(norm2_g, m_norm2_g, v_norm2_g), final_g=(final_g, m_final_g, v_final_g))

    def padded(a):
        rows = -(-a.size // (8 * LANES)) * 8
        return jnp.pad(a.reshape(-1), (0, rows * LANES - a.size)).reshape(rows, LANES)

    packs = [_pack([padded(given[k][j]) for k in small_names]) for j in range(3)]
    gpack = _pack([padded(sg[k].astype(F32)) for k in small_names])
    res_small = _adamw("adamw_small", packs[0], packs[1], packs[2], gpack)
    out = {}
    for j, kind in enumerate(["grad", "delta", "new_m", "new_v"]):
        r = 0
        for k in small_names:
            a = given[k][0]
            rows = -(-a.size // (8 * LANES)) * 8
            out[(kind, k)] = res_small[j][r:r + rows].reshape(-1)[:a.size].reshape(a.shape)
            r += rows

    res = _adamw("adamw_w_ada", w_ada.reshape(nl * d, cq), m_w_ada.reshape(nl * d, cq), v_w_ada.reshape(nl * d, cq),
                 g_w_ada.reshape(nl * d, cq))
    for kind, r in zip(["grad", "delta", "new_m", "new_v"], res):
        out[(kind, "w_ada")] = r.reshape(w_ada.shape)

    big = ["w_in", "w_pa", "w_pb", "w_out", "w_ff1", "w_ff2"]
    ws_given = dict(w_in=(w_in, m_w_in, v_w_in), w_pa=(w_pa, m_w_pa, v_w_pa), w_pb=(w_pb, m_w_pb, v_w_pb),
                    w_out=(w_out, m_w_out, v_w_out), w_ff1=(w_ff1, m_w_ff1, v_w_ff1), w_ff2=(w_ff2, m_w_ff2, v_w_ff2))
    got = _scatter_quarters([gw[k] for k in big], "scatter_grads")
    sums = []
    for k, g_own, g_got in zip(big, [gw[k] for k in big], got):
        cols = g_own.shape[-1]
        sums.append(_sum_partials(g_own.reshape(N_CHIP, -1, cols), g_got.reshape(3, -1, cols), myq, f"sum_{k}"))
    others = _swap_sibling(sums, "swap_sums")
    for k, s_mine, s_other in zip(big, sums, others):
        w, m, v = ws_given[k]
        cols = w.shape[-1]
        res = _adamw(f"adamw_{k}", w.reshape(-1, cols), m.reshape(-1, cols), v.reshape(-1, cols), s_mine, s_other)
        for kind, r in zip(["grad", "delta", "new_m", "new_v"], res):
            out[(kind, k)] = r.reshape(w.shape)

    order = ["w_ada", "b_ada", "norm1_g", "w_in", "a_ln_g", "a_ln_b", "a_ws", "a_bs", "w_pa", "b_conv_w", "b_conv_b",
             "b_ln_g", "b_ln_b", "w_pb", "w_out", "norm2_g", "w_ff1", "w_ff2", "final_g"]
    return (loss, grad_x, *[out[("grad", k)] for k in order], *[out[("delta", k)] for k in order],
            *[out[("new_m", k)] for k in order], *[out[("new_v", k)] for k in order])
```

```python
import functools

import jax
import jax.numpy as jnp
from jax import lax
from jax.experimental import pallas as pl
from jax.experimental.pallas import tpu as pltpu

F32 = jnp.float32
MXU_DTYPE = jnp.bfloat16
ACT_DTYPE = jnp.bfloat16
WIRE_DTYPE = jnp.bfloat16

EPS = 1e-6
CHUNK = 128
HEADS = 8
CONV_TAPS = 31
HALO = 32
N_DEV = 8
N_CHIP = 4
ADAM_LR, ADAM_B1, ADAM_B2, ADAM_EPS, ADAM_WD, ADAM_STEP = 0.001, 0.9, 0.999, 1e-08, 0.01, 10

V7X_VMEM_BYTES = 64 * 1024 * 1024
VMEM_LIMIT = V7X_VMEM_BYTES * 3 // 4
TOKEN_TILE = 512
CONV_ROWS = 64
LANES = 128
MESH_ID = pl.DeviceIdType.MESH


def _params(sem=None):
    return pltpu.CompilerParams(dimension_semantics=sem, vmem_limit_bytes=VMEM_LIMIT)


def _dot(a, b):
    return jnp.dot(a.astype(MXU_DTYPE), b.astype(MXU_DTYPE), preferred_element_type=F32)


def _dot_nt(a, b):
    return lax.dot_general(a.astype(MXU_DTYPE), b.astype(MXU_DTYPE), (((1,), (1,)), ((), ())),
                           preferred_element_type=F32)


def _dot_tn(a, b):
    return lax.dot_general(a.astype(MXU_DTYPE), b.astype(MXU_DTYPE), (((0,), (0,)), ((), ())),
                           preferred_element_type=F32)


def _colsum(a):
    return jnp.sum(a, axis=0, keepdims=True)


def _rowmean(a):
    return jnp.mean(a, axis=-1, keepdims=True)


def _sigmoid(a):
    return 1.0 / (1.0 + jnp.exp(-a))


def _modnorm_fwd(x, g, sc, sh):
    r = lax.rsqrt(_rowmean(x * x) + EPS)
    return (x * r) * (g * (1.0 + sc)) + sh


def _modnorm_bwd(x, dh, g, sc):
    r = lax.rsqrt(_rowmean(x * x) + EPS)
    xn = x * r
    dxn = dh * (g * (1.0 + sc))
    dx = r * (dxn - xn * _rowmean(dxn * xn))
    return dx, _colsum(dh), _colsum(dh * xn)


def _ln_stats(v):
    mu = _rowmean(v)
    vc = v - mu
    rstd = lax.rsqrt(_rowmean(vc * vc) + EPS)
    return vc * rstd, rstd


def _ln_bwd(dy, vhat, rstd, g):
    dvh = dy * g
    return rstd * (dvh - _rowmean(dvh) - vhat * _rowmean(dvh * vhat))


def _causal_mask():
    row = lax.broadcasted_iota(jnp.int32, (CHUNK, CHUNK), 0)
    col = lax.broadcasted_iota(jnp.int32, (CHUNK, CHUNK), 1)
    return row >= col


def _shifted(win, width):
    return [win if r == 0 else pltpu.roll(win, width - r, 0) for r in range(8)]


def _my_place():
    return lax.axis_index("x"), lax.axis_index("y"), lax.axis_index("c")


def _all_to_all(x, name, reduce=False):
    n, rows, cols = x.shape
    assert n == N_DEV

    def body(x_ref, o_ref, *scratch):
        if reduce:
            land, send_sems, recv_sems = scratch
        else:
            land = o_ref
            send_sems, recv_sems = scratch
        mx, my, mc = _my_place()
        me = 4 * mx + 2 * my + mc
        land[me] = x_ref[me]
        copies = []
        for k in range(1, N_DEV):
            px = (mx + ((k >> 2) & 1)) % 2
            py = (my + ((k >> 1) & 1)) % 2
            pc = (mc + (k & 1)) % 2
            peer = 4 * px + 2 * py + pc
            cp = pltpu.make_async_remote_copy(
                src_ref=x_ref.at[peer], dst_ref=land.at[me],
                send_sem=send_sems.at[k - 1], recv_sem=recv_sems.at[k - 1],
                device_id=(px, py, pc), device_id_type=MESH_ID)
            cp.start()
            copies.append(cp)
        for cp in copies:
            cp.wait()
        if reduce:
            acc = land[0]
            for s in range(1, N_DEV):
                acc = acc + land[s]
            o_ref[...] = acc

    scratch = [pltpu.SemaphoreType.DMA((N_DEV - 1,)), pltpu.SemaphoreType.DMA((N_DEV - 1,))]
    if reduce:
        scratch = [pltpu.VMEM((N_DEV, rows, cols), x.dtype)] + scratch
        out_shape = jax.ShapeDtypeStruct((rows, cols), x.dtype)
    else:
        out_shape = jax.ShapeDtypeStruct(x.shape, x.dtype)
    return pl.pallas_call(
        body, name=name, out_shape=out_shape,
        in_specs=[pl.BlockSpec(memory_space=pltpu.VMEM)],
        out_specs=pl.BlockSpec(memory_space=pltpu.VMEM),
        scratch_shapes=scratch,
        compiler_params=pltpu.CompilerParams(vmem_limit_bytes=VMEM_LIMIT),
    )(x)


def _other_chips(mx, my):
    return [(1 - mx, my), (mx, 1 - my), (1 - mx, 1 - my)]


HBM_SPEC = pl.BlockSpec(memory_space=pltpu.HBM)
SEM_SPEC = pl.BlockSpec(memory_space=pltpu.SEMAPHORE)
ANY_SPEC = pl.BlockSpec(memory_space=pl.ANY)
SPLIT_EFFECT = pltpu.SideEffectType.DATAFLOW_SIDE_EFFECTING


def _quarter_copies(mode, srcs, lands, send_sems, recv_sems):
    mx, my, mc = _my_place()
    myq = 2 * mx + my
    copies = []
    for a in range(len(lands)):
        for k, (px, py) in enumerate(_other_chips(mx, my)):
            if mode == "gather":
                src, dst = lands[a].at[myq], lands[a].at[myq]
            else:
                src, dst = srcs[a].at[2 * px + py], lands[a].at[k]
            copies.append(pltpu.make_async_remote_copy(
                src_ref=src, dst_ref=dst, send_sem=send_sems[a].at[k], recv_sem=recv_sems[a].at[k],
                device_id=(px, py, mc), device_id_type=MESH_ID))
    return copies


def _split_start(name, mode, srcs, lands, after):
    ns, n = len(srcs), len(lands)

    def body(*refs):
        outs = refs[ns + n + 1:]
        for cp in _quarter_copies(mode, refs[:ns], refs[ns:ns + n], outs[:n], outs[n:2 * n]):
            cp.start()
        token = outs[-1]
        token[...] = jnp.zeros_like(token)

    arrays = list(srcs) + list(lands)
    res = pl.pallas_call(
        body, name=name,
        out_shape=[pltpu.SemaphoreType.DMA((3,))] * (2 * n) + [pltpu.HBM(x.shape, x.dtype) for x in arrays]
        + [jax.ShapeDtypeStruct((8, LANES), F32)],
        in_specs=[HBM_SPEC] * (ns + n) + [ANY_SPEC],
        out_specs=[SEM_SPEC] * (2 * n) + [HBM_SPEC] * (ns + n) + [pl.BlockSpec(memory_space=pltpu.VMEM)],
        input_output_aliases={i: 2 * n + i for i in range(ns + n)},
        compiler_params=pltpu.CompilerParams(has_side_effects=SPLIT_EFFECT),
    )(*[pltpu.with_memory_space_constraint(x, pltpu.HBM) for x in arrays], after)
    return res[:n], res[n:2 * n], res[2 * n:2 * n + ns], res[2 * n + ns:2 * n + ns + n], res[-1]


def _split_wait(name, mode, send_sems, recv_sems, srcs, lands, after):
    ns, n = len(srcs), len(lands)

    def body(*refs):
        sems = refs[ns + n:ns + 3 * n]
        for cp in _quarter_copies(mode, refs[:ns], refs[ns:ns + n], sems[:n], sems[n:]):
            cp.wait_send()
            cp.wait_recv()

    arrays = list(srcs) + list(lands)
    res = pl.pallas_call(
        body, name=name,
        out_shape=[pltpu.HBM(x.shape, x.dtype) for x in arrays],
        in_specs=[HBM_SPEC] * (ns + n) + [SEM_SPEC] * (2 * n) + [ANY_SPEC],
        out_specs=[HBM_SPEC] * (ns + n),
        input_output_aliases={i: i for i in range(ns + n)},
        compiler_params=pltpu.CompilerParams(has_side_effects=SPLIT_EFFECT),
    )(*arrays, *send_sems, *recv_sems, after)
    return res[:ns], res[ns:]


def _swap_sibling(parts, name):
    n = len(parts)

    def body(*refs):
        ins, outs = refs[:n], refs[n:2 * n]
        send_sems, recv_sems = refs[2 * n:]
        mx, my, mc = _my_place()
        copies = []
        for a in range(n):
            cp = pltpu.make_async_remote_copy(
                src_ref=ins[a], dst_ref=outs[a], send_sem=send_sems.at[a], recv_sem=recv_sems.at[a],
                device_id=(mx, my, 1 - mc), device_id_type=MESH_ID)
            cp.start()
            copies.append(cp)
        for cp in copies:
            cp.wait()

    any_spec = pl.BlockSpec(memory_space=pl.ANY)
    return pl.pallas_call(
        body, name=name,
        out_shape=[jax.ShapeDtypeStruct(p.shape, p.dtype) for p in parts],
        in_specs=[any_spec] * n, out_specs=[any_spec] * n,
        scratch_shapes=[pltpu.SemaphoreType.DMA((n,)), pltpu.SemaphoreType.DMA((n,))],
        compiler_params=pltpu.CompilerParams(has_side_effects=True),
    )(*parts)


def _ada_forward(c_all, w_ada, b_ada3, myq):
    nl, d, cq = w_ada.shape
    nb = c_all.shape[0]

    def body(q_ref, c_ref, w_ref, b_ref, o_ref):
        c = c_ref[...]
        act = c * _sigmoid(c)
        o_ref[...] = _dot(act, w_ref[...]) + b_ref[...]

    return pl.pallas_call(
        body, name="ada_forward",
        out_shape=jax.ShapeDtypeStruct((nl, nb, cq), F32),
        grid_spec=pltpu.PrefetchScalarGridSpec(
            num_scalar_prefetch=1, grid=(nl,),
            in_specs=[pl.BlockSpec((nb, d), lambda l, q: (0, 0)),
                      pl.BlockSpec((None, d, cq), lambda l, q: (l, 0, 0)),
                      pl.BlockSpec((None, 1, cq), lambda l, q: (l, 0, q[0]))],
            out_specs=pl.BlockSpec((None, nb, cq), lambda l, q: (l, 0, 0))),
        compiler_params=_params(("arbitrary",)),
    )(myq, c_all, w_ada, b_ada3)


def _ada_backward(c_all, dmod_all, myq, cq):
    nb, d = c_all.shape
    nl = dmod_all.shape[0]
    full = dmod_all.shape[2]

    def body(q_ref, c_ref, dq_ref, dall_ref, gw_ref, gb_ref):
        c = c_ref[...]
        act = c * _sigmoid(c)
        gw_ref[...] = _dot_tn(act, dq_ref[...])
        gb_ref[...] = _colsum(dall_ref[...])

    return pl.pallas_call(
        body, name="ada_backward",
        out_shape=[jax.ShapeDtypeStruct((nl, d, cq), F32), jax.ShapeDtypeStruct((nl, 1, full), F32)],
        grid_spec=pltpu.PrefetchScalarGridSpec(
            num_scalar_prefetch=1, grid=(nl,),
            in_specs=[pl.BlockSpec((nb, d), lambda l, q: (0, 0)),
                      pl.BlockSpec((None, nb, cq), lambda l, q: (l, 0, q[0])),
                      pl.BlockSpec((None, nb, full), lambda l, q: (l, 0, 0))],
            out_specs=[pl.BlockSpec((None, d, cq), lambda l, q: (l, 0, 0)),
                       pl.BlockSpec((None, 1, full), lambda l, q: (l, 0, 0))]),
        compiler_params=_params(("arbitrary",)),
    )(myq, c_all, dmod_all, dmod_all)


def _in_proj(l, x, mod, g1, wg_in, t_len):
    n, d = x.shape
    tm = min(TOKEN_TILE, t_len)
    tpb = t_len // tm
    qc = wg_in.shape[-1]

    def body(x_ref, mod_ref, g_ref, w_ref, h_ref, proj_ref, h_s):
        @pl.when(pl.program_id(1) == 0)
        def _():
            h = _modnorm_fwd(x_ref[...], g_ref[...], mod_ref[1:2, :], mod_ref[0:1, :])
            h_s[...] = h.astype(MXU_DTYPE)
            h_ref[...] = h.astype(ACT_DTYPE)
        proj_ref[...] = jnp.dot(h_s[...], w_ref[...], preferred_element_type=F32).astype(ACT_DTYPE)

    return pl.pallas_call(
        body, name=f"in_proj_{l}",
        out_shape=[jax.ShapeDtypeStruct((n, d), ACT_DTYPE), jax.ShapeDtypeStruct((n, N_CHIP * qc), ACT_DTYPE)],
        grid=(n // tm, N_CHIP),
        in_specs=[pl.BlockSpec((tm, d), lambda i, j: (i, 0)),
                  pl.BlockSpec((None, None, 8, d), lambda i, j: (l, i // tpb, 0, 0)),
                  pl.BlockSpec((None, 1, d), lambda i, j: (l, 0, 0)),
                  pl.BlockSpec((None, d, qc), lambda i, j: (j, 0, 0))],
        out_specs=[pl.BlockSpec((tm, d), lambda i, j: (i, 0)),
                   pl.BlockSpec((tm, qc), lambda i, j: (i, j))],
        scratch_shapes=[pltpu.VMEM((tm, d), MXU_DTYPE)],
        compiler_params=_params(("arbitrary", "arbitrary")),
    )(x, mod, g1, wg_in)


def _masked_ws(ws_ref, wm_s):
    mask = _causal_mask()
    for h in range(HEADS):
        wm_s[h] = jnp.where(mask, ws_ref[h], 0.0).astype(MXU_DTYPE)


def _fill_z(i, tpb, a_ref, g_ref, ah_ref, gh_ref, zext):
    ah = ah_ref[...].astype(F32)
    gh = gh_ref[...].astype(F32)
    keep = jnp.where(i % tpb == 0, 0.0, 1.0)
    zext[0:HALO, :] = ah * _sigmoid(gh) * keep


def _conv_taps(src, w_ref, r0, lanes, flip):
    width = CONV_ROWS + HALO
    win = src[pl.ds(r0, width), lanes]
    rot = _shifted(win, width)
    acc = jnp.zeros((CONV_ROWS, LANES), F32)
    for s in range(0, HALO + 1):
        k = (CONV_TAPS - 1 - s) if flip else (s - 2)
        if k < 0 or k >= CONV_TAPS:
            continue
        q, r = divmod(s, 8)
        acc = acc + rot[r][8 * q:8 * q + CONV_ROWS] * w_ref[k:k + 1, lanes]
    return acc


def _branches_fwd(l, proj, lng, lnb, ws, bst, cw, cb, blg, blb, t_len):
    n = proj.shape[0]
    d = lng.shape[-1]
    tm = min(TOKEN_TILE, t_len)
    tpb = t_len // tm
    per = tm // HALO
    nchunk = tm // CHUNK

    def body(u_ref, v_ref, a_ref, g_ref, ah_ref, gh_ref, lng_ref, lnb_ref, ws_ref, bst_ref, cw_ref, cb_ref,
             blg_ref, blb_ref, ya_ref, yb_ref, zc_ref, wm_s, zext):
        i = pl.program_id(0)
        _masked_ws(ws_ref, wm_s)
        _fill_z(i, tpb, a_ref, g_ref, ah_ref, gh_ref, zext)

        def chunk(c, carry):
            r0 = pl.multiple_of(c * CHUNK, CHUNK)
            rows = pl.ds(r0, CHUNK)
            vhat, _ = _ln_stats(v_ref[rows, :].astype(F32))
            vn = (vhat * lng_ref[...] + lnb_ref[...]).astype(MXU_DTYPE)
            u = u_ref[rows, :].astype(F32)
            for h in range(HEADS):
                cols = slice(h * CHUNK, (h + 1) * CHUNK)
                s = jnp.dot(wm_s[h], vn[:, cols], preferred_element_type=F32) + bst_ref[:, h:h + 1]
                ya_ref[rows, cols] = (u[:, cols] * s).astype(ACT_DTYPE)
            a = a_ref[rows, :].astype(F32)
            g = g_ref[rows, :].astype(F32)
            zext[pl.ds(HALO + r0, CHUNK), :] = a * _sigmoid(g)
            return carry

        lax.fori_loop(0, nchunk, chunk, 0)

        for lc in range(d // LANES):
            lanes = slice(lc * LANES, (lc + 1) * LANES)

            def block(rb, carry):
                r0 = pl.multiple_of(rb * CONV_ROWS, CONV_ROWS)
                acc = _conv_taps(zext, cw_ref, r0, lanes, flip=False) + cb_ref[:, lanes]
                zc_ref[pl.ds(r0, CONV_ROWS), lanes] = acc.astype(ACT_DTYPE)
                return carry

            lax.fori_loop(0, tm // CONV_ROWS, block, 0)

        def chunk2(c, carry):
            r0 = pl.multiple_of(c * CHUNK, CHUNK)
            rows = pl.ds(r0, CHUNK)
            zhat, _ = _ln_stats(zc_ref[rows, :].astype(F32))
            zn = zhat * blg_ref[...] + blb_ref[...]
            yb_ref[rows, :] = (zn * _sigmoid(zn)).astype(ACT_DTYPE)
            return carry

        lax.fori_loop(0, nchunk, chunk2, 0)

    col = lambda k: pl.BlockSpec((tm, d), lambda i: (i, k))
    halo = lambda k: pl.BlockSpec((HALO, d), lambda i: (jnp.maximum(i * per - 1, 0), k))
    vec = pl.BlockSpec((None, 1, d), lambda i: (l, 0, 0))
    out = pl.BlockSpec((tm, d), lambda i: (i, 0))
    return pl.pallas_call(
        body, name=f"branches_fwd_{l}",
        out_shape=[jax.ShapeDtypeStruct((n, d), ACT_DTYPE)] * 3,
        grid=(n // tm,),
        in_specs=[col(0), col(1), col(2), col(3), halo(2), halo(3), vec, vec,
                  pl.BlockSpec((None, HEADS, CHUNK, CHUNK), lambda i: (l, 0, 0, 0)),
                  pl.BlockSpec((None, CHUNK, HEADS), lambda i: (l, 0, 0)),
                  pl.BlockSpec((None, HALO, d), lambda i: (l, 0, 0)), vec, vec, vec],
        out_specs=[out, out, out],
        scratch_shapes=[pltpu.VMEM((HEADS, CHUNK, CHUNK), MXU_DTYPE), pltpu.VMEM((HALO + tm, d), F32)],
        compiler_params=_params(("arbitrary",)),
    )(proj, proj, proj, proj, proj, proj, lng, lnb, ws, bst, cw, cb, blg, blb)


def _merge_out(l, x, mod, proj, ya_in, yb_in, wg_pa, wg_pb, wg_out, t_len):
    n, d = x.shape
    tm = min(TOKEN_TILE, t_len)
    tpb = t_len // tm
    rq = d // N_CHIP

    def body(x_ref, mod_ref, ga_ref, gb_ref, yai_ref, ybi_ref, wpa_ref, wpb_ref, wo_ref,
             ya_ref, yb_ref, mg_ref, o_ref, x1_ref):
        wpa = wpa_ref[...].reshape(d, d)
        wpb = wpb_ref[...].reshape(d, d)
        wo = wo_ref[...].reshape(d, d)
        ya = jnp.dot(yai_ref[...].astype(MXU_DTYPE), wpa, preferred_element_type=F32)
        yb = jnp.dot(ybi_ref[...].astype(MXU_DTYPE), wpb, preferred_element_type=F32)
        merged = _sigmoid(ga_ref[...].astype(F32)) * ya + _sigmoid(gb_ref[...].astype(F32)) * yb
        o = _dot(merged, wo)
        ya_ref[...] = ya.astype(ACT_DTYPE)
        yb_ref[...] = yb.astype(ACT_DTYPE)
        mg_ref[...] = merged.astype(ACT_DTYPE)
        o_ref[...] = o.astype(ACT_DTYPE)
        x1_ref[...] = x_ref[...] + mod_ref[2:3, :] * o

    tile = pl.BlockSpec((tm, d), lambda i: (i, 0))
    wspec = pl.BlockSpec((N_CHIP, rq, d), lambda i: (0, 0, 0))
    return pl.pallas_call(
        body, name=f"merge_out_{l}",
        out_shape=[jax.ShapeDtypeStruct((n, d), ACT_DTYPE)] * 4 + [jax.ShapeDtypeStruct((n, d), F32)],
        grid=(n // tm,),
        in_specs=[tile, pl.BlockSpec((None, None, 8, d), lambda i: (l, i // tpb, 0, 0)),
                  pl.BlockSpec((tm, d), lambda i: (i, 4)), pl.BlockSpec((tm, d), lambda i: (i, 5)),
                  tile, tile, wspec, wspec, wspec],
        out_specs=[tile] * 5,
        compiler_params=_params(("arbitrary",)),
    )(x, mod, proj, proj, ya_in, yb_in, wg_pa, wg_pb, wg_out)


def _ffn_fwd(l, x1, mod, g2, wg_ff1, wg_ff2, t_len):
    n, d = x1.shape
    tm = min(TOKEN_TILE, t_len)
    tpb = t_len // tm
    hq = wg_ff1.shape[-1]

    def body(x_ref, mod_ref, g_ref, w1_ref, w2_ref, h_ref, f_ref, o2_ref, x2_ref, h_s, acc):
        j = pl.program_id(1)

        @pl.when(j == 0)
        def _():
            h = _modnorm_fwd(x_ref[...], g_ref[...], mod_ref[4:5, :], mod_ref[3:4, :])
            h_s[...] = h.astype(MXU_DTYPE)
            h_ref[...] = h.astype(ACT_DTYPE)
            acc[...] = jnp.zeros_like(acc)

        f = jnp.dot(h_s[...], w1_ref[...], preferred_element_type=F32)
        f_ref[...] = f.astype(ACT_DTYPE)
        acc[...] += _dot(jnp.square(jnp.maximum(f, 0.0)), w2_ref[...])

        @pl.when(j == N_CHIP - 1)
        def _():
            o2 = acc[...]
            o2_ref[...] = o2.astype(ACT_DTYPE)
            x2_ref[...] = x_ref[...] + mod_ref[5:6, :] * o2

    tile = pl.BlockSpec((tm, d), lambda i, j: (i, 0))
    return pl.pallas_call(
        body, name=f"ffn_fwd_{l}",
        out_shape=[jax.ShapeDtypeStruct((n, d), ACT_DTYPE), jax.ShapeDtypeStruct((n, N_CHIP * hq), ACT_DTYPE),
                   jax.ShapeDtypeStruct((n, d), ACT_DTYPE), jax.ShapeDtypeStruct((n, d), F32)],
        grid=(n // tm, N_CHIP),
        in_specs=[tile, pl.BlockSpec((None, None, 8, d), lambda i, j: (l, i // tpb, 0, 0)),
                  pl.BlockSpec((None, 1, d), lambda i, j: (l, 0, 0)),
                  pl.BlockSpec((None, d, hq), lambda i, j: (j, 0, 0)),
                  pl.BlockSpec((None, hq, d), lambda i, j: (j, 0, 0))],
        out_specs=[tile, pl.BlockSpec((tm, hq), lambda i, j: (i, j)), tile, tile],
        scratch_shapes=[pltpu.VMEM((tm, d), MXU_DTYPE), pltpu.VMEM((tm, d), F32)],
        compiler_params=_params(("arbitrary", "arbitrary")),
    )(x1, mod, g2, wg_ff1, wg_ff2)


def _loss_head(x, final_g, target):
    n, d = x.shape
    tm = min(TOKEN_TILE, n)

    def body(x_ref, g_ref, t_ref, loss_ref, dx_ref, dg_ref):
        @pl.when(pl.program_id(0) == 0)
        def _():
            loss_ref[...] = jnp.zeros_like(loss_ref)
            dg_ref[...] = jnp.zeros_like(dg_ref)

        x_t = x_ref[...]
        g = g_ref[...]
        r = lax.rsqrt(_rowmean(x_t * x_t) + EPS)
        xn = x_t * r
        e = xn * g - t_ref[...]
        loss_ref[...] += jnp.sum(e * e) * (0.5 / d)
        dy = e * (1.0 / d)
        dxn = dy * g
        dx_ref[...] = r * (dxn - xn * _rowmean(dxn * xn))
        dg_ref[0:1, :] += _colsum(dy * xn)

    tile = pl.BlockSpec((tm, d), lambda i: (i, 0))
    return pl.pallas_call(
        body, name="loss_head",
        out_shape=[jax.ShapeDtypeStruct((8, LANES), F32), jax.ShapeDtypeStruct((n, d), F32),
                   jax.ShapeDtypeStruct((8, d), F32)],
        grid=(n // tm,),
        in_specs=[tile, pl.BlockSpec((1, d), lambda i: (0, 0)), tile],
        out_specs=[pl.BlockSpec((8, LANES), lambda i: (0, 0)), tile, pl.BlockSpec((8, d), lambda i: (0, 0))],
        compiler_params=_params(("arbitrary",)),
    )(x, final_g, target)


def _norm_tail(i, tpb, x_ref, dxin_ref, dh, g_ref, sc, dx_ref, dmod_ref, dg_ref, row_sh, row_sc):
    dxm, dsh, q = _modnorm_bwd(x_ref[...], dh, g_ref[...], sc)
    dx_ref[...] = dxin_ref[...] + dxm
    dmod_ref[row_sh:row_sh + 1, :] += dsh
    dmod_ref[row_sc:row_sc + 1, :] += g_ref[...] * q
    dg_ref[0:1, :] += (1.0 + sc) * q


def _ffn_bwd(l, dx2, x1, mod, g2, o2, f, wg_ff1, wg_ff2, t_len, nb):
    n, d = dx2.shape
    tm = min(TOKEN_TILE, t_len)
    tpb = t_len // tm
    hq = wg_ff1.shape[-1]

    def body(dx2_ref, x1_ref, mod_ref, g_ref, o2_ref, f_ref, w1_ref, w2_ref,
             do2_ref, df_ref, dx1_ref, dmod_ref, dg_ref, do_s, acc):
        i, j = pl.program_id(0), pl.program_id(1)

        @pl.when((i == 0) & (j == 0))
        def _():
            dg_ref[...] = jnp.zeros_like(dg_ref)

        @pl.when((i % tpb == 0) & (j == 0))
        def _():
            dmod_ref[...] = jnp.zeros_like(dmod_ref)

        @pl.when(j == 0)
        def _():
            dx2_t = dx2_ref[...]
            dmod_ref[5:6, :] += _colsum(dx2_t * o2_ref[...].astype(F32))
            do2 = dx2_t * mod_ref[5:6, :]
            do_s[...] = do2.astype(MXU_DTYPE)
            do2_ref[...] = do2.astype(ACT_DTYPE)
            acc[...] = jnp.zeros_like(acc)

        da2 = _dot_nt(do_s[...], w2_ref[...])
        df = da2 * (2.0 * jnp.maximum(f_ref[...].astype(F32), 0.0))
        df_ref[...] = df.astype(ACT_DTYPE)
        acc[...] += _dot_nt(df, w1_ref[...])

        @pl.when(j == N_CHIP - 1)
        def _():
            _norm_tail(i, tpb, x1_ref, dx2_ref, acc[...], g_ref, mod_ref[4:5, :], dx1_ref, dmod_ref, dg_ref, 3, 4)

    tile = pl.BlockSpec((tm, d), lambda i, j: (i, 0))
    wide = pl.BlockSpec((tm, hq), lambda i, j: (i, j))
    return pl.pallas_call(
        body, name=f"ffn_bwd_{l}",
        out_shape=[jax.ShapeDtypeStruct((n, d), ACT_DTYPE), jax.ShapeDtypeStruct((n, N_CHIP * hq), ACT_DTYPE),
                   jax.ShapeDtypeStruct((n, d), F32), jax.ShapeDtypeStruct((nb, 8, d), F32),
                   jax.ShapeDtypeStruct((8, d), F32)],
        grid=(n // tm, N_CHIP),
        in_specs=[tile, tile, pl.BlockSpec((None, None, 8, d), lambda i, j: (l, i // tpb, 0, 0)),
                  pl.BlockSpec((None, 1, d), lambda i, j: (l, 0, 0)), tile, wide,
                  pl.BlockSpec((None, d, hq), lambda i, j: (j, 0, 0)),
                  pl.BlockSpec((None, hq, d), lambda i, j: (j, 0, 0))],
        out_specs=[tile, wide, tile, pl.BlockSpec((None, 8, d), lambda i, j: (i // tpb, 0, 0)),
                   pl.BlockSpec((8, d), lambda i, j: (0, 0))],
        scratch_shapes=[pltpu.VMEM((tm, d), MXU_DTYPE), pltpu.VMEM((tm, d), F32)],
        compiler_params=_params(("arbitrary", "arbitrary")),
    )(dx2, x1, mod, g2, o2, f, wg_ff1, wg_ff2)


def _merge_bwd(l, dx1, mod, o, ya, yb, proj, wg_pa, wg_pb, wg_out, t_len, nb):
    n, d = dx1.shape
    tm = min(TOKEN_TILE, t_len)
    tpb = t_len // tm
    rq = d // N_CHIP

    def body(dx_ref, mod_ref, o_ref, ya_ref, yb_ref, ga_ref, gb_ref, wpa_ref, wpb_ref, wo_ref,
             do_ref, dya_ref, dyb_ref, dyai_ref, dybi_ref, dproj_ref, dmod_ref):
        i = pl.program_id(0)

        @pl.when(i % tpb == 0)
        def _():
            dmod_ref[...] = jnp.zeros_like(dmod_ref)

        dx = dx_ref[...]
        dmod_ref[2:3, :] += _colsum(dx * o_ref[...].astype(F32))
        do = (dx * mod_ref[2:3, :]).astype(MXU_DTYPE)
        do_ref[...] = do.astype(ACT_DTYPE)
        dm = _dot_nt(do, wo_ref[...].reshape(d, d))
        sa = _sigmoid(ga_ref[...].astype(F32))
        sb = _sigmoid(gb_ref[...].astype(F32))
        dya = (dm * sa).astype(MXU_DTYPE)
        dyb = (dm * sb).astype(MXU_DTYPE)
        dya_ref[...] = dya.astype(ACT_DTYPE)
        dyb_ref[...] = dyb.astype(ACT_DTYPE)
        dproj_ref[:, 0:d] = (dm * ya_ref[...].astype(F32) * sa * (1.0 - sa)).astype(ACT_DTYPE)
        dproj_ref[:, d:2 * d] = (dm * yb_ref[...].astype(F32) * sb * (1.0 - sb)).astype(ACT_DTYPE)
        dyai_ref[...] = _dot_nt(dya, wpa_ref[...].reshape(d, d)).astype(ACT_DTYPE)
        dybi_ref[...] = _dot_nt(dyb, wpb_ref[...].reshape(d, d)).astype(ACT_DTYPE)

    tile = pl.BlockSpec((tm, d), lambda i: (i, 0))
    wspec = pl.BlockSpec((N_CHIP, rq, d), lambda i: (0, 0, 0))
    return pl.pallas_call(
        body, name=f"merge_bwd_{l}",
        out_shape=[jax.ShapeDtypeStruct((n, d), ACT_DTYPE)] * 5
        + [jax.ShapeDtypeStruct((n, 6 * d), ACT_DTYPE), jax.ShapeDtypeStruct((nb, 8, d), F32)],
        grid=(n // tm,),
        in_specs=[tile, pl.BlockSpec((None, None, 8, d), lambda i: (l, i // tpb, 0, 0)), tile, tile, tile,
                  pl.BlockSpec((tm, d), lambda i: (i, 4)), pl.BlockSpec((tm, d), lambda i: (i, 5)),
                  wspec, wspec, wspec],
        out_specs=[tile] * 5 + [pl.BlockSpec((tm, 2 * d), lambda i: (i, 2)),
                                pl.BlockSpec((None, 8, d), lambda i: (i // tpb, 0, 0))],
        compiler_params=_params(("arbitrary",)),
    )(dx1, mod, o, ya, yb, proj, proj, wg_pa, wg_pb, wg_out)


def _branches_bwd(l, proj, zc, dya_in, dyb_in, dproj, lng, lnb, ws, bst, cw, blg, blb, t_len):
    n = proj.shape[0]
    d = lng.shape[-1]
    tm = min(TOKEN_TILE, t_len)
    tpb = t_len // tm
    per = tm // HALO
    nchunk = tm // CHUNK
    ntile = n // tm
    nblk = tm // CONV_ROWS

    def body(u_ref, v_ref, a_ref, g_ref, ah_ref, gh_ref, zc_ref, zcn_ref, dya_ref, dyb_ref, dybn_ref, dproj_in,
             lng_ref, lnb_ref, ws_ref, bst_ref, cw_ref, blg_ref, blb_ref,
             dproj_ref, dws_ref, dbst_ref, dcw_ref, vec_ref, wm_s, zext, dzext, dvn_s):
        i = pl.program_id(0)

        @pl.when(i == 0)
        def _():
            dws_ref[...] = jnp.zeros_like(dws_ref)
            dbst_ref[...] = jnp.zeros_like(dbst_ref)
            dcw_ref[...] = jnp.zeros_like(dcw_ref)
            vec_ref[...] = jnp.zeros_like(vec_ref)

        _masked_ws(ws_ref, wm_s)
        _fill_z(i, tpb, a_ref, g_ref, ah_ref, gh_ref, zext)

        def conv_ln_bwd(zc_t, dyb_t):
            zhat, rstd = _ln_stats(zc_t)
            zn = zhat * blg_ref[...] + blb_ref[...]
            sg = _sigmoid(zn)
            dzn = dyb_t * (sg * (1.0 + zn * (1.0 - sg)))
            return _ln_bwd(dzn, zhat, rstd, blg_ref[...]), _colsum(dzn * zhat), _colsum(dzn)

        def chunk(c, carry):
            r0 = pl.multiple_of(c * CHUNK, CHUNK)
            rows = pl.ds(r0, CHUNK)
            vhat, rstd = _ln_stats(v_ref[rows, :].astype(F32))
            vn = (vhat * lng_ref[...] + lnb_ref[...]).astype(MXU_DTYPE)
            u = u_ref[rows, :].astype(F32)
            dya = dya_ref[rows, :].astype(F32)
            for h in range(HEADS):
                cols = slice(h * CHUNK, (h + 1) * CHUNK)
                s = jnp.dot(wm_s[h], vn[:, cols], preferred_element_type=F32) + bst_ref[:, h:h + 1]
                dproj_ref[rows, cols] = (dya[:, cols] * s).astype(ACT_DTYPE)
                ds = dya[:, cols] * u[:, cols]
                dvn_s[:, cols] = _dot_tn(wm_s[h], ds)
                dws_ref[h] += _dot_nt(ds, vn[:, cols])
                dbst_ref[:, h:h + 1] += jnp.sum(ds, axis=1, keepdims=True)
            dvn = dvn_s[...]
            dproj_ref[rows, d:2 * d] = _ln_bwd(dvn, vhat, rstd, lng_ref[...]).astype(ACT_DTYPE)
            vec_ref[0:1, :] += _colsum(dvn * vhat)
            vec_ref[1:2, :] += _colsum(dvn)
            a = a_ref[rows, :].astype(F32)
            g = g_ref[rows, :].astype(F32)
            zext[pl.ds(HALO + r0, CHUNK), :] = a * _sigmoid(g)
            dzc, dblg, dblb = conv_ln_bwd(zc_ref[rows, :].astype(F32), dyb_ref[rows, :].astype(F32))
            dzext[rows, :] = dzc
            vec_ref[2:3, :] += _colsum(dzc)
            vec_ref[3:4, :] += dblg
            vec_ref[4:5, :] += dblb
            return carry

        lax.fori_loop(0, nchunk, chunk, 0)

        dzc_next, _, _ = conv_ln_bwd(zcn_ref[...].astype(F32), dybn_ref[...].astype(F32))
        dzext[tm:tm + HALO, :] = dzc_next * jnp.where(i % tpb == tpb - 1, 0.0, 1.0)

        for lc in range(d // LANES):
            lanes = slice(lc * LANES, (lc + 1) * LANES)

            def block(rb, carry):
                r0 = pl.multiple_of(rb * CONV_ROWS, CONV_ROWS)
                rows = pl.ds(r0, CONV_ROWS)
                dz = _conv_taps(dzext, cw_ref, r0, lanes, flip=True)
                a = a_ref[rows, lanes].astype(F32)
                sg = _sigmoid(g_ref[rows, lanes].astype(F32))
                dproj_ref[rows, 2 * d + lc * LANES:2 * d + (lc + 1) * LANES] = (dz * sg).astype(ACT_DTYPE)
                dproj_ref[rows, 3 * d + lc * LANES:3 * d + (lc + 1) * LANES] = (
                    dz * a * sg * (1.0 - sg)).astype(ACT_DTYPE)
                return carry

            lax.fori_loop(0, nblk, block, 0)

            def wblock(rb, accs):
                r0 = pl.multiple_of(rb * CONV_ROWS, CONV_ROWS)
                width = CONV_ROWS + HALO
                rot = _shifted(zext[pl.ds(r0, width), lanes], width)
                dzc = dzext[pl.ds(r0, CONV_ROWS), lanes]
                out = []
                for k in range(CONV_TAPS):
                    q, r = divmod(k + 2, 8)
                    prod = dzc * rot[r][8 * q:8 * q + CONV_ROWS]
                    part = prod[0:8]
                    for e in range(1, CONV_ROWS // 8):
                        part = part + prod[8 * e:8 * e + 8]
                    out.append(accs[k] + part)
                return tuple(out)

            accs = lax.fori_loop(0, nblk, wblock, tuple(jnp.zeros((8, LANES), F32) for _ in range(CONV_TAPS)))
            for k in range(CONV_TAPS):
                dcw_ref[k:k + 1, lanes] += _colsum(accs[k])

        @pl.when(i == ntile - 1)
        def _():
            mask = _causal_mask()
            for h in range(HEADS):
                dws_ref[h] = jnp.where(mask, dws_ref[h], 0.0)

    col = lambda k: pl.BlockSpec((tm, d), lambda i: (i, k))
    tile = pl.BlockSpec((tm, d), lambda i: (i, 0))
    before = lambda k: pl.BlockSpec((HALO, d), lambda i: (jnp.maximum(i * per - 1, 0), k))
    after = pl.BlockSpec((HALO, d), lambda i: (jnp.minimum((i + 1) * per, n // HALO - 1), 0))
    vec = pl.BlockSpec((None, 1, d), lambda i: (l, 0, 0))
    const2 = lambda r, c: pl.BlockSpec((r, c), lambda i: (0, 0))
    return pl.pallas_call(
        body, name=f"branches_bwd_{l}",
        out_shape=[jax.ShapeDtypeStruct((n, 6 * d), ACT_DTYPE), jax.ShapeDtypeStruct((HEADS, CHUNK, CHUNK), F32),
                   jax.ShapeDtypeStruct((CHUNK, HEADS), F32), jax.ShapeDtypeStruct((HALO, d), F32),
                   jax.ShapeDtypeStruct((8, d), F32)],
        grid=(ntile,),
        in_specs=[col(0), col(1), col(2), col(3), before(2), before(3), tile, after, tile, tile, after,
                  pl.BlockSpec(memory_space=pl.ANY), vec, vec,
                  pl.BlockSpec((None, HEADS, CHUNK, CHUNK), lambda i: (l, 0, 0, 0)),
                  pl.BlockSpec((None, CHUNK, HEADS), lambda i: (l, 0, 0)),
                  pl.BlockSpec((None, HALO, d), lambda i: (l, 0, 0)), vec, vec],
        out_specs=[pl.BlockSpec((tm, 4 * d), lambda i: (i, 0)),
                   pl.BlockSpec((HEADS, CHUNK, CHUNK), lambda i: (0, 0, 0)),
                   const2(CHUNK, HEADS), const2(HALO, d), const2(8, d)],
        scratch_shapes=[pltpu.VMEM((HEADS, CHUNK, CHUNK), MXU_DTYPE), pltpu.VMEM((HALO + tm, d), F32),
                        pltpu.VMEM((tm + HALO, d), F32), pltpu.VMEM((CHUNK, d), F32)],
        input_output_aliases={11: 0},
        compiler_params=_params(("arbitrary",)),
    )(proj, proj, proj, proj, proj, proj, zc, zc, dya_in, dyb_in, dyb_in, dproj, lng, lnb, ws, bst, cw, blg, blb)


def _in_proj_bwd(l, dproj, dx1, x, mod, g1, wg_in, t_len, nb):
    n, d = x.shape
    tm = min(TOKEN_TILE, t_len)
    tpb = t_len // tm
    qc = wg_in.shape[-1]

    def body(dp_ref, dx1_ref, x_ref, mod_ref, g_ref, w_ref, dx_ref, dmod_ref, dg_ref, acc):
        i, j = pl.program_id(0), pl.program_id(1)

        @pl.when((i == 0) & (j == 0))
        def _():
            dg_ref[...] = jnp.zeros_like(dg_ref)

        @pl.when((i % tpb == 0) & (j == 0))
        def _():
            dmod_ref[...] = jnp.zeros_like(dmod_ref)

        @pl.when(j == 0)
        def _():
            acc[...] = jnp.zeros_like(acc)

        acc[...] += _dot_nt(dp_ref[...], w_ref[...])

        @pl.when(j == N_CHIP - 1)
        def _():
            _norm_tail(i, tpb, x_ref, dx1_ref, acc[...], g_ref, mod_ref[1:2, :], dx_ref, dmod_ref, dg_ref, 0, 1)

    tile = pl.BlockSpec((tm, d), lambda i, j: (i, 0))
    return pl.pallas_call(
        body, name=f"in_proj_bwd_{l}",
        out_shape=[jax.ShapeDtypeStruct((n, d), F32), jax.ShapeDtypeStruct((nb, 8, d), F32),
                   jax.ShapeDtypeStruct((8, d), F32)],
        grid=(n // tm, N_CHIP),
        in_specs=[pl.BlockSpec((tm, qc), lambda i, j: (i, j)), tile, tile,
                  pl.BlockSpec((None, None, 8, d), lambda i, j: (l, i // tpb, 0, 0)),
                  pl.BlockSpec((None, 1, d), lambda i, j: (l, 0, 0)),
                  pl.BlockSpec((None, d, qc), lambda i, j: (j, 0, 0))],
        out_specs=[tile, pl.BlockSpec((None, 8, d), lambda i, j: (i // tpb, 0, 0)),
                   pl.BlockSpec((8, d), lambda i, j: (0, 0))],
        scratch_shapes=[pltpu.VMEM((tm, d), F32)],
        compiler_params=_params(("arbitrary", "arbitrary")),
    )(dproj, dx1, x, mod, g1, wg_in)


def _weight_grad(name, a, b, a_spec, b_spec, out_rows, out_spec, acc_shape, grid_ij, relu2=False):
    n = a.shape[0]
    tk = min(TOKEN_TILE, n)
    nk = n // tk
    cols = acc_shape[1]

    def body(a_ref, b_ref, o_ref, acc):
        k = pl.program_id(2)

        @pl.when(k == 0)
        def _():
            acc[...] = jnp.zeros_like(acc)

        a_t = a_ref[...]
        if relu2:
            a_t = jnp.square(jnp.maximum(a_t.astype(F32), 0.0))
        acc[...] += _dot_tn(a_t, b_ref[...])

        @pl.when(k == nk - 1)
        def _():
            o_ref[...] = acc[...].reshape(o_ref.shape).astype(WIRE_DTYPE)

    gi, gj = grid_ij
    return pl.pallas_call(
        body, name=name, out_shape=jax.ShapeDtypeStruct((N_CHIP, out_rows, cols), WIRE_DTYPE),
        grid=(gi, gj, nk),
        in_specs=[a_spec(tk), b_spec(tk)],
        out_specs=out_spec,
        scratch_shapes=[pltpu.VMEM(acc_shape, F32)],
        compiler_params=_params(("arbitrary", "arbitrary", "arbitrary")),
    )(a, b)


def _row_tile(rows, cols, arrays):
    budget = VMEM_LIMIT // 3
    t = budget // (arrays * 2 * cols * 4)
    t = max(8, min(rows, t // 8 * 8))
    while rows % t:
        t -= 8
    return t


def _sum_partials(name, own, got, myq, l, nl, prev):
    _, rows, cols = own.shape
    tr = _row_tile(rows, cols, 3)
    nt = rows // tr

    def body(q_ref, own_ref, got_ref, *rest):
        o_ref = rest[-1]
        acc = own_ref[...].astype(F32)
        for k in range(3):
            acc = acc + got_ref[k].astype(F32)
        o_ref[...] = acc

    operands = [myq, own, got] + ([] if prev is None else [prev])
    return pl.pallas_call(
        body, name=name, out_shape=jax.ShapeDtypeStruct((nl * rows, cols), F32),
        grid_spec=pltpu.PrefetchScalarGridSpec(
            num_scalar_prefetch=1, grid=(nt,),
            in_specs=[pl.BlockSpec((None, tr, cols), lambda i, q: (q[0], i, 0)),
                      pl.BlockSpec((3, tr, cols), lambda i, q: (0, i, 0))]
            + ([] if prev is None else [pl.BlockSpec(memory_space=pl.ANY)]),
            out_specs=pl.BlockSpec((tr, cols), lambda i, q: (l * nt + i, 0))),
        input_output_aliases={} if prev is None else {3: 0},
        compiler_params=_params(("arbitrary",)),
    )(*operands)


def _adamw(name, w, m, v, g_a, g_b=None):
    rows, cols = w.shape
    tr = _row_tile(rows, cols, 9)
    c1 = 1.0 - ADAM_B1 ** ADAM_STEP
    c2 = 1.0 - ADAM_B2 ** ADAM_STEP

    def body(*refs):
        if g_b is None:
            w_ref, m_ref, v_ref, ga_ref, g_ref, d_ref, m2_ref, v2_ref = refs
            g = ga_ref[...]
        else:
            w_ref, m_ref, v_ref, ga_ref, gb_ref, g_ref, d_ref, m2_ref, v2_ref = refs
            g = ga_ref[...] + gb_ref[...]
        m2 = ADAM_B1 * m_ref[...] + (1.0 - ADAM_B1) * g
        v2 = ADAM_B2 * v_ref[...] + (1.0 - ADAM_B2) * (g * g)
        g_ref[...] = g
        m2_ref[...] = m2
        v2_ref[...] = v2
        d_ref[...] = -ADAM_LR * ((m2 / c1) / (jnp.sqrt(v2 / c2) + ADAM_EPS) + ADAM_WD * w_ref[...])

    tile = pl.BlockSpec((tr, cols), lambda i: (i, 0))
    operands = [w, m, v, g_a] + ([] if g_b is None else [g_b])
    return pl.pallas_call(
        body, name=name, out_shape=[jax.ShapeDtypeStruct((rows, cols), F32)] * 4,
        grid=(rows // tr,), in_specs=[tile] * len(operands), out_specs=[tile] * 4,
        compiler_params=_params(("arbitrary",)),
    )(*operands)


def _pack(parts):
    flat = [p.reshape(-1, LANES) for p in parts]
    for f in flat:
        assert f.shape[0] % 8 == 0
    return jnp.concatenate(flat, axis=0)


def _unpack(packed, shapes):
    out, r = [], 0
    for s in shapes:
        size = 1
        for e in s:
            size *= e
        rows = size // LANES
        out.append(packed[r:r + rows].reshape(s))
        r += rows
    return out


def kernel(x, c, w_ada, b_ada, norm1_g, w_in, a_ln_g, a_ln_b, a_ws, a_bs, w_pa, b_conv_w, b_conv_b, b_ln_g, b_ln_b, w_pb, w_out, norm2_g, w_ff1, w_ff2, final_g, loss_target, m_w_ada, m_b_ada, m_norm1_g, m_w_in, m_a_ln_g, m_a_ln_b, m_a_ws, m_a_bs, m_w_pa, m_b_conv_w, m_b_conv_b, m_b_ln_g, m_b_ln_b, m_w_pb, m_w_out, m_norm2_g, m_w_ff1, m_w_ff2, m_final_g, v_w_ada, v_b_ada, v_norm1_g, v_w_in, v_a_ln_g, v_a_ln_b, v_a_ws, v_a_bs, v_w_pa, v_b_conv_w, v_b_conv_b, v_b_ln_g, v_b_ln_b, v_w_pb, v_w_out, v_norm2_g, v_w_ff1, v_w_ff2, v_final_g):
    nb, t_len, d = x.shape
    nl = w_in.shape[0]
    n = nb * t_len
    cq = w_ada.shape[-1]
    cc = d // N_CHIP
    mx, my, mc = _my_place()
    myq = (2 * mx + my).astype(jnp.int32).reshape(1)

    big = ["w_in", "w_pa", "w_pb", "w_out", "w_ff1", "w_ff2"]
    ws_given = dict(w_in=(w_in, m_w_in, v_w_in), w_pa=(w_pa, m_w_pa, v_w_pa), w_pb=(w_pb, m_w_pb, v_w_pb),
                    w_out=(w_out, m_w_out, v_w_out), w_ff1=(w_ff1, m_w_ff1, v_w_ff1), w_ff2=(w_ff2, m_w_ff2, v_w_ff2))

    def own_slot(w_l):
        empty = lax.empty((N_CHIP,) + w_l.shape, WIRE_DTYPE)
        return lax.dynamic_update_index_in_dim(empty, w_l.astype(WIRE_DTYPE), myq[0], 0)

    token = jnp.zeros((8, LANES), F32)
    gathers = []
    for l in range(nl):
        send_sems, recv_sems, _, lands, token = _split_start(
            f"gather_start_{l}", "gather", [], [own_slot(ws_given[k][0][l]) for k in big], token)
        gathers.append((send_sems, recv_sems, lands))

    def gather_wait(l, part, lo, hi, after):
        send_sems, recv_sems, lands = gathers[l]
        return _split_wait(f"gather_wait_{part}_{l}", "gather", send_sems[lo:hi], recv_sems[lo:hi], [],
                           lands[lo:hi], after)[1]

    c_slots = jnp.broadcast_to(c[None], (N_DEV, nb, d)) + token[0, 0]
    c_all = _all_to_all(c_slots, "gather_c").reshape(N_DEV * nb, d)
    mod_part = _ada_forward(c_all, w_ada, b_ada.reshape(nl, 1, N_CHIP * cq), myq)
    mod_slots = mod_part.reshape(nl, N_DEV, nb, cq).transpose(1, 0, 2, 3).reshape(N_DEV, nl * nb, cq)
    mod_got = _all_to_all(mod_slots, "exchange_mod").reshape(N_CHIP, 2, nl, nb, cq)[:, 0]
    mod6 = mod_got.transpose(1, 2, 0, 3).reshape(nl, nb, 6, d)
    mod = jnp.pad(mod6, ((0, 0), (0, 0), (0, 2), (0, 0)))

    cw_mine = b_conv_w.reshape(nl * CONV_TAPS, cc)
    cwg = _all_to_all(jnp.broadcast_to(cw_mine[None], (N_DEV,) + cw_mine.shape), "gather_conv_w")
    cwg = cwg.reshape(N_CHIP, 2, nl, CONV_TAPS, cc)[:, 0]
    cw = jnp.pad(cwg.transpose(1, 2, 0, 3).reshape(nl, CONV_TAPS, d), ((0, 0), (0, HALO - CONV_TAPS), (0, 0)))

    vec3 = lambda p: p.reshape(nl, 1, d)
    g1, g2 = vec3(norm1_g), vec3(norm2_g)
    lng, lnb, cb, blg, blb = vec3(a_ln_g), vec3(a_ln_b), vec3(b_conv_b), vec3(b_ln_g), vec3(b_ln_b)
    bst = a_bs.transpose(0, 2, 1)

    xs = x.reshape(n, d)
    saved = []
    weights = []
    for l in range(nl):
        (wg_in,) = gather_wait(l, "in", 0, 1, mod if l == 0 else xs)
        h, proj = _in_proj(l, xs, mod, g1, wg_in, t_len)
        ya_in, yb_in, zc = _branches_fwd(l, proj, lng, lnb, a_ws, bst, cw, cb, blg, blb, t_len)
        wg_pa, wg_pb, wg_out = gather_wait(l, "mid", 1, 4, ya_in)
        ya, yb, merged, o, x1 = _merge_out(l, xs, mod, proj, ya_in, yb_in, wg_pa, wg_pb, wg_out, t_len)
        wg_ff1, wg_ff2 = gather_wait(l, "ffn", 4, 6, x1)
        h2, f, o2, x2 = _ffn_fwd(l, x1, mod, g2, wg_ff1, wg_ff2, t_len)
        saved.append((xs, h, proj, ya_in, yb_in, zc, ya, yb, merged, o, x1, h2, f, o2))
        weights.append((wg_in, wg_pa, wg_pb, wg_out, wg_ff1, wg_ff2))
        xs = x2

    loss_blk, dx, dfinal = _loss_head(xs, final_g.reshape(1, d), loss_target.reshape(n, d))
    loss = lax.psum(loss_blk[0, 0], ("x", "y", "c"))

    tok = lambda w: (lambda tk: pl.BlockSpec((tk, w), lambda i, j, k: (k, 0)))
    tok_i = lambda w: (lambda tk: pl.BlockSpec((tk, w), lambda i, j, k: (k, i)))
    tok_j = lambda w: (lambda tk: pl.BlockSpec((tk, w), lambda i, j, k: (k, j)))
    qin = weights[0][0].shape[-1]
    hq = weights[0][4].shape[-1]
    rq = d // N_CHIP
    slot_i = lambda r, cdim: pl.BlockSpec((None, r, cdim), lambda i, j, k: (i, 0, 0))
    slot_j = lambda r, cdim: pl.BlockSpec((None, r, cdim), lambda i, j, k: (j, 0, 0))
    all_slots = pl.BlockSpec((N_CHIP, rq, d), lambda i, j, k: (0, 0, 0))
    scatters = []

    def scatter_start(l, part, names, grads, after):
        lands = [lax.empty((3,) + g.shape[1:], g.dtype) for g in grads]
        send_sems, recv_sems, srcs, lands, tok_out = _split_start(f"scatter_start_{part}_{l}", "scatter", grads, lands,
                                                                  after)
        scatters.append((f"scatter_wait_{part}_{l}", l, names, send_sems, recv_sems, srcs, lands))
        return tok_out

    dmods, small = [None] * nl, [None] * nl
    for l in reversed(range(nl)):
        x0, h, proj, ya_in, yb_in, zc, ya, yb, merged, o, x1, h2, f, o2 = saved[l]
        wg_in, wg_pa, wg_pb, wg_out, wg_ff1, wg_ff2 = weights[l]
        do2, df, dx1, dmod_c, dg2 = _ffn_bwd(l, dx, x1, mod, g2, o2, f, wg_ff1, wg_ff2, t_len, nb)
        g_ff2 = _weight_grad(f"grad_w_ff2_{l}", f, do2, tok_i(hq), tok(d), hq, slot_i(hq, d), (hq, d), (N_CHIP, 1),
                             relu2=True)
        g_ff1 = _weight_grad(f"grad_w_ff1_{l}", h2, df, tok(d), tok_j(hq), d, slot_j(d, hq), (d, hq), (1, N_CHIP))
        token = scatter_start(l, "ffn", ["w_ff2", "w_ff1"], [g_ff2, g_ff1], token)
        mod = mod + token[0, 0]
        do, dya, dyb, dya_in, dyb_in, dproj, dmod_b = _merge_bwd(l, dx1, mod, o, ya, yb, proj, wg_pa, wg_pb, wg_out,
                                                                 t_len, nb)
        g_out = _weight_grad(f"grad_w_out_{l}", merged, do, tok(d), tok(d), rq, all_slots, (d, d), (1, 1))
        g_pa = _weight_grad(f"grad_w_pa_{l}", ya_in, dya, tok(d), tok(d), rq, all_slots, (d, d), (1, 1))
        g_pb = _weight_grad(f"grad_w_pb_{l}", yb_in, dyb, tok(d), tok(d), rq, all_slots, (d, d), (1, 1))
        token = scatter_start(l, "mid", ["w_out", "w_pa", "w_pb"], [g_out, g_pa, g_pb], token)
        lng = lng + token[0, 0]
        dproj, dws, dbst, dcw, vecs = _branches_bwd(l, proj, zc, dya_in, dyb_in, dproj, lng, lnb, a_ws, bst, cw,
                                                    blg, blb, t_len)
        g_in = _weight_grad(f"grad_w_in_{l}", h, dproj, tok(d), tok_j(qin), d, slot_j(d, qin), (d, qin), (1, N_CHIP))
        token = scatter_start(l, "in", ["w_in"], [g_in], token)
        mod = mod + token[0, 0]
        dx, dmod_a, dg1 = _in_proj_bwd(l, dproj, dx1, x0, mod, g1, wg_in, t_len, nb)
        dmods[l] = jnp.concatenate([dmod_a[:, 0:2], dmod_b[:, 2:3], dmod_c[:, 3:6]], axis=1)
        small[l] = (dg1[0], vecs[0], vecs[1], dws, dbst.T, dcw[:CONV_TAPS], vecs[2], vecs[3], vecs[4], dg2[0])
    grad_x = dx.reshape(nb, t_len, d)

    dmod_mine = jnp.stack(dmods).reshape(nl * nb, 6 * d)
    dmod_all = _all_to_all(jnp.broadcast_to(dmod_mine[None], (N_DEV,) + dmod_mine.shape), "gather_dmod")
    dmod_all = dmod_all.reshape(N_DEV, nl, nb, 6 * d).transpose(1, 0, 2, 3).reshape(nl, N_DEV * nb, 6 * d)
    g_w_ada, g_b_ada = _ada_backward(c_all, dmod_all, myq, cq)

    names = ["norm1_g", "a_ln_g", "a_ln_b", "a_ws", "a_bs", "b_conv_w", "b_conv_b", "b_ln_g", "b_ln_b", "norm2_g"]
    stacked = [jnp.stack([small[l][k] for l in range(nl)]) for k in range(len(names))]
    stacked[5] = jnp.pad(stacked[5], ((0, 0), (0, HALO - CONV_TAPS), (0, 0)))
    stacked.append(dfinal)
    part_shapes = [s.shape for s in stacked]
    packed = _pack(stacked)
    prow = packed.shape[0]
    pad_rows = (-prow) % (8 * N_DEV)
    packed = jnp.pad(packed, ((0, pad_rows), (0, 0)))
    srow = packed.shape[0] // N_DEV
    mine = _all_to_all(packed.reshape(N_DEV, srow, LANES), "reduce_small", reduce=True)
    total = _all_to_all(jnp.broadcast_to(mine[None], (N_DEV, srow, LANES)), "gather_small")
    total = total.reshape(N_DEV * srow, LANES)[:prow]
    sg = dict(zip(names + ["final_g"], _unpack(total, part_shapes)))
    sg["b_conv_w"] = lax.dynamic_slice_in_dim(sg["b_conv_w"][:, :CONV_TAPS], myq[0] * cc, cc, axis=2).reshape(
        nl, CONV_TAPS, 1, cc)
    sg["final_g"] = sg["final_g"][0]
    sg["b_ada"] = g_b_ada.reshape(nl, N_CHIP * cq)
    small_names = ["b_ada", "norm1_g", "a_ln_g", "a_ln_b", "a_ws", "a_bs", "b_conv_w", "b_conv_b", "b_ln_g",
                   "b_ln_b", "norm2_g", "final_g"]
    given = dict(b_ada=(b_ada, m_b_ada, v_b_ada), norm1_g=(norm1_g, m_norm1_g, v_norm1_g),
                 a_ln_g=(a_ln_g, m_a_ln_g, v_a_ln_g), a_ln_b=(a_ln_b, m_a_ln_b, v_a_ln_b),
                 a_ws=(a_ws, m_a_ws, v_a_ws), a_bs=(a_bs, m_a_bs, v_a_bs),
                 b_conv_w=(b_conv_w, m_b_conv_w, v_b_conv_w), b_conv_b=(b_conv_b, m_b_conv_b, v_b_conv_b),
                 b_ln_g=(b_ln_g, m_b_ln_g, v_b_ln_g), b_ln_b=(b_ln_b, m_b_ln_b, v_b_ln_b),
                 norm2_g=(norm2_g, m_norm2_g, v_norm2_g), final_g=(final_g, m_final_g, v_final_g))

    def padded(a):
        rows = -(-a.size // (8 * LANES)) * 8
        return jnp.pad(a.reshape(-1), (0, rows * LANES - a.size)).reshape(rows, LANES)

    packs = [_pack([padded(given[k][j]) for k in small_names]) for j in range(3)]
    gpack = _pack([padded(sg[k].astype(F32)) for k in small_names])
    res_small = _adamw("adamw_small", packs[0], packs[1], packs[2], gpack)
    out = {}
    for j, kind in enumerate(["grad", "delta", "new_m", "new_v"]):
        r = 0
        for k in small_names:
            a = given[k][0]
            rows = -(-a.size // (8 * LANES)) * 8
            out[(kind, k)] = res_small[j][r:r + rows].reshape(-1)[:a.size].reshape(a.shape)
            r += rows

    res = _adamw("adamw_w_ada", w_ada.reshape(nl * d, cq), m_w_ada.reshape(nl * d, cq), v_w_ada.reshape(nl * d, cq),
                 g_w_ada.reshape(nl * d, cq))
    for kind, r in zip(["grad", "delta", "new_m", "new_v"], res):
        out[(kind, "w_ada")] = r.reshape(w_ada.shape)

    half = dict.fromkeys(big)
    for name, l, names, send_sems, recv_sems, srcs, lands in scatters:
        srcs, lands = _split_wait(name, "scatter", send_sems, recv_sems, srcs, lands, dx)
        for k, g_own, g_got in zip(names, srcs, lands):
            half[k] = _sum_partials(f"sum_{k}_{l}", g_own, g_got, myq, l, nl, half[k])
    sums = [half[k] for k in big]
    others = _swap_sibling(sums, "swap_sums")
    for k, s_mine, s_other in zip(big, sums, others):
        w, m, v = ws_given[k]
        cols = w.shape[-1]
        res = _adamw(f"adamw_{k}", w.reshape(-1, cols), m.reshape(-1, cols), v.reshape(-1, cols), s_mine, s_other)
        for kind, r in zip(["grad", "delta", "new_m", "new_v"], res):
            out[(kind, k)] = r.reshape(w.shape)

    order = ["w_ada", "b_ada", "norm1_g", "w_in", "a_ln_g", "a_ln_b", "a_ws", "a_bs", "w_pa", "b_conv_w", "b_conv_b",
             "b_ln_g", "b_ln_b", "w_pb", "w_out", "norm2_g", "w_ff1", "w_ff2", "final_g"]
    return (loss, grad_x, *[out[("grad", k)] for k in order], *[out[("delta", k)] for k in order],
            *[out[("new_m", k)] for k in order], *[out[("new_v", k)] for k in order])
```

```python
import functools

import jax
import jax.numpy as jnp
from jax import lax
from jax.experimental import pallas as pl
from jax.experimental.pallas import tpu as pltpu

F32 = jnp.float32
MXU_DTYPE = jnp.bfloat16
ACT_DTYPE = jnp.bfloat16
WIRE_DTYPE = jnp.bfloat16

EPS = 1e-6
CHUNK = 128
HEADS = 8
CONV_TAPS = 31
HALO = 32
N_DEV = 8
N_CHIP = 4
ADAM_LR, ADAM_B1, ADAM_B2, ADAM_EPS, ADAM_WD, ADAM_STEP = 0.001, 0.9, 0.999, 1e-08, 0.01, 10

V7X_VMEM_BYTES = 64 * 1024 * 1024
VMEM_LIMIT = V7X_VMEM_BYTES * 7 // 8
TOKEN_TILE = 512
MATMUL_TILE = 1024
CONV_ROWS = 64
LANES = 128
MESH_ID = pl.DeviceIdType.MESH


def _params(sem=None):
    return pltpu.CompilerParams(dimension_semantics=sem, vmem_limit_bytes=VMEM_LIMIT)


def _dot(a, b):
    return jnp.dot(a.astype(MXU_DTYPE), b.astype(MXU_DTYPE), preferred_element_type=F32)


def _dot_nt(a, b):
    return lax.dot_general(a.astype(MXU_DTYPE), b.astype(MXU_DTYPE), (((1,), (1,)), ((), ())),
                           preferred_element_type=F32)


def _dot_tn(a, b):
    return lax.dot_general(a.astype(MXU_DTYPE), b.astype(MXU_DTYPE), (((0,), (0,)), ((), ())),
                           preferred_element_type=F32)


def _colsum(a):
    return jnp.sum(a, axis=0, keepdims=True)


def _rowmean(a):
    return jnp.mean(a, axis=-1, keepdims=True)


def _sigmoid(a):
    return 1.0 / (1.0 + jnp.exp(-a))


def _modnorm_fwd(x, g, sc, sh):
    r = lax.rsqrt(_rowmean(x * x) + EPS)
    return (x * r) * (g * (1.0 + sc)) + sh


def _modnorm_bwd(x, dh, g, sc):
    r = lax.rsqrt(_rowmean(x * x) + EPS)
    xn = x * r
    dxn = dh * (g * (1.0 + sc))
    dx = r * (dxn - xn * _rowmean(dxn * xn))
    return dx, _colsum(dh), _colsum(dh * xn)


def _ln_stats(v):
    mu = _rowmean(v)
    vc = v - mu
    rstd = lax.rsqrt(_rowmean(vc * vc) + EPS)
    return vc * rstd, rstd


def _ln_bwd(dy, vhat, rstd, g):
    dvh = dy * g
    return rstd * (dvh - _rowmean(dvh) - vhat * _rowmean(dvh * vhat))


def _causal_mask():
    row = lax.broadcasted_iota(jnp.int32, (CHUNK, CHUNK), 0)
    col = lax.broadcasted_iota(jnp.int32, (CHUNK, CHUNK), 1)
    return row >= col


def _shifted(win, width):
    return [win if r == 0 else pltpu.roll(win, width - r, 0) for r in range(8)]


def _my_place():
    return lax.axis_index("x"), lax.axis_index("y"), lax.axis_index("c")


def _all_to_all(x, name, reduce=False):
    n, rows, cols = x.shape
    assert n == N_DEV

    def body(x_ref, o_ref, *scratch):
        if reduce:
            land, send_sems, recv_sems = scratch
        else:
            land = o_ref
            send_sems, recv_sems = scratch
        mx, my, mc = _my_place()
        me = 4 * mx + 2 * my + mc
        land[me] = x_ref[me]
        copies = []
        for k in range(1, N_DEV):
            px = (mx + ((k >> 2) & 1)) % 2
            py = (my + ((k >> 1) & 1)) % 2
            pc = (mc + (k & 1)) % 2
            peer = 4 * px + 2 * py + pc
            cp = pltpu.make_async_remote_copy(
                src_ref=x_ref.at[peer], dst_ref=land.at[me],
                send_sem=send_sems.at[k - 1], recv_sem=recv_sems.at[k - 1],
                device_id=(px, py, pc), device_id_type=MESH_ID)
            cp.start()
            copies.append(cp)
        for cp in copies:
            cp.wait()
        if reduce:
            acc = land[0]
            for s in range(1, N_DEV):
                acc = acc + land[s]
            o_ref[...] = acc

    scratch = [pltpu.SemaphoreType.DMA((N_DEV - 1,)), pltpu.SemaphoreType.DMA((N_DEV - 1,))]
    if reduce:
        scratch = [pltpu.VMEM((N_DEV, rows, cols), x.dtype)] + scratch
        out_shape = jax.ShapeDtypeStruct((rows, cols), x.dtype)
    else:
        out_shape = jax.ShapeDtypeStruct(x.shape, x.dtype)
    return pl.pallas_call(
        body, name=name, out_shape=out_shape,
        in_specs=[pl.BlockSpec(memory_space=pltpu.VMEM)],
        out_specs=pl.BlockSpec(memory_space=pltpu.VMEM),
        scratch_shapes=scratch,
        compiler_params=pltpu.CompilerParams(vmem_limit_bytes=VMEM_LIMIT),
    )(x)


def _other_chips(mx, my):
    return [(1 - mx, my), (mx, 1 - my), (1 - mx, 1 - my)]


HBM_SPEC = pl.BlockSpec(memory_space=pltpu.HBM)
SEM_SPEC = pl.BlockSpec(memory_space=pltpu.SEMAPHORE)
ANY_SPEC = pl.BlockSpec(memory_space=pl.ANY)
SPLIT_EFFECT = pltpu.SideEffectType.DATAFLOW_SIDE_EFFECTING


def _quarter_copies(mode, srcs, lands, send_sems, recv_sems):
    mx, my, mc = _my_place()
    myq = 2 * mx + my
    if mode == "swap":
        return [pltpu.make_async_remote_copy(
            src_ref=srcs[a], dst_ref=lands[a], send_sem=send_sems[a].at[0], recv_sem=recv_sems[a].at[0],
            device_id=(mx, my, 1 - mc), device_id_type=MESH_ID) for a in range(len(lands))]
    copies = []
    for a in range(len(lands)):
        for k, (px, py) in enumerate(_other_chips(mx, my)):
            if mode == "gather":
                src, dst = lands[a].at[myq], lands[a].at[myq]
            else:
                src, dst = srcs[a].at[2 * px + py], lands[a].at[k]
            copies.append(pltpu.make_async_remote_copy(
                src_ref=src, dst_ref=dst, send_sem=send_sems[a].at[k], recv_sem=recv_sems[a].at[k],
                device_id=(px, py, mc), device_id_type=MESH_ID))
    return copies


def _split_start(name, mode, srcs, lands, after):
    ns, n = len(srcs), len(lands)

    def body(*refs):
        outs = refs[ns + n + 1:]
        for cp in _quarter_copies(mode, refs[:ns], refs[ns:ns + n], outs[:n], outs[n:2 * n]):
            cp.start()
        token = outs[-1]
        token[...] = jnp.zeros_like(token)

    arrays = list(srcs) + list(lands)
    per_array = 1 if mode == "swap" else 3
    res = pl.pallas_call(
        body, name=name,
        out_shape=[pltpu.SemaphoreType.DMA((per_array,))] * (2 * n) + [pltpu.HBM(x.shape, x.dtype) for x in arrays]
        + [jax.ShapeDtypeStruct((8, LANES), F32)],
        in_specs=[HBM_SPEC] * (ns + n) + [ANY_SPEC],
        out_specs=[SEM_SPEC] * (2 * n) + [HBM_SPEC] * (ns + n) + [pl.BlockSpec(memory_space=pltpu.VMEM)],
        input_output_aliases={i: 2 * n + i for i in range(ns + n)},
        compiler_params=pltpu.CompilerParams(has_side_effects=SPLIT_EFFECT),
    )(*[pltpu.with_memory_space_constraint(x, pltpu.HBM) for x in arrays], after)
    return res[:n], res[n:2 * n], res[2 * n:2 * n + ns], res[2 * n + ns:2 * n + ns + n], res[-1]


def _split_wait(name, mode, send_sems, recv_sems, srcs, lands, after):
    ns, n = len(srcs), len(lands)

    def body(*refs):
        sems = refs[ns + n:ns + 3 * n]
        for cp in _quarter_copies(mode, refs[:ns], refs[ns:ns + n], sems[:n], sems[n:]):
            cp.wait_send()
            cp.wait_recv()

    arrays = list(srcs) + list(lands)
    res = pl.pallas_call(
        body, name=name,
        out_shape=[pltpu.HBM(x.shape, x.dtype) for x in arrays],
        in_specs=[HBM_SPEC] * (ns + n) + [SEM_SPEC] * (2 * n) + [ANY_SPEC],
        out_specs=[HBM_SPEC] * (ns + n),
        input_output_aliases={i: i for i in range(ns + n)},
        compiler_params=pltpu.CompilerParams(has_side_effects=SPLIT_EFFECT),
    )(*arrays, *send_sems, *recv_sems, after)
    return res[:ns], res[ns:]


def _ada_forward(c_all, w_ada, b_ada3, myq):
    nl, d, cq = w_ada.shape
    nb = c_all.shape[0]

    def body(q_ref, c_ref, w_ref, b_ref, o_ref):
        c = c_ref[...]
        act = c * _sigmoid(c)
        o_ref[...] = _dot(act, w_ref[...]) + b_ref[...]

    return pl.pallas_call(
        body, name="ada_forward",
        out_shape=jax.ShapeDtypeStruct((nl, nb, cq), F32),
        grid_spec=pltpu.PrefetchScalarGridSpec(
            num_scalar_prefetch=1, grid=(nl,),
            in_specs=[pl.BlockSpec((nb, d), lambda l, q: (0, 0)),
                      pl.BlockSpec((None, d, cq), lambda l, q: (l, 0, 0)),
                      pl.BlockSpec((None, 1, cq), lambda l, q: (l, 0, q[0]))],
            out_specs=pl.BlockSpec((None, nb, cq), lambda l, q: (l, 0, 0))),
        compiler_params=_params(("arbitrary",)),
    )(myq, c_all, w_ada, b_ada3)


def _ada_backward(c_all, dmod_all, myq, cq):
    nb, d = c_all.shape
    nl = dmod_all.shape[0]
    full = dmod_all.shape[2]

    def body(q_ref, c_ref, dq_ref, dall_ref, gw_ref, gb_ref):
        c = c_ref[...]
        act = c * _sigmoid(c)
        gw_ref[...] = _dot_tn(act, dq_ref[...])
        gb_ref[...] = _colsum(dall_ref[...])

    return pl.pallas_call(
        body, name="ada_backward",
        out_shape=[jax.ShapeDtypeStruct((nl, d, cq), F32), jax.ShapeDtypeStruct((nl, 1, full), F32)],
        grid_spec=pltpu.PrefetchScalarGridSpec(
            num_scalar_prefetch=1, grid=(nl,),
            in_specs=[pl.BlockSpec((nb, d), lambda l, q: (0, 0)),
                      pl.BlockSpec((None, nb, cq), lambda l, q: (l, 0, q[0])),
                      pl.BlockSpec((None, nb, full), lambda l, q: (l, 0, 0))],
            out_specs=[pl.BlockSpec((None, d, cq), lambda l, q: (l, 0, 0)),
                       pl.BlockSpec((None, 1, full), lambda l, q: (l, 0, 0))]),
        compiler_params=_params(("arbitrary",)),
    )(myq, c_all, dmod_all, dmod_all)


def _in_proj(l, x, mod, g1, wg_in, t_len):
    n, d = x.shape
    tm = min(MATMUL_TILE, t_len)
    tpb = t_len // tm
    qc = wg_in.shape[-1]

    def body(x_ref, mod_ref, g_ref, w_ref, h_ref, proj_ref, h_s):
        @pl.when(pl.program_id(1) == 0)
        def _():
            h = _modnorm_fwd(x_ref[...], g_ref[...], mod_ref[1:2, :], mod_ref[0:1, :])
            h_s[...] = h.astype(MXU_DTYPE)
            h_ref[...] = h.astype(ACT_DTYPE)
        proj_ref[...] = jnp.dot(h_s[...], w_ref[...], preferred_element_type=F32).astype(ACT_DTYPE)

    return pl.pallas_call(
        body, name=f"in_proj_{l}",
        out_shape=[jax.ShapeDtypeStruct((n, d), ACT_DTYPE), jax.ShapeDtypeStruct((n, N_CHIP * qc), ACT_DTYPE)],
        grid=(n // tm, N_CHIP),
        in_specs=[pl.BlockSpec((tm, d), lambda i, j: (i, 0)),
                  pl.BlockSpec((None, None, 8, d), lambda i, j: (l, i // tpb, 0, 0)),
                  pl.BlockSpec((None, 1, d), lambda i, j: (l, 0, 0)),
                  pl.BlockSpec((None, d, qc), lambda i, j: (j, 0, 0))],
        out_specs=[pl.BlockSpec((tm, d), lambda i, j: (i, 0)),
                   pl.BlockSpec((tm, qc), lambda i, j: (i, j))],
        scratch_shapes=[pltpu.VMEM((tm, d), MXU_DTYPE)],
        compiler_params=_params(("arbitrary", "arbitrary")),
    )(x, mod, g1, wg_in)


def _masked_ws(ws_ref, wm_s):
    mask = _causal_mask()
    for h in range(HEADS):
        wm_s[h] = jnp.where(mask, ws_ref[h], 0.0).astype(MXU_DTYPE)


def _fill_z(i, tpb, a_ref, g_ref, ah_ref, gh_ref, zext):
    ah = ah_ref[...].astype(F32)
    gh = gh_ref[...].astype(F32)
    keep = jnp.where(i % tpb == 0, 0.0, 1.0)
    zext[0:HALO, :] = ah * _sigmoid(gh) * keep


def _conv_taps(src, w_ref, r0, lanes, flip):
    width = CONV_ROWS + HALO
    win = src[pl.ds(r0, width), lanes]
    rot = _shifted(win, width)
    acc = jnp.zeros((CONV_ROWS, LANES), F32)
    for s in range(0, HALO + 1):
        k = (CONV_TAPS - 1 - s) if flip else (s - 2)
        if k < 0 or k >= CONV_TAPS:
            continue
        q, r = divmod(s, 8)
        acc = acc + rot[r][8 * q:8 * q + CONV_ROWS] * w_ref[k:k + 1, lanes]
    return acc


def _branches_fwd(l, proj, lng, lnb, ws, bst, cw, cb, blg, blb, t_len):
    n = proj.shape[0]
    d = lng.shape[-1]
    tm = min(TOKEN_TILE, t_len)
    tpb = t_len // tm
    per = tm // HALO
    nchunk = tm // CHUNK

    def body(u_ref, v_ref, a_ref, g_ref, ah_ref, gh_ref, lng_ref, lnb_ref, ws_ref, bst_ref, cw_ref, cb_ref,
             blg_ref, blb_ref, ya_ref, yb_ref, zc_ref, wm_s, zext):
        i = pl.program_id(0)
        _masked_ws(ws_ref, wm_s)
        _fill_z(i, tpb, a_ref, g_ref, ah_ref, gh_ref, zext)

        def chunk(c, carry):
            r0 = pl.multiple_of(c * CHUNK, CHUNK)
            rows = pl.ds(r0, CHUNK)
            vhat, _ = _ln_stats(v_ref[rows, :].astype(F32))
            vn = (vhat * lng_ref[...] + lnb_ref[...]).astype(MXU_DTYPE)
            u = u_ref[rows, :].astype(F32)
            for h in range(HEADS):
                cols = slice(h * CHUNK, (h + 1) * CHUNK)
                s = jnp.dot(wm_s[h], vn[:, cols], preferred_element_type=F32) + bst_ref[:, h:h + 1]
                ya_ref[rows, cols] = (u[:, cols] * s).astype(ACT_DTYPE)
            a = a_ref[rows, :].astype(F32)
            g = g_ref[rows, :].astype(F32)
            zext[pl.ds(HALO + r0, CHUNK), :] = a * _sigmoid(g)
            return carry

        lax.fori_loop(0, nchunk, chunk, 0)

        for lc in range(d // LANES):
            lanes = slice(lc * LANES, (lc + 1) * LANES)

            def block(rb, carry):
                r0 = pl.multiple_of(rb * CONV_ROWS, CONV_ROWS)
                acc = _conv_taps(zext, cw_ref, r0, lanes, flip=False) + cb_ref[:, lanes]
                zc_ref[pl.ds(r0, CONV_ROWS), lanes] = acc.astype(ACT_DTYPE)
                return carry

            lax.fori_loop(0, tm // CONV_ROWS, block, 0)

        def chunk2(c, carry):
            r0 = pl.multiple_of(c * CHUNK, CHUNK)
            rows = pl.ds(r0, CHUNK)
            zhat, _ = _ln_stats(zc_ref[rows, :].astype(F32))
            zn = zhat * blg_ref[...] + blb_ref[...]
            yb_ref[rows, :] = (zn * _sigmoid(zn)).astype(ACT_DTYPE)
            return carry

        lax.fori_loop(0, nchunk, chunk2, 0)

    col = lambda k: pl.BlockSpec((tm, d), lambda i: (i, k))
    halo = lambda k: pl.BlockSpec((HALO, d), lambda i: (jnp.maximum(i * per - 1, 0), k))
    vec = pl.BlockSpec((None, 1, d), lambda i: (l, 0, 0))
    out = pl.BlockSpec((tm, d), lambda i: (i, 0))
    return pl.pallas_call(
        body, name=f"branches_fwd_{l}",
        out_shape=[jax.ShapeDtypeStruct((n, d), ACT_DTYPE)] * 3,
        grid=(n // tm,),
        in_specs=[col(0), col(1), col(2), col(3), halo(2), halo(3), vec, vec,
                  pl.BlockSpec((None, HEADS, CHUNK, CHUNK), lambda i: (l, 0, 0, 0)),
                  pl.BlockSpec((None, CHUNK, HEADS), lambda i: (l, 0, 0)),
                  pl.BlockSpec((None, HALO, d), lambda i: (l, 0, 0)), vec, vec, vec],
        out_specs=[out, out, out],
        scratch_shapes=[pltpu.VMEM((HEADS, CHUNK, CHUNK), MXU_DTYPE), pltpu.VMEM((HALO + tm, d), F32)],
        compiler_params=_params(("arbitrary",)),
    )(proj, proj, proj, proj, proj, proj, lng, lnb, ws, bst, cw, cb, blg, blb)


def _merge_out(l, x, mod, proj, ya_in, yb_in, wg_pa, wg_pb, wg_out, t_len):
    n, d = x.shape
    tm = min(TOKEN_TILE, t_len)
    tpb = t_len // tm
    rq = d // N_CHIP

    def body(x_ref, mod_ref, ga_ref, gb_ref, yai_ref, ybi_ref, wpa_ref, wpb_ref, wo_ref,
             ya_ref, yb_ref, mg_ref, o_ref, x1_ref):
        wpa = wpa_ref[...].reshape(d, d)
        wpb = wpb_ref[...].reshape(d, d)
        wo = wo_ref[...].reshape(d, d)
        ya = jnp.dot(yai_ref[...].astype(MXU_DTYPE), wpa, preferred_element_type=F32)
        yb = jnp.dot(ybi_ref[...].astype(MXU_DTYPE), wpb, preferred_element_type=F32)
        merged = _sigmoid(ga_ref[...].astype(F32)) * ya + _sigmoid(gb_ref[...].astype(F32)) * yb
        o = _dot(merged, wo)
        ya_ref[...] = ya.astype(ACT_DTYPE)
        yb_ref[...] = yb.astype(ACT_DTYPE)
        mg_ref[...] = merged.astype(ACT_DTYPE)
        o_ref[...] = o.astype(ACT_DTYPE)
        x1_ref[...] = x_ref[...] + mod_ref[2:3, :] * o

    tile = pl.BlockSpec((tm, d), lambda i: (i, 0))
    wspec = pl.BlockSpec((N_CHIP, rq, d), lambda i: (0, 0, 0))
    return pl.pallas_call(
        body, name=f"merge_out_{l}",
        out_shape=[jax.ShapeDtypeStruct((n, d), ACT_DTYPE)] * 4 + [jax.ShapeDtypeStruct((n, d), F32)],
        grid=(n // tm,),
        in_specs=[tile, pl.BlockSpec((None, None, 8, d), lambda i: (l, i // tpb, 0, 0)),
                  pl.BlockSpec((tm, d), lambda i: (i, 4)), pl.BlockSpec((tm, d), lambda i: (i, 5)),
                  tile, tile, wspec, wspec, wspec],
        out_specs=[tile] * 5,
        compiler_params=_params(("arbitrary",)),
    )(x, mod, proj, proj, ya_in, yb_in, wg_pa, wg_pb, wg_out)


def _ffn_fwd(l, x1, mod, g2, wg_ff1, wg_ff2, t_len):
    n, d = x1.shape
    tm = min(TOKEN_TILE, t_len)
    tpb = t_len // tm
    hq = wg_ff1.shape[-1]

    def body(x_ref, mod_ref, g_ref, w1_ref, w2_ref, h_ref, f_ref, o2_ref, x2_ref, h_s, acc):
        j = pl.program_id(1)

        @pl.when(j == 0)
        def _():
            h = _modnorm_fwd(x_ref[...], g_ref[...], mod_ref[4:5, :], mod_ref[3:4, :])
            h_s[...] = h.astype(MXU_DTYPE)
            h_ref[...] = h.astype(ACT_DTYPE)
            acc[...] = jnp.zeros_like(acc)

        f = jnp.dot(h_s[...], w1_ref[...], preferred_element_type=F32)
        f_ref[...] = f.astype(ACT_DTYPE)
        acc[...] += _dot(jnp.square(jnp.maximum(f, 0.0)), w2_ref[...])

        @pl.when(j == N_CHIP - 1)
        def _():
            o2 = acc[...]
            o2_ref[...] = o2.astype(ACT_DTYPE)
            x2_ref[...] = x_ref[...] + mod_ref[5:6, :] * o2

    tile = pl.BlockSpec((tm, d), lambda i, j: (i, 0))
    return pl.pallas_call(
        body, name=f"ffn_fwd_{l}",
        out_shape=[jax.ShapeDtypeStruct((n, d), ACT_DTYPE), jax.ShapeDtypeStruct((n, N_CHIP * hq), ACT_DTYPE),
                   jax.ShapeDtypeStruct((n, d), ACT_DTYPE), jax.ShapeDtypeStruct((n, d), F32)],
        grid=(n // tm, N_CHIP),
        in_specs=[tile, pl.BlockSpec((None, None, 8, d), lambda i, j: (l, i // tpb, 0, 0)),
                  pl.BlockSpec((None, 1, d), lambda i, j: (l, 0, 0)),
                  pl.BlockSpec((None, d, hq), lambda i, j: (j, 0, 0)),
                  pl.BlockSpec((None, hq, d), lambda i, j: (j, 0, 0))],
        out_specs=[tile, pl.BlockSpec((tm, hq), lambda i, j: (i, j)), tile, tile],
        scratch_shapes=[pltpu.VMEM((tm, d), MXU_DTYPE), pltpu.VMEM((tm, d), F32)],
        compiler_params=_params(("arbitrary", "arbitrary")),
    )(x1, mod, g2, wg_ff1, wg_ff2)


def _loss_head(x, final_g, target):
    n, d = x.shape
    tm = min(TOKEN_TILE, n)

    def body(x_ref, g_ref, t_ref, loss_ref, dx_ref, dg_ref):
        @pl.when(pl.program_id(0) == 0)
        def _():
            loss_ref[...] = jnp.zeros_like(loss_ref)
            dg_ref[...] = jnp.zeros_like(dg_ref)

        x_t = x_ref[...]
        g = g_ref[...]
        r = lax.rsqrt(_rowmean(x_t * x_t) + EPS)
        xn = x_t * r
        e = xn * g - t_ref[...]
        loss_ref[...] += jnp.sum(e * e) * (0.5 / d)
        dy = e * (1.0 / d)
        dxn = dy * g
        dx_ref[...] = r * (dxn - xn * _rowmean(dxn * xn))
        dg_ref[0:1, :] += _colsum(dy * xn)

    tile = pl.BlockSpec((tm, d), lambda i: (i, 0))
    return pl.pallas_call(
        body, name="loss_head",
        out_shape=[jax.ShapeDtypeStruct((8, LANES), F32), jax.ShapeDtypeStruct((n, d), F32),
                   jax.ShapeDtypeStruct((8, d), F32)],
        grid=(n // tm,),
        in_specs=[tile, pl.BlockSpec((1, d), lambda i: (0, 0)), tile],
        out_specs=[pl.BlockSpec((8, LANES), lambda i: (0, 0)), tile, pl.BlockSpec((8, d), lambda i: (0, 0))],
        compiler_params=_params(("arbitrary",)),
    )(x, final_g, target)


def _norm_tail(i, tpb, x_ref, dxin_ref, dh, g_ref, sc, dx_ref, dmod_ref, dg_ref, row_sh, row_sc):
    dxm, dsh, q = _modnorm_bwd(x_ref[...], dh, g_ref[...], sc)
    dx_ref[...] = dxin_ref[...] + dxm
    dmod_ref[row_sh:row_sh + 1, :] += dsh
    dmod_ref[row_sc:row_sc + 1, :] += g_ref[...] * q
    dg_ref[0:1, :] += (1.0 + sc) * q


def _ffn_bwd(l, dx2, x1, mod, g2, o2, f, wg_ff1, wg_ff2, t_len, nb):
    n, d = dx2.shape
    tm = min(TOKEN_TILE, t_len)
    tpb = t_len // tm
    hq = wg_ff1.shape[-1]

    def body(dx2_ref, x1_ref, mod_ref, g_ref, o2_ref, f_ref, w1_ref, w2_ref,
             do2_ref, df_ref, dx1_ref, dmod_ref, dg_ref, do_s, acc):
        i, j = pl.program_id(0), pl.program_id(1)

        @pl.when((i == 0) & (j == 0))
        def _():
            dg_ref[...] = jnp.zeros_like(dg_ref)

        @pl.when((i % tpb == 0) & (j == 0))
        def _():
            dmod_ref[...] = jnp.zeros_like(dmod_ref)

        @pl.when(j == 0)
        def _():
            dx2_t = dx2_ref[...]
            dmod_ref[5:6, :] += _colsum(dx2_t * o2_ref[...].astype(F32))
            do2 = dx2_t * mod_ref[5:6, :]
            do_s[...] = do2.astype(MXU_DTYPE)
            do2_ref[...] = do2.astype(ACT_DTYPE)
            acc[...] = jnp.zeros_like(acc)

        da2 = _dot_nt(do_s[...], w2_ref[...])
        df = da2 * (2.0 * jnp.maximum(f_ref[...].astype(F32), 0.0))
        df_ref[...] = df.astype(ACT_DTYPE)
        acc[...] += _dot_nt(df, w1_ref[...])

        @pl.when(j == N_CHIP - 1)
        def _():
            _norm_tail(i, tpb, x1_ref, dx2_ref, acc[...], g_ref, mod_ref[4:5, :], dx1_ref, dmod_ref, dg_ref, 3, 4)

    tile = pl.BlockSpec((tm, d), lambda i, j: (i, 0))
    wide = pl.BlockSpec((tm, hq), lambda i, j: (i, j))
    return pl.pallas_call(
        body, name=f"ffn_bwd_{l}",
        out_shape=[jax.ShapeDtypeStruct((n, d), ACT_DTYPE), jax.ShapeDtypeStruct((n, N_CHIP * hq), ACT_DTYPE),
                   jax.ShapeDtypeStruct((n, d), F32), jax.ShapeDtypeStruct((nb, 8, d), F32),
                   jax.ShapeDtypeStruct((8, d), F32)],
        grid=(n // tm, N_CHIP),
        in_specs=[tile, tile, pl.BlockSpec((None, None, 8, d), lambda i, j: (l, i // tpb, 0, 0)),
                  pl.BlockSpec((None, 1, d), lambda i, j: (l, 0, 0)), tile, wide,
                  pl.BlockSpec((None, d, hq), lambda i, j: (j, 0, 0)),
                  pl.BlockSpec((None, hq, d), lambda i, j: (j, 0, 0))],
        out_specs=[tile, wide, tile, pl.BlockSpec((None, 8, d), lambda i, j: (i // tpb, 0, 0)),
                   pl.BlockSpec((8, d), lambda i, j: (0, 0))],
        scratch_shapes=[pltpu.VMEM((tm, d), MXU_DTYPE), pltpu.VMEM((tm, d), F32)],
        compiler_params=_params(("arbitrary", "arbitrary")),
    )(dx2, x1, mod, g2, o2, f, wg_ff1, wg_ff2)


def _merge_bwd(l, dx1, mod, o, ya, yb, proj, wg_pa, wg_pb, wg_out, t_len, nb):
    n, d = dx1.shape
    tm = min(TOKEN_TILE, t_len)
    tpb = t_len // tm
    rq = d // N_CHIP

    def body(dx_ref, mod_ref, o_ref, ya_ref, yb_ref, ga_ref, gb_ref, wpa_ref, wpb_ref, wo_ref,
             do_ref, dya_ref, dyb_ref, dyai_ref, dybi_ref, dproj_ref, dmod_ref):
        i = pl.program_id(0)

        @pl.when(i % tpb == 0)
        def _():
            dmod_ref[...] = jnp.zeros_like(dmod_ref)

        dx = dx_ref[...]
        dmod_ref[2:3, :] += _colsum(dx * o_ref[...].astype(F32))
        do = (dx * mod_ref[2:3, :]).astype(MXU_DTYPE)
        do_ref[...] = do.astype(ACT_DTYPE)
        dm = _dot_nt(do, wo_ref[...].reshape(d, d))
        sa = _sigmoid(ga_ref[...].astype(F32))
        sb = _sigmoid(gb_ref[...].astype(F32))
        dya = (dm * sa).astype(MXU_DTYPE)
        dyb = (dm * sb).astype(MXU_DTYPE)
        dya_ref[...] = dya.astype(ACT_DTYPE)
        dyb_ref[...] = dyb.astype(ACT_DTYPE)
        dproj_ref[:, 0:d] = (dm * ya_ref[...].astype(F32) * sa * (1.0 - sa)).astype(ACT_DTYPE)
        dproj_ref[:, d:2 * d] = (dm * yb_ref[...].astype(F32) * sb * (1.0 - sb)).astype(ACT_DTYPE)
        dyai_ref[...] = _dot_nt(dya, wpa_ref[...].reshape(d, d)).astype(ACT_DTYPE)
        dybi_ref[...] = _dot_nt(dyb, wpb_ref[...].reshape(d, d)).astype(ACT_DTYPE)

    tile = pl.BlockSpec((tm, d), lambda i: (i, 0))
    wspec = pl.BlockSpec((N_CHIP, rq, d), lambda i: (0, 0, 0))
    return pl.pallas_call(
        body, name=f"merge_bwd_{l}",
        out_shape=[jax.ShapeDtypeStruct((n, d), ACT_DTYPE)] * 5
        + [jax.ShapeDtypeStruct((n, 6 * d), ACT_DTYPE), jax.ShapeDtypeStruct((nb, 8, d), F32)],
        grid=(n // tm,),
        in_specs=[tile, pl.BlockSpec((None, None, 8, d), lambda i: (l, i // tpb, 0, 0)), tile, tile, tile,
                  pl.BlockSpec((tm, d), lambda i: (i, 4)), pl.BlockSpec((tm, d), lambda i: (i, 5)),
                  wspec, wspec, wspec],
        out_specs=[tile] * 5 + [pl.BlockSpec((tm, 2 * d), lambda i: (i, 2)),
                                pl.BlockSpec((None, 8, d), lambda i: (i // tpb, 0, 0))],
        compiler_params=_params(("arbitrary",)),
    )(dx1, mod, o, ya, yb, proj, proj, wg_pa, wg_pb, wg_out)


def _branches_bwd(l, proj, zc, dya_in, dyb_in, dproj, lng, lnb, ws, bst, cw, blg, blb, t_len):
    n = proj.shape[0]
    d = lng.shape[-1]
    tm = min(TOKEN_TILE, t_len)
    tpb = t_len // tm
    per = tm // HALO
    nchunk = tm // CHUNK
    ntile = n // tm
    nblk = tm // CONV_ROWS

    def body(u_ref, v_ref, a_ref, g_ref, ah_ref, gh_ref, zc_ref, zcn_ref, dya_ref, dyb_ref, dybn_ref, dproj_in,
             lng_ref, lnb_ref, ws_ref, bst_ref, cw_ref, blg_ref, blb_ref,
             dproj_ref, dws_ref, dbst_ref, dcw_ref, vec_ref, wm_s, zext, dzext, dvn_s):
        i = pl.program_id(0)

        @pl.when(i == 0)
        def _():
            dws_ref[...] = jnp.zeros_like(dws_ref)
            dbst_ref[...] = jnp.zeros_like(dbst_ref)
            dcw_ref[...] = jnp.zeros_like(dcw_ref)
            vec_ref[...] = jnp.zeros_like(vec_ref)

        _masked_ws(ws_ref, wm_s)
        _fill_z(i, tpb, a_ref, g_ref, ah_ref, gh_ref, zext)

        def conv_ln_bwd(zc_t, dyb_t):
            zhat, rstd = _ln_stats(zc_t)
            zn = zhat * blg_ref[...] + blb_ref[...]
            sg = _sigmoid(zn)
            dzn = dyb_t * (sg * (1.0 + zn * (1.0 - sg)))
            return _ln_bwd(dzn, zhat, rstd, blg_ref[...]), _colsum(dzn * zhat), _colsum(dzn)

        def chunk(c, carry):
            r0 = pl.multiple_of(c * CHUNK, CHUNK)
            rows = pl.ds(r0, CHUNK)
            vhat, rstd = _ln_stats(v_ref[rows, :].astype(F32))
            vn = (vhat * lng_ref[...] + lnb_ref[...]).astype(MXU_DTYPE)
            u = u_ref[rows, :].astype(F32)
            dya = dya_ref[rows, :].astype(F32)
            for h in range(HEADS):
                cols = slice(h * CHUNK, (h + 1) * CHUNK)
                s = jnp.dot(wm_s[h], vn[:, cols], preferred_element_type=F32) + bst_ref[:, h:h + 1]
                dproj_ref[rows, cols] = (dya[:, cols] * s).astype(ACT_DTYPE)
                ds = dya[:, cols] * u[:, cols]
                dvn_s[:, cols] = _dot_tn(wm_s[h], ds)
                dws_ref[h] += _dot_nt(ds, vn[:, cols])
                dbst_ref[:, h:h + 1] += jnp.sum(ds, axis=1, keepdims=True)
            dvn = dvn_s[...]
            dproj_ref[rows, d:2 * d] = _ln_bwd(dvn, vhat, rstd, lng_ref[...]).astype(ACT_DTYPE)
            vec_ref[0:1, :] += _colsum(dvn * vhat)
            vec_ref[1:2, :] += _colsum(dvn)
            a = a_ref[rows, :].astype(F32)
            g = g_ref[rows, :].astype(F32)
            zext[pl.ds(HALO + r0, CHUNK), :] = a * _sigmoid(g)
            dzc, dblg, dblb = conv_ln_bwd(zc_ref[rows, :].astype(F32), dyb_ref[rows, :].astype(F32))
            dzext[rows, :] = dzc
            vec_ref[2:3, :] += _colsum(dzc)
            vec_ref[3:4, :] += dblg
            vec_ref[4:5, :] += dblb
            return carry

        lax.fori_loop(0, nchunk, chunk, 0)

        dzc_next, _, _ = conv_ln_bwd(zcn_ref[...].astype(F32), dybn_ref[...].astype(F32))
        dzext[tm:tm + HALO, :] = dzc_next * jnp.where(i % tpb == tpb - 1, 0.0, 1.0)

        for lc in range(d // LANES):
            lanes = slice(lc * LANES, (lc + 1) * LANES)

            def block(rb, carry):
                r0 = pl.multiple_of(rb * CONV_ROWS, CONV_ROWS)
                rows = pl.ds(r0, CONV_ROWS)
                dz = _conv_taps(dzext, cw_ref, r0, lanes, flip=True)
                a = a_ref[rows, lanes].astype(F32)
                sg = _sigmoid(g_ref[rows, lanes].astype(F32))
                dproj_ref[rows, 2 * d + lc * LANES:2 * d + (lc + 1) * LANES] = (dz * sg).astype(ACT_DTYPE)
                dproj_ref[rows, 3 * d + lc * LANES:3 * d + (lc + 1) * LANES] = (
                    dz * a * sg * (1.0 - sg)).astype(ACT_DTYPE)
                return carry

            lax.fori_loop(0, nblk, block, 0)

            def wblock(rb, accs):
                r0 = pl.multiple_of(rb * CONV_ROWS, CONV_ROWS)
                width = CONV_ROWS + HALO
                rot = _shifted(zext[pl.ds(r0, width), lanes], width)
                dzc = dzext[pl.ds(r0, CONV_ROWS), lanes]
                out = []
                for k in range(CONV_TAPS):
                    q, r = divmod(k + 2, 8)
                    prod = dzc * rot[r][8 * q:8 * q + CONV_ROWS]
                    part = prod[0:8]
                    for e in range(1, CONV_ROWS // 8):
                        part = part + prod[8 * e:8 * e + 8]
                    out.append(accs[k] + part)
                return tuple(out)

            accs = lax.fori_loop(0, nblk, wblock, tuple(jnp.zeros((8, LANES), F32) for _ in range(CONV_TAPS)))
            for k in range(CONV_TAPS):
                dcw_ref[k:k + 1, lanes] += _colsum(accs[k])

        @pl.when(i == ntile - 1)
        def _():
            mask = _causal_mask()
            for h in range(HEADS):
                dws_ref[h] = jnp.where(mask, dws_ref[h], 0.0)

    col = lambda k: pl.BlockSpec((tm, d), lambda i: (i, k))
    tile = pl.BlockSpec((tm, d), lambda i: (i, 0))
    before = lambda k: pl.BlockSpec((HALO, d), lambda i: (jnp.maximum(i * per - 1, 0), k))
    after = pl.BlockSpec((HALO, d), lambda i: (jnp.minimum((i + 1) * per, n // HALO - 1), 0))
    vec = pl.BlockSpec((None, 1, d), lambda i: (l, 0, 0))
    const2 = lambda r, c: pl.BlockSpec((r, c), lambda i: (0, 0))
    return pl.pallas_call(
        body, name=f"branches_bwd_{l}",
        out_shape=[jax.ShapeDtypeStruct((n, 6 * d), ACT_DTYPE), jax.ShapeDtypeStruct((HEADS, CHUNK, CHUNK), F32),
                   jax.ShapeDtypeStruct((CHUNK, HEADS), F32), jax.ShapeDtypeStruct((HALO, d), F32),
                   jax.ShapeDtypeStruct((8, d), F32)],
        grid=(ntile,),
        in_specs=[col(0), col(1), col(2), col(3), before(2), before(3), tile, after, tile, tile, after,
                  pl.BlockSpec(memory_space=pl.ANY), vec, vec,
                  pl.BlockSpec((None, HEADS, CHUNK, CHUNK), lambda i: (l, 0, 0, 0)),
                  pl.BlockSpec((None, CHUNK, HEADS), lambda i: (l, 0, 0)),
                  pl.BlockSpec((None, HALO, d), lambda i: (l, 0, 0)), vec, vec],
        out_specs=[pl.BlockSpec((tm, 4 * d), lambda i: (i, 0)),
                   pl.BlockSpec((HEADS, CHUNK, CHUNK), lambda i: (0, 0, 0)),
                   const2(CHUNK, HEADS), const2(HALO, d), const2(8, d)],
        scratch_shapes=[pltpu.VMEM((HEADS, CHUNK, CHUNK), MXU_DTYPE), pltpu.VMEM((HALO + tm, d), F32),
                        pltpu.VMEM((tm + HALO, d), F32), pltpu.VMEM((CHUNK, d), F32)],
        input_output_aliases={11: 0},
        compiler_params=_params(("arbitrary",)),
    )(proj, proj, proj, proj, proj, proj, zc, zc, dya_in, dyb_in, dyb_in, dproj, lng, lnb, ws, bst, cw, blg, blb)


def _in_proj_bwd(l, dproj, dx1, x, mod, g1, wg_in, t_len, nb):
    n, d = x.shape
    tm = min(MATMUL_TILE, t_len)
    tpb = t_len // tm
    qc = wg_in.shape[-1]

    def body(dp_ref, dx1_ref, x_ref, mod_ref, g_ref, w_ref, dx_ref, dmod_ref, dg_ref, acc):
        i, j = pl.program_id(0), pl.program_id(1)

        @pl.when((i == 0) & (j == 0))
        def _():
            dg_ref[...] = jnp.zeros_like(dg_ref)

        @pl.when((i % tpb == 0) & (j == 0))
        def _():
            dmod_ref[...] = jnp.zeros_like(dmod_ref)

        @pl.when(j == 0)
        def _():
            acc[...] = jnp.zeros_like(acc)

        acc[...] += _dot_nt(dp_ref[...], w_ref[...])

        @pl.when(j == N_CHIP - 1)
        def _():
            _norm_tail(i, tpb, x_ref, dx1_ref, acc[...], g_ref, mod_ref[1:2, :], dx_ref, dmod_ref, dg_ref, 0, 1)

    tile = pl.BlockSpec((tm, d), lambda i, j: (i, 0))
    return pl.pallas_call(
        body, name=f"in_proj_bwd_{l}",
        out_shape=[jax.ShapeDtypeStruct((n, d), F32), jax.ShapeDtypeStruct((nb, 8, d), F32),
                   jax.ShapeDtypeStruct((8, d), F32)],
        grid=(n // tm, N_CHIP),
        in_specs=[pl.BlockSpec((tm, qc), lambda i, j: (i, j)), tile, tile,
                  pl.BlockSpec((None, None, 8, d), lambda i, j: (l, i // tpb, 0, 0)),
                  pl.BlockSpec((None, 1, d), lambda i, j: (l, 0, 0)),
                  pl.BlockSpec((None, d, qc), lambda i, j: (j, 0, 0))],
        out_specs=[tile, pl.BlockSpec((None, 8, d), lambda i, j: (i // tpb, 0, 0)),
                   pl.BlockSpec((8, d), lambda i, j: (0, 0))],
        scratch_shapes=[pltpu.VMEM((tm, d), F32)],
        compiler_params=_params(("arbitrary", "arbitrary")),
    )(dproj, dx1, x, mod, g1, wg_in)


def _weight_grad(name, a, b, a_spec, b_spec, out_rows, out_spec, acc_shape, grid_ij, relu2=False):
    n = a.shape[0]
    tk = min(MATMUL_TILE, n)
    nk = n // tk
    cols = acc_shape[1]

    def body(a_ref, b_ref, o_ref, acc):
        k = pl.program_id(2)

        @pl.when(k == 0)
        def _():
            acc[...] = jnp.zeros_like(acc)

        a_t = a_ref[...]
        if relu2:
            a_t = jnp.square(jnp.maximum(a_t.astype(F32), 0.0))
        acc[...] += _dot_tn(a_t, b_ref[...])

        @pl.when(k == nk - 1)
        def _():
            o_ref[...] = acc[...].reshape(o_ref.shape).astype(WIRE_DTYPE)

    gi, gj = grid_ij
    return pl.pallas_call(
        body, name=name, out_shape=jax.ShapeDtypeStruct((N_CHIP, out_rows, cols), WIRE_DTYPE),
        grid=(gi, gj, nk),
        in_specs=[a_spec(tk), b_spec(tk)],
        out_specs=out_spec,
        scratch_shapes=[pltpu.VMEM(acc_shape, F32)],
        compiler_params=_params(("arbitrary", "arbitrary", "arbitrary")),
    )(a, b)


def _row_tile(rows, cols, arrays):
    budget = VMEM_LIMIT // 3
    t = budget // (arrays * 2 * cols * 4)
    t = max(8, min(rows, t // 8 * 8))
    while rows % t:
        t -= 8
    return t


def _sum_partials(name, own, got, myq, l, nl, prev):
    _, rows, cols = own.shape
    tr = _row_tile(rows, cols, 3)
    nt = rows // tr

    def body(q_ref, own_ref, got_ref, *rest):
        o_ref = rest[-1]
        acc = own_ref[...].astype(F32)
        for k in range(3):
            acc = acc + got_ref[k].astype(F32)
        o_ref[...] = acc

    operands = [myq, own, got] + ([] if prev is None else [prev])
    return pl.pallas_call(
        body, name=name, out_shape=jax.ShapeDtypeStruct((nl * rows, cols), F32),
        grid_spec=pltpu.PrefetchScalarGridSpec(
            num_scalar_prefetch=1, grid=(nt,),
            in_specs=[pl.BlockSpec((None, tr, cols), lambda i, q: (q[0], i, 0)),
                      pl.BlockSpec((3, tr, cols), lambda i, q: (0, i, 0))]
            + ([] if prev is None else [pl.BlockSpec(memory_space=pl.ANY)]),
            out_specs=pl.BlockSpec((tr, cols), lambda i, q: (l * nt + i, 0))),
        input_output_aliases={} if prev is None else {3: 0},
        compiler_params=_params(("arbitrary",)),
    )(*operands)


def _adamw(name, w, m, v, g_a, g_b=None):
    rows, cols = w.shape
    tr = _row_tile(rows, cols, 9)
    c1 = 1.0 - ADAM_B1 ** ADAM_STEP
    c2 = 1.0 - ADAM_B2 ** ADAM_STEP

    def body(*refs):
        if g_b is None:
            w_ref, m_ref, v_ref, ga_ref, g_ref, d_ref, m2_ref, v2_ref = refs
            g = ga_ref[...]
        else:
            w_ref, m_ref, v_ref, ga_ref, gb_ref, g_ref, d_ref, m2_ref, v2_ref = refs
            g = ga_ref[...] + gb_ref[...]
        m2 = ADAM_B1 * m_ref[...] + (1.0 - ADAM_B1) * g
        v2 = ADAM_B2 * v_ref[...] + (1.0 - ADAM_B2) * (g * g)
        g_ref[...] = g
        m2_ref[...] = m2
        v2_ref[...] = v2
        d_ref[...] = -ADAM_LR * ((m2 / c1) / (jnp.sqrt(v2 / c2) + ADAM_EPS) + ADAM_WD * w_ref[...])

    tile = pl.BlockSpec((tr, cols), lambda i: (i, 0))
    operands = [w, m, v, g_a] + ([] if g_b is None else [g_b])
    return pl.pallas_call(
        body, name=name, out_shape=[jax.ShapeDtypeStruct((rows, cols), F32)] * 4,
        grid=(rows // tr,), in_specs=[tile] * len(operands), out_specs=[tile] * 4,
        compiler_params=_params(("arbitrary",)),
    )(*operands)


def _pack(parts):
    flat = [p.reshape(-1, LANES) for p in parts]
    for f in flat:
        assert f.shape[0] % 8 == 0
    return jnp.concatenate(flat, axis=0)


def _unpack(packed, shapes):
    out, r = [], 0
    for s in shapes:
        size = 1
        for e in s:
            size *= e
        rows = size // LANES
        out.append(packed[r:r + rows].reshape(s))
        r += rows
    return out


def kernel(x, c, w_ada, b_ada, norm1_g, w_in, a_ln_g, a_ln_b, a_ws, a_bs, w_pa, b_conv_w, b_conv_b, b_ln_g, b_ln_b, w_pb, w_out, norm2_g, w_ff1, w_ff2, final_g, loss_target, m_w_ada, m_b_ada, m_norm1_g, m_w_in, m_a_ln_g, m_a_ln_b, m_a_ws, m_a_bs, m_w_pa, m_b_conv_w, m_b_conv_b, m_b_ln_g, m_b_ln_b, m_w_pb, m_w_out, m_norm2_g, m_w_ff1, m_w_ff2, m_final_g, v_w_ada, v_b_ada, v_norm1_g, v_w_in, v_a_ln_g, v_a_ln_b, v_a_ws, v_a_bs, v_w_pa, v_b_conv_w, v_b_conv_b, v_b_ln_g, v_b_ln_b, v_w_pb, v_w_out, v_norm2_g, v_w_ff1, v_w_ff2, v_final_g):
    nb, t_len, d = x.shape
    nl = w_in.shape[0]
    n = nb * t_len
    cq = w_ada.shape[-1]
    cc = d // N_CHIP
    mx, my, mc = _my_place()
    myq = (2 * mx + my).astype(jnp.int32).reshape(1)

    c_slots = jnp.broadcast_to(c[None], (N_DEV, nb, d))
    c_all = _all_to_all(c_slots, "gather_c").reshape(N_DEV * nb, d)
    mod_part = _ada_forward(c_all, w_ada, b_ada.reshape(nl, 1, N_CHIP * cq), myq)
    mod_slots = mod_part.reshape(nl, N_DEV, nb, cq).transpose(1, 0, 2, 3).reshape(N_DEV, nl * nb, cq)
    mod_got = _all_to_all(mod_slots, "exchange_mod").reshape(N_CHIP, 2, nl, nb, cq)[:, 0]
    mod6 = mod_got.transpose(1, 2, 0, 3).reshape(nl, nb, 6, d)
    mod = jnp.pad(mod6, ((0, 0), (0, 0), (0, 2), (0, 0)))

    cw_mine = b_conv_w.reshape(nl * CONV_TAPS, cc)
    cwg = _all_to_all(jnp.broadcast_to(cw_mine[None], (N_DEV,) + cw_mine.shape), "gather_conv_w")
    cwg = cwg.reshape(N_CHIP, 2, nl, CONV_TAPS, cc)[:, 0]
    cw = jnp.pad(cwg.transpose(1, 2, 0, 3).reshape(nl, CONV_TAPS, d), ((0, 0), (0, HALO - CONV_TAPS), (0, 0)))

    big = ["w_in", "w_pa", "w_pb", "w_out", "w_ff1", "w_ff2"]
    ws_given = dict(w_in=(w_in, m_w_in, v_w_in), w_pa=(w_pa, m_w_pa, v_w_pa), w_pb=(w_pb, m_w_pb, v_w_pb),
                    w_out=(w_out, m_w_out, v_w_out), w_ff1=(w_ff1, m_w_ff1, v_w_ff1), w_ff2=(w_ff2, m_w_ff2, v_w_ff2))

    def own_slot(w_l):
        empty = lax.empty((N_CHIP,) + w_l.shape, WIRE_DTYPE)
        return lax.dynamic_update_index_in_dim(empty, w_l.astype(WIRE_DTYPE), myq[0], 0)

    def zero_after(*arrays):
        z = jnp.zeros((8, LANES), F32)
        for a in arrays:
            piece = a.reshape(-1, a.shape[-1])[:8, :LANES]
            z = z + jnp.where(jnp.isfinite(piece), piece, 0.0) * 0.0
        return z

    token = zero_after(cw, mod[:, 0])
    gathers = []
    for l in range(nl):
        send_sems, recv_sems, _, lands, token = _split_start(
            f"gather_start_{l}", "gather", [], [own_slot(ws_given[k][0][l]) for k in big], token)
        gathers.append((send_sems, recv_sems, lands))
    mod = mod + token[0, 0]

    def gather_wait(l, part, lo, hi, after):
        send_sems, recv_sems, lands = gathers[l]
        return _split_wait(f"gather_wait_{part}_{l}", "gather", send_sems[lo:hi], recv_sems[lo:hi], [],
                           lands[lo:hi], after)[1]

    vec3 = lambda p: p.reshape(nl, 1, d)
    g1, g2 = vec3(norm1_g), vec3(norm2_g)
    lng, lnb, cb, blg, blb = vec3(a_ln_g), vec3(a_ln_b), vec3(b_conv_b), vec3(b_ln_g), vec3(b_ln_b)
    bst = a_bs.transpose(0, 2, 1)

    xs = x.reshape(n, d)
    saved = []
    weights = []
    for l in range(nl):
        (wg_in,) = gather_wait(l, "in", 0, 1, mod if l == 0 else xs)
        h, proj = _in_proj(l, xs, mod, g1, wg_in, t_len)
        ya_in, yb_in, zc = _branches_fwd(l, proj, lng, lnb, a_ws, bst, cw, cb, blg, blb, t_len)
        wg_pa, wg_pb, wg_out = gather_wait(l, "mid", 1, 4, ya_in)
        ya, yb, merged, o, x1 = _merge_out(l, xs, mod, proj, ya_in, yb_in, wg_pa, wg_pb, wg_out, t_len)
        wg_ff1, wg_ff2 = gather_wait(l, "ffn", 4, 6, x1)
        h2, f, o2, x2 = _ffn_fwd(l, x1, mod, g2, wg_ff1, wg_ff2, t_len)
        saved.append((xs, h, proj, ya_in, yb_in, zc, ya, yb, merged, o, x1, h2, f, o2))
        weights.append((wg_in, wg_pa, wg_pb, wg_out, wg_ff1, wg_ff2))
        xs = x2

    loss_blk, dx, dfinal = _loss_head(xs, final_g.reshape(1, d), loss_target.reshape(n, d))
    loss = lax.psum(loss_blk[0, 0], ("x", "y", "c"))

    tok = lambda w: (lambda tk: pl.BlockSpec((tk, w), lambda i, j, k: (k, 0)))
    tok_i = lambda w: (lambda tk: pl.BlockSpec((tk, w), lambda i, j, k: (k, i)))
    tok_j = lambda w: (lambda tk: pl.BlockSpec((tk, w), lambda i, j, k: (k, j)))
    qin = weights[0][0].shape[-1]
    hq = weights[0][4].shape[-1]
    rq = d // N_CHIP
    slot_i = lambda r, cdim: pl.BlockSpec((None, r, cdim), lambda i, j, k: (i, 0, 0))
    slot_j = lambda r, cdim: pl.BlockSpec((None, r, cdim), lambda i, j, k: (j, 0, 0))
    all_slots = pl.BlockSpec((N_CHIP, rq, d), lambda i, j, k: (0, 0, 0))
    scatters = []

    def scatter_start(l, part, names, grads, after):
        lands = [lax.empty((3,) + g.shape[1:], g.dtype) for g in grads]
        send_sems, recv_sems, srcs, lands, tok_out = _split_start(f"scatter_start_{part}_{l}", "scatter", grads, lands,
                                                                  after)
        scatters.append((f"scatter_wait_{part}_{l}", l, names, send_sems, recv_sems, srcs, lands))
        return tok_out

    dmods, small = [None] * nl, [None] * nl
    for l in reversed(range(nl)):
        x0, h, proj, ya_in, yb_in, zc, ya, yb, merged, o, x1, h2, f, o2 = saved[l]
        wg_in, wg_pa, wg_pb, wg_out, wg_ff1, wg_ff2 = weights[l]
        do2, df, dx1, dmod_c, dg2 = _ffn_bwd(l, dx, x1, mod, g2, o2, f, wg_ff1, wg_ff2, t_len, nb)
        g_ff2 = _weight_grad(f"grad_w_ff2_{l}", f, do2, tok_i(hq), tok(d), hq, slot_i(hq, d), (hq, d), (N_CHIP, 1),
                             relu2=True)
        g_ff1 = _weight_grad(f"grad_w_ff1_{l}", h2, df, tok(d), tok_j(hq), d, slot_j(d, hq), (d, hq), (1, N_CHIP))
        token = scatter_start(l, "ffn", ["w_ff2", "w_ff1"], [g_ff2, g_ff1], token)
        mod = mod + token[0, 0]
        do, dya, dyb, dya_in, dyb_in, dproj, dmod_b = _merge_bwd(l, dx1, mod, o, ya, yb, proj, wg_pa, wg_pb, wg_out,
                                                                 t_len, nb)
        g_out = _weight_grad(f"grad_w_out_{l}", merged, do, tok(d), tok(d), rq, all_slots, (d, d), (1, 1))
        g_pa = _weight_grad(f"grad_w_pa_{l}", ya_in, dya, tok(d), tok(d), rq, all_slots, (d, d), (1, 1))
        g_pb = _weight_grad(f"grad_w_pb_{l}", yb_in, dyb, tok(d), tok(d), rq, all_slots, (d, d), (1, 1))
        token = scatter_start(l, "mid", ["w_out", "w_pa", "w_pb"], [g_out, g_pa, g_pb], token)
        lng = lng + token[0, 0]
        dproj, dws, dbst, dcw, vecs = _branches_bwd(l, proj, zc, dya_in, dyb_in, dproj, lng, lnb, a_ws, bst, cw,
                                                    blg, blb, t_len)
        g_in = _weight_grad(f"grad_w_in_{l}", h, dproj, tok(d), tok_j(qin), d, slot_j(d, qin), (d, qin), (1, N_CHIP))
        token = scatter_start(l, "in", ["w_in"], [g_in], token)
        mod = mod + token[0, 0]
        dx, dmod_a, dg1 = _in_proj_bwd(l, dproj, dx1, x0, mod, g1, wg_in, t_len, nb)
        dmods[l] = jnp.concatenate([dmod_a[:, 0:2], dmod_b[:, 2:3], dmod_c[:, 3:6]], axis=1)
        small[l] = (dg1[0], vecs[0], vecs[1], dws, dbst.T, dcw[:CONV_TAPS], vecs[2], vecs[3], vecs[4], dg2[0])
    grad_x = dx.reshape(nb, t_len, d)

    dmod_mine = jnp.stack(dmods).reshape(nl * nb, 6 * d)
    dmod_all = _all_to_all(jnp.broadcast_to(dmod_mine[None], (N_DEV,) + dmod_mine.shape), "gather_dmod")
    dmod_all = dmod_all.reshape(N_DEV, nl, nb, 6 * d).transpose(1, 0, 2, 3).reshape(nl, N_DEV * nb, 6 * d)

    names = ["norm1_g", "a_ln_g", "a_ln_b", "a_ws", "a_bs", "b_conv_w", "b_conv_b", "b_ln_g", "b_ln_b", "norm2_g"]
    stacked = [jnp.stack([small[l][k] for l in range(nl)]) for k in range(len(names))]
    stacked[5] = jnp.pad(stacked[5], ((0, 0), (0, HALO - CONV_TAPS), (0, 0)))
    stacked.append(dfinal)
    part_shapes = [s.shape for s in stacked]
    packed = _pack(stacked)
    prow = packed.shape[0]
    pad_rows = (-prow) % (8 * N_DEV)
    packed = jnp.pad(packed, ((0, pad_rows), (0, 0)))
    srow = packed.shape[0] // N_DEV
    mine = _all_to_all(packed.reshape(N_DEV, srow, LANES), "reduce_small", reduce=True)
    total = _all_to_all(jnp.broadcast_to(mine[None], (N_DEV, srow, LANES)), "gather_small")
    total = total.reshape(N_DEV * srow, LANES)[:prow]

    half = dict.fromkeys(big)
    for name, l, group, send_sems, recv_sems, srcs, lands in scatters:
        srcs, lands = _split_wait(name, "scatter", send_sems, recv_sems, srcs, lands, total)
        for k, g_own, g_got in zip(group, srcs, lands):
            half[k] = _sum_partials(f"sum_{k}_{l}", g_own, g_got, myq, l, nl, half[k])
    sums = [half[k] for k in big]
    swap_send, swap_recv, sums, others, token = _split_start(
        "swap_start", "swap", sums, [lax.empty(s.shape, s.dtype) for s in sums], total)
    dmod_all = dmod_all + token[0, 0]
    g_w_ada, g_b_ada = _ada_backward(c_all, dmod_all, myq, cq)

    sg = dict(zip(names + ["final_g"], _unpack(total, part_shapes)))
    sg["b_conv_w"] = lax.dynamic_slice_in_dim(sg["b_conv_w"][:, :CONV_TAPS], myq[0] * cc, cc, axis=2).reshape(
        nl, CONV_TAPS, 1, cc)
    sg["final_g"] = sg["final_g"][0]
    sg["b_ada"] = g_b_ada.reshape(nl, N_CHIP * cq)
    small_names = ["b_ada", "norm1_g", "a_ln_g", "a_ln_b", "a_ws", "a_bs", "b_conv_w", "b_conv_b", "b_ln_g",
                   "b_ln_b", "norm2_g", "final_g"]
    given = dict(b_ada=(b_ada, m_b_ada, v_b_ada), norm1_g=(norm1_g, m_norm1_g, v_norm1_g),
                 a_ln_g=(a_ln_g, m_a_ln_g, v_a_ln_g), a_ln_b=(a_ln_b, m_a_ln_b, v_a_ln_b),
                 a_ws=(a_ws, m_a_ws, v_a_ws), a_bs=(a_bs, m_a_bs, v_a_bs),
                 b_conv_w=(b_conv_w, m_b_conv_w, v_b_conv_w), b_conv_b=(b_conv_b, m_b_conv_b, v_b_conv_b),
                 b_ln_g=(b_ln_g, m_b_ln_g, v_b_ln_g), b_ln_b=(b_ln_b, m_b_ln_b, v_b_ln_b),
                 norm2_g=(norm2_g, m_norm2_g, v_norm2_g), final_g=(final_g, m_final_g, v_final_g))

    def padded(a):
        rows = -(-a.size // (8 * LANES)) * 8
        return jnp.pad(a.reshape(-1), (0, rows * LANES - a.size)).reshape(rows, LANES)

    packs = [_pack([padded(given[k][j]) for k in small_names]) for j in range(3)]
    gpack = _pack([padded(sg[k].astype(F32)) for k in small_names])
    res_small = _adamw("adamw_small", packs[0], packs[1], packs[2], gpack)
    out = {}
    for j, kind in enumerate(["grad", "delta", "new_m", "new_v"]):
        r = 0
        for k in small_names:
            a = given[k][0]
            rows = -(-a.size // (8 * LANES)) * 8
            out[(kind, k)] = res_small[j][r:r + rows].reshape(-1)[:a.size].reshape(a.shape)
            r += rows

    res = _adamw("adamw_w_ada", w_ada.reshape(nl * d, cq), m_w_ada.reshape(nl * d, cq), v_w_ada.reshape(nl * d, cq),
                 g_w_ada.reshape(nl * d, cq))
    for kind, r in zip(["grad", "delta", "new_m", "new_v"], res):
        out[(kind, "w_ada")] = r.reshape(w_ada.shape)

    sums, others = _split_wait("swap_wait", "swap", swap_send, swap_recv, sums, others, res[0])
    for k, s_mine, s_other in zip(big, sums, others):
        w, m, v = ws_given[k]
        cols = w.shape[-1]
        res = _adamw(f"adamw_{k}", w.reshape(-1, cols), m.reshape(-1, cols), v.reshape(-1, cols), s_mine, s_other)
        for kind, r in zip(["grad", "delta", "new_m", "new_v"], res):
            out[(kind, k)] = r.reshape(w.shape)

    order = ["w_ada", "b_ada", "norm1_g", "w_in", "a_ln_g", "a_ln_b", "a_ws", "a_bs", "w_pa", "b_conv_w", "b_conv_b",
             "b_ln_g", "b_ln_b", "w_pb", "w_out", "norm2_g", "w_ff1", "w_ff2", "final_g"]
    return (loss, grad_x, *[out[("grad", k)] for k in order], *[out[("delta", k)] for k in order],
            *[out[("new_m", k)] for k in order], *[out[("new_v", k)] for k in order])
```

```python
import functools

import jax
import jax.numpy as jnp
from jax import lax
from jax.experimental import pallas as pl
from jax.experimental.pallas import tpu as pltpu

F32 = jnp.float32
MXU_DTYPE = jnp.bfloat16
ACT_DTYPE = jnp.bfloat16
WIRE_DTYPE = jnp.bfloat16

EPS = 1e-6
CHUNK = 128
HEADS = 8
CONV_TAPS = 31
HALO = 32
N_DEV = 8
N_CHIP = 4
ADAM_LR, ADAM_B1, ADAM_B2, ADAM_EPS, ADAM_WD, ADAM_STEP = 0.001, 0.9, 0.999, 1e-08, 0.01, 10

V7X_VMEM_BYTES = 64 * 1024 * 1024
VMEM_LIMIT = V7X_VMEM_BYTES * 7 // 8
TOKEN_TILE = 512
MATMUL_TILE = 1024
FFN_BWD_TILE = 256
CONV_ROWS = 64
TAP_GRAD_ROWS = 32
TAP_GROUP = 16
LANES = 128
MESH_ID = pl.DeviceIdType.MESH


def _params(sem=None):
    return pltpu.CompilerParams(dimension_semantics=sem, vmem_limit_bytes=VMEM_LIMIT)


def _resident(shape):
    return pl.BlockSpec(shape, lambda *_: (0,) * len(shape), pipeline_mode=pl.Buffered(1))


def _dot(a, b):
    return jnp.dot(a.astype(MXU_DTYPE), b.astype(MXU_DTYPE), preferred_element_type=F32)


def _dot_nt(a, b):
    return lax.dot_general(a.astype(MXU_DTYPE), b.astype(MXU_DTYPE), (((1,), (1,)), ((), ())),
                           preferred_element_type=F32)


def _dot_tn(a, b):
    return lax.dot_general(a.astype(MXU_DTYPE), b.astype(MXU_DTYPE), (((0,), (0,)), ((), ())),
                           preferred_element_type=F32)


def _colsum(a):
    return jnp.sum(a, axis=0, keepdims=True)


def _rowmean(a):
    return jnp.mean(a, axis=-1, keepdims=True)


def _sigmoid(a):
    return 1.0 / (1.0 + jnp.exp(-a))


def _modnorm_fwd(x, g, sc, sh):
    r = lax.rsqrt(_rowmean(x * x) + EPS)
    return (x * r) * (g * (1.0 + sc)) + sh


def _modnorm_bwd(x, dh, g, sc):
    r = lax.rsqrt(_rowmean(x * x) + EPS)
    xn = x * r
    dxn = dh * (g * (1.0 + sc))
    dx = r * (dxn - xn * _rowmean(dxn * xn))
    return dx, _colsum(dh), _colsum(dh * xn)


def _ln_stats(v):
    mu = _rowmean(v)
    vc = v - mu
    rstd = lax.rsqrt(_rowmean(vc * vc) + EPS)
    return vc * rstd, rstd


def _ln_bwd(dy, vhat, rstd, g):
    dvh = dy * g
    return rstd * (dvh - _rowmean(dvh) - vhat * _rowmean(dvh * vhat))


def _causal_mask():
    row = lax.broadcasted_iota(jnp.int32, (CHUNK, CHUNK), 0)
    col = lax.broadcasted_iota(jnp.int32, (CHUNK, CHUNK), 1)
    return row >= col


def _my_place():
    return lax.axis_index("x"), lax.axis_index("y"), lax.axis_index("c")


def _all_to_all(x, name, reduce=False):
    n, rows, cols = x.shape
    assert n == N_DEV

    def body(x_ref, o_ref, *scratch):
        if reduce:
            land, send_sems, recv_sems = scratch
        else:
            land = o_ref
            send_sems, recv_sems = scratch
        mx, my, mc = _my_place()
        me = 4 * mx + 2 * my + mc
        land[me] = x_ref[me]
        copies = []
        for k in range(1, N_DEV):
            px = (mx + ((k >> 2) & 1)) % 2
            py = (my + ((k >> 1) & 1)) % 2
            pc = (mc + (k & 1)) % 2
            peer = 4 * px + 2 * py + pc
            cp = pltpu.make_async_remote_copy(
                src_ref=x_ref.at[peer], dst_ref=land.at[me],
                send_sem=send_sems.at[k - 1], recv_sem=recv_sems.at[k - 1],
                device_id=(px, py, pc), device_id_type=MESH_ID)
            cp.start()
            copies.append(cp)
        for cp in copies:
            cp.wait()
        if reduce:
            acc = land[0]
            for s in range(1, N_DEV):
                acc = acc + land[s]
            o_ref[...] = acc

    scratch = [pltpu.SemaphoreType.DMA((N_DEV - 1,)), pltpu.SemaphoreType.DMA((N_DEV - 1,))]
    if reduce:
        scratch = [pltpu.VMEM((N_DEV, rows, cols), x.dtype)] + scratch
        out_shape = jax.ShapeDtypeStruct((rows, cols), x.dtype)
    else:
        out_shape = jax.ShapeDtypeStruct(x.shape, x.dtype)
    return pl.pallas_call(
        body, name=name, out_shape=out_shape,
        in_specs=[pl.BlockSpec(memory_space=pltpu.VMEM)],
        out_specs=pl.BlockSpec(memory_space=pltpu.VMEM),
        scratch_shapes=scratch,
        compiler_params=pltpu.CompilerParams(vmem_limit_bytes=VMEM_LIMIT),
    )(x)


def _other_chips(mx, my):
    return [(1 - mx, my), (mx, 1 - my), (1 - mx, 1 - my)]


HBM_SPEC = pl.BlockSpec(memory_space=pltpu.HBM)
SEM_SPEC = pl.BlockSpec(memory_space=pltpu.SEMAPHORE)
ANY_SPEC = pl.BlockSpec(memory_space=pl.ANY)
SPLIT_EFFECT = pltpu.SideEffectType.DATAFLOW_SIDE_EFFECTING


def _quarter_copies(mode, srcs, lands, send_sems, recv_sems):
    mx, my, mc = _my_place()
    myq = 2 * mx + my
    if mode == "swap":
        return [pltpu.make_async_remote_copy(
            src_ref=srcs[a], dst_ref=lands[a], send_sem=send_sems[a].at[0], recv_sem=recv_sems[a].at[0],
            device_id=(mx, my, 1 - mc), device_id_type=MESH_ID) for a in range(len(lands))]
    copies = []
    for a in range(len(lands)):
        for k, (px, py) in enumerate(_other_chips(mx, my)):
            if mode == "gather":
                src, dst = lands[a].at[myq], lands[a].at[myq]
            else:
                src, dst = srcs[a].at[2 * px + py], lands[a].at[k]
            copies.append(pltpu.make_async_remote_copy(
                src_ref=src, dst_ref=dst, send_sem=send_sems[a].at[k], recv_sem=recv_sems[a].at[k],
                device_id=(px, py, mc), device_id_type=MESH_ID))
    return copies


def _split_start(name, mode, srcs, lands, after):
    ns, n = len(srcs), len(lands)

    def body(*refs):
        outs = refs[ns + n + 1:]
        for cp in _quarter_copies(mode, refs[:ns], refs[ns:ns + n], outs[:n], outs[n:2 * n]):
            cp.start()
        token = outs[-1]
        token[...] = jnp.zeros_like(token)

    arrays = list(srcs) + list(lands)
    per_array = 1 if mode == "swap" else 3
    res = pl.pallas_call(
        body, name=name,
        out_shape=[pltpu.SemaphoreType.DMA((per_array,))] * (2 * n) + [pltpu.HBM(x.shape, x.dtype) for x in arrays]
        + [jax.ShapeDtypeStruct((8, LANES), F32)],
        in_specs=[HBM_SPEC] * (ns + n) + [ANY_SPEC],
        out_specs=[SEM_SPEC] * (2 * n) + [HBM_SPEC] * (ns + n) + [pl.BlockSpec(memory_space=pltpu.VMEM)],
        input_output_aliases={i: 2 * n + i for i in range(ns + n)},
        compiler_params=pltpu.CompilerParams(has_side_effects=SPLIT_EFFECT),
    )(*[pltpu.with_memory_space_constraint(x, pltpu.HBM) for x in arrays], after)
    return res[:n], res[n:2 * n], res[2 * n:2 * n + ns], res[2 * n + ns:2 * n + ns + n], res[-1]


def _split_wait(name, mode, send_sems, recv_sems, srcs, lands, after):
    ns, n = len(srcs), len(lands)

    def body(*refs):
        sems = refs[ns + n:ns + 3 * n]
        for cp in _quarter_copies(mode, refs[:ns], refs[ns:ns + n], sems[:n], sems[n:]):
            cp.wait_send()
            cp.wait_recv()

    arrays = list(srcs) + list(lands)
    res = pl.pallas_call(
        body, name=name,
        out_shape=[pltpu.HBM(x.shape, x.dtype) for x in arrays],
        in_specs=[HBM_SPEC] * (ns + n) + [SEM_SPEC] * (2 * n) + [ANY_SPEC],
        out_specs=[HBM_SPEC] * (ns + n),
        input_output_aliases={i: i for i in range(ns + n)},
        compiler_params=pltpu.CompilerParams(has_side_effects=SPLIT_EFFECT),
    )(*arrays, *send_sems, *recv_sems, after)
    return res[:ns], res[ns:]


def _ada_forward(c_all, w_ada, b_ada3, myq):
    nl, d, cq = w_ada.shape
    nb = c_all.shape[0]

    def body(q_ref, c_ref, w_ref, b_ref, o_ref):
        c = c_ref[...]
        act = c * _sigmoid(c)
        o_ref[...] = _dot(act, w_ref[...]) + b_ref[...]

    return pl.pallas_call(
        body, name="ada_forward",
        out_shape=jax.ShapeDtypeStruct((nl, nb, cq), F32),
        grid_spec=pltpu.PrefetchScalarGridSpec(
            num_scalar_prefetch=1, grid=(nl,),
            in_specs=[pl.BlockSpec((nb, d), lambda l, q: (0, 0)),
                      pl.BlockSpec((None, d, cq), lambda l, q: (l, 0, 0)),
                      pl.BlockSpec((None, 1, cq), lambda l, q: (l, 0, q[0]))],
            out_specs=pl.BlockSpec((None, nb, cq), lambda l, q: (l, 0, 0))),
        compiler_params=_params(("arbitrary",)),
    )(myq, c_all, w_ada, b_ada3)


def _ada_backward(c_all, dmod_all, myq, cq):
    nb, d = c_all.shape
    nl = dmod_all.shape[0]
    full = dmod_all.shape[2]

    def body(q_ref, c_ref, dq_ref, dall_ref, gw_ref, gb_ref):
        c = c_ref[...]
        act = c * _sigmoid(c)
        gw_ref[...] = _dot_tn(act, dq_ref[...])
        gb_ref[...] = _colsum(dall_ref[...])

    return pl.pallas_call(
        body, name="ada_backward",
        out_shape=[jax.ShapeDtypeStruct((nl, d, cq), F32), jax.ShapeDtypeStruct((nl, 1, full), F32)],
        grid_spec=pltpu.PrefetchScalarGridSpec(
            num_scalar_prefetch=1, grid=(nl,),
            in_specs=[pl.BlockSpec((nb, d), lambda l, q: (0, 0)),
                      pl.BlockSpec((None, nb, cq), lambda l, q: (l, 0, q[0])),
                      pl.BlockSpec((None, nb, full), lambda l, q: (l, 0, 0))],
            out_specs=[pl.BlockSpec((None, d, cq), lambda l, q: (l, 0, 0)),
                       pl.BlockSpec((None, 1, full), lambda l, q: (l, 0, 0))]),
        compiler_params=_params(("arbitrary",)),
    )(myq, c_all, dmod_all, dmod_all)


def _in_proj(l, x, mod, g1, wg_in, t_len):
    n, d = x.shape
    tm = min(MATMUL_TILE, t_len)
    tpb = t_len // tm
    qc = wg_in.shape[-1]

    def body(x_ref, mod_ref, g_ref, w_ref, h_ref, proj_ref, h_s):
        @pl.when(pl.program_id(1) == 0)
        def _():
            h = _modnorm_fwd(x_ref[...], g_ref[...], mod_ref[1:2, :], mod_ref[0:1, :])
            h_s[...] = h.astype(MXU_DTYPE)
            h_ref[...] = h.astype(ACT_DTYPE)
        proj_ref[...] = jnp.dot(h_s[...], w_ref[...], preferred_element_type=F32).astype(ACT_DTYPE)

    return pl.pallas_call(
        body, name=f"in_proj_{l}",
        out_shape=[jax.ShapeDtypeStruct((n, d), ACT_DTYPE), jax.ShapeDtypeStruct((n, N_CHIP * qc), ACT_DTYPE)],
        grid=(n // tm, N_CHIP),
        in_specs=[pl.BlockSpec((tm, d), lambda i, j: (i, 0)),
                  pl.BlockSpec((None, None, 8, d), lambda i, j: (l, i // tpb, 0, 0)),
                  pl.BlockSpec((None, 1, d), lambda i, j: (l, 0, 0)),
                  pl.BlockSpec((None, d, qc), lambda i, j: (j, 0, 0))],
        out_specs=[pl.BlockSpec((tm, d), lambda i, j: (i, 0)),
                   pl.BlockSpec((tm, qc), lambda i, j: (i, j))],
        scratch_shapes=[pltpu.VMEM((tm, d), MXU_DTYPE)],
        compiler_params=_params(("arbitrary", "arbitrary")),
    )(x, mod, g1, wg_in)


def _masked_ws(ws_ref, wm_s):
    mask = _causal_mask()
    for h in range(HEADS):
        wm_s[h] = jnp.where(mask, ws_ref[h], 0.0).astype(MXU_DTYPE)


def _fill_z(i, tpb, a_ref, g_ref, ah_ref, gh_ref, zext):
    ah = ah_ref[...].astype(F32)
    gh = gh_ref[...].astype(F32)
    keep = jnp.where(i % tpb == 0, 0.0, 1.0)
    _put_lanes(zext, slice(0, HALO), ah * _sigmoid(gh) * keep)


def _put_lanes(dst3, rows, value):
    for lc in range(value.shape[-1] // LANES):
        dst3[lc, rows, :] = value[:, lc * LANES:(lc + 1) * LANES]


def _conv_taps(src3, w3_ref, dst3, lc, nrows, flip):
    for b in range(nrows // CONV_ROWS):
        acc = jnp.zeros((CONV_ROWS, LANES), F32)
        for k in range(CONV_TAPS):
            s = (CONV_TAPS - 1 - k) if flip else (k + 2)
            acc = acc + src3[lc, pl.ds(b * CONV_ROWS + s, CONV_ROWS), :] * w3_ref[lc, k:k + 1, :]
        dst3[lc, b * CONV_ROWS:(b + 1) * CONV_ROWS, :] = acc


def _branches_fwd(l, proj, lng, lnb, ws, bst, cw, cb, blg, blb, t_len):
    n = proj.shape[0]
    d = lng.shape[-1]
    tm = min(TOKEN_TILE, t_len)
    tpb = t_len // tm
    per = tm // HALO
    nchunk = tm // CHUNK

    def body(u_ref, v_ref, a_ref, g_ref, ah_ref, gh_ref, lng_ref, lnb_ref, ws_ref, bst_ref, cw_ref, cb_ref,
             blg_ref, blb_ref, ya_ref, yb_ref, zc_ref, wm_s, zext, zc3):
        i = pl.program_id(0)
        _masked_ws(ws_ref, wm_s)
        _fill_z(i, tpb, a_ref, g_ref, ah_ref, gh_ref, zext)

        def chunk(c, carry):
            r0 = pl.multiple_of(c * CHUNK, CHUNK)
            rows = pl.ds(r0, CHUNK)
            vhat, _ = _ln_stats(v_ref[rows, :].astype(F32))
            vn = (vhat * lng_ref[...] + lnb_ref[...]).astype(MXU_DTYPE)
            u = u_ref[rows, :].astype(F32)
            for h in range(HEADS):
                cols = slice(h * CHUNK, (h + 1) * CHUNK)
                s = jnp.dot(wm_s[h], vn[:, cols], preferred_element_type=F32) + bst_ref[:, h:h + 1]
                ya_ref[rows, cols] = (u[:, cols] * s).astype(ACT_DTYPE)
            a = a_ref[rows, :].astype(F32)
            g = g_ref[rows, :].astype(F32)
            _put_lanes(zext, pl.ds(HALO + r0, CHUNK), a * _sigmoid(g))
            return carry

        lax.fori_loop(0, nchunk, chunk, 0)

        def lane_chunk(lc, carry):
            _conv_taps(zext, cw_ref, zc3, lc, tm, flip=False)
            return carry

        lax.fori_loop(0, d // LANES, lane_chunk, 0)

        def chunk2(c, carry):
            r0 = pl.multiple_of(c * CHUNK, CHUNK)
            rows = pl.ds(r0, CHUNK)
            for lc in range(d // LANES):
                lanes = slice(lc * LANES, (lc + 1) * LANES)
                zc_ref[rows, lanes] = (zc3[lc, rows, :] + cb_ref[:, lanes]).astype(ACT_DTYPE)
            zhat, _ = _ln_stats(zc_ref[rows, :].astype(F32))
            zn = zhat * blg_ref[...] + blb_ref[...]
            yb_ref[rows, :] = (zn * _sigmoid(zn)).astype(ACT_DTYPE)
            return carry

        lax.fori_loop(0, nchunk, chunk2, 0)

    col = lambda k: pl.BlockSpec((tm, d), lambda i: (i, k))
    halo = lambda k: pl.BlockSpec((HALO, d), lambda i: (jnp.maximum(i * per - 1, 0), k))
    vec = pl.BlockSpec((None, 1, d), lambda i: (l, 0, 0))
    out = pl.BlockSpec((tm, d), lambda i: (i, 0))
    return pl.pallas_call(
        body, name=f"branches_fwd_{l}",
        out_shape=[jax.ShapeDtypeStruct((n, d), ACT_DTYPE)] * 3,
        grid=(n // tm,),
        in_specs=[col(0), col(1), col(2), col(3), halo(2), halo(3), vec, vec,
                  pl.BlockSpec((None, HEADS, CHUNK, CHUNK), lambda i: (l, 0, 0, 0)),
                  pl.BlockSpec((None, CHUNK, HEADS), lambda i: (l, 0, 0)),
                  pl.BlockSpec((None, d // LANES, HALO, LANES), lambda i: (l, 0, 0, 0)), vec, vec, vec],
        out_specs=[out, out, out],
        scratch_shapes=[pltpu.VMEM((HEADS, CHUNK, CHUNK), MXU_DTYPE), pltpu.VMEM((d // LANES, HALO + tm, LANES), F32),
                        pltpu.VMEM((d // LANES, tm, LANES), F32)],
        compiler_params=_params(("arbitrary",)),
    )(proj, proj, proj, proj, proj, proj, lng, lnb, ws, bst, cw, cb, blg, blb)


def _merge_out(l, x, mod, proj, ya_in, yb_in, wg_pa, wg_pb, wg_out, t_len):
    n, d = x.shape
    tm = min(TOKEN_TILE, t_len)
    tpb = t_len // tm
    rq = d // N_CHIP

    def body(x_ref, mod_ref, ga_ref, gb_ref, yai_ref, ybi_ref, wpa_ref, wpb_ref, wo_ref,
             ya_ref, yb_ref, mg_ref, o_ref, x1_ref):
        wpa = wpa_ref[...].reshape(d, d)
        wpb = wpb_ref[...].reshape(d, d)
        wo = wo_ref[...].reshape(d, d)
        ya = jnp.dot(yai_ref[...].astype(MXU_DTYPE), wpa, preferred_element_type=F32)
        yb = jnp.dot(ybi_ref[...].astype(MXU_DTYPE), wpb, preferred_element_type=F32)
        merged = _sigmoid(ga_ref[...].astype(F32)) * ya + _sigmoid(gb_ref[...].astype(F32)) * yb
        o = _dot(merged, wo)
        ya_ref[...] = ya.astype(ACT_DTYPE)
        yb_ref[...] = yb.astype(ACT_DTYPE)
        mg_ref[...] = merged.astype(ACT_DTYPE)
        o_ref[...] = o.astype(ACT_DTYPE)
        x1_ref[...] = x_ref[...] + mod_ref[2:3, :] * o

    tile = pl.BlockSpec((tm, d), lambda i: (i, 0))
    wspec = pl.BlockSpec((N_CHIP, rq, d), lambda i: (0, 0, 0))
    return pl.pallas_call(
        body, name=f"merge_out_{l}",
        out_shape=[jax.ShapeDtypeStruct((n, d), ACT_DTYPE)] * 4 + [jax.ShapeDtypeStruct((n, d), F32)],
        grid=(n // tm,),
        in_specs=[tile, pl.BlockSpec((None, None, 8, d), lambda i: (l, i // tpb, 0, 0)),
                  pl.BlockSpec((tm, d), lambda i: (i, 4)), pl.BlockSpec((tm, d), lambda i: (i, 5)),
                  tile, tile, wspec, wspec, wspec],
        out_specs=[tile] * 5,
        compiler_params=_params(("arbitrary",)),
    )(x, mod, proj, proj, ya_in, yb_in, wg_pa, wg_pb, wg_out)


def _ffn_fwd(l, x1, mod, g2, wg_ff1, wg_ff2, t_len):
    n, d = x1.shape
    tm = min(TOKEN_TILE, t_len)
    tpb = t_len // tm
    hq = wg_ff1.shape[-1]
    hid = N_CHIP * hq

    def body(x_ref, mod_ref, g_ref, w1_ref, w2_ref, h_ref, f_ref, o2_ref, x2_ref, a2_s):
        h = _modnorm_fwd(x_ref[...], g_ref[...], mod_ref[4:5, :], mod_ref[3:4, :]).astype(MXU_DTYPE)
        h_ref[...] = h.astype(ACT_DTYPE)
        for q in range(N_CHIP):
            cols = slice(q * hq, (q + 1) * hq)
            f = jnp.dot(h, w1_ref[q], preferred_element_type=F32)
            f_ref[:, cols] = f.astype(ACT_DTYPE)
            a2_s[:, cols] = jnp.square(jnp.maximum(f, 0.0)).astype(MXU_DTYPE)
        o2 = jnp.dot(a2_s[...], w2_ref[...].reshape(hid, d), preferred_element_type=F32)
        o2_ref[...] = o2.astype(ACT_DTYPE)
        x2_ref[...] = x_ref[...] + mod_ref[5:6, :] * o2

    tile = pl.BlockSpec((tm, d), lambda i: (i, 0))
    return pl.pallas_call(
        body, name=f"ffn_fwd_{l}",
        out_shape=[jax.ShapeDtypeStruct((n, d), ACT_DTYPE), jax.ShapeDtypeStruct((n, hid), ACT_DTYPE),
                   jax.ShapeDtypeStruct((n, d), ACT_DTYPE), jax.ShapeDtypeStruct((n, d), F32)],
        grid=(n // tm,),
        in_specs=[tile, pl.BlockSpec((None, None, 8, d), lambda i: (l, i // tpb, 0, 0)),
                  pl.BlockSpec((None, 1, d), lambda i: (l, 0, 0)),
                  _resident((N_CHIP, d, hq)), _resident((N_CHIP, hq, d))],
        out_specs=[tile, pl.BlockSpec((tm, hid), lambda i: (i, 0)), tile, tile],
        scratch_shapes=[pltpu.VMEM((tm, hid), MXU_DTYPE)],
        compiler_params=_params(("arbitrary",)),
    )(x1, mod, g2, wg_ff1, wg_ff2)


def _loss_head(x, final_g, target):
    n, d = x.shape
    tm = min(TOKEN_TILE, n)

    def body(x_ref, g_ref, t_ref, loss_ref, dx_ref, dg_ref):
        @pl.when(pl.program_id(0) == 0)
        def _():
            loss_ref[...] = jnp.zeros_like(loss_ref)
            dg_ref[...] = jnp.zeros_like(dg_ref)

        x_t = x_ref[...]
        g = g_ref[...]
        r = lax.rsqrt(_rowmean(x_t * x_t) + EPS)
        xn = x_t * r
        e = xn * g - t_ref[...]
        loss_ref[...] += jnp.sum(e * e) * (0.5 / d)
        dy = e * (1.0 / d)
        dxn = dy * g
        dx_ref[...] = r * (dxn - xn * _rowmean(dxn * xn))
        dg_ref[0:1, :] += _colsum(dy * xn)

    tile = pl.BlockSpec((tm, d), lambda i: (i, 0))
    return pl.pallas_call(
        body, name="loss_head",
        out_shape=[jax.ShapeDtypeStruct((8, LANES), F32), jax.ShapeDtypeStruct((n, d), F32),
                   jax.ShapeDtypeStruct((8, d), F32)],
        grid=(n // tm,),
        in_specs=[tile, pl.BlockSpec((1, d), lambda i: (0, 0)), tile],
        out_specs=[pl.BlockSpec((8, LANES), lambda i: (0, 0)), tile, pl.BlockSpec((8, d), lambda i: (0, 0))],
        compiler_params=_params(("arbitrary",)),
    )(x, final_g, target)


def _norm_tail(i, tpb, x_ref, dxin_ref, dh, g_ref, sc, dx_ref, dmod_ref, dg_ref, row_sh, row_sc):
    dxm, dsh, q = _modnorm_bwd(x_ref[...], dh, g_ref[...], sc)
    dx_ref[...] = dxin_ref[...] + dxm
    dmod_ref[row_sh:row_sh + 1, :] += dsh
    dmod_ref[row_sc:row_sc + 1, :] += g_ref[...] * q
    dg_ref[0:1, :] += (1.0 + sc) * q


def _ffn_bwd(l, dx2, x1, mod, g2, o2, f, wg_ff1, wg_ff2, t_len, nb):
    n, d = dx2.shape
    tm = min(FFN_BWD_TILE, t_len)
    tpb = t_len // tm
    hq = wg_ff1.shape[-1]
    hid = N_CHIP * hq

    def body(dx2_ref, x1_ref, mod_ref, g_ref, o2_ref, f_ref, w1_ref, w2_ref,
             do2_ref, df_ref, dx1_ref, dmod_ref, dg_ref):
        i = pl.program_id(0)

        @pl.when(i == 0)
        def _():
            dg_ref[...] = jnp.zeros_like(dg_ref)

        @pl.when(i % tpb == 0)
        def _():
            dmod_ref[...] = jnp.zeros_like(dmod_ref)

        dx2_t = dx2_ref[...]
        dmod_ref[5:6, :] += _colsum(dx2_t * o2_ref[...].astype(F32))
        do2 = (dx2_t * mod_ref[5:6, :]).astype(MXU_DTYPE)
        do2_ref[...] = do2.astype(ACT_DTYPE)
        dh = jnp.zeros((tm, d), F32)
        for q in range(N_CHIP):
            cols = slice(q * hq, (q + 1) * hq)
            da2 = _dot_nt(do2, w2_ref[q])
            df = (da2 * (2.0 * jnp.maximum(f_ref[:, cols].astype(F32), 0.0))).astype(MXU_DTYPE)
            df_ref[:, cols] = df.astype(ACT_DTYPE)
            dh = dh + _dot_nt(df, w1_ref[q])
        _norm_tail(i, tpb, x1_ref, dx2_ref, dh, g_ref, mod_ref[4:5, :], dx1_ref, dmod_ref, dg_ref, 3, 4)

    tile = pl.BlockSpec((tm, d), lambda i: (i, 0))
    wide = pl.BlockSpec((tm, hid), lambda i: (i, 0))
    return pl.pallas_call(
        body, name=f"ffn_bwd_{l}",
        out_shape=[jax.ShapeDtypeStruct((n, d), ACT_DTYPE), jax.ShapeDtypeStruct((n, hid), ACT_DTYPE),
                   jax.ShapeDtypeStruct((n, d), F32), jax.ShapeDtypeStruct((nb, 8, d), F32),
                   jax.ShapeDtypeStruct((8, d), F32)],
        grid=(n // tm,),
        in_specs=[tile, tile, pl.BlockSpec((None, None, 8, d), lambda i: (l, i // tpb, 0, 0)),
                  pl.BlockSpec((None, 1, d), lambda i: (l, 0, 0)), tile, wide,
                  _resident((N_CHIP, d, hq)), _resident((N_CHIP, hq, d))],
        out_specs=[tile, wide, tile, pl.BlockSpec((None, 8, d), lambda i: (i // tpb, 0, 0)),
                   pl.BlockSpec((8, d), lambda i: (0, 0))],
        compiler_params=_params(("arbitrary",)),
    )(dx2, x1, mod, g2, o2, f, wg_ff1, wg_ff2)


def _merge_bwd(l, dx1, mod, o, ya, yb, proj, wg_pa, wg_pb, wg_out, t_len, nb):
    n, d = dx1.shape
    tm = min(TOKEN_TILE, t_len)
    tpb = t_len // tm
    rq = d // N_CHIP

    def body(dx_ref, mod_ref, o_ref, ya_ref, yb_ref, ga_ref, gb_ref, wpa_ref, wpb_ref, wo_ref,
             do_ref, dya_ref, dyb_ref, dyai_ref, dybi_ref, dproj_ref, dmod_ref):
        i = pl.program_id(0)

        @pl.when(i % tpb == 0)
        def _():
            dmod_ref[...] = jnp.zeros_like(dmod_ref)

        dx = dx_ref[...]
        dmod_ref[2:3, :] += _colsum(dx * o_ref[...].astype(F32))
        do = (dx * mod_ref[2:3, :]).astype(MXU_DTYPE)
        do_ref[...] = do.astype(ACT_DTYPE)
        dm = _dot_nt(do, wo_ref[...].reshape(d, d))
        sa = _sigmoid(ga_ref[...].astype(F32))
        sb = _sigmoid(gb_ref[...].astype(F32))
        dya = (dm * sa).astype(MXU_DTYPE)
        dyb = (dm * sb).astype(MXU_DTYPE)
        dya_ref[...] = dya.astype(ACT_DTYPE)
        dyb_ref[...] = dyb.astype(ACT_DTYPE)
        dproj_ref[:, 0:d] = (dm * ya_ref[...].astype(F32) * sa * (1.0 - sa)).astype(ACT_DTYPE)
        dproj_ref[:, d:2 * d] = (dm * yb_ref[...].astype(F32) * sb * (1.0 - sb)).astype(ACT_DTYPE)
        dyai_ref[...] = _dot_nt(dya, wpa_ref[...].reshape(d, d)).astype(ACT_DTYPE)
        dybi_ref[...] = _dot_nt(dyb, wpb_ref[...].reshape(d, d)).astype(ACT_DTYPE)

    tile = pl.BlockSpec((tm, d), lambda i: (i, 0))
    wspec = pl.BlockSpec((N_CHIP, rq, d), lambda i: (0, 0, 0))
    return pl.pallas_call(
        body, name=f"merge_bwd_{l}",
        out_shape=[jax.ShapeDtypeStruct((n, d), ACT_DTYPE)] * 5
        + [jax.ShapeDtypeStruct((n, 6 * d), ACT_DTYPE), jax.ShapeDtypeStruct((nb, 8, d), F32)],
        grid=(n // tm,),
        in_specs=[tile, pl.BlockSpec((None, None, 8, d), lambda i: (l, i // tpb, 0, 0)), tile, tile, tile,
                  pl.BlockSpec((tm, d), lambda i: (i, 4)), pl.BlockSpec((tm, d), lambda i: (i, 5)),
                  wspec, wspec, wspec],
        out_specs=[tile] * 5 + [pl.BlockSpec((tm, 2 * d), lambda i: (i, 2)),
                                pl.BlockSpec((None, 8, d), lambda i: (i // tpb, 0, 0))],
        compiler_params=_params(("arbitrary",)),
    )(dx1, mod, o, ya, yb, proj, proj, wg_pa, wg_pb, wg_out)


def _branches_bwd(l, proj, zc, dya_in, dyb_in, dproj, lng, lnb, ws, bst, cw, blg, blb, t_len):
    n = proj.shape[0]
    d = lng.shape[-1]
    tm = min(TOKEN_TILE, t_len)
    tpb = t_len // tm
    per = tm // HALO
    nchunk = tm // CHUNK
    ntile = n // tm

    def body(u_ref, v_ref, a_ref, g_ref, ah_ref, gh_ref, zc_ref, zcn_ref, dya_ref, dyb_ref, dybn_ref, dproj_in,
             lng_ref, lnb_ref, ws_ref, bst_ref, cw_ref, blg_ref, blb_ref,
             dproj_ref, dws_ref, dbst_ref, dcw_ref, vec_ref, wm_s, zext, dzext, dz3, dvn_s):
        i = pl.program_id(0)

        @pl.when(i == 0)
        def _():
            dws_ref[...] = jnp.zeros_like(dws_ref)
            dbst_ref[...] = jnp.zeros_like(dbst_ref)
            dcw_ref[...] = jnp.zeros_like(dcw_ref)
            vec_ref[...] = jnp.zeros_like(vec_ref)

        _masked_ws(ws_ref, wm_s)
        _fill_z(i, tpb, a_ref, g_ref, ah_ref, gh_ref, zext)

        def conv_ln_bwd(zc_t, dyb_t):
            zhat, rstd = _ln_stats(zc_t)
            zn = zhat * blg_ref[...] + blb_ref[...]
            sg = _sigmoid(zn)
            dzn = dyb_t * (sg * (1.0 + zn * (1.0 - sg)))
            return _ln_bwd(dzn, zhat, rstd, blg_ref[...]), _colsum(dzn * zhat), _colsum(dzn)

        def chunk(c, carry):
            r0 = pl.multiple_of(c * CHUNK, CHUNK)
            rows = pl.ds(r0, CHUNK)
            vhat, rstd = _ln_stats(v_ref[rows, :].astype(F32))
            vn = (vhat * lng_ref[...] + lnb_ref[...]).astype(MXU_DTYPE)
            u = u_ref[rows, :].astype(F32)
            dya = dya_ref[rows, :].astype(F32)
            for h in range(HEADS):
                cols = slice(h * CHUNK, (h + 1) * CHUNK)
                s = jnp.dot(wm_s[h], vn[:, cols], preferred_element_type=F32) + bst_ref[:, h:h + 1]
                dproj_ref[rows, cols] = (dya[:, cols] * s).astype(ACT_DTYPE)
                ds = dya[:, cols] * u[:, cols]
                dvn_s[:, cols] = _dot_tn(wm_s[h], ds)
                dws_ref[h] += _dot_nt(ds, vn[:, cols])
                dbst_ref[:, h:h + 1] += jnp.sum(ds, axis=1, keepdims=True)
            dvn = dvn_s[...]
            dproj_ref[rows, d:2 * d] = _ln_bwd(dvn, vhat, rstd, lng_ref[...]).astype(ACT_DTYPE)
            vec_ref[0:1, :] += _colsum(dvn * vhat)
            vec_ref[1:2, :] += _colsum(dvn)
            a = a_ref[rows, :].astype(F32)
            g = g_ref[rows, :].astype(F32)
            _put_lanes(zext, pl.ds(HALO + r0, CHUNK), a * _sigmoid(g))
            dzc, dblg, dblb = conv_ln_bwd(zc_ref[rows, :].astype(F32), dyb_ref[rows, :].astype(F32))
            _put_lanes(dzext, rows, dzc)
            vec_ref[2:3, :] += _colsum(dzc)
            vec_ref[3:4, :] += dblg
            vec_ref[4:5, :] += dblb
            return carry

        lax.fori_loop(0, nchunk, chunk, 0)

        dzc_next, _, _ = conv_ln_bwd(zcn_ref[...].astype(F32), dybn_ref[...].astype(F32))
        _put_lanes(dzext, slice(tm, tm + HALO), dzc_next * jnp.where(i % tpb == tpb - 1, 0.0, 1.0))

        def lane_chunk(lc, carry):
            _conv_taps(dzext, cw_ref, dz3, lc, tm, flip=True)
            for k0 in range(0, CONV_TAPS, TAP_GROUP):
                taps = range(k0, min(k0 + TAP_GROUP, CONV_TAPS))
                accs = {k: jnp.zeros((8, LANES), F32) for k in taps}
                for b in range(tm // TAP_GRAD_ROWS):
                    dzc = dzext[lc, b * TAP_GRAD_ROWS:(b + 1) * TAP_GRAD_ROWS, :]
                    for k in taps:
                        prod = dzc * zext[lc, pl.ds(b * TAP_GRAD_ROWS + k + 2, TAP_GRAD_ROWS), :]
                        part = prod[0:8]
                        for e in range(1, TAP_GRAD_ROWS // 8):
                            part = part + prod[8 * e:8 * e + 8]
                        accs[k] = accs[k] + part
                for k in taps:
                    dcw_ref[lc, k:k + 1, :] += _colsum(accs[k])
            return carry

        lax.fori_loop(0, d // LANES, lane_chunk, 0)

        def glu_bwd(c, carry):
            r0 = pl.multiple_of(c * CHUNK, CHUNK)
            rows = pl.ds(r0, CHUNK)
            for lc in range(d // LANES):
                lanes = slice(lc * LANES, (lc + 1) * LANES)
                dz = dz3[lc, rows, :]
                a = a_ref[rows, lanes].astype(F32)
                sg = _sigmoid(g_ref[rows, lanes].astype(F32))
                dproj_ref[rows, 2 * d + lc * LANES:2 * d + (lc + 1) * LANES] = (dz * sg).astype(ACT_DTYPE)
                dproj_ref[rows, 3 * d + lc * LANES:3 * d + (lc + 1) * LANES] = (
                    dz * a * sg * (1.0 - sg)).astype(ACT_DTYPE)
            return carry

        lax.fori_loop(0, nchunk, glu_bwd, 0)

        @pl.when(i == ntile - 1)
        def _():
            mask = _causal_mask()
            for h in range(HEADS):
                dws_ref[h] = jnp.where(mask, dws_ref[h], 0.0)

    col = lambda k: pl.BlockSpec((tm, d), lambda i: (i, k))
    tile = pl.BlockSpec((tm, d), lambda i: (i, 0))
    before = lambda k: pl.BlockSpec((HALO, d), lambda i: (jnp.maximum(i * per - 1, 0), k))
    after = pl.BlockSpec((HALO, d), lambda i: (jnp.minimum((i + 1) * per, n // HALO - 1), 0))
    vec = pl.BlockSpec((None, 1, d), lambda i: (l, 0, 0))
    const2 = lambda r, c: pl.BlockSpec((r, c), lambda i: (0, 0))
    return pl.pallas_call(
        body, name=f"branches_bwd_{l}",
        out_shape=[jax.ShapeDtypeStruct((n, 6 * d), ACT_DTYPE), jax.ShapeDtypeStruct((HEADS, CHUNK, CHUNK), F32),
                   jax.ShapeDtypeStruct((CHUNK, HEADS), F32), jax.ShapeDtypeStruct((d // LANES, HALO, LANES), F32),
                   jax.ShapeDtypeStruct((8, d), F32)],
        grid=(ntile,),
        in_specs=[col(0), col(1), col(2), col(3), before(2), before(3), tile, after, tile, tile, after,
                  pl.BlockSpec(memory_space=pl.ANY), vec, vec,
                  pl.BlockSpec((None, HEADS, CHUNK, CHUNK), lambda i: (l, 0, 0, 0)),
                  pl.BlockSpec((None, CHUNK, HEADS), lambda i: (l, 0, 0)),
                  pl.BlockSpec((None, d // LANES, HALO, LANES), lambda i: (l, 0, 0, 0)), vec, vec],
        out_specs=[pl.BlockSpec((tm, 4 * d), lambda i: (i, 0)),
                   pl.BlockSpec((HEADS, CHUNK, CHUNK), lambda i: (0, 0, 0)),
                   const2(CHUNK, HEADS), pl.BlockSpec((d // LANES, HALO, LANES), lambda i: (0, 0, 0)), const2(8, d)],
        scratch_shapes=[pltpu.VMEM((HEADS, CHUNK, CHUNK), MXU_DTYPE),
                        pltpu.VMEM((d // LANES, HALO + tm, LANES), F32),
                        pltpu.VMEM((d // LANES, tm + HALO, LANES), F32),
                        pltpu.VMEM((d // LANES, tm, LANES), F32), pltpu.VMEM((CHUNK, d), F32)],
        input_output_aliases={11: 0},
        compiler_params=_params(("arbitrary",)),
    )(proj, proj, proj, proj, proj, proj, zc, zc, dya_in, dyb_in, dyb_in, dproj, lng, lnb, ws, bst, cw, blg, blb)


def _in_proj_bwd(l, dproj, dx1, x, mod, g1, wg_in, t_len, nb):
    n, d = x.shape
    tm = min(MATMUL_TILE, t_len)
    tpb = t_len // tm
    qc = wg_in.shape[-1]

    def body(dp_ref, dx1_ref, x_ref, mod_ref, g_ref, w_ref, dx_ref, dmod_ref, dg_ref, acc):
        i, j = pl.program_id(0), pl.program_id(1)

        @pl.when((i == 0) & (j == 0))
        def _():
            dg_ref[...] = jnp.zeros_like(dg_ref)

        @pl.when((i % tpb == 0) & (j == 0))
        def _():
            dmod_ref[...] = jnp.zeros_like(dmod_ref)

        @pl.when(j == 0)
        def _():
            acc[...] = jnp.zeros_like(acc)

        acc[...] += _dot_nt(dp_ref[...], w_ref[...])

        @pl.when(j == N_CHIP - 1)
        def _():
            _norm_tail(i, tpb, x_ref, dx1_ref, acc[...], g_ref, mod_ref[1:2, :], dx_ref, dmod_ref, dg_ref, 0, 1)

    tile = pl.BlockSpec((tm, d), lambda i, j: (i, 0))
    return pl.pallas_call(
        body, name=f"in_proj_bwd_{l}",
        out_shape=[jax.ShapeDtypeStruct((n, d), F32), jax.ShapeDtypeStruct((nb, 8, d), F32),
                   jax.ShapeDtypeStruct((8, d), F32)],
        grid=(n // tm, N_CHIP),
        in_specs=[pl.BlockSpec((tm, qc), lambda i, j: (i, j)), tile, tile,
                  pl.BlockSpec((None, None, 8, d), lambda i, j: (l, i // tpb, 0, 0)),
                  pl.BlockSpec((None, 1, d), lambda i, j: (l, 0, 0)),
                  pl.BlockSpec((None, d, qc), lambda i, j: (j, 0, 0))],
        out_specs=[tile, pl.BlockSpec((None, 8, d), lambda i, j: (i // tpb, 0, 0)),
                   pl.BlockSpec((8, d), lambda i, j: (0, 0))],
        scratch_shapes=[pltpu.VMEM((tm, d), F32)],
        compiler_params=_params(("arbitrary", "arbitrary")),
    )(dproj, dx1, x, mod, g1, wg_in)


def _weight_grad(name, a, b, a_spec, b_spec, out_rows, out_spec, acc_shape, grid_ij, relu2=False):
    n = a.shape[0]
    tk = min(MATMUL_TILE, n)
    nk = n // tk
    cols = acc_shape[1]

    def body(a_ref, b_ref, o_ref, acc):
        k = pl.program_id(2)

        @pl.when(k == 0)
        def _():
            acc[...] = jnp.zeros_like(acc)

        a_t = a_ref[...]
        if relu2:
            a_t = jnp.square(jnp.maximum(a_t.astype(F32), 0.0))
        acc[...] += _dot_tn(a_t, b_ref[...])

        @pl.when(k == nk - 1)
        def _():
            o_ref[...] = acc[...].reshape(o_ref.shape).astype(WIRE_DTYPE)

    gi, gj = grid_ij
    return pl.pallas_call(
        body, name=name, out_shape=jax.ShapeDtypeStruct((N_CHIP, out_rows, cols), WIRE_DTYPE),
        grid=(gi, gj, nk),
        in_specs=[a_spec(tk), b_spec(tk)],
        out_specs=out_spec,
        scratch_shapes=[pltpu.VMEM(acc_shape, F32)],
        compiler_params=_params(("arbitrary", "arbitrary", "arbitrary")),
    )(a, b)


def _row_tile(rows, cols, arrays):
    budget = VMEM_LIMIT // 3
    t = budget // (arrays * 2 * cols * 4)
    t = max(8, min(rows, t // 8 * 8))
    while rows % t:
        t -= 8
    return t


def _sum_partials(name, own, got, myq, l, nl, prev):
    _, rows, cols = own.shape
    tr = _row_tile(rows, cols, 3)
    nt = rows // tr

    def body(q_ref, own_ref, got_ref, *rest):
        o_ref = rest[-1]
        acc = own_ref[...].astype(F32)
        for k in range(3):
            acc = acc + got_ref[k].astype(F32)
        o_ref[...] = acc

    operands = [myq, own, got] + ([] if prev is None else [prev])
    return pl.pallas_call(
        body, name=name, out_shape=jax.ShapeDtypeStruct((nl * rows, cols), F32),
        grid_spec=pltpu.PrefetchScalarGridSpec(
            num_scalar_prefetch=1, grid=(nt,),
            in_specs=[pl.BlockSpec((None, tr, cols), lambda i, q: (q[0], i, 0)),
                      pl.BlockSpec((3, tr, cols), lambda i, q: (0, i, 0))]
            + ([] if prev is None else [pl.BlockSpec(memory_space=pl.ANY)]),
            out_specs=pl.BlockSpec((tr, cols), lambda i, q: (l * nt + i, 0))),
        input_output_aliases={} if prev is None else {3: 0},
        compiler_params=_params(("arbitrary",)),
    )(*operands)


def _adamw(name, w, m, v, g_a, g_b=None):
    rows, cols = w.shape
    tr = _row_tile(rows, cols, 9)
    c1 = 1.0 - ADAM_B1 ** ADAM_STEP
    c2 = 1.0 - ADAM_B2 ** ADAM_STEP

    def body(*refs):
        if g_b is None:
            w_ref, m_ref, v_ref, ga_ref, g_ref, d_ref, m2_ref, v2_ref = refs
            g = ga_ref[...]
        else:
            w_ref, m_ref, v_ref, ga_ref, gb_ref, g_ref, d_ref, m2_ref, v2_ref = refs
            g = ga_ref[...] + gb_ref[...]
        m2 = ADAM_B1 * m_ref[...] + (1.0 - ADAM_B1) * g
        v2 = ADAM_B2 * v_ref[...] + (1.0 - ADAM_B2) * (g * g)
        g_ref[...] = g
        m2_ref[...] = m2
        v2_ref[...] = v2
        d_ref[...] = -ADAM_LR * ((m2 / c1) / (jnp.sqrt(v2 / c2) + ADAM_EPS) + ADAM_WD * w_ref[...])

    tile = pl.BlockSpec((tr, cols), lambda i: (i, 0))
    operands = [w, m, v, g_a] + ([] if g_b is None else [g_b])
    return pl.pallas_call(
        body, name=name, out_shape=[jax.ShapeDtypeStruct((rows, cols), F32)] * 4,
        grid=(rows // tr,), in_specs=[tile] * len(operands), out_specs=[tile] * 4,
        compiler_params=_params(("arbitrary",)),
    )(*operands)


def _pack(parts):
    flat = [p.reshape(-1, LANES) for p in parts]
    for f in flat:
        assert f.shape[0] % 8 == 0
    return jnp.concatenate(flat, axis=0)


def _unpack(packed, shapes):
    out, r = [], 0
    for s in shapes:
        size = 1
        for e in s:
            size *= e
        rows = size // LANES
        out.append(packed[r:r + rows].reshape(s))
        r += rows
    return out


def kernel(x, c, w_ada, b_ada, norm1_g, w_in, a_ln_g, a_ln_b, a_ws, a_bs, w_pa, b_conv_w, b_conv_b, b_ln_g, b_ln_b, w_pb, w_out, norm2_g, w_ff1, w_ff2, final_g, loss_target, m_w_ada, m_b_ada, m_norm1_g, m_w_in, m_a_ln_g, m_a_ln_b, m_a_ws, m_a_bs, m_w_pa, m_b_conv_w, m_b_conv_b, m_b_ln_g, m_b_ln_b, m_w_pb, m_w_out, m_norm2_g, m_w_ff1, m_w_ff2, m_final_g, v_w_ada, v_b_ada, v_norm1_g, v_w_in, v_a_ln_g, v_a_ln_b, v_a_ws, v_a_bs, v_w_pa, v_b_conv_w, v_b_conv_b, v_b_ln_g, v_b_ln_b, v_w_pb, v_w_out, v_norm2_g, v_w_ff1, v_w_ff2, v_final_g):
    nb, t_len, d = x.shape
    nl = w_in.shape[0]
    n = nb * t_len
    cq = w_ada.shape[-1]
    cc = d // N_CHIP
    mx, my, mc = _my_place()
    myq = (2 * mx + my).astype(jnp.int32).reshape(1)

    c_slots = jnp.broadcast_to(c[None], (N_DEV, nb, d))
    c_all = _all_to_all(c_slots, "gather_c").reshape(N_DEV * nb, d)
    mod_part = _ada_forward(c_all, w_ada, b_ada.reshape(nl, 1, N_CHIP * cq), myq)
    mod_slots = mod_part.reshape(nl, N_DEV, nb, cq).transpose(1, 0, 2, 3).reshape(N_DEV, nl * nb, cq)
    mod_got = _all_to_all(mod_slots, "exchange_mod").reshape(N_CHIP, 2, nl, nb, cq)[:, 0]
    mod6 = mod_got.transpose(1, 2, 0, 3).reshape(nl, nb, 6, d)
    mod = jnp.pad(mod6, ((0, 0), (0, 0), (0, 2), (0, 0)))

    cw_mine = b_conv_w.reshape(nl * CONV_TAPS, cc)
    cwg = _all_to_all(jnp.broadcast_to(cw_mine[None], (N_DEV,) + cw_mine.shape), "gather_conv_w")
    cwg = cwg.reshape(N_CHIP, 2, nl, CONV_TAPS, cc)[:, 0]
    cw = jnp.pad(cwg.transpose(1, 2, 0, 3).reshape(nl, CONV_TAPS, d), ((0, 0), (0, HALO - CONV_TAPS), (0, 0)))
    cw = cw.reshape(nl, HALO, d // LANES, LANES).transpose(0, 2, 1, 3)

    big = ["w_in", "w_pa", "w_pb", "w_out", "w_ff1", "w_ff2"]
    ws_given = dict(w_in=(w_in, m_w_in, v_w_in), w_pa=(w_pa, m_w_pa, v_w_pa), w_pb=(w_pb, m_w_pb, v_w_pb),
                    w_out=(w_out, m_w_out, v_w_out), w_ff1=(w_ff1, m_w_ff1, v_w_ff1), w_ff2=(w_ff2, m_w_ff2, v_w_ff2))

    def own_slot(w_l):
        empty = lax.empty((N_CHIP,) + w_l.shape, WIRE_DTYPE)
        return lax.dynamic_update_index_in_dim(empty, w_l.astype(WIRE_DTYPE), myq[0], 0)

    def zero_after(*arrays):
        z = jnp.zeros((8, LANES), F32)
        for a in arrays:
            piece = a.reshape(-1, a.shape[-1])[:8, :LANES]
            z = z + jnp.where(jnp.isfinite(piece), piece, 0.0) * 0.0
        return z

    token = zero_after(cw, mod[:, 0])
    gathers = []
    for l in range(nl):
        send_sems, recv_sems, _, lands, token = _split_start(
            f"gather_start_{l}", "gather", [], [own_slot(ws_given[k][0][l]) for k in big], token)
        gathers.append((send_sems, recv_sems, lands))
    mod = mod + token[0, 0]

    def gather_wait(l, part, lo, hi, after):
        send_sems, recv_sems, lands = gathers[l]
        return _split_wait(f"gather_wait_{part}_{l}", "gather", send_sems[lo:hi], recv_sems[lo:hi], [],
                           lands[lo:hi], after)[1]

    vec3 = lambda p: p.reshape(nl, 1, d)
    g1, g2 = vec3(norm1_g), vec3(norm2_g)
    lng, lnb, cb, blg, blb = vec3(a_ln_g), vec3(a_ln_b), vec3(b_conv_b), vec3(b_ln_g), vec3(b_ln_b)
    bst = a_bs.transpose(0, 2, 1)

    xs = x.reshape(n, d)
    saved = []
    weights = []
    for l in range(nl):
        (wg_in,) = gather_wait(l, "in", 0, 1, mod if l == 0 else xs)
        h, proj = _in_proj(l, xs, mod, g1, wg_in, t_len)
        ya_in, yb_in, zc = _branches_fwd(l, proj, lng, lnb, a_ws, bst, cw, cb, blg, blb, t_len)
        wg_pa, wg_pb, wg_out = gather_wait(l, "mid", 1, 4, ya_in)
        ya, yb, merged, o, x1 = _merge_out(l, xs, mod, proj, ya_in, yb_in, wg_pa, wg_pb, wg_out, t_len)
        wg_ff1, wg_ff2 = gather_wait(l, "ffn", 4, 6, x1)
        h2, f, o2, x2 = _ffn_fwd(l, x1, mod, g2, wg_ff1, wg_ff2, t_len)
        saved.append((xs, h, proj, ya_in, yb_in, zc, ya, yb, merged, o, x1, h2, f, o2))
        weights.append((wg_in, wg_pa, wg_pb, wg_out, wg_ff1, wg_ff2))
        xs = x2

    loss_blk, dx, dfinal = _loss_head(xs, final_g.reshape(1, d), loss_target.reshape(n, d))
    loss = lax.psum(loss_blk[0, 0], ("x", "y", "c"))

    tok = lambda w: (lambda tk: pl.BlockSpec((tk, w), lambda i, j, k: (k, 0)))
    tok_i = lambda w: (lambda tk: pl.BlockSpec((tk, w), lambda i, j, k: (k, i)))
    tok_j = lambda w: (lambda tk: pl.BlockSpec((tk, w), lambda i, j, k: (k, j)))
    qin = weights[0][0].shape[-1]
    hq = weights[0][4].shape[-1]
    rq = d // N_CHIP
    slot_i = lambda r, cdim: pl.BlockSpec((None, r, cdim), lambda i, j, k: (i, 0, 0))
    slot_j = lambda r, cdim: pl.BlockSpec((None, r, cdim), lambda i, j, k: (j, 0, 0))
    all_slots = pl.BlockSpec((N_CHIP, rq, d), lambda i, j, k: (0, 0, 0))
    scatters = []

    def scatter_start(l, part, names, grads, after):
        lands = [lax.empty((3,) + g.shape[1:], g.dtype) for g in grads]
        send_sems, recv_sems, srcs, lands, tok_out = _split_start(f"scatter_start_{part}_{l}", "scatter", grads, lands,
                                                                  after)
        scatters.append((f"scatter_wait_{part}_{l}", l, names, send_sems, recv_sems, srcs, lands))
        return tok_out

    dmods, small = [None] * nl, [None] * nl
    for l in reversed(range(nl)):
        x0, h, proj, ya_in, yb_in, zc, ya, yb, merged, o, x1, h2, f, o2 = saved[l]
        wg_in, wg_pa, wg_pb, wg_out, wg_ff1, wg_ff2 = weights[l]
        do2, df, dx1, dmod_c, dg2 = _ffn_bwd(l, dx, x1, mod, g2, o2, f, wg_ff1, wg_ff2, t_len, nb)
        g_ff2 = _weight_grad(f"grad_w_ff2_{l}", f, do2, tok_i(hq), tok(d), hq, slot_i(hq, d), (hq, d), (N_CHIP, 1),
                             relu2=True)
        g_ff1 = _weight_grad(f"grad_w_ff1_{l}", h2, df, tok(d), tok_j(hq), d, slot_j(d, hq), (d, hq), (1, N_CHIP))
        token = scatter_start(l, "ffn", ["w_ff2", "w_ff1"], [g_ff2, g_ff1], token)
        mod = mod + token[0, 0]
        do, dya, dyb, dya_in, dyb_in, dproj, dmod_b = _merge_bwd(l, dx1, mod, o, ya, yb, proj, wg_pa, wg_pb, wg_out,
                                                                 t_len, nb)
        g_out = _weight_grad(f"grad_w_out_{l}", merged, do, tok(d), tok(d), rq, all_slots, (d, d), (1, 1))
        g_pa = _weight_grad(f"grad_w_pa_{l}", ya_in, dya, tok(d), tok(d), rq, all_slots, (d, d), (1, 1))
        g_pb = _weight_grad(f"grad_w_pb_{l}", yb_in, dyb, tok(d), tok(d), rq, all_slots, (d, d), (1, 1))
        token = scatter_start(l, "mid", ["w_out", "w_pa", "w_pb"], [g_out, g_pa, g_pb], token)
        lng = lng + token[0, 0]
        dproj, dws, dbst, dcw, vecs = _branches_bwd(l, proj, zc, dya_in, dyb_in, dproj, lng, lnb, a_ws, bst, cw,
                                                    blg, blb, t_len)
        g_in = _weight_grad(f"grad_w_in_{l}", h, dproj, tok(d), tok_j(qin), d, slot_j(d, qin), (d, qin), (1, N_CHIP))
        token = scatter_start(l, "in", ["w_in"], [g_in], token)
        mod = mod + token[0, 0]
        dx, dmod_a, dg1 = _in_proj_bwd(l, dproj, dx1, x0, mod, g1, wg_in, t_len, nb)
        dmods[l] = jnp.concatenate([dmod_a[:, 0:2], dmod_b[:, 2:3], dmod_c[:, 3:6]], axis=1)
        dcw = dcw.transpose(1, 0, 2).reshape(HALO, d)[:CONV_TAPS]
        small[l] = (dg1[0], vecs[0], vecs[1], dws, dbst.T, dcw, vecs[2], vecs[3], vecs[4], dg2[0])
    grad_x = dx.reshape(nb, t_len, d)

    dmod_mine = jnp.stack(dmods).reshape(nl * nb, 6 * d)
    dmod_all = _all_to_all(jnp.broadcast_to(dmod_mine[None], (N_DEV,) + dmod_mine.shape), "gather_dmod")
    dmod_all = dmod_all.reshape(N_DEV, nl, nb, 6 * d).transpose(1, 0, 2, 3).reshape(nl, N_DEV * nb, 6 * d)

    names = ["norm1_g", "a_ln_g", "a_ln_b", "a_ws", "a_bs", "b_conv_w", "b_conv_b", "b_ln_g", "b_ln_b", "norm2_g"]
    stacked = [jnp.stack([small[l][k] for l in range(nl)]) for k in range(len(names))]
    stacked[5] = jnp.pad(stacked[5], ((0, 0), (0, HALO - CONV_TAPS), (0, 0)))
    stacked.append(dfinal)
    part_shapes = [s.shape for s in stacked]
    packed = _pack(stacked)
    prow = packed.shape[0]
    pad_rows = (-prow) % (8 * N_DEV)
    packed = jnp.pad(packed, ((0, pad_rows), (0, 0)))
    srow = packed.shape[0] // N_DEV
    mine = _all_to_all(packed.reshape(N_DEV, srow, LANES), "reduce_small", reduce=True)
    total = _all_to_all(jnp.broadcast_to(mine[None], (N_DEV, srow, LANES)), "gather_small")
    total = total.reshape(N_DEV * srow, LANES)[:prow]

    half = dict.fromkeys(big)
    for name, l, group, send_sems, recv_sems, srcs, lands in scatters:
        srcs, lands = _split_wait(name, "scatter", send_sems, recv_sems, srcs, lands, total)
        for k, g_own, g_got in zip(group, srcs, lands):
            half[k] = _sum_partials(f"sum_{k}_{l}", g_own, g_got, myq, l, nl, half[k])
    sums = [half[k] for k in big]
    swap_send, swap_recv, sums, others, token = _split_start(
        "swap_start", "swap", sums, [lax.empty(s.shape, s.dtype) for s in sums], total)
    dmod_all = dmod_all + token[0, 0]
    g_w_ada, g_b_ada = _ada_backward(c_all, dmod_all, myq, cq)

    sg = dict(zip(names + ["final_g"], _unpack(total, part_shapes)))
    sg["b_conv_w"] = lax.dynamic_slice_in_dim(sg["b_conv_w"][:, :CONV_TAPS], myq[0] * cc, cc, axis=2).reshape(
        nl, CONV_TAPS, 1, cc)
    sg["final_g"] = sg["final_g"][0]
    sg["b_ada"] = g_b_ada.reshape(nl, N_CHIP * cq)
    small_names = ["b_ada", "norm1_g", "a_ln_g", "a_ln_b", "a_ws", "a_bs", "b_conv_w", "b_conv_b", "b_ln_g",
                   "b_ln_b", "norm2_g", "final_g"]
    given = dict(b_ada=(b_ada, m_b_ada, v_b_ada), norm1_g=(norm1_g, m_norm1_g, v_norm1_g),
                 a_ln_g=(a_ln_g, m_a_ln_g, v_a_ln_g), a_ln_b=(a_ln_b, m_a_ln_b, v_a_ln_b),
                 a_ws=(a_ws, m_a_ws, v_a_ws), a_bs=(a_bs, m_a_bs, v_a_bs),
                 b_conv_w=(b_conv_w, m_b_conv_w, v_b_conv_w), b_conv_b=(b_conv_b, m_b_conv_b, v_b_conv_b),
                 b_ln_g=(b_ln_g, m_b_ln_g, v_b_ln_g), b_ln_b=(b_ln_b, m_b_ln_b, v_b_ln_b),
                 norm2_g=(norm2_g, m_norm2_g, v_norm2_g), final_g=(final_g, m_final_g, v_final_g))

    def padded(a):
        rows = -(-a.size // (8 * LANES)) * 8
        return jnp.pad(a.reshape(-1), (0, rows * LANES - a.size)).reshape(rows, LANES)

    packs = [_pack([padded(given[k][j]) for k in small_names]) for j in range(3)]
    gpack = _pack([padded(sg[k].astype(F32)) for k in small_names])
    res_small = _adamw("adamw_small", packs[0], packs[1], packs[2], gpack)
    out = {}
    for j, kind in enumerate(["grad", "delta", "new_m", "new_v"]):
        r = 0
        for k in small_names:
            a = given[k][0]
            rows = -(-a.size // (8 * LANES)) * 8
            out[(kind, k)] = res_small[j][r:r + rows].reshape(-1)[:a.size].reshape(a.shape)
            r += rows

    res = _adamw("adamw_w_ada", w_ada.reshape(nl * d, cq), m_w_ada.reshape(nl * d, cq), v_w_ada.reshape(nl * d, cq),
                 g_w_ada.reshape(nl * d, cq))
    for kind, r in zip(["grad", "delta", "new_m", "new_v"], res):
        out[(kind, "w_ada")] = r.reshape(w_ada.shape)

    sums, others = _split_wait("swap_wait", "swap", swap_send, swap_recv, sums, others, res[0])
    for k, s_mine, s_other in zip(big, sums, others):
        w, m, v = ws_given[k]
        cols = w.shape[-1]
        res = _adamw(f"adamw_{k}", w.reshape(-1, cols), m.reshape(-1, cols), v.reshape(-1, cols), s_mine, s_other)
        for kind, r in zip(["grad", "delta", "new_m", "new_v"], res):
            out[(kind, k)] = r.reshape(w.shape)

    order = ["w_ada", "b_ada", "norm1_g", "w_in", "a_ln_g", "a_ln_b", "a_ws", "a_bs", "w_pa", "b_conv_w", "b_conv_b",
             "b_ln_g", "b_ln_b", "w_pb", "w_out", "norm2_g", "w_ff1", "w_ff2", "final_g"]
    return (loss, grad_x, *[out[("grad", k)] for k in order], *[out[("delta", k)] for k in order],
            *[out[("new_m", k)] for k in order], *[out[("new_v", k)] for k in order])
```

```python
import functools

import jax
import jax.numpy as jnp
from jax import lax
from jax.experimental import pallas as pl
from jax.experimental.pallas import tpu as pltpu

F32 = jnp.float32
MXU_DTYPE = jnp.bfloat16
ACT_DTYPE = jnp.bfloat16
WIRE_DTYPE = jnp.bfloat16

EPS = 1e-6
CHUNK = 128
HEADS = 8
CONV_TAPS = 31
HALO = 32
N_DEV = 8
N_CHIP = 4
ADAM_LR, ADAM_B1, ADAM_B2, ADAM_EPS, ADAM_WD, ADAM_STEP = 0.001, 0.9, 0.999, 1e-08, 0.01, 10

V7X_VMEM_BYTES = 64 * 1024 * 1024
VMEM_LIMIT = V7X_VMEM_BYTES * 7 // 8
TOKEN_TILE = 512
MATMUL_TILE = 2048
FFN_BWD_TILE = 256
CONV_ROWS = 64
TAP_GRAD_ROWS = 32
LANES = 128
MESH_ID = pl.DeviceIdType.MESH


def _params(sem=None):
    return pltpu.CompilerParams(dimension_semantics=sem, vmem_limit_bytes=VMEM_LIMIT)


def _resident(shape):
    return pl.BlockSpec(shape, lambda *_: (0,) * len(shape), pipeline_mode=pl.Buffered(1))


def _dot(a, b):
    return jnp.dot(a.astype(MXU_DTYPE), b.astype(MXU_DTYPE), preferred_element_type=F32)


def _dot_nt(a, b):
    return lax.dot_general(a.astype(MXU_DTYPE), b.astype(MXU_DTYPE), (((1,), (1,)), ((), ())),
                           preferred_element_type=F32)


def _dot_tn(a, b):
    return lax.dot_general(a.astype(MXU_DTYPE), b.astype(MXU_DTYPE), (((0,), (0,)), ((), ())),
                           preferred_element_type=F32)


def _colsum(a):
    return jnp.sum(a, axis=0, keepdims=True)


def _rowmean(a):
    return jnp.mean(a, axis=-1, keepdims=True)


def _sigmoid(a):
    return 1.0 / (1.0 + jnp.exp(-a))


def _modnorm_fwd(x, g, sc, sh):
    r = lax.rsqrt(_rowmean(x * x) + EPS)
    return (x * r) * (g * (1.0 + sc)) + sh


def _modnorm_bwd(x, dh, g, sc):
    r = lax.rsqrt(_rowmean(x * x) + EPS)
    xn = x * r
    dxn = dh * (g * (1.0 + sc))
    dx = r * (dxn - xn * _rowmean(dxn * xn))
    return dx, _colsum(dh), _colsum(dh * xn)


def _ln_stats(v):
    mu = _rowmean(v)
    vc = v - mu
    rstd = lax.rsqrt(_rowmean(vc * vc) + EPS)
    return vc * rstd, rstd


def _ln_bwd(dy, vhat, rstd, g):
    dvh = dy * g
    return rstd * (dvh - _rowmean(dvh) - vhat * _rowmean(dvh * vhat))


def _causal_mask():
    row = lax.broadcasted_iota(jnp.int32, (CHUNK, CHUNK), 0)
    col = lax.broadcasted_iota(jnp.int32, (CHUNK, CHUNK), 1)
    return row >= col


def _my_place():
    return lax.axis_index("x"), lax.axis_index("y"), lax.axis_index("c")


def _all_to_all(x, name, reduce=False):
    n, rows, cols = x.shape
    assert n == N_DEV

    def body(x_ref, o_ref, *scratch):
        if reduce:
            land, send_sems, recv_sems = scratch
        else:
            land = o_ref
            send_sems, recv_sems = scratch
        mx, my, mc = _my_place()
        me = 4 * mx + 2 * my + mc
        land[me] = x_ref[me]
        copies = []
        for k in range(1, N_DEV):
            px = (mx + ((k >> 2) & 1)) % 2
            py = (my + ((k >> 1) & 1)) % 2
            pc = (mc + (k & 1)) % 2
            peer = 4 * px + 2 * py + pc
            cp = pltpu.make_async_remote_copy(
                src_ref=x_ref.at[peer], dst_ref=land.at[me],
                send_sem=send_sems.at[k - 1], recv_sem=recv_sems.at[k - 1],
                device_id=(px, py, pc), device_id_type=MESH_ID)
            cp.start()
            copies.append(cp)
        for cp in copies:
            cp.wait()
        if reduce:
            acc = land[0]
            for s in range(1, N_DEV):
                acc = acc + land[s]
            o_ref[...] = acc

    scratch = [pltpu.SemaphoreType.DMA((N_DEV - 1,)), pltpu.SemaphoreType.DMA((N_DEV - 1,))]
    if reduce:
        scratch = [pltpu.VMEM((N_DEV, rows, cols), x.dtype)] + scratch
        out_shape = jax.ShapeDtypeStruct((rows, cols), x.dtype)
    else:
        out_shape = jax.ShapeDtypeStruct(x.shape, x.dtype)
    return pl.pallas_call(
        body, name=name, out_shape=out_shape,
        in_specs=[pl.BlockSpec(memory_space=pltpu.VMEM)],
        out_specs=pl.BlockSpec(memory_space=pltpu.VMEM),
        scratch_shapes=scratch,
        compiler_params=pltpu.CompilerParams(vmem_limit_bytes=VMEM_LIMIT),
    )(x)


def _other_chips(mx, my):
    return [(1 - mx, my), (mx, 1 - my), (1 - mx, 1 - my)]


HBM_SPEC = pl.BlockSpec(memory_space=pltpu.HBM)
SEM_SPEC = pl.BlockSpec(memory_space=pltpu.SEMAPHORE)
ANY_SPEC = pl.BlockSpec(memory_space=pl.ANY)
SPLIT_EFFECT = pltpu.SideEffectType.DATAFLOW_SIDE_EFFECTING


def _quarter_copies(mode, srcs, lands, send_sems, recv_sems):
    mx, my, mc = _my_place()
    myq = 2 * mx + my
    if mode == "swap":
        return [pltpu.make_async_remote_copy(
            src_ref=srcs[a], dst_ref=lands[a], send_sem=send_sems[a].at[0], recv_sem=recv_sems[a].at[0],
            device_id=(mx, my, 1 - mc), device_id_type=MESH_ID) for a in range(len(lands))]
    copies = []
    for a in range(len(lands)):
        for k, (px, py) in enumerate(_other_chips(mx, my)):
            if mode == "gather":
                src, dst = lands[a].at[myq], lands[a].at[myq]
            else:
                src, dst = srcs[a].at[2 * px + py], lands[a].at[k]
            copies.append(pltpu.make_async_remote_copy(
                src_ref=src, dst_ref=dst, send_sem=send_sems[a].at[k], recv_sem=recv_sems[a].at[k],
                device_id=(px, py, mc), device_id_type=MESH_ID))
    return copies


def _split_start(name, mode, srcs, lands, after):
    ns, n = len(srcs), len(lands)

    def body(*refs):
        outs = refs[ns + n + 1:]
        for cp in _quarter_copies(mode, refs[:ns], refs[ns:ns + n], outs[:n], outs[n:2 * n]):
            cp.start()
        token = outs[-1]
        token[...] = jnp.zeros_like(token)

    arrays = list(srcs) + list(lands)
    per_array = 1 if mode == "swap" else 3
    res = pl.pallas_call(
        body, name=name,
        out_shape=[pltpu.SemaphoreType.DMA((per_array,))] * (2 * n) + [pltpu.HBM(x.shape, x.dtype) for x in arrays]
        + [jax.ShapeDtypeStruct((8, LANES), F32)],
        in_specs=[HBM_SPEC] * (ns + n) + [ANY_SPEC],
        out_specs=[SEM_SPEC] * (2 * n) + [HBM_SPEC] * (ns + n) + [pl.BlockSpec(memory_space=pltpu.VMEM)],
        input_output_aliases={i: 2 * n + i for i in range(ns + n)},
        compiler_params=pltpu.CompilerParams(has_side_effects=SPLIT_EFFECT),
    )(*[pltpu.with_memory_space_constraint(x, pltpu.HBM) for x in arrays], after)
    return res[:n], res[n:2 * n], res[2 * n:2 * n + ns], res[2 * n + ns:2 * n + ns + n], res[-1]


def _split_wait(name, mode, send_sems, recv_sems, srcs, lands, after):
    ns, n = len(srcs), len(lands)

    def body(*refs):
        sems = refs[ns + n:ns + 3 * n]
        for cp in _quarter_copies(mode, refs[:ns], refs[ns:ns + n], sems[:n], sems[n:]):
            cp.wait_send()
            cp.wait_recv()

    arrays = list(srcs) + list(lands)
    res = pl.pallas_call(
        body, name=name,
        out_shape=[pltpu.HBM(x.shape, x.dtype) for x in arrays],
        in_specs=[HBM_SPEC] * (ns + n) + [SEM_SPEC] * (2 * n) + [ANY_SPEC],
        out_specs=[HBM_SPEC] * (ns + n),
        input_output_aliases={i: i for i in range(ns + n)},
        compiler_params=pltpu.CompilerParams(has_side_effects=SPLIT_EFFECT),
    )(*arrays, *send_sems, *recv_sems, after)
    return res[:ns], res[ns:]


def _ada_forward(c_all, w_ada, b_ada3, myq):
    nl, d, cq = w_ada.shape
    nb = c_all.shape[0]

    def body(q_ref, c_ref, w_ref, b_ref, o_ref):
        c = c_ref[...]
        act = c * _sigmoid(c)
        o_ref[...] = _dot(act, w_ref[...]) + b_ref[...]

    return pl.pallas_call(
        body, name="ada_forward",
        out_shape=jax.ShapeDtypeStruct((nl, nb, cq), F32),
        grid_spec=pltpu.PrefetchScalarGridSpec(
            num_scalar_prefetch=1, grid=(nl,),
            in_specs=[pl.BlockSpec((nb, d), lambda l, q: (0, 0)),
                      pl.BlockSpec((None, d, cq), lambda l, q: (l, 0, 0)),
                      pl.BlockSpec((None, 1, cq), lambda l, q: (l, 0, q[0]))],
            out_specs=pl.BlockSpec((None, nb, cq), lambda l, q: (l, 0, 0))),
        compiler_params=_params(("arbitrary",)),
    )(myq, c_all, w_ada, b_ada3)


def _ada_backward(c_all, dmod_all, myq, cq):
    nb, d = c_all.shape
    nl = dmod_all.shape[0]
    full = dmod_all.shape[2]

    def body(q_ref, c_ref, dq_ref, dall_ref, gw_ref, gb_ref):
        c = c_ref[...]
        act = c * _sigmoid(c)
        gw_ref[...] = _dot_tn(act, dq_ref[...])
        gb_ref[...] = _colsum(dall_ref[...])

    return pl.pallas_call(
        body, name="ada_backward",
        out_shape=[jax.ShapeDtypeStruct((nl, d, cq), F32), jax.ShapeDtypeStruct((nl, 1, full), F32)],
        grid_spec=pltpu.PrefetchScalarGridSpec(
            num_scalar_prefetch=1, grid=(nl,),
            in_specs=[pl.BlockSpec((nb, d), lambda l, q: (0, 0)),
                      pl.BlockSpec((None, nb, cq), lambda l, q: (l, 0, q[0])),
                      pl.BlockSpec((None, nb, full), lambda l, q: (l, 0, 0))],
            out_specs=[pl.BlockSpec((None, d, cq), lambda l, q: (l, 0, 0)),
                       pl.BlockSpec((None, 1, full), lambda l, q: (l, 0, 0))]),
        compiler_params=_params(("arbitrary",)),
    )(myq, c_all, dmod_all, dmod_all)


def _in_proj(l, x, mod, g1, wg_in, t_len):
    n, d = x.shape
    tm = min(TOKEN_TILE, t_len)
    tpb = t_len // tm
    qc = wg_in.shape[-1]

    def body(x_ref, mod_ref, g_ref, w_ref, h_ref, proj_ref):
        h = _modnorm_fwd(x_ref[...], g_ref[...], mod_ref[1:2, :], mod_ref[0:1, :]).astype(MXU_DTYPE)
        h_ref[...] = h.astype(ACT_DTYPE)
        for q in range(N_CHIP):
            proj_ref[:, q * qc:(q + 1) * qc] = jnp.dot(h, w_ref[q], preferred_element_type=F32).astype(ACT_DTYPE)

    return pl.pallas_call(
        body, name=f"in_proj_{l}",
        out_shape=[jax.ShapeDtypeStruct((n, d), ACT_DTYPE), jax.ShapeDtypeStruct((n, N_CHIP * qc), ACT_DTYPE)],
        grid=(n // tm,),
        in_specs=[pl.BlockSpec((tm, d), lambda i: (i, 0)),
                  pl.BlockSpec((None, None, 8, d), lambda i: (l, i // tpb, 0, 0)),
                  pl.BlockSpec((None, 1, d), lambda i: (l, 0, 0)),
                  _resident((N_CHIP, d, qc))],
        out_specs=[pl.BlockSpec((tm, d), lambda i: (i, 0)),
                   pl.BlockSpec((tm, N_CHIP * qc), lambda i: (i, 0))],
        compiler_params=_params(("arbitrary",)),
    )(x, mod, g1, wg_in)


def _masked_ws(ws_ref, wm_s):
    mask = _causal_mask()
    for h in range(HEADS):
        wm_s[h] = jnp.where(mask, ws_ref[h], 0.0).astype(MXU_DTYPE)


def _fill_z(i, tpb, a_ref, g_ref, ah_ref, gh_ref, zext):
    ah = ah_ref[...].astype(F32)
    gh = gh_ref[...].astype(F32)
    keep = jnp.where(i % tpb == 0, 0.0, 1.0)
    _put_lanes(zext, slice(0, HALO), ah * _sigmoid(gh) * keep)


def _put_lanes(dst3, rows, value):
    for lc in range(value.shape[-1] // LANES):
        dst3[lc, rows, :] = value[:, lc * LANES:(lc + 1) * LANES]


def _tap_windows(src3, lc, base, rows, flip):
    offs = {k: (CONV_TAPS - 1 - k) if flip else (k + 2) for k in range(CONV_TAPS)}
    for r in range(8):
        taps = [k for k in offs if offs[k] % 8 == r]
        lo = min(offs[k] for k in taps)
        hi = max(offs[k] for k in taps)
        win = src3[lc, pl.ds(base + lo, hi - lo + rows), :]
        for k in taps:
            yield k, win[offs[k] - lo:offs[k] - lo + rows]


def _conv_taps(src3, w3_ref, dst3, lc, nrows, flip):
    for b in range(nrows // CONV_ROWS):
        acc = jnp.zeros((CONV_ROWS, LANES), F32)
        for k, win in _tap_windows(src3, lc, b * CONV_ROWS, CONV_ROWS, flip):
            acc = acc + win * w3_ref[lc, k:k + 1, :]
        dst3[lc, b * CONV_ROWS:(b + 1) * CONV_ROWS, :] = acc


def _branches_fwd(l, proj, lng, lnb, ws, bst, cw, cb, blg, blb, t_len):
    n = proj.shape[0]
    d = lng.shape[-1]
    tm = min(TOKEN_TILE, t_len)
    tpb = t_len // tm
    per = tm // HALO
    nchunk = tm // CHUNK

    def body(u_ref, v_ref, a_ref, g_ref, ah_ref, gh_ref, lng_ref, lnb_ref, ws_ref, bst_ref, cw_ref, cb_ref,
             blg_ref, blb_ref, ya_ref, yb_ref, zc_ref, wm_s, zext, zc3):
        i = pl.program_id(0)
        _masked_ws(ws_ref, wm_s)
        _fill_z(i, tpb, a_ref, g_ref, ah_ref, gh_ref, zext)

        def chunk(c, carry):
            r0 = pl.multiple_of(c * CHUNK, CHUNK)
            rows = pl.ds(r0, CHUNK)
            vhat, _ = _ln_stats(v_ref[rows, :].astype(F32))
            vn = (vhat * lng_ref[...] + lnb_ref[...]).astype(MXU_DTYPE)
            u = u_ref[rows, :].astype(F32)
            for h in range(HEADS):
                cols = slice(h * CHUNK, (h + 1) * CHUNK)
                s = jnp.dot(wm_s[h], vn[:, cols], preferred_element_type=F32) + bst_ref[:, h:h + 1]
                ya_ref[rows, cols] = (u[:, cols] * s).astype(ACT_DTYPE)
            a = a_ref[rows, :].astype(F32)
            g = g_ref[rows, :].astype(F32)
            _put_lanes(zext, pl.ds(HALO + r0, CHUNK), a * _sigmoid(g))
            return carry

        lax.fori_loop(0, nchunk, chunk, 0)

        def lane_chunk(lc, carry):
            _conv_taps(zext, cw_ref, zc3, lc, tm, flip=False)
            return carry

        lax.fori_loop(0, d // LANES, lane_chunk, 0)

        def chunk2(c, carry):
            r0 = pl.multiple_of(c * CHUNK, CHUNK)
            rows = pl.ds(r0, CHUNK)
            for lc in range(d // LANES):
                lanes = slice(lc * LANES, (lc + 1) * LANES)
                zc_ref[rows, lanes] = (zc3[lc, rows, :] + cb_ref[:, lanes]).astype(ACT_DTYPE)
            zhat, _ = _ln_stats(zc_ref[rows, :].astype(F32))
            zn = zhat * blg_ref[...] + blb_ref[...]
            yb_ref[rows, :] = (zn * _sigmoid(zn)).astype(ACT_DTYPE)
            return carry

        lax.fori_loop(0, nchunk, chunk2, 0)

    col = lambda k: pl.BlockSpec((tm, d), lambda i: (i, k))
    halo = lambda k: pl.BlockSpec((HALO, d), lambda i: (jnp.maximum(i * per - 1, 0), k))
    vec = pl.BlockSpec((None, 1, d), lambda i: (l, 0, 0))
    out = pl.BlockSpec((tm, d), lambda i: (i, 0))
    return pl.pallas_call(
        body, name=f"branches_fwd_{l}",
        out_shape=[jax.ShapeDtypeStruct((n, d), ACT_DTYPE)] * 3,
        grid=(n // tm,),
        in_specs=[col(0), col(1), col(2), col(3), halo(2), halo(3), vec, vec,
                  pl.BlockSpec((None, HEADS, CHUNK, CHUNK), lambda i: (l, 0, 0, 0)),
                  pl.BlockSpec((None, CHUNK, HEADS), lambda i: (l, 0, 0)),
                  pl.BlockSpec((None, d // LANES, HALO, LANES), lambda i: (l, 0, 0, 0)), vec, vec, vec],
        out_specs=[out, out, out],
        scratch_shapes=[pltpu.VMEM((HEADS, CHUNK, CHUNK), MXU_DTYPE), pltpu.VMEM((d // LANES, HALO + tm, LANES), F32),
                        pltpu.VMEM((d // LANES, tm, LANES), F32)],
        compiler_params=_params(("arbitrary",)),
    )(proj, proj, proj, proj, proj, proj, lng, lnb, ws, bst, cw, cb, blg, blb)


def _merge_out(l, x, mod, proj, ya_in, yb_in, wg_pa, wg_pb, wg_out, t_len):
    n, d = x.shape
    tm = min(TOKEN_TILE, t_len)
    tpb = t_len // tm
    rq = d // N_CHIP

    def body(x_ref, mod_ref, ga_ref, gb_ref, yai_ref, ybi_ref, wpa_ref, wpb_ref, wo_ref,
             ya_ref, yb_ref, mg_ref, o_ref, x1_ref):
        wpa = wpa_ref[...].reshape(d, d)
        wpb = wpb_ref[...].reshape(d, d)
        wo = wo_ref[...].reshape(d, d)
        ya = jnp.dot(yai_ref[...].astype(MXU_DTYPE), wpa, preferred_element_type=F32)
        yb = jnp.dot(ybi_ref[...].astype(MXU_DTYPE), wpb, preferred_element_type=F32)
        merged = _sigmoid(ga_ref[...].astype(F32)) * ya + _sigmoid(gb_ref[...].astype(F32)) * yb
        o = _dot(merged, wo)
        ya_ref[...] = ya.astype(ACT_DTYPE)
        yb_ref[...] = yb.astype(ACT_DTYPE)
        mg_ref[...] = merged.astype(ACT_DTYPE)
        o_ref[...] = o.astype(ACT_DTYPE)
        x1_ref[...] = x_ref[...] + mod_ref[2:3, :] * o

    tile = pl.BlockSpec((tm, d), lambda i: (i, 0))
    wspec = pl.BlockSpec((N_CHIP, rq, d), lambda i: (0, 0, 0))
    return pl.pallas_call(
        body, name=f"merge_out_{l}",
        out_shape=[jax.ShapeDtypeStruct((n, d), ACT_DTYPE)] * 4 + [jax.ShapeDtypeStruct((n, d), F32)],
        grid=(n // tm,),
        in_specs=[tile, pl.BlockSpec((None, None, 8, d), lambda i: (l, i // tpb, 0, 0)),
                  pl.BlockSpec((tm, d), lambda i: (i, 4)), pl.BlockSpec((tm, d), lambda i: (i, 5)),
                  tile, tile, wspec, wspec, wspec],
        out_specs=[tile] * 5,
        compiler_params=_params(("arbitrary",)),
    )(x, mod, proj, proj, ya_in, yb_in, wg_pa, wg_pb, wg_out)


def _ffn_fwd(l, x1, mod, g2, wg_ff1, wg_ff2, t_len):
    n, d = x1.shape
    tm = min(TOKEN_TILE, t_len)
    tpb = t_len // tm
    hq = wg_ff1.shape[-1]
    hid = N_CHIP * hq

    def body(x_ref, mod_ref, g_ref, w1_ref, w2_ref, h_ref, f_ref, o2_ref, x2_ref, a2_s):
        h = _modnorm_fwd(x_ref[...], g_ref[...], mod_ref[4:5, :], mod_ref[3:4, :]).astype(MXU_DTYPE)
        h_ref[...] = h.astype(ACT_DTYPE)
        for q in range(N_CHIP):
            cols = slice(q * hq, (q + 1) * hq)
            f = jnp.dot(h, w1_ref[q], preferred_element_type=F32)
            f_ref[:, cols] = f.astype(ACT_DTYPE)
            a2_s[:, cols] = jnp.square(jnp.maximum(f, 0.0)).astype(MXU_DTYPE)
        o2 = jnp.dot(a2_s[...], w2_ref[...].reshape(hid, d), preferred_element_type=F32)
        o2_ref[...] = o2.astype(ACT_DTYPE)
        x2_ref[...] = x_ref[...] + mod_ref[5:6, :] * o2

    tile = pl.BlockSpec((tm, d), lambda i: (i, 0))
    return pl.pallas_call(
        body, name=f"ffn_fwd_{l}",
        out_shape=[jax.ShapeDtypeStruct((n, d), ACT_DTYPE), jax.ShapeDtypeStruct((n, hid), ACT_DTYPE),
                   jax.ShapeDtypeStruct((n, d), ACT_DTYPE), jax.ShapeDtypeStruct((n, d), F32)],
        grid=(n // tm,),
        in_specs=[tile, pl.BlockSpec((None, None, 8, d), lambda i: (l, i // tpb, 0, 0)),
                  pl.BlockSpec((None, 1, d), lambda i: (l, 0, 0)),
                  _resident((N_CHIP, d, hq)), _resident((N_CHIP, hq, d))],
        out_specs=[tile, pl.BlockSpec((tm, hid), lambda i: (i, 0)), tile, tile],
        scratch_shapes=[pltpu.VMEM((tm, hid), MXU_DTYPE)],
        compiler_params=_params(("arbitrary",)),
    )(x1, mod, g2, wg_ff1, wg_ff2)


def _loss_head(x, final_g, target):
    n, d = x.shape
    tm = min(TOKEN_TILE, n)

    def body(x_ref, g_ref, t_ref, loss_ref, dx_ref, dg_ref):
        @pl.when(pl.program_id(0) == 0)
        def _():
            loss_ref[...] = jnp.zeros_like(loss_ref)
            dg_ref[...] = jnp.zeros_like(dg_ref)

        x_t = x_ref[...]
        g = g_ref[...]
        r = lax.rsqrt(_rowmean(x_t * x_t) + EPS)
        xn = x_t * r
        e = xn * g - t_ref[...]
        loss_ref[...] += jnp.sum(e * e) * (0.5 / d)
        dy = e * (1.0 / d)
        dxn = dy * g
        dx_ref[...] = r * (dxn - xn * _rowmean(dxn * xn))
        dg_ref[0:1, :] += _colsum(dy * xn)

    tile = pl.BlockSpec((tm, d), lambda i: (i, 0))
    return pl.pallas_call(
        body, name="loss_head",
        out_shape=[jax.ShapeDtypeStruct((8, LANES), F32), jax.ShapeDtypeStruct((n, d), F32),
                   jax.ShapeDtypeStruct((8, d), F32)],
        grid=(n // tm,),
        in_specs=[tile, pl.BlockSpec((1, d), lambda i: (0, 0)), tile],
        out_specs=[pl.BlockSpec((8, LANES), lambda i: (0, 0)), tile, pl.BlockSpec((8, d), lambda i: (0, 0))],
        compiler_params=_params(("arbitrary",)),
    )(x, final_g, target)


def _norm_tail(i, tpb, x_ref, dxin_ref, dh, g_ref, sc, dx_ref, dmod_ref, dg_ref, row_sh, row_sc):
    dxm, dsh, q = _modnorm_bwd(x_ref[...], dh, g_ref[...], sc)
    dx_ref[...] = dxin_ref[...] + dxm
    dmod_ref[row_sh:row_sh + 1, :] += dsh
    dmod_ref[row_sc:row_sc + 1, :] += g_ref[...] * q
    dg_ref[0:1, :] += (1.0 + sc) * q


def _ffn_bwd(l, dx2, x1, mod, g2, o2, f, wg_ff1, wg_ff2, t_len, nb):
    n, d = dx2.shape
    tm = min(FFN_BWD_TILE, t_len)
    tpb = t_len // tm
    hq = wg_ff1.shape[-1]
    hid = N_CHIP * hq

    def body(dx2_ref, x1_ref, mod_ref, g_ref, o2_ref, f_ref, w1_ref, w2_ref,
             do2_ref, df_ref, dx1_ref, dmod_ref, dg_ref):
        i = pl.program_id(0)

        @pl.when(i == 0)
        def _():
            dg_ref[...] = jnp.zeros_like(dg_ref)

        @pl.when(i % tpb == 0)
        def _():
            dmod_ref[...] = jnp.zeros_like(dmod_ref)

        dx2_t = dx2_ref[...]
        dmod_ref[5:6, :] += _colsum(dx2_t * o2_ref[...].astype(F32))
        do2 = (dx2_t * mod_ref[5:6, :]).astype(MXU_DTYPE)
        do2_ref[...] = do2.astype(ACT_DTYPE)
        dh = jnp.zeros((tm, d), F32)
        for q in range(N_CHIP):
            cols = slice(q * hq, (q + 1) * hq)
            da2 = _dot_nt(do2, w2_ref[q])
            df = (da2 * (2.0 * jnp.maximum(f_ref[:, cols].astype(F32), 0.0))).astype(MXU_DTYPE)
            df_ref[:, cols] = df.astype(ACT_DTYPE)
            dh = dh + _dot_nt(df, w1_ref[q])
        _norm_tail(i, tpb, x1_ref, dx2_ref, dh, g_ref, mod_ref[4:5, :], dx1_ref, dmod_ref, dg_ref, 3, 4)

    tile = pl.BlockSpec((tm, d), lambda i: (i, 0))
    wide = pl.BlockSpec((tm, hid), lambda i: (i, 0))
    return pl.pallas_call(
        body, name=f"ffn_bwd_{l}",
        out_shape=[jax.ShapeDtypeStruct((n, d), ACT_DTYPE), jax.ShapeDtypeStruct((n, hid), ACT_DTYPE),
                   jax.ShapeDtypeStruct((n, d), F32), jax.ShapeDtypeStruct((nb, 8, d), F32),
                   jax.ShapeDtypeStruct((8, d), F32)],
        grid=(n // tm,),
        in_specs=[tile, tile, pl.BlockSpec((None, None, 8, d), lambda i: (l, i // tpb, 0, 0)),
                  pl.BlockSpec((None, 1, d), lambda i: (l, 0, 0)), tile, wide,
                  _resident((N_CHIP, d, hq)), _resident((N_CHIP, hq, d))],
        out_specs=[tile, wide, tile, pl.BlockSpec((None, 8, d), lambda i: (i // tpb, 0, 0)),
                   pl.BlockSpec((8, d), lambda i: (0, 0))],
        compiler_params=_params(("arbitrary",)),
    )(dx2, x1, mod, g2, o2, f, wg_ff1, wg_ff2)


def _merge_bwd(l, dx1, mod, o, ya, yb, proj, wg_pa, wg_pb, wg_out, t_len, nb):
    n, d = dx1.shape
    tm = min(TOKEN_TILE, t_len)
    tpb = t_len // tm
    rq = d // N_CHIP

    def body(dx_ref, mod_ref, o_ref, ya_ref, yb_ref, ga_ref, gb_ref, wpa_ref, wpb_ref, wo_ref,
             do_ref, dya_ref, dyb_ref, dyai_ref, dybi_ref, dproj_ref, dmod_ref):
        i = pl.program_id(0)

        @pl.when(i % tpb == 0)
        def _():
            dmod_ref[...] = jnp.zeros_like(dmod_ref)

        dx = dx_ref[...]
        dmod_ref[2:3, :] += _colsum(dx * o_ref[...].astype(F32))
        do = (dx * mod_ref[2:3, :]).astype(MXU_DTYPE)
        do_ref[...] = do.astype(ACT_DTYPE)
        dm = _dot_nt(do, wo_ref[...].reshape(d, d))
        sa = _sigmoid(ga_ref[...].astype(F32))
        sb = _sigmoid(gb_ref[...].astype(F32))
        dya = (dm * sa).astype(MXU_DTYPE)
        dyb = (dm * sb).astype(MXU_DTYPE)
        dya_ref[...] = dya.astype(ACT_DTYPE)
        dyb_ref[...] = dyb.astype(ACT_DTYPE)
        dproj_ref[:, 0:d] = (dm * ya_ref[...].astype(F32) * sa * (1.0 - sa)).astype(ACT_DTYPE)
        dproj_ref[:, d:2 * d] = (dm * yb_ref[...].astype(F32) * sb * (1.0 - sb)).astype(ACT_DTYPE)
        dyai_ref[...] = _dot_nt(dya, wpa_ref[...].reshape(d, d)).astype(ACT_DTYPE)
        dybi_ref[...] = _dot_nt(dyb, wpb_ref[...].reshape(d, d)).astype(ACT_DTYPE)

    tile = pl.BlockSpec((tm, d), lambda i: (i, 0))
    wspec = pl.BlockSpec((N_CHIP, rq, d), lambda i: (0, 0, 0))
    return pl.pallas_call(
        body, name=f"merge_bwd_{l}",
        out_shape=[jax.ShapeDtypeStruct((n, d), ACT_DTYPE)] * 5
        + [jax.ShapeDtypeStruct((n, 6 * d), ACT_DTYPE), jax.ShapeDtypeStruct((nb, 8, d), F32)],
        grid=(n // tm,),
        in_specs=[tile, pl.BlockSpec((None, None, 8, d), lambda i: (l, i // tpb, 0, 0)), tile, tile, tile,
                  pl.BlockSpec((tm, d), lambda i: (i, 4)), pl.BlockSpec((tm, d), lambda i: (i, 5)),
                  wspec, wspec, wspec],
        out_specs=[tile] * 5 + [pl.BlockSpec((tm, 2 * d), lambda i: (i, 2)),
                                pl.BlockSpec((None, 8, d), lambda i: (i // tpb, 0, 0))],
        compiler_params=_params(("arbitrary",)),
    )(dx1, mod, o, ya, yb, proj, proj, wg_pa, wg_pb, wg_out)


def _branches_bwd(l, proj, zc, dya_in, dyb_in, dproj, lng, lnb, ws, bst, cw, blg, blb, t_len):
    n = proj.shape[0]
    d = lng.shape[-1]
    tm = min(TOKEN_TILE, t_len)
    tpb = t_len // tm
    per = tm // HALO
    nchunk = tm // CHUNK
    ntile = n // tm

    def body(u_ref, v_ref, a_ref, g_ref, ah_ref, gh_ref, zc_ref, zcn_ref, dya_ref, dyb_ref, dybn_ref, dproj_in,
             lng_ref, lnb_ref, ws_ref, bst_ref, cw_ref, blg_ref, blb_ref,
             dproj_ref, dws_ref, dbst_ref, dcw_ref, vec_ref, wm_s, zext, dzext, dz3, dvn_s):
        i = pl.program_id(0)

        @pl.when(i == 0)
        def _():
            dws_ref[...] = jnp.zeros_like(dws_ref)
            dbst_ref[...] = jnp.zeros_like(dbst_ref)
            dcw_ref[...] = jnp.zeros_like(dcw_ref)
            vec_ref[...] = jnp.zeros_like(vec_ref)

        _masked_ws(ws_ref, wm_s)
        _fill_z(i, tpb, a_ref, g_ref, ah_ref, gh_ref, zext)

        def conv_ln_bwd(zc_t, dyb_t):
            zhat, rstd = _ln_stats(zc_t)
            zn = zhat * blg_ref[...] + blb_ref[...]
            sg = _sigmoid(zn)
            dzn = dyb_t * (sg * (1.0 + zn * (1.0 - sg)))
            return _ln_bwd(dzn, zhat, rstd, blg_ref[...]), _colsum(dzn * zhat), _colsum(dzn)

        def chunk(c, carry):
            r0 = pl.multiple_of(c * CHUNK, CHUNK)
            rows = pl.ds(r0, CHUNK)
            vhat, rstd = _ln_stats(v_ref[rows, :].astype(F32))
            vn = (vhat * lng_ref[...] + lnb_ref[...]).astype(MXU_DTYPE)
            u = u_ref[rows, :].astype(F32)
            dya = dya_ref[rows, :].astype(F32)
            for h in range(HEADS):
                cols = slice(h * CHUNK, (h + 1) * CHUNK)
                s = jnp.dot(wm_s[h], vn[:, cols], preferred_element_type=F32) + bst_ref[:, h:h + 1]
                dproj_ref[rows, cols] = (dya[:, cols] * s).astype(ACT_DTYPE)
                ds = dya[:, cols] * u[:, cols]
                dvn_s[:, cols] = _dot_tn(wm_s[h], ds)
                dws_ref[h] += _dot_nt(ds, vn[:, cols])
                dbst_ref[:, h:h + 1] += jnp.sum(ds, axis=1, keepdims=True)
            dvn = dvn_s[...]
            dproj_ref[rows, d:2 * d] = _ln_bwd(dvn, vhat, rstd, lng_ref[...]).astype(ACT_DTYPE)
            vec_ref[0:1, :] += _colsum(dvn * vhat)
            vec_ref[1:2, :] += _colsum(dvn)
            a = a_ref[rows, :].astype(F32)
            g = g_ref[rows, :].astype(F32)
            _put_lanes(zext, pl.ds(HALO + r0, CHUNK), a * _sigmoid(g))
            dzc, dblg, dblb = conv_ln_bwd(zc_ref[rows, :].astype(F32), dyb_ref[rows, :].astype(F32))
            _put_lanes(dzext, rows, dzc)
            vec_ref[2:3, :] += _colsum(dzc)
            vec_ref[3:4, :] += dblg
            vec_ref[4:5, :] += dblb
            return carry

        lax.fori_loop(0, nchunk, chunk, 0)

        dzc_next, _, _ = conv_ln_bwd(zcn_ref[...].astype(F32), dybn_ref[...].astype(F32))
        _put_lanes(dzext, slice(tm, tm + HALO), dzc_next * jnp.where(i % tpb == tpb - 1, 0.0, 1.0))

        def lane_chunk_dz(lc, carry):
            _conv_taps(dzext, cw_ref, dz3, lc, tm, flip=True)
            return carry

        lax.fori_loop(0, d // LANES, lane_chunk_dz, 0)

        def lane_chunk(lc, carry):
            accs = [jnp.zeros((8, LANES), F32) for _ in range(CONV_TAPS)]
            for b in range(tm // TAP_GRAD_ROWS):
                dzc = dzext[lc, b * TAP_GRAD_ROWS:(b + 1) * TAP_GRAD_ROWS, :]
                for k, win in _tap_windows(zext, lc, b * TAP_GRAD_ROWS, TAP_GRAD_ROWS, flip=False):
                    prod = dzc * win
                    part = prod[0:8]
                    for e in range(1, TAP_GRAD_ROWS // 8):
                        part = part + prod[8 * e:8 * e + 8]
                    accs[k] = accs[k] + part
            for k in range(CONV_TAPS):
                dcw_ref[lc, k:k + 1, :] += _colsum(accs[k])
            return carry

        lax.fori_loop(0, d // LANES, lane_chunk, 0)

        def glu_bwd(c, carry):
            r0 = pl.multiple_of(c * CHUNK, CHUNK)
            rows = pl.ds(r0, CHUNK)
            for lc in range(d // LANES):
                lanes = slice(lc * LANES, (lc + 1) * LANES)
                dz = dz3[lc, rows, :]
                a = a_ref[rows, lanes].astype(F32)
                sg = _sigmoid(g_ref[rows, lanes].astype(F32))
                dproj_ref[rows, 2 * d + lc * LANES:2 * d + (lc + 1) * LANES] = (dz * sg).astype(ACT_DTYPE)
                dproj_ref[rows, 3 * d + lc * LANES:3 * d + (lc + 1) * LANES] = (
                    dz * a * sg * (1.0 - sg)).astype(ACT_DTYPE)
            return carry

        lax.fori_loop(0, nchunk, glu_bwd, 0)

        @pl.when(i == ntile - 1)
        def _():
            mask = _causal_mask()
            for h in range(HEADS):
                dws_ref[h] = jnp.where(mask, dws_ref[h], 0.0)

    col = lambda k: pl.BlockSpec((tm, d), lambda i: (i, k))
    tile = pl.BlockSpec((tm, d), lambda i: (i, 0))
    before = lambda k: pl.BlockSpec((HALO, d), lambda i: (jnp.maximum(i * per - 1, 0), k))
    after = pl.BlockSpec((HALO, d), lambda i: (jnp.minimum((i + 1) * per, n // HALO - 1), 0))
    vec = pl.BlockSpec((None, 1, d), lambda i: (l, 0, 0))
    const2 = lambda r, c: pl.BlockSpec((r, c), lambda i: (0, 0))
    return pl.pallas_call(
        body, name=f"branches_bwd_{l}",
        out_shape=[jax.ShapeDtypeStruct((n, 6 * d), ACT_DTYPE), jax.ShapeDtypeStruct((HEADS, CHUNK, CHUNK), F32),
                   jax.ShapeDtypeStruct((CHUNK, HEADS), F32), jax.ShapeDtypeStruct((d // LANES, HALO, LANES), F32),
                   jax.ShapeDtypeStruct((8, d), F32)],
        grid=(ntile,),
        in_specs=[col(0), col(1), col(2), col(3), before(2), before(3), tile, after, tile, tile, after,
                  pl.BlockSpec(memory_space=pl.ANY), vec, vec,
                  pl.BlockSpec((None, HEADS, CHUNK, CHUNK), lambda i: (l, 0, 0, 0)),
                  pl.BlockSpec((None, CHUNK, HEADS), lambda i: (l, 0, 0)),
                  pl.BlockSpec((None, d // LANES, HALO, LANES), lambda i: (l, 0, 0, 0)), vec, vec],
        out_specs=[pl.BlockSpec((tm, 4 * d), lambda i: (i, 0)),
                   pl.BlockSpec((HEADS, CHUNK, CHUNK), lambda i: (0, 0, 0)),
                   const2(CHUNK, HEADS), pl.BlockSpec((d // LANES, HALO, LANES), lambda i: (0, 0, 0)), const2(8, d)],
        scratch_shapes=[pltpu.VMEM((HEADS, CHUNK, CHUNK), MXU_DTYPE),
                        pltpu.VMEM((d // LANES, HALO + tm, LANES), F32),
                        pltpu.VMEM((d // LANES, tm + HALO, LANES), F32),
                        pltpu.VMEM((d // LANES, tm, LANES), F32), pltpu.VMEM((CHUNK, d), F32)],
        input_output_aliases={11: 0},
        compiler_params=_params(("arbitrary",)),
    )(proj, proj, proj, proj, proj, proj, zc, zc, dya_in, dyb_in, dyb_in, dproj, lng, lnb, ws, bst, cw, blg, blb)


def _in_proj_bwd(l, dproj, dx1, x, mod, g1, wg_in, t_len, nb):
    n, d = x.shape
    tm = min(TOKEN_TILE, t_len)
    tpb = t_len // tm
    qc = wg_in.shape[-1]

    def body(dp_ref, dx1_ref, x_ref, mod_ref, g_ref, w_ref, dx_ref, dmod_ref, dg_ref):
        i = pl.program_id(0)

        @pl.when(i == 0)
        def _():
            dg_ref[...] = jnp.zeros_like(dg_ref)

        @pl.when(i % tpb == 0)
        def _():
            dmod_ref[...] = jnp.zeros_like(dmod_ref)

        dh = jnp.zeros((tm, d), F32)
        for q in range(N_CHIP):
            dh = dh + _dot_nt(dp_ref[:, q * qc:(q + 1) * qc], w_ref[q])
        _norm_tail(i, tpb, x_ref, dx1_ref, dh, g_ref, mod_ref[1:2, :], dx_ref, dmod_ref, dg_ref, 0, 1)

    tile = pl.BlockSpec((tm, d), lambda i: (i, 0))
    return pl.pallas_call(
        body, name=f"in_proj_bwd_{l}",
        out_shape=[jax.ShapeDtypeStruct((n, d), F32), jax.ShapeDtypeStruct((nb, 8, d), F32),
                   jax.ShapeDtypeStruct((8, d), F32)],
        grid=(n // tm,),
        in_specs=[pl.BlockSpec((tm, N_CHIP * qc), lambda i: (i, 0)), tile, tile,
                  pl.BlockSpec((None, None, 8, d), lambda i: (l, i // tpb, 0, 0)),
                  pl.BlockSpec((None, 1, d), lambda i: (l, 0, 0)),
                  _resident((N_CHIP, d, qc))],
        out_specs=[tile, pl.BlockSpec((None, 8, d), lambda i: (i // tpb, 0, 0)),
                   pl.BlockSpec((8, d), lambda i: (0, 0))],
        compiler_params=_params(("arbitrary",)),
    )(dproj, dx1, x, mod, g1, wg_in)


def _weight_grad(name, a, b, a_spec, b_spec, out_rows, out_spec, acc_shape, grid_ij, relu2=False):
    n = a.shape[0]
    tk = min(MATMUL_TILE, n)
    nk = n // tk
    cols = acc_shape[1]

    def body(a_ref, b_ref, o_ref, acc):
        k = pl.program_id(2)

        @pl.when(k == 0)
        def _():
            acc[...] = jnp.zeros_like(acc)

        a_t = a_ref[...]
        if relu2:
            a_t = jnp.square(jnp.maximum(a_t.astype(F32), 0.0))
        acc[...] += _dot_tn(a_t, b_ref[...])

        @pl.when(k == nk - 1)
        def _():
            o_ref[...] = acc[...].reshape(o_ref.shape).astype(WIRE_DTYPE)

    gi, gj = grid_ij
    return pl.pallas_call(
        body, name=name, out_shape=jax.ShapeDtypeStruct((N_CHIP, out_rows, cols), WIRE_DTYPE),
        grid=(gi, gj, nk),
        in_specs=[a_spec(tk), b_spec(tk)],
        out_specs=out_spec,
        scratch_shapes=[pltpu.VMEM(acc_shape, F32)],
        compiler_params=_params(("arbitrary", "arbitrary", "arbitrary")),
    )(a, b)


def _row_tile(rows, cols, arrays):
    budget = VMEM_LIMIT // 3
    t = budget // (arrays * 2 * cols * 4)
    t = max(8, min(rows, t // 8 * 8))
    while rows % t:
        t -= 8
    return t


def _sum_partials(name, own, got, myq, l, nl, prev):
    _, rows, cols = own.shape
    tr = _row_tile(rows, cols, 3)
    nt = rows // tr

    def body(q_ref, own_ref, got_ref, *rest):
        o_ref = rest[-1]
        acc = own_ref[...].astype(F32)
        for k in range(3):
            acc = acc + got_ref[k].astype(F32)
        o_ref[...] = acc

    operands = [myq, own, got] + ([] if prev is None else [prev])
    return pl.pallas_call(
        body, name=name, out_shape=jax.ShapeDtypeStruct((nl * rows, cols), F32),
        grid_spec=pltpu.PrefetchScalarGridSpec(
            num_scalar_prefetch=1, grid=(nt,),
            in_specs=[pl.BlockSpec((None, tr, cols), lambda i, q: (q[0], i, 0)),
                      pl.BlockSpec((3, tr, cols), lambda i, q: (0, i, 0))]
            + ([] if prev is None else [pl.BlockSpec(memory_space=pl.ANY)]),
            out_specs=pl.BlockSpec((tr, cols), lambda i, q: (l * nt + i, 0))),
        input_output_aliases={} if prev is None else {3: 0},
        compiler_params=_params(("arbitrary",)),
    )(*operands)


def _adamw(name, w, m, v, g_a, g_b=None):
    rows, cols = w.shape
    tr = _row_tile(rows, cols, 9)
    c1 = 1.0 - ADAM_B1 ** ADAM_STEP
    c2 = 1.0 - ADAM_B2 ** ADAM_STEP

    def body(*refs):
        if g_b is None:
            w_ref, m_ref, v_ref, ga_ref, g_ref, d_ref, m2_ref, v2_ref = refs
            g = ga_ref[...]
        else:
            w_ref, m_ref, v_ref, ga_ref, gb_ref, g_ref, d_ref, m2_ref, v2_ref = refs
            g = ga_ref[...] + gb_ref[...]
        m2 = ADAM_B1 * m_ref[...] + (1.0 - ADAM_B1) * g
        v2 = ADAM_B2 * v_ref[...] + (1.0 - ADAM_B2) * (g * g)
        g_ref[...] = g
        m2_ref[...] = m2
        v2_ref[...] = v2
        d_ref[...] = -ADAM_LR * ((m2 / c1) / (jnp.sqrt(v2 / c2) + ADAM_EPS) + ADAM_WD * w_ref[...])

    tile = pl.BlockSpec((tr, cols), lambda i: (i, 0))
    operands = [w, m, v, g_a] + ([] if g_b is None else [g_b])
    return pl.pallas_call(
        body, name=name, out_shape=[jax.ShapeDtypeStruct((rows, cols), F32)] * 4,
        grid=(rows // tr,), in_specs=[tile] * len(operands), out_specs=[tile] * 4,
        compiler_params=_params(("arbitrary",)),
    )(*operands)


def _pack(parts):
    flat = [p.reshape(-1, LANES) for p in parts]
    for f in flat:
        assert f.shape[0] % 8 == 0
    return jnp.concatenate(flat, axis=0)


def _unpack(packed, shapes):
    out, r = [], 0
    for s in shapes:
        size = 1
        for e in s:
            size *= e
        rows = size // LANES
        out.append(packed[r:r + rows].reshape(s))
        r += rows
    return out


def kernel(x, c, w_ada, b_ada, norm1_g, w_in, a_ln_g, a_ln_b, a_ws, a_bs, w_pa, b_conv_w, b_conv_b, b_ln_g, b_ln_b, w_pb, w_out, norm2_g, w_ff1, w_ff2, final_g, loss_target, m_w_ada, m_b_ada, m_norm1_g, m_w_in, m_a_ln_g, m_a_ln_b, m_a_ws, m_a_bs, m_w_pa, m_b_conv_w, m_b_conv_b, m_b_ln_g, m_b_ln_b, m_w_pb, m_w_out, m_norm2_g, m_w_ff1, m_w_ff2, m_final_g, v_w_ada, v_b_ada, v_norm1_g, v_w_in, v_a_ln_g, v_a_ln_b, v_a_ws, v_a_bs, v_w_pa, v_b_conv_w, v_b_conv_b, v_b_ln_g, v_b_ln_b, v_w_pb, v_w_out, v_norm2_g, v_w_ff1, v_w_ff2, v_final_g):
    nb, t_len, d = x.shape
    nl = w_in.shape[0]
    n = nb * t_len
    cq = w_ada.shape[-1]
    cc = d // N_CHIP
    mx, my, mc = _my_place()
    myq = (2 * mx + my).astype(jnp.int32).reshape(1)

    c_slots = jnp.broadcast_to(c[None], (N_DEV, nb, d))
    c_all = _all_to_all(c_slots, "gather_c").reshape(N_DEV * nb, d)
    mod_part = _ada_forward(c_all, w_ada, b_ada.reshape(nl, 1, N_CHIP * cq), myq)
    mod_slots = mod_part.reshape(nl, N_DEV, nb, cq).transpose(1, 0, 2, 3).reshape(N_DEV, nl * nb, cq)
    mod_got = _all_to_all(mod_slots, "exchange_mod").reshape(N_CHIP, 2, nl, nb, cq)[:, 0]
    mod6 = mod_got.transpose(1, 2, 0, 3).reshape(nl, nb, 6, d)
    mod = jnp.pad(mod6, ((0, 0), (0, 0), (0, 2), (0, 0)))

    cw_mine = b_conv_w.reshape(nl * CONV_TAPS, cc)
    cwg = _all_to_all(jnp.broadcast_to(cw_mine[None], (N_DEV,) + cw_mine.shape), "gather_conv_w")
    cwg = cwg.reshape(N_CHIP, 2, nl, CONV_TAPS, cc)[:, 0]
    cw = jnp.pad(cwg.transpose(1, 2, 0, 3).reshape(nl, CONV_TAPS, d), ((0, 0), (0, HALO - CONV_TAPS), (0, 0)))
    cw = cw.reshape(nl, HALO, d // LANES, LANES).transpose(0, 2, 1, 3)

    big = ["w_in", "w_pa", "w_pb", "w_out", "w_ff1", "w_ff2"]
    ws_given = dict(w_in=(w_in, m_w_in, v_w_in), w_pa=(w_pa, m_w_pa, v_w_pa), w_pb=(w_pb, m_w_pb, v_w_pb),
                    w_out=(w_out, m_w_out, v_w_out), w_ff1=(w_ff1, m_w_ff1, v_w_ff1), w_ff2=(w_ff2, m_w_ff2, v_w_ff2))

    def own_slot(w_l):
        empty = lax.empty((N_CHIP,) + w_l.shape, WIRE_DTYPE)
        return lax.dynamic_update_index_in_dim(empty, w_l.astype(WIRE_DTYPE), myq[0], 0)

    def zero_after(*arrays):
        z = jnp.zeros((8, LANES), F32)
        for a in arrays:
            piece = a.reshape(-1, a.shape[-1])[:8, :LANES]
            z = z + jnp.where(jnp.isfinite(piece), piece, 0.0) * 0.0
        return z

    token = zero_after(cw, mod[:, 0])
    gathers = []
    for l in range(nl):
        send_sems, recv_sems, _, lands, token = _split_start(
            f"gather_start_{l}", "gather", [], [own_slot(ws_given[k][0][l]) for k in big], token)
        gathers.append((send_sems, recv_sems, lands))
    mod = mod + token[0, 0]

    def gather_wait(l, part, lo, hi, after):
        send_sems, recv_sems, lands = gathers[l]
        return _split_wait(f"gather_wait_{part}_{l}", "gather", send_sems[lo:hi], recv_sems[lo:hi], [],
                           lands[lo:hi], after)[1]

    vec3 = lambda p: p.reshape(nl, 1, d)
    g1, g2 = vec3(norm1_g), vec3(norm2_g)
    lng, lnb, cb, blg, blb = vec3(a_ln_g), vec3(a_ln_b), vec3(b_conv_b), vec3(b_ln_g), vec3(b_ln_b)
    bst = a_bs.transpose(0, 2, 1)

    xs = x.reshape(n, d)
    saved = []
    weights = []
    for l in range(nl):
        (wg_in,) = gather_wait(l, "in", 0, 1, mod if l == 0 else xs)
        h, proj = _in_proj(l, xs, mod, g1, wg_in, t_len)
        ya_in, yb_in, zc = _branches_fwd(l, proj, lng, lnb, a_ws, bst, cw, cb, blg, blb, t_len)
        wg_pa, wg_pb, wg_out = gather_wait(l, "mid", 1, 4, ya_in)
        ya, yb, merged, o, x1 = _merge_out(l, xs, mod, proj, ya_in, yb_in, wg_pa, wg_pb, wg_out, t_len)
        wg_ff1, wg_ff2 = gather_wait(l, "ffn", 4, 6, x1)
        h2, f, o2, x2 = _ffn_fwd(l, x1, mod, g2, wg_ff1, wg_ff2, t_len)
        saved.append((xs, h, proj, ya_in, yb_in, zc, ya, yb, merged, o, x1, h2, f, o2))
        weights.append((wg_in, wg_pa, wg_pb, wg_out, wg_ff1, wg_ff2))
        xs = x2

    loss_blk, dx, dfinal = _loss_head(xs, final_g.reshape(1, d), loss_target.reshape(n, d))
    loss = lax.psum(loss_blk[0, 0], ("x", "y", "c"))

    tok = lambda w: (lambda tk: pl.BlockSpec((tk, w), lambda i, j, k: (k, 0)))
    tok_i = lambda w: (lambda tk: pl.BlockSpec((tk, w), lambda i, j, k: (k, i)))
    tok_j = lambda w: (lambda tk: pl.BlockSpec((tk, w), lambda i, j, k: (k, j)))
    qin = weights[0][0].shape[-1]
    hq = weights[0][4].shape[-1]
    rq = d // N_CHIP
    slot_i = lambda r, cdim: pl.BlockSpec((None, r, cdim), lambda i, j, k: (i, 0, 0))
    slot_j = lambda r, cdim: pl.BlockSpec((None, r, cdim), lambda i, j, k: (j, 0, 0))
    all_slots = pl.BlockSpec((N_CHIP, rq, d), lambda i, j, k: (0, 0, 0))
    scatters = []

    def scatter_start(l, part, names, grads, after):
        lands = [lax.empty((3,) + g.shape[1:], g.dtype) for g in grads]
        send_sems, recv_sems, srcs, lands, tok_out = _split_start(f"scatter_start_{part}_{l}", "scatter", grads, lands,
                                                                  after)
        scatters.append((f"scatter_wait_{part}_{l}", l, names, send_sems, recv_sems, srcs, lands))
        return tok_out

    dmods, small = [None] * nl, [None] * nl
    for l in reversed(range(nl)):
        x0, h, proj, ya_in, yb_in, zc, ya, yb, merged, o, x1, h2, f, o2 = saved[l]
        wg_in, wg_pa, wg_pb, wg_out, wg_ff1, wg_ff2 = weights[l]
        do2, df, dx1, dmod_c, dg2 = _ffn_bwd(l, dx, x1, mod, g2, o2, f, wg_ff1, wg_ff2, t_len, nb)
        g_ff2 = _weight_grad(f"grad_w_ff2_{l}", f, do2, tok_i(hq), tok(d), hq, slot_i(hq, d), (hq, d), (N_CHIP, 1),
                             relu2=True)
        g_ff1 = _weight_grad(f"grad_w_ff1_{l}", h2, df, tok(d), tok_j(hq), d, slot_j(d, hq), (d, hq), (1, N_CHIP))
        token = scatter_start(l, "ffn", ["w_ff2", "w_ff1"], [g_ff2, g_ff1], token)
        mod = mod + token[0, 0]
        do, dya, dyb, dya_in, dyb_in, dproj, dmod_b = _merge_bwd(l, dx1, mod, o, ya, yb, proj, wg_pa, wg_pb, wg_out,
                                                                 t_len, nb)
        g_out = _weight_grad(f"grad_w_out_{l}", merged, do, tok(d), tok(d), rq, all_slots, (d, d), (1, 1))
        g_pa = _weight_grad(f"grad_w_pa_{l}", ya_in, dya, tok(d), tok(d), rq, all_slots, (d, d), (1, 1))
        g_pb = _weight_grad(f"grad_w_pb_{l}", yb_in, dyb, tok(d), tok(d), rq, all_slots, (d, d), (1, 1))
        token = scatter_start(l, "mid", ["w_out", "w_pa", "w_pb"], [g_out, g_pa, g_pb], token)
        lng = lng + token[0, 0]
        dproj, dws, dbst, dcw, vecs = _branches_bwd(l, proj, zc, dya_in, dyb_in, dproj, lng, lnb, a_ws, bst, cw,
                                                    blg, blb, t_len)
        g_in = _weight_grad(f"grad_w_in_{l}", h, dproj, tok(d), tok_j(qin), d, slot_j(d, qin), (d, qin), (1, N_CHIP))
        token = scatter_start(l, "in", ["w_in"], [g_in], token)
        mod = mod + token[0, 0]
        dx, dmod_a, dg1 = _in_proj_bwd(l, dproj, dx1, x0, mod, g1, wg_in, t_len, nb)
        dmods[l] = jnp.concatenate([dmod_a[:, 0:2], dmod_b[:, 2:3], dmod_c[:, 3:6]], axis=1)
        dcw = dcw.transpose(1, 0, 2).reshape(HALO, d)[:CONV_TAPS]
        small[l] = (dg1[0], vecs[0], vecs[1], dws, dbst.T, dcw, vecs[2], vecs[3], vecs[4], dg2[0])
    grad_x = dx.reshape(nb, t_len, d)

    dmod_mine = jnp.stack(dmods).reshape(nl * nb, 6 * d)
    dmod_all = _all_to_all(jnp.broadcast_to(dmod_mine[None], (N_DEV,) + dmod_mine.shape), "gather_dmod")
    dmod_all = dmod_all.reshape(N_DEV, nl, nb, 6 * d).transpose(1, 0, 2, 3).reshape(nl, N_DEV * nb, 6 * d)

    names = ["norm1_g", "a_ln_g", "a_ln_b", "a_ws", "a_bs", "b_conv_w", "b_conv_b", "b_ln_g", "b_ln_b", "norm2_g"]
    stacked = [jnp.stack([small[l][k] for l in range(nl)]) for k in range(len(names))]
    stacked[5] = jnp.pad(stacked[5], ((0, 0), (0, HALO - CONV_TAPS), (0, 0)))
    stacked.append(dfinal)
    part_shapes = [s.shape for s in stacked]
    packed = _pack(stacked)
    prow = packed.shape[0]
    pad_rows = (-prow) % (8 * N_DEV)
    packed = jnp.pad(packed, ((0, pad_rows), (0, 0)))
    srow = packed.shape[0] // N_DEV
    mine = _all_to_all(packed.reshape(N_DEV, srow, LANES), "reduce_small", reduce=True)
    total = _all_to_all(jnp.broadcast_to(mine[None], (N_DEV, srow, LANES)), "gather_small")
    total = total.reshape(N_DEV * srow, LANES)[:prow]

    half = dict.fromkeys(big)
    for name, l, group, send_sems, recv_sems, srcs, lands in scatters:
        srcs, lands = _split_wait(name, "scatter", send_sems, recv_sems, srcs, lands, total)
        for k, g_own, g_got in zip(group, srcs, lands):
            half[k] = _sum_partials(f"sum_{k}_{l}", g_own, g_got, myq, l, nl, half[k])
    sums = [half[k] for k in big]
    swap_send, swap_recv, sums, others, token = _split_start(
        "swap_start", "swap", sums, [lax.empty(s.shape, s.dtype) for s in sums], total)
    dmod_all = dmod_all + token[0, 0]
    g_w_ada, g_b_ada = _ada_backward(c_all, dmod_all, myq, cq)

    sg = dict(zip(names + ["final_g"], _unpack(total, part_shapes)))
    sg["b_conv_w"] = lax.dynamic_slice_in_dim(sg["b_conv_w"][:, :CONV_TAPS], myq[0] * cc, cc, axis=2).reshape(
        nl, CONV_TAPS, 1, cc)
    sg["final_g"] = sg["final_g"][0]
    sg["b_ada"] = g_b_ada.reshape(nl, N_CHIP * cq)
    small_names = ["b_ada", "norm1_g", "a_ln_g", "a_ln_b", "a_ws", "a_bs", "b_conv_w", "b_conv_b", "b_ln_g",
                   "b_ln_b", "norm2_g", "final_g"]
    given = dict(b_ada=(b_ada, m_b_ada, v_b_ada), norm1_g=(norm1_g, m_norm1_g, v_norm1_g),
                 a_ln_g=(a_ln_g, m_a_ln_g, v_a_ln_g), a_ln_b=(a_ln_b, m_a_ln_b, v_a_ln_b),
                 a_ws=(a_ws, m_a_ws, v_a_ws), a_bs=(a_bs, m_a_bs, v_a_bs),
                 b_conv_w=(b_conv_w, m_b_conv_w, v_b_conv_w), b_conv_b=(b_conv_b, m_b_conv_b, v_b_conv_b),
                 b_ln_g=(b_ln_g, m_b_ln_g, v_b_ln_g), b_ln_b=(b_ln_b, m_b_ln_b, v_b_ln_b),
                 norm2_g=(norm2_g, m_norm2_g, v_norm2_g), final_g=(final_g, m_final_g, v_final_g))

    def padded(a):
        rows = -(-a.size // (8 * LANES)) * 8
        return jnp.pad(a.reshape(-1), (0, rows * LANES - a.size)).reshape(rows, LANES)

    packs = [_pack([padded(given[k][j]) for k in small_names]) for j in range(3)]
    gpack = _pack([padded(sg[k].astype(F32)) for k in small_names])
    res_small = _adamw("adamw_small", packs[0], packs[1], packs[2], gpack)
    out = {}
    for j, kind in enumerate(["grad", "delta", "new_m", "new_v"]):
        r = 0
        for k in small_names:
            a = given[k][0]
            rows = -(-a.size // (8 * LANES)) * 8
            out[(kind, k)] = res_small[j][r:r + rows].reshape(-1)[:a.size].reshape(a.shape)
            r += rows

    res = _adamw("adamw_w_ada", w_ada.reshape(nl * d, cq), m_w_ada.reshape(nl * d, cq), v_w_ada.reshape(nl * d, cq),
                 g_w_ada.reshape(nl * d, cq))
    for kind, r in zip(["grad", "delta", "new_m", "new_v"], res):
        out[(kind, "w_ada")] = r.reshape(w_ada.shape)

    sums, others = _split_wait("swap_wait", "swap", swap_send, swap_recv, sums, others, res[0])
    for k, s_mine, s_other in zip(big, sums, others):
        w, m, v = ws_given[k]
        cols = w.shape[-1]
        res = _adamw(f"adamw_{k}", w.reshape(-1, cols), m.reshape(-1, cols), v.reshape(-1, cols), s_mine, s_other)
        for kind, r in zip(["grad", "delta", "new_m", "new_v"], res):
            out[(kind, k)] = r.reshape(w.shape)

    order = ["w_ada", "b_ada", "norm1_g", "w_in", "a_ln_g", "a_ln_b", "a_ws", "a_bs", "w_pa", "b_conv_w", "b_conv_b",
             "b_ln_g", "b_ln_b", "w_pb", "w_out", "norm2_g", "w_ff1", "w_ff2", "final_g"]
    return (loss, grad_x, *[out[("grad", k)] for k in order], *[out[("delta", k)] for k in order],
            *[out[("new_m", k)] for k in order], *[out[("new_v", k)] for k in order])
```

```python
import functools

import jax
import jax.numpy as jnp
from jax import lax
from jax.experimental import pallas as pl
from jax.experimental.pallas import tpu as pltpu

F32 = jnp.float32
MXU_DTYPE = jnp.bfloat16
ACT_DTYPE = jnp.bfloat16
WIRE_DTYPE = jnp.bfloat16

EPS = 1e-6
CHUNK = 128
HEADS = 8
CONV_TAPS = 31
HALO = 32
N_DEV = 8
N_CHIP = 4
ADAM_LR, ADAM_B1, ADAM_B2, ADAM_EPS, ADAM_WD, ADAM_STEP = 0.001, 0.9, 0.999, 1e-08, 0.01, 10

V7X_VMEM_BYTES = 64 * 1024 * 1024
VMEM_LIMIT = V7X_VMEM_BYTES * 7 // 8
TOKEN_TILE = 512
MATMUL_TILE = 2048
FFN_BWD_TILE = 256
CONV_ROWS = 64
TAP_GRAD_ROWS = 32
LANES = 128
MESH_ID = pl.DeviceIdType.MESH


def _params(sem=None):
    return pltpu.CompilerParams(dimension_semantics=sem, vmem_limit_bytes=VMEM_LIMIT)


def _resident(shape):
    return pl.BlockSpec(shape, lambda *_: (0,) * len(shape), pipeline_mode=pl.Buffered(1))


def _dot(a, b):
    return jnp.dot(a.astype(MXU_DTYPE), b.astype(MXU_DTYPE), preferred_element_type=F32)


def _dot_nt(a, b):
    return lax.dot_general(a.astype(MXU_DTYPE), b.astype(MXU_DTYPE), (((1,), (1,)), ((), ())),
                           preferred_element_type=F32)


def _dot_tn(a, b):
    return lax.dot_general(a.astype(MXU_DTYPE), b.astype(MXU_DTYPE), (((0,), (0,)), ((), ())),
                           preferred_element_type=F32)


def _colsum(a):
    return jnp.sum(a, axis=0, keepdims=True)


def _rowmean(a):
    return jnp.mean(a, axis=-1, keepdims=True)


def _sigmoid(a):
    return 1.0 / (1.0 + jnp.exp(-a))


def _modnorm_fwd(x, g, sc, sh):
    r = lax.rsqrt(_rowmean(x * x) + EPS)
    return (x * r) * (g * (1.0 + sc)) + sh


def _modnorm_bwd(x, dh, g, sc):
    r = lax.rsqrt(_rowmean(x * x) + EPS)
    xn = x * r
    dxn = dh * (g * (1.0 + sc))
    dx = r * (dxn - xn * _rowmean(dxn * xn))
    return dx, _colsum(dh), _colsum(dh * xn)


def _ln_stats(v):
    mu = _rowmean(v)
    vc = v - mu
    rstd = lax.rsqrt(_rowmean(vc * vc) + EPS)
    return vc * rstd, rstd


def _ln_bwd(dy, vhat, rstd, g):
    dvh = dy * g
    return rstd * (dvh - _rowmean(dvh) - vhat * _rowmean(dvh * vhat))


def _causal_mask():
    row = lax.broadcasted_iota(jnp.int32, (CHUNK, CHUNK), 0)
    col = lax.broadcasted_iota(jnp.int32, (CHUNK, CHUNK), 1)
    return row >= col


def _my_place():
    return lax.axis_index("x"), lax.axis_index("y"), lax.axis_index("c")


def _all_to_all(x, name, reduce=False):
    n, rows, cols = x.shape
    assert n == N_DEV

    def body(x_ref, o_ref, *scratch):
        if reduce:
            land, send_sems, recv_sems = scratch
        else:
            land = o_ref
            send_sems, recv_sems = scratch
        mx, my, mc = _my_place()
        me = 4 * mx + 2 * my + mc
        land[me] = x_ref[me]
        copies = []
        for k in range(1, N_DEV):
            px = (mx + ((k >> 2) & 1)) % 2
            py = (my + ((k >> 1) & 1)) % 2
            pc = (mc + (k & 1)) % 2
            peer = 4 * px + 2 * py + pc
            cp = pltpu.make_async_remote_copy(
                src_ref=x_ref.at[peer], dst_ref=land.at[me],
                send_sem=send_sems.at[k - 1], recv_sem=recv_sems.at[k - 1],
                device_id=(px, py, pc), device_id_type=MESH_ID)
            cp.start()
            copies.append(cp)
        for cp in copies:
            cp.wait()
        if reduce:
            acc = land[0]
            for s in range(1, N_DEV):
                acc = acc + land[s]
            o_ref[...] = acc

    scratch = [pltpu.SemaphoreType.DMA((N_DEV - 1,)), pltpu.SemaphoreType.DMA((N_DEV - 1,))]
    if reduce:
        scratch = [pltpu.VMEM((N_DEV, rows, cols), x.dtype)] + scratch
        out_shape = jax.ShapeDtypeStruct((rows, cols), x.dtype)
    else:
        out_shape = jax.ShapeDtypeStruct(x.shape, x.dtype)
    return pl.pallas_call(
        body, name=name, out_shape=out_shape,
        in_specs=[pl.BlockSpec(memory_space=pltpu.VMEM)],
        out_specs=pl.BlockSpec(memory_space=pltpu.VMEM),
        scratch_shapes=scratch,
        compiler_params=pltpu.CompilerParams(vmem_limit_bytes=VMEM_LIMIT),
    )(x)


def _other_chips(mx, my):
    return [(1 - mx, my), (mx, 1 - my), (1 - mx, 1 - my)]


HBM_SPEC = pl.BlockSpec(memory_space=pltpu.HBM)
SEM_SPEC = pl.BlockSpec(memory_space=pltpu.SEMAPHORE)
ANY_SPEC = pl.BlockSpec(memory_space=pl.ANY)
SPLIT_EFFECT = pltpu.SideEffectType.DATAFLOW_SIDE_EFFECTING


def _quarter_copies(mode, srcs, lands, send_sems, recv_sems):
    mx, my, mc = _my_place()
    myq = 2 * mx + my
    if mode == "swap":
        return [pltpu.make_async_remote_copy(
            src_ref=srcs[a], dst_ref=lands[a], send_sem=send_sems[a].at[0], recv_sem=recv_sems[a].at[0],
            device_id=(mx, my, 1 - mc), device_id_type=MESH_ID) for a in range(len(lands))]
    copies = []
    for a in range(len(lands)):
        for k, (px, py) in enumerate(_other_chips(mx, my)):
            if mode == "gather":
                src, dst = lands[a].at[myq], lands[a].at[myq]
            else:
                src, dst = srcs[a].at[2 * px + py], lands[a].at[k]
            copies.append(pltpu.make_async_remote_copy(
                src_ref=src, dst_ref=dst, send_sem=send_sems[a].at[k], recv_sem=recv_sems[a].at[k],
                device_id=(px, py, mc), device_id_type=MESH_ID))
    return copies


def _split_start(name, mode, srcs, lands, after):
    ns, n = len(srcs), len(lands)

    def body(*refs):
        outs = refs[ns + n + 1:]
        for cp in _quarter_copies(mode, refs[:ns], refs[ns:ns + n], outs[:n], outs[n:2 * n]):
            cp.start()
        token = outs[-1]
        token[...] = jnp.zeros_like(token)

    arrays = list(srcs) + list(lands)
    per_array = 1 if mode == "swap" else 3
    res = pl.pallas_call(
        body, name=name,
        out_shape=[pltpu.SemaphoreType.DMA((per_array,))] * (2 * n) + [pltpu.HBM(x.shape, x.dtype) for x in arrays]
        + [jax.ShapeDtypeStruct((8, LANES), F32)],
        in_specs=[HBM_SPEC] * (ns + n) + [ANY_SPEC],
        out_specs=[SEM_SPEC] * (2 * n) + [HBM_SPEC] * (ns + n) + [pl.BlockSpec(memory_space=pltpu.VMEM)],
        input_output_aliases={i: 2 * n + i for i in range(ns + n)},
        compiler_params=pltpu.CompilerParams(has_side_effects=SPLIT_EFFECT),
    )(*[pltpu.with_memory_space_constraint(x, pltpu.HBM) for x in arrays], after)
    return res[:n], res[n:2 * n], res[2 * n:2 * n + ns], res[2 * n + ns:2 * n + ns + n], res[-1]


def _split_wait(name, mode, send_sems, recv_sems, srcs, lands, after):
    ns, n = len(srcs), len(lands)

    def body(*refs):
        sems = refs[ns + n:ns + 3 * n]
        for cp in _quarter_copies(mode, refs[:ns], refs[ns:ns + n], sems[:n], sems[n:]):
            cp.wait_send()
            cp.wait_recv()

    arrays = list(srcs) + list(lands)
    res = pl.pallas_call(
        body, name=name,
        out_shape=[pltpu.HBM(x.shape, x.dtype) for x in arrays],
        in_specs=[HBM_SPEC] * (ns + n) + [SEM_SPEC] * (2 * n) + [ANY_SPEC],
        out_specs=[HBM_SPEC] * (ns + n),
        input_output_aliases={i: i for i in range(ns + n)},
        compiler_params=pltpu.CompilerParams(has_side_effects=SPLIT_EFFECT),
    )(*arrays, *send_sems, *recv_sems, after)
    return res[:ns], res[ns:]


def _ada_forward(c_all, w_ada, b_ada3, myq):
    nl, d, cq = w_ada.shape
    nb = c_all.shape[0]

    def body(q_ref, c_ref, w_ref, b_ref, o_ref):
        c = c_ref[...]
        act = c * _sigmoid(c)
        o_ref[...] = _dot(act, w_ref[...]) + b_ref[...]

    return pl.pallas_call(
        body, name="ada_forward",
        out_shape=jax.ShapeDtypeStruct((nl, nb, cq), F32),
        grid_spec=pltpu.PrefetchScalarGridSpec(
            num_scalar_prefetch=1, grid=(nl,),
            in_specs=[pl.BlockSpec((nb, d), lambda l, q: (0, 0)),
                      pl.BlockSpec((None, d, cq), lambda l, q: (l, 0, 0)),
                      pl.BlockSpec((None, 1, cq), lambda l, q: (l, 0, q[0]))],
            out_specs=pl.BlockSpec((None, nb, cq), lambda l, q: (l, 0, 0))),
        compiler_params=_params(("arbitrary",)),
    )(myq, c_all, w_ada, b_ada3)


def _ada_backward(c_all, dmod_all, myq, cq, after):
    nb, d = c_all.shape
    nl = dmod_all.shape[0]
    full = dmod_all.shape[2]

    def body(q_ref, c_ref, dq_ref, dall_ref, after_ref, gw_ref, gb_ref):
        c = c_ref[...]
        act = c * _sigmoid(c)
        gw_ref[...] = _dot_tn(act, dq_ref[...])
        gb_ref[...] = _colsum(dall_ref[...])

    return pl.pallas_call(
        body, name="ada_backward",
        out_shape=[jax.ShapeDtypeStruct((nl, d, cq), F32), jax.ShapeDtypeStruct((nl, 1, full), F32)],
        grid_spec=pltpu.PrefetchScalarGridSpec(
            num_scalar_prefetch=1, grid=(nl,),
            in_specs=[pl.BlockSpec((nb, d), lambda l, q: (0, 0)),
                      pl.BlockSpec((None, nb, cq), lambda l, q: (l, 0, q[0])),
                      pl.BlockSpec((None, nb, full), lambda l, q: (l, 0, 0)), ANY_SPEC],
            out_specs=[pl.BlockSpec((None, d, cq), lambda l, q: (l, 0, 0)),
                       pl.BlockSpec((None, 1, full), lambda l, q: (l, 0, 0))]),
        compiler_params=_params(("arbitrary",)),
    )(myq, c_all, dmod_all, dmod_all, after)


def _in_proj(l, x, mod, g1, wg_in, t_len):
    n, d = x.shape
    tm = min(TOKEN_TILE, t_len)
    tpb = t_len // tm
    qc = wg_in.shape[-1]

    def body(x_ref, mod_ref, g_ref, w_ref, h_ref, proj_ref):
        h = _modnorm_fwd(x_ref[...], g_ref[...], mod_ref[1:2, :], mod_ref[0:1, :]).astype(MXU_DTYPE)
        h_ref[...] = h.astype(ACT_DTYPE)
        for q in range(N_CHIP):
            proj_ref[:, q * qc:(q + 1) * qc] = jnp.dot(h, w_ref[q], preferred_element_type=F32).astype(ACT_DTYPE)

    return pl.pallas_call(
        body, name=f"in_proj_{l}",
        out_shape=[jax.ShapeDtypeStruct((n, d), ACT_DTYPE), jax.ShapeDtypeStruct((n, N_CHIP * qc), ACT_DTYPE)],
        grid=(n // tm,),
        in_specs=[pl.BlockSpec((tm, d), lambda i: (i, 0)),
                  pl.BlockSpec((None, None, 8, d), lambda i: (l, i // tpb, 0, 0)),
                  pl.BlockSpec((None, 1, d), lambda i: (l, 0, 0)),
                  _resident((N_CHIP, d, qc))],
        out_specs=[pl.BlockSpec((tm, d), lambda i: (i, 0)),
                   pl.BlockSpec((tm, N_CHIP * qc), lambda i: (i, 0))],
        compiler_params=_params(("arbitrary",)),
    )(x, mod, g1, wg_in)


def _masked_ws(ws_ref, wm_s):
    mask = _causal_mask()
    for h in range(HEADS):
        wm_s[h] = jnp.where(mask, ws_ref[h], 0.0).astype(MXU_DTYPE)


def _fill_z(i, tpb, a_ref, g_ref, ah_ref, gh_ref, zext):
    ah = ah_ref[...].astype(F32)
    gh = gh_ref[...].astype(F32)
    keep = jnp.where(i % tpb == 0, 0.0, 1.0)
    _put_lanes(zext, slice(0, HALO), ah * _sigmoid(gh) * keep)


def _put_lanes(dst3, rows, value):
    for lc in range(value.shape[-1] // LANES):
        dst3[lc, rows, :] = value[:, lc * LANES:(lc + 1) * LANES]


def _tap_windows(src3, lc, base, rows, flip):
    offs = {k: (CONV_TAPS - 1 - k) if flip else (k + 2) for k in range(CONV_TAPS)}
    for r in range(8):
        taps = [k for k in offs if offs[k] % 8 == r]
        lo = min(offs[k] for k in taps)
        hi = max(offs[k] for k in taps)
        win = src3[lc, pl.ds(base + lo, hi - lo + rows), :]
        for k in taps:
            yield k, win[offs[k] - lo:offs[k] - lo + rows]


def _conv_taps(src3, w3_ref, dst3, lc, nrows, flip):
    for b in range(nrows // CONV_ROWS):
        acc = jnp.zeros((CONV_ROWS, LANES), F32)
        for k, win in _tap_windows(src3, lc, b * CONV_ROWS, CONV_ROWS, flip):
            acc = acc + win * w3_ref[lc, k:k + 1, :]
        dst3[lc, b * CONV_ROWS:(b + 1) * CONV_ROWS, :] = acc


def _branches_fwd(l, proj, lng, lnb, ws, bst, cw, cb, blg, blb, t_len):
    n = proj.shape[0]
    d = lng.shape[-1]
    tm = min(TOKEN_TILE, t_len)
    tpb = t_len // tm
    per = tm // HALO
    nchunk = tm // CHUNK

    def body(u_ref, v_ref, a_ref, g_ref, ah_ref, gh_ref, lng_ref, lnb_ref, ws_ref, bst_ref, cw_ref, cb_ref,
             blg_ref, blb_ref, ya_ref, yb_ref, zc_ref, wm_s, zext, zc3):
        i = pl.program_id(0)
        _masked_ws(ws_ref, wm_s)
        _fill_z(i, tpb, a_ref, g_ref, ah_ref, gh_ref, zext)

        def chunk(c, carry):
            r0 = pl.multiple_of(c * CHUNK, CHUNK)
            rows = pl.ds(r0, CHUNK)
            vhat, _ = _ln_stats(v_ref[rows, :].astype(F32))
            vn = (vhat * lng_ref[...] + lnb_ref[...]).astype(MXU_DTYPE)
            u = u_ref[rows, :].astype(F32)
            for h in range(HEADS):
                cols = slice(h * CHUNK, (h + 1) * CHUNK)
                s = jnp.dot(wm_s[h], vn[:, cols], preferred_element_type=F32) + bst_ref[:, h:h + 1]
                ya_ref[rows, cols] = (u[:, cols] * s).astype(ACT_DTYPE)
            a = a_ref[rows, :].astype(F32)
            g = g_ref[rows, :].astype(F32)
            _put_lanes(zext, pl.ds(HALO + r0, CHUNK), a * _sigmoid(g))
            return carry

        lax.fori_loop(0, nchunk, chunk, 0)

        def lane_chunk(lc, carry):
            _conv_taps(zext, cw_ref, zc3, lc, tm, flip=False)
            return carry

        lax.fori_loop(0, d // LANES, lane_chunk, 0)

        def chunk2(c, carry):
            r0 = pl.multiple_of(c * CHUNK, CHUNK)
            rows = pl.ds(r0, CHUNK)
            for lc in range(d // LANES):
                lanes = slice(lc * LANES, (lc + 1) * LANES)
                zc_ref[rows, lanes] = (zc3[lc, rows, :] + cb_ref[:, lanes]).astype(ACT_DTYPE)
            zhat, _ = _ln_stats(zc_ref[rows, :].astype(F32))
            zn = zhat * blg_ref[...] + blb_ref[...]
            yb_ref[rows, :] = (zn * _sigmoid(zn)).astype(ACT_DTYPE)
            return carry

        lax.fori_loop(0, nchunk, chunk2, 0)

    col = lambda k: pl.BlockSpec((tm, d), lambda i: (i, k))
    halo = lambda k: pl.BlockSpec((HALO, d), lambda i: (jnp.maximum(i * per - 1, 0), k))
    vec = pl.BlockSpec((None, 1, d), lambda i: (l, 0, 0))
    out = pl.BlockSpec((tm, d), lambda i: (i, 0))
    return pl.pallas_call(
        body, name=f"branches_fwd_{l}",
        out_shape=[jax.ShapeDtypeStruct((n, d), ACT_DTYPE)] * 3,
        grid=(n // tm,),
        in_specs=[col(0), col(1), col(2), col(3), halo(2), halo(3), vec, vec,
                  pl.BlockSpec((None, HEADS, CHUNK, CHUNK), lambda i: (l, 0, 0, 0)),
                  pl.BlockSpec((None, CHUNK, HEADS), lambda i: (l, 0, 0)),
                  pl.BlockSpec((None, d // LANES, HALO, LANES), lambda i: (l, 0, 0, 0)), vec, vec, vec],
        out_specs=[out, out, out],
        scratch_shapes=[pltpu.VMEM((HEADS, CHUNK, CHUNK), MXU_DTYPE), pltpu.VMEM((d // LANES, HALO + tm, LANES), F32),
                        pltpu.VMEM((d // LANES, tm, LANES), F32)],
        compiler_params=_params(("arbitrary",)),
    )(proj, proj, proj, proj, proj, proj, lng, lnb, ws, bst, cw, cb, blg, blb)


def _merge_out(l, x, mod, proj, ya_in, yb_in, wg_pa, wg_pb, wg_out, t_len):
    n, d = x.shape
    tm = min(TOKEN_TILE, t_len)
    tpb = t_len // tm
    rq = d // N_CHIP

    def body(x_ref, mod_ref, ga_ref, gb_ref, yai_ref, ybi_ref, wpa_ref, wpb_ref, wo_ref,
             ya_ref, yb_ref, mg_ref, o_ref, x1_ref):
        wpa = wpa_ref[...].reshape(d, d)
        wpb = wpb_ref[...].reshape(d, d)
        wo = wo_ref[...].reshape(d, d)
        ya = jnp.dot(yai_ref[...].astype(MXU_DTYPE), wpa, preferred_element_type=F32)
        yb = jnp.dot(ybi_ref[...].astype(MXU_DTYPE), wpb, preferred_element_type=F32)
        merged = _sigmoid(ga_ref[...].astype(F32)) * ya + _sigmoid(gb_ref[...].astype(F32)) * yb
        o = _dot(merged, wo)
        ya_ref[...] = ya.astype(ACT_DTYPE)
        yb_ref[...] = yb.astype(ACT_DTYPE)
        mg_ref[...] = merged.astype(ACT_DTYPE)
        o_ref[...] = o.astype(ACT_DTYPE)
        x1_ref[...] = x_ref[...] + mod_ref[2:3, :] * o

    tile = pl.BlockSpec((tm, d), lambda i: (i, 0))
    wspec = pl.BlockSpec((N_CHIP, rq, d), lambda i: (0, 0, 0))
    return pl.pallas_call(
        body, name=f"merge_out_{l}",
        out_shape=[jax.ShapeDtypeStruct((n, d), ACT_DTYPE)] * 4 + [jax.ShapeDtypeStruct((n, d), F32)],
        grid=(n // tm,),
        in_specs=[tile, pl.BlockSpec((None, None, 8, d), lambda i: (l, i // tpb, 0, 0)),
                  pl.BlockSpec((tm, d), lambda i: (i, 4)), pl.BlockSpec((tm, d), lambda i: (i, 5)),
                  tile, tile, wspec, wspec, wspec],
        out_specs=[tile] * 5,
        compiler_params=_params(("arbitrary",)),
    )(x, mod, proj, proj, ya_in, yb_in, wg_pa, wg_pb, wg_out)


def _ffn_fwd(l, x1, mod, g2, wg_ff1, wg_ff2, t_len):
    n, d = x1.shape
    tm = min(TOKEN_TILE, t_len)
    tpb = t_len // tm
    hq = wg_ff1.shape[-1]
    hid = N_CHIP * hq

    def body(x_ref, mod_ref, g_ref, w1_ref, w2_ref, h_ref, f_ref, o2_ref, x2_ref, a2_s):
        h = _modnorm_fwd(x_ref[...], g_ref[...], mod_ref[4:5, :], mod_ref[3:4, :]).astype(MXU_DTYPE)
        h_ref[...] = h.astype(ACT_DTYPE)
        for q in range(N_CHIP):
            cols = slice(q * hq, (q + 1) * hq)
            f = jnp.dot(h, w1_ref[q], preferred_element_type=F32)
            f_ref[:, cols] = f.astype(ACT_DTYPE)
            a2_s[:, cols] = jnp.square(jnp.maximum(f, 0.0)).astype(MXU_DTYPE)
        o2 = jnp.dot(a2_s[...], w2_ref[...].reshape(hid, d), preferred_element_type=F32)
        o2_ref[...] = o2.astype(ACT_DTYPE)
        x2_ref[...] = x_ref[...] + mod_ref[5:6, :] * o2

    tile = pl.BlockSpec((tm, d), lambda i: (i, 0))
    return pl.pallas_call(
        body, name=f"ffn_fwd_{l}",
        out_shape=[jax.ShapeDtypeStruct((n, d), ACT_DTYPE), jax.ShapeDtypeStruct((n, hid), ACT_DTYPE),
                   jax.ShapeDtypeStruct((n, d), ACT_DTYPE), jax.ShapeDtypeStruct((n, d), F32)],
        grid=(n // tm,),
        in_specs=[tile, pl.BlockSpec((None, None, 8, d), lambda i: (l, i // tpb, 0, 0)),
                  pl.BlockSpec((None, 1, d), lambda i: (l, 0, 0)),
                  _resident((N_CHIP, d, hq)), _resident((N_CHIP, hq, d))],
        out_specs=[tile, pl.BlockSpec((tm, hid), lambda i: (i, 0)), tile, tile],
        scratch_shapes=[pltpu.VMEM((tm, hid), MXU_DTYPE)],
        compiler_params=_params(("arbitrary",)),
    )(x1, mod, g2, wg_ff1, wg_ff2)


def _loss_head(x, final_g, target):
    n, d = x.shape
    tm = min(TOKEN_TILE, n)

    def body(x_ref, g_ref, t_ref, loss_ref, dx_ref, dg_ref):
        @pl.when(pl.program_id(0) == 0)
        def _():
            loss_ref[...] = jnp.zeros_like(loss_ref)
            dg_ref[...] = jnp.zeros_like(dg_ref)

        x_t = x_ref[...]
        g = g_ref[...]
        r = lax.rsqrt(_rowmean(x_t * x_t) + EPS)
        xn = x_t * r
        e = xn * g - t_ref[...]
        loss_ref[...] += jnp.sum(e * e) * (0.5 / d)
        dy = e * (1.0 / d)
        dxn = dy * g
        dx_ref[...] = r * (dxn - xn * _rowmean(dxn * xn))
        dg_ref[0:1, :] += _colsum(dy * xn)

    tile = pl.BlockSpec((tm, d), lambda i: (i, 0))
    return pl.pallas_call(
        body, name="loss_head",
        out_shape=[jax.ShapeDtypeStruct((8, LANES), F32), jax.ShapeDtypeStruct((n, d), F32),
                   jax.ShapeDtypeStruct((8, d), F32)],
        grid=(n // tm,),
        in_specs=[tile, pl.BlockSpec((1, d), lambda i: (0, 0)), tile],
        out_specs=[pl.BlockSpec((8, LANES), lambda i: (0, 0)), tile, pl.BlockSpec((8, d), lambda i: (0, 0))],
        compiler_params=_params(("arbitrary",)),
    )(x, final_g, target)


def _norm_tail(i, tpb, x_ref, dxin_ref, dh, g_ref, sc, dx_ref, dmod_ref, dg_ref, row_sh, row_sc):
    dxm, dsh, q = _modnorm_bwd(x_ref[...], dh, g_ref[...], sc)
    dx_ref[...] = dxin_ref[...] + dxm
    dmod_ref[row_sh:row_sh + 1, :] += dsh
    dmod_ref[row_sc:row_sc + 1, :] += g_ref[...] * q
    dg_ref[0:1, :] += (1.0 + sc) * q


def _ffn_bwd(l, dx2, x1, mod, g2, o2, f, wg_ff1, wg_ff2, t_len, nb):
    n, d = dx2.shape
    tm = min(FFN_BWD_TILE, t_len)
    tpb = t_len // tm
    hq = wg_ff1.shape[-1]
    hid = N_CHIP * hq

    def body(dx2_ref, x1_ref, mod_ref, g_ref, o2_ref, f_ref, w1_ref, w2_ref,
             do2_ref, df_ref, dx1_ref, dmod_ref, dg_ref):
        i = pl.program_id(0)

        @pl.when(i == 0)
        def _():
            dg_ref[...] = jnp.zeros_like(dg_ref)

        @pl.when(i % tpb == 0)
        def _():
            dmod_ref[...] = jnp.zeros_like(dmod_ref)

        dx2_t = dx2_ref[...]
        dmod_ref[5:6, :] += _colsum(dx2_t * o2_ref[...].astype(F32))
        do2 = (dx2_t * mod_ref[5:6, :]).astype(MXU_DTYPE)
        do2_ref[...] = do2.astype(ACT_DTYPE)
        dh = jnp.zeros((tm, d), F32)
        for q in range(N_CHIP):
            cols = slice(q * hq, (q + 1) * hq)
            da2 = _dot_nt(do2, w2_ref[q])
            df = (da2 * (2.0 * jnp.maximum(f_ref[:, cols].astype(F32), 0.0))).astype(MXU_DTYPE)
            df_ref[:, cols] = df.astype(ACT_DTYPE)
            dh = dh + _dot_nt(df, w1_ref[q])
        _norm_tail(i, tpb, x1_ref, dx2_ref, dh, g_ref, mod_ref[4:5, :], dx1_ref, dmod_ref, dg_ref, 3, 4)

    tile = pl.BlockSpec((tm, d), lambda i: (i, 0))
    wide = pl.BlockSpec((tm, hid), lambda i: (i, 0))
    return pl.pallas_call(
        body, name=f"ffn_bwd_{l}",
        out_shape=[jax.ShapeDtypeStruct((n, d), ACT_DTYPE), jax.ShapeDtypeStruct((n, hid), ACT_DTYPE),
                   jax.ShapeDtypeStruct((n, d), F32), jax.ShapeDtypeStruct((nb, 8, d), F32),
                   jax.ShapeDtypeStruct((8, d), F32)],
        grid=(n // tm,),
        in_specs=[tile, tile, pl.BlockSpec((None, None, 8, d), lambda i: (l, i // tpb, 0, 0)),
                  pl.BlockSpec((None, 1, d), lambda i: (l, 0, 0)), tile, wide,
                  _resident((N_CHIP, d, hq)), _resident((N_CHIP, hq, d))],
        out_specs=[tile, wide, tile, pl.BlockSpec((None, 8, d), lambda i: (i // tpb, 0, 0)),
                   pl.BlockSpec((8, d), lambda i: (0, 0))],
        compiler_params=_params(("arbitrary",)),
    )(dx2, x1, mod, g2, o2, f, wg_ff1, wg_ff2)


def _merge_bwd(l, dx1, mod, o, ya, yb, proj, wg_pa, wg_pb, wg_out, t_len, nb, after):
    n, d = dx1.shape
    tm = min(TOKEN_TILE, t_len)
    tpb = t_len // tm
    rq = d // N_CHIP

    def body(dx_ref, mod_ref, o_ref, ya_ref, yb_ref, ga_ref, gb_ref, wpa_ref, wpb_ref, wo_ref, after_ref,
             do_ref, dya_ref, dyb_ref, dyai_ref, dybi_ref, dproj_ref, dmod_ref):
        i = pl.program_id(0)

        @pl.when(i % tpb == 0)
        def _():
            dmod_ref[...] = jnp.zeros_like(dmod_ref)

        dx = dx_ref[...]
        dmod_ref[2:3, :] += _colsum(dx * o_ref[...].astype(F32))
        do = (dx * mod_ref[2:3, :]).astype(MXU_DTYPE)
        do_ref[...] = do.astype(ACT_DTYPE)
        dm = _dot_nt(do, wo_ref[...].reshape(d, d))
        sa = _sigmoid(ga_ref[...].astype(F32))
        sb = _sigmoid(gb_ref[...].astype(F32))
        dya = (dm * sa).astype(MXU_DTYPE)
        dyb = (dm * sb).astype(MXU_DTYPE)
        dya_ref[...] = dya.astype(ACT_DTYPE)
        dyb_ref[...] = dyb.astype(ACT_DTYPE)
        dproj_ref[:, 0:d] = (dm * ya_ref[...].astype(F32) * sa * (1.0 - sa)).astype(ACT_DTYPE)
        dproj_ref[:, d:2 * d] = (dm * yb_ref[...].astype(F32) * sb * (1.0 - sb)).astype(ACT_DTYPE)
        dyai_ref[...] = _dot_nt(dya, wpa_ref[...].reshape(d, d)).astype(ACT_DTYPE)
        dybi_ref[...] = _dot_nt(dyb, wpb_ref[...].reshape(d, d)).astype(ACT_DTYPE)

    tile = pl.BlockSpec((tm, d), lambda i: (i, 0))
    wspec = pl.BlockSpec((N_CHIP, rq, d), lambda i: (0, 0, 0))
    return pl.pallas_call(
        body, name=f"merge_bwd_{l}",
        out_shape=[jax.ShapeDtypeStruct((n, d), ACT_DTYPE)] * 5
        + [jax.ShapeDtypeStruct((n, 6 * d), ACT_DTYPE), jax.ShapeDtypeStruct((nb, 8, d), F32)],
        grid=(n // tm,),
        in_specs=[tile, pl.BlockSpec((None, None, 8, d), lambda i: (l, i // tpb, 0, 0)), tile, tile, tile,
                  pl.BlockSpec((tm, d), lambda i: (i, 4)), pl.BlockSpec((tm, d), lambda i: (i, 5)),
                  wspec, wspec, wspec, ANY_SPEC],
        out_specs=[tile] * 5 + [pl.BlockSpec((tm, 2 * d), lambda i: (i, 2)),
                                pl.BlockSpec((None, 8, d), lambda i: (i // tpb, 0, 0))],
        compiler_params=_params(("arbitrary",)),
    )(dx1, mod, o, ya, yb, proj, proj, wg_pa, wg_pb, wg_out, after)


def _branches_bwd(l, proj, zc, dya_in, dyb_in, dproj, lng, lnb, ws, bst, cw, blg, blb, t_len, after):
    n = proj.shape[0]
    d = lng.shape[-1]
    tm = min(TOKEN_TILE, t_len)
    tpb = t_len // tm
    per = tm // HALO
    nchunk = tm // CHUNK
    ntile = n // tm

    def body(u_ref, v_ref, a_ref, g_ref, ah_ref, gh_ref, zc_ref, zcn_ref, dya_ref, dyb_ref, dybn_ref, dproj_in,
             lng_ref, lnb_ref, ws_ref, bst_ref, cw_ref, blg_ref, blb_ref, after_ref,
             dproj_ref, dws_ref, dbst_ref, dcw_ref, vec_ref, wm_s, zext, dzext, dz3, dvn_s):
        i = pl.program_id(0)

        @pl.when(i == 0)
        def _():
            dws_ref[...] = jnp.zeros_like(dws_ref)
            dbst_ref[...] = jnp.zeros_like(dbst_ref)
            dcw_ref[...] = jnp.zeros_like(dcw_ref)
            vec_ref[...] = jnp.zeros_like(vec_ref)

        _masked_ws(ws_ref, wm_s)
        _fill_z(i, tpb, a_ref, g_ref, ah_ref, gh_ref, zext)

        def conv_ln_bwd(zc_t, dyb_t):
            zhat, rstd = _ln_stats(zc_t)
            zn = zhat * blg_ref[...] + blb_ref[...]
            sg = _sigmoid(zn)
            dzn = dyb_t * (sg * (1.0 + zn * (1.0 - sg)))
            return _ln_bwd(dzn, zhat, rstd, blg_ref[...]), _colsum(dzn * zhat), _colsum(dzn)

        def chunk(c, carry):
            r0 = pl.multiple_of(c * CHUNK, CHUNK)
            rows = pl.ds(r0, CHUNK)
            vhat, rstd = _ln_stats(v_ref[rows, :].astype(F32))
            vn = (vhat * lng_ref[...] + lnb_ref[...]).astype(MXU_DTYPE)
            u = u_ref[rows, :].astype(F32)
            dya = dya_ref[rows, :].astype(F32)
            for h in range(HEADS):
                cols = slice(h * CHUNK, (h + 1) * CHUNK)
                s = jnp.dot(wm_s[h], vn[:, cols], preferred_element_type=F32) + bst_ref[:, h:h + 1]
                dproj_ref[rows, cols] = (dya[:, cols] * s).astype(ACT_DTYPE)
                ds = dya[:, cols] * u[:, cols]
                dvn_s[:, cols] = _dot_tn(wm_s[h], ds)
                dws_ref[h] += _dot_nt(ds, vn[:, cols])
                dbst_ref[:, h:h + 1] += jnp.sum(ds, axis=1, keepdims=True)
            dvn = dvn_s[...]
            dproj_ref[rows, d:2 * d] = _ln_bwd(dvn, vhat, rstd, lng_ref[...]).astype(ACT_DTYPE)
            vec_ref[0:1, :] += _colsum(dvn * vhat)
            vec_ref[1:2, :] += _colsum(dvn)
            a = a_ref[rows, :].astype(F32)
            g = g_ref[rows, :].astype(F32)
            _put_lanes(zext, pl.ds(HALO + r0, CHUNK), a * _sigmoid(g))
            dzc, dblg, dblb = conv_ln_bwd(zc_ref[rows, :].astype(F32), dyb_ref[rows, :].astype(F32))
            _put_lanes(dzext, rows, dzc)
            vec_ref[2:3, :] += _colsum(dzc)
            vec_ref[3:4, :] += dblg
            vec_ref[4:5, :] += dblb
            return carry

        lax.fori_loop(0, nchunk, chunk, 0)

        dzc_next, _, _ = conv_ln_bwd(zcn_ref[...].astype(F32), dybn_ref[...].astype(F32))
        _put_lanes(dzext, slice(tm, tm + HALO), dzc_next * jnp.where(i % tpb == tpb - 1, 0.0, 1.0))

        def lane_chunk_dz(lc, carry):
            _conv_taps(dzext, cw_ref, dz3, lc, tm, flip=True)
            return carry

        lax.fori_loop(0, d // LANES, lane_chunk_dz, 0)

        def lane_chunk(lc, carry):
            accs = [jnp.zeros((8, LANES), F32) for _ in range(CONV_TAPS)]
            for b in range(tm // TAP_GRAD_ROWS):
                dzc = dzext[lc, b * TAP_GRAD_ROWS:(b + 1) * TAP_GRAD_ROWS, :]
                for k, win in _tap_windows(zext, lc, b * TAP_GRAD_ROWS, TAP_GRAD_ROWS, flip=False):
                    prod = dzc * win
                    part = prod[0:8]
                    for e in range(1, TAP_GRAD_ROWS // 8):
                        part = part + prod[8 * e:8 * e + 8]
                    accs[k] = accs[k] + part
            for k in range(CONV_TAPS):
                dcw_ref[lc, k:k + 1, :] += _colsum(accs[k])
            return carry

        lax.fori_loop(0, d // LANES, lane_chunk, 0)

        def glu_bwd(c, carry):
            r0 = pl.multiple_of(c * CHUNK, CHUNK)
            rows = pl.ds(r0, CHUNK)
            for lc in range(d // LANES):
                lanes = slice(lc * LANES, (lc + 1) * LANES)
                dz = dz3[lc, rows, :]
                a = a_ref[rows, lanes].astype(F32)
                sg = _sigmoid(g_ref[rows, lanes].astype(F32))
                dproj_ref[rows, 2 * d + lc * LANES:2 * d + (lc + 1) * LANES] = (dz * sg).astype(ACT_DTYPE)
                dproj_ref[rows, 3 * d + lc * LANES:3 * d + (lc + 1) * LANES] = (
                    dz * a * sg * (1.0 - sg)).astype(ACT_DTYPE)
            return carry

        lax.fori_loop(0, nchunk, glu_bwd, 0)

        @pl.when(i == ntile - 1)
        def _():
            mask = _causal_mask()
            for h in range(HEADS):
                dws_ref[h] = jnp.where(mask, dws_ref[h], 0.0)

    col = lambda k: pl.BlockSpec((tm, d), lambda i: (i, k))
    tile = pl.BlockSpec((tm, d), lambda i: (i, 0))
    before = lambda k: pl.BlockSpec((HALO, d), lambda i: (jnp.maximum(i * per - 1, 0), k))
    following = pl.BlockSpec((HALO, d), lambda i: (jnp.minimum((i + 1) * per, n // HALO - 1), 0))
    vec = pl.BlockSpec((None, 1, d), lambda i: (l, 0, 0))
    const2 = lambda r, c: pl.BlockSpec((r, c), lambda i: (0, 0))
    return pl.pallas_call(
        body, name=f"branches_bwd_{l}",
        out_shape=[jax.ShapeDtypeStruct((n, 6 * d), ACT_DTYPE), jax.ShapeDtypeStruct((HEADS, CHUNK, CHUNK), F32),
                   jax.ShapeDtypeStruct((CHUNK, HEADS), F32), jax.ShapeDtypeStruct((d // LANES, HALO, LANES), F32),
                   jax.ShapeDtypeStruct((8, d), F32)],
        grid=(ntile,),
        in_specs=[col(0), col(1), col(2), col(3), before(2), before(3), tile, following, tile, tile, following,
                  pl.BlockSpec(memory_space=pl.ANY), vec, vec,
                  pl.BlockSpec((None, HEADS, CHUNK, CHUNK), lambda i: (l, 0, 0, 0)),
                  pl.BlockSpec((None, CHUNK, HEADS), lambda i: (l, 0, 0)),
                  pl.BlockSpec((None, d // LANES, HALO, LANES), lambda i: (l, 0, 0, 0)), vec, vec, ANY_SPEC],
        out_specs=[pl.BlockSpec((tm, 4 * d), lambda i: (i, 0)),
                   pl.BlockSpec((HEADS, CHUNK, CHUNK), lambda i: (0, 0, 0)),
                   const2(CHUNK, HEADS), pl.BlockSpec((d // LANES, HALO, LANES), lambda i: (0, 0, 0)), const2(8, d)],
        scratch_shapes=[pltpu.VMEM((HEADS, CHUNK, CHUNK), MXU_DTYPE),
                        pltpu.VMEM((d // LANES, HALO + tm, LANES), F32),
                        pltpu.VMEM((d // LANES, tm + HALO, LANES), F32),
                        pltpu.VMEM((d // LANES, tm, LANES), F32), pltpu.VMEM((CHUNK, d), F32)],
        input_output_aliases={11: 0},
        compiler_params=_params(("arbitrary",)),
    )(proj, proj, proj, proj, proj, proj, zc, zc, dya_in, dyb_in, dyb_in, dproj, lng, lnb, ws, bst, cw, blg, blb, after)


def _in_proj_bwd(l, dproj, dx1, x, mod, g1, wg_in, t_len, nb, after):
    n, d = x.shape
    tm = min(TOKEN_TILE, t_len)
    tpb = t_len // tm
    qc = wg_in.shape[-1]

    def body(dp_ref, dx1_ref, x_ref, mod_ref, g_ref, w_ref, after_ref, dx_ref, dmod_ref, dg_ref):
        i = pl.program_id(0)

        @pl.when(i == 0)
        def _():
            dg_ref[...] = jnp.zeros_like(dg_ref)

        @pl.when(i % tpb == 0)
        def _():
            dmod_ref[...] = jnp.zeros_like(dmod_ref)

        dh = jnp.zeros((tm, d), F32)
        for q in range(N_CHIP):
            dh = dh + _dot_nt(dp_ref[:, q * qc:(q + 1) * qc], w_ref[q])
        _norm_tail(i, tpb, x_ref, dx1_ref, dh, g_ref, mod_ref[1:2, :], dx_ref, dmod_ref, dg_ref, 0, 1)

    tile = pl.BlockSpec((tm, d), lambda i: (i, 0))
    return pl.pallas_call(
        body, name=f"in_proj_bwd_{l}",
        out_shape=[jax.ShapeDtypeStruct((n, d), F32), jax.ShapeDtypeStruct((nb, 8, d), F32),
                   jax.ShapeDtypeStruct((8, d), F32)],
        grid=(n // tm,),
        in_specs=[pl.BlockSpec((tm, N_CHIP * qc), lambda i: (i, 0)), tile, tile,
                  pl.BlockSpec((None, None, 8, d), lambda i: (l, i // tpb, 0, 0)),
                  pl.BlockSpec((None, 1, d), lambda i: (l, 0, 0)),
                  _resident((N_CHIP, d, qc)), ANY_SPEC],
        out_specs=[tile, pl.BlockSpec((None, 8, d), lambda i: (i // tpb, 0, 0)),
                   pl.BlockSpec((8, d), lambda i: (0, 0))],
        compiler_params=_params(("arbitrary",)),
    )(dproj, dx1, x, mod, g1, wg_in, after)


def _weight_grad(name, a, b, a_spec, b_spec, out_rows, out_spec, acc_shape, grid_ij, relu2=False):
    n = a.shape[0]
    tk = min(MATMUL_TILE, n)
    nk = n // tk
    cols = acc_shape[1]

    def body(a_ref, b_ref, o_ref, acc):
        k = pl.program_id(2)

        @pl.when(k == 0)
        def _():
            acc[...] = jnp.zeros_like(acc)

        a_t = a_ref[...]
        if relu2:
            a_t = jnp.square(jnp.maximum(a_t.astype(F32), 0.0))
        acc[...] += _dot_tn(a_t, b_ref[...])

        @pl.when(k == nk - 1)
        def _():
            o_ref[...] = acc[...].reshape(o_ref.shape).astype(WIRE_DTYPE)

    gi, gj = grid_ij
    return pl.pallas_call(
        body, name=name, out_shape=jax.ShapeDtypeStruct((N_CHIP, out_rows, cols), WIRE_DTYPE),
        grid=(gi, gj, nk),
        in_specs=[a_spec(tk), b_spec(tk)],
        out_specs=out_spec,
        scratch_shapes=[pltpu.VMEM(acc_shape, F32)],
        compiler_params=_params(("arbitrary", "arbitrary", "arbitrary")),
    )(a, b)


def _row_tile(rows, cols, arrays):
    budget = VMEM_LIMIT // 3
    t = budget // (arrays * 2 * cols * 4)
    t = max(8, min(rows, t // 8 * 8))
    while rows % t:
        t -= 8
    return t


def _sum_partials(name, own, got, myq, l, nl, prev):
    _, rows, cols = own.shape
    tr = _row_tile(rows, cols, 3)
    nt = rows // tr

    def body(q_ref, own_ref, got_ref, *rest):
        o_ref = rest[-1]
        acc = own_ref[...].astype(F32)
        for k in range(3):
            acc = acc + got_ref[k].astype(F32)
        o_ref[...] = acc

    operands = [myq, own, got] + ([] if prev is None else [prev])
    return pl.pallas_call(
        body, name=name, out_shape=jax.ShapeDtypeStruct((nl * rows, cols), F32),
        grid_spec=pltpu.PrefetchScalarGridSpec(
            num_scalar_prefetch=1, grid=(nt,),
            in_specs=[pl.BlockSpec((None, tr, cols), lambda i, q: (q[0], i, 0)),
                      pl.BlockSpec((3, tr, cols), lambda i, q: (0, i, 0))]
            + ([] if prev is None else [pl.BlockSpec(memory_space=pl.ANY)]),
            out_specs=pl.BlockSpec((tr, cols), lambda i, q: (l * nt + i, 0))),
        input_output_aliases={} if prev is None else {3: 0},
        compiler_params=_params(("arbitrary",)),
    )(*operands)


def _adamw(name, w, m, v, g_a, g_b=None):
    rows, cols = w.shape
    tr = _row_tile(rows, cols, 9)
    c1 = 1.0 - ADAM_B1 ** ADAM_STEP
    c2 = 1.0 - ADAM_B2 ** ADAM_STEP

    def body(*refs):
        if g_b is None:
            w_ref, m_ref, v_ref, ga_ref, g_ref, d_ref, m2_ref, v2_ref = refs
            g = ga_ref[...]
        else:
            w_ref, m_ref, v_ref, ga_ref, gb_ref, g_ref, d_ref, m2_ref, v2_ref = refs
            g = ga_ref[...] + gb_ref[...]
        m2 = ADAM_B1 * m_ref[...] + (1.0 - ADAM_B1) * g
        v2 = ADAM_B2 * v_ref[...] + (1.0 - ADAM_B2) * (g * g)
        g_ref[...] = g
        m2_ref[...] = m2
        v2_ref[...] = v2
        d_ref[...] = -ADAM_LR * ((m2 / c1) / (jnp.sqrt(v2 / c2) + ADAM_EPS) + ADAM_WD * w_ref[...])

    tile = pl.BlockSpec((tr, cols), lambda i: (i, 0))
    operands = [w, m, v, g_a] + ([] if g_b is None else [g_b])
    return pl.pallas_call(
        body, name=name, out_shape=[jax.ShapeDtypeStruct((rows, cols), F32)] * 4,
        grid=(rows // tr,), in_specs=[tile] * len(operands), out_specs=[tile] * 4,
        compiler_params=_params(("arbitrary",)),
    )(*operands)


def _pack(parts):
    flat = [p.reshape(-1, LANES) for p in parts]
    for f in flat:
        assert f.shape[0] % 8 == 0
    return jnp.concatenate(flat, axis=0)


def _unpack(packed, shapes):
    out, r = [], 0
    for s in shapes:
        size = 1
        for e in s:
            size *= e
        rows = size // LANES
        out.append(packed[r:r + rows].reshape(s))
        r += rows
    return out


def kernel(x, c, w_ada, b_ada, norm1_g, w_in, a_ln_g, a_ln_b, a_ws, a_bs, w_pa, b_conv_w, b_conv_b, b_ln_g, b_ln_b, w_pb, w_out, norm2_g, w_ff1, w_ff2, final_g, loss_target, m_w_ada, m_b_ada, m_norm1_g, m_w_in, m_a_ln_g, m_a_ln_b, m_a_ws, m_a_bs, m_w_pa, m_b_conv_w, m_b_conv_b, m_b_ln_g, m_b_ln_b, m_w_pb, m_w_out, m_norm2_g, m_w_ff1, m_w_ff2, m_final_g, v_w_ada, v_b_ada, v_norm1_g, v_w_in, v_a_ln_g, v_a_ln_b, v_a_ws, v_a_bs, v_w_pa, v_b_conv_w, v_b_conv_b, v_b_ln_g, v_b_ln_b, v_w_pb, v_w_out, v_norm2_g, v_w_ff1, v_w_ff2, v_final_g):
    nb, t_len, d = x.shape
    nl = w_in.shape[0]
    n = nb * t_len
    cq = w_ada.shape[-1]
    cc = d // N_CHIP
    mx, my, mc = _my_place()
    myq = (2 * mx + my).astype(jnp.int32).reshape(1)

    taps = jnp.pad(b_conv_w.reshape(nl, CONV_TAPS, cc), ((0, 0), (0, HALO - CONV_TAPS), (0, 0)))
    first = jnp.concatenate([jnp.pad(c, ((0, 8 - nb), (0, 0))), taps.reshape(nl * HALO * cc // d, d)], axis=0)
    first = _all_to_all(jnp.broadcast_to(first[None], (N_DEV,) + first.shape), "gather_c_and_taps")
    c_all = first[:, :nb].reshape(N_DEV * nb, d)
    cwg = first[:, 8:].reshape(N_CHIP, 2, nl, HALO, cc)[:, 0]
    cw = cwg.transpose(1, 2, 0, 3).reshape(nl, HALO, d)
    cw = cw.reshape(nl, HALO, d // LANES, LANES).transpose(0, 2, 1, 3)
    mod_part = _ada_forward(c_all, w_ada, b_ada.reshape(nl, 1, N_CHIP * cq), myq)
    mod_slots = mod_part.reshape(nl, N_DEV, nb, cq).transpose(1, 0, 2, 3).reshape(N_DEV, nl * nb, cq)
    mod_got = _all_to_all(mod_slots, "exchange_mod").reshape(N_CHIP, 2, nl, nb, cq)[:, 0]
    mod6 = mod_got.transpose(1, 2, 0, 3).reshape(nl, nb, 6, d)
    mod = jnp.pad(mod6, ((0, 0), (0, 0), (0, 2), (0, 0)))

    big = ["w_in", "w_pa", "w_pb", "w_out", "w_ff1", "w_ff2"]
    ws_given = dict(w_in=(w_in, m_w_in, v_w_in), w_pa=(w_pa, m_w_pa, v_w_pa), w_pb=(w_pb, m_w_pb, v_w_pb),
                    w_out=(w_out, m_w_out, v_w_out), w_ff1=(w_ff1, m_w_ff1, v_w_ff1), w_ff2=(w_ff2, m_w_ff2, v_w_ff2))

    def own_slot(w_l):
        empty = lax.empty((N_CHIP,) + w_l.shape, WIRE_DTYPE)
        return lax.dynamic_update_index_in_dim(empty, w_l.astype(WIRE_DTYPE), myq[0], 0)

    def zero_after(*arrays):
        z = jnp.zeros((8, LANES), F32)
        for a in arrays:
            piece = a.reshape(-1, a.shape[-1])[:8, :LANES]
            z = z + jnp.where(jnp.isfinite(piece), piece, 0.0) * 0.0
        return z

    token = zero_after(cw, mod[:, 0])
    gathers = []
    for l in range(nl):
        send_sems, recv_sems, _, lands, token = _split_start(
            f"gather_start_{l}", "gather", [], [own_slot(ws_given[k][0][l]) for k in big], token)
        gathers.append((send_sems, recv_sems, lands))
    mod = mod + token[0, 0]

    def gather_wait(l, part, lo, hi, after):
        send_sems, recv_sems, lands = gathers[l]
        return _split_wait(f"gather_wait_{part}_{l}", "gather", send_sems[lo:hi], recv_sems[lo:hi], [],
                           lands[lo:hi], after)[1]

    vec3 = lambda p: p.reshape(nl, 1, d)
    g1, g2 = vec3(norm1_g), vec3(norm2_g)
    lng, lnb, cb, blg, blb = vec3(a_ln_g), vec3(a_ln_b), vec3(b_conv_b), vec3(b_ln_g), vec3(b_ln_b)
    bst = a_bs.transpose(0, 2, 1)

    xs = x.reshape(n, d)
    saved = []
    weights = []
    for l in range(nl):
        (wg_in,) = gather_wait(l, "in", 0, 1, mod if l == 0 else xs)
        h, proj = _in_proj(l, xs, mod, g1, wg_in, t_len)
        ya_in, yb_in, zc = _branches_fwd(l, proj, lng, lnb, a_ws, bst, cw, cb, blg, blb, t_len)
        wg_pa, wg_pb, wg_out = gather_wait(l, "mid", 1, 4, ya_in)
        ya, yb, merged, o, x1 = _merge_out(l, xs, mod, proj, ya_in, yb_in, wg_pa, wg_pb, wg_out, t_len)
        wg_ff1, wg_ff2 = gather_wait(l, "ffn", 4, 6, x1)
        h2, f, o2, x2 = _ffn_fwd(l, x1, mod, g2, wg_ff1, wg_ff2, t_len)
        saved.append((xs, h, proj, ya_in, yb_in, zc, ya, yb, merged, o, x1, h2, f, o2))
        weights.append((wg_in, wg_pa, wg_pb, wg_out, wg_ff1, wg_ff2))
        xs = x2

    loss_blk, dx, dfinal = _loss_head(xs, final_g.reshape(1, d), loss_target.reshape(n, d))

    tok = lambda w: (lambda tk: pl.BlockSpec((tk, w), lambda i, j, k: (k, 0)))
    tok_i = lambda w: (lambda tk: pl.BlockSpec((tk, w), lambda i, j, k: (k, i)))
    tok_j = lambda w: (lambda tk: pl.BlockSpec((tk, w), lambda i, j, k: (k, j)))
    qin = weights[0][0].shape[-1]
    hq = weights[0][4].shape[-1]
    rq = d // N_CHIP
    slot_i = lambda r, cdim: pl.BlockSpec((None, r, cdim), lambda i, j, k: (i, 0, 0))
    slot_j = lambda r, cdim: pl.BlockSpec((None, r, cdim), lambda i, j, k: (j, 0, 0))
    all_slots = pl.BlockSpec((N_CHIP, rq, d), lambda i, j, k: (0, 0, 0))
    scatters = []

    def scatter_start(l, part, names, grads, after):
        lands = [lax.empty((3,) + g.shape[1:], g.dtype) for g in grads]
        send_sems, recv_sems, srcs, lands, tok_out = _split_start(f"scatter_start_{part}_{l}", "scatter", grads, lands,
                                                                  after)
        scatters.append((f"scatter_wait_{part}_{l}", l, names, send_sems, recv_sems, srcs, lands))
        return tok_out

    dmods, small = [None] * nl, [None] * nl
    for l in reversed(range(nl)):
        x0, h, proj, ya_in, yb_in, zc, ya, yb, merged, o, x1, h2, f, o2 = saved[l]
        wg_in, wg_pa, wg_pb, wg_out, wg_ff1, wg_ff2 = weights[l]
        do2, df, dx1, dmod_c, dg2 = _ffn_bwd(l, dx, x1, mod, g2, o2, f, wg_ff1, wg_ff2, t_len, nb)
        g_ff2 = _weight_grad(f"grad_w_ff2_{l}", f, do2, tok_i(hq), tok(d), hq, slot_i(hq, d), (hq, d), (N_CHIP, 1),
                             relu2=True)
        g_ff1 = _weight_grad(f"grad_w_ff1_{l}", h2, df, tok(d), tok_j(hq), d, slot_j(d, hq), (d, hq), (1, N_CHIP))
        token = scatter_start(l, "ffn", ["w_ff2", "w_ff1"], [g_ff2, g_ff1], token)
        do, dya, dyb, dya_in, dyb_in, dproj, dmod_b = _merge_bwd(l, dx1, mod, o, ya, yb, proj, wg_pa, wg_pb, wg_out,
                                                                 t_len, nb, token)
        g_out = _weight_grad(f"grad_w_out_{l}", merged, do, tok(d), tok(d), rq, all_slots, (d, d), (1, 1))
        g_pa = _weight_grad(f"grad_w_pa_{l}", ya_in, dya, tok(d), tok(d), rq, all_slots, (d, d), (1, 1))
        g_pb = _weight_grad(f"grad_w_pb_{l}", yb_in, dyb, tok(d), tok(d), rq, all_slots, (d, d), (1, 1))
        token = scatter_start(l, "mid", ["w_out", "w_pa", "w_pb"], [g_out, g_pa, g_pb], token)
        dproj, dws, dbst, dcw, vecs = _branches_bwd(l, proj, zc, dya_in, dyb_in, dproj, lng, lnb, a_ws, bst, cw,
                                                    blg, blb, t_len, token)
        g_in = _weight_grad(f"grad_w_in_{l}", h, dproj, tok(d), tok_j(qin), d, slot_j(d, qin), (d, qin), (1, N_CHIP))
        token = scatter_start(l, "in", ["w_in"], [g_in], token)
        dx, dmod_a, dg1 = _in_proj_bwd(l, dproj, dx1, x0, mod, g1, wg_in, t_len, nb, token)
        dmods[l] = jnp.concatenate([dmod_a[:, 0:2], dmod_b[:, 2:3], dmod_c[:, 3:6]], axis=1)
        dcw = dcw.transpose(1, 0, 2).reshape(HALO, d)[:CONV_TAPS]
        small[l] = (dg1[0], vecs[0], vecs[1], dws, dbst.T, dcw, vecs[2], vecs[3], vecs[4], dg2[0])
    grad_x = dx.reshape(nb, t_len, d)

    names = ["norm1_g", "a_ln_g", "a_ln_b", "a_ws", "a_bs", "b_conv_w", "b_conv_b", "b_ln_g", "b_ln_b", "norm2_g"]
    stacked = [jnp.stack([small[l][k] for l in range(nl)]) for k in range(len(names))]
    stacked[5] = jnp.pad(stacked[5], ((0, 0), (0, HALO - CONV_TAPS), (0, 0)))
    stacked += [dfinal, loss_blk]
    part_shapes = [s.shape for s in stacked]
    packed = _pack(stacked)
    prow = packed.shape[0]
    pad_rows = (-prow) % (8 * N_DEV)
    packed = jnp.pad(packed, ((0, pad_rows), (0, 0)))
    srow = packed.shape[0] // N_DEV
    mine = _all_to_all(packed.reshape(N_DEV, srow, LANES), "reduce_small", reduce=True)
    dmod_rows = nl * nb * 6 * d // LANES
    second = jnp.concatenate([mine, jnp.stack(dmods).reshape(dmod_rows, LANES)], axis=0)
    second = _all_to_all(jnp.broadcast_to(second[None], (N_DEV,) + second.shape), "gather_small_and_dmod")
    total = second[:, :srow].reshape(N_DEV * srow, LANES)[:prow]
    dmod_all = second[:, srow:].reshape(N_DEV, nl, nb, 6 * d).transpose(1, 0, 2, 3).reshape(nl, N_DEV * nb, 6 * d)

    half = dict.fromkeys(big)
    for name, l, group, send_sems, recv_sems, srcs, lands in scatters:
        srcs, lands = _split_wait(name, "scatter", send_sems, recv_sems, srcs, lands, total)
        for k, g_own, g_got in zip(group, srcs, lands):
            half[k] = _sum_partials(f"sum_{k}_{l}", g_own, g_got, myq, l, nl, half[k])
    sums = [half[k] for k in big]
    swap_send, swap_recv, sums, others, token = _split_start(
        "swap_start", "swap", sums, [lax.empty(s.shape, s.dtype) for s in sums], total)
    g_w_ada, g_b_ada = _ada_backward(c_all, dmod_all, myq, cq, token)

    sg = dict(zip(names + ["final_g", "loss"], _unpack(total, part_shapes)))
    loss = sg["loss"][0, 0]
    sg["b_conv_w"] = lax.dynamic_slice_in_dim(sg["b_conv_w"][:, :CONV_TAPS], myq[0] * cc, cc, axis=2).reshape(
        nl, CONV_TAPS, 1, cc)
    sg["final_g"] = sg["final_g"][0]
    sg["b_ada"] = g_b_ada.reshape(nl, N_CHIP * cq)
    small_names = ["b_ada", "norm1_g", "a_ln_g", "a_ln_b", "a_ws", "a_bs", "b_conv_w", "b_conv_b", "b_ln_g",
                   "b_ln_b", "norm2_g", "final_g"]
    given = dict(b_ada=(b_ada, m_b_ada, v_b_ada), norm1_g=(norm1_g, m_norm1_g, v_norm1_g),
                 a_ln_g=(a_ln_g, m_a_ln_g, v_a_ln_g), a_ln_b=(a_ln_b, m_a_ln_b, v_a_ln_b),
                 a_ws=(a_ws, m_a_ws, v_a_ws), a_bs=(a_bs, m_a_bs, v_a_bs),
                 b_conv_w=(b_conv_w, m_b_conv_w, v_b_conv_w), b_conv_b=(b_conv_b, m_b_conv_b, v_b_conv_b),
                 b_ln_g=(b_ln_g, m_b_ln_g, v_b_ln_g), b_ln_b=(b_ln_b, m_b_ln_b, v_b_ln_b),
                 norm2_g=(norm2_g, m_norm2_g, v_norm2_g), final_g=(final_g, m_final_g, v_final_g))

    def padded(a):
        rows = -(-a.size // (8 * LANES)) * 8
        return jnp.pad(a.reshape(-1), (0, rows * LANES - a.size)).reshape(rows, LANES)

    packs = [_pack([padded(given[k][j]) for k in small_names]) for j in range(3)]
    gpack = _pack([padded(sg[k].astype(F32)) for k in small_names])
    res_small = _adamw("adamw_small", packs[0], packs[1], packs[2], gpack)
    out = {}
    for j, kind in enumerate(["grad", "delta", "new_m", "new_v"]):
        r = 0
        for k in small_names:
            a = given[k][0]
            rows = -(-a.size // (8 * LANES)) * 8
            out[(kind, k)] = res_small[j][r:r + rows].reshape(-1)[:a.size].reshape(a.shape)
            r += rows

    res = _adamw("adamw_w_ada", w_ada.reshape(nl * d, cq), m_w_ada.reshape(nl * d, cq), v_w_ada.reshape(nl * d, cq),
                 g_w_ada.reshape(nl * d, cq))
    for kind, r in zip(["grad", "delta", "new_m", "new_v"], res):
        out[(kind, "w_ada")] = r.reshape(w_ada.shape)

    sums, others = _split_wait("swap_wait", "swap", swap_send, swap_recv, sums, others, res[0])
    for k, s_mine, s_other in zip(big, sums, others):
        w, m, v = ws_given[k]
        cols = w.shape[-1]
        res = _adamw(f"adamw_{k}", w.reshape(-1, cols), m.reshape(-1, cols), v.reshape(-1, cols), s_mine, s_other)
        for kind, r in zip(["grad", "delta", "new_m", "new_v"], res):
            out[(kind, k)] = r.reshape(w.shape)

    order = ["w_ada", "b_ada", "norm1_g", "w_in", "a_ln_g", "a_ln_b", "a_ws", "a_bs", "w_pa", "b_conv_w", "b_conv_b",
             "b_ln_g", "b_ln_b", "w_pb", "w_out", "norm2_g", "w_ff1", "w_ff2", "final_g"]
    return (loss, grad_x, *[out[("grad", k)] for k in order], *[out[("delta", k)] for k in order],
            *[out[("new_m", k)] for k in order], *[out[("new_v", k)] for k in order])
```

```python
import functools

import jax
import jax.numpy as jnp
from jax import lax
from jax.experimental import pallas as pl
from jax.experimental.pallas import tpu as pltpu

F32 = jnp.float32
MXU_DTYPE = jnp.bfloat16
ACT_DTYPE = jnp.bfloat16
WIRE_DTYPE = jnp.bfloat16

EPS = 1e-6
CHUNK = 128
HEADS = 8
CONV_TAPS = 31
HALO = 32
N_DEV = 8
N_CHIP = 4
ADAM_LR, ADAM_B1, ADAM_B2, ADAM_EPS, ADAM_WD, ADAM_STEP = 0.001, 0.9, 0.999, 1e-08, 0.01, 10

V7X_VMEM_BYTES = 64 * 1024 * 1024
VMEM_LIMIT = V7X_VMEM_BYTES * 7 // 8
TOKEN_TILE = 512
MATMUL_TILE = 2048
FFN_BWD_TILE = 256
CONV_ROWS = 64
TAP_GRAD_ROWS = 32
LANES = 128
MESH_ID = pl.DeviceIdType.MESH


def _params(sem=None):
    return pltpu.CompilerParams(dimension_semantics=sem, vmem_limit_bytes=VMEM_LIMIT)


def _resident(shape):
    return pl.BlockSpec(shape, lambda *_: (0,) * len(shape), pipeline_mode=pl.Buffered(1))


def _dot(a, b):
    return jnp.dot(a.astype(MXU_DTYPE), b.astype(MXU_DTYPE), preferred_element_type=F32)


def _dot_nt(a, b):
    return lax.dot_general(a.astype(MXU_DTYPE), b.astype(MXU_DTYPE), (((1,), (1,)), ((), ())),
                           preferred_element_type=F32)


def _dot_tn(a, b):
    return lax.dot_general(a.astype(MXU_DTYPE), b.astype(MXU_DTYPE), (((0,), (0,)), ((), ())),
                           preferred_element_type=F32)


def _colsum(a):
    return jnp.sum(a, axis=0, keepdims=True)


def _rowmean(a):
    return jnp.mean(a, axis=-1, keepdims=True)


def _sigmoid(a):
    return 1.0 / (1.0 + jnp.exp(-a))


def _modnorm_fwd(x, g, sc, sh):
    r = lax.rsqrt(_rowmean(x * x) + EPS)
    return (x * r) * (g * (1.0 + sc)) + sh


def _modnorm_bwd(x, dh, g, sc):
    r = lax.rsqrt(_rowmean(x * x) + EPS)
    xn = x * r
    dxn = dh * (g * (1.0 + sc))
    dx = r * (dxn - xn * _rowmean(dxn * xn))
    return dx, _colsum(dh), _colsum(dh * xn)


def _ln_stats(v):
    mu = _rowmean(v)
    vc = v - mu
    rstd = lax.rsqrt(_rowmean(vc * vc) + EPS)
    return vc * rstd, rstd


def _ln_bwd(dy, vhat, rstd, g):
    dvh = dy * g
    return rstd * (dvh - _rowmean(dvh) - vhat * _rowmean(dvh * vhat))


def _causal_mask():
    row = lax.broadcasted_iota(jnp.int32, (CHUNK, CHUNK), 0)
    col = lax.broadcasted_iota(jnp.int32, (CHUNK, CHUNK), 1)
    return row >= col


def _my_place():
    return lax.axis_index("x"), lax.axis_index("y"), lax.axis_index("c")


def _all_to_all(x, name, reduce=False):
    n, rows, cols = x.shape
    assert n == N_DEV

    def body(x_ref, o_ref, *scratch):
        if reduce:
            land, send_sems, recv_sems = scratch
        else:
            land = o_ref
            send_sems, recv_sems = scratch
        mx, my, mc = _my_place()
        me = 4 * mx + 2 * my + mc
        land[me] = x_ref[me]
        copies = []
        for k in range(1, N_DEV):
            px = (mx + ((k >> 2) & 1)) % 2
            py = (my + ((k >> 1) & 1)) % 2
            pc = (mc + (k & 1)) % 2
            peer = 4 * px + 2 * py + pc
            cp = pltpu.make_async_remote_copy(
                src_ref=x_ref.at[peer], dst_ref=land.at[me],
                send_sem=send_sems.at[k - 1], recv_sem=recv_sems.at[k - 1],
                device_id=(px, py, pc), device_id_type=MESH_ID)
            cp.start()
            copies.append(cp)
        for cp in copies:
            cp.wait()
        if reduce:
            acc = land[0]
            for s in range(1, N_DEV):
                acc = acc + land[s]
            o_ref[...] = acc

    scratch = [pltpu.SemaphoreType.DMA((N_DEV - 1,)), pltpu.SemaphoreType.DMA((N_DEV - 1,))]
    if reduce:
        scratch = [pltpu.VMEM((N_DEV, rows, cols), x.dtype)] + scratch
        out_shape = jax.ShapeDtypeStruct((rows, cols), x.dtype)
    else:
        out_shape = jax.ShapeDtypeStruct(x.shape, x.dtype)
    return pl.pallas_call(
        body, name=name, out_shape=out_shape,
        in_specs=[pl.BlockSpec(memory_space=pltpu.VMEM)],
        out_specs=pl.BlockSpec(memory_space=pltpu.VMEM),
        scratch_shapes=scratch,
        compiler_params=pltpu.CompilerParams(vmem_limit_bytes=VMEM_LIMIT),
    )(x)


def _other_chips(mx, my):
    return [(1 - mx, my), (mx, 1 - my), (1 - mx, 1 - my)]


HBM_SPEC = pl.BlockSpec(memory_space=pltpu.HBM)
SEM_SPEC = pl.BlockSpec(memory_space=pltpu.SEMAPHORE)
ANY_SPEC = pl.BlockSpec(memory_space=pl.ANY)
SPLIT_EFFECT = pltpu.SideEffectType.DATAFLOW_SIDE_EFFECTING


def _quarter_copies(mode, srcs, lands, send_sems, recv_sems, peers=(0, 1, 2)):
    mx, my, mc = _my_place()
    myq = 2 * mx + my
    if mode == "swap":
        return [pltpu.make_async_remote_copy(
            src_ref=srcs[a], dst_ref=lands[a], send_sem=send_sems[a].at[0], recv_sem=recv_sems[a].at[0],
            device_id=(mx, my, 1 - mc), device_id_type=MESH_ID) for a in range(len(lands))]
    copies = []
    for a in range(len(lands)):
        for k, (px, py) in enumerate(_other_chips(mx, my)):
            if k not in peers:
                continue
            if mode == "gather":
                src, dst = lands[a].at[myq], lands[a].at[myq]
            else:
                src, dst = srcs[a].at[2 * px + py], lands[a].at[k]
            copies.append(pltpu.make_async_remote_copy(
                src_ref=src, dst_ref=dst, send_sem=send_sems[a].at[k], recv_sem=recv_sems[a].at[k],
                device_id=(px, py, mc), device_id_type=MESH_ID))
    return copies


def _split_start(name, mode, srcs, lands, after):
    ns, n = len(srcs), len(lands)

    def body(*refs):
        outs = refs[ns + n + 1:]
        for cp in _quarter_copies(mode, refs[:ns], refs[ns:ns + n], outs[:n], outs[n:2 * n]):
            cp.start()
        token = outs[-1]
        token[...] = jnp.zeros_like(token)

    arrays = list(srcs) + list(lands)
    per_array = 1 if mode == "swap" else 3
    res = pl.pallas_call(
        body, name=name,
        out_shape=[pltpu.SemaphoreType.DMA((per_array,))] * (2 * n) + [pltpu.HBM(x.shape, x.dtype) for x in arrays]
        + [jax.ShapeDtypeStruct((8, LANES), F32)],
        in_specs=[HBM_SPEC] * (ns + n) + [ANY_SPEC],
        out_specs=[SEM_SPEC] * (2 * n) + [HBM_SPEC] * (ns + n) + [pl.BlockSpec(memory_space=pltpu.VMEM)],
        input_output_aliases={i: 2 * n + i for i in range(ns + n)},
        compiler_params=pltpu.CompilerParams(has_side_effects=SPLIT_EFFECT),
    )(*[pltpu.with_memory_space_constraint(x, pltpu.HBM) for x in arrays], after)
    return res[:n], res[n:2 * n], res[2 * n:2 * n + ns], res[2 * n + ns:2 * n + ns + n], res[-1]


def _split_wait(name, mode, send_sems, recv_sems, srcs, lands, after, peers=(0, 1, 2)):
    ns, n = len(srcs), len(lands)

    def body(*refs):
        sems = refs[ns + n:ns + 3 * n]
        for cp in _quarter_copies(mode, refs[:ns], refs[ns:ns + n], sems[:n], sems[n:], peers):
            cp.wait_send()
            cp.wait_recv()

    arrays = list(srcs) + list(lands)
    res = pl.pallas_call(
        body, name=name,
        out_shape=[pltpu.HBM(x.shape, x.dtype) for x in arrays],
        in_specs=[HBM_SPEC] * (ns + n) + [SEM_SPEC] * (2 * n) + [ANY_SPEC],
        out_specs=[HBM_SPEC] * (ns + n),
        input_output_aliases={i: i for i in range(ns + n)},
        compiler_params=pltpu.CompilerParams(has_side_effects=SPLIT_EFFECT),
    )(*arrays, *send_sems, *recv_sems, after)
    return res[:ns], res[ns:]


def _ada_forward(c_all, w_ada, b_ada3, myq):
    nl, d, cq = w_ada.shape
    nb = c_all.shape[0]

    def body(q_ref, c_ref, w_ref, b_ref, o_ref):
        c = c_ref[...]
        act = c * _sigmoid(c)
        o_ref[...] = _dot(act, w_ref[...]) + b_ref[...]

    return pl.pallas_call(
        body, name="ada_forward",
        out_shape=jax.ShapeDtypeStruct((nl, nb, cq), F32),
        grid_spec=pltpu.PrefetchScalarGridSpec(
            num_scalar_prefetch=1, grid=(nl,),
            in_specs=[pl.BlockSpec((nb, d), lambda l, q: (0, 0)),
                      pl.BlockSpec((None, d, cq), lambda l, q: (l, 0, 0)),
                      pl.BlockSpec((None, 1, cq), lambda l, q: (l, 0, q[0]))],
            out_specs=pl.BlockSpec((None, nb, cq), lambda l, q: (l, 0, 0))),
        compiler_params=_params(("arbitrary",)),
    )(myq, c_all, w_ada, b_ada3)


def _ada_backward(c_all, dmod_all, myq, cq, after):
    nb, d = c_all.shape
    nl = dmod_all.shape[0]
    full = dmod_all.shape[2]

    def body(q_ref, c_ref, dq_ref, dall_ref, after_ref, gw_ref, gb_ref):
        c = c_ref[...]
        act = c * _sigmoid(c)
        gw_ref[...] = _dot_tn(act, dq_ref[...])
        gb_ref[...] = _colsum(dall_ref[...])

    return pl.pallas_call(
        body, name="ada_backward",
        out_shape=[jax.ShapeDtypeStruct((nl, d, cq), F32), jax.ShapeDtypeStruct((nl, 1, full), F32)],
        grid_spec=pltpu.PrefetchScalarGridSpec(
            num_scalar_prefetch=1, grid=(nl,),
            in_specs=[pl.BlockSpec((nb, d), lambda l, q: (0, 0)),
                      pl.BlockSpec((None, nb, cq), lambda l, q: (l, 0, q[0])),
                      pl.BlockSpec((None, nb, full), lambda l, q: (l, 0, 0)), ANY_SPEC],
            out_specs=[pl.BlockSpec((None, d, cq), lambda l, q: (l, 0, 0)),
                       pl.BlockSpec((None, 1, full), lambda l, q: (l, 0, 0))]),
        compiler_params=_params(("arbitrary",)),
    )(myq, c_all, dmod_all, dmod_all, after)


def _in_proj(l, x, mod, g1, wg_in, t_len):
    n, d = x.shape
    tm = min(TOKEN_TILE, t_len)
    tpb = t_len // tm
    qc = wg_in.shape[-1]

    def body(x_ref, mod_ref, g_ref, w_ref, h_ref, proj_ref):
        h = _modnorm_fwd(x_ref[...], g_ref[...], mod_ref[1:2, :], mod_ref[0:1, :]).astype(MXU_DTYPE)
        h_ref[...] = h.astype(ACT_DTYPE)
        for q in range(N_CHIP):
            proj_ref[:, q * qc:(q + 1) * qc] = jnp.dot(h, w_ref[q], preferred_element_type=F32).astype(ACT_DTYPE)

    return pl.pallas_call(
        body, name=f"in_proj_{l}",
        out_shape=[jax.ShapeDtypeStruct((n, d), ACT_DTYPE), jax.ShapeDtypeStruct((n, N_CHIP * qc), ACT_DTYPE)],
        grid=(n // tm,),
        in_specs=[pl.BlockSpec((tm, d), lambda i: (i, 0)),
                  pl.BlockSpec((None, None, 8, d), lambda i: (l, i // tpb, 0, 0)),
                  pl.BlockSpec((None, 1, d), lambda i: (l, 0, 0)),
                  _resident((N_CHIP, d, qc))],
        out_specs=[pl.BlockSpec((tm, d), lambda i: (i, 0)),
                   pl.BlockSpec((tm, N_CHIP * qc), lambda i: (i, 0))],
        compiler_params=_params(("arbitrary",)),
    )(x, mod, g1, wg_in)


def _in_proj_quarter(name, l, src, mod, g1, wg_in, quarter, t_len, prev=None):
    n, d = src.shape
    tm = min(TOKEN_TILE, t_len)
    tpb = t_len // tm
    qc = wg_in.shape[-1]
    first = prev is None

    def body(q_ref, *refs):
        if first:
            x_ref, mod_ref, g_ref, w_ref, h_ref, proj_ref = refs
            h = _modnorm_fwd(x_ref[...], g_ref[...], mod_ref[1:2, :], mod_ref[0:1, :]).astype(MXU_DTYPE)
            h_ref[...] = h.astype(ACT_DTYPE)
        else:
            h_in_ref, w_ref, _, proj_ref = refs
            h = h_in_ref[...].astype(MXU_DTYPE)
        proj_ref[...] = jnp.dot(h, w_ref[...], preferred_element_type=F32).astype(ACT_DTYPE)

    tile = pl.BlockSpec((tm, d), lambda i, q: (i, 0))
    w_spec = pl.BlockSpec((None, d, qc), lambda i, q: (q[0], 0, 0))
    proj_spec = pl.BlockSpec((tm, qc), lambda i, q: (i, q[0]))
    proj_shape = jax.ShapeDtypeStruct((n, N_CHIP * qc), ACT_DTYPE)
    if first:
        operands = [quarter, src, mod, g1, wg_in]
        in_specs = [tile, pl.BlockSpec((None, None, 8, d), lambda i, q: (l, i // tpb, 0, 0)),
                    pl.BlockSpec((None, 1, d), lambda i, q: (l, 0, 0)), w_spec]
        out_shape, out_specs, aliases = [jax.ShapeDtypeStruct((n, d), ACT_DTYPE), proj_shape], [tile, proj_spec], {}
    else:
        operands = [quarter, src, wg_in, prev]
        in_specs = [tile, w_spec, ANY_SPEC]
        out_shape, out_specs, aliases = proj_shape, proj_spec, {3: 0}
    return pl.pallas_call(
        body, name=name, out_shape=out_shape,
        grid_spec=pltpu.PrefetchScalarGridSpec(num_scalar_prefetch=1, grid=(n // tm,), in_specs=in_specs,
                                               out_specs=out_specs),
        input_output_aliases=aliases,
        compiler_params=_params(("arbitrary",)),
    )(*operands)


def _masked_ws(ws_ref, wm_s):
    mask = _causal_mask()
    for h in range(HEADS):
        wm_s[h] = jnp.where(mask, ws_ref[h], 0.0).astype(MXU_DTYPE)


def _fill_z(i, tpb, a_ref, g_ref, ah_ref, gh_ref, zext):
    ah = ah_ref[...].astype(F32)
    gh = gh_ref[...].astype(F32)
    keep = jnp.where(i % tpb == 0, 0.0, 1.0)
    _put_lanes(zext, slice(0, HALO), ah * _sigmoid(gh) * keep)


def _put_lanes(dst3, rows, value):
    for lc in range(value.shape[-1] // LANES):
        dst3[lc, rows, :] = value[:, lc * LANES:(lc + 1) * LANES]


def _tap_windows(src3, lc, base, rows, flip):
    offs = {k: (CONV_TAPS - 1 - k) if flip else (k + 2) for k in range(CONV_TAPS)}
    for r in range(8):
        taps = [k for k in offs if offs[k] % 8 == r]
        lo = min(offs[k] for k in taps)
        hi = max(offs[k] for k in taps)
        win = src3[lc, pl.ds(base + lo, hi - lo + rows), :]
        for k in taps:
            yield k, win[offs[k] - lo:offs[k] - lo + rows]


def _conv_taps(src3, w3_ref, dst3, lc, nrows, flip):
    for b in range(nrows // CONV_ROWS):
        acc = jnp.zeros((CONV_ROWS, LANES), F32)
        for k, win in _tap_windows(src3, lc, b * CONV_ROWS, CONV_ROWS, flip):
            acc = acc + win * w3_ref[lc, k:k + 1, :]
        dst3[lc, b * CONV_ROWS:(b + 1) * CONV_ROWS, :] = acc


def _branches_fwd(l, proj, lng, lnb, ws, bst, cw, cb, blg, blb, t_len):
    n = proj.shape[0]
    d = lng.shape[-1]
    tm = min(TOKEN_TILE, t_len)
    tpb = t_len // tm
    per = tm // HALO
    nchunk = tm // CHUNK

    def body(u_ref, v_ref, a_ref, g_ref, ah_ref, gh_ref, lng_ref, lnb_ref, ws_ref, bst_ref, cw_ref, cb_ref,
             blg_ref, blb_ref, ya_ref, yb_ref, zc_ref, wm_s, zext, zc3):
        i = pl.program_id(0)
        _masked_ws(ws_ref, wm_s)
        _fill_z(i, tpb, a_ref, g_ref, ah_ref, gh_ref, zext)

        def chunk(c, carry):
            r0 = pl.multiple_of(c * CHUNK, CHUNK)
            rows = pl.ds(r0, CHUNK)
            vhat, _ = _ln_stats(v_ref[rows, :].astype(F32))
            vn = (vhat * lng_ref[...] + lnb_ref[...]).astype(MXU_DTYPE)
            u = u_ref[rows, :].astype(F32)
            for h in range(HEADS):
                cols = slice(h * CHUNK, (h + 1) * CHUNK)
                s = jnp.dot(wm_s[h], vn[:, cols], preferred_element_type=F32) + bst_ref[:, h:h + 1]
                ya_ref[rows, cols] = (u[:, cols] * s).astype(ACT_DTYPE)
            a = a_ref[rows, :].astype(F32)
            g = g_ref[rows, :].astype(F32)
            _put_lanes(zext, pl.ds(HALO + r0, CHUNK), a * _sigmoid(g))
            return carry

        lax.fori_loop(0, nchunk, chunk, 0)

        def lane_chunk(lc, carry):
            _conv_taps(zext, cw_ref, zc3, lc, tm, flip=False)
            return carry

        lax.fori_loop(0, d // LANES, lane_chunk, 0)

        def chunk2(c, carry):
            r0 = pl.multiple_of(c * CHUNK, CHUNK)
            rows = pl.ds(r0, CHUNK)
            for lc in range(d // LANES):
                lanes = slice(lc * LANES, (lc + 1) * LANES)
                zc_ref[rows, lanes] = (zc3[lc, rows, :] + cb_ref[:, lanes]).astype(ACT_DTYPE)
            zhat, _ = _ln_stats(zc_ref[rows, :].astype(F32))
            zn = zhat * blg_ref[...] + blb_ref[...]
            yb_ref[rows, :] = (zn * _sigmoid(zn)).astype(ACT_DTYPE)
            return carry

        lax.fori_loop(0, nchunk, chunk2, 0)

    col = lambda k: pl.BlockSpec((tm, d), lambda i: (i, k))
    halo = lambda k: pl.BlockSpec((HALO, d), lambda i: (jnp.maximum(i * per - 1, 0), k))
    vec = pl.BlockSpec((None, 1, d), lambda i: (l, 0, 0))
    out = pl.BlockSpec((tm, d), lambda i: (i, 0))
    return pl.pallas_call(
        body, name=f"branches_fwd_{l}",
        out_shape=[jax.ShapeDtypeStruct((n, d), ACT_DTYPE)] * 3,
        grid=(n // tm,),
        in_specs=[col(0), col(1), col(2), col(3), halo(2), halo(3), vec, vec,
                  pl.BlockSpec((None, HEADS, CHUNK, CHUNK), lambda i: (l, 0, 0, 0)),
                  pl.BlockSpec((None, CHUNK, HEADS), lambda i: (l, 0, 0)),
                  pl.BlockSpec((None, d // LANES, HALO, LANES), lambda i: (l, 0, 0, 0)), vec, vec, vec],
        out_specs=[out, out, out],
        scratch_shapes=[pltpu.VMEM((HEADS, CHUNK, CHUNK), MXU_DTYPE), pltpu.VMEM((d // LANES, HALO + tm, LANES), F32),
                        pltpu.VMEM((d // LANES, tm, LANES), F32)],
        compiler_params=_params(("arbitrary",)),
    )(proj, proj, proj, proj, proj, proj, lng, lnb, ws, bst, cw, cb, blg, blb)


def _merge_out(l, x, mod, proj, ya_in, yb_in, wg_pa, wg_pb, wg_out, t_len):
    n, d = x.shape
    tm = min(TOKEN_TILE, t_len)
    tpb = t_len // tm
    rq = d // N_CHIP

    def body(x_ref, mod_ref, ga_ref, gb_ref, yai_ref, ybi_ref, wpa_ref, wpb_ref, wo_ref,
             ya_ref, yb_ref, mg_ref, o_ref, x1_ref):
        wpa = wpa_ref[...].reshape(d, d)
        wpb = wpb_ref[...].reshape(d, d)
        wo = wo_ref[...].reshape(d, d)
        ya = jnp.dot(yai_ref[...].astype(MXU_DTYPE), wpa, preferred_element_type=F32)
        yb = jnp.dot(ybi_ref[...].astype(MXU_DTYPE), wpb, preferred_element_type=F32)
        merged = _sigmoid(ga_ref[...].astype(F32)) * ya + _sigmoid(gb_ref[...].astype(F32)) * yb
        o = _dot(merged, wo)
        ya_ref[...] = ya.astype(ACT_DTYPE)
        yb_ref[...] = yb.astype(ACT_DTYPE)
        mg_ref[...] = merged.astype(ACT_DTYPE)
        o_ref[...] = o.astype(ACT_DTYPE)
        x1_ref[...] = x_ref[...] + mod_ref[2:3, :] * o

    tile = pl.BlockSpec((tm, d), lambda i: (i, 0))
    wspec = pl.BlockSpec((N_CHIP, rq, d), lambda i: (0, 0, 0))
    return pl.pallas_call(
        body, name=f"merge_out_{l}",
        out_shape=[jax.ShapeDtypeStruct((n, d), ACT_DTYPE)] * 4 + [jax.ShapeDtypeStruct((n, d), F32)],
        grid=(n // tm,),
        in_specs=[tile, pl.BlockSpec((None, None, 8, d), lambda i: (l, i // tpb, 0, 0)),
                  pl.BlockSpec((tm, d), lambda i: (i, 4)), pl.BlockSpec((tm, d), lambda i: (i, 5)),
                  tile, tile, wspec, wspec, wspec],
        out_specs=[tile] * 5,
        compiler_params=_params(("arbitrary",)),
    )(x, mod, proj, proj, ya_in, yb_in, wg_pa, wg_pb, wg_out)


def _ffn_fwd(l, x1, mod, g2, wg_ff1, wg_ff2, t_len):
    n, d = x1.shape
    tm = min(TOKEN_TILE, t_len)
    tpb = t_len // tm
    hq = wg_ff1.shape[-1]
    hid = N_CHIP * hq

    def body(x_ref, mod_ref, g_ref, w1_ref, w2_ref, h_ref, f_ref, o2_ref, x2_ref, a2_s):
        h = _modnorm_fwd(x_ref[...], g_ref[...], mod_ref[4:5, :], mod_ref[3:4, :]).astype(MXU_DTYPE)
        h_ref[...] = h.astype(ACT_DTYPE)
        for q in range(N_CHIP):
            cols = slice(q * hq, (q + 1) * hq)
            f = jnp.dot(h, w1_ref[q], preferred_element_type=F32)
            f_ref[:, cols] = f.astype(ACT_DTYPE)
            a2_s[:, cols] = jnp.square(jnp.maximum(f, 0.0)).astype(MXU_DTYPE)
        o2 = jnp.dot(a2_s[...], w2_ref[...].reshape(hid, d), preferred_element_type=F32)
        o2_ref[...] = o2.astype(ACT_DTYPE)
        x2_ref[...] = x_ref[...] + mod_ref[5:6, :] * o2

    tile = pl.BlockSpec((tm, d), lambda i: (i, 0))
    return pl.pallas_call(
        body, name=f"ffn_fwd_{l}",
        out_shape=[jax.ShapeDtypeStruct((n, d), ACT_DTYPE), jax.ShapeDtypeStruct((n, hid), ACT_DTYPE),
                   jax.ShapeDtypeStruct((n, d), ACT_DTYPE), jax.ShapeDtypeStruct((n, d), F32)],
        grid=(n // tm,),
        in_specs=[tile, pl.BlockSpec((None, None, 8, d), lambda i: (l, i // tpb, 0, 0)),
                  pl.BlockSpec((None, 1, d), lambda i: (l, 0, 0)),
                  _resident((N_CHIP, d, hq)), _resident((N_CHIP, hq, d))],
        out_specs=[tile, pl.BlockSpec((tm, hid), lambda i: (i, 0)), tile, tile],
        scratch_shapes=[pltpu.VMEM((tm, hid), MXU_DTYPE)],
        compiler_params=_params(("arbitrary",)),
    )(x1, mod, g2, wg_ff1, wg_ff2)


def _loss_head(x, final_g, target):
    n, d = x.shape
    tm = min(TOKEN_TILE, n)

    def body(x_ref, g_ref, t_ref, loss_ref, dx_ref, dg_ref):
        @pl.when(pl.program_id(0) == 0)
        def _():
            loss_ref[...] = jnp.zeros_like(loss_ref)
            dg_ref[...] = jnp.zeros_like(dg_ref)

        x_t = x_ref[...]
        g = g_ref[...]
        r = lax.rsqrt(_rowmean(x_t * x_t) + EPS)
        xn = x_t * r
        e = xn * g - t_ref[...]
        loss_ref[...] += jnp.sum(e * e) * (0.5 / d)
        dy = e * (1.0 / d)
        dxn = dy * g
        dx_ref[...] = r * (dxn - xn * _rowmean(dxn * xn))
        dg_ref[0:1, :] += _colsum(dy * xn)

    tile = pl.BlockSpec((tm, d), lambda i: (i, 0))
    return pl.pallas_call(
        body, name="loss_head",
        out_shape=[jax.ShapeDtypeStruct((8, LANES), F32), jax.ShapeDtypeStruct((n, d), F32),
                   jax.ShapeDtypeStruct((8, d), F32)],
        grid=(n // tm,),
        in_specs=[tile, pl.BlockSpec((1, d), lambda i: (0, 0)), tile],
        out_specs=[pl.BlockSpec((8, LANES), lambda i: (0, 0)), tile, pl.BlockSpec((8, d), lambda i: (0, 0))],
        compiler_params=_params(("arbitrary",)),
    )(x, final_g, target)


def _norm_tail(i, tpb, x_ref, dxin_ref, dh, g_ref, sc, dx_ref, dmod_ref, dg_ref, row_sh, row_sc):
    dxm, dsh, q = _modnorm_bwd(x_ref[...], dh, g_ref[...], sc)
    dx_ref[...] = dxin_ref[...] + dxm
    dmod_ref[row_sh:row_sh + 1, :] += dsh
    dmod_ref[row_sc:row_sc + 1, :] += g_ref[...] * q
    dg_ref[0:1, :] += (1.0 + sc) * q


def _ffn_bwd(l, dx2, x1, mod, g2, o2, f, wg_ff1, wg_ff2, t_len, nb):
    n, d = dx2.shape
    tm = min(FFN_BWD_TILE, t_len)
    tpb = t_len // tm
    hq = wg_ff1.shape[-1]
    hid = N_CHIP * hq

    def body(dx2_ref, x1_ref, mod_ref, g_ref, o2_ref, f_ref, w1_ref, w2_ref,
             do2_ref, df_ref, dx1_ref, dmod_ref, dg_ref):
        i = pl.program_id(0)

        @pl.when(i == 0)
        def _():
            dg_ref[...] = jnp.zeros_like(dg_ref)

        @pl.when(i % tpb == 0)
        def _():
            dmod_ref[...] = jnp.zeros_like(dmod_ref)

        dx2_t = dx2_ref[...]
        dmod_ref[5:6, :] += _colsum(dx2_t * o2_ref[...].astype(F32))
        do2 = (dx2_t * mod_ref[5:6, :]).astype(MXU_DTYPE)
        do2_ref[...] = do2.astype(ACT_DTYPE)
        dh = jnp.zeros((tm, d), F32)
        for q in range(N_CHIP):
            cols = slice(q * hq, (q + 1) * hq)
            da2 = _dot_nt(do2, w2_ref[q])
            df = (da2 * (2.0 * jnp.maximum(f_ref[:, cols].astype(F32), 0.0))).astype(MXU_DTYPE)
            df_ref[:, cols] = df.astype(ACT_DTYPE)
            dh = dh + _dot_nt(df, w1_ref[q])
        _norm_tail(i, tpb, x1_ref, dx2_ref, dh, g_ref, mod_ref[4:5, :], dx1_ref, dmod_ref, dg_ref, 3, 4)

    tile = pl.BlockSpec((tm, d), lambda i: (i, 0))
    wide = pl.BlockSpec((tm, hid), lambda i: (i, 0))
    return pl.pallas_call(
        body, name=f"ffn_bwd_{l}",
        out_shape=[jax.ShapeDtypeStruct((n, d), ACT_DTYPE), jax.ShapeDtypeStruct((n, hid), ACT_DTYPE),
                   jax.ShapeDtypeStruct((n, d), F32), jax.ShapeDtypeStruct((nb, 8, d), F32),
                   jax.ShapeDtypeStruct((8, d), F32)],
        grid=(n // tm,),
        in_specs=[tile, tile, pl.BlockSpec((None, None, 8, d), lambda i: (l, i // tpb, 0, 0)),
                  pl.BlockSpec((None, 1, d), lambda i: (l, 0, 0)), tile, wide,
                  _resident((N_CHIP, d, hq)), _resident((N_CHIP, hq, d))],
        out_specs=[tile, wide, tile, pl.BlockSpec((None, 8, d), lambda i: (i // tpb, 0, 0)),
                   pl.BlockSpec((8, d), lambda i: (0, 0))],
        compiler_params=_params(("arbitrary",)),
    )(dx2, x1, mod, g2, o2, f, wg_ff1, wg_ff2)


def _merge_bwd(l, dx1, mod, o, ya, yb, proj, wg_pa, wg_pb, wg_out, t_len, nb, after):
    n, d = dx1.shape
    tm = min(TOKEN_TILE, t_len)
    tpb = t_len // tm
    rq = d // N_CHIP

    def body(dx_ref, mod_ref, o_ref, ya_ref, yb_ref, ga_ref, gb_ref, wpa_ref, wpb_ref, wo_ref, after_ref,
             do_ref, dya_ref, dyb_ref, dyai_ref, dybi_ref, dproj_ref, dmod_ref):
        i = pl.program_id(0)

        @pl.when(i % tpb == 0)
        def _():
            dmod_ref[...] = jnp.zeros_like(dmod_ref)

        dx = dx_ref[...]
        dmod_ref[2:3, :] += _colsum(dx * o_ref[...].astype(F32))
        do = (dx * mod_ref[2:3, :]).astype(MXU_DTYPE)
        do_ref[...] = do.astype(ACT_DTYPE)
        dm = _dot_nt(do, wo_ref[...].reshape(d, d))
        sa = _sigmoid(ga_ref[...].astype(F32))
        sb = _sigmoid(gb_ref[...].astype(F32))
        dya = (dm * sa).astype(MXU_DTYPE)
        dyb = (dm * sb).astype(MXU_DTYPE)
        dya_ref[...] = dya.astype(ACT_DTYPE)
        dyb_ref[...] = dyb.astype(ACT_DTYPE)
        dproj_ref[:, 0:d] = (dm * ya_ref[...].astype(F32) * sa * (1.0 - sa)).astype(ACT_DTYPE)
        dproj_ref[:, d:2 * d] = (dm * yb_ref[...].astype(F32) * sb * (1.0 - sb)).astype(ACT_DTYPE)
        dyai_ref[...] = _dot_nt(dya, wpa_ref[...].reshape(d, d)).astype(ACT_DTYPE)
        dybi_ref[...] = _dot_nt(dyb, wpb_ref[...].reshape(d, d)).astype(ACT_DTYPE)

    tile = pl.BlockSpec((tm, d), lambda i: (i, 0))
    wspec = pl.BlockSpec((N_CHIP, rq, d), lambda i: (0, 0, 0))
    return pl.pallas_call(
        body, name=f"merge_bwd_{l}",
        out_shape=[jax.ShapeDtypeStruct((n, d), ACT_DTYPE)] * 5
        + [jax.ShapeDtypeStruct((n, 6 * d), ACT_DTYPE), jax.ShapeDtypeStruct((nb, 8, d), F32)],
        grid=(n // tm,),
        in_specs=[tile, pl.BlockSpec((None, None, 8, d), lambda i: (l, i // tpb, 0, 0)), tile, tile, tile,
                  pl.BlockSpec((tm, d), lambda i: (i, 4)), pl.BlockSpec((tm, d), lambda i: (i, 5)),
                  wspec, wspec, wspec, ANY_SPEC],
        out_specs=[tile] * 5 + [pl.BlockSpec((tm, 2 * d), lambda i: (i, 2)),
                                pl.BlockSpec((None, 8, d), lambda i: (i // tpb, 0, 0))],
        compiler_params=_params(("arbitrary",)),
    )(dx1, mod, o, ya, yb, proj, proj, wg_pa, wg_pb, wg_out, after)


def _branches_bwd(l, proj, zc, dya_in, dyb_in, dproj, lng, lnb, ws, bst, cw, blg, blb, t_len, after):
    n = proj.shape[0]
    d = lng.shape[-1]
    tm = min(TOKEN_TILE, t_len)
    tpb = t_len // tm
    per = tm // HALO
    nchunk = tm // CHUNK
    ntile = n // tm

    def body(u_ref, v_ref, a_ref, g_ref, ah_ref, gh_ref, zc_ref, zcn_ref, dya_ref, dyb_ref, dybn_ref, dproj_in,
             lng_ref, lnb_ref, ws_ref, bst_ref, cw_ref, blg_ref, blb_ref, after_ref,
             dproj_ref, dws_ref, dbst_ref, dcw_ref, vec_ref, wm_s, zext, dzext, dz3, dvn_s):
        i = pl.program_id(0)

        @pl.when(i == 0)
        def _():
            dws_ref[...] = jnp.zeros_like(dws_ref)
            dbst_ref[...] = jnp.zeros_like(dbst_ref)
            dcw_ref[...] = jnp.zeros_like(dcw_ref)
            vec_ref[...] = jnp.zeros_like(vec_ref)

        _masked_ws(ws_ref, wm_s)
        _fill_z(i, tpb, a_ref, g_ref, ah_ref, gh_ref, zext)

        def conv_ln_bwd(zc_t, dyb_t):
            zhat, rstd = _ln_stats(zc_t)
            zn = zhat * blg_ref[...] + blb_ref[...]
            sg = _sigmoid(zn)
            dzn = dyb_t * (sg * (1.0 + zn * (1.0 - sg)))
            return _ln_bwd(dzn, zhat, rstd, blg_ref[...]), _colsum(dzn * zhat), _colsum(dzn)

        def chunk(c, carry):
            r0 = pl.multiple_of(c * CHUNK, CHUNK)
            rows = pl.ds(r0, CHUNK)
            vhat, rstd = _ln_stats(v_ref[rows, :].astype(F32))
            vn = (vhat * lng_ref[...] + lnb_ref[...]).astype(MXU_DTYPE)
            u = u_ref[rows, :].astype(F32)
            dya = dya_ref[rows, :].astype(F32)
            for h in range(HEADS):
                cols = slice(h * CHUNK, (h + 1) * CHUNK)
                s = jnp.dot(wm_s[h], vn[:, cols], preferred_element_type=F32) + bst_ref[:, h:h + 1]
                dproj_ref[rows, cols] = (dya[:, cols] * s).astype(ACT_DTYPE)
                ds = dya[:, cols] * u[:, cols]
                dvn_s[:, cols] = _dot_tn(wm_s[h], ds)
                dws_ref[h] += _dot_nt(ds, vn[:, cols])
                dbst_ref[:, h:h + 1] += jnp.sum(ds, axis=1, keepdims=True)
            dvn = dvn_s[...]
            dproj_ref[rows, d:2 * d] = _ln_bwd(dvn, vhat, rstd, lng_ref[...]).astype(ACT_DTYPE)
            vec_ref[0:1, :] += _colsum(dvn * vhat)
            vec_ref[1:2, :] += _colsum(dvn)
            a = a_ref[rows, :].astype(F32)
            g = g_ref[rows, :].astype(F32)
            _put_lanes(zext, pl.ds(HALO + r0, CHUNK), a * _sigmoid(g))
            dzc, dblg, dblb = conv_ln_bwd(zc_ref[rows, :].astype(F32), dyb_ref[rows, :].astype(F32))
            _put_lanes(dzext, rows, dzc)
            vec_ref[2:3, :] += _colsum(dzc)
            vec_ref[3:4, :] += dblg
            vec_ref[4:5, :] += dblb
            return carry

        lax.fori_loop(0, nchunk, chunk, 0)

        dzc_next, _, _ = conv_ln_bwd(zcn_ref[...].astype(F32), dybn_ref[...].astype(F32))
        _put_lanes(dzext, slice(tm, tm + HALO), dzc_next * jnp.where(i % tpb == tpb - 1, 0.0, 1.0))

        def lane_chunk_dz(lc, carry):
            _conv_taps(dzext, cw_ref, dz3, lc, tm, flip=True)
            return carry

        lax.fori_loop(0, d // LANES, lane_chunk_dz, 0)

        def lane_chunk(lc, carry):
            accs = [jnp.zeros((8, LANES), F32) for _ in range(CONV_TAPS)]
            for b in range(tm // TAP_GRAD_ROWS):
                dzc = dzext[lc, b * TAP_GRAD_ROWS:(b + 1) * TAP_GRAD_ROWS, :]
                for k, win in _tap_windows(zext, lc, b * TAP_GRAD_ROWS, TAP_GRAD_ROWS, flip=False):
                    prod = dzc * win
                    part = prod[0:8]
                    for e in range(1, TAP_GRAD_ROWS // 8):
                        part = part + prod[8 * e:8 * e + 8]
                    accs[k] = accs[k] + part
            for k in range(CONV_TAPS):
                dcw_ref[lc, k:k + 1, :] += _colsum(accs[k])
            return carry

        lax.fori_loop(0, d // LANES, lane_chunk, 0)

        def glu_bwd(c, carry):
            r0 = pl.multiple_of(c * CHUNK, CHUNK)
            rows = pl.ds(r0, CHUNK)
            for lc in range(d // LANES):
                lanes = slice(lc * LANES, (lc + 1) * LANES)
                dz = dz3[lc, rows, :]
                a = a_ref[rows, lanes].astype(F32)
                sg = _sigmoid(g_ref[rows, lanes].astype(F32))
                dproj_ref[rows, 2 * d + lc * LANES:2 * d + (lc + 1) * LANES] = (dz * sg).astype(ACT_DTYPE)
                dproj_ref[rows, 3 * d + lc * LANES:3 * d + (lc + 1) * LANES] = (
                    dz * a * sg * (1.0 - sg)).astype(ACT_DTYPE)
            return carry

        lax.fori_loop(0, nchunk, glu_bwd, 0)

        @pl.when(i == ntile - 1)
        def _():
            mask = _causal_mask()
            for h in range(HEADS):
                dws_ref[h] = jnp.where(mask, dws_ref[h], 0.0)

    col = lambda k: pl.BlockSpec((tm, d), lambda i: (i, k))
    tile = pl.BlockSpec((tm, d), lambda i: (i, 0))
    before = lambda k: pl.BlockSpec((HALO, d), lambda i: (jnp.maximum(i * per - 1, 0), k))
    following = pl.BlockSpec((HALO, d), lambda i: (jnp.minimum((i + 1) * per, n // HALO - 1), 0))
    vec = pl.BlockSpec((None, 1, d), lambda i: (l, 0, 0))
    const2 = lambda r, c: pl.BlockSpec((r, c), lambda i: (0, 0))
    return pl.pallas_call(
        body, name=f"branches_bwd_{l}",
        out_shape=[jax.ShapeDtypeStruct((n, 6 * d), ACT_DTYPE), jax.ShapeDtypeStruct((HEADS, CHUNK, CHUNK), F32),
                   jax.ShapeDtypeStruct((CHUNK, HEADS), F32), jax.ShapeDtypeStruct((d // LANES, HALO, LANES), F32),
                   jax.ShapeDtypeStruct((8, d), F32)],
        grid=(ntile,),
        in_specs=[col(0), col(1), col(2), col(3), before(2), before(3), tile, following, tile, tile, following,
                  pl.BlockSpec(memory_space=pl.ANY), vec, vec,
                  pl.BlockSpec((None, HEADS, CHUNK, CHUNK), lambda i: (l, 0, 0, 0)),
                  pl.BlockSpec((None, CHUNK, HEADS), lambda i: (l, 0, 0)),
                  pl.BlockSpec((None, d // LANES, HALO, LANES), lambda i: (l, 0, 0, 0)), vec, vec, ANY_SPEC],
        out_specs=[pl.BlockSpec((tm, 4 * d), lambda i: (i, 0)),
                   pl.BlockSpec((HEADS, CHUNK, CHUNK), lambda i: (0, 0, 0)),
                   const2(CHUNK, HEADS), pl.BlockSpec((d // LANES, HALO, LANES), lambda i: (0, 0, 0)), const2(8, d)],
        scratch_shapes=[pltpu.VMEM((HEADS, CHUNK, CHUNK), MXU_DTYPE),
                        pltpu.VMEM((d // LANES, HALO + tm, LANES), F32),
                        pltpu.VMEM((d // LANES, tm + HALO, LANES), F32),
                        pltpu.VMEM((d // LANES, tm, LANES), F32), pltpu.VMEM((CHUNK, d), F32)],
        input_output_aliases={11: 0},
        compiler_params=_params(("arbitrary",)),
    )(proj, proj, proj, proj, proj, proj, zc, zc, dya_in, dyb_in, dyb_in, dproj, lng, lnb, ws, bst, cw, blg, blb, after)


def _in_proj_bwd(l, dproj, dx1, x, mod, g1, wg_in, t_len, nb, after):
    n, d = x.shape
    tm = min(TOKEN_TILE, t_len)
    tpb = t_len // tm
    qc = wg_in.shape[-1]

    def body(dp_ref, dx1_ref, x_ref, mod_ref, g_ref, w_ref, after_ref, dx_ref, dmod_ref, dg_ref):
        i = pl.program_id(0)

        @pl.when(i == 0)
        def _():
            dg_ref[...] = jnp.zeros_like(dg_ref)

        @pl.when(i % tpb == 0)
        def _():
            dmod_ref[...] = jnp.zeros_like(dmod_ref)

        dh = jnp.zeros((tm, d), F32)
        for q in range(N_CHIP):
            dh = dh + _dot_nt(dp_ref[:, q * qc:(q + 1) * qc], w_ref[q])
        _norm_tail(i, tpb, x_ref, dx1_ref, dh, g_ref, mod_ref[1:2, :], dx_ref, dmod_ref, dg_ref, 0, 1)

    tile = pl.BlockSpec((tm, d), lambda i: (i, 0))
    return pl.pallas_call(
        body, name=f"in_proj_bwd_{l}",
        out_shape=[jax.ShapeDtypeStruct((n, d), F32), jax.ShapeDtypeStruct((nb, 8, d), F32),
                   jax.ShapeDtypeStruct((8, d), F32)],
        grid=(n // tm,),
        in_specs=[pl.BlockSpec((tm, N_CHIP * qc), lambda i: (i, 0)), tile, tile,
                  pl.BlockSpec((None, None, 8, d), lambda i: (l, i // tpb, 0, 0)),
                  pl.BlockSpec((None, 1, d), lambda i: (l, 0, 0)),
                  _resident((N_CHIP, d, qc)), ANY_SPEC],
        out_specs=[tile, pl.BlockSpec((None, 8, d), lambda i: (i // tpb, 0, 0)),
                   pl.BlockSpec((8, d), lambda i: (0, 0))],
        compiler_params=_params(("arbitrary",)),
    )(dproj, dx1, x, mod, g1, wg_in, after)


def _weight_grad(name, a, b, a_spec, b_spec, out_rows, out_spec, acc_shape, grid_ij, relu2=False):
    n = a.shape[0]
    tk = min(MATMUL_TILE, n)
    nk = n // tk
    cols = acc_shape[1]

    def body(a_ref, b_ref, o_ref, acc):
        k = pl.program_id(2)

        @pl.when(k == 0)
        def _():
            acc[...] = jnp.zeros_like(acc)

        a_t = a_ref[...]
        if relu2:
            a_t = jnp.square(jnp.maximum(a_t.astype(F32), 0.0))
        acc[...] += _dot_tn(a_t, b_ref[...])

        @pl.when(k == nk - 1)
        def _():
            o_ref[...] = acc[...].reshape(o_ref.shape).astype(WIRE_DTYPE)

    gi, gj = grid_ij
    return pl.pallas_call(
        body, name=name, out_shape=jax.ShapeDtypeStruct((N_CHIP, out_rows, cols), WIRE_DTYPE),
        grid=(gi, gj, nk),
        in_specs=[a_spec(tk), b_spec(tk)],
        out_specs=out_spec,
        scratch_shapes=[pltpu.VMEM(acc_shape, F32)],
        compiler_params=_params(("arbitrary", "arbitrary", "arbitrary")),
    )(a, b)


def _row_tile(rows, cols, arrays):
    budget = VMEM_LIMIT // 3
    t = budget // (arrays * 2 * cols * 4)
    t = max(8, min(rows, t // 8 * 8))
    while rows % t:
        t -= 8
    return t


def _sum_partials(name, own, got, myq, l, nl, prev):
    _, rows, cols = own.shape
    tr = _row_tile(rows, cols, 3)
    nt = rows // tr

    def body(q_ref, own_ref, got_ref, *rest):
        o_ref = rest[-1]
        acc = own_ref[...].astype(F32)
        for k in range(3):
            acc = acc + got_ref[k].astype(F32)
        o_ref[...] = acc

    operands = [myq, own, got] + ([] if prev is None else [prev])
    return pl.pallas_call(
        body, name=name, out_shape=jax.ShapeDtypeStruct((nl * rows, cols), F32),
        grid_spec=pltpu.PrefetchScalarGridSpec(
            num_scalar_prefetch=1, grid=(nt,),
            in_specs=[pl.BlockSpec((None, tr, cols), lambda i, q: (q[0], i, 0)),
                      pl.BlockSpec((3, tr, cols), lambda i, q: (0, i, 0))]
            + ([] if prev is None else [pl.BlockSpec(memory_space=pl.ANY)]),
            out_specs=pl.BlockSpec((tr, cols), lambda i, q: (l * nt + i, 0))),
        input_output_aliases={} if prev is None else {3: 0},
        compiler_params=_params(("arbitrary",)),
    )(*operands)


def _adamw(name, w, m, v, g_a, g_b=None):
    rows, cols = w.shape
    tr = _row_tile(rows, cols, 9)
    c1 = 1.0 - ADAM_B1 ** ADAM_STEP
    c2 = 1.0 - ADAM_B2 ** ADAM_STEP

    def body(*refs):
        if g_b is None:
            w_ref, m_ref, v_ref, ga_ref, g_ref, d_ref, m2_ref, v2_ref = refs
            g = ga_ref[...]
        else:
            w_ref, m_ref, v_ref, ga_ref, gb_ref, g_ref, d_ref, m2_ref, v2_ref = refs
            g = ga_ref[...] + gb_ref[...]
        m2 = ADAM_B1 * m_ref[...] + (1.0 - ADAM_B1) * g
        v2 = ADAM_B2 * v_ref[...] + (1.0 - ADAM_B2) * (g * g)
        g_ref[...] = g
        m2_ref[...] = m2
        v2_ref[...] = v2
        d_ref[...] = -ADAM_LR * ((m2 / c1) / (jnp.sqrt(v2 / c2) + ADAM_EPS) + ADAM_WD * w_ref[...])

    tile = pl.BlockSpec((tr, cols), lambda i: (i, 0))
    operands = [w, m, v, g_a] + ([] if g_b is None else [g_b])
    return pl.pallas_call(
        body, name=name, out_shape=[jax.ShapeDtypeStruct((rows, cols), F32)] * 4,
        grid=(rows // tr,), in_specs=[tile] * len(operands), out_specs=[tile] * 4,
        compiler_params=_params(("arbitrary",)),
    )(*operands)


def _pack(parts):
    flat = [p.reshape(-1, LANES) for p in parts]
    for f in flat:
        assert f.shape[0] % 8 == 0
    return jnp.concatenate(flat, axis=0)


def _unpack(packed, shapes):
    out, r = [], 0
    for s in shapes:
        size = 1
        for e in s:
            size *= e
        rows = size // LANES
        out.append(packed[r:r + rows].reshape(s))
        r += rows
    return out


def kernel(x, c, w_ada, b_ada, norm1_g, w_in, a_ln_g, a_ln_b, a_ws, a_bs, w_pa, b_conv_w, b_conv_b, b_ln_g, b_ln_b, w_pb, w_out, norm2_g, w_ff1, w_ff2, final_g, loss_target, m_w_ada, m_b_ada, m_norm1_g, m_w_in, m_a_ln_g, m_a_ln_b, m_a_ws, m_a_bs, m_w_pa, m_b_conv_w, m_b_conv_b, m_b_ln_g, m_b_ln_b, m_w_pb, m_w_out, m_norm2_g, m_w_ff1, m_w_ff2, m_final_g, v_w_ada, v_b_ada, v_norm1_g, v_w_in, v_a_ln_g, v_a_ln_b, v_a_ws, v_a_bs, v_w_pa, v_b_conv_w, v_b_conv_b, v_b_ln_g, v_b_ln_b, v_w_pb, v_w_out, v_norm2_g, v_w_ff1, v_w_ff2, v_final_g):
    nb, t_len, d = x.shape
    nl = w_in.shape[0]
    n = nb * t_len
    cq = w_ada.shape[-1]
    cc = d // N_CHIP
    mx, my, mc = _my_place()
    myq = (2 * mx + my).astype(jnp.int32).reshape(1)

    taps = jnp.pad(b_conv_w.reshape(nl, CONV_TAPS, cc), ((0, 0), (0, HALO - CONV_TAPS), (0, 0)))
    first = jnp.concatenate([jnp.pad(c, ((0, 8 - nb), (0, 0))), taps.reshape(nl * HALO * cc // d, d)], axis=0)
    first = _all_to_all(jnp.broadcast_to(first[None], (N_DEV,) + first.shape), "gather_c_and_taps")
    c_all = first[:, :nb].reshape(N_DEV * nb, d)
    cwg = first[:, 8:].reshape(N_CHIP, 2, nl, HALO, cc)[:, 0]
    cw = cwg.transpose(1, 2, 0, 3).reshape(nl, HALO, d)
    cw = cw.reshape(nl, HALO, d // LANES, LANES).transpose(0, 2, 1, 3)
    mod_part = _ada_forward(c_all, w_ada, b_ada.reshape(nl, 1, N_CHIP * cq), myq)
    mod_slots = mod_part.reshape(nl, N_DEV, nb, cq).transpose(1, 0, 2, 3).reshape(N_DEV, nl * nb, cq)
    mod_got = _all_to_all(mod_slots, "exchange_mod").reshape(N_CHIP, 2, nl, nb, cq)[:, 0]
    mod6 = mod_got.transpose(1, 2, 0, 3).reshape(nl, nb, 6, d)
    mod = jnp.pad(mod6, ((0, 0), (0, 0), (0, 2), (0, 0)))

    big = ["w_in", "w_pa", "w_pb", "w_out", "w_ff1", "w_ff2"]
    ws_given = dict(w_in=(w_in, m_w_in, v_w_in), w_pa=(w_pa, m_w_pa, v_w_pa), w_pb=(w_pb, m_w_pb, v_w_pb),
                    w_out=(w_out, m_w_out, v_w_out), w_ff1=(w_ff1, m_w_ff1, v_w_ff1), w_ff2=(w_ff2, m_w_ff2, v_w_ff2))

    def own_slot(w_l, after=None):
        if after is not None:
            w_l = w_l - after[0, 0]
        empty = lax.empty((N_CHIP,) + w_l.shape, WIRE_DTYPE)
        return lax.dynamic_update_index_in_dim(empty, w_l.astype(WIRE_DTYPE), myq[0], 0)

    def zero_after(*arrays):
        z = jnp.zeros((8, LANES), F32)
        for a in arrays:
            piece = a.reshape(-1, a.shape[-1])[:8, :LANES]
            z = z + jnp.where(jnp.isfinite(piece), piece, 0.0) * 0.0
        return z

    first_sems = _split_start("gather_start_in_0", "gather", [], [own_slot(w_in[0])], zero_after(cw, mod[:, 0]))
    token = first = first_sems[4]
    gathers = []
    for l in range(nl):
        group = big[1:] if l == 0 else big
        send_sems, recv_sems, _, lands, token = _split_start(
            f"gather_start_{l}", "gather", [], [own_slot(ws_given[k][0][l], first) for k in group], token)
        if l == 0:
            send_sems = list(first_sems[0]) + list(send_sems)
            recv_sems = list(first_sems[1]) + list(recv_sems)
            lands = list(first_sems[3]) + list(lands)
        gathers.append((send_sems, recv_sems, lands))
    mod = mod + token[0, 0]

    def gather_wait(l, part, lo, hi, after):
        send_sems, recv_sems, lands = gathers[l]
        return _split_wait(f"gather_wait_{part}_{l}", "gather", send_sems[lo:hi], recv_sems[lo:hi], [],
                           lands[lo:hi], after)[1]

    vec3 = lambda p: p.reshape(nl, 1, d)
    g1, g2 = vec3(norm1_g), vec3(norm2_g)
    lng, lnb, cb, blg, blb = vec3(a_ln_g), vec3(a_ln_b), vec3(b_conv_b), vec3(b_ln_g), vec3(b_ln_b)
    bst = a_bs.transpose(0, 2, 1)

    xs = x.reshape(n, d)
    saved = []
    weights = []
    for l in range(nl):
        if l == 0:
            send_sems, recv_sems, lands = gathers[0]
            wg_in = lands[0]
            h, proj = _in_proj_quarter("in_proj_0_own", l, xs, mod, g1, wg_in, myq, t_len)
            for k, (px, py) in enumerate(_other_chips(mx, my)):
                (wg_in,) = _split_wait(f"gather_wait_in_0_{k}", "gather", send_sems[:1], recv_sems[:1], [], [wg_in],
                                       proj, peers=(k,))[1]
                quarter = (2 * px + py).astype(jnp.int32).reshape(1)
                proj = _in_proj_quarter(f"in_proj_0_{k}", l, h, mod, g1, wg_in, quarter, t_len, proj)
        else:
            (wg_in,) = gather_wait(l, "in", 0, 1, xs)
            h, proj = _in_proj(l, xs, mod, g1, wg_in, t_len)
        ya_in, yb_in, zc = _branches_fwd(l, proj, lng, lnb, a_ws, bst, cw, cb, blg, blb, t_len)
        wg_pa, wg_pb, wg_out = gather_wait(l, "mid", 1, 4, ya_in)
        ya, yb, merged, o, x1 = _merge_out(l, xs, mod, proj, ya_in, yb_in, wg_pa, wg_pb, wg_out, t_len)
        wg_ff1, wg_ff2 = gather_wait(l, "ffn", 4, 6, x1)
        h2, f, o2, x2 = _ffn_fwd(l, x1, mod, g2, wg_ff1, wg_ff2, t_len)
        saved.append((xs, h, proj, ya_in, yb_in, zc, ya, yb, merged, o, x1, h2, f, o2))
        weights.append((wg_in, wg_pa, wg_pb, wg_out, wg_ff1, wg_ff2))
        xs = x2

    loss_blk, dx, dfinal = _loss_head(xs, final_g.reshape(1, d), loss_target.reshape(n, d))

    tok = lambda w: (lambda tk: pl.BlockSpec((tk, w), lambda i, j, k: (k, 0)))
    tok_i = lambda w: (lambda tk: pl.BlockSpec((tk, w), lambda i, j, k: (k, i)))
    tok_j = lambda w: (lambda tk: pl.BlockSpec((tk, w), lambda i, j, k: (k, j)))
    qin = weights[0][0].shape[-1]
    hq = weights[0][4].shape[-1]
    rq = d // N_CHIP
    slot_i = lambda r, cdim: pl.BlockSpec((None, r, cdim), lambda i, j, k: (i, 0, 0))
    slot_j = lambda r, cdim: pl.BlockSpec((None, r, cdim), lambda i, j, k: (j, 0, 0))
    all_slots = pl.BlockSpec((N_CHIP, rq, d), lambda i, j, k: (0, 0, 0))
    scatters = []

    def scatter_start(l, part, names, grads, after):
        lands = [lax.empty((3,) + g.shape[1:], g.dtype) for g in grads]
        send_sems, recv_sems, srcs, lands, tok_out = _split_start(f"scatter_start_{part}_{l}", "scatter", grads, lands,
                                                                  after)
        scatters.append((f"scatter_wait_{part}_{l}", l, names, send_sems, recv_sems, srcs, lands))
        return tok_out

    dmods, small = [None] * nl, [None] * nl
    for l in reversed(range(nl)):
        x0, h, proj, ya_in, yb_in, zc, ya, yb, merged, o, x1, h2, f, o2 = saved[l]
        wg_in, wg_pa, wg_pb, wg_out, wg_ff1, wg_ff2 = weights[l]
        do2, df, dx1, dmod_c, dg2 = _ffn_bwd(l, dx, x1, mod, g2, o2, f, wg_ff1, wg_ff2, t_len, nb)
        g_ff2 = _weight_grad(f"grad_w_ff2_{l}", f, do2, tok_i(hq), tok(d), hq, slot_i(hq, d), (hq, d), (N_CHIP, 1),
                             relu2=True)
        g_ff1 = _weight_grad(f"grad_w_ff1_{l}", h2, df, tok(d), tok_j(hq), d, slot_j(d, hq), (d, hq), (1, N_CHIP))
        token = scatter_start(l, "ffn", ["w_ff2", "w_ff1"], [g_ff2, g_ff1], token)
        do, dya, dyb, dya_in, dyb_in, dproj, dmod_b = _merge_bwd(l, dx1, mod, o, ya, yb, proj, wg_pa, wg_pb, wg_out,
                                                                 t_len, nb, token)
        g_out = _weight_grad(f"grad_w_out_{l}", merged, do, tok(d), tok(d), rq, all_slots, (d, d), (1, 1))
        g_pa = _weight_grad(f"grad_w_pa_{l}", ya_in, dya, tok(d), tok(d), rq, all_slots, (d, d), (1, 1))
        g_pb = _weight_grad(f"grad_w_pb_{l}", yb_in, dyb, tok(d), tok(d), rq, all_slots, (d, d), (1, 1))
        token = scatter_start(l, "mid", ["w_out", "w_pa", "w_pb"], [g_out, g_pa, g_pb], token)
        dproj, dws, dbst, dcw, vecs = _branches_bwd(l, proj, zc, dya_in, dyb_in, dproj, lng, lnb, a_ws, bst, cw,
                                                    blg, blb, t_len, token)
        g_in = _weight_grad(f"grad_w_in_{l}", h, dproj, tok(d), tok_j(qin), d, slot_j(d, qin), (d, qin), (1, N_CHIP))
        token = scatter_start(l, "in", ["w_in"], [g_in], token)
        dx, dmod_a, dg1 = _in_proj_bwd(l, dproj, dx1, x0, mod, g1, wg_in, t_len, nb, token)
        dmods[l] = jnp.concatenate([dmod_a[:, 0:2], dmod_b[:, 2:3], dmod_c[:, 3:6]], axis=1)
        dcw = dcw.transpose(1, 0, 2).reshape(HALO, d)[:CONV_TAPS]
        small[l] = (dg1[0], vecs[0], vecs[1], dws, dbst.T, dcw, vecs[2], vecs[3], vecs[4], dg2[0])
    grad_x = dx.reshape(nb, t_len, d)

    names = ["norm1_g", "a_ln_g", "a_ln_b", "a_ws", "a_bs", "b_conv_w", "b_conv_b", "b_ln_g", "b_ln_b", "norm2_g"]
    stacked = [jnp.stack([small[l][k] for l in range(nl)]) for k in range(len(names))]
    stacked[5] = jnp.pad(stacked[5], ((0, 0), (0, HALO - CONV_TAPS), (0, 0)))
    stacked += [dfinal, loss_blk]
    part_shapes = [s.shape for s in stacked]
    packed = _pack(stacked)
    prow = packed.shape[0]
    pad_rows = (-prow) % (8 * N_DEV)
    packed = jnp.pad(packed, ((0, pad_rows), (0, 0)))
    srow = packed.shape[0] // N_DEV
    mine = _all_to_all(packed.reshape(N_DEV, srow, LANES), "reduce_small", reduce=True)
    dmod_rows = nl * nb * 6 * d // LANES
    second = jnp.concatenate([mine, jnp.stack(dmods).reshape(dmod_rows, LANES)], axis=0)
    second = _all_to_all(jnp.broadcast_to(second[None], (N_DEV,) + second.shape), "gather_small_and_dmod")
    total = second[:, :srow].reshape(N_DEV * srow, LANES)[:prow]
    dmod_all = second[:, srow:].reshape(N_DEV, nl, nb, 6 * d).transpose(1, 0, 2, 3).reshape(nl, N_DEV * nb, 6 * d)

    half = dict.fromkeys(big)
    for name, l, group, send_sems, recv_sems, srcs, lands in scatters:
        srcs, lands = _split_wait(name, "scatter", send_sems, recv_sems, srcs, lands, total)
        for k, g_own, g_got in zip(group, srcs, lands):
            half[k] = _sum_partials(f"sum_{k}_{l}", g_own, g_got, myq, l, nl, half[k])
    sums = [half[k] for k in big]
    swap_send, swap_recv, sums, others, token = _split_start(
        "swap_start", "swap", sums, [lax.empty(s.shape, s.dtype) for s in sums], total)
    g_w_ada, g_b_ada = _ada_backward(c_all, dmod_all, myq, cq, token)

    sg = dict(zip(names + ["final_g", "loss"], _unpack(total, part_shapes)))
    loss = sg["loss"][0, 0]
    sg["b_conv_w"] = lax.dynamic_slice_in_dim(sg["b_conv_w"][:, :CONV_TAPS], myq[0] * cc, cc, axis=2).reshape(
        nl, CONV_TAPS, 1, cc)
    sg["final_g"] = sg["final_g"][0]
    sg["b_ada"] = g_b_ada.reshape(nl, N_CHIP * cq)
    small_names = ["b_ada", "norm1_g", "a_ln_g", "a_ln_b", "a_ws", "a_bs", "b_conv_w", "b_conv_b", "b_ln_g",
                   "b_ln_b", "norm2_g", "final_g"]
    given = dict(b_ada=(b_ada, m_b_ada, v_b_ada), norm1_g=(norm1_g, m_norm1_g, v_norm1_g),
                 a_ln_g=(a_ln_g, m_a_ln_g, v_a_ln_g), a_ln_b=(a_ln_b, m_a_ln_b, v_a_ln_b),
                 a_ws=(a_ws, m_a_ws, v_a_ws), a_bs=(a_bs, m_a_bs, v_a_bs),
                 b_conv_w=(b_conv_w, m_b_conv_w, v_b_conv_w), b_conv_b=(b_conv_b, m_b_conv_b, v_b_conv_b),
                 b_ln_g=(b_ln_g, m_b_ln_g, v_b_ln_g), b_ln_b=(b_ln_b, m_b_ln_b, v_b_ln_b),
                 norm2_g=(norm2_g, m_norm2_g, v_norm2_g), final_g=(final_g, m_final_g, v_final_g))

    def padded(a):
        rows = -(-a.size // (8 * LANES)) * 8
        return jnp.pad(a.reshape(-1), (0, rows * LANES - a.size)).reshape(rows, LANES)

    packs = [_pack([padded(given[k][j]) for k in small_names]) for j in range(3)]
    gpack = _pack([padded(sg[k].astype(F32)) for k in small_names])
    res_small = _adamw("adamw_small", packs[0], packs[1], packs[2], gpack)
    out = {}
    for j, kind in enumerate(["grad", "delta", "new_m", "new_v"]):
        r = 0
        for k in small_names:
            a = given[k][0]
            rows = -(-a.size // (8 * LANES)) * 8
            out[(kind, k)] = res_small[j][r:r + rows].reshape(-1)[:a.size].reshape(a.shape)
            r += rows

    res = _adamw("adamw_w_ada", w_ada.reshape(nl * d, cq), m_w_ada.reshape(nl * d, cq), v_w_ada.reshape(nl * d, cq),
                 g_w_ada.reshape(nl * d, cq))
    for kind, r in zip(["grad", "delta", "new_m", "new_v"], res):
        out[(kind, "w_ada")] = r.reshape(w_ada.shape)

    sums, others = _split_wait("swap_wait", "swap", swap_send, swap_recv, sums, others, res[0])
    for k, s_mine, s_other in zip(big, sums, others):
        w, m, v = ws_given[k]
        cols = w.shape[-1]
        res = _adamw(f"adamw_{k}", w.reshape(-1, cols), m.reshape(-1, cols), v.reshape(-1, cols), s_mine, s_other)
        for kind, r in zip(["grad", "delta", "new_m", "new_v"], res):
            out[(kind, k)] = r.reshape(w.shape)

    order = ["w_ada", "b_ada", "norm1_g", "w_in", "a_ln_g", "a_ln_b", "a_ws", "a_bs", "w_pa", "b_conv_w", "b_conv_b",
             "b_ln_g", "b_ln_b", "w_pb", "w_out", "norm2_g", "w_ff1", "w_ff2", "final_g"]
    return (loss, grad_x, *[out[("grad", k)] for k in order], *[out[("delta", k)] for k in order],
            *[out[("new_m", k)] for k in order], *[out[("new_v", k)] for k in order])
```

```python
import jax
import jax.numpy as jnp
from jax import lax
from jax.experimental import pallas as pl
from jax.experimental.pallas import tpu as pltpu

F32 = jnp.float32
MXU_DTYPE = jnp.bfloat16
ACT_DTYPE = jnp.bfloat16
WIRE_DTYPE = jnp.bfloat16

EPS = 1e-6
CHUNK = 128
HEADS = 8
CONV_TAPS = 31
HALO = 32
N_DEV = 8
N_CHIP = 4
ADAM_LR, ADAM_B1, ADAM_B2, ADAM_EPS, ADAM_WD, ADAM_STEP = 0.001, 0.9, 0.999, 1e-08, 0.01, 10

V7X_VMEM_BYTES = 64 * 1024 * 1024
VMEM_LIMIT = V7X_VMEM_BYTES * 7 // 8
TOKEN_TILE = 512
MATMUL_TILE = 2048
FFN_BWD_TILE = 512
CONV_ROWS = 64
TAP_GRAD_ROWS = 32
LANES = 128
MESH_ID = pl.DeviceIdType.MESH


def _params(sem=None):
    return pltpu.CompilerParams(dimension_semantics=sem, vmem_limit_bytes=VMEM_LIMIT)


def _resident(shape):
    return pl.BlockSpec(shape, lambda *_: (0,) * len(shape), pipeline_mode=pl.Buffered(1))


def _dot(a, b):
    return jnp.dot(a.astype(MXU_DTYPE), b.astype(MXU_DTYPE), preferred_element_type=F32)


def _dot_nt(a, b):
    return lax.dot_general(a.astype(MXU_DTYPE), b.astype(MXU_DTYPE), (((1,), (1,)), ((), ())),
                           preferred_element_type=F32)


def _dot_tn(a, b):
    return lax.dot_general(a.astype(MXU_DTYPE), b.astype(MXU_DTYPE), (((0,), (0,)), ((), ())),
                           preferred_element_type=F32)


def _colsum(a):
    return jnp.sum(a, axis=0, keepdims=True)


def _rowmean(a):
    return jnp.mean(a, axis=-1, keepdims=True)


def _sigmoid(a):
    return 1.0 / (1.0 + jnp.exp(-a))


def _modnorm_fwd(x, g, sc, sh):
    r = lax.rsqrt(_rowmean(x * x) + EPS)
    return (x * r) * (g * (1.0 + sc)) + sh


def _modnorm_bwd(x, dh, g, sc):
    r = lax.rsqrt(_rowmean(x * x) + EPS)
    xn = x * r
    dxn = dh * (g * (1.0 + sc))
    dx = r * (dxn - xn * _rowmean(dxn * xn))
    return dx, _colsum(dh), _colsum(dh * xn)


def _ln_stats(v):
    mu = _rowmean(v)
    vc = v - mu
    rstd = lax.rsqrt(_rowmean(vc * vc) + EPS)
    return vc * rstd, rstd


def _ln_bwd(dy, vhat, rstd, g):
    dvh = dy * g
    return rstd * (dvh - _rowmean(dvh) - vhat * _rowmean(dvh * vhat))


def _causal_mask():
    row = lax.broadcasted_iota(jnp.int32, (CHUNK, CHUNK), 0)
    col = lax.broadcasted_iota(jnp.int32, (CHUNK, CHUNK), 1)
    return row >= col


def _my_place():
    return lax.axis_index("x"), lax.axis_index("y"), lax.axis_index("c")


def _all_to_all(x, name, reduce=False):
    n, rows, cols = x.shape
    assert n == N_DEV

    def body(x_ref, o_ref, *scratch):
        if reduce:
            land, send_sems, recv_sems = scratch
        else:
            land = o_ref
            send_sems, recv_sems = scratch
        mx, my, mc = _my_place()
        me = 4 * mx + 2 * my + mc
        land[me] = x_ref[me]
        copies = []
        for k in range(1, N_DEV):
            px = (mx + ((k >> 2) & 1)) % 2
            py = (my + ((k >> 1) & 1)) % 2
            pc = (mc + (k & 1)) % 2
            peer = 4 * px + 2 * py + pc
            cp = pltpu.make_async_remote_copy(
                src_ref=x_ref.at[peer], dst_ref=land.at[me],
                send_sem=send_sems.at[k - 1], recv_sem=recv_sems.at[k - 1],
                device_id=(px, py, pc), device_id_type=MESH_ID)
            cp.start()
            copies.append(cp)
        for cp in copies:
            cp.wait()
        if reduce:
            acc = land[0]
            for s in range(1, N_DEV):
                acc = acc + land[s]
            o_ref[...] = acc

    scratch = [pltpu.SemaphoreType.DMA((N_DEV - 1,)), pltpu.SemaphoreType.DMA((N_DEV - 1,))]
    if reduce:
        scratch = [pltpu.VMEM((N_DEV, rows, cols), x.dtype)] + scratch
        out_shape = jax.ShapeDtypeStruct((rows, cols), x.dtype)
    else:
        out_shape = jax.ShapeDtypeStruct(x.shape, x.dtype)
    return pl.pallas_call(
        body, name=name, out_shape=out_shape,
        in_specs=[pl.BlockSpec(memory_space=pltpu.VMEM)],
        out_specs=pl.BlockSpec(memory_space=pltpu.VMEM),
        scratch_shapes=scratch,
        compiler_params=pltpu.CompilerParams(vmem_limit_bytes=VMEM_LIMIT),
    )(x)


def _other_chips(mx, my):
    return [(1 - mx, my), (mx, 1 - my), (1 - mx, 1 - my)]


HBM_SPEC = pl.BlockSpec(memory_space=pltpu.HBM)
SEM_SPEC = pl.BlockSpec(memory_space=pltpu.SEMAPHORE)
ANY_SPEC = pl.BlockSpec(memory_space=pl.ANY)
SPLIT_EFFECT = pltpu.SideEffectType.DATAFLOW_SIDE_EFFECTING


def _quarter_copies(mode, srcs, lands, send_sems, recv_sems, peers=(0, 1, 2)):
    mx, my, mc = _my_place()
    myq = 2 * mx + my
    if mode == "swap":
        return [pltpu.make_async_remote_copy(
            src_ref=srcs[a], dst_ref=lands[a], send_sem=send_sems[a].at[0], recv_sem=recv_sems[a].at[0],
            device_id=(mx, my, 1 - mc), device_id_type=MESH_ID) for a in range(len(lands))]
    copies = []
    for a in range(len(lands)):
        for k, (px, py) in enumerate(_other_chips(mx, my)):
            if k not in peers:
                continue
            if mode == "gather":
                src, dst = lands[a].at[myq], lands[a].at[myq]
            else:
                src, dst = srcs[a].at[2 * px + py], lands[a].at[k]
            copies.append(pltpu.make_async_remote_copy(
                src_ref=src, dst_ref=dst, send_sem=send_sems[a].at[k], recv_sem=recv_sems[a].at[k],
                device_id=(px, py, mc), device_id_type=MESH_ID))
    return copies


def _split_start(name, mode, srcs, lands, after):
    ns, n = len(srcs), len(lands)

    def body(*refs):
        outs = refs[ns + n + 1:]
        for cp in _quarter_copies(mode, refs[:ns], refs[ns:ns + n], outs[:n], outs[n:2 * n]):
            cp.start()
        token = outs[-1]
        token[...] = jnp.zeros_like(token)

    arrays = list(srcs) + list(lands)
    per_array = 1 if mode == "swap" else 3
    res = pl.pallas_call(
        body, name=name,
        out_shape=[pltpu.SemaphoreType.DMA((per_array,))] * (2 * n) + [pltpu.HBM(x.shape, x.dtype) for x in arrays]
        + [jax.ShapeDtypeStruct((8, LANES), F32)],
        in_specs=[HBM_SPEC] * (ns + n) + [ANY_SPEC],
        out_specs=[SEM_SPEC] * (2 * n) + [HBM_SPEC] * (ns + n) + [pl.BlockSpec(memory_space=pltpu.VMEM)],
        input_output_aliases={i: 2 * n + i for i in range(ns + n)},
        compiler_params=pltpu.CompilerParams(has_side_effects=SPLIT_EFFECT),
    )(*[pltpu.with_memory_space_constraint(x, pltpu.HBM) for x in arrays], after)
    return res[:n], res[n:2 * n], res[2 * n:2 * n + ns], res[2 * n + ns:2 * n + ns + n], res[-1]


def _split_wait(name, mode, send_sems, recv_sems, srcs, lands, after, peers=(0, 1, 2)):
    ns, n = len(srcs), len(lands)

    def body(*refs):
        sems = refs[ns + n:ns + 3 * n]
        for cp in _quarter_copies(mode, refs[:ns], refs[ns:ns + n], sems[:n], sems[n:], peers):
            cp.wait_send()
            cp.wait_recv()

    arrays = list(srcs) + list(lands)
    res = pl.pallas_call(
        body, name=name,
        out_shape=[pltpu.HBM(x.shape, x.dtype) for x in arrays],
        in_specs=[HBM_SPEC] * (ns + n) + [SEM_SPEC] * (2 * n) + [ANY_SPEC],
        out_specs=[HBM_SPEC] * (ns + n),
        input_output_aliases={i: i for i in range(ns + n)},
        compiler_params=pltpu.CompilerParams(has_side_effects=SPLIT_EFFECT),
    )(*arrays, *send_sems, *recv_sems, after)
    return res[:ns], res[ns:]


def _ada_forward(c_all, w_ada, b_ada3, myq):
    nl, d, cq = w_ada.shape
    nb = c_all.shape[0]

    def body(q_ref, c_ref, w_ref, b_ref, o_ref):
        c = c_ref[...]
        act = c * _sigmoid(c)
        o_ref[...] = _dot(act, w_ref[...]) + b_ref[...]

    return pl.pallas_call(
        body, name="ada_forward",
        out_shape=jax.ShapeDtypeStruct((nl, nb, cq), F32),
        grid_spec=pltpu.PrefetchScalarGridSpec(
            num_scalar_prefetch=1, grid=(nl,),
            in_specs=[pl.BlockSpec((nb, d), lambda l, q: (0, 0)),
                      pl.BlockSpec((None, d, cq), lambda l, q: (l, 0, 0)),
                      pl.BlockSpec((None, 1, cq), lambda l, q: (l, 0, q[0]))],
            out_specs=pl.BlockSpec((None, nb, cq), lambda l, q: (l, 0, 0))),
        compiler_params=_params(("arbitrary",)),
    )(myq, c_all, w_ada, b_ada3)


def _ada_backward(c_all, dmod_all, myq, cq, after):
    nb, d = c_all.shape
    nl = dmod_all.shape[0]
    full = dmod_all.shape[2]

    def body(q_ref, c_ref, dq_ref, dall_ref, after_ref, gw_ref, gb_ref):
        c = c_ref[...]
        act = c * _sigmoid(c)
        gw_ref[...] = _dot_tn(act, dq_ref[...])
        gb_ref[...] = _colsum(dall_ref[...])

    return pl.pallas_call(
        body, name="ada_backward",
        out_shape=[jax.ShapeDtypeStruct((nl, d, cq), F32), jax.ShapeDtypeStruct((nl, 1, full), F32)],
        grid_spec=pltpu.PrefetchScalarGridSpec(
            num_scalar_prefetch=1, grid=(nl,),
            in_specs=[pl.BlockSpec((nb, d), lambda l, q: (0, 0)),
                      pl.BlockSpec((None, nb, cq), lambda l, q: (l, 0, q[0])),
                      pl.BlockSpec((None, nb, full), lambda l, q: (l, 0, 0)), ANY_SPEC],
            out_specs=[pl.BlockSpec((None, d, cq), lambda l, q: (l, 0, 0)),
                       pl.BlockSpec((None, 1, full), lambda l, q: (l, 0, 0))]),
        compiler_params=_params(("arbitrary",)),
    )(myq, c_all, dmod_all, dmod_all, after)


def _in_proj(l, x, mod, g1, wg_in, t_len):
    n, d = x.shape
    tm = min(TOKEN_TILE, t_len)
    tpb = t_len // tm
    qc = wg_in.shape[-1]

    def body(x_ref, mod_ref, g_ref, w_ref, h_ref, proj_ref):
        h = _modnorm_fwd(x_ref[...], g_ref[...], mod_ref[1:2, :], mod_ref[0:1, :]).astype(MXU_DTYPE)
        h_ref[...] = h.astype(ACT_DTYPE)
        for q in range(N_CHIP):
            proj_ref[:, q * qc:(q + 1) * qc] = jnp.dot(h, w_ref[q], preferred_element_type=F32).astype(ACT_DTYPE)

    return pl.pallas_call(
        body, name=f"in_proj_{l}",
        out_shape=[jax.ShapeDtypeStruct((n, d), ACT_DTYPE), jax.ShapeDtypeStruct((n, N_CHIP * qc), ACT_DTYPE)],
        grid=(n // tm,),
        in_specs=[pl.BlockSpec((tm, d), lambda i: (i, 0)),
                  pl.BlockSpec((None, None, 8, d), lambda i: (l, i // tpb, 0, 0)),
                  pl.BlockSpec((None, 1, d), lambda i: (l, 0, 0)),
                  _resident((N_CHIP, d, qc))],
        out_specs=[pl.BlockSpec((tm, d), lambda i: (i, 0)),
                   pl.BlockSpec((tm, N_CHIP * qc), lambda i: (i, 0))],
        compiler_params=_params(("arbitrary",)),
    )(x, mod, g1, wg_in)


def _in_proj_quarter(name, l, src, mod, g1, wg_in, quarter, t_len, prev=None):
    n, d = src.shape
    tm = min(TOKEN_TILE, t_len)
    tpb = t_len // tm
    qc = wg_in.shape[-1]
    first = prev is None

    def body(q_ref, *refs):
        if first:
            x_ref, mod_ref, g_ref, w_ref, h_ref, proj_ref = refs
            h = _modnorm_fwd(x_ref[...], g_ref[...], mod_ref[1:2, :], mod_ref[0:1, :]).astype(MXU_DTYPE)
            h_ref[...] = h.astype(ACT_DTYPE)
        else:
            h_in_ref, w_ref, _, proj_ref = refs
            h = h_in_ref[...].astype(MXU_DTYPE)
        proj_ref[...] = jnp.dot(h, w_ref[...], preferred_element_type=F32).astype(ACT_DTYPE)

    tile = pl.BlockSpec((tm, d), lambda i, q: (i, 0))
    w_spec = pl.BlockSpec((None, d, qc), lambda i, q: (q[0], 0, 0))
    proj_spec = pl.BlockSpec((tm, qc), lambda i, q: (i, q[0]))
    proj_shape = jax.ShapeDtypeStruct((n, N_CHIP * qc), ACT_DTYPE)
    if first:
        operands = [quarter, src, mod, g1, wg_in]
        in_specs = [tile, pl.BlockSpec((None, None, 8, d), lambda i, q: (l, i // tpb, 0, 0)),
                    pl.BlockSpec((None, 1, d), lambda i, q: (l, 0, 0)), w_spec]
        out_shape, out_specs, aliases = [jax.ShapeDtypeStruct((n, d), ACT_DTYPE), proj_shape], [tile, proj_spec], {}
    else:
        operands = [quarter, src, wg_in, prev]
        in_specs = [tile, w_spec, ANY_SPEC]
        out_shape, out_specs, aliases = proj_shape, proj_spec, {3: 0}
    return pl.pallas_call(
        body, name=name, out_shape=out_shape,
        grid_spec=pltpu.PrefetchScalarGridSpec(num_scalar_prefetch=1, grid=(n // tm,), in_specs=in_specs,
                                               out_specs=out_specs),
        input_output_aliases=aliases,
        compiler_params=_params(("arbitrary",)),
    )(*operands)


def _masked_ws(ws_ref, wm_s):
    mask = _causal_mask()
    for h in range(HEADS):
        wm_s[h] = jnp.where(mask, ws_ref[h], 0.0).astype(MXU_DTYPE)


def _fill_z(i, tpb, ah_ref, gh_ref, zext):
    ah = ah_ref[...].astype(F32)
    gh = gh_ref[...].astype(F32)
    keep = jnp.where(i % tpb == 0, 0.0, 1.0)
    _put_lanes(zext, slice(0, HALO), ah * _sigmoid(gh) * keep)


def _put_lanes(dst3, rows, value):
    for lc in range(value.shape[-1] // LANES):
        dst3[lc, rows, :] = value[:, lc * LANES:(lc + 1) * LANES]


def _tap_windows(src3, lc, base, rows, flip):
    offs = {k: (CONV_TAPS - 1 - k) if flip else (k + 2) for k in range(CONV_TAPS)}
    for r in range(8):
        taps = [k for k in offs if offs[k] % 8 == r]
        lo = min(offs[k] for k in taps)
        hi = max(offs[k] for k in taps)
        win = src3[lc, pl.ds(base + lo, hi - lo + rows), :]
        for k in taps:
            yield k, win[offs[k] - lo:offs[k] - lo + rows]


def _conv_taps(src3, w3_ref, dst3, lc, nrows, flip):
    for b in range(nrows // CONV_ROWS):
        acc = jnp.zeros((CONV_ROWS, LANES), F32)
        for k, win in _tap_windows(src3, lc, b * CONV_ROWS, CONV_ROWS, flip):
            acc = acc + win * w3_ref[lc, k:k + 1, :]
        dst3[lc, b * CONV_ROWS:(b + 1) * CONV_ROWS, :] = acc


def _branches_fwd(l, proj, lng, lnb, ws, bst, cw, cb, blg, blb, t_len):
    n = proj.shape[0]
    d = lng.shape[-1]
    tm = min(TOKEN_TILE, t_len)
    tpb = t_len // tm
    per = tm // HALO
    nchunk = tm // CHUNK

    def body(u_ref, v_ref, a_ref, g_ref, ah_ref, gh_ref, lng_ref, lnb_ref, ws_ref, bst_ref, cw_ref, cb_ref,
             blg_ref, blb_ref, ya_ref, yb_ref, zc_ref, wm_s, zext, zc3):
        i = pl.program_id(0)
        _masked_ws(ws_ref, wm_s)
        _fill_z(i, tpb, ah_ref, gh_ref, zext)

        def chunk(c, carry):
            r0 = pl.multiple_of(c * CHUNK, CHUNK)
            rows = pl.ds(r0, CHUNK)
            vhat, _ = _ln_stats(v_ref[rows, :].astype(F32))
            vn = (vhat * lng_ref[...] + lnb_ref[...]).astype(MXU_DTYPE)
            u = u_ref[rows, :].astype(F32)
            for h in range(HEADS):
                cols = slice(h * CHUNK, (h + 1) * CHUNK)
                s = jnp.dot(wm_s[h], vn[:, cols], preferred_element_type=F32) + bst_ref[:, h:h + 1]
                ya_ref[rows, cols] = (u[:, cols] * s).astype(ACT_DTYPE)
            a = a_ref[rows, :].astype(F32)
            g = g_ref[rows, :].astype(F32)
            _put_lanes(zext, pl.ds(HALO + r0, CHUNK), a * _sigmoid(g))
            return carry

        lax.fori_loop(0, nchunk, chunk, 0)

        def lane_chunk(lc, carry):
            _conv_taps(zext, cw_ref, zc3, lc, tm, flip=False)
            return carry

        lax.fori_loop(0, d // LANES, lane_chunk, 0)

        def chunk2(c, carry):
            r0 = pl.multiple_of(c * CHUNK, CHUNK)
            rows = pl.ds(r0, CHUNK)
            for lc in range(d // LANES):
                lanes = slice(lc * LANES, (lc + 1) * LANES)
                zc_ref[rows, lanes] = (zc3[lc, rows, :] + cb_ref[:, lanes]).astype(ACT_DTYPE)
            zhat, _ = _ln_stats(zc_ref[rows, :].astype(F32))
            zn = zhat * blg_ref[...] + blb_ref[...]
            yb_ref[rows, :] = (zn * _sigmoid(zn)).astype(ACT_DTYPE)
            return carry

        lax.fori_loop(0, nchunk, chunk2, 0)

    col = lambda k: pl.BlockSpec((tm, d), lambda i: (i, k))
    halo = lambda k: pl.BlockSpec((HALO, d), lambda i: (jnp.maximum(i * per - 1, 0), k))
    vec = pl.BlockSpec((None, 1, d), lambda i: (l, 0, 0))
    out = pl.BlockSpec((tm, d), lambda i: (i, 0))
    return pl.pallas_call(
        body, name=f"branches_fwd_{l}",
        out_shape=[jax.ShapeDtypeStruct((n, d), ACT_DTYPE)] * 3,
        grid=(n // tm,),
        in_specs=[col(0), col(1), col(2), col(3), halo(2), halo(3), vec, vec,
                  pl.BlockSpec((None, HEADS, CHUNK, CHUNK), lambda i: (l, 0, 0, 0)),
                  pl.BlockSpec((None, CHUNK, HEADS), lambda i: (l, 0, 0)),
                  pl.BlockSpec((None, d // LANES, HALO, LANES), lambda i: (l, 0, 0, 0)), vec, vec, vec],
        out_specs=[out, out, out],
        scratch_shapes=[pltpu.VMEM((HEADS, CHUNK, CHUNK), MXU_DTYPE), pltpu.VMEM((d // LANES, HALO + tm, LANES), F32),
                        pltpu.VMEM((d // LANES, tm, LANES), F32)],
        compiler_params=_params(("arbitrary",)),
    )(proj, proj, proj, proj, proj, proj, lng, lnb, ws, bst, cw, cb, blg, blb)


def _merge_out(l, x, mod, proj, ya_in, yb_in, wg_pa, wg_pb, wg_out, t_len):
    n, d = x.shape
    tm = min(TOKEN_TILE, t_len)
    tpb = t_len // tm
    rq = d // N_CHIP

    def body(x_ref, mod_ref, ga_ref, gb_ref, yai_ref, ybi_ref, wpa_ref, wpb_ref, wo_ref,
             ya_ref, yb_ref, mg_ref, o_ref, x1_ref):
        wpa = wpa_ref[...].reshape(d, d)
        wpb = wpb_ref[...].reshape(d, d)
        wo = wo_ref[...].reshape(d, d)
        ya = jnp.dot(yai_ref[...].astype(MXU_DTYPE), wpa, preferred_element_type=F32)
        yb = jnp.dot(ybi_ref[...].astype(MXU_DTYPE), wpb, preferred_element_type=F32)
        merged = _sigmoid(ga_ref[...].astype(F32)) * ya + _sigmoid(gb_ref[...].astype(F32)) * yb
        o = _dot(merged, wo)
        ya_ref[...] = ya.astype(ACT_DTYPE)
        yb_ref[...] = yb.astype(ACT_DTYPE)
        mg_ref[...] = merged.astype(ACT_DTYPE)
        o_ref[...] = o.astype(ACT_DTYPE)
        x1_ref[...] = x_ref[...] + mod_ref[2:3, :] * o

    tile = pl.BlockSpec((tm, d), lambda i: (i, 0))
    wspec = pl.BlockSpec((N_CHIP, rq, d), lambda i: (0, 0, 0))
    return pl.pallas_call(
        body, name=f"merge_out_{l}",
        out_shape=[jax.ShapeDtypeStruct((n, d), ACT_DTYPE)] * 4 + [jax.ShapeDtypeStruct((n, d), F32)],
        grid=(n // tm,),
        in_specs=[tile, pl.BlockSpec((None, None, 8, d), lambda i: (l, i // tpb, 0, 0)),
                  pl.BlockSpec((tm, d), lambda i: (i, 4)), pl.BlockSpec((tm, d), lambda i: (i, 5)),
                  tile, tile, wspec, wspec, wspec],
        out_specs=[tile] * 5,
        compiler_params=_params(("arbitrary",)),
    )(x, mod, proj, proj, ya_in, yb_in, wg_pa, wg_pb, wg_out)


def _ffn_fwd(l, x1, mod, g2, wg_ff1, wg_ff2, t_len):
    n, d = x1.shape
    tm = min(TOKEN_TILE, t_len)
    tpb = t_len // tm
    hq = wg_ff1.shape[-1]
    hid = N_CHIP * hq

    def body(x_ref, mod_ref, g_ref, w1_ref, w2_ref, h_ref, f_ref, o2_ref, x2_ref, a2_s):
        h = _modnorm_fwd(x_ref[...], g_ref[...], mod_ref[4:5, :], mod_ref[3:4, :]).astype(MXU_DTYPE)
        h_ref[...] = h.astype(ACT_DTYPE)
        for q in range(N_CHIP):
            cols = slice(q * hq, (q + 1) * hq)
            f = jnp.dot(h, w1_ref[q], preferred_element_type=F32)
            f_ref[:, cols] = f.astype(ACT_DTYPE)
            a2_s[:, cols] = jnp.square(jnp.maximum(f, 0.0)).astype(MXU_DTYPE)
        o2 = jnp.dot(a2_s[...], w2_ref[...].reshape(hid, d), preferred_element_type=F32)
        o2_ref[...] = o2.astype(ACT_DTYPE)
        x2_ref[...] = x_ref[...] + mod_ref[5:6, :] * o2

    tile = pl.BlockSpec((tm, d), lambda i: (i, 0))
    return pl.pallas_call(
        body, name=f"ffn_fwd_{l}",
        out_shape=[jax.ShapeDtypeStruct((n, d), ACT_DTYPE), jax.ShapeDtypeStruct((n, hid), ACT_DTYPE),
                   jax.ShapeDtypeStruct((n, d), ACT_DTYPE), jax.ShapeDtypeStruct((n, d), F32)],
        grid=(n // tm,),
        in_specs=[tile, pl.BlockSpec((None, None, 8, d), lambda i: (l, i // tpb, 0, 0)),
                  pl.BlockSpec((None, 1, d), lambda i: (l, 0, 0)),
                  _resident((N_CHIP, d, hq)), _resident((N_CHIP, hq, d))],
        out_specs=[tile, pl.BlockSpec((tm, hid), lambda i: (i, 0)), tile, tile],
        scratch_shapes=[pltpu.VMEM((tm, hid), MXU_DTYPE)],
        compiler_params=_params(("arbitrary",)),
    )(x1, mod, g2, wg_ff1, wg_ff2)


def _loss_head(x, final_g, target):
    n, d = x.shape
    tm = min(TOKEN_TILE, n)

    def body(x_ref, g_ref, t_ref, loss_ref, dx_ref, dg_ref):
        @pl.when(pl.program_id(0) == 0)
        def _():
            loss_ref[...] = jnp.zeros_like(loss_ref)
            dg_ref[...] = jnp.zeros_like(dg_ref)

        x_t = x_ref[...]
        g = g_ref[...]
        r = lax.rsqrt(_rowmean(x_t * x_t) + EPS)
        xn = x_t * r
        e = xn * g - t_ref[...]
        loss_ref[...] += jnp.sum(e * e) * (0.5 / d)
        dy = e * (1.0 / d)
        dxn = dy * g
        dx_ref[...] = r * (dxn - xn * _rowmean(dxn * xn))
        dg_ref[0:1, :] += _colsum(dy * xn)

    tile = pl.BlockSpec((tm, d), lambda i: (i, 0))
    return pl.pallas_call(
        body, name="loss_head",
        out_shape=[jax.ShapeDtypeStruct((8, LANES), F32), jax.ShapeDtypeStruct((n, d), F32),
                   jax.ShapeDtypeStruct((8, d), F32)],
        grid=(n // tm,),
        in_specs=[tile, pl.BlockSpec((1, d), lambda i: (0, 0)), tile],
        out_specs=[pl.BlockSpec((8, LANES), lambda i: (0, 0)), tile, pl.BlockSpec((8, d), lambda i: (0, 0))],
        compiler_params=_params(("arbitrary",)),
    )(x, final_g, target)


def _norm_tail(x_ref, dxin_ref, dh, g_ref, sc, dx_ref, dmod_ref, dg_ref, row_sh, row_sc):
    dxm, dsh, q = _modnorm_bwd(x_ref[...], dh, g_ref[...], sc)
    dx_ref[...] = dxin_ref[...] + dxm
    dmod_ref[row_sh:row_sh + 1, :] += dsh
    dmod_ref[row_sc:row_sc + 1, :] += g_ref[...] * q
    dg_ref[0:1, :] += (1.0 + sc) * q


def _ffn_bwd(l, dx2, x1, mod, g2, o2, f, wg_ff1, wg_ff2, t_len, nb):
    n, d = dx2.shape
    tm = min(FFN_BWD_TILE, t_len)
    tpb = t_len // tm
    hq = wg_ff1.shape[-1]
    hid = N_CHIP * hq

    def body(dx2_ref, x1_ref, mod_ref, g_ref, o2_ref, f_ref, w1_ref, w2_ref,
             do2_ref, df_ref, dx1_ref, dmod_ref, dg_ref):
        i = pl.program_id(0)

        @pl.when(i == 0)
        def _():
            dg_ref[...] = jnp.zeros_like(dg_ref)

        @pl.when(i % tpb == 0)
        def _():
            dmod_ref[...] = jnp.zeros_like(dmod_ref)

        dx2_t = dx2_ref[...]
        dmod_ref[5:6, :] += _colsum(dx2_t * o2_ref[...].astype(F32))
        do2 = (dx2_t * mod_ref[5:6, :]).astype(MXU_DTYPE)
        do2_ref[...] = do2.astype(ACT_DTYPE)
        dh = jnp.zeros((tm, d), F32)
        for q in range(N_CHIP):
            cols = slice(q * hq, (q + 1) * hq)
            da2 = _dot_nt(do2, w2_ref[q])
            df = (da2 * (2.0 * jnp.maximum(f_ref[:, cols].astype(F32), 0.0))).astype(MXU_DTYPE)
            df_ref[:, cols] = df.astype(ACT_DTYPE)
            dh = dh + _dot_nt(df, w1_ref[q])
        _norm_tail(x1_ref, dx2_ref, dh, g_ref, mod_ref[4:5, :], dx1_ref, dmod_ref, dg_ref, 3, 4)

    tile = pl.BlockSpec((tm, d), lambda i: (i, 0))
    wide = pl.BlockSpec((tm, hid), lambda i: (i, 0))
    return pl.pallas_call(
        body, name=f"ffn_bwd_{l}",
        out_shape=[jax.ShapeDtypeStruct((n, d), ACT_DTYPE), jax.ShapeDtypeStruct((n, hid), ACT_DTYPE),
                   jax.ShapeDtypeStruct((n, d), F32), jax.ShapeDtypeStruct((nb, 8, d), F32),
                   jax.ShapeDtypeStruct((8, d), F32)],
        grid=(n // tm,),
        in_specs=[tile, tile, pl.BlockSpec((None, None, 8, d), lambda i: (l, i // tpb, 0, 0)),
                  pl.BlockSpec((None, 1, d), lambda i: (l, 0, 0)), tile, wide,
                  _resident((N_CHIP, d, hq)), _resident((N_CHIP, hq, d))],
        out_specs=[tile, wide, tile, pl.BlockSpec((None, 8, d), lambda i: (i // tpb, 0, 0)),
                   pl.BlockSpec((8, d), lambda i: (0, 0))],
        compiler_params=_params(("arbitrary",)),
    )(dx2, x1, mod, g2, o2, f, wg_ff1, wg_ff2)


def _merge_bwd(l, dx1, mod, o, ya, yb, proj, wg_pa, wg_pb, wg_out, t_len, nb, after):
    n, d = dx1.shape
    tm = min(TOKEN_TILE, t_len)
    tpb = t_len // tm
    rq = d // N_CHIP

    def body(dx_ref, mod_ref, o_ref, ya_ref, yb_ref, ga_ref, gb_ref, wpa_ref, wpb_ref, wo_ref, after_ref,
             do_ref, dya_ref, dyb_ref, dyai_ref, dybi_ref, dproj_ref, dmod_ref):
        i = pl.program_id(0)

        @pl.when(i % tpb == 0)
        def _():
            dmod_ref[...] = jnp.zeros_like(dmod_ref)

        dx = dx_ref[...]
        dmod_ref[2:3, :] += _colsum(dx * o_ref[...].astype(F32))
        do = (dx * mod_ref[2:3, :]).astype(MXU_DTYPE)
        do_ref[...] = do.astype(ACT_DTYPE)
        dm = _dot_nt(do, wo_ref[...].reshape(d, d))
        sa = _sigmoid(ga_ref[...].astype(F32))
        sb = _sigmoid(gb_ref[...].astype(F32))
        dya = (dm * sa).astype(MXU_DTYPE)
        dyb = (dm * sb).astype(MXU_DTYPE)
        dya_ref[...] = dya.astype(ACT_DTYPE)
        dyb_ref[...] = dyb.astype(ACT_DTYPE)
        dproj_ref[:, 0:d] = (dm * ya_ref[...].astype(F32) * sa * (1.0 - sa)).astype(ACT_DTYPE)
        dproj_ref[:, d:2 * d] = (dm * yb_ref[...].astype(F32) * sb * (1.0 - sb)).astype(ACT_DTYPE)
        dyai_ref[...] = _dot_nt(dya, wpa_ref[...].reshape(d, d)).astype(ACT_DTYPE)
        dybi_ref[...] = _dot_nt(dyb, wpb_ref[...].reshape(d, d)).astype(ACT_DTYPE)

    tile = pl.BlockSpec((tm, d), lambda i: (i, 0))
    wspec = pl.BlockSpec((N_CHIP, rq, d), lambda i: (0, 0, 0))
    return pl.pallas_call(
        body, name=f"merge_bwd_{l}",
        out_shape=[jax.ShapeDtypeStruct((n, d), ACT_DTYPE)] * 5
        + [jax.ShapeDtypeStruct((n, 6 * d), ACT_DTYPE), jax.ShapeDtypeStruct((nb, 8, d), F32)],
        grid=(n // tm,),
        in_specs=[tile, pl.BlockSpec((None, None, 8, d), lambda i: (l, i // tpb, 0, 0)), tile, tile, tile,
                  pl.BlockSpec((tm, d), lambda i: (i, 4)), pl.BlockSpec((tm, d), lambda i: (i, 5)),
                  wspec, wspec, wspec, ANY_SPEC],
        out_specs=[tile] * 5 + [pl.BlockSpec((tm, 2 * d), lambda i: (i, 2)),
                                pl.BlockSpec((None, 8, d), lambda i: (i // tpb, 0, 0))],
        compiler_params=_params(("arbitrary",)),
    )(dx1, mod, o, ya, yb, proj, proj, wg_pa, wg_pb, wg_out, after)


def _branches_bwd(l, proj, zc, dya_in, dyb_in, dproj, lng, lnb, ws, bst, cw, blg, blb, t_len, after):
    n = proj.shape[0]
    d = lng.shape[-1]
    tm = min(TOKEN_TILE, t_len)
    tpb = t_len // tm
    per = tm // HALO
    nchunk = tm // CHUNK
    ntile = n // tm

    def body(u_ref, v_ref, a_ref, g_ref, ah_ref, gh_ref, zc_ref, zcn_ref, dya_ref, dyb_ref, dybn_ref, dproj_in,
             lng_ref, lnb_ref, ws_ref, bst_ref, cw_ref, blg_ref, blb_ref, after_ref,
             dproj_ref, dws_ref, dbst_ref, dcw_ref, vec_ref, wm_s, zext, dzext, dz3, dvn_s):
        i = pl.program_id(0)

        @pl.when(i == 0)
        def _():
            dws_ref[...] = jnp.zeros_like(dws_ref)
            dbst_ref[...] = jnp.zeros_like(dbst_ref)
            dcw_ref[...] = jnp.zeros_like(dcw_ref)
            vec_ref[...] = jnp.zeros_like(vec_ref)

        _masked_ws(ws_ref, wm_s)
        _fill_z(i, tpb, ah_ref, gh_ref, zext)

        def conv_ln_bwd(zc_t, dyb_t):
            zhat, rstd = _ln_stats(zc_t)
            zn = zhat * blg_ref[...] + blb_ref[...]
            sg = _sigmoid(zn)
            dzn = dyb_t * (sg * (1.0 + zn * (1.0 - sg)))
            return _ln_bwd(dzn, zhat, rstd, blg_ref[...]), _colsum(dzn * zhat), _colsum(dzn)

        def chunk(c, carry):
            r0 = pl.multiple_of(c * CHUNK, CHUNK)
            rows = pl.ds(r0, CHUNK)
            vhat, rstd = _ln_stats(v_ref[rows, :].astype(F32))
            vn = (vhat * lng_ref[...] + lnb_ref[...]).astype(MXU_DTYPE)
            u = u_ref[rows, :].astype(F32)
            dya = dya_ref[rows, :].astype(F32)
            for h in range(HEADS):
                cols = slice(h * CHUNK, (h + 1) * CHUNK)
                s = jnp.dot(wm_s[h], vn[:, cols], preferred_element_type=F32) + bst_ref[:, h:h + 1]
                dproj_ref[rows, cols] = (dya[:, cols] * s).astype(ACT_DTYPE)
                ds = dya[:, cols] * u[:, cols]
                dvn_s[:, cols] = _dot_tn(wm_s[h], ds)
                dws_ref[h] += _dot_nt(ds, vn[:, cols])
                dbst_ref[:, h:h + 1] += jnp.sum(ds, axis=1, keepdims=True)
            dvn = dvn_s[...]
            dproj_ref[rows, d:2 * d] = _ln_bwd(dvn, vhat, rstd, lng_ref[...]).astype(ACT_DTYPE)
            vec_ref[0:1, :] += _colsum(dvn * vhat)
            vec_ref[1:2, :] += _colsum(dvn)
            a = a_ref[rows, :].astype(F32)
            g = g_ref[rows, :].astype(F32)
            _put_lanes(zext, pl.ds(HALO + r0, CHUNK), a * _sigmoid(g))
            dzc, dblg, dblb = conv_ln_bwd(zc_ref[rows, :].astype(F32), dyb_ref[rows, :].astype(F32))
            _put_lanes(dzext, rows, dzc)
            vec_ref[2:3, :] += _colsum(dzc)
            vec_ref[3:4, :] += dblg
            vec_ref[4:5, :] += dblb
            return carry

        lax.fori_loop(0, nchunk, chunk, 0)

        dzc_next, _, _ = conv_ln_bwd(zcn_ref[...].astype(F32), dybn_ref[...].astype(F32))
        _put_lanes(dzext, slice(tm, tm + HALO), dzc_next * jnp.where(i % tpb == tpb - 1, 0.0, 1.0))

        def lane_chunk_dz(lc, carry):
            _conv_taps(dzext, cw_ref, dz3, lc, tm, flip=True)
            return carry

        lax.fori_loop(0, d // LANES, lane_chunk_dz, 0)

        def lane_chunk(lc, carry):
            accs = [jnp.zeros((8, LANES), F32) for _ in range(CONV_TAPS)]
            for b in range(tm // TAP_GRAD_ROWS):
                dzc = dzext[lc, b * TAP_GRAD_ROWS:(b + 1) * TAP_GRAD_ROWS, :]
                for k, win in _tap_windows(zext, lc, b * TAP_GRAD_ROWS, TAP_GRAD_ROWS, flip=False):
                    prod = dzc * win
                    part = prod[0:8]
                    for e in range(1, TAP_GRAD_ROWS // 8):
                        part = part + prod[8 * e:8 * e + 8]
                    accs[k] = accs[k] + part
            for k in range(CONV_TAPS):
                dcw_ref[lc, k:k + 1, :] += _colsum(accs[k])
            return carry

        lax.fori_loop(0, d // LANES, lane_chunk, 0)

        def glu_bwd(c, carry):
            r0 = pl.multiple_of(c * CHUNK, CHUNK)
            rows = pl.ds(r0, CHUNK)
            for lc in range(d // LANES):
                lanes = slice(lc * LANES, (lc + 1) * LANES)
                dz = dz3[lc, rows, :]
                a = a_ref[rows, lanes].astype(F32)
                sg = _sigmoid(g_ref[rows, lanes].astype(F32))
                dproj_ref[rows, 2 * d + lc * LANES:2 * d + (lc + 1) * LANES] = (dz * sg).astype(ACT_DTYPE)
                dproj_ref[rows, 3 * d + lc * LANES:3 * d + (lc + 1) * LANES] = (
                    dz * a * sg * (1.0 - sg)).astype(ACT_DTYPE)
            return carry

        lax.fori_loop(0, nchunk, glu_bwd, 0)

        @pl.when(i == ntile - 1)
        def _():
            mask = _causal_mask()
            for h in range(HEADS):
                dws_ref[h] = jnp.where(mask, dws_ref[h], 0.0)

    col = lambda k: pl.BlockSpec((tm, d), lambda i: (i, k))
    tile = pl.BlockSpec((tm, d), lambda i: (i, 0))
    before = lambda k: pl.BlockSpec((HALO, d), lambda i: (jnp.maximum(i * per - 1, 0), k))
    following = pl.BlockSpec((HALO, d), lambda i: (jnp.minimum((i + 1) * per, n // HALO - 1), 0))
    vec = pl.BlockSpec((None, 1, d), lambda i: (l, 0, 0))
    const2 = lambda r, c: pl.BlockSpec((r, c), lambda i: (0, 0))
    return pl.pallas_call(
        body, name=f"branches_bwd_{l}",
        out_shape=[jax.ShapeDtypeStruct((n, 6 * d), ACT_DTYPE), jax.ShapeDtypeStruct((HEADS, CHUNK, CHUNK), F32),
                   jax.ShapeDtypeStruct((CHUNK, HEADS), F32), jax.ShapeDtypeStruct((d // LANES, HALO, LANES), F32),
                   jax.ShapeDtypeStruct((8, d), F32)],
        grid=(ntile,),
        in_specs=[col(0), col(1), col(2), col(3), before(2), before(3), tile, following, tile, tile, following,
                  pl.BlockSpec(memory_space=pl.ANY), vec, vec,
                  pl.BlockSpec((None, HEADS, CHUNK, CHUNK), lambda i: (l, 0, 0, 0)),
                  pl.BlockSpec((None, CHUNK, HEADS), lambda i: (l, 0, 0)),
                  pl.BlockSpec((None, d // LANES, HALO, LANES), lambda i: (l, 0, 0, 0)), vec, vec, ANY_SPEC],
        out_specs=[pl.BlockSpec((tm, 4 * d), lambda i: (i, 0)),
                   pl.BlockSpec((HEADS, CHUNK, CHUNK), lambda i: (0, 0, 0)),
                   const2(CHUNK, HEADS), pl.BlockSpec((d // LANES, HALO, LANES), lambda i: (0, 0, 0)), const2(8, d)],
        scratch_shapes=[pltpu.VMEM((HEADS, CHUNK, CHUNK), MXU_DTYPE),
                        pltpu.VMEM((d // LANES, HALO + tm, LANES), F32),
                        pltpu.VMEM((d // LANES, tm + HALO, LANES), F32),
                        pltpu.VMEM((d // LANES, tm, LANES), F32), pltpu.VMEM((CHUNK, d), F32)],
        input_output_aliases={11: 0},
        compiler_params=_params(("arbitrary",)),
    )(proj, proj, proj, proj, proj, proj, zc, zc, dya_in, dyb_in, dyb_in, dproj, lng, lnb, ws, bst, cw, blg, blb, after)


def _in_proj_bwd(l, dproj, dx1, x, mod, g1, wg_in, t_len, nb, after):
    n, d = x.shape
    tm = min(TOKEN_TILE, t_len)
    tpb = t_len // tm
    qc = wg_in.shape[-1]

    def body(dp_ref, dx1_ref, x_ref, mod_ref, g_ref, w_ref, after_ref, dx_ref, dmod_ref, dg_ref):
        i = pl.program_id(0)

        @pl.when(i == 0)
        def _():
            dg_ref[...] = jnp.zeros_like(dg_ref)

        @pl.when(i % tpb == 0)
        def _():
            dmod_ref[...] = jnp.zeros_like(dmod_ref)

        dh = jnp.zeros((tm, d), F32)
        for q in range(N_CHIP):
            dh = dh + _dot_nt(dp_ref[:, q * qc:(q + 1) * qc], w_ref[q])
        _norm_tail(x_ref, dx1_ref, dh, g_ref, mod_ref[1:2, :], dx_ref, dmod_ref, dg_ref, 0, 1)

    tile = pl.BlockSpec((tm, d), lambda i: (i, 0))
    return pl.pallas_call(
        body, name=f"in_proj_bwd_{l}",
        out_shape=[jax.ShapeDtypeStruct((n, d), F32), jax.ShapeDtypeStruct((nb, 8, d), F32),
                   jax.ShapeDtypeStruct((8, d), F32)],
        grid=(n // tm,),
        in_specs=[pl.BlockSpec((tm, N_CHIP * qc), lambda i: (i, 0)), tile, tile,
                  pl.BlockSpec((None, None, 8, d), lambda i: (l, i // tpb, 0, 0)),
                  pl.BlockSpec((None, 1, d), lambda i: (l, 0, 0)),
                  _resident((N_CHIP, d, qc)), ANY_SPEC],
        out_specs=[tile, pl.BlockSpec((None, 8, d), lambda i: (i // tpb, 0, 0)),
                   pl.BlockSpec((8, d), lambda i: (0, 0))],
        compiler_params=_params(("arbitrary",)),
    )(dproj, dx1, x, mod, g1, wg_in, after)


def _weight_grad(name, a, b, a_spec, b_spec, out_rows, out_spec, acc_shape, grid_ij, relu2=False):
    n = a.shape[0]
    tk = min(MATMUL_TILE, n)
    nk = n // tk
    cols = acc_shape[1]

    def body(a_ref, b_ref, o_ref, acc):
        k = pl.program_id(2)

        @pl.when(k == 0)
        def _():
            acc[...] = jnp.zeros_like(acc)

        a_t = a_ref[...]
        if relu2:
            a_t = jnp.square(jnp.maximum(a_t.astype(F32), 0.0))
        acc[...] += _dot_tn(a_t, b_ref[...])

        @pl.when(k == nk - 1)
        def _():
            o_ref[...] = acc[...].reshape(o_ref.shape).astype(WIRE_DTYPE)

    gi, gj = grid_ij
    return pl.pallas_call(
        body, name=name, out_shape=jax.ShapeDtypeStruct((N_CHIP, out_rows, cols), WIRE_DTYPE),
        grid=(gi, gj, nk),
        in_specs=[a_spec(tk), b_spec(tk)],
        out_specs=out_spec,
        scratch_shapes=[pltpu.VMEM(acc_shape, F32)],
        compiler_params=_params(("arbitrary", "arbitrary", "arbitrary")),
    )(a, b)


def _row_tile(rows, cols, arrays):
    budget = VMEM_LIMIT // 3
    t = budget // (arrays * 2 * cols * 4)
    t = max(8, min(rows, t // 8 * 8))
    while rows % t:
        t -= 8
    return t


def _sum_partials(name, own, got, myq, l, nl, prev):
    _, rows, cols = own.shape
    tr = _row_tile(rows, cols, 3)
    nt = rows // tr

    def body(q_ref, own_ref, got_ref, *rest):
        o_ref = rest[-1]
        acc = own_ref[...].astype(F32)
        for k in range(3):
            acc = acc + got_ref[k].astype(F32)
        o_ref[...] = acc

    operands = [myq, own, got] + ([] if prev is None else [prev])
    return pl.pallas_call(
        body, name=name, out_shape=jax.ShapeDtypeStruct((nl * rows, cols), F32),
        grid_spec=pltpu.PrefetchScalarGridSpec(
            num_scalar_prefetch=1, grid=(nt,),
            in_specs=[pl.BlockSpec((None, tr, cols), lambda i, q: (q[0], i, 0)),
                      pl.BlockSpec((3, tr, cols), lambda i, q: (0, i, 0))]
            + ([] if prev is None else [pl.BlockSpec(memory_space=pl.ANY)]),
            out_specs=pl.BlockSpec((tr, cols), lambda i, q: (l * nt + i, 0))),
        input_output_aliases={} if prev is None else {3: 0},
        compiler_params=_params(("arbitrary",)),
    )(*operands)


def _adamw(name, w, m, v, g_a, g_b=None):
    rows, cols = w.shape
    tr = _row_tile(rows, cols, 9)
    c1 = 1.0 - ADAM_B1 ** ADAM_STEP
    c2 = 1.0 - ADAM_B2 ** ADAM_STEP

    def body(*refs):
        if g_b is None:
            w_ref, m_ref, v_ref, ga_ref, g_ref, d_ref, m2_ref, v2_ref = refs
            g = ga_ref[...]
        else:
            w_ref, m_ref, v_ref, ga_ref, gb_ref, g_ref, d_ref, m2_ref, v2_ref = refs
            g = ga_ref[...] + gb_ref[...]
        m2 = ADAM_B1 * m_ref[...] + (1.0 - ADAM_B1) * g
        v2 = ADAM_B2 * v_ref[...] + (1.0 - ADAM_B2) * (g * g)
        g_ref[...] = g
        m2_ref[...] = m2
        v2_ref[...] = v2
        d_ref[...] = -ADAM_LR * ((m2 / c1) / (jnp.sqrt(v2 / c2) + ADAM_EPS) + ADAM_WD * w_ref[...])

    tile = pl.BlockSpec((tr, cols), lambda i: (i, 0))
    operands = [w, m, v, g_a] + ([] if g_b is None else [g_b])
    return pl.pallas_call(
        body, name=name, out_shape=[jax.ShapeDtypeStruct((rows, cols), F32)] * 4,
        grid=(rows // tr,), in_specs=[tile] * len(operands), out_specs=[tile] * 4,
        compiler_params=_params(("arbitrary",)),
    )(*operands)


def _pack(parts):
    flat = [p.reshape(-1, LANES) for p in parts]
    for f in flat:
        assert f.shape[0] % 8 == 0
    return jnp.concatenate(flat, axis=0)


def _unpack(packed, shapes):
    out, r = [], 0
    for s in shapes:
        size = 1
        for e in s:
            size *= e
        rows = size // LANES
        out.append(packed[r:r + rows].reshape(s))
        r += rows
    return out


def kernel(x, c, w_ada, b_ada, norm1_g, w_in, a_ln_g, a_ln_b, a_ws, a_bs, w_pa, b_conv_w, b_conv_b, b_ln_g, b_ln_b, w_pb, w_out, norm2_g, w_ff1, w_ff2, final_g, loss_target, m_w_ada, m_b_ada, m_norm1_g, m_w_in, m_a_ln_g, m_a_ln_b, m_a_ws, m_a_bs, m_w_pa, m_b_conv_w, m_b_conv_b, m_b_ln_g, m_b_ln_b, m_w_pb, m_w_out, m_norm2_g, m_w_ff1, m_w_ff2, m_final_g, v_w_ada, v_b_ada, v_norm1_g, v_w_in, v_a_ln_g, v_a_ln_b, v_a_ws, v_a_bs, v_w_pa, v_b_conv_w, v_b_conv_b, v_b_ln_g, v_b_ln_b, v_w_pb, v_w_out, v_norm2_g, v_w_ff1, v_w_ff2, v_final_g):
    nb, t_len, d = x.shape
    nl = w_in.shape[0]
    n = nb * t_len
    cq = w_ada.shape[-1]
    cc = d // N_CHIP
    mx, my, mc = _my_place()
    myq = (2 * mx + my).astype(jnp.int32).reshape(1)

    taps = jnp.pad(b_conv_w.reshape(nl, CONV_TAPS, cc), ((0, 0), (0, HALO - CONV_TAPS), (0, 0)))
    first = jnp.concatenate([jnp.pad(c, ((0, 8 - nb), (0, 0))), taps.reshape(nl * HALO * cc // d, d)], axis=0)
    first = _all_to_all(jnp.broadcast_to(first[None], (N_DEV,) + first.shape), "gather_c_and_taps")
    c_all = first[:, :nb].reshape(N_DEV * nb, d)
    cwg = first[:, 8:].reshape(N_CHIP, 2, nl, HALO, cc)[:, 0]
    cw = cwg.transpose(1, 2, 0, 3).reshape(nl, HALO, d)
    cw = cw.reshape(nl, HALO, d // LANES, LANES).transpose(0, 2, 1, 3)
    mod_part = _ada_forward(c_all, w_ada, b_ada.reshape(nl, 1, N_CHIP * cq), myq)
    mod_slots = mod_part.reshape(nl, N_DEV, nb, cq).transpose(1, 0, 2, 3).reshape(N_DEV, nl * nb, cq)
    mod_got = _all_to_all(mod_slots, "exchange_mod").reshape(N_CHIP, 2, nl, nb, cq)[:, 0]
    mod6 = mod_got.transpose(1, 2, 0, 3).reshape(nl, nb, 6, d)
    mod = jnp.pad(mod6, ((0, 0), (0, 0), (0, 2), (0, 0)))

    big = ["w_in", "w_pa", "w_pb", "w_out", "w_ff1", "w_ff2"]
    ws_given = dict(w_in=(w_in, m_w_in, v_w_in), w_pa=(w_pa, m_w_pa, v_w_pa), w_pb=(w_pb, m_w_pb, v_w_pb),
                    w_out=(w_out, m_w_out, v_w_out), w_ff1=(w_ff1, m_w_ff1, v_w_ff1), w_ff2=(w_ff2, m_w_ff2, v_w_ff2))

    def own_slot(w_l, after=None):
        if after is not None:
            w_l = w_l - after[0, 0]
        empty = lax.empty((N_CHIP,) + w_l.shape, WIRE_DTYPE)
        return lax.dynamic_update_index_in_dim(empty, w_l.astype(WIRE_DTYPE), myq[0], 0)

    def zero_after(*arrays):
        z = jnp.zeros((8, LANES), F32)
        for a in arrays:
            piece = a.reshape(-1, a.shape[-1])[:8, :LANES]
            z = z + jnp.where(jnp.isfinite(piece), piece, 0.0) * 0.0
        return z

    first_sems = _split_start("gather_start_in_0", "gather", [], [own_slot(w_in[0])], zero_after(cw, mod[:, 0]))
    token = first = first_sems[4]
    gathers = []
    for l in range(nl):
        group = big[1:] if l == 0 else big
        send_sems, recv_sems, _, lands, token = _split_start(
            f"gather_start_{l}", "gather", [], [own_slot(ws_given[k][0][l], first) for k in group], token)
        if l == 0:
            send_sems = list(first_sems[0]) + list(send_sems)
            recv_sems = list(first_sems[1]) + list(recv_sems)
            lands = list(first_sems[3]) + list(lands)
        gathers.append((send_sems, recv_sems, lands))
    mod = mod + token[0, 0]

    def gather_wait(l, part, lo, hi, after):
        send_sems, recv_sems, lands = gathers[l]
        return _split_wait(f"gather_wait_{part}_{l}", "gather", send_sems[lo:hi], recv_sems[lo:hi], [],
                           lands[lo:hi], after)[1]

    vec3 = lambda p: p.reshape(nl, 1, d)
    g1, g2 = vec3(norm1_g), vec3(norm2_g)
    lng, lnb, cb, blg, blb = vec3(a_ln_g), vec3(a_ln_b), vec3(b_conv_b), vec3(b_ln_g), vec3(b_ln_b)
    bst = a_bs.transpose(0, 2, 1)

    xs = x.reshape(n, d)
    saved = []
    weights = []
    for l in range(nl):
        if l == 0:
            send_sems, recv_sems, lands = gathers[0]
            wg_in = lands[0]
            h, proj = _in_proj_quarter("in_proj_0_own", l, xs, mod, g1, wg_in, myq, t_len)
            for k, (px, py) in enumerate(_other_chips(mx, my)):
                (wg_in,) = _split_wait(f"gather_wait_in_0_{k}", "gather", send_sems[:1], recv_sems[:1], [], [wg_in],
                                       proj, peers=(k,))[1]
                quarter = (2 * px + py).astype(jnp.int32).reshape(1)
                proj = _in_proj_quarter(f"in_proj_0_{k}", l, h, None, None, wg_in, quarter, t_len, proj)
        else:
            (wg_in,) = gather_wait(l, "in", 0, 1, xs)
            h, proj = _in_proj(l, xs, mod, g1, wg_in, t_len)
        ya_in, yb_in, zc = _branches_fwd(l, proj, lng, lnb, a_ws, bst, cw, cb, blg, blb, t_len)
        wg_pa, wg_pb, wg_out = gather_wait(l, "mid", 1, 4, ya_in)
        ya, yb, merged, o, x1 = _merge_out(l, xs, mod, proj, ya_in, yb_in, wg_pa, wg_pb, wg_out, t_len)
        wg_ff1, wg_ff2 = gather_wait(l, "ffn", 4, 6, x1)
        h2, f, o2, x2 = _ffn_fwd(l, x1, mod, g2, wg_ff1, wg_ff2, t_len)
        saved.append((xs, h, proj, ya_in, yb_in, zc, ya, yb, merged, o, x1, h2, f, o2))
        weights.append((wg_in, wg_pa, wg_pb, wg_out, wg_ff1, wg_ff2))
        xs = x2

    loss_blk, dx, dfinal = _loss_head(xs, final_g.reshape(1, d), loss_target.reshape(n, d))

    tok = lambda w: (lambda tk: pl.BlockSpec((tk, w), lambda i, j, k: (k, 0)))
    tok_i = lambda w: (lambda tk: pl.BlockSpec((tk, w), lambda i, j, k: (k, i)))
    tok_j = lambda w: (lambda tk: pl.BlockSpec((tk, w), lambda i, j, k: (k, j)))
    qin = weights[0][0].shape[-1]
    hq = weights[0][4].shape[-1]
    rq = d // N_CHIP
    slot_i = lambda r, cdim: pl.BlockSpec((None, r, cdim), lambda i, j, k: (i, 0, 0))
    slot_j = lambda r, cdim: pl.BlockSpec((None, r, cdim), lambda i, j, k: (j, 0, 0))
    all_slots = pl.BlockSpec((N_CHIP, rq, d), lambda i, j, k: (0, 0, 0))
    scatters = []

    def scatter_start(l, part, names, grads, after):
        lands = [lax.empty((3,) + g.shape[1:], g.dtype) for g in grads]
        send_sems, recv_sems, srcs, lands, tok_out = _split_start(f"scatter_start_{part}_{l}", "scatter", grads, lands,
                                                                  after)
        scatters.append((f"scatter_wait_{part}_{l}", l, names, send_sems, recv_sems, srcs, lands))
        return tok_out

    dmods, small = [None] * nl, [None] * nl
    for l in reversed(range(nl)):
        x0, h, proj, ya_in, yb_in, zc, ya, yb, merged, o, x1, h2, f, o2 = saved[l]
        wg_in, wg_pa, wg_pb, wg_out, wg_ff1, wg_ff2 = weights[l]
        do2, df, dx1, dmod_c, dg2 = _ffn_bwd(l, dx, x1, mod, g2, o2, f, wg_ff1, wg_ff2, t_len, nb)
        g_ff2 = _weight_grad(f"grad_w_ff2_{l}", f, do2, tok_i(hq), tok(d), hq, slot_i(hq, d), (hq, d), (N_CHIP, 1),
                             relu2=True)
        g_ff1 = _weight_grad(f"grad_w_ff1_{l}", h2, df, tok(d), tok_j(hq), d, slot_j(d, hq), (d, hq), (1, N_CHIP))
        if l == 0:
            token = scatter_start(l, "ffn", ["w_ff2", "w_ff1"], [g_ff2, g_ff1], token)
        do, dya, dyb, dya_in, dyb_in, dproj, dmod_b = _merge_bwd(l, dx1, mod, o, ya, yb, proj, wg_pa, wg_pb, wg_out,
                                                                 t_len, nb, token)
        g_out = _weight_grad(f"grad_w_out_{l}", merged, do, tok(d), tok(d), rq, all_slots, (d, d), (1, 1))
        g_pa = _weight_grad(f"grad_w_pa_{l}", ya_in, dya, tok(d), tok(d), rq, all_slots, (d, d), (1, 1))
        g_pb = _weight_grad(f"grad_w_pb_{l}", yb_in, dyb, tok(d), tok(d), rq, all_slots, (d, d), (1, 1))
        if l == 0:
            token = scatter_start(l, "mid", ["w_out", "w_pa", "w_pb"], [g_out, g_pa, g_pb], token)
        dproj, dws, dbst, dcw, vecs = _branches_bwd(l, proj, zc, dya_in, dyb_in, dproj, lng, lnb, a_ws, bst, cw,
                                                    blg, blb, t_len, token)
        g_in = _weight_grad(f"grad_w_in_{l}", h, dproj, tok(d), tok_j(qin), d, slot_j(d, qin), (d, qin), (1, N_CHIP))
        if l == 0:
            token = scatter_start(l, "in", ["w_in"], [g_in], token)
        else:
            token = scatter_start(l, "all", ["w_ff2", "w_ff1", "w_out", "w_pa", "w_pb", "w_in"],
                                  [g_ff2, g_ff1, g_out, g_pa, g_pb, g_in], token)
        dx, dmod_a, dg1 = _in_proj_bwd(l, dproj, dx1, x0, mod, g1, wg_in, t_len, nb, token)
        dmods[l] = jnp.concatenate([dmod_a[:, 0:2], dmod_b[:, 2:3], dmod_c[:, 3:6]], axis=1)
        dcw = dcw.transpose(1, 0, 2).reshape(HALO, d)[:CONV_TAPS]
        small[l] = (dg1[0], vecs[0], vecs[1], dws, dbst.T, dcw, vecs[2], vecs[3], vecs[4], dg2[0])
    grad_x = dx.reshape(nb, t_len, d)

    names = ["norm1_g", "a_ln_g", "a_ln_b", "a_ws", "a_bs", "b_conv_w", "b_conv_b", "b_ln_g", "b_ln_b", "norm2_g"]
    stacked = [jnp.stack([small[l][k] for l in range(nl)]) for k in range(len(names))]
    stacked[5] = jnp.pad(stacked[5], ((0, 0), (0, HALO - CONV_TAPS), (0, 0)))
    stacked += [dfinal, loss_blk]
    part_shapes = [s.shape for s in stacked]
    packed = _pack(stacked)
    prow = packed.shape[0]
    pad_rows = (-prow) % (8 * N_DEV)
    packed = jnp.pad(packed, ((0, pad_rows), (0, 0)))
    srow = packed.shape[0] // N_DEV
    mine = _all_to_all(packed.reshape(N_DEV, srow, LANES), "reduce_small", reduce=True)
    dmod_rows = nl * nb * 6 * d // LANES
    second = jnp.concatenate([mine, jnp.stack(dmods).reshape(dmod_rows, LANES)], axis=0)
    second = _all_to_all(jnp.broadcast_to(second[None], (N_DEV,) + second.shape), "gather_small_and_dmod")
    total = second[:, :srow].reshape(N_DEV * srow, LANES)[:prow]
    dmod_all = second[:, srow:].reshape(N_DEV, nl, nb, 6 * d).transpose(1, 0, 2, 3).reshape(nl, N_DEV * nb, 6 * d)

    half = dict.fromkeys(big)
    for name, l, group, send_sems, recv_sems, srcs, lands in scatters:
        srcs, lands = _split_wait(name, "scatter", send_sems, recv_sems, srcs, lands, total)
        for k, g_own, g_got in zip(group, srcs, lands):
            half[k] = _sum_partials(f"sum_{k}_{l}", g_own, g_got, myq, l, nl, half[k])
    sums = [half[k] for k in big]
    swap_send, swap_recv, sums, others, token = _split_start(
        "swap_start", "swap", sums, [lax.empty(s.shape, s.dtype) for s in sums], total)
    g_w_ada, g_b_ada = _ada_backward(c_all, dmod_all, myq, cq, token)

    sg = dict(zip(names + ["final_g", "loss"], _unpack(total, part_shapes)))
    loss = sg["loss"][0, 0]
    sg["b_conv_w"] = lax.dynamic_slice_in_dim(sg["b_conv_w"][:, :CONV_TAPS], myq[0] * cc, cc, axis=2).reshape(
        nl, CONV_TAPS, 1, cc)
    sg["final_g"] = sg["final_g"][0]
    sg["b_ada"] = g_b_ada.reshape(nl, N_CHIP * cq)
    small_names = ["b_ada", "norm1_g", "a_ln_g", "a_ln_b", "a_ws", "a_bs", "b_conv_w", "b_conv_b", "b_ln_g",
                   "b_ln_b", "norm2_g", "final_g"]
    given = dict(b_ada=(b_ada, m_b_ada, v_b_ada), norm1_g=(norm1_g, m_norm1_g, v_norm1_g),
                 a_ln_g=(a_ln_g, m_a_ln_g, v_a_ln_g), a_ln_b=(a_ln_b, m_a_ln_b, v_a_ln_b),
                 a_ws=(a_ws, m_a_ws, v_a_ws), a_bs=(a_bs, m_a_bs, v_a_bs),
                 b_conv_w=(b_conv_w, m_b_conv_w, v_b_conv_w), b_conv_b=(b_conv_b, m_b_conv_b, v_b_conv_b),
                 b_ln_g=(b_ln_g, m_b_ln_g, v_b_ln_g), b_ln_b=(b_ln_b, m_b_ln_b, v_b_ln_b),
                 norm2_g=(norm2_g, m_norm2_g, v_norm2_g), final_g=(final_g, m_final_g, v_final_g))

    def padded(a):
        rows = -(-a.size // (8 * LANES)) * 8
        return jnp.pad(a.reshape(-1), (0, rows * LANES - a.size)).reshape(rows, LANES)

    packs = [_pack([padded(given[k][j]) for k in small_names]) for j in range(3)]
    gpack = _pack([padded(sg[k].astype(F32)) for k in small_names])
    res_small = _adamw("adamw_small", packs[0], packs[1], packs[2], gpack)
    out = {}
    for j, kind in enumerate(["grad", "delta", "new_m", "new_v"]):
        r = 0
        for k in small_names:
            a = given[k][0]
            rows = -(-a.size // (8 * LANES)) * 8
            out[(kind, k)] = res_small[j][r:r + rows].reshape(-1)[:a.size].reshape(a.shape)
            r += rows

    res = _adamw("adamw_w_ada", w_ada.reshape(nl * d, cq), m_w_ada.reshape(nl * d, cq), v_w_ada.reshape(nl * d, cq),
                 g_w_ada.reshape(nl * d, cq))
    for kind, r in zip(["grad", "delta", "new_m", "new_v"], res):
        out[(kind, "w_ada")] = r.reshape(w_ada.shape)

    sums, others = _split_wait("swap_wait", "swap", swap_send, swap_recv, sums, others, res[0])
    for k, s_mine, s_other in zip(big, sums, others):
        w, m, v = ws_given[k]
        cols = w.shape[-1]
        res = _adamw(f"adamw_{k}", w.reshape(-1, cols), m.reshape(-1, cols), v.reshape(-1, cols), s_mine, s_other)
        for kind, r in zip(["grad", "delta", "new_m", "new_v"], res):
            out[(kind, k)] = r.reshape(w.shape)

    order = ["w_ada", "b_ada", "norm1_g", "w_in", "a_ln_g", "a_ln_b", "a_ws", "a_bs", "w_pa", "b_conv_w", "b_conv_b",
             "b_ln_g", "b_ln_b", "w_pb", "w_out", "norm2_g", "w_ff1", "w_ff2", "final_g"]
    return (loss, grad_x, *[out[("grad", k)] for k in order], *[out[("delta", k)] for k in order],
            *[out[("new_m", k)] for k in order], *[out[("new_v", k)] for k in order])
```

```python
import jax
import jax.numpy as jnp
from jax import lax
from jax.experimental import pallas as pl
from jax.experimental.pallas import tpu as pltpu

F32 = jnp.float32
MXU_DTYPE = jnp.bfloat16
ACT_DTYPE = jnp.bfloat16
WIRE_DTYPE = jnp.bfloat16

EPS = 1e-6
CHUNK = 128
HEADS = 8
CONV_TAPS = 31
HALO = 32
N_DEV = 8
N_CHIP = 4
ADAM_LR, ADAM_B1, ADAM_B2, ADAM_EPS, ADAM_WD, ADAM_STEP = 0.001, 0.9, 0.999, 1e-08, 0.01, 10

V7X_VMEM_BYTES = 64 * 1024 * 1024
VMEM_LIMIT = V7X_VMEM_BYTES * 7 // 8
TOKEN_TILE = 512
MATMUL_TILE = 2048
FFN_BWD_TILE = 512
CONV_ROWS = 64
TAP_GRAD_ROWS = 32
LANES = 128
MESH_ID = pl.DeviceIdType.MESH


def _params(sem=None):
    return pltpu.CompilerParams(dimension_semantics=sem, vmem_limit_bytes=VMEM_LIMIT)


def _resident(shape):
    return pl.BlockSpec(shape, lambda *_: (0,) * len(shape), pipeline_mode=pl.Buffered(1))


def _dot(a, b):
    return jnp.dot(a.astype(MXU_DTYPE), b.astype(MXU_DTYPE), preferred_element_type=F32)


def _dot_nt(a, b):
    return lax.dot_general(a.astype(MXU_DTYPE), b.astype(MXU_DTYPE), (((1,), (1,)), ((), ())),
                           preferred_element_type=F32)


def _dot_tn(a, b):
    return lax.dot_general(a.astype(MXU_DTYPE), b.astype(MXU_DTYPE), (((0,), (0,)), ((), ())),
                           preferred_element_type=F32)


def _colsum(a):
    return jnp.sum(a, axis=0, keepdims=True)


def _rowmean(a):
    return jnp.mean(a, axis=-1, keepdims=True)


def _sigmoid(a):
    return 1.0 / (1.0 + jnp.exp(-a))


def _modnorm_fwd(x, g, sc, sh):
    r = lax.rsqrt(_rowmean(x * x) + EPS)
    return (x * r) * (g * (1.0 + sc)) + sh


def _modnorm_bwd(x, dh, g, sc):
    r = lax.rsqrt(_rowmean(x * x) + EPS)
    xn = x * r
    dxn = dh * (g * (1.0 + sc))
    dx = r * (dxn - xn * _rowmean(dxn * xn))
    return dx, _colsum(dh), _colsum(dh * xn)


def _ln_stats(v):
    mu = _rowmean(v)
    vc = v - mu
    rstd = lax.rsqrt(_rowmean(vc * vc) + EPS)
    return vc * rstd, rstd


def _ln_bwd(dy, vhat, rstd, g):
    dvh = dy * g
    return rstd * (dvh - _rowmean(dvh) - vhat * _rowmean(dvh * vhat))


def _causal_mask():
    row = lax.broadcasted_iota(jnp.int32, (CHUNK, CHUNK), 0)
    col = lax.broadcasted_iota(jnp.int32, (CHUNK, CHUNK), 1)
    return row >= col


def _my_place():
    return lax.axis_index("x"), lax.axis_index("y"), lax.axis_index("c")


def _other_chips(mx, my):
    return [(1 - mx, my), (mx, 1 - my), (1 - mx, 1 - my)]


def _other_devices(mx, my, mc):
    return [((mx + ((k >> 2) & 1)) % 2, (my + ((k >> 1) & 1)) % 2, (mc + (k & 1)) % 2) for k in range(1, N_DEV)]


def _all_to_all(x, name):
    assert x.shape[0] == N_DEV

    def body(x_ref, o_ref, send_sems, recv_sems):
        mx, my, mc = _my_place()
        me = 4 * mx + 2 * my + mc
        o_ref[me] = x_ref[me]
        copies = []
        for k, (px, py, pc) in enumerate(_other_devices(mx, my, mc)):
            cp = pltpu.make_async_remote_copy(
                src_ref=x_ref.at[4 * px + 2 * py + pc], dst_ref=o_ref.at[me],
                send_sem=send_sems.at[k], recv_sem=recv_sems.at[k],
                device_id=(px, py, pc), device_id_type=MESH_ID)
            cp.start()
            copies.append(cp)
        for cp in copies:
            cp.wait()

    return pl.pallas_call(
        body, name=name, out_shape=jax.ShapeDtypeStruct(x.shape, x.dtype),
        in_specs=[pl.BlockSpec(memory_space=pltpu.VMEM)],
        out_specs=pl.BlockSpec(memory_space=pltpu.VMEM),
        scratch_shapes=[pltpu.SemaphoreType.DMA((N_DEV - 1,)), pltpu.SemaphoreType.DMA((N_DEV - 1,))],
        compiler_params=pltpu.CompilerParams(vmem_limit_bytes=VMEM_LIMIT),
    )(x)


def _sum_slots(x, name):
    def body(x_ref, o_ref):
        acc = x_ref[0]
        for s in range(1, N_DEV):
            acc = acc + x_ref[s]
        o_ref[...] = acc

    return pl.pallas_call(
        body, name=name, out_shape=jax.ShapeDtypeStruct(x.shape[1:], x.dtype),
        in_specs=[pl.BlockSpec(memory_space=pltpu.VMEM)], out_specs=pl.BlockSpec(memory_space=pltpu.VMEM),
        compiler_params=pltpu.CompilerParams(vmem_limit_bytes=VMEM_LIMIT),
    )(x)


HBM_SPEC = pl.BlockSpec(memory_space=pltpu.HBM)
SEM_SPEC = pl.BlockSpec(memory_space=pltpu.SEMAPHORE)
ANY_SPEC = pl.BlockSpec(memory_space=pl.ANY)
SPLIT_EFFECT = pltpu.SideEffectType.DATAFLOW_SIDE_EFFECTING


def _quarter_copies(mode, srcs, lands, send_sems, recv_sems, peers=(0, 1, 2)):
    mx, my, mc = _my_place()
    myq = 2 * mx + my
    if mode == "swap":
        return [pltpu.make_async_remote_copy(
            src_ref=srcs[a], dst_ref=lands[a], send_sem=send_sems[a].at[0], recv_sem=recv_sems[a].at[0],
            device_id=(mx, my, 1 - mc), device_id_type=MESH_ID) for a in range(len(lands))]
    copies = []
    if mode == "exchange":
        me = 4 * mx + 2 * my + mc
        for a in range(len(lands)):
            for k, (px, py, pc) in enumerate(_other_devices(mx, my, mc)):
                copies.append(pltpu.make_async_remote_copy(
                    src_ref=srcs[a].at[4 * px + 2 * py + pc], dst_ref=lands[a].at[me],
                    send_sem=send_sems[a].at[k], recv_sem=recv_sems[a].at[k],
                    device_id=(px, py, pc), device_id_type=MESH_ID))
        return copies
    for a in range(len(lands)):
        for k, (px, py) in enumerate(_other_chips(mx, my)):
            if k not in peers:
                continue
            if mode == "gather":
                src, dst = lands[a].at[myq], lands[a].at[myq]
            else:
                src, dst = srcs[a].at[2 * px + py], lands[a].at[k]
            copies.append(pltpu.make_async_remote_copy(
                src_ref=src, dst_ref=dst, send_sem=send_sems[a].at[k], recv_sem=recv_sems[a].at[k],
                device_id=(px, py, mc), device_id_type=MESH_ID))
    return copies


def _split_start(name, mode, srcs, lands, after):
    ns, n = len(srcs), len(lands)

    def body(*refs):
        outs = refs[ns + n + 1:]
        for cp in _quarter_copies(mode, refs[:ns], refs[ns:ns + n], outs[:n], outs[n:2 * n]):
            cp.start()
        token = outs[-1]
        token[...] = jnp.zeros_like(token)

    arrays = list(srcs) + list(lands)
    per_array = {"swap": 1, "exchange": N_DEV - 1}.get(mode, 3)
    res = pl.pallas_call(
        body, name=name,
        out_shape=[pltpu.SemaphoreType.DMA((per_array,))] * (2 * n) + [pltpu.HBM(x.shape, x.dtype) for x in arrays]
        + [jax.ShapeDtypeStruct((8, LANES), F32)],
        in_specs=[HBM_SPEC] * (ns + n) + [ANY_SPEC],
        out_specs=[SEM_SPEC] * (2 * n) + [HBM_SPEC] * (ns + n) + [pl.BlockSpec(memory_space=pltpu.VMEM)],
        input_output_aliases={i: 2 * n + i for i in range(ns + n)},
        compiler_params=pltpu.CompilerParams(has_side_effects=SPLIT_EFFECT),
    )(*[pltpu.with_memory_space_constraint(x, pltpu.HBM) for x in arrays], after)
    return res[:n], res[n:2 * n], res[2 * n:2 * n + ns], res[2 * n + ns:2 * n + ns + n], res[-1]


def _split_wait(name, mode, send_sems, recv_sems, srcs, lands, after, peers=(0, 1, 2)):
    ns, n = len(srcs), len(lands)

    def body(*refs):
        sems = refs[ns + n:ns + 3 * n]
        for cp in _quarter_copies(mode, refs[:ns], refs[ns:ns + n], sems[:n], sems[n:], peers):
            cp.wait_send()
            cp.wait_recv()

    arrays = list(srcs) + list(lands)
    res = pl.pallas_call(
        body, name=name,
        out_shape=[pltpu.HBM(x.shape, x.dtype) for x in arrays],
        in_specs=[HBM_SPEC] * (ns + n) + [SEM_SPEC] * (2 * n) + [ANY_SPEC],
        out_specs=[HBM_SPEC] * (ns + n),
        input_output_aliases={i: i for i in range(ns + n)},
        compiler_params=pltpu.CompilerParams(has_side_effects=SPLIT_EFFECT),
    )(*arrays, *send_sems, *recv_sems, after)
    return res[:ns], res[ns:]


def _ada_forward(c_all, w_ada, b_ada3, myq):
    nl, d, cq = w_ada.shape
    nb = c_all.shape[0]

    def body(q_ref, c_ref, w_ref, b_ref, o_ref):
        c = c_ref[...]
        act = c * _sigmoid(c)
        o_ref[...] = _dot(act, w_ref[...]) + b_ref[...]

    return pl.pallas_call(
        body, name="ada_forward",
        out_shape=jax.ShapeDtypeStruct((nl, nb, cq), F32),
        grid_spec=pltpu.PrefetchScalarGridSpec(
            num_scalar_prefetch=1, grid=(nl,),
            in_specs=[pl.BlockSpec((nb, d), lambda l, q: (0, 0)),
                      pl.BlockSpec((None, d, cq), lambda l, q: (l, 0, 0)),
                      pl.BlockSpec((None, 1, cq), lambda l, q: (l, 0, q[0]))],
            out_specs=pl.BlockSpec((None, nb, cq), lambda l, q: (l, 0, 0))),
        compiler_params=_params(("arbitrary",)),
    )(myq, c_all, w_ada, b_ada3)


def _ada_backward(c_all, dmod_all, myq, cq, after):
    nb, d = c_all.shape
    nl = dmod_all.shape[0]
    full = dmod_all.shape[2]

    def body(q_ref, c_ref, dq_ref, dall_ref, after_ref, gw_ref, gb_ref):
        c = c_ref[...]
        act = c * _sigmoid(c)
        gw_ref[...] = _dot_tn(act, dq_ref[...])
        gb_ref[...] = _colsum(dall_ref[...])

    return pl.pallas_call(
        body, name="ada_backward",
        out_shape=[jax.ShapeDtypeStruct((nl, d, cq), F32), jax.ShapeDtypeStruct((nl, 1, full), F32)],
        grid_spec=pltpu.PrefetchScalarGridSpec(
            num_scalar_prefetch=1, grid=(nl,),
            in_specs=[pl.BlockSpec((nb, d), lambda l, q: (0, 0)),
                      pl.BlockSpec((None, nb, cq), lambda l, q: (l, 0, q[0])),
                      pl.BlockSpec((None, nb, full), lambda l, q: (l, 0, 0)), ANY_SPEC],
            out_specs=[pl.BlockSpec((None, d, cq), lambda l, q: (l, 0, 0)),
                       pl.BlockSpec((None, 1, full), lambda l, q: (l, 0, 0))]),
        compiler_params=_params(("arbitrary",)),
    )(myq, c_all, dmod_all, dmod_all, after)


def _in_proj(l, x, mod, g1, wg_in, t_len):
    n, d = x.shape
    tm = min(TOKEN_TILE, t_len)
    tpb = t_len // tm
    qc = wg_in.shape[-1]

    def body(x_ref, mod_ref, g_ref, w_ref, h_ref, proj_ref):
        h = _modnorm_fwd(x_ref[...], g_ref[...], mod_ref[1:2, :], mod_ref[0:1, :]).astype(MXU_DTYPE)
        h_ref[...] = h.astype(ACT_DTYPE)
        for q in range(N_CHIP):
            proj_ref[:, q * qc:(q + 1) * qc] = jnp.dot(h, w_ref[q], preferred_element_type=F32).astype(ACT_DTYPE)

    return pl.pallas_call(
        body, name=f"in_proj_{l}",
        out_shape=[jax.ShapeDtypeStruct((n, d), ACT_DTYPE), jax.ShapeDtypeStruct((n, N_CHIP * qc), ACT_DTYPE)],
        grid=(n // tm,),
        in_specs=[pl.BlockSpec((tm, d), lambda i: (i, 0)),
                  pl.BlockSpec((None, None, 8, d), lambda i: (l, i // tpb, 0, 0)),
                  pl.BlockSpec((None, 1, d), lambda i: (l, 0, 0)),
                  _resident((N_CHIP, d, qc))],
        out_specs=[pl.BlockSpec((tm, d), lambda i: (i, 0)),
                   pl.BlockSpec((tm, N_CHIP * qc), lambda i: (i, 0))],
        compiler_params=_params(("arbitrary",)),
    )(x, mod, g1, wg_in)


def _in_proj_quarter(name, l, src, mod, g1, wg_in, quarter, t_len, prev=None):
    n, d = src.shape
    tm = min(TOKEN_TILE, t_len)
    tpb = t_len // tm
    qc = wg_in.shape[-1]
    first = prev is None

    def body(q_ref, *refs):
        if first:
            x_ref, mod_ref, g_ref, w_ref, h_ref, proj_ref = refs
            h = _modnorm_fwd(x_ref[...], g_ref[...], mod_ref[1:2, :], mod_ref[0:1, :]).astype(MXU_DTYPE)
            h_ref[...] = h.astype(ACT_DTYPE)
        else:
            h_in_ref, w_ref, _, proj_ref = refs
            h = h_in_ref[...].astype(MXU_DTYPE)
        proj_ref[...] = jnp.dot(h, w_ref[...], preferred_element_type=F32).astype(ACT_DTYPE)

    tile = pl.BlockSpec((tm, d), lambda i, q: (i, 0))
    w_spec = pl.BlockSpec((None, d, qc), lambda i, q: (q[0], 0, 0))
    proj_spec = pl.BlockSpec((tm, qc), lambda i, q: (i, q[0]))
    proj_shape = jax.ShapeDtypeStruct((n, N_CHIP * qc), ACT_DTYPE)
    if first:
        operands = [quarter, src, mod, g1, wg_in]
        in_specs = [tile, pl.BlockSpec((None, None, 8, d), lambda i, q: (l, i // tpb, 0, 0)),
                    pl.BlockSpec((None, 1, d), lambda i, q: (l, 0, 0)), w_spec]
        out_shape, out_specs, aliases = [jax.ShapeDtypeStruct((n, d), ACT_DTYPE), proj_shape], [tile, proj_spec], {}
    else:
        operands = [quarter, src, wg_in, prev]
        in_specs = [tile, w_spec, ANY_SPEC]
        out_shape, out_specs, aliases = proj_shape, proj_spec, {3: 0}
    return pl.pallas_call(
        body, name=name, out_shape=out_shape,
        grid_spec=pltpu.PrefetchScalarGridSpec(num_scalar_prefetch=1, grid=(n // tm,), in_specs=in_specs,
                                               out_specs=out_specs),
        input_output_aliases=aliases,
        compiler_params=_params(("arbitrary",)),
    )(*operands)


def _masked_ws(ws_ref, wm_s):
    mask = _causal_mask()
    for h in range(HEADS):
        wm_s[h] = jnp.where(mask, ws_ref[h], 0.0).astype(MXU_DTYPE)


def _fill_z(i, tpb, ah_ref, gh_ref, zext):
    ah = ah_ref[...].astype(F32)
    gh = gh_ref[...].astype(F32)
    keep = jnp.where(i % tpb == 0, 0.0, 1.0)
    _put_lanes(zext, slice(0, HALO), ah * _sigmoid(gh) * keep)


def _put_lanes(dst3, rows, value):
    for lc in range(value.shape[-1] // LANES):
        dst3[lc, rows, :] = value[:, lc * LANES:(lc + 1) * LANES]


def _tap_windows(src3, lc, base, rows, flip):
    offs = {k: (CONV_TAPS - 1 - k) if flip else (k + 2) for k in range(CONV_TAPS)}
    for r in range(8):
        taps = [k for k in offs if offs[k] % 8 == r]
        lo = min(offs[k] for k in taps)
        hi = max(offs[k] for k in taps)
        win = src3[lc, pl.ds(base + lo, hi - lo + rows), :]
        for k in taps:
            yield k, win[offs[k] - lo:offs[k] - lo + rows]


def _conv_taps(src3, w3_ref, dst3, lc, nrows, flip):
    for b in range(nrows // CONV_ROWS):
        acc = jnp.zeros((CONV_ROWS, LANES), F32)
        for k, win in _tap_windows(src3, lc, b * CONV_ROWS, CONV_ROWS, flip):
            acc = acc + win * w3_ref[lc, k:k + 1, :]
        dst3[lc, b * CONV_ROWS:(b + 1) * CONV_ROWS, :] = acc


def _branches_fwd(l, proj, lng, lnb, ws, bst, cw, cb, blg, blb, t_len):
    n = proj.shape[0]
    d = lng.shape[-1]
    tm = min(TOKEN_TILE, t_len)
    tpb = t_len // tm
    per = tm // HALO
    nchunk = tm // CHUNK

    def body(u_ref, v_ref, a_ref, g_ref, ah_ref, gh_ref, lng_ref, lnb_ref, ws_ref, bst_ref, cw_ref, cb_ref,
             blg_ref, blb_ref, ya_ref, yb_ref, zc_ref, wm_s, zext, zc3):
        i = pl.program_id(0)
        _masked_ws(ws_ref, wm_s)
        _fill_z(i, tpb, ah_ref, gh_ref, zext)

        def chunk(c, carry):
            r0 = pl.multiple_of(c * CHUNK, CHUNK)
            rows = pl.ds(r0, CHUNK)
            vhat, _ = _ln_stats(v_ref[rows, :].astype(F32))
            vn = (vhat * lng_ref[...] + lnb_ref[...]).astype(MXU_DTYPE)
            u = u_ref[rows, :].astype(F32)
            for h in range(HEADS):
                cols = slice(h * CHUNK, (h + 1) * CHUNK)
                s = jnp.dot(wm_s[h], vn[:, cols], preferred_element_type=F32) + bst_ref[:, h:h + 1]
                ya_ref[rows, cols] = (u[:, cols] * s).astype(ACT_DTYPE)
            a = a_ref[rows, :].astype(F32)
            g = g_ref[rows, :].astype(F32)
            _put_lanes(zext, pl.ds(HALO + r0, CHUNK), a * _sigmoid(g))
            return carry

        lax.fori_loop(0, nchunk, chunk, 0)

        def lane_chunk(lc, carry):
            _conv_taps(zext, cw_ref, zc3, lc, tm, flip=False)
            return carry

        lax.fori_loop(0, d // LANES, lane_chunk, 0)

        def chunk2(c, carry):
            r0 = pl.multiple_of(c * CHUNK, CHUNK)
            rows = pl.ds(r0, CHUNK)
            for lc in range(d // LANES):
                lanes = slice(lc * LANES, (lc + 1) * LANES)
                zc_ref[rows, lanes] = (zc3[lc, rows, :] + cb_ref[:, lanes]).astype(ACT_DTYPE)
            zhat, _ = _ln_stats(zc_ref[rows, :].astype(F32))
            zn = zhat * blg_ref[...] + blb_ref[...]
            yb_ref[rows, :] = (zn * _sigmoid(zn)).astype(ACT_DTYPE)
            return carry

        lax.fori_loop(0, nchunk, chunk2, 0)

    col = lambda k: pl.BlockSpec((tm, d), lambda i: (i, k))
    halo = lambda k: pl.BlockSpec((HALO, d), lambda i: (jnp.maximum(i * per - 1, 0), k))
    vec = pl.BlockSpec((None, 1, d), lambda i: (l, 0, 0))
    out = pl.BlockSpec((tm, d), lambda i: (i, 0))
    return pl.pallas_call(
        body, name=f"branches_fwd_{l}",
        out_shape=[jax.ShapeDtypeStruct((n, d), ACT_DTYPE)] * 3,
        grid=(n // tm,),
        in_specs=[col(0), col(1), col(2), col(3), halo(2), halo(3), vec, vec,
                  pl.BlockSpec((None, HEADS, CHUNK, CHUNK), lambda i: (l, 0, 0, 0)),
                  pl.BlockSpec((None, CHUNK, HEADS), lambda i: (l, 0, 0)),
                  pl.BlockSpec((None, d // LANES, HALO, LANES), lambda i: (l, 0, 0, 0)), vec, vec, vec],
        out_specs=[out, out, out],
        scratch_shapes=[pltpu.VMEM((HEADS, CHUNK, CHUNK), MXU_DTYPE), pltpu.VMEM((d // LANES, HALO + tm, LANES), F32),
                        pltpu.VMEM((d // LANES, tm, LANES), F32)],
        compiler_params=_params(("arbitrary",)),
    )(proj, proj, proj, proj, proj, proj, lng, lnb, ws, bst, cw, cb, blg, blb)


def _merge_out(l, x, mod, proj, ya_in, yb_in, wg_pa, wg_pb, wg_out, t_len):
    n, d = x.shape
    tm = min(TOKEN_TILE, t_len)
    tpb = t_len // tm
    rq = d // N_CHIP

    def body(x_ref, mod_ref, ga_ref, gb_ref, yai_ref, ybi_ref, wpa_ref, wpb_ref, wo_ref,
             ya_ref, yb_ref, mg_ref, o_ref, x1_ref):
        wpa = wpa_ref[...].reshape(d, d)
        wpb = wpb_ref[...].reshape(d, d)
        wo = wo_ref[...].reshape(d, d)
        ya = jnp.dot(yai_ref[...].astype(MXU_DTYPE), wpa, preferred_element_type=F32)
        yb = jnp.dot(ybi_ref[...].astype(MXU_DTYPE), wpb, preferred_element_type=F32)
        merged = _sigmoid(ga_ref[...].astype(F32)) * ya + _sigmoid(gb_ref[...].astype(F32)) * yb
        o = _dot(merged, wo)
        ya_ref[...] = ya.astype(ACT_DTYPE)
        yb_ref[...] = yb.astype(ACT_DTYPE)
        mg_ref[...] = merged.astype(ACT_DTYPE)
        o_ref[...] = o.astype(ACT_DTYPE)
        x1_ref[...] = x_ref[...] + mod_ref[2:3, :] * o

    tile = pl.BlockSpec((tm, d), lambda i: (i, 0))
    wspec = pl.BlockSpec((N_CHIP, rq, d), lambda i: (0, 0, 0))
    return pl.pallas_call(
        body, name=f"merge_out_{l}",
        out_shape=[jax.ShapeDtypeStruct((n, d), ACT_DTYPE)] * 4 + [jax.ShapeDtypeStruct((n, d), F32)],
        grid=(n // tm,),
        in_specs=[tile, pl.BlockSpec((None, None, 8, d), lambda i: (l, i // tpb, 0, 0)),
                  pl.BlockSpec((tm, d), lambda i: (i, 4)), pl.BlockSpec((tm, d), lambda i: (i, 5)),
                  tile, tile, wspec, wspec, wspec],
        out_specs=[tile] * 5,
        compiler_params=_params(("arbitrary",)),
    )(x, mod, proj, proj, ya_in, yb_in, wg_pa, wg_pb, wg_out)


def _ffn_fwd(l, x1, mod, g2, wg_ff1, wg_ff2, t_len):
    n, d = x1.shape
    tm = min(TOKEN_TILE, t_len)
    tpb = t_len // tm
    hq = wg_ff1.shape[-1]
    hid = N_CHIP * hq

    def body(x_ref, mod_ref, g_ref, w1_ref, w2_ref, h_ref, f_ref, o2_ref, x2_ref, a2_s):
        h = _modnorm_fwd(x_ref[...], g_ref[...], mod_ref[4:5, :], mod_ref[3:4, :]).astype(MXU_DTYPE)
        h_ref[...] = h.astype(ACT_DTYPE)
        for q in range(N_CHIP):
            cols = slice(q * hq, (q + 1) * hq)
            f = jnp.dot(h, w1_ref[q], preferred_element_type=F32)
            f_ref[:, cols] = f.astype(ACT_DTYPE)
            a2_s[:, cols] = jnp.square(jnp.maximum(f, 0.0)).astype(MXU_DTYPE)
        o2 = jnp.dot(a2_s[...], w2_ref[...].reshape(hid, d), preferred_element_type=F32)
        o2_ref[...] = o2.astype(ACT_DTYPE)
        x2_ref[...] = x_ref[...] + mod_ref[5:6, :] * o2

    tile = pl.BlockSpec((tm, d), lambda i: (i, 0))
    return pl.pallas_call(
        body, name=f"ffn_fwd_{l}",
        out_shape=[jax.ShapeDtypeStruct((n, d), ACT_DTYPE), jax.ShapeDtypeStruct((n, hid), ACT_DTYPE),
                   jax.ShapeDtypeStruct((n, d), ACT_DTYPE), jax.ShapeDtypeStruct((n, d), F32)],
        grid=(n // tm,),
        in_specs=[tile, pl.BlockSpec((None, None, 8, d), lambda i: (l, i // tpb, 0, 0)),
                  pl.BlockSpec((None, 1, d), lambda i: (l, 0, 0)),
                  _resident((N_CHIP, d, hq)), _resident((N_CHIP, hq, d))],
        out_specs=[tile, pl.BlockSpec((tm, hid), lambda i: (i, 0)), tile, tile],
        scratch_shapes=[pltpu.VMEM((tm, hid), MXU_DTYPE)],
        compiler_params=_params(("arbitrary",)),
    )(x1, mod, g2, wg_ff1, wg_ff2)


def _loss_head(x, final_g, target):
    n, d = x.shape
    tm = min(TOKEN_TILE, n)

    def body(x_ref, g_ref, t_ref, loss_ref, dx_ref, dg_ref):
        @pl.when(pl.program_id(0) == 0)
        def _():
            loss_ref[...] = jnp.zeros_like(loss_ref)
            dg_ref[...] = jnp.zeros_like(dg_ref)

        x_t = x_ref[...]
        g = g_ref[...]
        r = lax.rsqrt(_rowmean(x_t * x_t) + EPS)
        xn = x_t * r
        e = xn * g - t_ref[...]
        loss_ref[...] += jnp.sum(e * e) * (0.5 / d)
        dy = e * (1.0 / d)
        dxn = dy * g
        dx_ref[...] = r * (dxn - xn * _rowmean(dxn * xn))
        dg_ref[0:1, :] += _colsum(dy * xn)

    tile = pl.BlockSpec((tm, d), lambda i: (i, 0))
    return pl.pallas_call(
        body, name="loss_head",
        out_shape=[jax.ShapeDtypeStruct((8, LANES), F32), jax.ShapeDtypeStruct((n, d), F32),
                   jax.ShapeDtypeStruct((8, d), F32)],
        grid=(n // tm,),
        in_specs=[tile, pl.BlockSpec((1, d), lambda i: (0, 0)), tile],
        out_specs=[pl.BlockSpec((8, LANES), lambda i: (0, 0)), tile, pl.BlockSpec((8, d), lambda i: (0, 0))],
        compiler_params=_params(("arbitrary",)),
    )(x, final_g, target)


def _norm_tail(x_ref, dxin_ref, dh, g_ref, sc, dx_ref, dmod_ref, dg_ref, row_sh, row_sc):
    dxm, dsh, q = _modnorm_bwd(x_ref[...], dh, g_ref[...], sc)
    dx_ref[...] = dxin_ref[...] + dxm
    dmod_ref[row_sh:row_sh + 1, :] += dsh
    dmod_ref[row_sc:row_sc + 1, :] += g_ref[...] * q
    dg_ref[0:1, :] += (1.0 + sc) * q


def _ffn_bwd(l, dx2, x1, mod, g2, o2, f, wg_ff1, wg_ff2, t_len, nb):
    n, d = dx2.shape
    tm = min(FFN_BWD_TILE, t_len)
    tpb = t_len // tm
    hq = wg_ff1.shape[-1]
    hid = N_CHIP * hq

    def body(dx2_ref, x1_ref, mod_ref, g_ref, o2_ref, f_ref, w1_ref, w2_ref,
             do2_ref, df_ref, dx1_ref, dmod_ref, dg_ref):
        i = pl.program_id(0)

        @pl.when(i == 0)
        def _():
            dg_ref[...] = jnp.zeros_like(dg_ref)

        @pl.when(i % tpb == 0)
        def _():
            dmod_ref[...] = jnp.zeros_like(dmod_ref)

        dx2_t = dx2_ref[...]
        dmod_ref[5:6, :] += _colsum(dx2_t * o2_ref[...].astype(F32))
        do2 = (dx2_t * mod_ref[5:6, :]).astype(MXU_DTYPE)
        do2_ref[...] = do2.astype(ACT_DTYPE)
        dh = jnp.zeros((tm, d), F32)
        for q in range(N_CHIP):
            cols = slice(q * hq, (q + 1) * hq)
            da2 = _dot_nt(do2, w2_ref[q])
            df = (da2 * (2.0 * jnp.maximum(f_ref[:, cols].astype(F32), 0.0))).astype(MXU_DTYPE)
            df_ref[:, cols] = df.astype(ACT_DTYPE)
            dh = dh + _dot_nt(df, w1_ref[q])
        _norm_tail(x1_ref, dx2_ref, dh, g_ref, mod_ref[4:5, :], dx1_ref, dmod_ref, dg_ref, 3, 4)

    tile = pl.BlockSpec((tm, d), lambda i: (i, 0))
    wide = pl.BlockSpec((tm, hid), lambda i: (i, 0))
    return pl.pallas_call(
        body, name=f"ffn_bwd_{l}",
        out_shape=[jax.ShapeDtypeStruct((n, d), ACT_DTYPE), jax.ShapeDtypeStruct((n, hid), ACT_DTYPE),
                   jax.ShapeDtypeStruct((n, d), F32), jax.ShapeDtypeStruct((nb, 8, d), F32),
                   jax.ShapeDtypeStruct((8, d), F32)],
        grid=(n // tm,),
        in_specs=[tile, tile, pl.BlockSpec((None, None, 8, d), lambda i: (l, i // tpb, 0, 0)),
                  pl.BlockSpec((None, 1, d), lambda i: (l, 0, 0)), tile, wide,
                  _resident((N_CHIP, d, hq)), _resident((N_CHIP, hq, d))],
        out_specs=[tile, wide, tile, pl.BlockSpec((None, 8, d), lambda i: (i // tpb, 0, 0)),
                   pl.BlockSpec((8, d), lambda i: (0, 0))],
        compiler_params=_params(("arbitrary",)),
    )(dx2, x1, mod, g2, o2, f, wg_ff1, wg_ff2)


def _merge_bwd(l, dx1, mod, o, ya, yb, proj, wg_pa, wg_pb, wg_out, t_len, nb, after):
    n, d = dx1.shape
    tm = min(TOKEN_TILE, t_len)
    tpb = t_len // tm
    rq = d // N_CHIP

    def body(dx_ref, mod_ref, o_ref, ya_ref, yb_ref, ga_ref, gb_ref, wpa_ref, wpb_ref, wo_ref, after_ref,
             do_ref, dya_ref, dyb_ref, dyai_ref, dybi_ref, dproj_ref, dmod_ref):
        i = pl.program_id(0)

        @pl.when(i % tpb == 0)
        def _():
            dmod_ref[...] = jnp.zeros_like(dmod_ref)

        dx = dx_ref[...]
        dmod_ref[2:3, :] += _colsum(dx * o_ref[...].astype(F32))
        do = (dx * mod_ref[2:3, :]).astype(MXU_DTYPE)
        do_ref[...] = do.astype(ACT_DTYPE)
        dm = _dot_nt(do, wo_ref[...].reshape(d, d))
        sa = _sigmoid(ga_ref[...].astype(F32))
        sb = _sigmoid(gb_ref[...].astype(F32))
        dya = (dm * sa).astype(MXU_DTYPE)
        dyb = (dm * sb).astype(MXU_DTYPE)
        dya_ref[...] = dya.astype(ACT_DTYPE)
        dyb_ref[...] = dyb.astype(ACT_DTYPE)
        dproj_ref[:, 0:d] = (dm * ya_ref[...].astype(F32) * sa * (1.0 - sa)).astype(ACT_DTYPE)
        dproj_ref[:, d:2 * d] = (dm * yb_ref[...].astype(F32) * sb * (1.0 - sb)).astype(ACT_DTYPE)
        dyai_ref[...] = _dot_nt(dya, wpa_ref[...].reshape(d, d)).astype(ACT_DTYPE)
        dybi_ref[...] = _dot_nt(dyb, wpb_ref[...].reshape(d, d)).astype(ACT_DTYPE)

    tile = pl.BlockSpec((tm, d), lambda i: (i, 0))
    wspec = pl.BlockSpec((N_CHIP, rq, d), lambda i: (0, 0, 0))
    return pl.pallas_call(
        body, name=f"merge_bwd_{l}",
        out_shape=[jax.ShapeDtypeStruct((n, d), ACT_DTYPE)] * 5
        + [jax.ShapeDtypeStruct((n, 6 * d), ACT_DTYPE), jax.ShapeDtypeStruct((nb, 8, d), F32)],
        grid=(n // tm,),
        in_specs=[tile, pl.BlockSpec((None, None, 8, d), lambda i: (l, i // tpb, 0, 0)), tile, tile, tile,
                  pl.BlockSpec((tm, d), lambda i: (i, 4)), pl.BlockSpec((tm, d), lambda i: (i, 5)),
                  wspec, wspec, wspec, ANY_SPEC],
        out_specs=[tile] * 5 + [pl.BlockSpec((tm, 2 * d), lambda i: (i, 2)),
                                pl.BlockSpec((None, 8, d), lambda i: (i // tpb, 0, 0))],
        compiler_params=_params(("arbitrary",)),
    )(dx1, mod, o, ya, yb, proj, proj, wg_pa, wg_pb, wg_out, after)


def _branches_bwd(l, proj, zc, dya_in, dyb_in, dproj, lng, lnb, ws, bst, cw, blg, blb, t_len, after):
    n = proj.shape[0]
    d = lng.shape[-1]
    tm = min(TOKEN_TILE, t_len)
    tpb = t_len // tm
    per = tm // HALO
    nchunk = tm // CHUNK
    ntile = n // tm

    def body(u_ref, v_ref, a_ref, g_ref, ah_ref, gh_ref, zc_ref, zcn_ref, dya_ref, dyb_ref, dybn_ref, dproj_in,
             lng_ref, lnb_ref, ws_ref, bst_ref, cw_ref, blg_ref, blb_ref, after_ref,
             dproj_ref, dws_ref, dbst_ref, dcw_ref, vec_ref, wm_s, zext, dzext, dz3, dvn_s):
        i = pl.program_id(0)

        @pl.when(i == 0)
        def _():
            dws_ref[...] = jnp.zeros_like(dws_ref)
            dbst_ref[...] = jnp.zeros_like(dbst_ref)
            dcw_ref[...] = jnp.zeros_like(dcw_ref)
            vec_ref[...] = jnp.zeros_like(vec_ref)

        _masked_ws(ws_ref, wm_s)
        _fill_z(i, tpb, ah_ref, gh_ref, zext)

        def conv_ln_bwd(zc_t, dyb_t):
            zhat, rstd = _ln_stats(zc_t)
            zn = zhat * blg_ref[...] + blb_ref[...]
            sg = _sigmoid(zn)
            dzn = dyb_t * (sg * (1.0 + zn * (1.0 - sg)))
            return _ln_bwd(dzn, zhat, rstd, blg_ref[...]), _colsum(dzn * zhat), _colsum(dzn)

        def chunk(c, carry):
            r0 = pl.multiple_of(c * CHUNK, CHUNK)
            rows = pl.ds(r0, CHUNK)
            vhat, rstd = _ln_stats(v_ref[rows, :].astype(F32))
            vn = (vhat * lng_ref[...] + lnb_ref[...]).astype(MXU_DTYPE)
            u = u_ref[rows, :].astype(F32)
            dya = dya_ref[rows, :].astype(F32)
            for h in range(HEADS):
                cols = slice(h * CHUNK, (h + 1) * CHUNK)
                s = jnp.dot(wm_s[h], vn[:, cols], preferred_element_type=F32) + bst_ref[:, h:h + 1]
                dproj_ref[rows, cols] = (dya[:, cols] * s).astype(ACT_DTYPE)
                ds = dya[:, cols] * u[:, cols]
                dvn_s[:, cols] = _dot_tn(wm_s[h], ds)
                dws_ref[h] += _dot_nt(ds, vn[:, cols])
                dbst_ref[:, h:h + 1] += jnp.sum(ds, axis=1, keepdims=True)
            dvn = dvn_s[...]
            dproj_ref[rows, d:2 * d] = _ln_bwd(dvn, vhat, rstd, lng_ref[...]).astype(ACT_DTYPE)
            vec_ref[0:1, :] += _colsum(dvn * vhat)
            vec_ref[1:2, :] += _colsum(dvn)
            a = a_ref[rows, :].astype(F32)
            g = g_ref[rows, :].astype(F32)
            _put_lanes(zext, pl.ds(HALO + r0, CHUNK), a * _sigmoid(g))
            dzc, dblg, dblb = conv_ln_bwd(zc_ref[rows, :].astype(F32), dyb_ref[rows, :].astype(F32))
            _put_lanes(dzext, rows, dzc)
            vec_ref[2:3, :] += _colsum(dzc)
            vec_ref[3:4, :] += dblg
            vec_ref[4:5, :] += dblb
            return carry

        lax.fori_loop(0, nchunk, chunk, 0)

        dzc_next, _, _ = conv_ln_bwd(zcn_ref[...].astype(F32), dybn_ref[...].astype(F32))
        _put_lanes(dzext, slice(tm, tm + HALO), dzc_next * jnp.where(i % tpb == tpb - 1, 0.0, 1.0))

        def lane_chunk_dz(lc, carry):
            _conv_taps(dzext, cw_ref, dz3, lc, tm, flip=True)
            return carry

        lax.fori_loop(0, d // LANES, lane_chunk_dz, 0)

        def lane_chunk(lc, carry):
            accs = [jnp.zeros((8, LANES), F32) for _ in range(CONV_TAPS)]
            for b in range(tm // TAP_GRAD_ROWS):
                dzc = dzext[lc, b * TAP_GRAD_ROWS:(b + 1) * TAP_GRAD_ROWS, :]
                for k, win in _tap_windows(zext, lc, b * TAP_GRAD_ROWS, TAP_GRAD_ROWS, flip=False):
                    prod = dzc * win
                    part = prod[0:8]
                    for e in range(1, TAP_GRAD_ROWS // 8):
                        part = part + prod[8 * e:8 * e + 8]
                    accs[k] = accs[k] + part
            for k in range(CONV_TAPS):
                dcw_ref[lc, k:k + 1, :] += _colsum(accs[k])
            return carry

        lax.fori_loop(0, d // LANES, lane_chunk, 0)

        def glu_bwd(c, carry):
            r0 = pl.multiple_of(c * CHUNK, CHUNK)
            rows = pl.ds(r0, CHUNK)
            for lc in range(d // LANES):
                lanes = slice(lc * LANES, (lc + 1) * LANES)
                dz = dz3[lc, rows, :]
                a = a_ref[rows, lanes].astype(F32)
                sg = _sigmoid(g_ref[rows, lanes].astype(F32))
                dproj_ref[rows, 2 * d + lc * LANES:2 * d + (lc + 1) * LANES] = (dz * sg).astype(ACT_DTYPE)
                dproj_ref[rows, 3 * d + lc * LANES:3 * d + (lc + 1) * LANES] = (
                    dz * a * sg * (1.0 - sg)).astype(ACT_DTYPE)
            return carry

        lax.fori_loop(0, nchunk, glu_bwd, 0)

        @pl.when(i == ntile - 1)
        def _():
            mask = _causal_mask()
            for h in range(HEADS):
                dws_ref[h] = jnp.where(mask, dws_ref[h], 0.0)

    col = lambda k: pl.BlockSpec((tm, d), lambda i: (i, k))
    tile = pl.BlockSpec((tm, d), lambda i: (i, 0))
    before = lambda k: pl.BlockSpec((HALO, d), lambda i: (jnp.maximum(i * per - 1, 0), k))
    following = pl.BlockSpec((HALO, d), lambda i: (jnp.minimum((i + 1) * per, n // HALO - 1), 0))
    vec = pl.BlockSpec((None, 1, d), lambda i: (l, 0, 0))
    const2 = lambda r, c: pl.BlockSpec((r, c), lambda i: (0, 0))
    return pl.pallas_call(
        body, name=f"branches_bwd_{l}",
        out_shape=[jax.ShapeDtypeStruct((n, 6 * d), ACT_DTYPE), jax.ShapeDtypeStruct((HEADS, CHUNK, CHUNK), F32),
                   jax.ShapeDtypeStruct((CHUNK, HEADS), F32), jax.ShapeDtypeStruct((d // LANES, HALO, LANES), F32),
                   jax.ShapeDtypeStruct((8, d), F32)],
        grid=(ntile,),
        in_specs=[col(0), col(1), col(2), col(3), before(2), before(3), tile, following, tile, tile, following,
                  pl.BlockSpec(memory_space=pl.ANY), vec, vec,
                  pl.BlockSpec((None, HEADS, CHUNK, CHUNK), lambda i: (l, 0, 0, 0)),
                  pl.BlockSpec((None, CHUNK, HEADS), lambda i: (l, 0, 0)),
                  pl.BlockSpec((None, d // LANES, HALO, LANES), lambda i: (l, 0, 0, 0)), vec, vec, ANY_SPEC],
        out_specs=[pl.BlockSpec((tm, 4 * d), lambda i: (i, 0)),
                   pl.BlockSpec((HEADS, CHUNK, CHUNK), lambda i: (0, 0, 0)),
                   const2(CHUNK, HEADS), pl.BlockSpec((d // LANES, HALO, LANES), lambda i: (0, 0, 0)), const2(8, d)],
        scratch_shapes=[pltpu.VMEM((HEADS, CHUNK, CHUNK), MXU_DTYPE),
                        pltpu.VMEM((d // LANES, HALO + tm, LANES), F32),
                        pltpu.VMEM((d // LANES, tm + HALO, LANES), F32),
                        pltpu.VMEM((d // LANES, tm, LANES), F32), pltpu.VMEM((CHUNK, d), F32)],
        input_output_aliases={11: 0},
        compiler_params=_params(("arbitrary",)),
    )(proj, proj, proj, proj, proj, proj, zc, zc, dya_in, dyb_in, dyb_in, dproj, lng, lnb, ws, bst, cw, blg, blb, after)


def _in_proj_bwd(l, dproj, dx1, x, mod, g1, wg_in, t_len, nb, after):
    n, d = x.shape
    tm = min(TOKEN_TILE, t_len)
    tpb = t_len // tm
    qc = wg_in.shape[-1]

    def body(dp_ref, dx1_ref, x_ref, mod_ref, g_ref, w_ref, after_ref, dx_ref, dmod_ref, dg_ref):
        i = pl.program_id(0)

        @pl.when(i == 0)
        def _():
            dg_ref[...] = jnp.zeros_like(dg_ref)

        @pl.when(i % tpb == 0)
        def _():
            dmod_ref[...] = jnp.zeros_like(dmod_ref)

        dh = jnp.zeros((tm, d), F32)
        for q in range(N_CHIP):
            dh = dh + _dot_nt(dp_ref[:, q * qc:(q + 1) * qc], w_ref[q])
        _norm_tail(x_ref, dx1_ref, dh, g_ref, mod_ref[1:2, :], dx_ref, dmod_ref, dg_ref, 0, 1)

    tile = pl.BlockSpec((tm, d), lambda i: (i, 0))
    return pl.pallas_call(
        body, name=f"in_proj_bwd_{l}",
        out_shape=[jax.ShapeDtypeStruct((n, d), F32), jax.ShapeDtypeStruct((nb, 8, d), F32),
                   jax.ShapeDtypeStruct((8, d), F32)],
        grid=(n // tm,),
        in_specs=[pl.BlockSpec((tm, N_CHIP * qc), lambda i: (i, 0)), tile, tile,
                  pl.BlockSpec((None, None, 8, d), lambda i: (l, i // tpb, 0, 0)),
                  pl.BlockSpec((None, 1, d), lambda i: (l, 0, 0)),
                  _resident((N_CHIP, d, qc)), ANY_SPEC],
        out_specs=[tile, pl.BlockSpec((None, 8, d), lambda i: (i // tpb, 0, 0)),
                   pl.BlockSpec((8, d), lambda i: (0, 0))],
        compiler_params=_params(("arbitrary",)),
    )(dproj, dx1, x, mod, g1, wg_in, after)


def _weight_grad(name, a, b, a_spec, b_spec, out_rows, out_spec, acc_shape, grid_ij, relu2=False):
    n = a.shape[0]
    tk = min(MATMUL_TILE, n)
    nk = n // tk
    cols = acc_shape[1]

    def body(a_ref, b_ref, o_ref, acc):
        k = pl.program_id(2)

        @pl.when(k == 0)
        def _():
            acc[...] = jnp.zeros_like(acc)

        a_t = a_ref[...]
        if relu2:
            a_t = jnp.square(jnp.maximum(a_t.astype(F32), 0.0))
        acc[...] += _dot_tn(a_t, b_ref[...])

        @pl.when(k == nk - 1)
        def _():
            o_ref[...] = acc[...].reshape(o_ref.shape).astype(WIRE_DTYPE)

    gi, gj = grid_ij
    return pl.pallas_call(
        body, name=name, out_shape=jax.ShapeDtypeStruct((N_CHIP, out_rows, cols), WIRE_DTYPE),
        grid=(gi, gj, nk),
        in_specs=[a_spec(tk), b_spec(tk)],
        out_specs=out_spec,
        scratch_shapes=[pltpu.VMEM(acc_shape, F32)],
        compiler_params=_params(("arbitrary", "arbitrary", "arbitrary")),
    )(a, b)


def _row_tile(rows, cols, arrays):
    budget = VMEM_LIMIT // 3
    t = budget // (arrays * 2 * cols * 4)
    t = max(8, min(rows, t // 8 * 8))
    while rows % t:
        t -= 8
    return t


def _sum_partials(name, own, got, myq, l, nl, prev):
    _, rows, cols = own.shape
    tr = _row_tile(rows, cols, 3)
    nt = rows // tr

    def body(q_ref, own_ref, got_ref, *rest):
        o_ref = rest[-1]
        acc = own_ref[...].astype(F32)
        for k in range(3):
            acc = acc + got_ref[k].astype(F32)
        o_ref[...] = acc

    operands = [myq, own, got] + ([] if prev is None else [prev])
    return pl.pallas_call(
        body, name=name, out_shape=jax.ShapeDtypeStruct((nl * rows, cols), F32),
        grid_spec=pltpu.PrefetchScalarGridSpec(
            num_scalar_prefetch=1, grid=(nt,),
            in_specs=[pl.BlockSpec((None, tr, cols), lambda i, q: (q[0], i, 0)),
                      pl.BlockSpec((3, tr, cols), lambda i, q: (0, i, 0))]
            + ([] if prev is None else [pl.BlockSpec(memory_space=pl.ANY)]),
            out_specs=pl.BlockSpec((tr, cols), lambda i, q: (l * nt + i, 0))),
        input_output_aliases={} if prev is None else {3: 0},
        compiler_params=_params(("arbitrary",)),
    )(*operands)


def _adamw(name, w, m, v, g_a, g_b=None):
    rows, cols = w.shape
    tr = _row_tile(rows, cols, 9)
    c1 = 1.0 - ADAM_B1 ** ADAM_STEP
    c2 = 1.0 - ADAM_B2 ** ADAM_STEP

    def body(*refs):
        if g_b is None:
            w_ref, m_ref, v_ref, ga_ref, g_ref, d_ref, m2_ref, v2_ref = refs
            g = ga_ref[...]
        else:
            w_ref, m_ref, v_ref, ga_ref, gb_ref, g_ref, d_ref, m2_ref, v2_ref = refs
            g = ga_ref[...] + gb_ref[...]
        m2 = ADAM_B1 * m_ref[...] + (1.0 - ADAM_B1) * g
        v2 = ADAM_B2 * v_ref[...] + (1.0 - ADAM_B2) * (g * g)
        g_ref[...] = g
        m2_ref[...] = m2
        v2_ref[...] = v2
        d_ref[...] = -ADAM_LR * ((m2 / c1) / (jnp.sqrt(v2 / c2) + ADAM_EPS) + ADAM_WD * w_ref[...])

    tile = pl.BlockSpec((tr, cols), lambda i: (i, 0))
    operands = [w, m, v, g_a] + ([] if g_b is None else [g_b])
    return pl.pallas_call(
        body, name=name, out_shape=[jax.ShapeDtypeStruct((rows, cols), F32)] * 4,
        grid=(rows // tr,), in_specs=[tile] * len(operands), out_specs=[tile] * 4,
        compiler_params=_params(("arbitrary",)),
    )(*operands)


def _pack(parts):
    flat = [p.reshape(-1, LANES) for p in parts]
    for f in flat:
        assert f.shape[0] % 8 == 0
    return jnp.concatenate(flat, axis=0)


def _unpack(packed, shapes):
    out, r = [], 0
    for s in shapes:
        size = 1
        for e in s:
            size *= e
        rows = size // LANES
        out.append(packed[r:r + rows].reshape(s))
        r += rows
    return out


def kernel(x, c, w_ada, b_ada, norm1_g, w_in, a_ln_g, a_ln_b, a_ws, a_bs, w_pa, b_conv_w, b_conv_b, b_ln_g, b_ln_b, w_pb, w_out, norm2_g, w_ff1, w_ff2, final_g, loss_target, m_w_ada, m_b_ada, m_norm1_g, m_w_in, m_a_ln_g, m_a_ln_b, m_a_ws, m_a_bs, m_w_pa, m_b_conv_w, m_b_conv_b, m_b_ln_g, m_b_ln_b, m_w_pb, m_w_out, m_norm2_g, m_w_ff1, m_w_ff2, m_final_g, v_w_ada, v_b_ada, v_norm1_g, v_w_in, v_a_ln_g, v_a_ln_b, v_a_ws, v_a_bs, v_w_pa, v_b_conv_w, v_b_conv_b, v_b_ln_g, v_b_ln_b, v_w_pb, v_w_out, v_norm2_g, v_w_ff1, v_w_ff2, v_final_g):
    nb, t_len, d = x.shape
    nl = w_in.shape[0]
    n = nb * t_len
    cq = w_ada.shape[-1]
    cc = d // N_CHIP
    mx, my, mc = _my_place()
    myq = (2 * mx + my).astype(jnp.int32).reshape(1)

    taps = jnp.pad(b_conv_w.reshape(nl, CONV_TAPS, cc), ((0, 0), (0, HALO - CONV_TAPS), (0, 0)))
    first = jnp.concatenate([jnp.pad(c, ((0, 8 - nb), (0, 0))), taps.reshape(nl * HALO * cc // d, d)], axis=0)
    first = _all_to_all(jnp.broadcast_to(first[None], (N_DEV,) + first.shape), "gather_c_and_taps")
    c_all = first[:, :nb].reshape(N_DEV * nb, d)
    cwg = first[:, 8:].reshape(N_CHIP, 2, nl, HALO, cc)[:, 0]
    cw = cwg.transpose(1, 2, 0, 3).reshape(nl, HALO, d)
    cw = cw.reshape(nl, HALO, d // LANES, LANES).transpose(0, 2, 1, 3)
    mod_part = _ada_forward(c_all, w_ada, b_ada.reshape(nl, 1, N_CHIP * cq), myq)
    mod_slots = mod_part.reshape(nl, N_DEV, nb, cq).transpose(1, 0, 2, 3).reshape(N_DEV, nl * nb, cq)
    mod_got = _all_to_all(mod_slots, "exchange_mod").reshape(N_CHIP, 2, nl, nb, cq)[:, 0]
    mod6 = mod_got.transpose(1, 2, 0, 3).reshape(nl, nb, 6, d)
    mod = jnp.pad(mod6, ((0, 0), (0, 0), (0, 2), (0, 0)))

    big = ["w_in", "w_pa", "w_pb", "w_out", "w_ff1", "w_ff2"]
    ws_given = dict(w_in=(w_in, m_w_in, v_w_in), w_pa=(w_pa, m_w_pa, v_w_pa), w_pb=(w_pb, m_w_pb, v_w_pb),
                    w_out=(w_out, m_w_out, v_w_out), w_ff1=(w_ff1, m_w_ff1, v_w_ff1), w_ff2=(w_ff2, m_w_ff2, v_w_ff2))

    def own_slot(w_l, after=None):
        if after is not None:
            w_l = w_l - after[0, 0]
        empty = lax.empty((N_CHIP,) + w_l.shape, WIRE_DTYPE)
        return lax.dynamic_update_index_in_dim(empty, w_l.astype(WIRE_DTYPE), myq[0], 0)

    def zero_after(*arrays):
        z = jnp.zeros((8, LANES), F32)
        for a in arrays:
            piece = a.reshape(-1, a.shape[-1])[:8, :LANES]
            z = z + jnp.where(jnp.isfinite(piece), piece, 0.0) * 0.0
        return z

    first_sems = _split_start("gather_start_in_0", "gather", [], [own_slot(w_in[0])], zero_after(cw, mod[:, 0]))
    token = first = first_sems[4]
    gathers = []
    for l in range(nl):
        group = big[1:] if l == 0 else big
        send_sems, recv_sems, _, lands, token = _split_start(
            f"gather_start_{l}", "gather", [], [own_slot(ws_given[k][0][l], first) for k in group], token)
        if l == 0:
            send_sems = list(first_sems[0]) + list(send_sems)
            recv_sems = list(first_sems[1]) + list(recv_sems)
            lands = list(first_sems[3]) + list(lands)
        gathers.append((send_sems, recv_sems, lands))
    mod = mod + token[0, 0]

    def gather_wait(l, part, lo, hi, after):
        send_sems, recv_sems, lands = gathers[l]
        return _split_wait(f"gather_wait_{part}_{l}", "gather", send_sems[lo:hi], recv_sems[lo:hi], [],
                           lands[lo:hi], after)[1]

    vec3 = lambda p: p.reshape(nl, 1, d)
    g1, g2 = vec3(norm1_g), vec3(norm2_g)
    lng, lnb, cb, blg, blb = vec3(a_ln_g), vec3(a_ln_b), vec3(b_conv_b), vec3(b_ln_g), vec3(b_ln_b)
    bst = a_bs.transpose(0, 2, 1)

    xs = x.reshape(n, d)
    saved = []
    weights = []
    for l in range(nl):
        if l == 0:
            send_sems, recv_sems, lands = gathers[0]
            wg_in = lands[0]
            h, proj = _in_proj_quarter("in_proj_0_own", l, xs, mod, g1, wg_in, myq, t_len)
            for k, (px, py) in enumerate(_other_chips(mx, my)):
                (wg_in,) = _split_wait(f"gather_wait_in_0_{k}", "gather", send_sems[:1], recv_sems[:1], [], [wg_in],
                                       proj, peers=(k,))[1]
                quarter = (2 * px + py).astype(jnp.int32).reshape(1)
                proj = _in_proj_quarter(f"in_proj_0_{k}", l, h, None, None, wg_in, quarter, t_len, proj)
        else:
            (wg_in,) = gather_wait(l, "in", 0, 1, xs)
            h, proj = _in_proj(l, xs, mod, g1, wg_in, t_len)
        ya_in, yb_in, zc = _branches_fwd(l, proj, lng, lnb, a_ws, bst, cw, cb, blg, blb, t_len)
        wg_pa, wg_pb, wg_out = gather_wait(l, "mid", 1, 4, ya_in)
        ya, yb, merged, o, x1 = _merge_out(l, xs, mod, proj, ya_in, yb_in, wg_pa, wg_pb, wg_out, t_len)
        wg_ff1, wg_ff2 = gather_wait(l, "ffn", 4, 6, x1)
        h2, f, o2, x2 = _ffn_fwd(l, x1, mod, g2, wg_ff1, wg_ff2, t_len)
        saved.append((xs, h, proj, ya_in, yb_in, zc, ya, yb, merged, o, x1, h2, f, o2))
        weights.append((wg_in, wg_pa, wg_pb, wg_out, wg_ff1, wg_ff2))
        xs = x2

    loss_blk, dx, dfinal = _loss_head(xs, final_g.reshape(1, d), loss_target.reshape(n, d))

    tok = lambda w: (lambda tk: pl.BlockSpec((tk, w), lambda i, j, k: (k, 0)))
    tok_i = lambda w: (lambda tk: pl.BlockSpec((tk, w), lambda i, j, k: (k, i)))
    tok_j = lambda w: (lambda tk: pl.BlockSpec((tk, w), lambda i, j, k: (k, j)))
    qin = weights[0][0].shape[-1]
    hq = weights[0][4].shape[-1]
    rq = d // N_CHIP
    slot_i = lambda r, cdim: pl.BlockSpec((None, r, cdim), lambda i, j, k: (i, 0, 0))
    slot_j = lambda r, cdim: pl.BlockSpec((None, r, cdim), lambda i, j, k: (j, 0, 0))
    all_slots = pl.BlockSpec((N_CHIP, rq, d), lambda i, j, k: (0, 0, 0))
    scatters = []

    def scatter_start(l, part, names, grads, after):
        lands = [lax.empty((3,) + g.shape[1:], g.dtype) for g in grads]
        send_sems, recv_sems, srcs, lands, tok_out = _split_start(f"scatter_start_{part}_{l}", "scatter", grads, lands,
                                                                  after)
        scatters.append((f"scatter_wait_{part}_{l}", l, names, send_sems, recv_sems, srcs, lands))
        return tok_out

    dmods, small = [None] * nl, [None] * nl
    for l in reversed(range(nl)):
        x0, h, proj, ya_in, yb_in, zc, ya, yb, merged, o, x1, h2, f, o2 = saved[l]
        wg_in, wg_pa, wg_pb, wg_out, wg_ff1, wg_ff2 = weights[l]
        do2, df, dx1, dmod_c, dg2 = _ffn_bwd(l, dx, x1, mod, g2, o2, f, wg_ff1, wg_ff2, t_len, nb)
        g_ff2 = _weight_grad(f"grad_w_ff2_{l}", f, do2, tok_i(hq), tok(d), hq, slot_i(hq, d), (hq, d), (N_CHIP, 1),
                             relu2=True)
        g_ff1 = _weight_grad(f"grad_w_ff1_{l}", h2, df, tok(d), tok_j(hq), d, slot_j(d, hq), (d, hq), (1, N_CHIP))
        if l == 0:
            token = scatter_start(l, "ffn", ["w_ff2", "w_ff1"], [g_ff2, g_ff1], token)
        do, dya, dyb, dya_in, dyb_in, dproj, dmod_b = _merge_bwd(l, dx1, mod, o, ya, yb, proj, wg_pa, wg_pb, wg_out,
                                                                 t_len, nb, token)
        g_out = _weight_grad(f"grad_w_out_{l}", merged, do, tok(d), tok(d), rq, all_slots, (d, d), (1, 1))
        g_pa = _weight_grad(f"grad_w_pa_{l}", ya_in, dya, tok(d), tok(d), rq, all_slots, (d, d), (1, 1))
        g_pb = _weight_grad(f"grad_w_pb_{l}", yb_in, dyb, tok(d), tok(d), rq, all_slots, (d, d), (1, 1))
        if l == 0:
            token = scatter_start(l, "mid", ["w_out", "w_pa", "w_pb"], [g_out, g_pa, g_pb], token)
        dproj, dws, dbst, dcw, vecs = _branches_bwd(l, proj, zc, dya_in, dyb_in, dproj, lng, lnb, a_ws, bst, cw,
                                                    blg, blb, t_len, token)
        g_in = _weight_grad(f"grad_w_in_{l}", h, dproj, tok(d), tok_j(qin), d, slot_j(d, qin), (d, qin), (1, N_CHIP))
        if l == 0:
            token = scatter_start(l, "in", ["w_in"], [g_in], token)
        else:
            token = scatter_start(l, "all", ["w_ff2", "w_ff1", "w_out", "w_pa", "w_pb", "w_in"],
                                  [g_ff2, g_ff1, g_out, g_pa, g_pb, g_in], token)
        dx, dmod_a, dg1 = _in_proj_bwd(l, dproj, dx1, x0, mod, g1, wg_in, t_len, nb, token)
        dmods[l] = jnp.concatenate([dmod_a[:, 0:2], dmod_b[:, 2:3], dmod_c[:, 3:6]], axis=1)
        dcw = dcw.transpose(1, 0, 2).reshape(HALO, d)[:CONV_TAPS]
        small[l] = (dg1[0], vecs[0], vecs[1], dws, dbst.T, dcw, vecs[2], vecs[3], vecs[4], dg2[0])
    grad_x = dx.reshape(nb, t_len, d)

    names = ["norm1_g", "a_ln_g", "a_ln_b", "a_ws", "a_bs", "b_conv_w", "b_conv_b", "b_ln_g", "b_ln_b", "norm2_g"]
    stacked = [jnp.stack([small[l][k] for l in range(nl)]) for k in range(len(names))]
    stacked[5] = jnp.pad(stacked[5], ((0, 0), (0, HALO - CONV_TAPS), (0, 0)))
    stacked += [dfinal, loss_blk]
    part_shapes = [s.shape for s in stacked]
    packed = _pack(stacked)
    prow = packed.shape[0]
    pad_rows = (-prow) % (8 * N_DEV)
    packed = jnp.pad(packed, ((0, pad_rows), (0, 0)))
    srow = packed.shape[0] // N_DEV
    me = 4 * mx + 2 * my + mc

    def exchange_start(name, slots, after):
        send_sems, recv_sems, srcs, lands, tok_out = _split_start(
            name + "_start", "exchange", [slots], [lax.empty(slots.shape, slots.dtype)], after)
        return (name, send_sems, recv_sems, srcs, lands), tok_out

    def exchange_wait(handle, after):
        name, send_sems, recv_sems, srcs, lands = handle
        srcs, lands = _split_wait(name + "_wait", "exchange", send_sems, recv_sems, srcs, lands, after)
        own = lax.dynamic_slice_in_dim(srcs[0], me, 1, axis=0)
        return lax.dynamic_update_slice_in_dim(lands[0], own, me, axis=0)

    half = dict.fromkeys(big)

    def sum_arrived(entries, after):
        last = after
        for name, l, group, send_sems, recv_sems, srcs, lands in entries:
            srcs, lands = _split_wait(name, "scatter", send_sems, recv_sems, srcs, lands, after)
            for k, g_own, g_got in zip(group, srcs, lands):
                last = half[k] = _sum_partials(f"sum_{k}_{l}", g_own, g_got, myq, l, nl, half[k])
        return last

    early = max(1, (nl - 1) * 2 // 3)
    reduce_handle, token = exchange_start("reduce_small", packed.reshape(N_DEV, srow, LANES), token)
    last = sum_arrived(scatters[:early], token)
    mine = _sum_slots(exchange_wait(reduce_handle, last), "sum_small")
    dmod_rows = nl * nb * 6 * d // LANES
    second = jnp.concatenate([mine, jnp.stack(dmods).reshape(dmod_rows, LANES)], axis=0)
    gather_handle, token = exchange_start("gather_small_and_dmod",
                                          jnp.broadcast_to(second[None], (N_DEV,) + second.shape), token)
    last = sum_arrived(scatters[early:], token)
    second = exchange_wait(gather_handle, last)
    total = second[:, :srow].reshape(N_DEV * srow, LANES)[:prow]
    dmod_all = second[:, srow:].reshape(N_DEV, nl, nb, 6 * d).transpose(1, 0, 2, 3).reshape(nl, N_DEV * nb, 6 * d)

    sums = [half[k] for k in big]
    swap_send, swap_recv, sums, others, token = _split_start(
        "swap_start", "swap", sums, [lax.empty(s.shape, s.dtype) for s in sums], total)
    g_w_ada, g_b_ada = _ada_backward(c_all, dmod_all, myq, cq, token)

    sg = dict(zip(names + ["final_g", "loss"], _unpack(total, part_shapes)))
    loss = sg["loss"][0, 0]
    sg["b_conv_w"] = lax.dynamic_slice_in_dim(sg["b_conv_w"][:, :CONV_TAPS], myq[0] * cc, cc, axis=2).reshape(
        nl, CONV_TAPS, 1, cc)
    sg["final_g"] = sg["final_g"][0]
    sg["b_ada"] = g_b_ada.reshape(nl, N_CHIP * cq)
    small_names = ["b_ada", "norm1_g", "a_ln_g", "a_ln_b", "a_ws", "a_bs", "b_conv_w", "b_conv_b", "b_ln_g",
                   "b_ln_b", "norm2_g", "final_g"]
    given = dict(b_ada=(b_ada, m_b_ada, v_b_ada), norm1_g=(norm1_g, m_norm1_g, v_norm1_g),
                 a_ln_g=(a_ln_g, m_a_ln_g, v_a_ln_g), a_ln_b=(a_ln_b, m_a_ln_b, v_a_ln_b),
                 a_ws=(a_ws, m_a_ws, v_a_ws), a_bs=(a_bs, m_a_bs, v_a_bs),
                 b_conv_w=(b_conv_w, m_b_conv_w, v_b_conv_w), b_conv_b=(b_conv_b, m_b_conv_b, v_b_conv_b),
                 b_ln_g=(b_ln_g, m_b_ln_g, v_b_ln_g), b_ln_b=(b_ln_b, m_b_ln_b, v_b_ln_b),
                 norm2_g=(norm2_g, m_norm2_g, v_norm2_g), final_g=(final_g, m_final_g, v_final_g))

    def padded(a):
        rows = -(-a.size // (8 * LANES)) * 8
        return jnp.pad(a.reshape(-1), (0, rows * LANES - a.size)).reshape(rows, LANES)

    packs = [_pack([padded(given[k][j]) for k in small_names]) for j in range(3)]
    gpack = _pack([padded(sg[k].astype(F32)) for k in small_names])
    res_small = _adamw("adamw_small", packs[0], packs[1], packs[2], gpack)
    out = {}
    for j, kind in enumerate(["grad", "delta", "new_m", "new_v"]):
        r = 0
        for k in small_names:
            a = given[k][0]
            rows = -(-a.size // (8 * LANES)) * 8
            out[(kind, k)] = res_small[j][r:r + rows].reshape(-1)[:a.size].reshape(a.shape)
            r += rows

    res = _adamw("adamw_w_ada", w_ada.reshape(nl * d, cq), m_w_ada.reshape(nl * d, cq), v_w_ada.reshape(nl * d, cq),
                 g_w_ada.reshape(nl * d, cq))
    for kind, r in zip(["grad", "delta", "new_m", "new_v"], res):
        out[(kind, "w_ada")] = r.reshape(w_ada.shape)

    sums, others = _split_wait("swap_wait", "swap", swap_send, swap_recv, sums, others, res[0])
    for k, s_mine, s_other in zip(big, sums, others):
        w, m, v = ws_given[k]
        cols = w.shape[-1]
        res = _adamw(f"adamw_{k}", w.reshape(-1, cols), m.reshape(-1, cols), v.reshape(-1, cols), s_mine, s_other)
        for kind, r in zip(["grad", "delta", "new_m", "new_v"], res):
            out[(kind, k)] = r.reshape(w.shape)

    order = ["w_ada", "b_ada", "norm1_g", "w_in", "a_ln_g", "a_ln_b", "a_ws", "a_bs", "w_pa", "b_conv_w", "b_conv_b",
             "b_ln_g", "b_ln_b", "w_pb", "w_out", "norm2_g", "w_ff1", "w_ff2", "final_g"]
    return (loss, grad_x, *[out[("grad", k)] for k in order], *[out[("delta", k)] for k in order],
            *[out[("new_m", k)] for k in order], *[out[("new_v", k)] for k in order])
```

```python
import jax
import jax.numpy as jnp
from jax import lax
from jax.experimental import pallas as pl
from jax.experimental.pallas import tpu as pltpu

F32 = jnp.float32
MXU_DTYPE = jnp.bfloat16
ACT_DTYPE = jnp.bfloat16
WIRE_DTYPE = jnp.bfloat16

EPS = 1e-6
CHUNK = 128
HEADS = 8
CONV_TAPS = 31
HALO = 32
N_DEV = 8
N_CHIP = 4
ADAM_LR, ADAM_B1, ADAM_B2, ADAM_EPS, ADAM_WD, ADAM_STEP = 0.001, 0.9, 0.999, 1e-08, 0.01, 10

V7X_VMEM_BYTES = 64 * 1024 * 1024
VMEM_LIMIT = V7X_VMEM_BYTES * 7 // 8
TOKEN_TILE = 512
MATMUL_TILE = 2048
FFN_BWD_TILE = 512
CONV_ROWS = 64
TAP_GRAD_ROWS = 32
LANES = 128
MESH_ID = pl.DeviceIdType.MESH


def _params(sem=None):
    return pltpu.CompilerParams(dimension_semantics=sem, vmem_limit_bytes=VMEM_LIMIT)


def _resident(shape):
    return pl.BlockSpec(shape, lambda *_: (0,) * len(shape), pipeline_mode=pl.Buffered(1))


def _dot(a, b):
    return jnp.dot(a.astype(MXU_DTYPE), b.astype(MXU_DTYPE), preferred_element_type=F32)


def _dot_nt(a, b):
    return lax.dot_general(a.astype(MXU_DTYPE), b.astype(MXU_DTYPE), (((1,), (1,)), ((), ())),
                           preferred_element_type=F32)


def _dot_tn(a, b):
    return lax.dot_general(a.astype(MXU_DTYPE), b.astype(MXU_DTYPE), (((0,), (0,)), ((), ())),
                           preferred_element_type=F32)


def _colsum(a):
    return jnp.sum(a, axis=0, keepdims=True)


def _rowmean(a):
    return jnp.mean(a, axis=-1, keepdims=True)


def _sigmoid(a):
    return 1.0 / (1.0 + jnp.exp(-a))


def _modnorm_fwd(x, g, sc, sh):
    r = lax.rsqrt(_rowmean(x * x) + EPS)
    return (x * r) * (g * (1.0 + sc)) + sh


def _modnorm_bwd(x, dh, g, sc):
    r = lax.rsqrt(_rowmean(x * x) + EPS)
    xn = x * r
    dxn = dh * (g * (1.0 + sc))
    dx = r * (dxn - xn * _rowmean(dxn * xn))
    return dx, _colsum(dh), _colsum(dh * xn)


def _ln_stats(v):
    mu = _rowmean(v)
    vc = v - mu
    rstd = lax.rsqrt(_rowmean(vc * vc) + EPS)
    return vc * rstd, rstd


def _ln_bwd(dy, vhat, rstd, g):
    dvh = dy * g
    return rstd * (dvh - _rowmean(dvh) - vhat * _rowmean(dvh * vhat))


def _causal_mask():
    row = lax.broadcasted_iota(jnp.int32, (CHUNK, CHUNK), 0)
    col = lax.broadcasted_iota(jnp.int32, (CHUNK, CHUNK), 1)
    return row >= col


def _my_place():
    return lax.axis_index("x"), lax.axis_index("y"), lax.axis_index("c")


def _other_chips(mx, my):
    return [(1 - mx, my), (mx, 1 - my), (1 - mx, 1 - my)]


def _other_devices(mx, my, mc):
    return [((mx + ((k >> 2) & 1)) % 2, (my + ((k >> 1) & 1)) % 2, (mc + (k & 1)) % 2) for k in range(1, N_DEV)]


def _all_to_all(x, name):
    assert x.shape[0] == N_DEV

    def body(x_ref, o_ref, send_sems, recv_sems):
        mx, my, mc = _my_place()
        me = 4 * mx + 2 * my + mc
        o_ref[me] = x_ref[me]
        copies = []
        for k, (px, py, pc) in enumerate(_other_devices(mx, my, mc)):
            cp = pltpu.make_async_remote_copy(
                src_ref=x_ref.at[4 * px + 2 * py + pc], dst_ref=o_ref.at[me],
                send_sem=send_sems.at[k], recv_sem=recv_sems.at[k],
                device_id=(px, py, pc), device_id_type=MESH_ID)
            cp.start()
            copies.append(cp)
        for cp in copies:
            cp.wait()

    return pl.pallas_call(
        body, name=name, out_shape=jax.ShapeDtypeStruct(x.shape, x.dtype),
        in_specs=[pl.BlockSpec(memory_space=pltpu.VMEM)],
        out_specs=pl.BlockSpec(memory_space=pltpu.VMEM),
        scratch_shapes=[pltpu.SemaphoreType.DMA((N_DEV - 1,)), pltpu.SemaphoreType.DMA((N_DEV - 1,))],
        compiler_params=pltpu.CompilerParams(vmem_limit_bytes=VMEM_LIMIT),
    )(x)


def _sum_slots(x, name):
    def body(x_ref, o_ref):
        acc = x_ref[0]
        for s in range(1, N_DEV):
            acc = acc + x_ref[s]
        o_ref[...] = acc

    return pl.pallas_call(
        body, name=name, out_shape=jax.ShapeDtypeStruct(x.shape[1:], x.dtype),
        in_specs=[pl.BlockSpec(memory_space=pltpu.VMEM)], out_specs=pl.BlockSpec(memory_space=pltpu.VMEM),
        compiler_params=pltpu.CompilerParams(vmem_limit_bytes=VMEM_LIMIT),
    )(x)


HBM_SPEC = pl.BlockSpec(memory_space=pltpu.HBM)
SEM_SPEC = pl.BlockSpec(memory_space=pltpu.SEMAPHORE)
ANY_SPEC = pl.BlockSpec(memory_space=pl.ANY)
SPLIT_EFFECT = pltpu.SideEffectType.DATAFLOW_SIDE_EFFECTING


def _quarter_copies(mode, srcs, lands, send_sems, recv_sems, peers=(0, 1, 2)):
    mx, my, mc = _my_place()
    myq = 2 * mx + my
    if mode == "swap":
        return [pltpu.make_async_remote_copy(
            src_ref=srcs[a], dst_ref=lands[a], send_sem=send_sems[a].at[0], recv_sem=recv_sems[a].at[0],
            device_id=(mx, my, 1 - mc), device_id_type=MESH_ID) for a in range(len(lands))]
    copies = []
    if mode == "exchange":
        me = 4 * mx + 2 * my + mc
        for a in range(len(lands)):
            for k, (px, py, pc) in enumerate(_other_devices(mx, my, mc)):
                copies.append(pltpu.make_async_remote_copy(
                    src_ref=srcs[a].at[4 * px + 2 * py + pc], dst_ref=lands[a].at[me],
                    send_sem=send_sems[a].at[k], recv_sem=recv_sems[a].at[k],
                    device_id=(px, py, pc), device_id_type=MESH_ID))
        return copies
    for a in range(len(lands)):
        for k, (px, py) in enumerate(_other_chips(mx, my)):
            if k not in peers:
                continue
            if mode == "gather":
                src, dst = lands[a].at[myq], lands[a].at[myq]
            else:
                src, dst = srcs[a].at[2 * px + py], lands[a].at[k]
            copies.append(pltpu.make_async_remote_copy(
                src_ref=src, dst_ref=dst, send_sem=send_sems[a].at[k], recv_sem=recv_sems[a].at[k],
                device_id=(px, py, mc), device_id_type=MESH_ID))
    return copies


def _split_start(name, mode, srcs, lands, after):
    ns, n = len(srcs), len(lands)

    def body(*refs):
        outs = refs[ns + n + 1:]
        for cp in _quarter_copies(mode, refs[:ns], refs[ns:ns + n], outs[:n], outs[n:2 * n]):
            cp.start()
        token = outs[-1]
        token[...] = jnp.zeros_like(token)

    arrays = list(srcs) + list(lands)
    per_array = {"swap": 1, "exchange": N_DEV - 1}.get(mode, 3)
    res = pl.pallas_call(
        body, name=name,
        out_shape=[pltpu.SemaphoreType.DMA((per_array,))] * (2 * n) + [pltpu.HBM(x.shape, x.dtype) for x in arrays]
        + [jax.ShapeDtypeStruct((8, LANES), F32)],
        in_specs=[HBM_SPEC] * (ns + n) + [ANY_SPEC],
        out_specs=[SEM_SPEC] * (2 * n) + [HBM_SPEC] * (ns + n) + [pl.BlockSpec(memory_space=pltpu.VMEM)],
        input_output_aliases={i: 2 * n + i for i in range(ns + n)},
        compiler_params=pltpu.CompilerParams(has_side_effects=SPLIT_EFFECT),
    )(*[pltpu.with_memory_space_constraint(x, pltpu.HBM) for x in arrays], after)
    return res[:n], res[n:2 * n], res[2 * n:2 * n + ns], res[2 * n + ns:2 * n + ns + n], res[-1]


def _split_wait(name, mode, send_sems, recv_sems, srcs, lands, after, peers=(0, 1, 2)):
    ns, n = len(srcs), len(lands)

    def body(*refs):
        sems = refs[ns + n:ns + 3 * n]
        for cp in _quarter_copies(mode, refs[:ns], refs[ns:ns + n], sems[:n], sems[n:], peers):
            cp.wait_send()
            cp.wait_recv()

    arrays = list(srcs) + list(lands)
    res = pl.pallas_call(
        body, name=name,
        out_shape=[pltpu.HBM(x.shape, x.dtype) for x in arrays],
        in_specs=[HBM_SPEC] * (ns + n) + [SEM_SPEC] * (2 * n) + [ANY_SPEC],
        out_specs=[HBM_SPEC] * (ns + n),
        input_output_aliases={i: i for i in range(ns + n)},
        compiler_params=pltpu.CompilerParams(has_side_effects=SPLIT_EFFECT),
    )(*arrays, *send_sems, *recv_sems, after)
    return res[:ns], res[ns:]


def _ada_forward(c_all, w_ada, b_ada3, myq):
    nl, d, cq = w_ada.shape
    nb = c_all.shape[0]

    def body(q_ref, c_ref, w_ref, b_ref, o_ref):
        c = c_ref[...]
        act = c * _sigmoid(c)
        o_ref[...] = _dot(act, w_ref[...]) + b_ref[...]

    return pl.pallas_call(
        body, name="ada_forward",
        out_shape=jax.ShapeDtypeStruct((nl, nb, cq), F32),
        grid_spec=pltpu.PrefetchScalarGridSpec(
            num_scalar_prefetch=1, grid=(nl,),
            in_specs=[pl.BlockSpec((nb, d), lambda l, q: (0, 0)),
                      pl.BlockSpec((None, d, cq), lambda l, q: (l, 0, 0)),
                      pl.BlockSpec((None, 1, cq), lambda l, q: (l, 0, q[0]))],
            out_specs=pl.BlockSpec((None, nb, cq), lambda l, q: (l, 0, 0))),
        compiler_params=_params(("arbitrary",)),
    )(myq, c_all, w_ada, b_ada3)


def _ada_backward(c_all, dmod_all, myq, cq, after):
    nb, d = c_all.shape
    nl = dmod_all.shape[0]
    full = dmod_all.shape[2]

    def body(q_ref, c_ref, dq_ref, dall_ref, after_ref, gw_ref, gb_ref):
        c = c_ref[...]
        act = c * _sigmoid(c)
        gw_ref[...] = _dot_tn(act, dq_ref[...])
        gb_ref[...] = _colsum(dall_ref[...])

    return pl.pallas_call(
        body, name="ada_backward",
        out_shape=[jax.ShapeDtypeStruct((nl, d, cq), F32), jax.ShapeDtypeStruct((nl, 1, full), F32)],
        grid_spec=pltpu.PrefetchScalarGridSpec(
            num_scalar_prefetch=1, grid=(nl,),
            in_specs=[pl.BlockSpec((nb, d), lambda l, q: (0, 0)),
                      pl.BlockSpec((None, nb, cq), lambda l, q: (l, 0, q[0])),
                      pl.BlockSpec((None, nb, full), lambda l, q: (l, 0, 0)), ANY_SPEC],
            out_specs=[pl.BlockSpec((None, d, cq), lambda l, q: (l, 0, 0)),
                       pl.BlockSpec((None, 1, full), lambda l, q: (l, 0, 0))]),
        compiler_params=_params(("arbitrary",)),
    )(myq, c_all, dmod_all, dmod_all, after)


def _in_proj(l, x, mod, g1, wg_in, t_len):
    n, d = x.shape
    tm = min(TOKEN_TILE, t_len)
    tpb = t_len // tm
    qc = wg_in.shape[-1]

    def body(x_ref, mod_ref, g_ref, w_ref, h_ref, proj_ref):
        h = _modnorm_fwd(x_ref[...], g_ref[...], mod_ref[1:2, :], mod_ref[0:1, :]).astype(MXU_DTYPE)
        h_ref[...] = h.astype(ACT_DTYPE)
        for q in range(N_CHIP):
            proj_ref[:, q * qc:(q + 1) * qc] = jnp.dot(h, w_ref[q], preferred_element_type=F32).astype(ACT_DTYPE)

    return pl.pallas_call(
        body, name=f"in_proj_{l}",
        out_shape=[jax.ShapeDtypeStruct((n, d), ACT_DTYPE), jax.ShapeDtypeStruct((n, N_CHIP * qc), ACT_DTYPE)],
        grid=(n // tm,),
        in_specs=[pl.BlockSpec((tm, d), lambda i: (i, 0)),
                  pl.BlockSpec((None, None, 8, d), lambda i: (l, i // tpb, 0, 0)),
                  pl.BlockSpec((None, 1, d), lambda i: (l, 0, 0)),
                  _resident((N_CHIP, d, qc))],
        out_specs=[pl.BlockSpec((tm, d), lambda i: (i, 0)),
                   pl.BlockSpec((tm, N_CHIP * qc), lambda i: (i, 0))],
        compiler_params=_params(("arbitrary",)),
    )(x, mod, g1, wg_in)


def _in_proj_quarter(name, l, src, mod, g1, wg_in, quarter, t_len, prev=None):
    n, d = src.shape
    tm = min(TOKEN_TILE, t_len)
    tpb = t_len // tm
    qc = wg_in.shape[-1]
    first = prev is None

    def body(q_ref, *refs):
        if first:
            x_ref, mod_ref, g_ref, w_ref, h_ref, proj_ref = refs
            h = _modnorm_fwd(x_ref[...], g_ref[...], mod_ref[1:2, :], mod_ref[0:1, :]).astype(MXU_DTYPE)
            h_ref[...] = h.astype(ACT_DTYPE)
        else:
            h_in_ref, w_ref, _, proj_ref = refs
            h = h_in_ref[...].astype(MXU_DTYPE)
        proj_ref[...] = jnp.dot(h, w_ref[...], preferred_element_type=F32).astype(ACT_DTYPE)

    tile = pl.BlockSpec((tm, d), lambda i, q: (i, 0))
    w_spec = pl.BlockSpec((None, d, qc), lambda i, q: (q[0], 0, 0))
    proj_spec = pl.BlockSpec((tm, qc), lambda i, q: (i, q[0]))
    proj_shape = jax.ShapeDtypeStruct((n, N_CHIP * qc), ACT_DTYPE)
    if first:
        operands = [quarter, src, mod, g1, wg_in]
        in_specs = [tile, pl.BlockSpec((None, None, 8, d), lambda i, q: (l, i // tpb, 0, 0)),
                    pl.BlockSpec((None, 1, d), lambda i, q: (l, 0, 0)), w_spec]
        out_shape, out_specs, aliases = [jax.ShapeDtypeStruct((n, d), ACT_DTYPE), proj_shape], [tile, proj_spec], {}
    else:
        operands = [quarter, src, wg_in, prev]
        in_specs = [tile, w_spec, ANY_SPEC]
        out_shape, out_specs, aliases = proj_shape, proj_spec, {3: 0}
    return pl.pallas_call(
        body, name=name, out_shape=out_shape,
        grid_spec=pltpu.PrefetchScalarGridSpec(num_scalar_prefetch=1, grid=(n // tm,), in_specs=in_specs,
                                               out_specs=out_specs),
        input_output_aliases=aliases,
        compiler_params=_params(("arbitrary",)),
    )(*operands)


def _masked_ws(ws_ref, wm_s):
    mask = _causal_mask()
    for h in range(HEADS):
        wm_s[h] = jnp.where(mask, ws_ref[h], 0.0).astype(MXU_DTYPE)


def _fill_z(i, tpb, ah_ref, gh_ref, zext):
    ah = ah_ref[...].astype(F32)
    gh = gh_ref[...].astype(F32)
    keep = jnp.where(i % tpb == 0, 0.0, 1.0)
    _put_lanes(zext, slice(0, HALO), ah * _sigmoid(gh) * keep)


def _put_lanes(dst3, rows, value):
    for lc in range(value.shape[-1] // LANES):
        dst3[lc, rows, :] = value[:, lc * LANES:(lc + 1) * LANES]


def _tap_windows(src3, lc, base, rows, flip):
    offs = {k: (CONV_TAPS - 1 - k) if flip else (k + 2) for k in range(CONV_TAPS)}
    for r in range(8):
        taps = [k for k in offs if offs[k] % 8 == r]
        lo = min(offs[k] for k in taps)
        hi = max(offs[k] for k in taps)
        win = src3[lc, pl.ds(base + lo, hi - lo + rows), :]
        for k in taps:
            yield k, win[offs[k] - lo:offs[k] - lo + rows]


def _conv_taps(src3, w3_ref, dst3, lc, nrows, flip):
    for b in range(nrows // CONV_ROWS):
        acc = jnp.zeros((CONV_ROWS, LANES), F32)
        for k, win in _tap_windows(src3, lc, b * CONV_ROWS, CONV_ROWS, flip):
            acc = acc + win * w3_ref[lc, k:k + 1, :]
        dst3[lc, b * CONV_ROWS:(b + 1) * CONV_ROWS, :] = acc


def _branches_fwd(l, proj, lng, lnb, ws, bst, cw, cb, blg, blb, t_len):
    n = proj.shape[0]
    d = lng.shape[-1]
    tm = min(TOKEN_TILE, t_len)
    tpb = t_len // tm
    per = tm // HALO
    nchunk = tm // CHUNK

    def body(u_ref, v_ref, a_ref, g_ref, ah_ref, gh_ref, lng_ref, lnb_ref, ws_ref, bst_ref, cw_ref, cb_ref,
             blg_ref, blb_ref, ya_ref, yb_ref, zc_ref, wm_s, zext, zc3):
        i = pl.program_id(0)
        _masked_ws(ws_ref, wm_s)
        _fill_z(i, tpb, ah_ref, gh_ref, zext)

        def chunk(c, carry):
            r0 = pl.multiple_of(c * CHUNK, CHUNK)
            rows = pl.ds(r0, CHUNK)
            vhat, _ = _ln_stats(v_ref[rows, :].astype(F32))
            vn = (vhat * lng_ref[...] + lnb_ref[...]).astype(MXU_DTYPE)
            u = u_ref[rows, :].astype(F32)
            for h in range(HEADS):
                cols = slice(h * CHUNK, (h + 1) * CHUNK)
                s = jnp.dot(wm_s[h], vn[:, cols], preferred_element_type=F32) + bst_ref[:, h:h + 1]
                ya_ref[rows, cols] = (u[:, cols] * s).astype(ACT_DTYPE)
            a = a_ref[rows, :].astype(F32)
            g = g_ref[rows, :].astype(F32)
            _put_lanes(zext, pl.ds(HALO + r0, CHUNK), a * _sigmoid(g))
            return carry

        lax.fori_loop(0, nchunk, chunk, 0)

        def lane_chunk(lc, carry):
            _conv_taps(zext, cw_ref, zc3, lc, tm, flip=False)
            return carry

        lax.fori_loop(0, d // LANES, lane_chunk, 0)

        def chunk2(c, carry):
            r0 = pl.multiple_of(c * CHUNK, CHUNK)
            rows = pl.ds(r0, CHUNK)
            for lc in range(d // LANES):
                lanes = slice(lc * LANES, (lc + 1) * LANES)
                zc_ref[rows, lanes] = (zc3[lc, rows, :] + cb_ref[:, lanes]).astype(ACT_DTYPE)
            zhat, _ = _ln_stats(zc_ref[rows, :].astype(F32))
            zn = zhat * blg_ref[...] + blb_ref[...]
            yb_ref[rows, :] = (zn * _sigmoid(zn)).astype(ACT_DTYPE)
            return carry

        lax.fori_loop(0, nchunk, chunk2, 0)

    col = lambda k: pl.BlockSpec((tm, d), lambda i: (i, k))
    halo = lambda k: pl.BlockSpec((HALO, d), lambda i: (jnp.maximum(i * per - 1, 0), k))
    vec = pl.BlockSpec((None, 1, d), lambda i: (l, 0, 0))
    out = pl.BlockSpec((tm, d), lambda i: (i, 0))
    return pl.pallas_call(
        body, name=f"branches_fwd_{l}",
        out_shape=[jax.ShapeDtypeStruct((n, d), ACT_DTYPE)] * 3,
        grid=(n // tm,),
        in_specs=[col(0), col(1), col(2), col(3), halo(2), halo(3), vec, vec,
                  pl.BlockSpec((None, HEADS, CHUNK, CHUNK), lambda i: (l, 0, 0, 0)),
                  pl.BlockSpec((None, CHUNK, HEADS), lambda i: (l, 0, 0)),
                  pl.BlockSpec((None, d // LANES, HALO, LANES), lambda i: (l, 0, 0, 0)), vec, vec, vec],
        out_specs=[out, out, out],
        scratch_shapes=[pltpu.VMEM((HEADS, CHUNK, CHUNK), MXU_DTYPE), pltpu.VMEM((d // LANES, HALO + tm, LANES), F32),
                        pltpu.VMEM((d // LANES, tm, LANES), F32)],
        compiler_params=_params(("arbitrary",)),
    )(proj, proj, proj, proj, proj, proj, lng, lnb, ws, bst, cw, cb, blg, blb)


def _merge_out(l, x, mod, proj, ya_in, yb_in, wg_pa, wg_pb, wg_out, t_len):
    n, d = x.shape
    tm = min(TOKEN_TILE, t_len)
    tpb = t_len // tm
    rq = d // N_CHIP

    def body(x_ref, mod_ref, ga_ref, gb_ref, yai_ref, ybi_ref, wpa_ref, wpb_ref, wo_ref,
             ya_ref, yb_ref, mg_ref, o_ref, x1_ref):
        wpa = wpa_ref[...].reshape(d, d)
        wpb = wpb_ref[...].reshape(d, d)
        wo = wo_ref[...].reshape(d, d)
        ya = jnp.dot(yai_ref[...].astype(MXU_DTYPE), wpa, preferred_element_type=F32)
        yb = jnp.dot(ybi_ref[...].astype(MXU_DTYPE), wpb, preferred_element_type=F32)
        merged = _sigmoid(ga_ref[...].astype(F32)) * ya + _sigmoid(gb_ref[...].astype(F32)) * yb
        o = _dot(merged, wo)
        ya_ref[...] = ya.astype(ACT_DTYPE)
        yb_ref[...] = yb.astype(ACT_DTYPE)
        mg_ref[...] = merged.astype(ACT_DTYPE)
        o_ref[...] = o.astype(ACT_DTYPE)
        x1_ref[...] = x_ref[...] + mod_ref[2:3, :] * o

    tile = pl.BlockSpec((tm, d), lambda i: (i, 0))
    wspec = pl.BlockSpec((N_CHIP, rq, d), lambda i: (0, 0, 0))
    return pl.pallas_call(
        body, name=f"merge_out_{l}",
        out_shape=[jax.ShapeDtypeStruct((n, d), ACT_DTYPE)] * 4 + [jax.ShapeDtypeStruct((n, d), F32)],
        grid=(n // tm,),
        in_specs=[tile, pl.BlockSpec((None, None, 8, d), lambda i: (l, i // tpb, 0, 0)),
                  pl.BlockSpec((tm, d), lambda i: (i, 4)), pl.BlockSpec((tm, d), lambda i: (i, 5)),
                  tile, tile, wspec, wspec, wspec],
        out_specs=[tile] * 5,
        compiler_params=_params(("arbitrary",)),
    )(x, mod, proj, proj, ya_in, yb_in, wg_pa, wg_pb, wg_out)


def _ffn_fwd(l, x1, mod, g2, wg_ff1, wg_ff2, t_len):
    n, d = x1.shape
    tm = min(TOKEN_TILE, t_len)
    tpb = t_len // tm
    hq = wg_ff1.shape[-1]
    hid = N_CHIP * hq

    def body(x_ref, mod_ref, g_ref, w1_ref, w2_ref, h_ref, f_ref, o2_ref, x2_ref, a2_s):
        h = _modnorm_fwd(x_ref[...], g_ref[...], mod_ref[4:5, :], mod_ref[3:4, :]).astype(MXU_DTYPE)
        h_ref[...] = h.astype(ACT_DTYPE)
        for q in range(N_CHIP):
            cols = slice(q * hq, (q + 1) * hq)
            f = jnp.dot(h, w1_ref[q], preferred_element_type=F32)
            f_ref[:, cols] = f.astype(ACT_DTYPE)
            a2_s[:, cols] = jnp.square(jnp.maximum(f, 0.0)).astype(MXU_DTYPE)
        o2 = jnp.dot(a2_s[...], w2_ref[...].reshape(hid, d), preferred_element_type=F32)
        o2_ref[...] = o2.astype(ACT_DTYPE)
        x2_ref[...] = x_ref[...] + mod_ref[5:6, :] * o2

    tile = pl.BlockSpec((tm, d), lambda i: (i, 0))
    return pl.pallas_call(
        body, name=f"ffn_fwd_{l}",
        out_shape=[jax.ShapeDtypeStruct((n, d), ACT_DTYPE), jax.ShapeDtypeStruct((n, hid), ACT_DTYPE),
                   jax.ShapeDtypeStruct((n, d), ACT_DTYPE), jax.ShapeDtypeStruct((n, d), F32)],
        grid=(n // tm,),
        in_specs=[tile, pl.BlockSpec((None, None, 8, d), lambda i: (l, i // tpb, 0, 0)),
                  pl.BlockSpec((None, 1, d), lambda i: (l, 0, 0)),
                  _resident((N_CHIP, d, hq)), _resident((N_CHIP, hq, d))],
        out_specs=[tile, pl.BlockSpec((tm, hid), lambda i: (i, 0)), tile, tile],
        scratch_shapes=[pltpu.VMEM((tm, hid), MXU_DTYPE)],
        compiler_params=_params(("arbitrary",)),
    )(x1, mod, g2, wg_ff1, wg_ff2)


def _loss_head(x, final_g, target):
    n, d = x.shape
    tm = min(TOKEN_TILE, n)

    def body(x_ref, g_ref, t_ref, loss_ref, dx_ref, dg_ref):
        @pl.when(pl.program_id(0) == 0)
        def _():
            loss_ref[...] = jnp.zeros_like(loss_ref)
            dg_ref[...] = jnp.zeros_like(dg_ref)

        x_t = x_ref[...]
        g = g_ref[...]
        r = lax.rsqrt(_rowmean(x_t * x_t) + EPS)
        xn = x_t * r
        e = xn * g - t_ref[...]
        loss_ref[...] += jnp.sum(e * e) * (0.5 / d)
        dy = e * (1.0 / d)
        dxn = dy * g
        dx_ref[...] = r * (dxn - xn * _rowmean(dxn * xn))
        dg_ref[0:1, :] += _colsum(dy * xn)

    tile = pl.BlockSpec((tm, d), lambda i: (i, 0))
    return pl.pallas_call(
        body, name="loss_head",
        out_shape=[jax.ShapeDtypeStruct((8, LANES), F32), jax.ShapeDtypeStruct((n, d), F32),
                   jax.ShapeDtypeStruct((8, d), F32)],
        grid=(n // tm,),
        in_specs=[tile, pl.BlockSpec((1, d), lambda i: (0, 0)), tile],
        out_specs=[pl.BlockSpec((8, LANES), lambda i: (0, 0)), tile, pl.BlockSpec((8, d), lambda i: (0, 0))],
        compiler_params=_params(("arbitrary",)),
    )(x, final_g, target)


def _norm_tail(x_ref, dxin_ref, dh, g_ref, sc, dx_ref, dmod_ref, dg_ref, row_sh, row_sc):
    dxm, dsh, q = _modnorm_bwd(x_ref[...], dh, g_ref[...], sc)
    dx_ref[...] = dxin_ref[...] + dxm
    dmod_ref[row_sh:row_sh + 1, :] += dsh
    dmod_ref[row_sc:row_sc + 1, :] += g_ref[...] * q
    dg_ref[0:1, :] += (1.0 + sc) * q


def _ffn_bwd(l, dx2, x1, mod, g2, o2, f, wg_ff1, wg_ff2, t_len, nb):
    n, d = dx2.shape
    tm = min(FFN_BWD_TILE, t_len)
    tpb = t_len // tm
    hq = wg_ff1.shape[-1]
    hid = N_CHIP * hq

    def body(dx2_ref, x1_ref, mod_ref, g_ref, o2_ref, f_ref, w1_ref, w2_ref,
             do2_ref, df_ref, dx1_ref, dmod_ref, dg_ref):
        i = pl.program_id(0)

        @pl.when(i == 0)
        def _():
            dg_ref[...] = jnp.zeros_like(dg_ref)

        @pl.when(i % tpb == 0)
        def _():
            dmod_ref[...] = jnp.zeros_like(dmod_ref)

        dx2_t = dx2_ref[...]
        dmod_ref[5:6, :] += _colsum(dx2_t * o2_ref[...].astype(F32))
        do2 = (dx2_t * mod_ref[5:6, :]).astype(MXU_DTYPE)
        do2_ref[...] = do2.astype(ACT_DTYPE)
        dh = jnp.zeros((tm, d), F32)
        for q in range(N_CHIP):
            cols = slice(q * hq, (q + 1) * hq)
            da2 = _dot_nt(do2, w2_ref[q])
            df = (da2 * (2.0 * jnp.maximum(f_ref[:, cols].astype(F32), 0.0))).astype(MXU_DTYPE)
            df_ref[:, cols] = df.astype(ACT_DTYPE)
            dh = dh + _dot_nt(df, w1_ref[q])
        _norm_tail(x1_ref, dx2_ref, dh, g_ref, mod_ref[4:5, :], dx1_ref, dmod_ref, dg_ref, 3, 4)

    tile = pl.BlockSpec((tm, d), lambda i: (i, 0))
    wide = pl.BlockSpec((tm, hid), lambda i: (i, 0))
    return pl.pallas_call(
        body, name=f"ffn_bwd_{l}",
        out_shape=[jax.ShapeDtypeStruct((n, d), ACT_DTYPE), jax.ShapeDtypeStruct((n, hid), ACT_DTYPE),
                   jax.ShapeDtypeStruct((n, d), F32), jax.ShapeDtypeStruct((nb, 8, d), F32),
                   jax.ShapeDtypeStruct((8, d), F32)],
        grid=(n // tm,),
        in_specs=[tile, tile, pl.BlockSpec((None, None, 8, d), lambda i: (l, i // tpb, 0, 0)),
                  pl.BlockSpec((None, 1, d), lambda i: (l, 0, 0)), tile, wide,
                  _resident((N_CHIP, d, hq)), _resident((N_CHIP, hq, d))],
        out_specs=[tile, wide, tile, pl.BlockSpec((None, 8, d), lambda i: (i // tpb, 0, 0)),
                   pl.BlockSpec((8, d), lambda i: (0, 0))],
        compiler_params=_params(("arbitrary",)),
    )(dx2, x1, mod, g2, o2, f, wg_ff1, wg_ff2)


def _merge_bwd(l, dx1, mod, o, ya, yb, proj, wg_pa, wg_pb, wg_out, t_len, nb, after):
    n, d = dx1.shape
    tm = min(TOKEN_TILE, t_len)
    tpb = t_len // tm
    rq = d // N_CHIP

    def body(dx_ref, mod_ref, o_ref, ya_ref, yb_ref, ga_ref, gb_ref, wpa_ref, wpb_ref, wo_ref, after_ref,
             do_ref, dya_ref, dyb_ref, dyai_ref, dybi_ref, dproj_ref, dmod_ref):
        i = pl.program_id(0)

        @pl.when(i % tpb == 0)
        def _():
            dmod_ref[...] = jnp.zeros_like(dmod_ref)

        dx = dx_ref[...]
        dmod_ref[2:3, :] += _colsum(dx * o_ref[...].astype(F32))
        do = (dx * mod_ref[2:3, :]).astype(MXU_DTYPE)
        do_ref[...] = do.astype(ACT_DTYPE)
        dm = _dot_nt(do, wo_ref[...].reshape(d, d))
        sa = _sigmoid(ga_ref[...].astype(F32))
        sb = _sigmoid(gb_ref[...].astype(F32))
        dya = (dm * sa).astype(MXU_DTYPE)
        dyb = (dm * sb).astype(MXU_DTYPE)
        dya_ref[...] = dya.astype(ACT_DTYPE)
        dyb_ref[...] = dyb.astype(ACT_DTYPE)
        dproj_ref[:, 0:d] = (dm * ya_ref[...].astype(F32) * sa * (1.0 - sa)).astype(ACT_DTYPE)
        dproj_ref[:, d:2 * d] = (dm * yb_ref[...].astype(F32) * sb * (1.0 - sb)).astype(ACT_DTYPE)
        dyai_ref[...] = _dot_nt(dya, wpa_ref[...].reshape(d, d)).astype(ACT_DTYPE)
        dybi_ref[...] = _dot_nt(dyb, wpb_ref[...].reshape(d, d)).astype(ACT_DTYPE)

    tile = pl.BlockSpec((tm, d), lambda i: (i, 0))
    wspec = pl.BlockSpec((N_CHIP, rq, d), lambda i: (0, 0, 0))
    return pl.pallas_call(
        body, name=f"merge_bwd_{l}",
        out_shape=[jax.ShapeDtypeStruct((n, d), ACT_DTYPE)] * 5
        + [jax.ShapeDtypeStruct((n, 6 * d), ACT_DTYPE), jax.ShapeDtypeStruct((nb, 8, d), F32)],
        grid=(n // tm,),
        in_specs=[tile, pl.BlockSpec((None, None, 8, d), lambda i: (l, i // tpb, 0, 0)), tile, tile, tile,
                  pl.BlockSpec((tm, d), lambda i: (i, 4)), pl.BlockSpec((tm, d), lambda i: (i, 5)),
                  wspec, wspec, wspec, ANY_SPEC],
        out_specs=[tile] * 5 + [pl.BlockSpec((tm, 2 * d), lambda i: (i, 2)),
                                pl.BlockSpec((None, 8, d), lambda i: (i // tpb, 0, 0))],
        compiler_params=_params(("arbitrary",)),
    )(dx1, mod, o, ya, yb, proj, proj, wg_pa, wg_pb, wg_out, after)


def _branches_bwd(l, proj, zc, dya_in, dyb_in, dproj, lng, lnb, ws, bst, cw, blg, blb, t_len, after):
    n = proj.shape[0]
    d = lng.shape[-1]
    tm = min(TOKEN_TILE, t_len)
    tpb = t_len // tm
    per = tm // HALO
    nchunk = tm // CHUNK
    ntile = n // tm

    def body(u_ref, v_ref, a_ref, g_ref, ah_ref, gh_ref, zc_ref, zcn_ref, dya_ref, dyb_ref, dybn_ref, dproj_in,
             lng_ref, lnb_ref, ws_ref, bst_ref, cw_ref, blg_ref, blb_ref, after_ref,
             dproj_ref, dws_ref, dbst_ref, dcw_ref, vec_ref, wm_s, zext, dzext, dz3, dvn_s):
        i = pl.program_id(0)

        @pl.when(i == 0)
        def _():
            dws_ref[...] = jnp.zeros_like(dws_ref)
            dbst_ref[...] = jnp.zeros_like(dbst_ref)
            dcw_ref[...] = jnp.zeros_like(dcw_ref)
            vec_ref[...] = jnp.zeros_like(vec_ref)

        _masked_ws(ws_ref, wm_s)
        _fill_z(i, tpb, ah_ref, gh_ref, zext)

        def conv_ln_bwd(zc_t, dyb_t):
            zhat, rstd = _ln_stats(zc_t)
            zn = zhat * blg_ref[...] + blb_ref[...]
            sg = _sigmoid(zn)
            dzn = dyb_t * (sg * (1.0 + zn * (1.0 - sg)))
            return _ln_bwd(dzn, zhat, rstd, blg_ref[...]), _colsum(dzn * zhat), _colsum(dzn)

        def chunk(c, carry):
            r0 = pl.multiple_of(c * CHUNK, CHUNK)
            rows = pl.ds(r0, CHUNK)
            vhat, rstd = _ln_stats(v_ref[rows, :].astype(F32))
            vn = (vhat * lng_ref[...] + lnb_ref[...]).astype(MXU_DTYPE)
            u = u_ref[rows, :].astype(F32)
            dya = dya_ref[rows, :].astype(F32)
            for h in range(HEADS):
                cols = slice(h * CHUNK, (h + 1) * CHUNK)
                s = jnp.dot(wm_s[h], vn[:, cols], preferred_element_type=F32) + bst_ref[:, h:h + 1]
                dproj_ref[rows, cols] = (dya[:, cols] * s).astype(ACT_DTYPE)
                ds = dya[:, cols] * u[:, cols]
                dvn_s[:, cols] = _dot_tn(wm_s[h], ds)
                dws_ref[h] += _dot_nt(ds, vn[:, cols])
                dbst_ref[:, h:h + 1] += jnp.sum(ds, axis=1, keepdims=True)
            dvn = dvn_s[...]
            dproj_ref[rows, d:2 * d] = _ln_bwd(dvn, vhat, rstd, lng_ref[...]).astype(ACT_DTYPE)
            vec_ref[0:1, :] += _colsum(dvn * vhat)
            vec_ref[1:2, :] += _colsum(dvn)
            a = a_ref[rows, :].astype(F32)
            g = g_ref[rows, :].astype(F32)
            _put_lanes(zext, pl.ds(HALO + r0, CHUNK), a * _sigmoid(g))
            dzc, dblg, dblb = conv_ln_bwd(zc_ref[rows, :].astype(F32), dyb_ref[rows, :].astype(F32))
            _put_lanes(dzext, rows, dzc)
            vec_ref[2:3, :] += _colsum(dzc)
            vec_ref[3:4, :] += dblg
            vec_ref[4:5, :] += dblb
            return carry

        lax.fori_loop(0, nchunk, chunk, 0)

        dzc_next, _, _ = conv_ln_bwd(zcn_ref[...].astype(F32), dybn_ref[...].astype(F32))
        _put_lanes(dzext, slice(tm, tm + HALO), dzc_next * jnp.where(i % tpb == tpb - 1, 0.0, 1.0))

        def lane_chunk_dz(lc, carry):
            _conv_taps(dzext, cw_ref, dz3, lc, tm, flip=True)
            return carry

        lax.fori_loop(0, d // LANES, lane_chunk_dz, 0)

        def lane_chunk(lc, carry):
            accs = [jnp.zeros((8, LANES), F32) for _ in range(CONV_TAPS)]
            for b in range(tm // TAP_GRAD_ROWS):
                dzc = dzext[lc, b * TAP_GRAD_ROWS:(b + 1) * TAP_GRAD_ROWS, :]
                for k, win in _tap_windows(zext, lc, b * TAP_GRAD_ROWS, TAP_GRAD_ROWS, flip=False):
                    prod = dzc * win
                    part = prod[0:8]
                    for e in range(1, TAP_GRAD_ROWS // 8):
                        part = part + prod[8 * e:8 * e + 8]
                    accs[k] = accs[k] + part
            for k in range(CONV_TAPS):
                dcw_ref[lc, k:k + 1, :] += _colsum(accs[k])
            return carry

        lax.fori_loop(0, d // LANES, lane_chunk, 0)

        def glu_bwd(c, carry):
            r0 = pl.multiple_of(c * CHUNK, CHUNK)
            rows = pl.ds(r0, CHUNK)
            for lc in range(d // LANES):
                lanes = slice(lc * LANES, (lc + 1) * LANES)
                dz = dz3[lc, rows, :]
                a = a_ref[rows, lanes].astype(F32)
                sg = _sigmoid(g_ref[rows, lanes].astype(F32))
                dproj_ref[rows, 2 * d + lc * LANES:2 * d + (lc + 1) * LANES] = (dz * sg).astype(ACT_DTYPE)
                dproj_ref[rows, 3 * d + lc * LANES:3 * d + (lc + 1) * LANES] = (
                    dz * a * sg * (1.0 - sg)).astype(ACT_DTYPE)
            return carry

        lax.fori_loop(0, nchunk, glu_bwd, 0)

        @pl.when(i == ntile - 1)
        def _():
            mask = _causal_mask()
            for h in range(HEADS):
                dws_ref[h] = jnp.where(mask, dws_ref[h], 0.0)

    col = lambda k: pl.BlockSpec((tm, d), lambda i: (i, k))
    tile = pl.BlockSpec((tm, d), lambda i: (i, 0))
    before = lambda k: pl.BlockSpec((HALO, d), lambda i: (jnp.maximum(i * per - 1, 0), k))
    following = pl.BlockSpec((HALO, d), lambda i: (jnp.minimum((i + 1) * per, n // HALO - 1), 0))
    vec = pl.BlockSpec((None, 1, d), lambda i: (l, 0, 0))
    const2 = lambda r, c: pl.BlockSpec((r, c), lambda i: (0, 0))
    return pl.pallas_call(
        body, name=f"branches_bwd_{l}",
        out_shape=[jax.ShapeDtypeStruct((n, 6 * d), ACT_DTYPE), jax.ShapeDtypeStruct((HEADS, CHUNK, CHUNK), F32),
                   jax.ShapeDtypeStruct((CHUNK, HEADS), F32), jax.ShapeDtypeStruct((d // LANES, HALO, LANES), F32),
                   jax.ShapeDtypeStruct((8, d), F32)],
        grid=(ntile,),
        in_specs=[col(0), col(1), col(2), col(3), before(2), before(3), tile, following, tile, tile, following,
                  pl.BlockSpec(memory_space=pl.ANY), vec, vec,
                  pl.BlockSpec((None, HEADS, CHUNK, CHUNK), lambda i: (l, 0, 0, 0)),
                  pl.BlockSpec((None, CHUNK, HEADS), lambda i: (l, 0, 0)),
                  pl.BlockSpec((None, d // LANES, HALO, LANES), lambda i: (l, 0, 0, 0)), vec, vec, ANY_SPEC],
        out_specs=[pl.BlockSpec((tm, 4 * d), lambda i: (i, 0)),
                   pl.BlockSpec((HEADS, CHUNK, CHUNK), lambda i: (0, 0, 0)),
                   const2(CHUNK, HEADS), pl.BlockSpec((d // LANES, HALO, LANES), lambda i: (0, 0, 0)), const2(8, d)],
        scratch_shapes=[pltpu.VMEM((HEADS, CHUNK, CHUNK), MXU_DTYPE),
                        pltpu.VMEM((d // LANES, HALO + tm, LANES), F32),
                        pltpu.VMEM((d // LANES, tm + HALO, LANES), F32),
                        pltpu.VMEM((d // LANES, tm, LANES), F32), pltpu.VMEM((CHUNK, d), F32)],
        input_output_aliases={11: 0},
        compiler_params=_params(("arbitrary",)),
    )(proj, proj, proj, proj, proj, proj, zc, zc, dya_in, dyb_in, dyb_in, dproj, lng, lnb, ws, bst, cw, blg, blb, after)


def _in_proj_bwd(l, dproj, dx1, x, mod, g1, wg_in, t_len, nb, after):
    n, d = x.shape
    tm = min(TOKEN_TILE, t_len)
    tpb = t_len // tm
    qc = wg_in.shape[-1]

    def body(dp_ref, dx1_ref, x_ref, mod_ref, g_ref, w_ref, after_ref, dx_ref, dmod_ref, dg_ref):
        i = pl.program_id(0)

        @pl.when(i == 0)
        def _():
            dg_ref[...] = jnp.zeros_like(dg_ref)

        @pl.when(i % tpb == 0)
        def _():
            dmod_ref[...] = jnp.zeros_like(dmod_ref)

        dh = jnp.zeros((tm, d), F32)
        for q in range(N_CHIP):
            dh = dh + _dot_nt(dp_ref[:, q * qc:(q + 1) * qc], w_ref[q])
        _norm_tail(x_ref, dx1_ref, dh, g_ref, mod_ref[1:2, :], dx_ref, dmod_ref, dg_ref, 0, 1)

    tile = pl.BlockSpec((tm, d), lambda i: (i, 0))
    return pl.pallas_call(
        body, name=f"in_proj_bwd_{l}",
        out_shape=[jax.ShapeDtypeStruct((n, d), F32), jax.ShapeDtypeStruct((nb, 8, d), F32),
                   jax.ShapeDtypeStruct((8, d), F32)],
        grid=(n // tm,),
        in_specs=[pl.BlockSpec((tm, N_CHIP * qc), lambda i: (i, 0)), tile, tile,
                  pl.BlockSpec((None, None, 8, d), lambda i: (l, i // tpb, 0, 0)),
                  pl.BlockSpec((None, 1, d), lambda i: (l, 0, 0)),
                  _resident((N_CHIP, d, qc)), ANY_SPEC],
        out_specs=[tile, pl.BlockSpec((None, 8, d), lambda i: (i // tpb, 0, 0)),
                   pl.BlockSpec((8, d), lambda i: (0, 0))],
        compiler_params=_params(("arbitrary",)),
    )(dproj, dx1, x, mod, g1, wg_in, after)


def _weight_grad(name, a, b, a_spec, b_spec, out_rows, out_spec, acc_shape, grid_ij, relu2=False):
    n = a.shape[0]
    tk = min(MATMUL_TILE, n)
    nk = n // tk
    cols = acc_shape[1]

    def body(a_ref, b_ref, o_ref, acc):
        k = pl.program_id(2)

        @pl.when(k == 0)
        def _():
            acc[...] = jnp.zeros_like(acc)

        a_t = a_ref[...]
        if relu2:
            a_t = jnp.square(jnp.maximum(a_t.astype(F32), 0.0))
        acc[...] += _dot_tn(a_t, b_ref[...])

        @pl.when(k == nk - 1)
        def _():
            o_ref[...] = acc[...].reshape(o_ref.shape).astype(WIRE_DTYPE)

    gi, gj = grid_ij
    return pl.pallas_call(
        body, name=name, out_shape=jax.ShapeDtypeStruct((N_CHIP, out_rows, cols), WIRE_DTYPE),
        grid=(gi, gj, nk),
        in_specs=[a_spec(tk), b_spec(tk)],
        out_specs=out_spec,
        scratch_shapes=[pltpu.VMEM(acc_shape, F32)],
        compiler_params=_params(("arbitrary", "arbitrary", "arbitrary")),
    )(a, b)


def _row_tile(rows, cols, arrays):
    budget = VMEM_LIMIT // 3
    t = budget // (arrays * 2 * cols * 4)
    t = max(8, min(rows, t // 8 * 8))
    while rows % t:
        t -= 8
    return t


def _sum_partials(name, own, got, myq, l, nl, prev):
    _, rows, cols = own.shape
    tr = _row_tile(rows, cols, 3)
    nt = rows // tr

    def body(q_ref, own_ref, got_ref, *rest):
        o_ref = rest[-1]
        acc = own_ref[...].astype(F32)
        for k in range(3):
            acc = acc + got_ref[k].astype(F32)
        o_ref[...] = acc

    operands = [myq, own, got] + ([] if prev is None else [prev])
    return pl.pallas_call(
        body, name=name, out_shape=jax.ShapeDtypeStruct((nl * rows, cols), F32),
        grid_spec=pltpu.PrefetchScalarGridSpec(
            num_scalar_prefetch=1, grid=(nt,),
            in_specs=[pl.BlockSpec((None, tr, cols), lambda i, q: (q[0], i, 0)),
                      pl.BlockSpec((3, tr, cols), lambda i, q: (0, i, 0))]
            + ([] if prev is None else [pl.BlockSpec(memory_space=pl.ANY)]),
            out_specs=pl.BlockSpec((tr, cols), lambda i, q: (l * nt + i, 0))),
        input_output_aliases={} if prev is None else {3: 0},
        compiler_params=_params(("arbitrary",)),
    )(*operands)


def _adamw(name, w, m, v, g_a, g_b=None):
    rows, cols = w.shape
    tr = _row_tile(rows, cols, 9)
    c1 = 1.0 - ADAM_B1 ** ADAM_STEP
    c2 = 1.0 - ADAM_B2 ** ADAM_STEP

    def body(*refs):
        if g_b is None:
            w_ref, m_ref, v_ref, ga_ref, g_ref, d_ref, m2_ref, v2_ref = refs
            g = ga_ref[...]
        else:
            w_ref, m_ref, v_ref, ga_ref, gb_ref, g_ref, d_ref, m2_ref, v2_ref = refs
            g = ga_ref[...] + gb_ref[...]
        m2 = ADAM_B1 * m_ref[...] + (1.0 - ADAM_B1) * g
        v2 = ADAM_B2 * v_ref[...] + (1.0 - ADAM_B2) * (g * g)
        g_ref[...] = g
        m2_ref[...] = m2
        v2_ref[...] = v2
        d_ref[...] = -ADAM_LR * ((m2 / c1) / (jnp.sqrt(v2 / c2) + ADAM_EPS) + ADAM_WD * w_ref[...])

    tile = pl.BlockSpec((tr, cols), lambda i: (i, 0))
    operands = [w, m, v, g_a] + ([] if g_b is None else [g_b])
    return pl.pallas_call(
        body, name=name, out_shape=[jax.ShapeDtypeStruct((rows, cols), F32)] * 4,
        grid=(rows // tr,), in_specs=[tile] * len(operands), out_specs=[tile] * 4,
        compiler_params=_params(("arbitrary",)),
    )(*operands)


def _pack(parts):
    flat = [p.reshape(-1, LANES) for p in parts]
    for f in flat:
        assert f.shape[0] % 8 == 0
    return jnp.concatenate(flat, axis=0)


def _unpack(packed, shapes):
    out, r = [], 0
    for s in shapes:
        size = 1
        for e in s:
            size *= e
        rows = size // LANES
        out.append(packed[r:r + rows].reshape(s))
        r += rows
    return out


def kernel(x, c, w_ada, b_ada, norm1_g, w_in, a_ln_g, a_ln_b, a_ws, a_bs, w_pa, b_conv_w, b_conv_b, b_ln_g, b_ln_b, w_pb, w_out, norm2_g, w_ff1, w_ff2, final_g, loss_target, m_w_ada, m_b_ada, m_norm1_g, m_w_in, m_a_ln_g, m_a_ln_b, m_a_ws, m_a_bs, m_w_pa, m_b_conv_w, m_b_conv_b, m_b_ln_g, m_b_ln_b, m_w_pb, m_w_out, m_norm2_g, m_w_ff1, m_w_ff2, m_final_g, v_w_ada, v_b_ada, v_norm1_g, v_w_in, v_a_ln_g, v_a_ln_b, v_a_ws, v_a_bs, v_w_pa, v_b_conv_w, v_b_conv_b, v_b_ln_g, v_b_ln_b, v_w_pb, v_w_out, v_norm2_g, v_w_ff1, v_w_ff2, v_final_g):
    nb, t_len, d = x.shape
    nl = w_in.shape[0]
    n = nb * t_len
    cq = w_ada.shape[-1]
    cc = d // N_CHIP
    mx, my, mc = _my_place()
    myq = (2 * mx + my).astype(jnp.int32).reshape(1)
    me = 4 * mx + 2 * my + mc

    def exchange_start(name, slots, after):
        send_sems, recv_sems, srcs, lands, tok_out = _split_start(
            name + "_start", "exchange", [slots], [lax.empty(slots.shape, slots.dtype)], after)
        return (name, send_sems, recv_sems, srcs, lands), tok_out

    def exchange_wait(handle, after):
        name, send_sems, recv_sems, srcs, lands = handle
        srcs, lands = _split_wait(name + "_wait", "exchange", send_sems, recv_sems, srcs, lands, after)
        own = lax.dynamic_slice_in_dim(srcs[0], me, 1, axis=0)
        return lax.dynamic_update_slice_in_dim(lands[0], own, me, axis=0)

    taps = jnp.pad(b_conv_w.reshape(nl, CONV_TAPS, cc), ((0, 0), (0, HALO - CONV_TAPS), (0, 0)))
    first = jnp.concatenate([jnp.pad(c, ((0, 8 - nb), (0, 0))), taps.reshape(nl * HALO * cc // d, d)], axis=0)
    first = _all_to_all(jnp.broadcast_to(first[None], (N_DEV,) + first.shape), "gather_c_and_taps")
    c_all = first[:, :nb].reshape(N_DEV * nb, d)
    cwg = first[:, 8:].reshape(N_CHIP, 2, nl, HALO, cc)[:, 0]
    cw = cwg.transpose(1, 2, 0, 3).reshape(nl, HALO, d)
    cw = cw.reshape(nl, HALO, d // LANES, LANES).transpose(0, 2, 1, 3)
    mod_part = _ada_forward(c_all, w_ada, b_ada.reshape(nl, 1, N_CHIP * cq), myq)
    mod_slots = mod_part.reshape(nl, N_DEV, nb, cq).transpose(1, 0, 2, 3).reshape(N_DEV, nl * nb, cq)
    mod_got = _all_to_all(mod_slots, "exchange_mod").reshape(N_CHIP, 2, nl, nb, cq)[:, 0]
    mod6 = mod_got.transpose(1, 2, 0, 3).reshape(nl, nb, 6, d)
    mod = jnp.pad(mod6, ((0, 0), (0, 0), (0, 2), (0, 0)))

    big = ["w_in", "w_pa", "w_pb", "w_out", "w_ff1", "w_ff2"]
    ws_given = dict(w_in=(w_in, m_w_in, v_w_in), w_pa=(w_pa, m_w_pa, v_w_pa), w_pb=(w_pb, m_w_pb, v_w_pb),
                    w_out=(w_out, m_w_out, v_w_out), w_ff1=(w_ff1, m_w_ff1, v_w_ff1), w_ff2=(w_ff2, m_w_ff2, v_w_ff2))

    def own_slot(w_l, after=None):
        if after is not None:
            w_l = w_l - after[0, 0]
        empty = lax.empty((N_CHIP,) + w_l.shape, WIRE_DTYPE)
        return lax.dynamic_update_index_in_dim(empty, w_l.astype(WIRE_DTYPE), myq[0], 0)

    def zero_after(*arrays):
        z = jnp.zeros((8, LANES), F32)
        for a in arrays:
            piece = a.reshape(-1, a.shape[-1])[:8, :LANES]
            z = z + jnp.where(jnp.isfinite(piece), piece, 0.0) * 0.0
        return z

    first_sems = _split_start("gather_start_in_0", "gather", [], [own_slot(w_in[0])], zero_after(cw, mod[:, 0]))
    token = first = first_sems[4]
    gathers = []
    for l in range(nl):
        group = big[1:] if l == 0 else big
        send_sems, recv_sems, _, lands, token = _split_start(
            f"gather_start_{l}", "gather", [], [own_slot(ws_given[k][0][l], first) for k in group], token)
        if l == 0:
            send_sems = list(first_sems[0]) + list(send_sems)
            recv_sems = list(first_sems[1]) + list(recv_sems)
            lands = list(first_sems[3]) + list(lands)
        gathers.append((send_sems, recv_sems, lands))
    mod = mod + token[0, 0]

    def gather_wait(l, part, lo, hi, after):
        send_sems, recv_sems, lands = gathers[l]
        return _split_wait(f"gather_wait_{part}_{l}", "gather", send_sems[lo:hi], recv_sems[lo:hi], [],
                           lands[lo:hi], after)[1]

    vec3 = lambda p: p.reshape(nl, 1, d)
    g1, g2 = vec3(norm1_g), vec3(norm2_g)
    lng, lnb, cb, blg, blb = vec3(a_ln_g), vec3(a_ln_b), vec3(b_conv_b), vec3(b_ln_g), vec3(b_ln_b)
    bst = a_bs.transpose(0, 2, 1)

    xs = x.reshape(n, d)
    saved = []
    weights = []
    for l in range(nl):
        if l == 0:
            send_sems, recv_sems, lands = gathers[0]
            wg_in = lands[0]
            h, proj = _in_proj_quarter("in_proj_0_own", l, xs, mod, g1, wg_in, myq, t_len)
            for k, (px, py) in enumerate(_other_chips(mx, my)):
                (wg_in,) = _split_wait(f"gather_wait_in_0_{k}", "gather", send_sems[:1], recv_sems[:1], [], [wg_in],
                                       proj, peers=(k,))[1]
                quarter = (2 * px + py).astype(jnp.int32).reshape(1)
                proj = _in_proj_quarter(f"in_proj_0_{k}", l, h, None, None, wg_in, quarter, t_len, proj)
        else:
            (wg_in,) = gather_wait(l, "in", 0, 1, xs)
            h, proj = _in_proj(l, xs, mod, g1, wg_in, t_len)
        ya_in, yb_in, zc = _branches_fwd(l, proj, lng, lnb, a_ws, bst, cw, cb, blg, blb, t_len)
        wg_pa, wg_pb, wg_out = gather_wait(l, "mid", 1, 4, ya_in)
        ya, yb, merged, o, x1 = _merge_out(l, xs, mod, proj, ya_in, yb_in, wg_pa, wg_pb, wg_out, t_len)
        wg_ff1, wg_ff2 = gather_wait(l, "ffn", 4, 6, x1)
        h2, f, o2, x2 = _ffn_fwd(l, x1, mod, g2, wg_ff1, wg_ff2, t_len)
        saved.append((xs, h, proj, ya_in, yb_in, zc, ya, yb, merged, o, x1, h2, f, o2))
        weights.append((wg_in, wg_pa, wg_pb, wg_out, wg_ff1, wg_ff2))
        xs = x2

    loss_blk, dx, dfinal = _loss_head(xs, final_g.reshape(1, d), loss_target.reshape(n, d))

    tok = lambda w: (lambda tk: pl.BlockSpec((tk, w), lambda i, j, k: (k, 0)))
    tok_i = lambda w: (lambda tk: pl.BlockSpec((tk, w), lambda i, j, k: (k, i)))
    tok_j = lambda w: (lambda tk: pl.BlockSpec((tk, w), lambda i, j, k: (k, j)))
    qin = weights[0][0].shape[-1]
    hq = weights[0][4].shape[-1]
    rq = d // N_CHIP
    slot_i = lambda r, cdim: pl.BlockSpec((None, r, cdim), lambda i, j, k: (i, 0, 0))
    slot_j = lambda r, cdim: pl.BlockSpec((None, r, cdim), lambda i, j, k: (j, 0, 0))
    all_slots = pl.BlockSpec((N_CHIP, rq, d), lambda i, j, k: (0, 0, 0))
    scatters = []

    def scatter_start(l, part, names, grads, after):
        lands = [lax.empty((3,) + g.shape[1:], g.dtype) for g in grads]
        send_sems, recv_sems, srcs, lands, tok_out = _split_start(f"scatter_start_{part}_{l}", "scatter", grads, lands,
                                                                  after)
        scatters.append((f"scatter_wait_{part}_{l}", l, names, send_sems, recv_sems, srcs, lands))
        return tok_out

    dmods, small = [None] * nl, [None] * nl
    for l in reversed(range(nl)):
        x0, h, proj, ya_in, yb_in, zc, ya, yb, merged, o, x1, h2, f, o2 = saved[l]
        wg_in, wg_pa, wg_pb, wg_out, wg_ff1, wg_ff2 = weights[l]
        do2, df, dx1, dmod_c, dg2 = _ffn_bwd(l, dx, x1, mod, g2, o2, f, wg_ff1, wg_ff2, t_len, nb)
        g_ff2 = _weight_grad(f"grad_w_ff2_{l}", f, do2, tok_i(hq), tok(d), hq, slot_i(hq, d), (hq, d), (N_CHIP, 1),
                             relu2=True)
        g_ff1 = _weight_grad(f"grad_w_ff1_{l}", h2, df, tok(d), tok_j(hq), d, slot_j(d, hq), (d, hq), (1, N_CHIP))
        if l == 0:
            token = scatter_start(l, "ffn", ["w_ff2", "w_ff1"], [g_ff2, g_ff1], token)
        do, dya, dyb, dya_in, dyb_in, dproj, dmod_b = _merge_bwd(l, dx1, mod, o, ya, yb, proj, wg_pa, wg_pb, wg_out,
                                                                 t_len, nb, token)
        g_out = _weight_grad(f"grad_w_out_{l}", merged, do, tok(d), tok(d), rq, all_slots, (d, d), (1, 1))
        g_pa = _weight_grad(f"grad_w_pa_{l}", ya_in, dya, tok(d), tok(d), rq, all_slots, (d, d), (1, 1))
        g_pb = _weight_grad(f"grad_w_pb_{l}", yb_in, dyb, tok(d), tok(d), rq, all_slots, (d, d), (1, 1))
        if l == 0:
            token = scatter_start(l, "mid", ["w_out", "w_pa", "w_pb"], [g_out, g_pa, g_pb], token)
        dproj, dws, dbst, dcw, vecs = _branches_bwd(l, proj, zc, dya_in, dyb_in, dproj, lng, lnb, a_ws, bst, cw,
                                                    blg, blb, t_len, token)
        g_in = _weight_grad(f"grad_w_in_{l}", h, dproj, tok(d), tok_j(qin), d, slot_j(d, qin), (d, qin), (1, N_CHIP))
        if l == 0:
            token = scatter_start(l, "in", ["w_in"], [g_in], token)
        else:
            token = scatter_start(l, "all", ["w_ff2", "w_ff1", "w_out", "w_pa", "w_pb", "w_in"],
                                  [g_ff2, g_ff1, g_out, g_pa, g_pb, g_in], token)
        dx, dmod_a, dg1 = _in_proj_bwd(l, dproj, dx1, x0, mod, g1, wg_in, t_len, nb, token)
        dmods[l] = jnp.concatenate([dmod_a[:, 0:2], dmod_b[:, 2:3], dmod_c[:, 3:6]], axis=1)
        dcw = dcw.transpose(1, 0, 2).reshape(HALO, d)[:CONV_TAPS]
        small[l] = (dg1[0], vecs[0], vecs[1], dws, dbst.T, dcw, vecs[2], vecs[3], vecs[4], dg2[0])
    grad_x = dx.reshape(nb, t_len, d)

    names = ["norm1_g", "a_ln_g", "a_ln_b", "a_ws", "a_bs", "b_conv_w", "b_conv_b", "b_ln_g", "b_ln_b", "norm2_g"]
    stacked = [jnp.stack([small[l][k] for l in range(nl)]) for k in range(len(names))]
    stacked[5] = jnp.pad(stacked[5], ((0, 0), (0, HALO - CONV_TAPS), (0, 0)))
    stacked += [dfinal, loss_blk]
    part_shapes = [s.shape for s in stacked]
    packed = _pack(stacked)
    prow = packed.shape[0]
    pad_rows = (-prow) % (8 * N_DEV)
    packed = jnp.pad(packed, ((0, pad_rows), (0, 0)))
    srow = packed.shape[0] // N_DEV
    half = dict.fromkeys(big)

    def sum_arrived(entries, after):
        last = after
        for name, l, group, send_sems, recv_sems, srcs, lands in entries:
            srcs, lands = _split_wait(name, "scatter", send_sems, recv_sems, srcs, lands, after)
            for k, g_own, g_got in zip(group, srcs, lands):
                last = half[k] = _sum_partials(f"sum_{k}_{l}", g_own, g_got, myq, l, nl, half[k])
        return last

    early = max(1, (nl - 1) * 2 // 3)
    reduce_handle, token = exchange_start("reduce_small", packed.reshape(N_DEV, srow, LANES), token)
    last = sum_arrived(scatters[:early], token)
    mine = _sum_slots(exchange_wait(reduce_handle, last), "sum_small")
    dmod_rows = nl * nb * 6 * d // LANES
    second = jnp.concatenate([mine, jnp.stack(dmods).reshape(dmod_rows, LANES)], axis=0)
    gather_handle, token = exchange_start("gather_small_and_dmod",
                                          jnp.broadcast_to(second[None], (N_DEV,) + second.shape), token)
    last = sum_arrived(scatters[early:], token)
    sums = [half[k] for k in big]
    swap_send, swap_recv, sums, others, token = _split_start(
        "swap_start", "swap", sums, [lax.empty(s.shape, s.dtype) for s in sums], last)
    second = exchange_wait(gather_handle, token)
    total = second[:, :srow].reshape(N_DEV * srow, LANES)[:prow]
    dmod_all = second[:, srow:].reshape(N_DEV, nl, nb, 6 * d).transpose(1, 0, 2, 3).reshape(nl, N_DEV * nb, 6 * d)
    g_w_ada, g_b_ada = _ada_backward(c_all, dmod_all, myq, cq, token)

    sg = dict(zip(names + ["final_g", "loss"], _unpack(total, part_shapes)))
    loss = sg["loss"][0, 0]
    sg["b_conv_w"] = lax.dynamic_slice_in_dim(sg["b_conv_w"][:, :CONV_TAPS], myq[0] * cc, cc, axis=2).reshape(
        nl, CONV_TAPS, 1, cc)
    sg["final_g"] = sg["final_g"][0]
    sg["b_ada"] = g_b_ada.reshape(nl, N_CHIP * cq)
    small_names = ["b_ada", "norm1_g", "a_ln_g", "a_ln_b", "a_ws", "a_bs", "b_conv_w", "b_conv_b", "b_ln_g",
                   "b_ln_b", "norm2_g", "final_g"]
    given = dict(b_ada=(b_ada, m_b_ada, v_b_ada), norm1_g=(norm1_g, m_norm1_g, v_norm1_g),
                 a_ln_g=(a_ln_g, m_a_ln_g, v_a_ln_g), a_ln_b=(a_ln_b, m_a_ln_b, v_a_ln_b),
                 a_ws=(a_ws, m_a_ws, v_a_ws), a_bs=(a_bs, m_a_bs, v_a_bs),
                 b_conv_w=(b_conv_w, m_b_conv_w, v_b_conv_w), b_conv_b=(b_conv_b, m_b_conv_b, v_b_conv_b),
                 b_ln_g=(b_ln_g, m_b_ln_g, v_b_ln_g), b_ln_b=(b_ln_b, m_b_ln_b, v_b_ln_b),
                 norm2_g=(norm2_g, m_norm2_g, v_norm2_g), final_g=(final_g, m_final_g, v_final_g))

    def padded(a):
        rows = -(-a.size // (8 * LANES)) * 8
        return jnp.pad(a.reshape(-1), (0, rows * LANES - a.size)).reshape(rows, LANES)

    packs = [_pack([padded(given[k][j]) for k in small_names]) for j in range(3)]
    gpack = _pack([padded(sg[k].astype(F32)) for k in small_names])
    res_small = _adamw("adamw_small", packs[0], packs[1], packs[2], gpack)
    out = {}
    for j, kind in enumerate(["grad", "delta", "new_m", "new_v"]):
        r = 0
        for k in small_names:
            a = given[k][0]
            rows = -(-a.size // (8 * LANES)) * 8
            out[(kind, k)] = res_small[j][r:r + rows].reshape(-1)[:a.size].reshape(a.shape)
            r += rows

    res = _adamw("adamw_w_ada", w_ada.reshape(nl * d, cq), m_w_ada.reshape(nl * d, cq), v_w_ada.reshape(nl * d, cq),
                 g_w_ada.reshape(nl * d, cq))
    for kind, r in zip(["grad", "delta", "new_m", "new_v"], res):
        out[(kind, "w_ada")] = r.reshape(w_ada.shape)

    sums, others = _split_wait("swap_wait", "swap", swap_send, swap_recv, sums, others, res[0])
    for k, s_mine, s_other in zip(big, sums, others):
        w, m, v = ws_given[k]
        cols = w.shape[-1]
        res = _adamw(f"adamw_{k}", w.reshape(-1, cols), m.reshape(-1, cols), v.reshape(-1, cols), s_mine, s_other)
        for kind, r in zip(["grad", "delta", "new_m", "new_v"], res):
            out[(kind, k)] = r.reshape(w.shape)

    order = ["w_ada", "b_ada", "norm1_g", "w_in", "a_ln_g", "a_ln_b", "a_ws", "a_bs", "w_pa", "b_conv_w", "b_conv_b",
             "b_ln_g", "b_ln_b", "w_pb", "w_out", "norm2_g", "w_ff1", "w_ff2", "final_g"]
    return (loss, grad_x, *[out[("grad", k)] for k in order], *[out[("delta", k)] for k in order],
            *[out[("new_m", k)] for k in order], *[out[("new_v", k)] for k in order])
```

```python
import jax
import jax.numpy as jnp
from jax import lax
from jax.experimental import pallas as pl
from jax.experimental.pallas import tpu as pltpu

F32 = jnp.float32
MXU_DTYPE = jnp.bfloat16
ACT_DTYPE = jnp.bfloat16
WIRE_DTYPE = jnp.bfloat16

EPS = 1e-6
CHUNK = 128
HEADS = 8
CONV_TAPS = 31
HALO = 32
N_DEV = 8
N_CHIP = 4
ADAM_LR, ADAM_B1, ADAM_B2, ADAM_EPS, ADAM_WD, ADAM_STEP = 0.001, 0.9, 0.999, 1e-08, 0.01, 10

V7X_VMEM_BYTES = 64 * 1024 * 1024
VMEM_LIMIT = V7X_VMEM_BYTES * 7 // 8
TOKEN_TILE = 512
MATMUL_TILE = 2048
FFN_BWD_TILE = 512
CONV_ROWS = 64
TAP_GRAD_ROWS = 32
LANES = 128
MESH_ID = pl.DeviceIdType.MESH


def _params(sem=None):
    return pltpu.CompilerParams(dimension_semantics=sem, vmem_limit_bytes=VMEM_LIMIT)


def _resident(shape):
    return pl.BlockSpec(shape, lambda *_: (0,) * len(shape), pipeline_mode=pl.Buffered(1))


def _dot(a, b):
    return jnp.dot(a.astype(MXU_DTYPE), b.astype(MXU_DTYPE), preferred_element_type=F32)


def _dot_nt(a, b):
    return lax.dot_general(a.astype(MXU_DTYPE), b.astype(MXU_DTYPE), (((1,), (1,)), ((), ())),
                           preferred_element_type=F32)


def _dot_tn(a, b):
    return lax.dot_general(a.astype(MXU_DTYPE), b.astype(MXU_DTYPE), (((0,), (0,)), ((), ())),
                           preferred_element_type=F32)


def _colsum(a):
    return jnp.sum(a, axis=0, keepdims=True)


def _rowmean(a):
    return jnp.mean(a, axis=-1, keepdims=True)


def _sigmoid(a):
    return 1.0 / (1.0 + jnp.exp(-a))


def _modnorm_fwd(x, g, sc, sh):
    r = lax.rsqrt(_rowmean(x * x) + EPS)
    return (x * r) * (g * (1.0 + sc)) + sh


def _modnorm_bwd(x, dh, g, sc):
    r = lax.rsqrt(_rowmean(x * x) + EPS)
    xn = x * r
    dxn = dh * (g * (1.0 + sc))
    dx = r * (dxn - xn * _rowmean(dxn * xn))
    return dx, _colsum(dh), _colsum(dh * xn)


def _ln_stats(v):
    mu = _rowmean(v)
    vc = v - mu
    rstd = lax.rsqrt(_rowmean(vc * vc) + EPS)
    return vc * rstd, rstd


def _ln_bwd(dy, vhat, rstd, g):
    dvh = dy * g
    return rstd * (dvh - _rowmean(dvh) - vhat * _rowmean(dvh * vhat))


def _causal_mask():
    row = lax.broadcasted_iota(jnp.int32, (CHUNK, CHUNK), 0)
    col = lax.broadcasted_iota(jnp.int32, (CHUNK, CHUNK), 1)
    return row >= col


def _my_place():
    return lax.axis_index("x"), lax.axis_index("y"), lax.axis_index("c")


def _other_chips(mx, my):
    return [(1 - mx, my), (mx, 1 - my), (1 - mx, 1 - my)]


def _other_devices(mx, my, mc):
    return [((mx + ((k >> 2) & 1)) % 2, (my + ((k >> 1) & 1)) % 2, (mc + (k & 1)) % 2) for k in range(1, N_DEV)]


def _all_to_all(x, name):
    assert x.shape[0] == N_DEV

    def body(x_ref, o_ref, send_sems, recv_sems):
        mx, my, mc = _my_place()
        me = 4 * mx + 2 * my + mc
        o_ref[me] = x_ref[me]
        copies = []
        for k, (px, py, pc) in enumerate(_other_devices(mx, my, mc)):
            cp = pltpu.make_async_remote_copy(
                src_ref=x_ref.at[4 * px + 2 * py + pc], dst_ref=o_ref.at[me],
                send_sem=send_sems.at[k], recv_sem=recv_sems.at[k],
                device_id=(px, py, pc), device_id_type=MESH_ID)
            cp.start()
            copies.append(cp)
        for cp in copies:
            cp.wait()

    return pl.pallas_call(
        body, name=name, out_shape=jax.ShapeDtypeStruct(x.shape, x.dtype),
        in_specs=[pl.BlockSpec(memory_space=pltpu.VMEM)],
        out_specs=pl.BlockSpec(memory_space=pltpu.VMEM),
        scratch_shapes=[pltpu.SemaphoreType.DMA((N_DEV - 1,)), pltpu.SemaphoreType.DMA((N_DEV - 1,))],
        compiler_params=pltpu.CompilerParams(vmem_limit_bytes=VMEM_LIMIT),
    )(x)


def _sum_slots(x, name):
    def body(x_ref, o_ref):
        acc = x_ref[0]
        for s in range(1, N_DEV):
            acc = acc + x_ref[s]
        o_ref[...] = acc

    return pl.pallas_call(
        body, name=name, out_shape=jax.ShapeDtypeStruct(x.shape[1:], x.dtype),
        in_specs=[pl.BlockSpec(memory_space=pltpu.VMEM)], out_specs=pl.BlockSpec(memory_space=pltpu.VMEM),
        compiler_params=pltpu.CompilerParams(vmem_limit_bytes=VMEM_LIMIT),
    )(x)


HBM_SPEC = pl.BlockSpec(memory_space=pltpu.HBM)
SEM_SPEC = pl.BlockSpec(memory_space=pltpu.SEMAPHORE)
ANY_SPEC = pl.BlockSpec(memory_space=pl.ANY)
SPLIT_EFFECT = pltpu.SideEffectType.DATAFLOW_SIDE_EFFECTING


def _quarter_copies(mode, srcs, lands, send_sems, recv_sems, peers=(0, 1, 2)):
    mx, my, mc = _my_place()
    myq = 2 * mx + my
    if mode == "swap":
        return [pltpu.make_async_remote_copy(
            src_ref=srcs[a], dst_ref=lands[a], send_sem=send_sems[a].at[0], recv_sem=recv_sems[a].at[0],
            device_id=(mx, my, 1 - mc), device_id_type=MESH_ID) for a in range(len(lands))]
    copies = []
    if mode in ("exchange", "allgather"):
        me = 4 * mx + 2 * my + mc
        for a in range(len(lands)):
            for k, (px, py, pc) in enumerate(_other_devices(mx, my, mc)):
                copies.append(pltpu.make_async_remote_copy(
                    src_ref=srcs[a].at[4 * px + 2 * py + pc] if mode == "exchange" else srcs[a],
                    dst_ref=lands[a].at[me],
                    send_sem=send_sems[a].at[k], recv_sem=recv_sems[a].at[k],
                    device_id=(px, py, pc), device_id_type=MESH_ID))
        return copies
    for a in range(len(lands)):
        for k, (px, py) in enumerate(_other_chips(mx, my)):
            if k not in peers:
                continue
            if mode == "gather":
                src, dst = lands[a].at[myq], lands[a].at[myq]
            else:
                src, dst = srcs[a].at[2 * px + py], lands[a].at[k]
            copies.append(pltpu.make_async_remote_copy(
                src_ref=src, dst_ref=dst, send_sem=send_sems[a].at[k], recv_sem=recv_sems[a].at[k],
                device_id=(px, py, mc), device_id_type=MESH_ID))
    return copies


def _split_start(name, mode, srcs, lands, after):
    ns, n = len(srcs), len(lands)

    def body(*refs):
        outs = refs[ns + n + 1:]
        for cp in _quarter_copies(mode, refs[:ns], refs[ns:ns + n], outs[:n], outs[n:2 * n]):
            cp.start()
        token = outs[-1]
        token[...] = jnp.zeros_like(token)

    arrays = list(srcs) + list(lands)
    per_array = {"swap": 1, "exchange": N_DEV - 1, "allgather": N_DEV - 1}.get(mode, 3)
    res = pl.pallas_call(
        body, name=name,
        out_shape=[pltpu.SemaphoreType.DMA((per_array,))] * (2 * n) + [pltpu.HBM(x.shape, x.dtype) for x in arrays]
        + [jax.ShapeDtypeStruct((8, LANES), F32)],
        in_specs=[HBM_SPEC] * (ns + n) + [ANY_SPEC],
        out_specs=[SEM_SPEC] * (2 * n) + [HBM_SPEC] * (ns + n) + [pl.BlockSpec(memory_space=pltpu.VMEM)],
        input_output_aliases={i: 2 * n + i for i in range(ns + n)},
        compiler_params=pltpu.CompilerParams(has_side_effects=SPLIT_EFFECT),
    )(*[pltpu.with_memory_space_constraint(x, pltpu.HBM) for x in arrays], after)
    return res[:n], res[n:2 * n], res[2 * n:2 * n + ns], res[2 * n + ns:2 * n + ns + n], res[-1]


def _split_wait(name, mode, send_sems, recv_sems, srcs, lands, after, peers=(0, 1, 2)):
    ns, n = len(srcs), len(lands)
    afters = list(after) if isinstance(after, (list, tuple)) else [after]

    def body(*refs):
        sems = refs[ns + n:ns + 3 * n]
        for cp in _quarter_copies(mode, refs[:ns], refs[ns:ns + n], sems[:n], sems[n:], peers):
            cp.wait_send()
            cp.wait_recv()

    arrays = list(srcs) + list(lands)
    res = pl.pallas_call(
        body, name=name,
        out_shape=[pltpu.HBM(x.shape, x.dtype) for x in arrays],
        in_specs=[HBM_SPEC] * (ns + n) + [SEM_SPEC] * (2 * n) + [ANY_SPEC] * len(afters),
        out_specs=[HBM_SPEC] * (ns + n),
        input_output_aliases={i: i for i in range(ns + n)},
        compiler_params=pltpu.CompilerParams(has_side_effects=SPLIT_EFFECT),
    )(*arrays, *send_sems, *recv_sems, *afters)
    return res[:ns], res[ns:]


def _ada_forward(c_all, w_ada, b_ada3, myq):
    nl, d, cq = w_ada.shape
    nb = c_all.shape[0]

    def body(q_ref, c_ref, w_ref, b_ref, o_ref):
        c = c_ref[...]
        act = c * _sigmoid(c)
        o_ref[...] = _dot(act, w_ref[...]) + b_ref[...]

    return pl.pallas_call(
        body, name="ada_forward",
        out_shape=jax.ShapeDtypeStruct((nl, nb, cq), F32),
        grid_spec=pltpu.PrefetchScalarGridSpec(
            num_scalar_prefetch=1, grid=(nl,),
            in_specs=[pl.BlockSpec((nb, d), lambda l, q: (0, 0)),
                      pl.BlockSpec((None, d, cq), lambda l, q: (l, 0, 0)),
                      pl.BlockSpec((None, 1, cq), lambda l, q: (l, 0, q[0]))],
            out_specs=pl.BlockSpec((None, nb, cq), lambda l, q: (l, 0, 0))),
        compiler_params=_params(("arbitrary",)),
    )(myq, c_all, w_ada, b_ada3)


def _ada_backward(c_all, dmod_all, myq, cq, after):
    nb, d = c_all.shape
    nl = dmod_all.shape[0]
    full = dmod_all.shape[2]

    def body(q_ref, c_ref, dq_ref, dall_ref, after_ref, gw_ref, gb_ref):
        c = c_ref[...]
        act = c * _sigmoid(c)
        gw_ref[...] = _dot_tn(act, dq_ref[...])
        gb_ref[...] = _colsum(dall_ref[...])

    return pl.pallas_call(
        body, name="ada_backward",
        out_shape=[jax.ShapeDtypeStruct((nl, d, cq), F32), jax.ShapeDtypeStruct((nl, 1, full), F32)],
        grid_spec=pltpu.PrefetchScalarGridSpec(
            num_scalar_prefetch=1, grid=(nl,),
            in_specs=[pl.BlockSpec((nb, d), lambda l, q: (0, 0)),
                      pl.BlockSpec((None, nb, cq), lambda l, q: (l, 0, q[0])),
                      pl.BlockSpec((None, nb, full), lambda l, q: (l, 0, 0)), ANY_SPEC],
            out_specs=[pl.BlockSpec((None, d, cq), lambda l, q: (l, 0, 0)),
                       pl.BlockSpec((None, 1, full), lambda l, q: (l, 0, 0))]),
        compiler_params=_params(("arbitrary",)),
    )(myq, c_all, dmod_all, dmod_all, after)


def _in_proj(l, x, mod, g1, wg_in, t_len):
    n, d = x.shape
    tm = min(TOKEN_TILE, t_len)
    tpb = t_len // tm
    qc = wg_in.shape[-1]

    def body(x_ref, mod_ref, g_ref, w_ref, h_ref, proj_ref):
        h = _modnorm_fwd(x_ref[...], g_ref[...], mod_ref[1:2, :], mod_ref[0:1, :]).astype(MXU_DTYPE)
        h_ref[...] = h.astype(ACT_DTYPE)
        for q in range(N_CHIP):
            proj_ref[:, q * qc:(q + 1) * qc] = jnp.dot(h, w_ref[q], preferred_element_type=F32).astype(ACT_DTYPE)

    return pl.pallas_call(
        body, name=f"in_proj_{l}",
        out_shape=[jax.ShapeDtypeStruct((n, d), ACT_DTYPE), jax.ShapeDtypeStruct((n, N_CHIP * qc), ACT_DTYPE)],
        grid=(n // tm,),
        in_specs=[pl.BlockSpec((tm, d), lambda i: (i, 0)),
                  pl.BlockSpec((None, None, 8, d), lambda i: (l, i // tpb, 0, 0)),
                  pl.BlockSpec((None, 1, d), lambda i: (l, 0, 0)),
                  _resident((N_CHIP, d, qc))],
        out_specs=[pl.BlockSpec((tm, d), lambda i: (i, 0)),
                   pl.BlockSpec((tm, N_CHIP * qc), lambda i: (i, 0))],
        compiler_params=_params(("arbitrary",)),
    )(x, mod, g1, wg_in)


def _in_proj_quarter(name, l, src, mod, g1, wg_in, quarter, t_len, prev=None):
    n, d = src.shape
    tm = min(TOKEN_TILE, t_len)
    tpb = t_len // tm
    qc = wg_in.shape[-1]
    first = prev is None

    def body(q_ref, *refs):
        if first:
            x_ref, mod_ref, g_ref, w_ref, h_ref, proj_ref = refs
            h = _modnorm_fwd(x_ref[...], g_ref[...], mod_ref[1:2, :], mod_ref[0:1, :]).astype(MXU_DTYPE)
            h_ref[...] = h.astype(ACT_DTYPE)
        else:
            h_in_ref, w_ref, _, proj_ref = refs
            h = h_in_ref[...].astype(MXU_DTYPE)
        proj_ref[...] = jnp.dot(h, w_ref[...], preferred_element_type=F32).astype(ACT_DTYPE)

    tile = pl.BlockSpec((tm, d), lambda i, q: (i, 0))
    w_spec = pl.BlockSpec((None, d, qc), lambda i, q: (q[0], 0, 0))
    proj_spec = pl.BlockSpec((tm, qc), lambda i, q: (i, q[0]))
    proj_shape = jax.ShapeDtypeStruct((n, N_CHIP * qc), ACT_DTYPE)
    if first:
        operands = [quarter, src, mod, g1, wg_in]
        in_specs = [tile, pl.BlockSpec((None, None, 8, d), lambda i, q: (l, i // tpb, 0, 0)),
                    pl.BlockSpec((None, 1, d), lambda i, q: (l, 0, 0)), w_spec]
        out_shape, out_specs, aliases = [jax.ShapeDtypeStruct((n, d), ACT_DTYPE), proj_shape], [tile, proj_spec], {}
    else:
        operands = [quarter, src, wg_in, prev]
        in_specs = [tile, w_spec, ANY_SPEC]
        out_shape, out_specs, aliases = proj_shape, proj_spec, {3: 0}
    return pl.pallas_call(
        body, name=name, out_shape=out_shape,
        grid_spec=pltpu.PrefetchScalarGridSpec(num_scalar_prefetch=1, grid=(n // tm,), in_specs=in_specs,
                                               out_specs=out_specs),
        input_output_aliases=aliases,
        compiler_params=_params(("arbitrary",)),
    )(*operands)


def _masked_ws(ws_ref, wm_s):
    mask = _causal_mask()
    for h in range(HEADS):
        wm_s[h] = jnp.where(mask, ws_ref[h], 0.0).astype(MXU_DTYPE)


def _fill_z(i, tpb, ah_ref, gh_ref, zext):
    ah = ah_ref[...].astype(F32)
    gh = gh_ref[...].astype(F32)
    keep = jnp.where(i % tpb == 0, 0.0, 1.0)
    _put_lanes(zext, slice(0, HALO), ah * _sigmoid(gh) * keep)


def _put_lanes(dst3, rows, value):
    for lc in range(value.shape[-1] // LANES):
        dst3[lc, rows, :] = value[:, lc * LANES:(lc + 1) * LANES]


def _tap_windows(src3, lc, base, rows, flip):
    offs = {k: (CONV_TAPS - 1 - k) if flip else (k + 2) for k in range(CONV_TAPS)}
    for r in range(8):
        taps = [k for k in offs if offs[k] % 8 == r]
        lo = min(offs[k] for k in taps)
        hi = max(offs[k] for k in taps)
        win = src3[lc, pl.ds(base + lo, hi - lo + rows), :]
        for k in taps:
            yield k, win[offs[k] - lo:offs[k] - lo + rows]


def _conv_taps(src3, w3_ref, dst3, lc, nrows, flip):
    for b in range(nrows // CONV_ROWS):
        acc = jnp.zeros((CONV_ROWS, LANES), F32)
        for k, win in _tap_windows(src3, lc, b * CONV_ROWS, CONV_ROWS, flip):
            acc = acc + win * w3_ref[lc, k:k + 1, :]
        dst3[lc, b * CONV_ROWS:(b + 1) * CONV_ROWS, :] = acc


def _branches_fwd(l, proj, lng, lnb, ws, bst, cw, cb, blg, blb, t_len):
    n = proj.shape[0]
    d = lng.shape[-1]
    tm = min(TOKEN_TILE, t_len)
    tpb = t_len // tm
    per = tm // HALO
    nchunk = tm // CHUNK

    def body(u_ref, v_ref, a_ref, g_ref, ah_ref, gh_ref, lng_ref, lnb_ref, ws_ref, bst_ref, cw_ref, cb_ref,
             blg_ref, blb_ref, ya_ref, yb_ref, zc_ref, wm_s, zext, zc3):
        i = pl.program_id(0)
        _masked_ws(ws_ref, wm_s)
        _fill_z(i, tpb, ah_ref, gh_ref, zext)

        def chunk(c, carry):
            r0 = pl.multiple_of(c * CHUNK, CHUNK)
            rows = pl.ds(r0, CHUNK)
            vhat, _ = _ln_stats(v_ref[rows, :].astype(F32))
            vn = (vhat * lng_ref[...] + lnb_ref[...]).astype(MXU_DTYPE)
            u = u_ref[rows, :].astype(F32)
            for h in range(HEADS):
                cols = slice(h * CHUNK, (h + 1) * CHUNK)
                s = jnp.dot(wm_s[h], vn[:, cols], preferred_element_type=F32) + bst_ref[:, h:h + 1]
                ya_ref[rows, cols] = (u[:, cols] * s).astype(ACT_DTYPE)
            a = a_ref[rows, :].astype(F32)
            g = g_ref[rows, :].astype(F32)
            _put_lanes(zext, pl.ds(HALO + r0, CHUNK), a * _sigmoid(g))
            return carry

        lax.fori_loop(0, nchunk, chunk, 0)

        def lane_chunk(lc, carry):
            _conv_taps(zext, cw_ref, zc3, lc, tm, flip=False)
            return carry

        lax.fori_loop(0, d // LANES, lane_chunk, 0)

        def chunk2(c, carry):
            r0 = pl.multiple_of(c * CHUNK, CHUNK)
            rows = pl.ds(r0, CHUNK)
            for lc in range(d // LANES):
                lanes = slice(lc * LANES, (lc + 1) * LANES)
                zc_ref[rows, lanes] = (zc3[lc, rows, :] + cb_ref[:, lanes]).astype(ACT_DTYPE)
            zhat, _ = _ln_stats(zc_ref[rows, :].astype(F32))
            zn = zhat * blg_ref[...] + blb_ref[...]
            yb_ref[rows, :] = (zn * _sigmoid(zn)).astype(ACT_DTYPE)
            return carry

        lax.fori_loop(0, nchunk, chunk2, 0)

    col = lambda k: pl.BlockSpec((tm, d), lambda i: (i, k))
    halo = lambda k: pl.BlockSpec((HALO, d), lambda i: (jnp.maximum(i * per - 1, 0), k))
    vec = pl.BlockSpec((None, 1, d), lambda i: (l, 0, 0))
    out = pl.BlockSpec((tm, d), lambda i: (i, 0))
    return pl.pallas_call(
        body, name=f"branches_fwd_{l}",
        out_shape=[jax.ShapeDtypeStruct((n, d), ACT_DTYPE)] * 3,
        grid=(n // tm,),
        in_specs=[col(0), col(1), col(2), col(3), halo(2), halo(3), vec, vec,
                  pl.BlockSpec((None, HEADS, CHUNK, CHUNK), lambda i: (l, 0, 0, 0)),
                  pl.BlockSpec((None, CHUNK, HEADS), lambda i: (l, 0, 0)),
                  pl.BlockSpec((None, d // LANES, HALO, LANES), lambda i: (l, 0, 0, 0)), vec, vec, vec],
        out_specs=[out, out, out],
        scratch_shapes=[pltpu.VMEM((HEADS, CHUNK, CHUNK), MXU_DTYPE), pltpu.VMEM((d // LANES, HALO + tm, LANES), F32),
                        pltpu.VMEM((d // LANES, tm, LANES), F32)],
        compiler_params=_params(("arbitrary",)),
    )(proj, proj, proj, proj, proj, proj, lng, lnb, ws, bst, cw, cb, blg, blb)


def _merge_out(l, x, mod, proj, ya_in, yb_in, wg_pa, wg_pb, wg_out, t_len):
    n, d = x.shape
    tm = min(TOKEN_TILE, t_len)
    tpb = t_len // tm
    rq = d // N_CHIP

    def body(x_ref, mod_ref, ga_ref, gb_ref, yai_ref, ybi_ref, wpa_ref, wpb_ref, wo_ref,
             ya_ref, yb_ref, mg_ref, o_ref, x1_ref):
        wpa = wpa_ref[...].reshape(d, d)
        wpb = wpb_ref[...].reshape(d, d)
        wo = wo_ref[...].reshape(d, d)
        ya = jnp.dot(yai_ref[...].astype(MXU_DTYPE), wpa, preferred_element_type=F32)
        yb = jnp.dot(ybi_ref[...].astype(MXU_DTYPE), wpb, preferred_element_type=F32)
        merged = _sigmoid(ga_ref[...].astype(F32)) * ya + _sigmoid(gb_ref[...].astype(F32)) * yb
        o = _dot(merged, wo)
        ya_ref[...] = ya.astype(ACT_DTYPE)
        yb_ref[...] = yb.astype(ACT_DTYPE)
        mg_ref[...] = merged.astype(ACT_DTYPE)
        o_ref[...] = o.astype(ACT_DTYPE)
        x1_ref[...] = x_ref[...] + mod_ref[2:3, :] * o

    tile = pl.BlockSpec((tm, d), lambda i: (i, 0))
    wspec = pl.BlockSpec((N_CHIP, rq, d), lambda i: (0, 0, 0))
    return pl.pallas_call(
        body, name=f"merge_out_{l}",
        out_shape=[jax.ShapeDtypeStruct((n, d), ACT_DTYPE)] * 4 + [jax.ShapeDtypeStruct((n, d), F32)],
        grid=(n // tm,),
        in_specs=[tile, pl.BlockSpec((None, None, 8, d), lambda i: (l, i // tpb, 0, 0)),
                  pl.BlockSpec((tm, d), lambda i: (i, 4)), pl.BlockSpec((tm, d), lambda i: (i, 5)),
                  tile, tile, wspec, wspec, wspec],
        out_specs=[tile] * 5,
        compiler_params=_params(("arbitrary",)),
    )(x, mod, proj, proj, ya_in, yb_in, wg_pa, wg_pb, wg_out)


def _ffn_fwd(l, x1, mod, g2, wg_ff1, wg_ff2, t_len):
    n, d = x1.shape
    tm = min(TOKEN_TILE, t_len)
    tpb = t_len // tm
    hq = wg_ff1.shape[-1]
    hid = N_CHIP * hq

    def body(x_ref, mod_ref, g_ref, w1_ref, w2_ref, h_ref, f_ref, o2_ref, x2_ref, a2_s):
        h = _modnorm_fwd(x_ref[...], g_ref[...], mod_ref[4:5, :], mod_ref[3:4, :]).astype(MXU_DTYPE)
        h_ref[...] = h.astype(ACT_DTYPE)
        for q in range(N_CHIP):
            cols = slice(q * hq, (q + 1) * hq)
            f = jnp.dot(h, w1_ref[q], preferred_element_type=F32)
            f_ref[:, cols] = f.astype(ACT_DTYPE)
            a2_s[:, cols] = jnp.square(jnp.maximum(f, 0.0)).astype(MXU_DTYPE)
        o2 = jnp.dot(a2_s[...], w2_ref[...].reshape(hid, d), preferred_element_type=F32)
        o2_ref[...] = o2.astype(ACT_DTYPE)
        x2_ref[...] = x_ref[...] + mod_ref[5:6, :] * o2

    tile = pl.BlockSpec((tm, d), lambda i: (i, 0))
    return pl.pallas_call(
        body, name=f"ffn_fwd_{l}",
        out_shape=[jax.ShapeDtypeStruct((n, d), ACT_DTYPE), jax.ShapeDtypeStruct((n, hid), ACT_DTYPE),
                   jax.ShapeDtypeStruct((n, d), ACT_DTYPE), jax.ShapeDtypeStruct((n, d), F32)],
        grid=(n // tm,),
        in_specs=[tile, pl.BlockSpec((None, None, 8, d), lambda i: (l, i // tpb, 0, 0)),
                  pl.BlockSpec((None, 1, d), lambda i: (l, 0, 0)),
                  _resident((N_CHIP, d, hq)), _resident((N_CHIP, hq, d))],
        out_specs=[tile, pl.BlockSpec((tm, hid), lambda i: (i, 0)), tile, tile],
        scratch_shapes=[pltpu.VMEM((tm, hid), MXU_DTYPE)],
        compiler_params=_params(("arbitrary",)),
    )(x1, mod, g2, wg_ff1, wg_ff2)


def _loss_head(x, final_g, target):
    n, d = x.shape
    tm = min(TOKEN_TILE, n)

    def body(x_ref, g_ref, t_ref, loss_ref, dx_ref, dg_ref):
        @pl.when(pl.program_id(0) == 0)
        def _():
            loss_ref[...] = jnp.zeros_like(loss_ref)
            dg_ref[...] = jnp.zeros_like(dg_ref)

        x_t = x_ref[...]
        g = g_ref[...]
        r = lax.rsqrt(_rowmean(x_t * x_t) + EPS)
        xn = x_t * r
        e = xn * g - t_ref[...]
        loss_ref[...] += jnp.sum(e * e) * (0.5 / d)
        dy = e * (1.0 / d)
        dxn = dy * g
        dx_ref[...] = r * (dxn - xn * _rowmean(dxn * xn))
        dg_ref[0:1, :] += _colsum(dy * xn)

    tile = pl.BlockSpec((tm, d), lambda i: (i, 0))
    return pl.pallas_call(
        body, name="loss_head",
        out_shape=[jax.ShapeDtypeStruct((8, LANES), F32), jax.ShapeDtypeStruct((n, d), F32),
                   jax.ShapeDtypeStruct((8, d), F32)],
        grid=(n // tm,),
        in_specs=[tile, pl.BlockSpec((1, d), lambda i: (0, 0)), tile],
        out_specs=[pl.BlockSpec((8, LANES), lambda i: (0, 0)), tile, pl.BlockSpec((8, d), lambda i: (0, 0))],
        compiler_params=_params(("arbitrary",)),
    )(x, final_g, target)


def _norm_tail(x_ref, dxin_ref, dh, g_ref, sc, dx_ref, dmod_ref, dg_ref, row_sh, row_sc):
    dxm, dsh, q = _modnorm_bwd(x_ref[...], dh, g_ref[...], sc)
    dx_ref[...] = dxin_ref[...] + dxm
    dmod_ref[row_sh:row_sh + 1, :] += dsh
    dmod_ref[row_sc:row_sc + 1, :] += g_ref[...] * q
    dg_ref[0:1, :] += (1.0 + sc) * q


def _ffn_bwd(l, dx2, x1, mod, g2, o2, f, wg_ff1, wg_ff2, t_len, nb):
    n, d = dx2.shape
    tm = min(FFN_BWD_TILE, t_len)
    tpb = t_len // tm
    hq = wg_ff1.shape[-1]
    hid = N_CHIP * hq

    def body(dx2_ref, x1_ref, mod_ref, g_ref, o2_ref, f_ref, w1_ref, w2_ref,
             do2_ref, df_ref, dx1_ref, dmod_ref, dg_ref):
        i = pl.program_id(0)

        @pl.when(i == 0)
        def _():
            dg_ref[...] = jnp.zeros_like(dg_ref)

        @pl.when(i % tpb == 0)
        def _():
            dmod_ref[...] = jnp.zeros_like(dmod_ref)

        dx2_t = dx2_ref[...]
        dmod_ref[5:6, :] += _colsum(dx2_t * o2_ref[...].astype(F32))
        do2 = (dx2_t * mod_ref[5:6, :]).astype(MXU_DTYPE)
        do2_ref[...] = do2.astype(ACT_DTYPE)
        dh = jnp.zeros((tm, d), F32)
        for q in range(N_CHIP):
            cols = slice(q * hq, (q + 1) * hq)
            da2 = _dot_nt(do2, w2_ref[q])
            df = (da2 * (2.0 * jnp.maximum(f_ref[:, cols].astype(F32), 0.0))).astype(MXU_DTYPE)
            df_ref[:, cols] = df.astype(ACT_DTYPE)
            dh = dh + _dot_nt(df, w1_ref[q])
        _norm_tail(x1_ref, dx2_ref, dh, g_ref, mod_ref[4:5, :], dx1_ref, dmod_ref, dg_ref, 3, 4)

    tile = pl.BlockSpec((tm, d), lambda i: (i, 0))
    wide = pl.BlockSpec((tm, hid), lambda i: (i, 0))
    return pl.pallas_call(
        body, name=f"ffn_bwd_{l}",
        out_shape=[jax.ShapeDtypeStruct((n, d), ACT_DTYPE), jax.ShapeDtypeStruct((n, hid), ACT_DTYPE),
                   jax.ShapeDtypeStruct((n, d), F32), jax.ShapeDtypeStruct((nb, 8, d), F32),
                   jax.ShapeDtypeStruct((8, d), F32)],
        grid=(n // tm,),
        in_specs=[tile, tile, pl.BlockSpec((None, None, 8, d), lambda i: (l, i // tpb, 0, 0)),
                  pl.BlockSpec((None, 1, d), lambda i: (l, 0, 0)), tile, wide,
                  _resident((N_CHIP, d, hq)), _resident((N_CHIP, hq, d))],
        out_specs=[tile, wide, tile, pl.BlockSpec((None, 8, d), lambda i: (i // tpb, 0, 0)),
                   pl.BlockSpec((8, d), lambda i: (0, 0))],
        compiler_params=_params(("arbitrary",)),
    )(dx2, x1, mod, g2, o2, f, wg_ff1, wg_ff2)


def _merge_bwd(l, dx1, mod, o, ya, yb, proj, wg_pa, wg_pb, wg_out, t_len, nb, after):
    n, d = dx1.shape
    tm = min(TOKEN_TILE, t_len)
    tpb = t_len // tm
    rq = d // N_CHIP

    def body(dx_ref, mod_ref, o_ref, ya_ref, yb_ref, ga_ref, gb_ref, wpa_ref, wpb_ref, wo_ref, after_ref,
             do_ref, dya_ref, dyb_ref, dyai_ref, dybi_ref, dproj_ref, dmod_ref):
        i = pl.program_id(0)

        @pl.when(i % tpb == 0)
        def _():
            dmod_ref[...] = jnp.zeros_like(dmod_ref)

        dx = dx_ref[...]
        dmod_ref[2:3, :] += _colsum(dx * o_ref[...].astype(F32))
        do = (dx * mod_ref[2:3, :]).astype(MXU_DTYPE)
        do_ref[...] = do.astype(ACT_DTYPE)
        dm = _dot_nt(do, wo_ref[...].reshape(d, d))
        sa = _sigmoid(ga_ref[...].astype(F32))
        sb = _sigmoid(gb_ref[...].astype(F32))
        dya = (dm * sa).astype(MXU_DTYPE)
        dyb = (dm * sb).astype(MXU_DTYPE)
        dya_ref[...] = dya.astype(ACT_DTYPE)
        dyb_ref[...] = dyb.astype(ACT_DTYPE)
        dproj_ref[:, 0:d] = (dm * ya_ref[...].astype(F32) * sa * (1.0 - sa)).astype(ACT_DTYPE)
        dproj_ref[:, d:2 * d] = (dm * yb_ref[...].astype(F32) * sb * (1.0 - sb)).astype(ACT_DTYPE)
        dyai_ref[...] = _dot_nt(dya, wpa_ref[...].reshape(d, d)).astype(ACT_DTYPE)
        dybi_ref[...] = _dot_nt(dyb, wpb_ref[...].reshape(d, d)).astype(ACT_DTYPE)

    tile = pl.BlockSpec((tm, d), lambda i: (i, 0))
    wspec = pl.BlockSpec((N_CHIP, rq, d), lambda i: (0, 0, 0))
    return pl.pallas_call(
        body, name=f"merge_bwd_{l}",
        out_shape=[jax.ShapeDtypeStruct((n, d), ACT_DTYPE)] * 5
        + [jax.ShapeDtypeStruct((n, 6 * d), ACT_DTYPE), jax.ShapeDtypeStruct((nb, 8, d), F32)],
        grid=(n // tm,),
        in_specs=[tile, pl.BlockSpec((None, None, 8, d), lambda i: (l, i // tpb, 0, 0)), tile, tile, tile,
                  pl.BlockSpec((tm, d), lambda i: (i, 4)), pl.BlockSpec((tm, d), lambda i: (i, 5)),
                  wspec, wspec, wspec, ANY_SPEC],
        out_specs=[tile] * 5 + [pl.BlockSpec((tm, 2 * d), lambda i: (i, 2)),
                                pl.BlockSpec((None, 8, d), lambda i: (i // tpb, 0, 0))],
        compiler_params=_params(("arbitrary",)),
    )(dx1, mod, o, ya, yb, proj, proj, wg_pa, wg_pb, wg_out, after)


def _branches_bwd(l, proj, zc, dya_in, dyb_in, dproj, lng, lnb, ws, bst, cw, blg, blb, t_len, after):
    n = proj.shape[0]
    d = lng.shape[-1]
    tm = min(TOKEN_TILE, t_len)
    tpb = t_len // tm
    per = tm // HALO
    nchunk = tm // CHUNK
    ntile = n // tm

    def body(u_ref, v_ref, a_ref, g_ref, ah_ref, gh_ref, zc_ref, zcn_ref, dya_ref, dyb_ref, dybn_ref, dproj_in,
             lng_ref, lnb_ref, ws_ref, bst_ref, cw_ref, blg_ref, blb_ref, after_ref,
             dproj_ref, dws_ref, dbst_ref, dcw_ref, vec_ref, wm_s, zext, dzext, dz3, dvn_s):
        i = pl.program_id(0)

        @pl.when(i == 0)
        def _():
            dws_ref[...] = jnp.zeros_like(dws_ref)
            dbst_ref[...] = jnp.zeros_like(dbst_ref)
            dcw_ref[...] = jnp.zeros_like(dcw_ref)
            vec_ref[...] = jnp.zeros_like(vec_ref)

        _masked_ws(ws_ref, wm_s)
        _fill_z(i, tpb, ah_ref, gh_ref, zext)

        def conv_ln_bwd(zc_t, dyb_t):
            zhat, rstd = _ln_stats(zc_t)
            zn = zhat * blg_ref[...] + blb_ref[...]
            sg = _sigmoid(zn)
            dzn = dyb_t * (sg * (1.0 + zn * (1.0 - sg)))
            return _ln_bwd(dzn, zhat, rstd, blg_ref[...]), _colsum(dzn * zhat), _colsum(dzn)

        def chunk(c, carry):
            r0 = pl.multiple_of(c * CHUNK, CHUNK)
            rows = pl.ds(r0, CHUNK)
            vhat, rstd = _ln_stats(v_ref[rows, :].astype(F32))
            vn = (vhat * lng_ref[...] + lnb_ref[...]).astype(MXU_DTYPE)
            u = u_ref[rows, :].astype(F32)
            dya = dya_ref[rows, :].astype(F32)
            for h in range(HEADS):
                cols = slice(h * CHUNK, (h + 1) * CHUNK)
                s = jnp.dot(wm_s[h], vn[:, cols], preferred_element_type=F32) + bst_ref[:, h:h + 1]
                dproj_ref[rows, cols] = (dya[:, cols] * s).astype(ACT_DTYPE)
                ds = dya[:, cols] * u[:, cols]
                dvn_s[:, cols] = _dot_tn(wm_s[h], ds)
                dws_ref[h] += _dot_nt(ds, vn[:, cols])
                dbst_ref[:, h:h + 1] += jnp.sum(ds, axis=1, keepdims=True)
            dvn = dvn_s[...]
            dproj_ref[rows, d:2 * d] = _ln_bwd(dvn, vhat, rstd, lng_ref[...]).astype(ACT_DTYPE)
            vec_ref[0:1, :] += _colsum(dvn * vhat)
            vec_ref[1:2, :] += _colsum(dvn)
            a = a_ref[rows, :].astype(F32)
            g = g_ref[rows, :].astype(F32)
            _put_lanes(zext, pl.ds(HALO + r0, CHUNK), a * _sigmoid(g))
            dzc, dblg, dblb = conv_ln_bwd(zc_ref[rows, :].astype(F32), dyb_ref[rows, :].astype(F32))
            _put_lanes(dzext, rows, dzc)
            vec_ref[2:3, :] += _colsum(dzc)
            vec_ref[3:4, :] += dblg
            vec_ref[4:5, :] += dblb
            return carry

        lax.fori_loop(0, nchunk, chunk, 0)

        dzc_next, _, _ = conv_ln_bwd(zcn_ref[...].astype(F32), dybn_ref[...].astype(F32))
        _put_lanes(dzext, slice(tm, tm + HALO), dzc_next * jnp.where(i % tpb == tpb - 1, 0.0, 1.0))

        def lane_chunk_dz(lc, carry):
            _conv_taps(dzext, cw_ref, dz3, lc, tm, flip=True)
            return carry

        lax.fori_loop(0, d // LANES, lane_chunk_dz, 0)

        def lane_chunk(lc, carry):
            accs = [jnp.zeros((8, LANES), F32) for _ in range(CONV_TAPS)]
            for b in range(tm // TAP_GRAD_ROWS):
                dzc = dzext[lc, b * TAP_GRAD_ROWS:(b + 1) * TAP_GRAD_ROWS, :]
                for k, win in _tap_windows(zext, lc, b * TAP_GRAD_ROWS, TAP_GRAD_ROWS, flip=False):
                    prod = dzc * win
                    part = prod[0:8]
                    for e in range(1, TAP_GRAD_ROWS // 8):
                        part = part + prod[8 * e:8 * e + 8]
                    accs[k] = accs[k] + part
            for k in range(CONV_TAPS):
                dcw_ref[lc, k:k + 1, :] += _colsum(accs[k])
            return carry

        lax.fori_loop(0, d // LANES, lane_chunk, 0)

        def glu_bwd(c, carry):
            r0 = pl.multiple_of(c * CHUNK, CHUNK)
            rows = pl.ds(r0, CHUNK)
            for lc in range(d // LANES):
                lanes = slice(lc * LANES, (lc + 1) * LANES)
                dz = dz3[lc, rows, :]
                a = a_ref[rows, lanes].astype(F32)
                sg = _sigmoid(g_ref[rows, lanes].astype(F32))
                dproj_ref[rows, 2 * d + lc * LANES:2 * d + (lc + 1) * LANES] = (dz * sg).astype(ACT_DTYPE)
                dproj_ref[rows, 3 * d + lc * LANES:3 * d + (lc + 1) * LANES] = (
                    dz * a * sg * (1.0 - sg)).astype(ACT_DTYPE)
            return carry

        lax.fori_loop(0, nchunk, glu_bwd, 0)

        @pl.when(i == ntile - 1)
        def _():
            mask = _causal_mask()
            for h in range(HEADS):
                dws_ref[h] = jnp.where(mask, dws_ref[h], 0.0)

    col = lambda k: pl.BlockSpec((tm, d), lambda i: (i, k))
    tile = pl.BlockSpec((tm, d), lambda i: (i, 0))
    before = lambda k: pl.BlockSpec((HALO, d), lambda i: (jnp.maximum(i * per - 1, 0), k))
    following = pl.BlockSpec((HALO, d), lambda i: (jnp.minimum((i + 1) * per, n // HALO - 1), 0))
    vec = pl.BlockSpec((None, 1, d), lambda i: (l, 0, 0))
    const2 = lambda r, c: pl.BlockSpec((r, c), lambda i: (0, 0))
    return pl.pallas_call(
        body, name=f"branches_bwd_{l}",
        out_shape=[jax.ShapeDtypeStruct((n, 6 * d), ACT_DTYPE), jax.ShapeDtypeStruct((HEADS, CHUNK, CHUNK), F32),
                   jax.ShapeDtypeStruct((CHUNK, HEADS), F32), jax.ShapeDtypeStruct((d // LANES, HALO, LANES), F32),
                   jax.ShapeDtypeStruct((8, d), F32)],
        grid=(ntile,),
        in_specs=[col(0), col(1), col(2), col(3), before(2), before(3), tile, following, tile, tile, following,
                  pl.BlockSpec(memory_space=pl.ANY), vec, vec,
                  pl.BlockSpec((None, HEADS, CHUNK, CHUNK), lambda i: (l, 0, 0, 0)),
                  pl.BlockSpec((None, CHUNK, HEADS), lambda i: (l, 0, 0)),
                  pl.BlockSpec((None, d // LANES, HALO, LANES), lambda i: (l, 0, 0, 0)), vec, vec, ANY_SPEC],
        out_specs=[pl.BlockSpec((tm, 4 * d), lambda i: (i, 0)),
                   pl.BlockSpec((HEADS, CHUNK, CHUNK), lambda i: (0, 0, 0)),
                   const2(CHUNK, HEADS), pl.BlockSpec((d // LANES, HALO, LANES), lambda i: (0, 0, 0)), const2(8, d)],
        scratch_shapes=[pltpu.VMEM((HEADS, CHUNK, CHUNK), MXU_DTYPE),
                        pltpu.VMEM((d // LANES, HALO + tm, LANES), F32),
                        pltpu.VMEM((d // LANES, tm + HALO, LANES), F32),
                        pltpu.VMEM((d // LANES, tm, LANES), F32), pltpu.VMEM((CHUNK, d), F32)],
        input_output_aliases={11: 0},
        compiler_params=_params(("arbitrary",)),
    )(proj, proj, proj, proj, proj, proj, zc, zc, dya_in, dyb_in, dyb_in, dproj, lng, lnb, ws, bst, cw, blg, blb, after)


def _in_proj_bwd(l, dproj, dx1, x, mod, g1, wg_in, t_len, nb, after):
    n, d = x.shape
    tm = min(TOKEN_TILE, t_len)
    tpb = t_len // tm
    qc = wg_in.shape[-1]

    def body(dp_ref, dx1_ref, x_ref, mod_ref, g_ref, w_ref, after_ref, dx_ref, dmod_ref, dg_ref):
        i = pl.program_id(0)

        @pl.when(i == 0)
        def _():
            dg_ref[...] = jnp.zeros_like(dg_ref)

        @pl.when(i % tpb == 0)
        def _():
            dmod_ref[...] = jnp.zeros_like(dmod_ref)

        dh = jnp.zeros((tm, d), F32)
        for q in range(N_CHIP):
            dh = dh + _dot_nt(dp_ref[:, q * qc:(q + 1) * qc], w_ref[q])
        _norm_tail(x_ref, dx1_ref, dh, g_ref, mod_ref[1:2, :], dx_ref, dmod_ref, dg_ref, 0, 1)

    tile = pl.BlockSpec((tm, d), lambda i: (i, 0))
    return pl.pallas_call(
        body, name=f"in_proj_bwd_{l}",
        out_shape=[jax.ShapeDtypeStruct((n, d), F32), jax.ShapeDtypeStruct((nb, 8, d), F32),
                   jax.ShapeDtypeStruct((8, d), F32)],
        grid=(n // tm,),
        in_specs=[pl.BlockSpec((tm, N_CHIP * qc), lambda i: (i, 0)), tile, tile,
                  pl.BlockSpec((None, None, 8, d), lambda i: (l, i // tpb, 0, 0)),
                  pl.BlockSpec((None, 1, d), lambda i: (l, 0, 0)),
                  _resident((N_CHIP, d, qc)), ANY_SPEC],
        out_specs=[tile, pl.BlockSpec((None, 8, d), lambda i: (i // tpb, 0, 0)),
                   pl.BlockSpec((8, d), lambda i: (0, 0))],
        compiler_params=_params(("arbitrary",)),
    )(dproj, dx1, x, mod, g1, wg_in, after)


def _weight_grad(name, a, b, a_spec, b_spec, out_rows, out_spec, acc_shape, grid_ij, relu2=False):
    n = a.shape[0]
    tk = min(MATMUL_TILE, n)
    nk = n // tk
    cols = acc_shape[1]

    def body(a_ref, b_ref, o_ref, acc):
        k = pl.program_id(2)

        @pl.when(k == 0)
        def _():
            acc[...] = jnp.zeros_like(acc)

        a_t = a_ref[...]
        if relu2:
            a_t = jnp.square(jnp.maximum(a_t.astype(F32), 0.0))
        acc[...] += _dot_tn(a_t, b_ref[...])

        @pl.when(k == nk - 1)
        def _():
            o_ref[...] = acc[...].reshape(o_ref.shape).astype(WIRE_DTYPE)

    gi, gj = grid_ij
    return pl.pallas_call(
        body, name=name, out_shape=jax.ShapeDtypeStruct((N_CHIP, out_rows, cols), WIRE_DTYPE),
        grid=(gi, gj, nk),
        in_specs=[a_spec(tk), b_spec(tk)],
        out_specs=out_spec,
        scratch_shapes=[pltpu.VMEM(acc_shape, F32)],
        compiler_params=_params(("arbitrary", "arbitrary", "arbitrary")),
    )(a, b)


def _row_tile(rows, cols, arrays):
    budget = VMEM_LIMIT // 3
    t = budget // (arrays * 2 * cols * 4)
    t = max(8, min(rows, t // 8 * 8))
    while rows % t:
        t -= 8
    return t


def _sum_partials(name, own, got, myq, l, nl, prev):
    _, rows, cols = own.shape
    tr = _row_tile(rows, cols, 3)
    nt = rows // tr

    def body(q_ref, own_ref, got_ref, *rest):
        o_ref = rest[-1]
        acc = own_ref[...].astype(F32)
        for k in range(3):
            acc = acc + got_ref[k].astype(F32)
        o_ref[...] = acc

    operands = [myq, own, got] + ([] if prev is None else [prev])
    return pl.pallas_call(
        body, name=name, out_shape=jax.ShapeDtypeStruct((nl * rows, cols), F32),
        grid_spec=pltpu.PrefetchScalarGridSpec(
            num_scalar_prefetch=1, grid=(nt,),
            in_specs=[pl.BlockSpec((None, tr, cols), lambda i, q: (q[0], i, 0)),
                      pl.BlockSpec((3, tr, cols), lambda i, q: (0, i, 0))]
            + ([] if prev is None else [pl.BlockSpec(memory_space=pl.ANY)]),
            out_specs=pl.BlockSpec((tr, cols), lambda i, q: (l * nt + i, 0))),
        input_output_aliases={} if prev is None else {3: 0},
        compiler_params=_params(("arbitrary",)),
    )(*operands)


def _adamw(name, w, m, v, g_a, g_b=None):
    rows, cols = w.shape
    tr = _row_tile(rows, cols, 9)
    c1 = 1.0 - ADAM_B1 ** ADAM_STEP
    c2 = 1.0 - ADAM_B2 ** ADAM_STEP

    def body(*refs):
        if g_b is None:
            w_ref, m_ref, v_ref, ga_ref, g_ref, d_ref, m2_ref, v2_ref = refs
            g = ga_ref[...]
        else:
            w_ref, m_ref, v_ref, ga_ref, gb_ref, g_ref, d_ref, m2_ref, v2_ref = refs
            g = ga_ref[...] + gb_ref[...]
        m2 = ADAM_B1 * m_ref[...] + (1.0 - ADAM_B1) * g
        v2 = ADAM_B2 * v_ref[...] + (1.0 - ADAM_B2) * (g * g)
        g_ref[...] = g
        m2_ref[...] = m2
        v2_ref[...] = v2
        d_ref[...] = -ADAM_LR * ((m2 / c1) / (jnp.sqrt(v2 / c2) + ADAM_EPS) + ADAM_WD * w_ref[...])

    tile = pl.BlockSpec((tr, cols), lambda i: (i, 0))
    operands = [w, m, v, g_a] + ([] if g_b is None else [g_b])
    return pl.pallas_call(
        body, name=name, out_shape=[jax.ShapeDtypeStruct((rows, cols), F32)] * 4,
        grid=(rows // tr,), in_specs=[tile] * len(operands), out_specs=[tile] * 4,
        compiler_params=_params(("arbitrary",)),
    )(*operands)


def _pack(parts):
    flat = [p.reshape(-1, LANES) for p in parts]
    for f in flat:
        assert f.shape[0] % 8 == 0
    return jnp.concatenate(flat, axis=0)


def _unpack(packed, shapes):
    out, r = [], 0
    for s in shapes:
        size = 1
        for e in s:
            size *= e
        rows = size // LANES
        out.append(packed[r:r + rows].reshape(s))
        r += rows
    return out


def kernel(x, c, w_ada, b_ada, norm1_g, w_in, a_ln_g, a_ln_b, a_ws, a_bs, w_pa, b_conv_w, b_conv_b, b_ln_g, b_ln_b, w_pb, w_out, norm2_g, w_ff1, w_ff2, final_g, loss_target, m_w_ada, m_b_ada, m_norm1_g, m_w_in, m_a_ln_g, m_a_ln_b, m_a_ws, m_a_bs, m_w_pa, m_b_conv_w, m_b_conv_b, m_b_ln_g, m_b_ln_b, m_w_pb, m_w_out, m_norm2_g, m_w_ff1, m_w_ff2, m_final_g, v_w_ada, v_b_ada, v_norm1_g, v_w_in, v_a_ln_g, v_a_ln_b, v_a_ws, v_a_bs, v_w_pa, v_b_conv_w, v_b_conv_b, v_b_ln_g, v_b_ln_b, v_w_pb, v_w_out, v_norm2_g, v_w_ff1, v_w_ff2, v_final_g):
    nb, t_len, d = x.shape
    nl = w_in.shape[0]
    n = nb * t_len
    cq = w_ada.shape[-1]
    cc = d // N_CHIP
    mx, my, mc = _my_place()
    myq = (2 * mx + my).astype(jnp.int32).reshape(1)
    me = 4 * mx + 2 * my + mc

    def exchange_start(name, mode, src, after):
        land = lax.empty((N_DEV,) + src.shape[-2:], src.dtype)
        send_sems, recv_sems, srcs, lands, tok_out = _split_start(name + "_start", mode, [src], [land], after)
        return (name, mode, send_sems, recv_sems, srcs, lands), tok_out

    def exchange_wait(handle, after):
        name, mode, send_sems, recv_sems, srcs, lands = handle
        srcs, lands = _split_wait(name + "_wait", mode, send_sems, recv_sems, srcs, lands, after)
        own = lax.dynamic_slice_in_dim(srcs[0], me, 1, axis=0) if mode == "exchange" else srcs[0][None]
        return lax.dynamic_update_slice_in_dim(lands[0], own, me, axis=0)

    taps = jnp.pad(b_conv_w.reshape(nl, CONV_TAPS, cc), ((0, 0), (0, HALO - CONV_TAPS), (0, 0)))
    first = jnp.concatenate([jnp.pad(c, ((0, 8 - nb), (0, 0))), taps.reshape(nl * HALO * cc // d, d)], axis=0)
    first = _all_to_all(jnp.broadcast_to(first[None], (N_DEV,) + first.shape), "gather_c_and_taps")
    c_all = first[:, :nb].reshape(N_DEV * nb, d)
    cwg = first[:, 8:].reshape(N_CHIP, 2, nl, HALO, cc)[:, 0]
    cw = cwg.transpose(1, 2, 0, 3).reshape(nl, HALO, d)
    cw = cw.reshape(nl, HALO, d // LANES, LANES).transpose(0, 2, 1, 3)
    mod_part = _ada_forward(c_all, w_ada, b_ada.reshape(nl, 1, N_CHIP * cq), myq)
    mod_slots = mod_part.reshape(nl, N_DEV, nb, cq).transpose(1, 0, 2, 3).reshape(N_DEV, nl * nb, cq)
    mod_got = _all_to_all(mod_slots, "exchange_mod").reshape(N_CHIP, 2, nl, nb, cq)[:, 0]
    mod6 = mod_got.transpose(1, 2, 0, 3).reshape(nl, nb, 6, d)
    mod = jnp.pad(mod6, ((0, 0), (0, 0), (0, 2), (0, 0)))

    big = ["w_in", "w_pa", "w_pb", "w_out", "w_ff1", "w_ff2"]
    ws_given = dict(w_in=(w_in, m_w_in, v_w_in), w_pa=(w_pa, m_w_pa, v_w_pa), w_pb=(w_pb, m_w_pb, v_w_pb),
                    w_out=(w_out, m_w_out, v_w_out), w_ff1=(w_ff1, m_w_ff1, v_w_ff1), w_ff2=(w_ff2, m_w_ff2, v_w_ff2))

    def own_slot(w_l, after=None):
        if after is not None:
            w_l = w_l - after[0, 0]
        empty = lax.empty((N_CHIP,) + w_l.shape, WIRE_DTYPE)
        return lax.dynamic_update_index_in_dim(empty, w_l.astype(WIRE_DTYPE), myq[0], 0)

    def zero_after(*arrays):
        z = jnp.zeros((8, LANES), F32)
        for a in arrays:
            piece = a.reshape(-1, a.shape[-1])[:8, :LANES]
            z = z + jnp.where(jnp.isfinite(piece), piece, 0.0) * 0.0
        return z

    first_sems = _split_start("gather_start_in_0", "gather", [], [own_slot(w_in[0])], zero_after(cw, mod[:, 0]))
    token = first = first_sems[4]
    gathers = []
    for l in range(nl):
        group = big[1:] if l == 0 else big
        send_sems, recv_sems, _, lands, token = _split_start(
            f"gather_start_{l}", "gather", [], [own_slot(ws_given[k][0][l], first) for k in group], token)
        if l == 0:
            send_sems = list(first_sems[0]) + list(send_sems)
            recv_sems = list(first_sems[1]) + list(recv_sems)
            lands = list(first_sems[3]) + list(lands)
        gathers.append((send_sems, recv_sems, lands))
    mod = mod + token[0, 0]

    def gather_wait(l, part, lo, hi, after):
        send_sems, recv_sems, lands = gathers[l]
        return _split_wait(f"gather_wait_{part}_{l}", "gather", send_sems[lo:hi], recv_sems[lo:hi], [],
                           lands[lo:hi], after)[1]

    vec3 = lambda p: p.reshape(nl, 1, d)
    g1, g2 = vec3(norm1_g), vec3(norm2_g)
    lng, lnb, cb, blg, blb = vec3(a_ln_g), vec3(a_ln_b), vec3(b_conv_b), vec3(b_ln_g), vec3(b_ln_b)
    bst = a_bs.transpose(0, 2, 1)

    xs = x.reshape(n, d)
    saved = []
    weights = []
    for l in range(nl):
        if l == 0:
            send_sems, recv_sems, lands = gathers[0]
            wg_in = lands[0]
            h, proj = _in_proj_quarter("in_proj_0_own", l, xs, mod, g1, wg_in, myq, t_len)
            for k, (px, py) in enumerate(_other_chips(mx, my)):
                (wg_in,) = _split_wait(f"gather_wait_in_0_{k}", "gather", send_sems[:1], recv_sems[:1], [], [wg_in],
                                       proj, peers=(k,))[1]
                quarter = (2 * px + py).astype(jnp.int32).reshape(1)
                proj = _in_proj_quarter(f"in_proj_0_{k}", l, h, None, None, wg_in, quarter, t_len, proj)
        else:
            (wg_in,) = gather_wait(l, "in", 0, 1, xs)
            h, proj = _in_proj(l, xs, mod, g1, wg_in, t_len)
        ya_in, yb_in, zc = _branches_fwd(l, proj, lng, lnb, a_ws, bst, cw, cb, blg, blb, t_len)
        wg_pa, wg_pb, wg_out = gather_wait(l, "mid", 1, 4, ya_in)
        ya, yb, merged, o, x1 = _merge_out(l, xs, mod, proj, ya_in, yb_in, wg_pa, wg_pb, wg_out, t_len)
        wg_ff1, wg_ff2 = gather_wait(l, "ffn", 4, 6, x1)
        h2, f, o2, x2 = _ffn_fwd(l, x1, mod, g2, wg_ff1, wg_ff2, t_len)
        saved.append((xs, h, proj, ya_in, yb_in, zc, ya, yb, merged, o, x1, h2, f, o2))
        weights.append((wg_in, wg_pa, wg_pb, wg_out, wg_ff1, wg_ff2))
        xs = x2

    loss_blk, dx, dfinal = _loss_head(xs, final_g.reshape(1, d), loss_target.reshape(n, d))

    tok = lambda w: (lambda tk: pl.BlockSpec((tk, w), lambda i, j, k: (k, 0)))
    tok_i = lambda w: (lambda tk: pl.BlockSpec((tk, w), lambda i, j, k: (k, i)))
    tok_j = lambda w: (lambda tk: pl.BlockSpec((tk, w), lambda i, j, k: (k, j)))
    qin = weights[0][0].shape[-1]
    hq = weights[0][4].shape[-1]
    rq = d // N_CHIP
    slot_i = lambda r, cdim: pl.BlockSpec((None, r, cdim), lambda i, j, k: (i, 0, 0))
    slot_j = lambda r, cdim: pl.BlockSpec((None, r, cdim), lambda i, j, k: (j, 0, 0))
    all_slots = pl.BlockSpec((N_CHIP, rq, d), lambda i, j, k: (0, 0, 0))
    scatters = []

    def scatter_start(l, part, names, grads, after):
        lands = [lax.empty((3,) + g.shape[1:], g.dtype) for g in grads]
        send_sems, recv_sems, srcs, lands, tok_out = _split_start(f"scatter_start_{part}_{l}", "scatter", grads, lands,
                                                                  after)
        scatters.append((f"scatter_wait_{part}_{l}", l, names, send_sems, recv_sems, srcs, lands))
        return tok_out

    dmods, small = [None] * nl, [None] * nl
    for l in reversed(range(nl)):
        x0, h, proj, ya_in, yb_in, zc, ya, yb, merged, o, x1, h2, f, o2 = saved[l]
        wg_in, wg_pa, wg_pb, wg_out, wg_ff1, wg_ff2 = weights[l]
        do2, df, dx1, dmod_c, dg2 = _ffn_bwd(l, dx, x1, mod, g2, o2, f, wg_ff1, wg_ff2, t_len, nb)
        g_ff2 = _weight_grad(f"grad_w_ff2_{l}", f, do2, tok_i(hq), tok(d), hq, slot_i(hq, d), (hq, d), (N_CHIP, 1),
                             relu2=True)
        g_ff1 = _weight_grad(f"grad_w_ff1_{l}", h2, df, tok(d), tok_j(hq), d, slot_j(d, hq), (d, hq), (1, N_CHIP))
        if l == 0:
            token = scatter_start(l, "ffn", ["w_ff2", "w_ff1"], [g_ff2, g_ff1], token)
        do, dya, dyb, dya_in, dyb_in, dproj, dmod_b = _merge_bwd(l, dx1, mod, o, ya, yb, proj, wg_pa, wg_pb, wg_out,
                                                                 t_len, nb, token)
        g_out = _weight_grad(f"grad_w_out_{l}", merged, do, tok(d), tok(d), rq, all_slots, (d, d), (1, 1))
        g_pa = _weight_grad(f"grad_w_pa_{l}", ya_in, dya, tok(d), tok(d), rq, all_slots, (d, d), (1, 1))
        g_pb = _weight_grad(f"grad_w_pb_{l}", yb_in, dyb, tok(d), tok(d), rq, all_slots, (d, d), (1, 1))
        if l == 0:
            token = scatter_start(l, "mid", ["w_out", "w_pa", "w_pb"], [g_out, g_pa, g_pb], token)
        dproj, dws, dbst, dcw, vecs = _branches_bwd(l, proj, zc, dya_in, dyb_in, dproj, lng, lnb, a_ws, bst, cw,
                                                    blg, blb, t_len, token)
        g_in = _weight_grad(f"grad_w_in_{l}", h, dproj, tok(d), tok_j(qin), d, slot_j(d, qin), (d, qin), (1, N_CHIP))
        if l == 0:
            token = scatter_start(l, "in", ["w_in"], [g_in], token)
        else:
            token = scatter_start(l, "all", ["w_ff2", "w_ff1", "w_out", "w_pa", "w_pb", "w_in"],
                                  [g_ff2, g_ff1, g_out, g_pa, g_pb, g_in], token)
        dx, dmod_a, dg1 = _in_proj_bwd(l, dproj, dx1, x0, mod, g1, wg_in, t_len, nb, token)
        dmods[l] = jnp.concatenate([dmod_a[:, 0:2], dmod_b[:, 2:3], dmod_c[:, 3:6]], axis=1)
        dcw = dcw.transpose(1, 0, 2).reshape(HALO, d)[:CONV_TAPS]
        small[l] = (dg1[0], vecs[0], vecs[1], dws, dbst.T, dcw, vecs[2], vecs[3], vecs[4], dg2[0])
    grad_x = dx.reshape(nb, t_len, d)

    names = ["norm1_g", "a_ln_g", "a_ln_b", "a_ws", "a_bs", "b_conv_w", "b_conv_b", "b_ln_g", "b_ln_b", "norm2_g"]
    stacked = [jnp.stack([small[l][k] for l in range(nl)]) for k in range(len(names))]
    stacked[5] = jnp.pad(stacked[5], ((0, 0), (0, HALO - CONV_TAPS), (0, 0)))
    stacked += [dfinal, loss_blk]
    part_shapes = [s.shape for s in stacked]
    packed = _pack(stacked)
    prow = packed.shape[0]
    pad_rows = (-prow) % (8 * N_DEV)
    packed = jnp.pad(packed, ((0, pad_rows), (0, 0)))
    srow = packed.shape[0] // N_DEV
    half = dict.fromkeys(big)

    def sum_arrived(entries, after):
        last = after
        for name, l, group, send_sems, recv_sems, srcs, lands in entries:
            srcs, lands = _split_wait(name, "scatter", send_sems, recv_sems, srcs, lands, after)
            for k, g_own, g_got in zip(group, srcs, lands):
                last = half[k] = _sum_partials(f"sum_{k}_{l}", g_own, g_got, myq, l, nl, half[k])
        return last

    early = max(1, (nl - 1) * 2 // 3)
    reduce_handle, token = exchange_start("reduce_small", "exchange", packed.reshape(N_DEV, srow, LANES), token)
    sum_arrived(scatters[:early], token)
    mine = _sum_slots(exchange_wait(reduce_handle, [h for h in half.values() if h is not None]), "sum_small")
    dmod_rows = nl * nb * 6 * d // LANES
    second = jnp.concatenate([mine, jnp.stack(dmods).reshape(dmod_rows, LANES)], axis=0)
    gather_handle, token = exchange_start("gather_small_and_dmod", "allgather", second, token)
    last = sum_arrived(scatters[early:], token)
    sums = [half[k] for k in big]
    swap_send, swap_recv, sums, others, token = _split_start(
        "swap_start", "swap", sums, [lax.empty(s.shape, s.dtype) for s in sums], last)
    second = exchange_wait(gather_handle, token)
    total = second[:, :srow].reshape(N_DEV * srow, LANES)[:prow]
    dmod_all = second[:, srow:].reshape(N_DEV, nl, nb, 6 * d).transpose(1, 0, 2, 3).reshape(nl, N_DEV * nb, 6 * d)
    g_w_ada, g_b_ada = _ada_backward(c_all, dmod_all, myq, cq, token)

    sg = dict(zip(names + ["final_g", "loss"], _unpack(total, part_shapes)))
    loss = sg["loss"][0, 0]
    sg["b_conv_w"] = lax.dynamic_slice_in_dim(sg["b_conv_w"][:, :CONV_TAPS], myq[0] * cc, cc, axis=2).reshape(
        nl, CONV_TAPS, 1, cc)
    sg["final_g"] = sg["final_g"][0]
    sg["b_ada"] = g_b_ada.reshape(nl, N_CHIP * cq)
    small_names = ["b_ada", "norm1_g", "a_ln_g", "a_ln_b", "a_ws", "a_bs", "b_conv_w", "b_conv_b", "b_ln_g",
                   "b_ln_b", "norm2_g", "final_g"]
    given = dict(b_ada=(b_ada, m_b_ada, v_b_ada), norm1_g=(norm1_g, m_norm1_g, v_norm1_g),
                 a_ln_g=(a_ln_g, m_a_ln_g, v_a_ln_g), a_ln_b=(a_ln_b, m_a_ln_b, v_a_ln_b),
                 a_ws=(a_ws, m_a_ws, v_a_ws), a_bs=(a_bs, m_a_bs, v_a_bs),
                 b_conv_w=(b_conv_w, m_b_conv_w, v_b_conv_w), b_conv_b=(b_conv_b, m_b_conv_b, v_b_conv_b),
                 b_ln_g=(b_ln_g, m_b_ln_g, v_b_ln_g), b_ln_b=(b_ln_b, m_b_ln_b, v_b_ln_b),
                 norm2_g=(norm2_g, m_norm2_g, v_norm2_g), final_g=(final_g, m_final_g, v_final_g))

    def padded(a):
        rows = -(-a.size // (8 * LANES)) * 8
        return jnp.pad(a.reshape(-1), (0, rows * LANES - a.size)).reshape(rows, LANES)

    packs = [_pack([padded(given[k][j]) for k in small_names]) for j in range(3)]
    gpack = _pack([padded(sg[k].astype(F32)) for k in small_names])
    res_small = _adamw("adamw_small", packs[0], packs[1], packs[2], gpack)
    out = {}
    for j, kind in enumerate(["grad", "delta", "new_m", "new_v"]):
        r = 0
        for k in small_names:
            a = given[k][0]
            rows = -(-a.size // (8 * LANES)) * 8
            out[(kind, k)] = res_small[j][r:r + rows].reshape(-1)[:a.size].reshape(a.shape)
            r += rows

    res = _adamw("adamw_w_ada", w_ada.reshape(nl * d, cq), m_w_ada.reshape(nl * d, cq), v_w_ada.reshape(nl * d, cq),
                 g_w_ada.reshape(nl * d, cq))
    for kind, r in zip(["grad", "delta", "new_m", "new_v"], res):
        out[(kind, "w_ada")] = r.reshape(w_ada.shape)

    sums, others = _split_wait("swap_wait", "swap", swap_send, swap_recv, sums, others, res[0])
    for k, s_mine, s_other in zip(big, sums, others):
        w, m, v = ws_given[k]
        cols = w.shape[-1]
        res = _adamw(f"adamw_{k}", w.reshape(-1, cols), m.reshape(-1, cols), v.reshape(-1, cols), s_mine, s_other)
        for kind, r in zip(["grad", "delta", "new_m", "new_v"], res):
            out[(kind, k)] = r.reshape(w.shape)

    order = ["w_ada", "b_ada", "norm1_g", "w_in", "a_ln_g", "a_ln_b", "a_ws", "a_bs", "w_pa", "b_conv_w", "b_conv_b",
             "b_ln_g", "b_ln_b", "w_pb", "w_out", "norm2_g", "w_ff1", "w_ff2", "final_g"]
    return (loss, grad_x, *[out[("grad", k)] for k in order], *[out[("delta", k)] for k in order],
            *[out[("new_m", k)] for k in order], *[out[("new_v", k)] for k in order])
```

```python
import jax
import jax.numpy as jnp
from jax import lax
from jax.experimental import pallas as pl
from jax.experimental.pallas import tpu as pltpu

F32 = jnp.float32
MXU_DTYPE = jnp.bfloat16
ACT_DTYPE = jnp.bfloat16
WIRE_DTYPE = jnp.bfloat16

EPS = 1e-6
CHUNK = 128
HEADS = 8
CONV_TAPS = 31
HALO = 32
N_DEV = 8
N_CHIP = 4
ADAM_LR, ADAM_B1, ADAM_B2, ADAM_EPS, ADAM_WD, ADAM_STEP = 0.001, 0.9, 0.999, 1e-08, 0.01, 10

V7X_VMEM_BYTES = 64 * 1024 * 1024
VMEM_LIMIT = V7X_VMEM_BYTES * 7 // 8
TOKEN_TILE = 512
MATMUL_TILE = 2048
FFN_BWD_TILE = 512
CONV_ROWS = 64
TAP_GRAD_ROWS = 32
LANES = 128
MESH_ID = pl.DeviceIdType.MESH


def _params(sem=None):
    return pltpu.CompilerParams(dimension_semantics=sem, vmem_limit_bytes=VMEM_LIMIT)


def _resident(shape):
    return pl.BlockSpec(shape, lambda *_: (0,) * len(shape), pipeline_mode=pl.Buffered(1))


def _dot(a, b):
    return jnp.dot(a.astype(MXU_DTYPE), b.astype(MXU_DTYPE), preferred_element_type=F32)


def _dot_nt(a, b):
    return lax.dot_general(a.astype(MXU_DTYPE), b.astype(MXU_DTYPE), (((1,), (1,)), ((), ())),
                           preferred_element_type=F32)


def _dot_tn(a, b):
    return lax.dot_general(a.astype(MXU_DTYPE), b.astype(MXU_DTYPE), (((0,), (0,)), ((), ())),
                           preferred_element_type=F32)


def _colsum(a):
    return jnp.sum(a, axis=0, keepdims=True)


def _rowmean(a):
    return jnp.mean(a, axis=-1, keepdims=True)


def _sigmoid(a):
    return 1.0 / (1.0 + jnp.exp(-a))


def _modnorm_fwd(x, g, sc, sh):
    r = lax.rsqrt(_rowmean(x * x) + EPS)
    return (x * r) * (g * (1.0 + sc)) + sh


def _modnorm_bwd(x, dh, g, sc):
    r = lax.rsqrt(_rowmean(x * x) + EPS)
    xn = x * r
    dxn = dh * (g * (1.0 + sc))
    dx = r * (dxn - xn * _rowmean(dxn * xn))
    return dx, _colsum(dh), _colsum(dh * xn)


def _ln_stats(v):
    mu = _rowmean(v)
    vc = v - mu
    rstd = lax.rsqrt(_rowmean(vc * vc) + EPS)
    return vc * rstd, rstd


def _ln_bwd(dy, vhat, rstd, g):
    dvh = dy * g
    return rstd * (dvh - _rowmean(dvh) - vhat * _rowmean(dvh * vhat))


def _causal_mask():
    row = lax.broadcasted_iota(jnp.int32, (CHUNK, CHUNK), 0)
    col = lax.broadcasted_iota(jnp.int32, (CHUNK, CHUNK), 1)
    return row >= col


def _my_place():
    return lax.axis_index("x"), lax.axis_index("y"), lax.axis_index("c")


def _other_chips(mx, my):
    return [(1 - mx, my), (mx, 1 - my), (1 - mx, 1 - my)]


def _other_devices(mx, my, mc):
    return [((mx + ((k >> 2) & 1)) % 2, (my + ((k >> 1) & 1)) % 2, (mc + (k & 1)) % 2) for k in range(1, N_DEV)]


def _all_to_all(x, name):
    assert x.shape[0] == N_DEV

    def body(x_ref, o_ref, send_sems, recv_sems):
        mx, my, mc = _my_place()
        me = 4 * mx + 2 * my + mc
        o_ref[me] = x_ref[me]
        copies = []
        for k, (px, py, pc) in enumerate(_other_devices(mx, my, mc)):
            cp = pltpu.make_async_remote_copy(
                src_ref=x_ref.at[4 * px + 2 * py + pc], dst_ref=o_ref.at[me],
                send_sem=send_sems.at[k], recv_sem=recv_sems.at[k],
                device_id=(px, py, pc), device_id_type=MESH_ID)
            cp.start()
            copies.append(cp)
        for cp in copies:
            cp.wait()

    return pl.pallas_call(
        body, name=name, out_shape=jax.ShapeDtypeStruct(x.shape, x.dtype),
        in_specs=[pl.BlockSpec(memory_space=pltpu.VMEM)],
        out_specs=pl.BlockSpec(memory_space=pltpu.VMEM),
        scratch_shapes=[pltpu.SemaphoreType.DMA((N_DEV - 1,)), pltpu.SemaphoreType.DMA((N_DEV - 1,))],
        compiler_params=pltpu.CompilerParams(vmem_limit_bytes=VMEM_LIMIT),
    )(x)


def _sum_slots(x, name):
    def body(x_ref, o_ref):
        acc = x_ref[0]
        for s in range(1, N_DEV):
            acc = acc + x_ref[s]
        o_ref[...] = acc

    return pl.pallas_call(
        body, name=name, out_shape=jax.ShapeDtypeStruct(x.shape[1:], x.dtype),
        in_specs=[pl.BlockSpec(memory_space=pltpu.VMEM)], out_specs=pl.BlockSpec(memory_space=pltpu.VMEM),
        compiler_params=pltpu.CompilerParams(vmem_limit_bytes=VMEM_LIMIT),
    )(x)


HBM_SPEC = pl.BlockSpec(memory_space=pltpu.HBM)
SEM_SPEC = pl.BlockSpec(memory_space=pltpu.SEMAPHORE)
ANY_SPEC = pl.BlockSpec(memory_space=pl.ANY)
SPLIT_EFFECT = pltpu.SideEffectType.DATAFLOW_SIDE_EFFECTING


def _quarter_copies(mode, srcs, lands, send_sems, recv_sems, peers=(0, 1, 2)):
    mx, my, mc = _my_place()
    myq = 2 * mx + my
    if mode == "swap":
        return [pltpu.make_async_remote_copy(
            src_ref=srcs[a], dst_ref=lands[a], send_sem=send_sems[a].at[0], recv_sem=recv_sems[a].at[0],
            device_id=(mx, my, 1 - mc), device_id_type=MESH_ID) for a in range(len(lands))]
    copies = []
    if mode in ("exchange", "allgather"):
        me = 4 * mx + 2 * my + mc
        for a in range(len(lands)):
            for k, (px, py, pc) in enumerate(_other_devices(mx, my, mc)):
                copies.append(pltpu.make_async_remote_copy(
                    src_ref=srcs[a].at[4 * px + 2 * py + pc] if mode == "exchange" else srcs[a],
                    dst_ref=lands[a].at[me],
                    send_sem=send_sems[a].at[k], recv_sem=recv_sems[a].at[k],
                    device_id=(px, py, pc), device_id_type=MESH_ID))
        return copies
    for a in range(len(lands)):
        for k, (px, py) in enumerate(_other_chips(mx, my)):
            if k not in peers:
                continue
            if mode == "gather":
                src, dst = lands[a].at[myq], lands[a].at[myq]
            else:
                src, dst = srcs[a].at[2 * px + py], lands[a].at[k]
            copies.append(pltpu.make_async_remote_copy(
                src_ref=src, dst_ref=dst, send_sem=send_sems[a].at[k], recv_sem=recv_sems[a].at[k],
                device_id=(px, py, mc), device_id_type=MESH_ID))
    return copies


def _split_start(name, mode, srcs, lands, after):
    ns, n = len(srcs), len(lands)

    def body(*refs):
        outs = refs[ns + n + 1:]
        for cp in _quarter_copies(mode, refs[:ns], refs[ns:ns + n], outs[:n], outs[n:2 * n]):
            cp.start()
        token = outs[-1]
        token[...] = jnp.zeros_like(token)

    arrays = list(srcs) + list(lands)
    per_array = {"swap": 1, "exchange": N_DEV - 1, "allgather": N_DEV - 1}.get(mode, 3)
    res = pl.pallas_call(
        body, name=name,
        out_shape=[pltpu.SemaphoreType.DMA((per_array,))] * (2 * n) + [pltpu.HBM(x.shape, x.dtype) for x in arrays]
        + [jax.ShapeDtypeStruct((8, LANES), F32)],
        in_specs=[HBM_SPEC] * (ns + n) + [ANY_SPEC],
        out_specs=[SEM_SPEC] * (2 * n) + [HBM_SPEC] * (ns + n) + [pl.BlockSpec(memory_space=pltpu.VMEM)],
        input_output_aliases={i: 2 * n + i for i in range(ns + n)},
        compiler_params=pltpu.CompilerParams(has_side_effects=SPLIT_EFFECT),
    )(*[pltpu.with_memory_space_constraint(x, pltpu.HBM) for x in arrays], after)
    return res[:n], res[n:2 * n], res[2 * n:2 * n + ns], res[2 * n + ns:2 * n + ns + n], res[-1]


def _split_wait(name, mode, send_sems, recv_sems, srcs, lands, after, peers=(0, 1, 2)):
    ns, n = len(srcs), len(lands)
    afters = list(after) if isinstance(after, (list, tuple)) else [after]

    def body(*refs):
        sems = refs[ns + n:ns + 3 * n]
        for cp in _quarter_copies(mode, refs[:ns], refs[ns:ns + n], sems[:n], sems[n:], peers):
            cp.wait_send()
            cp.wait_recv()

    arrays = list(srcs) + list(lands)
    res = pl.pallas_call(
        body, name=name,
        out_shape=[pltpu.HBM(x.shape, x.dtype) for x in arrays],
        in_specs=[HBM_SPEC] * (ns + n) + [SEM_SPEC] * (2 * n) + [ANY_SPEC] * len(afters),
        out_specs=[HBM_SPEC] * (ns + n),
        input_output_aliases={i: i for i in range(ns + n)},
        compiler_params=pltpu.CompilerParams(has_side_effects=SPLIT_EFFECT),
    )(*arrays, *send_sems, *recv_sems, *afters)
    return res[:ns], res[ns:]


def _ada_forward(c_all, w_ada, b_ada3, myq):
    nl, d, cq = w_ada.shape
    nb = c_all.shape[0]

    def body(q_ref, c_ref, w_ref, b_ref, o_ref):
        c = c_ref[...]
        act = c * _sigmoid(c)
        o_ref[...] = _dot(act, w_ref[...]) + b_ref[...]

    return pl.pallas_call(
        body, name="ada_forward",
        out_shape=jax.ShapeDtypeStruct((nl, nb, cq), F32),
        grid_spec=pltpu.PrefetchScalarGridSpec(
            num_scalar_prefetch=1, grid=(nl,),
            in_specs=[pl.BlockSpec((nb, d), lambda l, q: (0, 0)),
                      pl.BlockSpec((None, d, cq), lambda l, q: (l, 0, 0)),
                      pl.BlockSpec((None, 1, cq), lambda l, q: (l, 0, q[0]))],
            out_specs=pl.BlockSpec((None, nb, cq), lambda l, q: (l, 0, 0))),
        compiler_params=_params(("arbitrary",)),
    )(myq, c_all, w_ada, b_ada3)


def _ada_backward(c_all, dmod_all, myq, cq, after):
    nb, d = c_all.shape
    nl = dmod_all.shape[0]
    full = dmod_all.shape[2]

    def body(q_ref, c_ref, dq_ref, dall_ref, after_ref, gw_ref, gb_ref):
        c = c_ref[...]
        act = c * _sigmoid(c)
        gw_ref[...] = _dot_tn(act, dq_ref[...])
        gb_ref[...] = _colsum(dall_ref[...])

    return pl.pallas_call(
        body, name="ada_backward",
        out_shape=[jax.ShapeDtypeStruct((nl, d, cq), F32), jax.ShapeDtypeStruct((nl, 1, full), F32)],
        grid_spec=pltpu.PrefetchScalarGridSpec(
            num_scalar_prefetch=1, grid=(nl,),
            in_specs=[pl.BlockSpec((nb, d), lambda l, q: (0, 0)),
                      pl.BlockSpec((None, nb, cq), lambda l, q: (l, 0, q[0])),
                      pl.BlockSpec((None, nb, full), lambda l, q: (l, 0, 0)), ANY_SPEC],
            out_specs=[pl.BlockSpec((None, d, cq), lambda l, q: (l, 0, 0)),
                       pl.BlockSpec((None, 1, full), lambda l, q: (l, 0, 0))]),
        compiler_params=_params(("arbitrary",)),
    )(myq, c_all, dmod_all, dmod_all, after)


def _in_proj(l, x, mod, g1, wg_in, t_len):
    n, d = x.shape
    tm = min(TOKEN_TILE, t_len)
    tpb = t_len // tm
    qc = wg_in.shape[-1]

    def body(x_ref, mod_ref, g_ref, w_ref, h_ref, proj_ref):
        h = _modnorm_fwd(x_ref[...], g_ref[...], mod_ref[1:2, :], mod_ref[0:1, :]).astype(MXU_DTYPE)
        h_ref[...] = h.astype(ACT_DTYPE)
        for q in range(N_CHIP):
            proj_ref[:, q * qc:(q + 1) * qc] = jnp.dot(h, w_ref[q], preferred_element_type=F32).astype(ACT_DTYPE)

    return pl.pallas_call(
        body, name=f"in_proj_{l}",
        out_shape=[jax.ShapeDtypeStruct((n, d), ACT_DTYPE), jax.ShapeDtypeStruct((n, N_CHIP * qc), ACT_DTYPE)],
        grid=(n // tm,),
        in_specs=[pl.BlockSpec((tm, d), lambda i: (i, 0)),
                  pl.BlockSpec((None, None, 8, d), lambda i: (l, i // tpb, 0, 0)),
                  pl.BlockSpec((None, 1, d), lambda i: (l, 0, 0)),
                  _resident((N_CHIP, d, qc))],
        out_specs=[pl.BlockSpec((tm, d), lambda i: (i, 0)),
                   pl.BlockSpec((tm, N_CHIP * qc), lambda i: (i, 0))],
        compiler_params=_params(("arbitrary",)),
    )(x, mod, g1, wg_in)


def _in_proj_quarter(name, l, src, mod, g1, wg_in, quarter, t_len, prev=None):
    n, d = src.shape
    tm = min(TOKEN_TILE, t_len)
    tpb = t_len // tm
    qc = wg_in.shape[-1]
    first = prev is None

    def body(q_ref, *refs):
        if first:
            x_ref, mod_ref, g_ref, w_ref, h_ref, proj_ref = refs
            h = _modnorm_fwd(x_ref[...], g_ref[...], mod_ref[1:2, :], mod_ref[0:1, :]).astype(MXU_DTYPE)
            h_ref[...] = h.astype(ACT_DTYPE)
        else:
            h_in_ref, w_ref, _, proj_ref = refs
            h = h_in_ref[...].astype(MXU_DTYPE)
        proj_ref[...] = jnp.dot(h, w_ref[...], preferred_element_type=F32).astype(ACT_DTYPE)

    tile = pl.BlockSpec((tm, d), lambda i, q: (i, 0))
    w_spec = pl.BlockSpec((None, d, qc), lambda i, q: (q[0], 0, 0))
    proj_spec = pl.BlockSpec((tm, qc), lambda i, q: (i, q[0]))
    proj_shape = jax.ShapeDtypeStruct((n, N_CHIP * qc), ACT_DTYPE)
    if first:
        operands = [quarter, src, mod, g1, wg_in]
        in_specs = [tile, pl.BlockSpec((None, None, 8, d), lambda i, q: (l, i // tpb, 0, 0)),
                    pl.BlockSpec((None, 1, d), lambda i, q: (l, 0, 0)), w_spec]
        out_shape, out_specs, aliases = [jax.ShapeDtypeStruct((n, d), ACT_DTYPE), proj_shape], [tile, proj_spec], {}
    else:
        operands = [quarter, src, wg_in, prev]
        in_specs = [tile, w_spec, ANY_SPEC]
        out_shape, out_specs, aliases = proj_shape, proj_spec, {3: 0}
    return pl.pallas_call(
        body, name=name, out_shape=out_shape,
        grid_spec=pltpu.PrefetchScalarGridSpec(num_scalar_prefetch=1, grid=(n // tm,), in_specs=in_specs,
                                               out_specs=out_specs),
        input_output_aliases=aliases,
        compiler_params=_params(("arbitrary",)),
    )(*operands)


def _masked_ws(ws_ref, wm_s):
    mask = _causal_mask()
    for h in range(HEADS):
        wm_s[h] = jnp.where(mask, ws_ref[h], 0.0).astype(MXU_DTYPE)


def _fill_z(i, tpb, ah_ref, gh_ref, zext):
    ah = ah_ref[...].astype(F32)
    gh = gh_ref[...].astype(F32)
    keep = jnp.where(i % tpb == 0, 0.0, 1.0)
    _put_lanes(zext, slice(0, HALO), ah * _sigmoid(gh) * keep)


def _put_lanes(dst3, rows, value):
    for lc in range(value.shape[-1] // LANES):
        dst3[lc, rows, :] = value[:, lc * LANES:(lc + 1) * LANES]


def _tap_windows(src3, lc, base, rows, flip):
    offs = {k: (CONV_TAPS - 1 - k) if flip else (k + 2) for k in range(CONV_TAPS)}
    for r in range(8):
        taps = [k for k in offs if offs[k] % 8 == r]
        lo = min(offs[k] for k in taps)
        hi = max(offs[k] for k in taps)
        win = src3[lc, pl.ds(base + lo, hi - lo + rows), :]
        for k in taps:
            yield k, win[offs[k] - lo:offs[k] - lo + rows]


def _conv_taps(src3, w3_ref, dst3, lc, nrows, flip):
    for b in range(nrows // CONV_ROWS):
        acc = jnp.zeros((CONV_ROWS, LANES), F32)
        for k, win in _tap_windows(src3, lc, b * CONV_ROWS, CONV_ROWS, flip):
            acc = acc + win * w3_ref[lc, k:k + 1, :]
        dst3[lc, b * CONV_ROWS:(b + 1) * CONV_ROWS, :] = acc


def _branches_fwd(l, proj, lng, lnb, ws, bst, cw, cb, blg, blb, t_len):
    n = proj.shape[0]
    d = lng.shape[-1]
    tm = min(TOKEN_TILE, t_len)
    tpb = t_len // tm
    per = tm // HALO
    nchunk = tm // CHUNK

    def body(u_ref, v_ref, a_ref, g_ref, ah_ref, gh_ref, lng_ref, lnb_ref, ws_ref, bst_ref, cw_ref, cb_ref,
             blg_ref, blb_ref, ya_ref, yb_ref, zc_ref, wm_s, zext, zc3):
        i = pl.program_id(0)
        _masked_ws(ws_ref, wm_s)
        _fill_z(i, tpb, ah_ref, gh_ref, zext)

        def chunk(c, carry):
            r0 = pl.multiple_of(c * CHUNK, CHUNK)
            rows = pl.ds(r0, CHUNK)
            vhat, _ = _ln_stats(v_ref[rows, :].astype(F32))
            vn = (vhat * lng_ref[...] + lnb_ref[...]).astype(MXU_DTYPE)
            u = u_ref[rows, :].astype(F32)
            for h in range(HEADS):
                cols = slice(h * CHUNK, (h + 1) * CHUNK)
                s = jnp.dot(wm_s[h], vn[:, cols], preferred_element_type=F32) + bst_ref[:, h:h + 1]
                ya_ref[rows, cols] = (u[:, cols] * s).astype(ACT_DTYPE)
            a = a_ref[rows, :].astype(F32)
            g = g_ref[rows, :].astype(F32)
            _put_lanes(zext, pl.ds(HALO + r0, CHUNK), a * _sigmoid(g))
            return carry

        lax.fori_loop(0, nchunk, chunk, 0)

        def lane_chunk(lc, carry):
            _conv_taps(zext, cw_ref, zc3, lc, tm, flip=False)
            return carry

        lax.fori_loop(0, d // LANES, lane_chunk, 0)

        def chunk2(c, carry):
            r0 = pl.multiple_of(c * CHUNK, CHUNK)
            rows = pl.ds(r0, CHUNK)
            for lc in range(d // LANES):
                lanes = slice(lc * LANES, (lc + 1) * LANES)
                zc_ref[rows, lanes] = (zc3[lc, rows, :] + cb_ref[:, lanes]).astype(ACT_DTYPE)
            zhat, _ = _ln_stats(zc_ref[rows, :].astype(F32))
            zn = zhat * blg_ref[...] + blb_ref[...]
            yb_ref[rows, :] = (zn * _sigmoid(zn)).astype(ACT_DTYPE)
            return carry

        lax.fori_loop(0, nchunk, chunk2, 0)

    col = lambda k: pl.BlockSpec((tm, d), lambda i: (i, k))
    halo = lambda k: pl.BlockSpec((HALO, d), lambda i: (jnp.maximum(i * per - 1, 0), k))
    vec = pl.BlockSpec((None, 1, d), lambda i: (l, 0, 0))
    out = pl.BlockSpec((tm, d), lambda i: (i, 0))
    return pl.pallas_call(
        body, name=f"branches_fwd_{l}",
        out_shape=[jax.ShapeDtypeStruct((n, d), ACT_DTYPE)] * 3,
        grid=(n // tm,),
        in_specs=[col(0), col(1), col(2), col(3), halo(2), halo(3), vec, vec,
                  pl.BlockSpec((None, HEADS, CHUNK, CHUNK), lambda i: (l, 0, 0, 0)),
                  pl.BlockSpec((None, CHUNK, HEADS), lambda i: (l, 0, 0)),
                  pl.BlockSpec((None, d // LANES, HALO, LANES), lambda i: (l, 0, 0, 0)), vec, vec, vec],
        out_specs=[out, out, out],
        scratch_shapes=[pltpu.VMEM((HEADS, CHUNK, CHUNK), MXU_DTYPE), pltpu.VMEM((d // LANES, HALO + tm, LANES), F32),
                        pltpu.VMEM((d // LANES, tm, LANES), F32)],
        compiler_params=_params(("arbitrary",)),
    )(proj, proj, proj, proj, proj, proj, lng, lnb, ws, bst, cw, cb, blg, blb)


def _merge_out(l, x, mod, proj, ya_in, yb_in, wg_pa, wg_pb, wg_out, t_len):
    n, d = x.shape
    tm = min(TOKEN_TILE, t_len)
    tpb = t_len // tm
    rq = d // N_CHIP

    def body(x_ref, mod_ref, ga_ref, gb_ref, yai_ref, ybi_ref, wpa_ref, wpb_ref, wo_ref,
             ya_ref, yb_ref, mg_ref, o_ref, x1_ref):
        wpa = wpa_ref[...].reshape(d, d)
        wpb = wpb_ref[...].reshape(d, d)
        wo = wo_ref[...].reshape(d, d)
        ya = jnp.dot(yai_ref[...].astype(MXU_DTYPE), wpa, preferred_element_type=F32)
        yb = jnp.dot(ybi_ref[...].astype(MXU_DTYPE), wpb, preferred_element_type=F32)
        merged = _sigmoid(ga_ref[...].astype(F32)) * ya + _sigmoid(gb_ref[...].astype(F32)) * yb
        o = _dot(merged, wo)
        ya_ref[...] = ya.astype(ACT_DTYPE)
        yb_ref[...] = yb.astype(ACT_DTYPE)
        mg_ref[...] = merged.astype(ACT_DTYPE)
        o_ref[...] = o.astype(ACT_DTYPE)
        x1_ref[...] = x_ref[...] + mod_ref[2:3, :] * o

    tile = pl.BlockSpec((tm, d), lambda i: (i, 0))
    wspec = pl.BlockSpec((N_CHIP, rq, d), lambda i: (0, 0, 0))
    return pl.pallas_call(
        body, name=f"merge_out_{l}",
        out_shape=[jax.ShapeDtypeStruct((n, d), ACT_DTYPE)] * 4 + [jax.ShapeDtypeStruct((n, d), F32)],
        grid=(n // tm,),
        in_specs=[tile, pl.BlockSpec((None, None, 8, d), lambda i: (l, i // tpb, 0, 0)),
                  pl.BlockSpec((tm, d), lambda i: (i, 4)), pl.BlockSpec((tm, d), lambda i: (i, 5)),
                  tile, tile, wspec, wspec, wspec],
        out_specs=[tile] * 5,
        compiler_params=_params(("arbitrary",)),
    )(x, mod, proj, proj, ya_in, yb_in, wg_pa, wg_pb, wg_out)


def _ffn_fwd(l, x1, mod, g2, wg_ff1, wg_ff2, t_len):
    n, d = x1.shape
    tm = min(TOKEN_TILE, t_len)
    tpb = t_len // tm
    hq = wg_ff1.shape[-1]
    hid = N_CHIP * hq

    def body(x_ref, mod_ref, g_ref, w1_ref, w2_ref, h_ref, f_ref, o2_ref, x2_ref, a2_s):
        h = _modnorm_fwd(x_ref[...], g_ref[...], mod_ref[4:5, :], mod_ref[3:4, :]).astype(MXU_DTYPE)
        h_ref[...] = h.astype(ACT_DTYPE)
        for q in range(N_CHIP):
            cols = slice(q * hq, (q + 1) * hq)
            f = jnp.dot(h, w1_ref[q], preferred_element_type=F32)
            f_ref[:, cols] = f.astype(ACT_DTYPE)
            a2_s[:, cols] = jnp.square(jnp.maximum(f, 0.0)).astype(MXU_DTYPE)
        o2 = jnp.dot(a2_s[...], w2_ref[...].reshape(hid, d), preferred_element_type=F32)
        o2_ref[...] = o2.astype(ACT_DTYPE)
        x2_ref[...] = x_ref[...] + mod_ref[5:6, :] * o2

    tile = pl.BlockSpec((tm, d), lambda i: (i, 0))
    return pl.pallas_call(
        body, name=f"ffn_fwd_{l}",
        out_shape=[jax.ShapeDtypeStruct((n, d), ACT_DTYPE), jax.ShapeDtypeStruct((n, hid), ACT_DTYPE),
                   jax.ShapeDtypeStruct((n, d), ACT_DTYPE), jax.ShapeDtypeStruct((n, d), F32)],
        grid=(n // tm,),
        in_specs=[tile, pl.BlockSpec((None, None, 8, d), lambda i: (l, i // tpb, 0, 0)),
                  pl.BlockSpec((None, 1, d), lambda i: (l, 0, 0)),
                  _resident((N_CHIP, d, hq)), _resident((N_CHIP, hq, d))],
        out_specs=[tile, pl.BlockSpec((tm, hid), lambda i: (i, 0)), tile, tile],
        scratch_shapes=[pltpu.VMEM((tm, hid), MXU_DTYPE)],
        compiler_params=_params(("arbitrary",)),
    )(x1, mod, g2, wg_ff1, wg_ff2)


def _loss_head(x, final_g, target):
    n, d = x.shape
    tm = min(TOKEN_TILE, n)

    def body(x_ref, g_ref, t_ref, loss_ref, dx_ref, dg_ref):
        @pl.when(pl.program_id(0) == 0)
        def _():
            loss_ref[...] = jnp.zeros_like(loss_ref)
            dg_ref[...] = jnp.zeros_like(dg_ref)

        x_t = x_ref[...]
        g = g_ref[...]
        r = lax.rsqrt(_rowmean(x_t * x_t) + EPS)
        xn = x_t * r
        e = xn * g - t_ref[...]
        loss_ref[...] += jnp.sum(e * e) * (0.5 / d)
        dy = e * (1.0 / d)
        dxn = dy * g
        dx_ref[...] = r * (dxn - xn * _rowmean(dxn * xn))
        dg_ref[0:1, :] += _colsum(dy * xn)

    tile = pl.BlockSpec((tm, d), lambda i: (i, 0))
    return pl.pallas_call(
        body, name="loss_head",
        out_shape=[jax.ShapeDtypeStruct((8, LANES), F32), jax.ShapeDtypeStruct((n, d), F32),
                   jax.ShapeDtypeStruct((8, d), F32)],
        grid=(n // tm,),
        in_specs=[tile, pl.BlockSpec((1, d), lambda i: (0, 0)), tile],
        out_specs=[pl.BlockSpec((8, LANES), lambda i: (0, 0)), tile, pl.BlockSpec((8, d), lambda i: (0, 0))],
        compiler_params=_params(("arbitrary",)),
    )(x, final_g, target)


def _norm_tail(x_ref, dxin_ref, dh, g_ref, sc, dx_ref, dmod_ref, dg_ref, row_sh, row_sc):
    dxm, dsh, q = _modnorm_bwd(x_ref[...], dh, g_ref[...], sc)
    dx_ref[...] = dxin_ref[...] + dxm
    dmod_ref[row_sh:row_sh + 1, :] += dsh
    dmod_ref[row_sc:row_sc + 1, :] += g_ref[...] * q
    dg_ref[0:1, :] += (1.0 + sc) * q


def _ffn_bwd(l, dx2, x1, mod, g2, o2, f, wg_ff1, wg_ff2, t_len, nb):
    n, d = dx2.shape
    tm = min(FFN_BWD_TILE, t_len)
    tpb = t_len // tm
    hq = wg_ff1.shape[-1]
    hid = N_CHIP * hq

    def body(dx2_ref, x1_ref, mod_ref, g_ref, o2_ref, f_ref, w1_ref, w2_ref,
             do2_ref, df_ref, dx1_ref, dmod_ref, dg_ref):
        i = pl.program_id(0)

        @pl.when(i == 0)
        def _():
            dg_ref[...] = jnp.zeros_like(dg_ref)

        @pl.when(i % tpb == 0)
        def _():
            dmod_ref[...] = jnp.zeros_like(dmod_ref)

        dx2_t = dx2_ref[...]
        dmod_ref[5:6, :] += _colsum(dx2_t * o2_ref[...].astype(F32))
        do2 = (dx2_t * mod_ref[5:6, :]).astype(MXU_DTYPE)
        do2_ref[...] = do2.astype(ACT_DTYPE)
        dh = jnp.zeros((tm, d), F32)
        for q in range(N_CHIP):
            cols = slice(q * hq, (q + 1) * hq)
            da2 = _dot_nt(do2, w2_ref[q])
            df = (da2 * (2.0 * jnp.maximum(f_ref[:, cols].astype(F32), 0.0))).astype(MXU_DTYPE)
            df_ref[:, cols] = df.astype(ACT_DTYPE)
            dh = dh + _dot_nt(df, w1_ref[q])
        _norm_tail(x1_ref, dx2_ref, dh, g_ref, mod_ref[4:5, :], dx1_ref, dmod_ref, dg_ref, 3, 4)

    tile = pl.BlockSpec((tm, d), lambda i: (i, 0))
    wide = pl.BlockSpec((tm, hid), lambda i: (i, 0))
    return pl.pallas_call(
        body, name=f"ffn_bwd_{l}",
        out_shape=[jax.ShapeDtypeStruct((n, d), ACT_DTYPE), jax.ShapeDtypeStruct((n, hid), ACT_DTYPE),
                   jax.ShapeDtypeStruct((n, d), F32), jax.ShapeDtypeStruct((nb, 8, d), F32),
                   jax.ShapeDtypeStruct((8, d), F32)],
        grid=(n // tm,),
        in_specs=[tile, tile, pl.BlockSpec((None, None, 8, d), lambda i: (l, i // tpb, 0, 0)),
                  pl.BlockSpec((None, 1, d), lambda i: (l, 0, 0)), tile, wide,
                  _resident((N_CHIP, d, hq)), _resident((N_CHIP, hq, d))],
        out_specs=[tile, wide, tile, pl.BlockSpec((None, 8, d), lambda i: (i // tpb, 0, 0)),
                   pl.BlockSpec((8, d), lambda i: (0, 0))],
        compiler_params=_params(("arbitrary",)),
    )(dx2, x1, mod, g2, o2, f, wg_ff1, wg_ff2)


def _merge_bwd(l, dx1, mod, o, ya, yb, proj, wg_pa, wg_pb, wg_out, t_len, nb, after):
    n, d = dx1.shape
    tm = min(TOKEN_TILE, t_len)
    tpb = t_len // tm
    rq = d // N_CHIP

    def body(dx_ref, mod_ref, o_ref, ya_ref, yb_ref, ga_ref, gb_ref, wpa_ref, wpb_ref, wo_ref, after_ref,
             do_ref, dya_ref, dyb_ref, dyai_ref, dybi_ref, dproj_ref, dmod_ref):
        i = pl.program_id(0)

        @pl.when(i % tpb == 0)
        def _():
            dmod_ref[...] = jnp.zeros_like(dmod_ref)

        dx = dx_ref[...]
        dmod_ref[2:3, :] += _colsum(dx * o_ref[...].astype(F32))
        do = (dx * mod_ref[2:3, :]).astype(MXU_DTYPE)
        do_ref[...] = do.astype(ACT_DTYPE)
        dm = _dot_nt(do, wo_ref[...].reshape(d, d))
        sa = _sigmoid(ga_ref[...].astype(F32))
        sb = _sigmoid(gb_ref[...].astype(F32))
        dya = (dm * sa).astype(MXU_DTYPE)
        dyb = (dm * sb).astype(MXU_DTYPE)
        dya_ref[...] = dya.astype(ACT_DTYPE)
        dyb_ref[...] = dyb.astype(ACT_DTYPE)
        dproj_ref[:, 0:d] = (dm * ya_ref[...].astype(F32) * sa * (1.0 - sa)).astype(ACT_DTYPE)
        dproj_ref[:, d:2 * d] = (dm * yb_ref[...].astype(F32) * sb * (1.0 - sb)).astype(ACT_DTYPE)
        dyai_ref[...] = _dot_nt(dya, wpa_ref[...].reshape(d, d)).astype(ACT_DTYPE)
        dybi_ref[...] = _dot_nt(dyb, wpb_ref[...].reshape(d, d)).astype(ACT_DTYPE)

    tile = pl.BlockSpec((tm, d), lambda i: (i, 0))
    wspec = pl.BlockSpec((N_CHIP, rq, d), lambda i: (0, 0, 0))
    return pl.pallas_call(
        body, name=f"merge_bwd_{l}",
        out_shape=[jax.ShapeDtypeStruct((n, d), ACT_DTYPE)] * 5
        + [jax.ShapeDtypeStruct((n, 6 * d), ACT_DTYPE), jax.ShapeDtypeStruct((nb, 8, d), F32)],
        grid=(n // tm,),
        in_specs=[tile, pl.BlockSpec((None, None, 8, d), lambda i: (l, i // tpb, 0, 0)), tile, tile, tile,
                  pl.BlockSpec((tm, d), lambda i: (i, 4)), pl.BlockSpec((tm, d), lambda i: (i, 5)),
                  wspec, wspec, wspec, ANY_SPEC],
        out_specs=[tile] * 5 + [pl.BlockSpec((tm, 2 * d), lambda i: (i, 2)),
                                pl.BlockSpec((None, 8, d), lambda i: (i // tpb, 0, 0))],
        compiler_params=_params(("arbitrary",)),
    )(dx1, mod, o, ya, yb, proj, proj, wg_pa, wg_pb, wg_out, after)


def _branches_bwd(l, proj, zc, dya_in, dyb_in, dproj, lng, lnb, ws, bst, cw, blg, blb, t_len, after):
    n = proj.shape[0]
    d = lng.shape[-1]
    tm = min(TOKEN_TILE, t_len)
    tpb = t_len // tm
    per = tm // HALO
    nchunk = tm // CHUNK
    ntile = n // tm

    def body(u_ref, v_ref, a_ref, g_ref, ah_ref, gh_ref, zc_ref, zcn_ref, dya_ref, dyb_ref, dybn_ref, dproj_in,
             lng_ref, lnb_ref, ws_ref, bst_ref, cw_ref, blg_ref, blb_ref, after_ref,
             dproj_ref, dws_ref, dbst_ref, dcw_ref, vec_ref, wm_s, zext, dzext, dz3, dvn_s):
        i = pl.program_id(0)

        @pl.when(i == 0)
        def _():
            dws_ref[...] = jnp.zeros_like(dws_ref)
            dbst_ref[...] = jnp.zeros_like(dbst_ref)
            dcw_ref[...] = jnp.zeros_like(dcw_ref)
            vec_ref[...] = jnp.zeros_like(vec_ref)

        _masked_ws(ws_ref, wm_s)
        _fill_z(i, tpb, ah_ref, gh_ref, zext)

        def conv_ln_bwd(zc_t, dyb_t):
            zhat, rstd = _ln_stats(zc_t)
            zn = zhat * blg_ref[...] + blb_ref[...]
            sg = _sigmoid(zn)
            dzn = dyb_t * (sg * (1.0 + zn * (1.0 - sg)))
            return _ln_bwd(dzn, zhat, rstd, blg_ref[...]), _colsum(dzn * zhat), _colsum(dzn)

        def chunk(c, carry):
            r0 = pl.multiple_of(c * CHUNK, CHUNK)
            rows = pl.ds(r0, CHUNK)
            vhat, rstd = _ln_stats(v_ref[rows, :].astype(F32))
            vn = (vhat * lng_ref[...] + lnb_ref[...]).astype(MXU_DTYPE)
            u = u_ref[rows, :].astype(F32)
            dya = dya_ref[rows, :].astype(F32)
            for h in range(HEADS):
                cols = slice(h * CHUNK, (h + 1) * CHUNK)
                s = jnp.dot(wm_s[h], vn[:, cols], preferred_element_type=F32) + bst_ref[:, h:h + 1]
                dproj_ref[rows, cols] = (dya[:, cols] * s).astype(ACT_DTYPE)
                ds = dya[:, cols] * u[:, cols]
                dvn_s[:, cols] = _dot_tn(wm_s[h], ds)
                dws_ref[h] += _dot_nt(ds, vn[:, cols])
                dbst_ref[:, h:h + 1] += jnp.sum(ds, axis=1, keepdims=True)
            dvn = dvn_s[...]
            dproj_ref[rows, d:2 * d] = _ln_bwd(dvn, vhat, rstd, lng_ref[...]).astype(ACT_DTYPE)
            vec_ref[0:1, :] += _colsum(dvn * vhat)
            vec_ref[1:2, :] += _colsum(dvn)
            a = a_ref[rows, :].astype(F32)
            g = g_ref[rows, :].astype(F32)
            _put_lanes(zext, pl.ds(HALO + r0, CHUNK), a * _sigmoid(g))
            dzc, dblg, dblb = conv_ln_bwd(zc_ref[rows, :].astype(F32), dyb_ref[rows, :].astype(F32))
            _put_lanes(dzext, rows, dzc)
            vec_ref[2:3, :] += _colsum(dzc)
            vec_ref[3:4, :] += dblg
            vec_ref[4:5, :] += dblb
            return carry

        lax.fori_loop(0, nchunk, chunk, 0)

        dzc_next, _, _ = conv_ln_bwd(zcn_ref[...].astype(F32), dybn_ref[...].astype(F32))
        _put_lanes(dzext, slice(tm, tm + HALO), dzc_next * jnp.where(i % tpb == tpb - 1, 0.0, 1.0))

        def lane_chunk_dz(lc, carry):
            _conv_taps(dzext, cw_ref, dz3, lc, tm, flip=True)
            return carry

        lax.fori_loop(0, d // LANES, lane_chunk_dz, 0)

        def lane_chunk(lc, carry):
            accs = [jnp.zeros((8, LANES), F32) for _ in range(CONV_TAPS)]
            for b in range(tm // TAP_GRAD_ROWS):
                dzc = dzext[lc, b * TAP_GRAD_ROWS:(b + 1) * TAP_GRAD_ROWS, :]
                for k, win in _tap_windows(zext, lc, b * TAP_GRAD_ROWS, TAP_GRAD_ROWS, flip=False):
                    prod = dzc * win
                    part = prod[0:8]
                    for e in range(1, TAP_GRAD_ROWS // 8):
                        part = part + prod[8 * e:8 * e + 8]
                    accs[k] = accs[k] + part
            for k in range(CONV_TAPS):
                dcw_ref[lc, k:k + 1, :] += _colsum(accs[k])
            return carry

        lax.fori_loop(0, d // LANES, lane_chunk, 0)

        def glu_bwd(c, carry):
            r0 = pl.multiple_of(c * CHUNK, CHUNK)
            rows = pl.ds(r0, CHUNK)
            for lc in range(d // LANES):
                lanes = slice(lc * LANES, (lc + 1) * LANES)
                dz = dz3[lc, rows, :]
                a = a_ref[rows, lanes].astype(F32)
                sg = _sigmoid(g_ref[rows, lanes].astype(F32))
                dproj_ref[rows, 2 * d + lc * LANES:2 * d + (lc + 1) * LANES] = (dz * sg).astype(ACT_DTYPE)
                dproj_ref[rows, 3 * d + lc * LANES:3 * d + (lc + 1) * LANES] = (
                    dz * a * sg * (1.0 - sg)).astype(ACT_DTYPE)
            return carry

        lax.fori_loop(0, nchunk, glu_bwd, 0)

        @pl.when(i == ntile - 1)
        def _():
            mask = _causal_mask()
            for h in range(HEADS):
                dws_ref[h] = jnp.where(mask, dws_ref[h], 0.0)

    col = lambda k: pl.BlockSpec((tm, d), lambda i: (i, k))
    tile = pl.BlockSpec((tm, d), lambda i: (i, 0))
    before = lambda k: pl.BlockSpec((HALO, d), lambda i: (jnp.maximum(i * per - 1, 0), k))
    following = pl.BlockSpec((HALO, d), lambda i: (jnp.minimum((i + 1) * per, n // HALO - 1), 0))
    vec = pl.BlockSpec((None, 1, d), lambda i: (l, 0, 0))
    const2 = lambda r, c: pl.BlockSpec((r, c), lambda i: (0, 0))
    return pl.pallas_call(
        body, name=f"branches_bwd_{l}",
        out_shape=[jax.ShapeDtypeStruct((n, 6 * d), ACT_DTYPE), jax.ShapeDtypeStruct((HEADS, CHUNK, CHUNK), F32),
                   jax.ShapeDtypeStruct((CHUNK, HEADS), F32), jax.ShapeDtypeStruct((d // LANES, HALO, LANES), F32),
                   jax.ShapeDtypeStruct((8, d), F32)],
        grid=(ntile,),
        in_specs=[col(0), col(1), col(2), col(3), before(2), before(3), tile, following, tile, tile, following,
                  pl.BlockSpec(memory_space=pl.ANY), vec, vec,
                  pl.BlockSpec((None, HEADS, CHUNK, CHUNK), lambda i: (l, 0, 0, 0)),
                  pl.BlockSpec((None, CHUNK, HEADS), lambda i: (l, 0, 0)),
                  pl.BlockSpec((None, d // LANES, HALO, LANES), lambda i: (l, 0, 0, 0)), vec, vec, ANY_SPEC],
        out_specs=[pl.BlockSpec((tm, 4 * d), lambda i: (i, 0)),
                   pl.BlockSpec((HEADS, CHUNK, CHUNK), lambda i: (0, 0, 0)),
                   const2(CHUNK, HEADS), pl.BlockSpec((d // LANES, HALO, LANES), lambda i: (0, 0, 0)), const2(8, d)],
        scratch_shapes=[pltpu.VMEM((HEADS, CHUNK, CHUNK), MXU_DTYPE),
                        pltpu.VMEM((d // LANES, HALO + tm, LANES), F32),
                        pltpu.VMEM((d // LANES, tm + HALO, LANES), F32),
                        pltpu.VMEM((d // LANES, tm, LANES), F32), pltpu.VMEM((CHUNK, d), F32)],
        input_output_aliases={11: 0},
        compiler_params=_params(("arbitrary",)),
    )(proj, proj, proj, proj, proj, proj, zc, zc, dya_in, dyb_in, dyb_in, dproj, lng, lnb, ws, bst, cw, blg, blb, after)


def _in_proj_bwd(l, dproj, dx1, x, mod, g1, wg_in, t_len, nb, after):
    n, d = x.shape
    tm = min(TOKEN_TILE, t_len)
    tpb = t_len // tm
    qc = wg_in.shape[-1]

    def body(dp_ref, dx1_ref, x_ref, mod_ref, g_ref, w_ref, after_ref, dx_ref, dmod_ref, dg_ref):
        i = pl.program_id(0)

        @pl.when(i == 0)
        def _():
            dg_ref[...] = jnp.zeros_like(dg_ref)

        @pl.when(i % tpb == 0)
        def _():
            dmod_ref[...] = jnp.zeros_like(dmod_ref)

        dh = jnp.zeros((tm, d), F32)
        for q in range(N_CHIP):
            dh = dh + _dot_nt(dp_ref[:, q * qc:(q + 1) * qc], w_ref[q])
        _norm_tail(x_ref, dx1_ref, dh, g_ref, mod_ref[1:2, :], dx_ref, dmod_ref, dg_ref, 0, 1)

    tile = pl.BlockSpec((tm, d), lambda i: (i, 0))
    return pl.pallas_call(
        body, name=f"in_proj_bwd_{l}",
        out_shape=[jax.ShapeDtypeStruct((n, d), F32), jax.ShapeDtypeStruct((nb, 8, d), F32),
                   jax.ShapeDtypeStruct((8, d), F32)],
        grid=(n // tm,),
        in_specs=[pl.BlockSpec((tm, N_CHIP * qc), lambda i: (i, 0)), tile, tile,
                  pl.BlockSpec((None, None, 8, d), lambda i: (l, i // tpb, 0, 0)),
                  pl.BlockSpec((None, 1, d), lambda i: (l, 0, 0)),
                  _resident((N_CHIP, d, qc)), ANY_SPEC],
        out_specs=[tile, pl.BlockSpec((None, 8, d), lambda i: (i // tpb, 0, 0)),
                   pl.BlockSpec((8, d), lambda i: (0, 0))],
        compiler_params=_params(("arbitrary",)),
    )(dproj, dx1, x, mod, g1, wg_in, after)


def _weight_grad(name, a, b, a_spec, b_spec, out_rows, out_spec, acc_shape, grid_ij, relu2=False):
    n = a.shape[0]
    tk = min(MATMUL_TILE, n)
    nk = n // tk
    cols = acc_shape[1]

    def body(a_ref, b_ref, o_ref, acc):
        k = pl.program_id(2)

        @pl.when(k == 0)
        def _():
            acc[...] = jnp.zeros_like(acc)

        a_t = a_ref[...]
        if relu2:
            a_t = jnp.square(jnp.maximum(a_t.astype(F32), 0.0))
        acc[...] += _dot_tn(a_t, b_ref[...])

        @pl.when(k == nk - 1)
        def _():
            o_ref[...] = acc[...].reshape(o_ref.shape).astype(WIRE_DTYPE)

    gi, gj = grid_ij
    return pl.pallas_call(
        body, name=name, out_shape=jax.ShapeDtypeStruct((N_CHIP, out_rows, cols), WIRE_DTYPE),
        grid=(gi, gj, nk),
        in_specs=[a_spec(tk), b_spec(tk)],
        out_specs=out_spec,
        scratch_shapes=[pltpu.VMEM(acc_shape, F32)],
        compiler_params=_params(("arbitrary", "arbitrary", "arbitrary")),
    )(a, b)


def _row_tile(rows, cols, arrays):
    budget = VMEM_LIMIT // 3
    t = budget // (arrays * 2 * cols * 4)
    t = max(8, min(rows, t // 8 * 8))
    while rows % t:
        t -= 8
    return t


def _sum_partials(name, own, got, myq, l, nl, prev):
    _, rows, cols = own.shape
    tr = _row_tile(rows, cols, 3)
    nt = rows // tr

    def body(q_ref, own_ref, got_ref, *rest):
        o_ref = rest[-1]
        acc = own_ref[...].astype(F32)
        for k in range(3):
            acc = acc + got_ref[k].astype(F32)
        o_ref[...] = acc

    operands = [myq, own, got] + ([] if prev is None else [prev])
    return pl.pallas_call(
        body, name=name, out_shape=jax.ShapeDtypeStruct((nl * rows, cols), F32),
        grid_spec=pltpu.PrefetchScalarGridSpec(
            num_scalar_prefetch=1, grid=(nt,),
            in_specs=[pl.BlockSpec((None, tr, cols), lambda i, q: (q[0], i, 0)),
                      pl.BlockSpec((3, tr, cols), lambda i, q: (0, i, 0))]
            + ([] if prev is None else [pl.BlockSpec(memory_space=pl.ANY)]),
            out_specs=pl.BlockSpec((tr, cols), lambda i, q: (l * nt + i, 0))),
        input_output_aliases={} if prev is None else {3: 0},
        compiler_params=_params(("arbitrary",)),
    )(*operands)


def _adamw(name, w, m, v, g_a, g_b=None):
    rows, cols = w.shape
    tr = _row_tile(rows, cols, 9)
    c1 = 1.0 - ADAM_B1 ** ADAM_STEP
    c2 = 1.0 - ADAM_B2 ** ADAM_STEP

    def body(*refs):
        if g_b is None:
            w_ref, m_ref, v_ref, ga_ref, g_ref, d_ref, m2_ref, v2_ref = refs
            g = ga_ref[...]
        else:
            w_ref, m_ref, v_ref, ga_ref, gb_ref, g_ref, d_ref, m2_ref, v2_ref = refs
            g = ga_ref[...] + gb_ref[...]
        m2 = ADAM_B1 * m_ref[...] + (1.0 - ADAM_B1) * g
        v2 = ADAM_B2 * v_ref[...] + (1.0 - ADAM_B2) * (g * g)
        g_ref[...] = g
        m2_ref[...] = m2
        v2_ref[...] = v2
        d_ref[...] = -ADAM_LR * ((m2 / c1) / (jnp.sqrt(v2 / c2) + ADAM_EPS) + ADAM_WD * w_ref[...])

    tile = pl.BlockSpec((tr, cols), lambda i: (i, 0))
    operands = [w, m, v, g_a] + ([] if g_b is None else [g_b])
    return pl.pallas_call(
        body, name=name, out_shape=[jax.ShapeDtypeStruct((rows, cols), F32)] * 4,
        grid=(rows // tr,), in_specs=[tile] * len(operands), out_specs=[tile] * 4,
        compiler_params=_params(("arbitrary",)),
    )(*operands)


def _pack(parts):
    flat = [p.reshape(-1, LANES) for p in parts]
    for f in flat:
        assert f.shape[0] % 8 == 0
    return jnp.concatenate(flat, axis=0)


def _unpack(packed, shapes):
    out, r = [], 0
    for s in shapes:
        size = 1
        for e in s:
            size *= e
        rows = size // LANES
        out.append(packed[r:r + rows].reshape(s))
        r += rows
    return out


def kernel(x, c, w_ada, b_ada, norm1_g, w_in, a_ln_g, a_ln_b, a_ws, a_bs, w_pa, b_conv_w, b_conv_b, b_ln_g, b_ln_b, w_pb, w_out, norm2_g, w_ff1, w_ff2, final_g, loss_target, m_w_ada, m_b_ada, m_norm1_g, m_w_in, m_a_ln_g, m_a_ln_b, m_a_ws, m_a_bs, m_w_pa, m_b_conv_w, m_b_conv_b, m_b_ln_g, m_b_ln_b, m_w_pb, m_w_out, m_norm2_g, m_w_ff1, m_w_ff2, m_final_g, v_w_ada, v_b_ada, v_norm1_g, v_w_in, v_a_ln_g, v_a_ln_b, v_a_ws, v_a_bs, v_w_pa, v_b_conv_w, v_b_conv_b, v_b_ln_g, v_b_ln_b, v_w_pb, v_w_out, v_norm2_g, v_w_ff1, v_w_ff2, v_final_g):
    nb, t_len, d = x.shape
    nl = w_in.shape[0]
    n = nb * t_len
    cq = w_ada.shape[-1]
    cc = d // N_CHIP
    mx, my, mc = _my_place()
    myq = (2 * mx + my).astype(jnp.int32).reshape(1)
    me = 4 * mx + 2 * my + mc

    def exchange_start(name, mode, src, after):
        land = lax.empty((N_DEV,) + src.shape[-2:], src.dtype)
        send_sems, recv_sems, srcs, lands, tok_out = _split_start(name + "_start", mode, [src], [land], after)
        return (name, mode, send_sems, recv_sems, srcs, lands), tok_out

    def exchange_wait(handle, after):
        name, mode, send_sems, recv_sems, srcs, lands = handle
        srcs, lands = _split_wait(name + "_wait", mode, send_sems, recv_sems, srcs, lands, after)
        own = lax.dynamic_slice_in_dim(srcs[0], me, 1, axis=0) if mode == "exchange" else srcs[0][None]
        return lax.dynamic_update_slice_in_dim(lands[0], own, me, axis=0)

    taps = jnp.pad(b_conv_w.reshape(nl, CONV_TAPS, cc), ((0, 0), (0, HALO - CONV_TAPS), (0, 0)))
    first = jnp.concatenate([jnp.pad(c, ((0, 8 - nb), (0, 0))), taps.reshape(nl * HALO * cc // d, d)], axis=0)
    first = _all_to_all(jnp.broadcast_to(first[None], (N_DEV,) + first.shape), "gather_c_and_taps")
    c_all = first[:, :nb].reshape(N_DEV * nb, d)
    cwg = first[:, 8:].reshape(N_CHIP, 2, nl, HALO, cc)[:, 0]
    cw = cwg.transpose(1, 2, 0, 3).reshape(nl, HALO, d)
    cw = cw.reshape(nl, HALO, d // LANES, LANES).transpose(0, 2, 1, 3)
    mod_part = _ada_forward(c_all, w_ada, b_ada.reshape(nl, 1, N_CHIP * cq), myq)
    mod_slots = mod_part.reshape(nl, N_DEV, nb, cq).transpose(1, 0, 2, 3).reshape(N_DEV, nl * nb, cq)
    mod_got = _all_to_all(mod_slots, "exchange_mod").reshape(N_CHIP, 2, nl, nb, cq)[:, 0]
    mod6 = mod_got.transpose(1, 2, 0, 3).reshape(nl, nb, 6, d)
    mod = jnp.pad(mod6, ((0, 0), (0, 0), (0, 2), (0, 0)))

    big = ["w_in", "w_pa", "w_pb", "w_out", "w_ff1", "w_ff2"]
    ws_given = dict(w_in=(w_in, m_w_in, v_w_in), w_pa=(w_pa, m_w_pa, v_w_pa), w_pb=(w_pb, m_w_pb, v_w_pb),
                    w_out=(w_out, m_w_out, v_w_out), w_ff1=(w_ff1, m_w_ff1, v_w_ff1), w_ff2=(w_ff2, m_w_ff2, v_w_ff2))

    def own_slot(w_l, after=None):
        if after is not None:
            w_l = w_l - after[0, 0]
        empty = lax.empty((N_CHIP,) + w_l.shape, WIRE_DTYPE)
        return lax.dynamic_update_index_in_dim(empty, w_l.astype(WIRE_DTYPE), myq[0], 0)

    def zero_after(*arrays):
        z = jnp.zeros((8, LANES), F32)
        for a in arrays:
            piece = a.reshape(-1, a.shape[-1])[:8, :LANES]
            z = z + jnp.where(jnp.isfinite(piece), piece, 0.0) * 0.0
        return z

    first_sems = _split_start("gather_start_in_0", "gather", [], [own_slot(w_in[0])], zero_after(cw, mod[:, 0]))
    token = first = first_sems[4]
    gathers = []
    for l in range(nl):
        group = big[1:] if l == 0 else big
        send_sems, recv_sems, _, lands, token = _split_start(
            f"gather_start_{l}", "gather", [], [own_slot(ws_given[k][0][l], first) for k in group], token)
        if l == 0:
            send_sems = list(first_sems[0]) + list(send_sems)
            recv_sems = list(first_sems[1]) + list(recv_sems)
            lands = list(first_sems[3]) + list(lands)
        gathers.append((send_sems, recv_sems, lands))
    mod = mod + token[0, 0]

    def gather_wait(l, part, lo, hi, after):
        send_sems, recv_sems, lands = gathers[l]
        return _split_wait(f"gather_wait_{part}_{l}", "gather", send_sems[lo:hi], recv_sems[lo:hi], [],
                           lands[lo:hi], after)[1]

    vec3 = lambda p: p.reshape(nl, 1, d)
    g1, g2 = vec3(norm1_g), vec3(norm2_g)
    lng, lnb, cb, blg, blb = vec3(a_ln_g), vec3(a_ln_b), vec3(b_conv_b), vec3(b_ln_g), vec3(b_ln_b)
    bst = a_bs.transpose(0, 2, 1)

    xs = x.reshape(n, d)
    saved = []
    weights = []
    for l in range(nl):
        if l == 0:
            send_sems, recv_sems, lands = gathers[0]
            wg_in = lands[0]
            h, proj = _in_proj_quarter("in_proj_0_own", l, xs, mod, g1, wg_in, myq, t_len)
            for k, (px, py) in enumerate(_other_chips(mx, my)):
                (wg_in,) = _split_wait(f"gather_wait_in_0_{k}", "gather", send_sems[:1], recv_sems[:1], [], [wg_in],
                                       proj, peers=(k,))[1]
                quarter = (2 * px + py).astype(jnp.int32).reshape(1)
                proj = _in_proj_quarter(f"in_proj_0_{k}", l, h, None, None, wg_in, quarter, t_len, proj)
        else:
            (wg_in,) = gather_wait(l, "in", 0, 1, xs)
            h, proj = _in_proj(l, xs, mod, g1, wg_in, t_len)
        ya_in, yb_in, zc = _branches_fwd(l, proj, lng, lnb, a_ws, bst, cw, cb, blg, blb, t_len)
        wg_pa, wg_pb, wg_out = gather_wait(l, "mid", 1, 4, ya_in)
        ya, yb, merged, o, x1 = _merge_out(l, xs, mod, proj, ya_in, yb_in, wg_pa, wg_pb, wg_out, t_len)
        wg_ff1, wg_ff2 = gather_wait(l, "ffn", 4, 6, x1)
        h2, f, o2, x2 = _ffn_fwd(l, x1, mod, g2, wg_ff1, wg_ff2, t_len)
        saved.append((xs, h, proj, ya_in, yb_in, zc, ya, yb, merged, o, x1, h2, f, o2))
        weights.append((wg_in, wg_pa, wg_pb, wg_out, wg_ff1, wg_ff2))
        xs = x2

    loss_blk, dx, dfinal = _loss_head(xs, final_g.reshape(1, d), loss_target.reshape(n, d))

    tok = lambda w: (lambda tk: pl.BlockSpec((tk, w), lambda i, j, k: (k, 0)))
    tok_i = lambda w: (lambda tk: pl.BlockSpec((tk, w), lambda i, j, k: (k, i)))
    tok_j = lambda w: (lambda tk: pl.BlockSpec((tk, w), lambda i, j, k: (k, j)))
    qin = weights[0][0].shape[-1]
    hq = weights[0][4].shape[-1]
    rq = d // N_CHIP
    slot_i = lambda r, cdim: pl.BlockSpec((None, r, cdim), lambda i, j, k: (i, 0, 0))
    slot_j = lambda r, cdim: pl.BlockSpec((None, r, cdim), lambda i, j, k: (j, 0, 0))
    all_slots = pl.BlockSpec((N_CHIP, rq, d), lambda i, j, k: (0, 0, 0))
    scatters = []

    def scatter_start(l, part, names, grads, after):
        lands = [lax.empty((3,) + g.shape[1:], g.dtype) for g in grads]
        send_sems, recv_sems, srcs, lands, tok_out = _split_start(f"scatter_start_{part}_{l}", "scatter", grads, lands,
                                                                  after)
        scatters.append((f"scatter_wait_{part}_{l}", l, names, send_sems, recv_sems, srcs, lands))
        return tok_out

    dmods, small = [None] * nl, [None] * nl
    for l in reversed(range(nl)):
        x0, h, proj, ya_in, yb_in, zc, ya, yb, merged, o, x1, h2, f, o2 = saved[l]
        wg_in, wg_pa, wg_pb, wg_out, wg_ff1, wg_ff2 = weights[l]
        do2, df, dx1, dmod_c, dg2 = _ffn_bwd(l, dx, x1, mod, g2, o2, f, wg_ff1, wg_ff2, t_len, nb)
        g_ff2 = _weight_grad(f"grad_w_ff2_{l}", f, do2, tok_i(hq), tok(d), hq, slot_i(hq, d), (hq, d), (N_CHIP, 1),
                             relu2=True)
        g_ff1 = _weight_grad(f"grad_w_ff1_{l}", h2, df, tok(d), tok_j(hq), d, slot_j(d, hq), (d, hq), (1, N_CHIP))
        if l == 0:
            token = scatter_start(l, "ffn", ["w_ff2", "w_ff1"], [g_ff2, g_ff1], token)
        do, dya, dyb, dya_in, dyb_in, dproj, dmod_b = _merge_bwd(l, dx1, mod, o, ya, yb, proj, wg_pa, wg_pb, wg_out,
                                                                 t_len, nb, token)
        g_out = _weight_grad(f"grad_w_out_{l}", merged, do, tok(d), tok(d), rq, all_slots, (d, d), (1, 1))
        g_pa = _weight_grad(f"grad_w_pa_{l}", ya_in, dya, tok(d), tok(d), rq, all_slots, (d, d), (1, 1))
        g_pb = _weight_grad(f"grad_w_pb_{l}", yb_in, dyb, tok(d), tok(d), rq, all_slots, (d, d), (1, 1))
        if l == 0:
            token = scatter_start(l, "mid", ["w_out", "w_pa", "w_pb"], [g_out, g_pa, g_pb], token)
        dproj, dws, dbst, dcw, vecs = _branches_bwd(l, proj, zc, dya_in, dyb_in, dproj, lng, lnb, a_ws, bst, cw,
                                                    blg, blb, t_len, token)
        g_in = _weight_grad(f"grad_w_in_{l}", h, dproj, tok(d), tok_j(qin), d, slot_j(d, qin), (d, qin), (1, N_CHIP))
        if l == 0:
            token = scatter_start(l, "in", ["w_in"], [g_in], token)
        else:
            token = scatter_start(l, "all", ["w_ff2", "w_ff1", "w_out", "w_pa", "w_pb", "w_in"],
                                  [g_ff2, g_ff1, g_out, g_pa, g_pb, g_in], token)
        dx, dmod_a, dg1 = _in_proj_bwd(l, dproj, dx1, x0, mod, g1, wg_in, t_len, nb, token)
        dmods[l] = jnp.concatenate([dmod_a[:, 0:2], dmod_b[:, 2:3], dmod_c[:, 3:6]], axis=1)
        dcw = dcw.transpose(1, 0, 2).reshape(HALO, d)[:CONV_TAPS]
        small[l] = (dg1[0], vecs[0], vecs[1], dws, dbst.T, dcw, vecs[2], vecs[3], vecs[4], dg2[0])
    grad_x = dx.reshape(nb, t_len, d)

    names = ["norm1_g", "a_ln_g", "a_ln_b", "a_ws", "a_bs", "b_conv_w", "b_conv_b", "b_ln_g", "b_ln_b", "norm2_g"]
    stacked = [jnp.stack([small[l][k] for l in range(nl)]) for k in range(len(names))]
    stacked[5] = jnp.pad(stacked[5], ((0, 0), (0, HALO - CONV_TAPS), (0, 0)))
    stacked += [dfinal, loss_blk]
    part_shapes = [s.shape for s in stacked]
    packed = _pack(stacked)
    prow = packed.shape[0]
    pad_rows = (-prow) % (8 * N_DEV)
    packed = jnp.pad(packed, ((0, pad_rows), (0, 0)))
    srow = packed.shape[0] // N_DEV
    half = dict.fromkeys(big)

    def sum_arrived(entries, after):
        last = after
        for name, l, group, send_sems, recv_sems, srcs, lands in entries:
            srcs, lands = _split_wait(name, "scatter", send_sems, recv_sems, srcs, lands, after)
            for k, g_own, g_got in zip(group, srcs, lands):
                last = half[k] = _sum_partials(f"sum_{k}_{l}", g_own, g_got, myq, l, nl, half[k])
        return last

    early = max(1, (nl - 1) * 2 // 3)
    reduce_handle, token = exchange_start("reduce_small", "exchange", packed.reshape(N_DEV, srow, LANES), token)
    sum_arrived(scatters[:early], token)
    summed = zero_after(*[h for h in half.values() if h is not None])
    mine = _sum_slots(exchange_wait(reduce_handle, summed), "sum_small")
    dmod_rows = nl * nb * 6 * d // LANES
    second = jnp.concatenate([mine, jnp.stack(dmods).reshape(dmod_rows, LANES)], axis=0)
    gather_handle, token = exchange_start("gather_small_and_dmod", "allgather", second, token)
    last = sum_arrived(scatters[early:], token)
    sums = [half[k] for k in big]
    swap_send, swap_recv, sums, others, token = _split_start(
        "swap_start", "swap", sums, [lax.empty(s.shape, s.dtype) for s in sums], last)
    second = exchange_wait(gather_handle, token)
    total = second[:, :srow].reshape(N_DEV * srow, LANES)[:prow]
    dmod_all = second[:, srow:].reshape(N_DEV, nl, nb, 6 * d).transpose(1, 0, 2, 3).reshape(nl, N_DEV * nb, 6 * d)
    g_w_ada, g_b_ada = _ada_backward(c_all, dmod_all, myq, cq, token)

    sg = dict(zip(names + ["final_g", "loss"], _unpack(total, part_shapes)))
    loss = sg["loss"][0, 0]
    sg["b_conv_w"] = lax.dynamic_slice_in_dim(sg["b_conv_w"][:, :CONV_TAPS], myq[0] * cc, cc, axis=2).reshape(
        nl, CONV_TAPS, 1, cc)
    sg["final_g"] = sg["final_g"][0]
    sg["b_ada"] = g_b_ada.reshape(nl, N_CHIP * cq)
    small_names = ["b_ada", "norm1_g", "a_ln_g", "a_ln_b", "a_ws", "a_bs", "b_conv_w", "b_conv_b", "b_ln_g",
                   "b_ln_b", "norm2_g", "final_g"]
    given = dict(b_ada=(b_ada, m_b_ada, v_b_ada), norm1_g=(norm1_g, m_norm1_g, v_norm1_g),
                 a_ln_g=(a_ln_g, m_a_ln_g, v_a_ln_g), a_ln_b=(a_ln_b, m_a_ln_b, v_a_ln_b),
                 a_ws=(a_ws, m_a_ws, v_a_ws), a_bs=(a_bs, m_a_bs, v_a_bs),
                 b_conv_w=(b_conv_w, m_b_conv_w, v_b_conv_w), b_conv_b=(b_conv_b, m_b_conv_b, v_b_conv_b),
                 b_ln_g=(b_ln_g, m_b_ln_g, v_b_ln_g), b_ln_b=(b_ln_b, m_b_ln_b, v_b_ln_b),
                 norm2_g=(norm2_g, m_norm2_g, v_norm2_g), final_g=(final_g, m_final_g, v_final_g))

    def padded(a):
        rows = -(-a.size // (8 * LANES)) * 8
        return jnp.pad(a.reshape(-1), (0, rows * LANES - a.size)).reshape(rows, LANES)

    packs = [_pack([padded(given[k][j]) for k in small_names]) for j in range(3)]
    gpack = _pack([padded(sg[k].astype(F32)) for k in small_names])
    res_small = _adamw("adamw_small", packs[0], packs[1], packs[2], gpack)
    out = {}
    for j, kind in enumerate(["grad", "delta", "new_m", "new_v"]):
        r = 0
        for k in small_names:
            a = given[k][0]
            rows = -(-a.size // (8 * LANES)) * 8
            out[(kind, k)] = res_small[j][r:r + rows].reshape(-1)[:a.size].reshape(a.shape)
            r += rows

    res = _adamw("adamw_w_ada", w_ada.reshape(nl * d, cq), m_w_ada.reshape(nl * d, cq), v_w_ada.reshape(nl * d, cq),
                 g_w_ada.reshape(nl * d, cq))
    for kind, r in zip(["grad", "delta", "new_m", "new_v"], res):
        out[(kind, "w_ada")] = r.reshape(w_ada.shape)

    sums, others = _split_wait("swap_wait", "swap", swap_send, swap_recv, sums, others, res[0])
    for k, s_mine, s_other in zip(big, sums, others):
        w, m, v = ws_given[k]
        cols = w.shape[-1]
        res = _adamw(f"adamw_{k}", w.reshape(-1, cols), m.reshape(-1, cols), v.reshape(-1, cols), s_mine, s_other)
        for kind, r in zip(["grad", "delta", "new_m", "new_v"], res):
            out[(kind, k)] = r.reshape(w.shape)

    order = ["w_ada", "b_ada", "norm1_g", "w_in", "a_ln_g", "a_ln_b", "a_ws", "a_bs", "w_pa", "b_conv_w", "b_conv_b",
             "b_ln_g", "b_ln_b", "w_pb", "w_out", "norm2_g", "w_ff1", "w_ff2", "final_g"]
    return (loss, grad_x, *[out[("grad", k)] for k in order], *[out[("delta", k)] for k in order],
            *[out[("new_m", k)] for k in order], *[out[("new_v", k)] for k in order])
```

```python
import jax
import jax.numpy as jnp
from jax import lax
from jax.experimental import pallas as pl
from jax.experimental.pallas import tpu as pltpu

F32 = jnp.float32
MXU_DTYPE = jnp.bfloat16
ACT_DTYPE = jnp.bfloat16
WIRE_DTYPE = jnp.bfloat16

EPS = 1e-6
CHUNK = 128
HEADS = 8
CONV_TAPS = 31
HALO = 32
N_DEV = 8
N_CHIP = 4
ADAM_LR, ADAM_B1, ADAM_B2, ADAM_EPS, ADAM_WD, ADAM_STEP = 0.001, 0.9, 0.999, 1e-08, 0.01, 10

V7X_VMEM_BYTES = 64 * 1024 * 1024
VMEM_LIMIT = V7X_VMEM_BYTES * 7 // 8
TOKEN_TILE = 512
MATMUL_TILE = 2048
FFN_BWD_TILE = 512
CONV_ROWS = 64
TAP_GRAD_ROWS = 32
LANES = 128
MESH_ID = pl.DeviceIdType.MESH


def _params(sem=None):
    return pltpu.CompilerParams(dimension_semantics=sem, vmem_limit_bytes=VMEM_LIMIT)


def _resident(shape):
    return pl.BlockSpec(shape, lambda *_: (0,) * len(shape), pipeline_mode=pl.Buffered(1))


def _dot(a, b):
    return jnp.dot(a.astype(MXU_DTYPE), b.astype(MXU_DTYPE), preferred_element_type=F32)


def _dot_nt(a, b):
    return lax.dot_general(a.astype(MXU_DTYPE), b.astype(MXU_DTYPE), (((1,), (1,)), ((), ())),
                           preferred_element_type=F32)


def _dot_tn(a, b):
    return lax.dot_general(a.astype(MXU_DTYPE), b.astype(MXU_DTYPE), (((0,), (0,)), ((), ())),
                           preferred_element_type=F32)


def _colsum(a):
    return jnp.sum(a, axis=0, keepdims=True)


def _rowmean(a):
    return jnp.mean(a, axis=-1, keepdims=True)


def _sigmoid(a):
    return 1.0 / (1.0 + jnp.exp(-a))


def _modnorm_fwd(x, g, sc, sh):
    r = lax.rsqrt(_rowmean(x * x) + EPS)
    return (x * r) * (g * (1.0 + sc)) + sh


def _modnorm_bwd(x, dh, g, sc):
    r = lax.rsqrt(_rowmean(x * x) + EPS)
    xn = x * r
    dxn = dh * (g * (1.0 + sc))
    dx = r * (dxn - xn * _rowmean(dxn * xn))
    return dx, _colsum(dh), _colsum(dh * xn)


def _ln_stats(v):
    mu = _rowmean(v)
    vc = v - mu
    rstd = lax.rsqrt(_rowmean(vc * vc) + EPS)
    return vc * rstd, rstd


def _ln_bwd(dy, vhat, rstd, g):
    dvh = dy * g
    return rstd * (dvh - _rowmean(dvh) - vhat * _rowmean(dvh * vhat))


def _causal_mask():
    row = lax.broadcasted_iota(jnp.int32, (CHUNK, CHUNK), 0)
    col = lax.broadcasted_iota(jnp.int32, (CHUNK, CHUNK), 1)
    return row >= col


def _my_place():
    return lax.axis_index("x"), lax.axis_index("y"), lax.axis_index("c")


def _other_chips(mx, my):
    return [(1 - mx, my), (mx, 1 - my), (1 - mx, 1 - my)]


def _other_devices(mx, my, mc):
    return [((mx + ((k >> 2) & 1)) % 2, (my + ((k >> 1) & 1)) % 2, (mc + (k & 1)) % 2) for k in range(1, N_DEV)]


def _all_to_all(x, name):
    assert x.shape[0] == N_DEV

    def body(x_ref, o_ref, send_sems, recv_sems):
        mx, my, mc = _my_place()
        me = 4 * mx + 2 * my + mc
        o_ref[me] = x_ref[me]
        copies = []
        for k, (px, py, pc) in enumerate(_other_devices(mx, my, mc)):
            cp = pltpu.make_async_remote_copy(
                src_ref=x_ref.at[4 * px + 2 * py + pc], dst_ref=o_ref.at[me],
                send_sem=send_sems.at[k], recv_sem=recv_sems.at[k],
                device_id=(px, py, pc), device_id_type=MESH_ID)
            cp.start()
            copies.append(cp)
        for cp in copies:
            cp.wait()

    return pl.pallas_call(
        body, name=name, out_shape=jax.ShapeDtypeStruct(x.shape, x.dtype),
        in_specs=[pl.BlockSpec(memory_space=pltpu.VMEM)],
        out_specs=pl.BlockSpec(memory_space=pltpu.VMEM),
        scratch_shapes=[pltpu.SemaphoreType.DMA((N_DEV - 1,)), pltpu.SemaphoreType.DMA((N_DEV - 1,))],
        compiler_params=pltpu.CompilerParams(vmem_limit_bytes=VMEM_LIMIT),
    )(x)


def _sum_slots(x, name):
    def body(x_ref, o_ref):
        acc = x_ref[0]
        for s in range(1, N_DEV):
            acc = acc + x_ref[s]
        o_ref[...] = acc

    return pl.pallas_call(
        body, name=name, out_shape=jax.ShapeDtypeStruct(x.shape[1:], x.dtype),
        in_specs=[pl.BlockSpec(memory_space=pltpu.VMEM)], out_specs=pl.BlockSpec(memory_space=pltpu.VMEM),
        compiler_params=pltpu.CompilerParams(vmem_limit_bytes=VMEM_LIMIT),
    )(x)


HBM_SPEC = pl.BlockSpec(memory_space=pltpu.HBM)
SEM_SPEC = pl.BlockSpec(memory_space=pltpu.SEMAPHORE)
ANY_SPEC = pl.BlockSpec(memory_space=pl.ANY)
SPLIT_EFFECT = pltpu.SideEffectType.DATAFLOW_SIDE_EFFECTING


def _quarter_copies(mode, srcs, lands, send_sems, recv_sems, peers=(0, 1, 2)):
    mx, my, mc = _my_place()
    myq = 2 * mx + my
    if mode == "swap":
        return [pltpu.make_async_remote_copy(
            src_ref=srcs[a], dst_ref=lands[a], send_sem=send_sems[a].at[0], recv_sem=recv_sems[a].at[0],
            device_id=(mx, my, 1 - mc), device_id_type=MESH_ID) for a in range(len(lands))]
    copies = []
    if mode in ("exchange", "allgather"):
        me = 4 * mx + 2 * my + mc
        for a in range(len(lands)):
            for k, (px, py, pc) in enumerate(_other_devices(mx, my, mc)):
                copies.append(pltpu.make_async_remote_copy(
                    src_ref=srcs[a].at[4 * px + 2 * py + pc] if mode == "exchange" else srcs[a],
                    dst_ref=lands[a].at[me],
                    send_sem=send_sems[a].at[k], recv_sem=recv_sems[a].at[k],
                    device_id=(px, py, pc), device_id_type=MESH_ID))
        return copies
    for a in range(len(lands)):
        for k, (px, py) in enumerate(_other_chips(mx, my)):
            if k not in peers:
                continue
            if mode == "gather":
                src, dst = lands[a].at[myq], lands[a].at[myq]
            else:
                src, dst = srcs[a].at[2 * px + py], lands[a].at[k]
            copies.append(pltpu.make_async_remote_copy(
                src_ref=src, dst_ref=dst, send_sem=send_sems[a].at[k], recv_sem=recv_sems[a].at[k],
                device_id=(px, py, mc), device_id_type=MESH_ID))
    return copies


def _split_start(name, mode, srcs, lands, after):
    ns, n = len(srcs), len(lands)

    def body(*refs):
        outs = refs[ns + n + 1:]
        for cp in _quarter_copies(mode, refs[:ns], refs[ns:ns + n], outs[:n], outs[n:2 * n]):
            cp.start()
        token = outs[-1]
        token[...] = jnp.zeros_like(token)

    arrays = list(srcs) + list(lands)
    per_array = {"swap": 1, "exchange": N_DEV - 1, "allgather": N_DEV - 1}.get(mode, 3)
    res = pl.pallas_call(
        body, name=name,
        out_shape=[pltpu.SemaphoreType.DMA((per_array,))] * (2 * n) + [pltpu.HBM(x.shape, x.dtype) for x in arrays]
        + [jax.ShapeDtypeStruct((8, LANES), F32)],
        in_specs=[HBM_SPEC] * (ns + n) + [ANY_SPEC],
        out_specs=[SEM_SPEC] * (2 * n) + [HBM_SPEC] * (ns + n) + [pl.BlockSpec(memory_space=pltpu.VMEM)],
        input_output_aliases={i: 2 * n + i for i in range(ns + n)},
        compiler_params=pltpu.CompilerParams(has_side_effects=SPLIT_EFFECT),
    )(*[pltpu.with_memory_space_constraint(x, pltpu.HBM) for x in arrays], after)
    return res[:n], res[n:2 * n], res[2 * n:2 * n + ns], res[2 * n + ns:2 * n + ns + n], res[-1]


def _split_wait(name, mode, send_sems, recv_sems, srcs, lands, after, peers=(0, 1, 2)):
    ns, n = len(srcs), len(lands)
    afters = list(after) if isinstance(after, (list, tuple)) else [after]

    def body(*refs):
        sems = refs[ns + n:ns + 3 * n]
        for cp in _quarter_copies(mode, refs[:ns], refs[ns:ns + n], sems[:n], sems[n:], peers):
            cp.wait_send()
            cp.wait_recv()

    arrays = list(srcs) + list(lands)
    res = pl.pallas_call(
        body, name=name,
        out_shape=[pltpu.HBM(x.shape, x.dtype) for x in arrays],
        in_specs=[HBM_SPEC] * (ns + n) + [SEM_SPEC] * (2 * n) + [ANY_SPEC] * len(afters),
        out_specs=[HBM_SPEC] * (ns + n),
        input_output_aliases={i: i for i in range(ns + n)},
        compiler_params=pltpu.CompilerParams(has_side_effects=SPLIT_EFFECT),
    )(*arrays, *send_sems, *recv_sems, *afters)
    return res[:ns], res[ns:]


def _ada_forward(c_all, w_ada, b_ada3, myq):
    nl, d, cq = w_ada.shape
    nb = c_all.shape[0]

    def body(q_ref, c_ref, w_ref, b_ref, o_ref):
        c = c_ref[...]
        act = c * _sigmoid(c)
        o_ref[...] = _dot(act, w_ref[...]) + b_ref[...]

    return pl.pallas_call(
        body, name="ada_forward",
        out_shape=jax.ShapeDtypeStruct((nl, nb, cq), F32),
        grid_spec=pltpu.PrefetchScalarGridSpec(
            num_scalar_prefetch=1, grid=(nl,),
            in_specs=[pl.BlockSpec((nb, d), lambda l, q: (0, 0)),
                      pl.BlockSpec((None, d, cq), lambda l, q: (l, 0, 0)),
                      pl.BlockSpec((None, 1, cq), lambda l, q: (l, 0, q[0]))],
            out_specs=pl.BlockSpec((None, nb, cq), lambda l, q: (l, 0, 0))),
        compiler_params=_params(("arbitrary",)),
    )(myq, c_all, w_ada, b_ada3)


def _ada_backward(c_all, dmod_all, myq, cq, after):
    nb, d = c_all.shape
    nl = dmod_all.shape[0]
    full = dmod_all.shape[2]

    def body(q_ref, c_ref, dq_ref, dall_ref, after_ref, gw_ref, gb_ref):
        c = c_ref[...]
        act = c * _sigmoid(c)
        gw_ref[...] = _dot_tn(act, dq_ref[...])
        gb_ref[...] = _colsum(dall_ref[...])

    return pl.pallas_call(
        body, name="ada_backward",
        out_shape=[jax.ShapeDtypeStruct((nl, d, cq), F32), jax.ShapeDtypeStruct((nl, 1, full), F32)],
        grid_spec=pltpu.PrefetchScalarGridSpec(
            num_scalar_prefetch=1, grid=(nl,),
            in_specs=[pl.BlockSpec((nb, d), lambda l, q: (0, 0)),
                      pl.BlockSpec((None, nb, cq), lambda l, q: (l, 0, q[0])),
                      pl.BlockSpec((None, nb, full), lambda l, q: (l, 0, 0)), ANY_SPEC],
            out_specs=[pl.BlockSpec((None, d, cq), lambda l, q: (l, 0, 0)),
                       pl.BlockSpec((None, 1, full), lambda l, q: (l, 0, 0))]),
        compiler_params=_params(("arbitrary",)),
    )(myq, c_all, dmod_all, dmod_all, after)


def _in_proj(l, x, mod, g1, wg_in, t_len):
    n, d = x.shape
    tm = min(TOKEN_TILE, t_len)
    tpb = t_len // tm
    qc = wg_in.shape[-1]

    def body(x_ref, mod_ref, g_ref, w_ref, h_ref, proj_ref):
        h = _modnorm_fwd(x_ref[...], g_ref[...], mod_ref[1:2, :], mod_ref[0:1, :]).astype(MXU_DTYPE)
        h_ref[...] = h.astype(ACT_DTYPE)
        for q in range(N_CHIP):
            proj_ref[:, q * qc:(q + 1) * qc] = jnp.dot(h, w_ref[q], preferred_element_type=F32).astype(ACT_DTYPE)

    return pl.pallas_call(
        body, name=f"in_proj_{l}",
        out_shape=[jax.ShapeDtypeStruct((n, d), ACT_DTYPE), jax.ShapeDtypeStruct((n, N_CHIP * qc), ACT_DTYPE)],
        grid=(n // tm,),
        in_specs=[pl.BlockSpec((tm, d), lambda i: (i, 0)),
                  pl.BlockSpec((None, None, 8, d), lambda i: (l, i // tpb, 0, 0)),
                  pl.BlockSpec((None, 1, d), lambda i: (l, 0, 0)),
                  _resident((N_CHIP, d, qc))],
        out_specs=[pl.BlockSpec((tm, d), lambda i: (i, 0)),
                   pl.BlockSpec((tm, N_CHIP * qc), lambda i: (i, 0))],
        compiler_params=_params(("arbitrary",)),
    )(x, mod, g1, wg_in)


def _in_proj_quarter(name, l, src, mod, g1, wg_in, quarter, t_len, prev=None):
    n, d = src.shape
    tm = min(TOKEN_TILE, t_len)
    tpb = t_len // tm
    qc = wg_in.shape[-1]
    first = prev is None

    def body(q_ref, *refs):
        if first:
            x_ref, mod_ref, g_ref, w_ref, h_ref, proj_ref = refs
            h = _modnorm_fwd(x_ref[...], g_ref[...], mod_ref[1:2, :], mod_ref[0:1, :]).astype(MXU_DTYPE)
            h_ref[...] = h.astype(ACT_DTYPE)
        else:
            h_in_ref, w_ref, _, proj_ref = refs
            h = h_in_ref[...].astype(MXU_DTYPE)
        proj_ref[...] = jnp.dot(h, w_ref[...], preferred_element_type=F32).astype(ACT_DTYPE)

    tile = pl.BlockSpec((tm, d), lambda i, q: (i, 0))
    w_spec = pl.BlockSpec((None, d, qc), lambda i, q: (q[0], 0, 0))
    proj_spec = pl.BlockSpec((tm, qc), lambda i, q: (i, q[0]))
    proj_shape = jax.ShapeDtypeStruct((n, N_CHIP * qc), ACT_DTYPE)
    if first:
        operands = [quarter, src, mod, g1, wg_in]
        in_specs = [tile, pl.BlockSpec((None, None, 8, d), lambda i, q: (l, i // tpb, 0, 0)),
                    pl.BlockSpec((None, 1, d), lambda i, q: (l, 0, 0)), w_spec]
        out_shape, out_specs, aliases = [jax.ShapeDtypeStruct((n, d), ACT_DTYPE), proj_shape], [tile, proj_spec], {}
    else:
        operands = [quarter, src, wg_in, prev]
        in_specs = [tile, w_spec, ANY_SPEC]
        out_shape, out_specs, aliases = proj_shape, proj_spec, {3: 0}
    return pl.pallas_call(
        body, name=name, out_shape=out_shape,
        grid_spec=pltpu.PrefetchScalarGridSpec(num_scalar_prefetch=1, grid=(n // tm,), in_specs=in_specs,
                                               out_specs=out_specs),
        input_output_aliases=aliases,
        compiler_params=_params(("arbitrary",)),
    )(*operands)


def _masked_ws(ws_ref, wm_s):
    mask = _causal_mask()
    for h in range(HEADS):
        wm_s[h] = jnp.where(mask, ws_ref[h], 0.0).astype(MXU_DTYPE)


def _fill_z(i, tpb, ah_ref, gh_ref, zext):
    ah = ah_ref[...].astype(F32)
    gh = gh_ref[...].astype(F32)
    keep = jnp.where(i % tpb == 0, 0.0, 1.0)
    _put_lanes(zext, slice(0, HALO), ah * _sigmoid(gh) * keep)


def _put_lanes(dst3, rows, value):
    for lc in range(value.shape[-1] // LANES):
        dst3[lc, rows, :] = value[:, lc * LANES:(lc + 1) * LANES]


def _tap_windows(src3, lc, base, rows, flip):
    offs = {k: (CONV_TAPS - 1 - k) if flip else (k + 2) for k in range(CONV_TAPS)}
    for r in range(8):
        taps = [k for k in offs if offs[k] % 8 == r]
        lo = min(offs[k] for k in taps)
        hi = max(offs[k] for k in taps)
        win = src3[lc, pl.ds(base + lo, hi - lo + rows), :]
        for k in taps:
            yield k, win[offs[k] - lo:offs[k] - lo + rows]


def _conv_taps(src3, w3_ref, dst3, lc, nrows, flip):
    for b in range(nrows // CONV_ROWS):
        acc = jnp.zeros((CONV_ROWS, LANES), F32)
        for k, win in _tap_windows(src3, lc, b * CONV_ROWS, CONV_ROWS, flip):
            acc = acc + win * w3_ref[lc, k:k + 1, :]
        dst3[lc, b * CONV_ROWS:(b + 1) * CONV_ROWS, :] = acc


def _branches_fwd(l, proj, lng, lnb, ws, bst, cw, cb, blg, blb, t_len):
    n = proj.shape[0]
    d = lng.shape[-1]
    tm = min(TOKEN_TILE, t_len)
    tpb = t_len // tm
    per = tm // HALO
    nchunk = tm // CHUNK

    def body(u_ref, v_ref, a_ref, g_ref, ah_ref, gh_ref, lng_ref, lnb_ref, ws_ref, bst_ref, cw_ref, cb_ref,
             blg_ref, blb_ref, ya_ref, yb_ref, zc_ref, wm_s, zext, zc3):
        i = pl.program_id(0)
        _masked_ws(ws_ref, wm_s)
        _fill_z(i, tpb, ah_ref, gh_ref, zext)

        def chunk(c, carry):
            r0 = pl.multiple_of(c * CHUNK, CHUNK)
            rows = pl.ds(r0, CHUNK)
            vhat, _ = _ln_stats(v_ref[rows, :].astype(F32))
            vn = (vhat * lng_ref[...] + lnb_ref[...]).astype(MXU_DTYPE)
            u = u_ref[rows, :].astype(F32)
            for h in range(HEADS):
                cols = slice(h * CHUNK, (h + 1) * CHUNK)
                s = jnp.dot(wm_s[h], vn[:, cols], preferred_element_type=F32) + bst_ref[:, h:h + 1]
                ya_ref[rows, cols] = (u[:, cols] * s).astype(ACT_DTYPE)
            a = a_ref[rows, :].astype(F32)
            g = g_ref[rows, :].astype(F32)
            _put_lanes(zext, pl.ds(HALO + r0, CHUNK), a * _sigmoid(g))
            return carry

        lax.fori_loop(0, nchunk, chunk, 0)

        def lane_chunk(lc, carry):
            _conv_taps(zext, cw_ref, zc3, lc, tm, flip=False)
            return carry

        lax.fori_loop(0, d // LANES, lane_chunk, 0)

        def chunk2(c, carry):
            r0 = pl.multiple_of(c * CHUNK, CHUNK)
            rows = pl.ds(r0, CHUNK)
            for lc in range(d // LANES):
                lanes = slice(lc * LANES, (lc + 1) * LANES)
                zc_ref[rows, lanes] = (zc3[lc, rows, :] + cb_ref[:, lanes]).astype(ACT_DTYPE)
            zhat, _ = _ln_stats(zc_ref[rows, :].astype(F32))
            zn = zhat * blg_ref[...] + blb_ref[...]
            yb_ref[rows, :] = (zn * _sigmoid(zn)).astype(ACT_DTYPE)
            return carry

        lax.fori_loop(0, nchunk, chunk2, 0)

    col = lambda k: pl.BlockSpec((tm, d), lambda i: (i, k))
    halo = lambda k: pl.BlockSpec((HALO, d), lambda i: (jnp.maximum(i * per - 1, 0), k))
    vec = pl.BlockSpec((None, 1, d), lambda i: (l, 0, 0))
    out = pl.BlockSpec((tm, d), lambda i: (i, 0))
    return pl.pallas_call(
        body, name=f"branches_fwd_{l}",
        out_shape=[jax.ShapeDtypeStruct((n, d), ACT_DTYPE)] * 3,
        grid=(n // tm,),
        in_specs=[col(0), col(1), col(2), col(3), halo(2), halo(3), vec, vec,
                  pl.BlockSpec((None, HEADS, CHUNK, CHUNK), lambda i: (l, 0, 0, 0)),
                  pl.BlockSpec((None, CHUNK, HEADS), lambda i: (l, 0, 0)),
                  pl.BlockSpec((None, d // LANES, HALO, LANES), lambda i: (l, 0, 0, 0)), vec, vec, vec],
        out_specs=[out, out, out],
        scratch_shapes=[pltpu.VMEM((HEADS, CHUNK, CHUNK), MXU_DTYPE), pltpu.VMEM((d // LANES, HALO + tm, LANES), F32),
                        pltpu.VMEM((d // LANES, tm, LANES), F32)],
        compiler_params=_params(("arbitrary",)),
    )(proj, proj, proj, proj, proj, proj, lng, lnb, ws, bst, cw, cb, blg, blb)


def _merge_out(l, x, mod, proj, ya_in, yb_in, wg_pa, wg_pb, wg_out, t_len):
    n, d = x.shape
    tm = min(TOKEN_TILE, t_len)
    tpb = t_len // tm
    rq = d // N_CHIP

    def body(x_ref, mod_ref, ga_ref, gb_ref, yai_ref, ybi_ref, wpa_ref, wpb_ref, wo_ref,
             ya_ref, yb_ref, mg_ref, o_ref, x1_ref):
        wpa = wpa_ref[...].reshape(d, d)
        wpb = wpb_ref[...].reshape(d, d)
        wo = wo_ref[...].reshape(d, d)
        ya = jnp.dot(yai_ref[...].astype(MXU_DTYPE), wpa, preferred_element_type=F32)
        yb = jnp.dot(ybi_ref[...].astype(MXU_DTYPE), wpb, preferred_element_type=F32)
        merged = _sigmoid(ga_ref[...].astype(F32)) * ya + _sigmoid(gb_ref[...].astype(F32)) * yb
        o = _dot(merged, wo)
        ya_ref[...] = ya.astype(ACT_DTYPE)
        yb_ref[...] = yb.astype(ACT_DTYPE)
        mg_ref[...] = merged.astype(ACT_DTYPE)
        o_ref[...] = o.astype(ACT_DTYPE)
        x1_ref[...] = x_ref[...] + mod_ref[2:3, :] * o

    tile = pl.BlockSpec((tm, d), lambda i: (i, 0))
    wspec = pl.BlockSpec((N_CHIP, rq, d), lambda i: (0, 0, 0))
    return pl.pallas_call(
        body, name=f"merge_out_{l}",
        out_shape=[jax.ShapeDtypeStruct((n, d), ACT_DTYPE)] * 4 + [jax.ShapeDtypeStruct((n, d), F32)],
        grid=(n // tm,),
        in_specs=[tile, pl.BlockSpec((None, None, 8, d), lambda i: (l, i // tpb, 0, 0)),
                  pl.BlockSpec((tm, d), lambda i: (i, 4)), pl.BlockSpec((tm, d), lambda i: (i, 5)),
                  tile, tile, wspec, wspec, wspec],
        out_specs=[tile] * 5,
        compiler_params=_params(("arbitrary",)),
    )(x, mod, proj, proj, ya_in, yb_in, wg_pa, wg_pb, wg_out)


def _ffn_fwd(l, x1, mod, g2, wg_ff1, wg_ff2, t_len):
    n, d = x1.shape
    tm = min(TOKEN_TILE, t_len)
    tpb = t_len // tm
    hq = wg_ff1.shape[-1]
    hid = N_CHIP * hq

    def body(x_ref, mod_ref, g_ref, w1_ref, w2_ref, h_ref, f_ref, o2_ref, x2_ref, a2_s):
        h = _modnorm_fwd(x_ref[...], g_ref[...], mod_ref[4:5, :], mod_ref[3:4, :]).astype(MXU_DTYPE)
        h_ref[...] = h.astype(ACT_DTYPE)
        for q in range(N_CHIP):
            cols = slice(q * hq, (q + 1) * hq)
            f = jnp.dot(h, w1_ref[q], preferred_element_type=F32)
            f_ref[:, cols] = f.astype(ACT_DTYPE)
            a2_s[:, cols] = jnp.square(jnp.maximum(f, 0.0)).astype(MXU_DTYPE)
        o2 = jnp.dot(a2_s[...], w2_ref[...].reshape(hid, d), preferred_element_type=F32)
        o2_ref[...] = o2.astype(ACT_DTYPE)
        x2_ref[...] = x_ref[...] + mod_ref[5:6, :] * o2

    tile = pl.BlockSpec((tm, d), lambda i: (i, 0))
    return pl.pallas_call(
        body, name=f"ffn_fwd_{l}",
        out_shape=[jax.ShapeDtypeStruct((n, d), ACT_DTYPE), jax.ShapeDtypeStruct((n, hid), ACT_DTYPE),
                   jax.ShapeDtypeStruct((n, d), ACT_DTYPE), jax.ShapeDtypeStruct((n, d), F32)],
        grid=(n // tm,),
        in_specs=[tile, pl.BlockSpec((None, None, 8, d), lambda i: (l, i // tpb, 0, 0)),
                  pl.BlockSpec((None, 1, d), lambda i: (l, 0, 0)),
                  _resident((N_CHIP, d, hq)), _resident((N_CHIP, hq, d))],
        out_specs=[tile, pl.BlockSpec((tm, hid), lambda i: (i, 0)), tile, tile],
        scratch_shapes=[pltpu.VMEM((tm, hid), MXU_DTYPE)],
        compiler_params=_params(("arbitrary",)),
    )(x1, mod, g2, wg_ff1, wg_ff2)


def _loss_head(x, final_g, target):
    n, d = x.shape
    tm = min(TOKEN_TILE, n)

    def body(x_ref, g_ref, t_ref, loss_ref, dx_ref, dg_ref):
        @pl.when(pl.program_id(0) == 0)
        def _():
            loss_ref[...] = jnp.zeros_like(loss_ref)
            dg_ref[...] = jnp.zeros_like(dg_ref)

        x_t = x_ref[...]
        g = g_ref[...]
        r = lax.rsqrt(_rowmean(x_t * x_t) + EPS)
        xn = x_t * r
        e = xn * g - t_ref[...]
        loss_ref[...] += jnp.sum(e * e) * (0.5 / d)
        dy = e * (1.0 / d)
        dxn = dy * g
        dx_ref[...] = r * (dxn - xn * _rowmean(dxn * xn))
        dg_ref[0:1, :] += _colsum(dy * xn)

    tile = pl.BlockSpec((tm, d), lambda i: (i, 0))
    return pl.pallas_call(
        body, name="loss_head",
        out_shape=[jax.ShapeDtypeStruct((8, LANES), F32), jax.ShapeDtypeStruct((n, d), F32),
                   jax.ShapeDtypeStruct((8, d), F32)],
        grid=(n // tm,),
        in_specs=[tile, pl.BlockSpec((1, d), lambda i: (0, 0)), tile],
        out_specs=[pl.BlockSpec((8, LANES), lambda i: (0, 0)), tile, pl.BlockSpec((8, d), lambda i: (0, 0))],
        compiler_params=_params(("arbitrary",)),
    )(x, final_g, target)


def _norm_tail(x_ref, dxin_ref, dh, g_ref, sc, dx_ref, dmod_ref, dg_ref, row_sh, row_sc):
    dxm, dsh, q = _modnorm_bwd(x_ref[...], dh, g_ref[...], sc)
    dx_ref[...] = dxin_ref[...] + dxm
    dmod_ref[row_sh:row_sh + 1, :] += dsh
    dmod_ref[row_sc:row_sc + 1, :] += g_ref[...] * q
    dg_ref[0:1, :] += (1.0 + sc) * q


def _ffn_bwd(l, dx2, x1, mod, g2, o2, f, wg_ff1, wg_ff2, t_len, nb):
    n, d = dx2.shape
    tm = min(FFN_BWD_TILE, t_len)
    tpb = t_len // tm
    hq = wg_ff1.shape[-1]
    hid = N_CHIP * hq

    def body(dx2_ref, x1_ref, mod_ref, g_ref, o2_ref, f_ref, w1_ref, w2_ref,
             do2_ref, df_ref, dx1_ref, dmod_ref, dg_ref):
        i = pl.program_id(0)

        @pl.when(i == 0)
        def _():
            dg_ref[...] = jnp.zeros_like(dg_ref)

        @pl.when(i % tpb == 0)
        def _():
            dmod_ref[...] = jnp.zeros_like(dmod_ref)

        dx2_t = dx2_ref[...]
        dmod_ref[5:6, :] += _colsum(dx2_t * o2_ref[...].astype(F32))
        do2 = (dx2_t * mod_ref[5:6, :]).astype(MXU_DTYPE)
        do2_ref[...] = do2.astype(ACT_DTYPE)
        dh = jnp.zeros((tm, d), F32)
        for q in range(N_CHIP):
            cols = slice(q * hq, (q + 1) * hq)
            da2 = _dot_nt(do2, w2_ref[q])
            df = (da2 * (2.0 * jnp.maximum(f_ref[:, cols].astype(F32), 0.0))).astype(MXU_DTYPE)
            df_ref[:, cols] = df.astype(ACT_DTYPE)
            dh = dh + _dot_nt(df, w1_ref[q])
        _norm_tail(x1_ref, dx2_ref, dh, g_ref, mod_ref[4:5, :], dx1_ref, dmod_ref, dg_ref, 3, 4)

    tile = pl.BlockSpec((tm, d), lambda i: (i, 0))
    wide = pl.BlockSpec((tm, hid), lambda i: (i, 0))
    return pl.pallas_call(
        body, name=f"ffn_bwd_{l}",
        out_shape=[jax.ShapeDtypeStruct((n, d), ACT_DTYPE), jax.ShapeDtypeStruct((n, hid), ACT_DTYPE),
                   jax.ShapeDtypeStruct((n, d), F32), jax.ShapeDtypeStruct((nb, 8, d), F32),
                   jax.ShapeDtypeStruct((8, d), F32)],
        grid=(n // tm,),
        in_specs=[tile, tile, pl.BlockSpec((None, None, 8, d), lambda i: (l, i // tpb, 0, 0)),
                  pl.BlockSpec((None, 1, d), lambda i: (l, 0, 0)), tile, wide,
                  _resident((N_CHIP, d, hq)), _resident((N_CHIP, hq, d))],
        out_specs=[tile, wide, tile, pl.BlockSpec((None, 8, d), lambda i: (i // tpb, 0, 0)),
                   pl.BlockSpec((8, d), lambda i: (0, 0))],
        compiler_params=_params(("arbitrary",)),
    )(dx2, x1, mod, g2, o2, f, wg_ff1, wg_ff2)


def _merge_bwd(l, dx1, mod, o, ya, yb, proj, wg_pa, wg_pb, wg_out, t_len, nb, after):
    n, d = dx1.shape
    tm = min(TOKEN_TILE, t_len)
    tpb = t_len // tm
    rq = d // N_CHIP

    def body(dx_ref, mod_ref, o_ref, ya_ref, yb_ref, ga_ref, gb_ref, wpa_ref, wpb_ref, wo_ref, after_ref,
             do_ref, dya_ref, dyb_ref, dyai_ref, dybi_ref, dproj_ref, dmod_ref):
        i = pl.program_id(0)

        @pl.when(i % tpb == 0)
        def _():
            dmod_ref[...] = jnp.zeros_like(dmod_ref)

        dx = dx_ref[...]
        dmod_ref[2:3, :] += _colsum(dx * o_ref[...].astype(F32))
        do = (dx * mod_ref[2:3, :]).astype(MXU_DTYPE)
        do_ref[...] = do.astype(ACT_DTYPE)
        dm = _dot_nt(do, wo_ref[...].reshape(d, d))
        sa = _sigmoid(ga_ref[...].astype(F32))
        sb = _sigmoid(gb_ref[...].astype(F32))
        dya = (dm * sa).astype(MXU_DTYPE)
        dyb = (dm * sb).astype(MXU_DTYPE)
        dya_ref[...] = dya.astype(ACT_DTYPE)
        dyb_ref[...] = dyb.astype(ACT_DTYPE)
        dproj_ref[:, 0:d] = (dm * ya_ref[...].astype(F32) * sa * (1.0 - sa)).astype(ACT_DTYPE)
        dproj_ref[:, d:2 * d] = (dm * yb_ref[...].astype(F32) * sb * (1.0 - sb)).astype(ACT_DTYPE)
        dyai_ref[...] = _dot_nt(dya, wpa_ref[...].reshape(d, d)).astype(ACT_DTYPE)
        dybi_ref[...] = _dot_nt(dyb, wpb_ref[...].reshape(d, d)).astype(ACT_DTYPE)

    tile = pl.BlockSpec((tm, d), lambda i: (i, 0))
    wspec = pl.BlockSpec((N_CHIP, rq, d), lambda i: (0, 0, 0))
    return pl.pallas_call(
        body, name=f"merge_bwd_{l}",
        out_shape=[jax.ShapeDtypeStruct((n, d), ACT_DTYPE)] * 5
        + [jax.ShapeDtypeStruct((n, 6 * d), ACT_DTYPE), jax.ShapeDtypeStruct((nb, 8, d), F32)],
        grid=(n // tm,),
        in_specs=[tile, pl.BlockSpec((None, None, 8, d), lambda i: (l, i // tpb, 0, 0)), tile, tile, tile,
                  pl.BlockSpec((tm, d), lambda i: (i, 4)), pl.BlockSpec((tm, d), lambda i: (i, 5)),
                  wspec, wspec, wspec, ANY_SPEC],
        out_specs=[tile] * 5 + [pl.BlockSpec((tm, 2 * d), lambda i: (i, 2)),
                                pl.BlockSpec((None, 8, d), lambda i: (i // tpb, 0, 0))],
        compiler_params=_params(("arbitrary",)),
    )(dx1, mod, o, ya, yb, proj, proj, wg_pa, wg_pb, wg_out, after)


def _branches_bwd(l, proj, zc, dya_in, dyb_in, dproj, lng, lnb, ws, bst, cw, blg, blb, t_len, after):
    n = proj.shape[0]
    d = lng.shape[-1]
    tm = min(TOKEN_TILE, t_len)
    tpb = t_len // tm
    per = tm // HALO
    nchunk = tm // CHUNK
    ntile = n // tm

    def body(u_ref, v_ref, a_ref, g_ref, ah_ref, gh_ref, zc_ref, zcn_ref, dya_ref, dyb_ref, dybn_ref, dproj_in,
             lng_ref, lnb_ref, ws_ref, bst_ref, cw_ref, blg_ref, blb_ref, after_ref,
             dproj_ref, dws_ref, dbst_ref, dcw_ref, vec_ref, wm_s, zext, dzext, dz3, dvn_s):
        i = pl.program_id(0)

        @pl.when(i == 0)
        def _():
            dws_ref[...] = jnp.zeros_like(dws_ref)
            dbst_ref[...] = jnp.zeros_like(dbst_ref)
            dcw_ref[...] = jnp.zeros_like(dcw_ref)
            vec_ref[...] = jnp.zeros_like(vec_ref)

        _masked_ws(ws_ref, wm_s)
        _fill_z(i, tpb, ah_ref, gh_ref, zext)

        def conv_ln_bwd(zc_t, dyb_t):
            zhat, rstd = _ln_stats(zc_t)
            zn = zhat * blg_ref[...] + blb_ref[...]
            sg = _sigmoid(zn)
            dzn = dyb_t * (sg * (1.0 + zn * (1.0 - sg)))
            return _ln_bwd(dzn, zhat, rstd, blg_ref[...]), _colsum(dzn * zhat), _colsum(dzn)

        def chunk(c, carry):
            r0 = pl.multiple_of(c * CHUNK, CHUNK)
            rows = pl.ds(r0, CHUNK)
            vhat, rstd = _ln_stats(v_ref[rows, :].astype(F32))
            vn = (vhat * lng_ref[...] + lnb_ref[...]).astype(MXU_DTYPE)
            u = u_ref[rows, :].astype(F32)
            dya = dya_ref[rows, :].astype(F32)
            for h in range(HEADS):
                cols = slice(h * CHUNK, (h + 1) * CHUNK)
                s = jnp.dot(wm_s[h], vn[:, cols], preferred_element_type=F32) + bst_ref[:, h:h + 1]
                dproj_ref[rows, cols] = (dya[:, cols] * s).astype(ACT_DTYPE)
                ds = dya[:, cols] * u[:, cols]
                dvn_s[:, cols] = _dot_tn(wm_s[h], ds)
                dws_ref[h] += _dot_nt(ds, vn[:, cols])
                dbst_ref[:, h:h + 1] += jnp.sum(ds, axis=1, keepdims=True)
            dvn = dvn_s[...]
            dproj_ref[rows, d:2 * d] = _ln_bwd(dvn, vhat, rstd, lng_ref[...]).astype(ACT_DTYPE)
            vec_ref[0:1, :] += _colsum(dvn * vhat)
            vec_ref[1:2, :] += _colsum(dvn)
            a = a_ref[rows, :].astype(F32)
            g = g_ref[rows, :].astype(F32)
            _put_lanes(zext, pl.ds(HALO + r0, CHUNK), a * _sigmoid(g))
            dzc, dblg, dblb = conv_ln_bwd(zc_ref[rows, :].astype(F32), dyb_ref[rows, :].astype(F32))
            _put_lanes(dzext, rows, dzc)
            vec_ref[2:3, :] += _colsum(dzc)
            vec_ref[3:4, :] += dblg
            vec_ref[4:5, :] += dblb
            return carry

        lax.fori_loop(0, nchunk, chunk, 0)

        dzc_next, _, _ = conv_ln_bwd(zcn_ref[...].astype(F32), dybn_ref[...].astype(F32))
        _put_lanes(dzext, slice(tm, tm + HALO), dzc_next * jnp.where(i % tpb == tpb - 1, 0.0, 1.0))

        def lane_chunk_dz(lc, carry):
            _conv_taps(dzext, cw_ref, dz3, lc, tm, flip=True)
            return carry

        lax.fori_loop(0, d // LANES, lane_chunk_dz, 0)

        def lane_chunk(lc, carry):
            accs = [jnp.zeros((8, LANES), F32) for _ in range(CONV_TAPS)]
            for b in range(tm // TAP_GRAD_ROWS):
                dzc = dzext[lc, b * TAP_GRAD_ROWS:(b + 1) * TAP_GRAD_ROWS, :]
                for k, win in _tap_windows(zext, lc, b * TAP_GRAD_ROWS, TAP_GRAD_ROWS, flip=False):
                    prod = dzc * win
                    part = prod[0:8]
                    for e in range(1, TAP_GRAD_ROWS // 8):
                        part = part + prod[8 * e:8 * e + 8]
                    accs[k] = accs[k] + part
            for k in range(CONV_TAPS):
                dcw_ref[lc, k:k + 1, :] += _colsum(accs[k])
            return carry

        lax.fori_loop(0, d // LANES, lane_chunk, 0)

        def glu_bwd(c, carry):
            r0 = pl.multiple_of(c * CHUNK, CHUNK)
            rows = pl.ds(r0, CHUNK)
            for lc in range(d // LANES):
                lanes = slice(lc * LANES, (lc + 1) * LANES)
                dz = dz3[lc, rows, :]
                a = a_ref[rows, lanes].astype(F32)
                sg = _sigmoid(g_ref[rows, lanes].astype(F32))
                dproj_ref[rows, 2 * d + lc * LANES:2 * d + (lc + 1) * LANES] = (dz * sg).astype(ACT_DTYPE)
                dproj_ref[rows, 3 * d + lc * LANES:3 * d + (lc + 1) * LANES] = (
                    dz * a * sg * (1.0 - sg)).astype(ACT_DTYPE)
            return carry

        lax.fori_loop(0, nchunk, glu_bwd, 0)

        @pl.when(i == ntile - 1)
        def _():
            mask = _causal_mask()
            for h in range(HEADS):
                dws_ref[h] = jnp.where(mask, dws_ref[h], 0.0)

    col = lambda k: pl.BlockSpec((tm, d), lambda i: (i, k))
    tile = pl.BlockSpec((tm, d), lambda i: (i, 0))
    before = lambda k: pl.BlockSpec((HALO, d), lambda i: (jnp.maximum(i * per - 1, 0), k))
    following = pl.BlockSpec((HALO, d), lambda i: (jnp.minimum((i + 1) * per, n // HALO - 1), 0))
    vec = pl.BlockSpec((None, 1, d), lambda i: (l, 0, 0))
    const2 = lambda r, c: pl.BlockSpec((r, c), lambda i: (0, 0))
    return pl.pallas_call(
        body, name=f"branches_bwd_{l}",
        out_shape=[jax.ShapeDtypeStruct((n, 6 * d), ACT_DTYPE), jax.ShapeDtypeStruct((HEADS, CHUNK, CHUNK), F32),
                   jax.ShapeDtypeStruct((CHUNK, HEADS), F32), jax.ShapeDtypeStruct((d // LANES, HALO, LANES), F32),
                   jax.ShapeDtypeStruct((8, d), F32)],
        grid=(ntile,),
        in_specs=[col(0), col(1), col(2), col(3), before(2), before(3), tile, following, tile, tile, following,
                  pl.BlockSpec(memory_space=pl.ANY), vec, vec,
                  pl.BlockSpec((None, HEADS, CHUNK, CHUNK), lambda i: (l, 0, 0, 0)),
                  pl.BlockSpec((None, CHUNK, HEADS), lambda i: (l, 0, 0)),
                  pl.BlockSpec((None, d // LANES, HALO, LANES), lambda i: (l, 0, 0, 0)), vec, vec, ANY_SPEC],
        out_specs=[pl.BlockSpec((tm, 4 * d), lambda i: (i, 0)),
                   pl.BlockSpec((HEADS, CHUNK, CHUNK), lambda i: (0, 0, 0)),
                   const2(CHUNK, HEADS), pl.BlockSpec((d // LANES, HALO, LANES), lambda i: (0, 0, 0)), const2(8, d)],
        scratch_shapes=[pltpu.VMEM((HEADS, CHUNK, CHUNK), MXU_DTYPE),
                        pltpu.VMEM((d // LANES, HALO + tm, LANES), F32),
                        pltpu.VMEM((d // LANES, tm + HALO, LANES), F32),
                        pltpu.VMEM((d // LANES, tm, LANES), F32), pltpu.VMEM((CHUNK, d), F32)],
        input_output_aliases={11: 0},
        compiler_params=_params(("arbitrary",)),
    )(proj, proj, proj, proj, proj, proj, zc, zc, dya_in, dyb_in, dyb_in, dproj, lng, lnb, ws, bst, cw, blg, blb, after)


def _in_proj_bwd(l, dproj, dx1, x, mod, g1, wg_in, t_len, nb, after):
    n, d = x.shape
    tm = min(TOKEN_TILE, t_len)
    tpb = t_len // tm
    qc = wg_in.shape[-1]

    def body(dp_ref, dx1_ref, x_ref, mod_ref, g_ref, w_ref, after_ref, dx_ref, dmod_ref, dg_ref):
        i = pl.program_id(0)

        @pl.when(i == 0)
        def _():
            dg_ref[...] = jnp.zeros_like(dg_ref)

        @pl.when(i % tpb == 0)
        def _():
            dmod_ref[...] = jnp.zeros_like(dmod_ref)

        dh = jnp.zeros((tm, d), F32)
        for q in range(N_CHIP):
            dh = dh + _dot_nt(dp_ref[:, q * qc:(q + 1) * qc], w_ref[q])
        _norm_tail(x_ref, dx1_ref, dh, g_ref, mod_ref[1:2, :], dx_ref, dmod_ref, dg_ref, 0, 1)

    tile = pl.BlockSpec((tm, d), lambda i: (i, 0))
    return pl.pallas_call(
        body, name=f"in_proj_bwd_{l}",
        out_shape=[jax.ShapeDtypeStruct((n, d), F32), jax.ShapeDtypeStruct((nb, 8, d), F32),
                   jax.ShapeDtypeStruct((8, d), F32)],
        grid=(n // tm,),
        in_specs=[pl.BlockSpec((tm, N_CHIP * qc), lambda i: (i, 0)), tile, tile,
                  pl.BlockSpec((None, None, 8, d), lambda i: (l, i // tpb, 0, 0)),
                  pl.BlockSpec((None, 1, d), lambda i: (l, 0, 0)),
                  _resident((N_CHIP, d, qc)), ANY_SPEC],
        out_specs=[tile, pl.BlockSpec((None, 8, d), lambda i: (i // tpb, 0, 0)),
                   pl.BlockSpec((8, d), lambda i: (0, 0))],
        compiler_params=_params(("arbitrary",)),
    )(dproj, dx1, x, mod, g1, wg_in, after)


def _weight_grad(name, a, b, a_spec, b_spec, out_rows, out_spec, acc_shape, grid_ij, relu2=False):
    n = a.shape[0]
    tk = min(MATMUL_TILE, n)
    nk = n // tk
    cols = acc_shape[1]

    def body(a_ref, b_ref, o_ref, acc):
        k = pl.program_id(2)

        @pl.when(k == 0)
        def _():
            acc[...] = jnp.zeros_like(acc)

        a_t = a_ref[...]
        if relu2:
            a_t = jnp.square(jnp.maximum(a_t.astype(F32), 0.0))
        acc[...] += _dot_tn(a_t, b_ref[...])

        @pl.when(k == nk - 1)
        def _():
            o_ref[...] = acc[...].reshape(o_ref.shape).astype(WIRE_DTYPE)

    gi, gj = grid_ij
    return pl.pallas_call(
        body, name=name, out_shape=jax.ShapeDtypeStruct((N_CHIP, out_rows, cols), WIRE_DTYPE),
        grid=(gi, gj, nk),
        in_specs=[a_spec(tk), b_spec(tk)],
        out_specs=out_spec,
        scratch_shapes=[pltpu.VMEM(acc_shape, F32)],
        compiler_params=_params(("arbitrary", "arbitrary", "arbitrary")),
    )(a, b)


def _row_tile(rows, cols, arrays):
    budget = VMEM_LIMIT // 3
    t = budget // (arrays * 2 * cols * 4)
    t = max(8, min(rows, t // 8 * 8))
    while rows % t:
        t -= 8
    return t


def _sum_partials(name, own, got, myq, l, nl, prev):
    _, rows, cols = own.shape
    tr = _row_tile(rows, cols, 3)
    nt = rows // tr

    def body(q_ref, own_ref, got_ref, *rest):
        o_ref = rest[-1]
        acc = own_ref[...].astype(F32)
        for k in range(3):
            acc = acc + got_ref[k].astype(F32)
        o_ref[...] = acc

    operands = [myq, own, got] + ([] if prev is None else [prev])
    return pl.pallas_call(
        body, name=name, out_shape=jax.ShapeDtypeStruct((nl * rows, cols), F32),
        grid_spec=pltpu.PrefetchScalarGridSpec(
            num_scalar_prefetch=1, grid=(nt,),
            in_specs=[pl.BlockSpec((None, tr, cols), lambda i, q: (q[0], i, 0)),
                      pl.BlockSpec((3, tr, cols), lambda i, q: (0, i, 0))]
            + ([] if prev is None else [pl.BlockSpec(memory_space=pl.ANY)]),
            out_specs=pl.BlockSpec((tr, cols), lambda i, q: (l * nt + i, 0))),
        input_output_aliases={} if prev is None else {3: 0},
        compiler_params=_params(("arbitrary",)),
    )(*operands)


def _adamw(name, w, m, v, g_a, g_b=None):
    rows, cols = w.shape
    tr = _row_tile(rows, cols, 9)
    c1 = 1.0 - ADAM_B1 ** ADAM_STEP
    c2 = 1.0 - ADAM_B2 ** ADAM_STEP

    def body(*refs):
        if g_b is None:
            w_ref, m_ref, v_ref, ga_ref, g_ref, d_ref, m2_ref, v2_ref = refs
            g = ga_ref[...]
        else:
            w_ref, m_ref, v_ref, ga_ref, gb_ref, g_ref, d_ref, m2_ref, v2_ref = refs
            g = ga_ref[...] + gb_ref[...]
        m2 = ADAM_B1 * m_ref[...] + (1.0 - ADAM_B1) * g
        v2 = ADAM_B2 * v_ref[...] + (1.0 - ADAM_B2) * (g * g)
        g_ref[...] = g
        m2_ref[...] = m2
        v2_ref[...] = v2
        d_ref[...] = -ADAM_LR * ((m2 / c1) / (jnp.sqrt(v2 / c2) + ADAM_EPS) + ADAM_WD * w_ref[...])

    tile = pl.BlockSpec((tr, cols), lambda i: (i, 0))
    operands = [w, m, v, g_a] + ([] if g_b is None else [g_b])
    return pl.pallas_call(
        body, name=name, out_shape=[jax.ShapeDtypeStruct((rows, cols), F32)] * 4,
        grid=(rows // tr,), in_specs=[tile] * len(operands), out_specs=[tile] * 4,
        compiler_params=_params(("arbitrary",)),
    )(*operands)


def _pack(parts):
    flat = [p.reshape(-1, LANES) for p in parts]
    for f in flat:
        assert f.shape[0] % 8 == 0
    return jnp.concatenate(flat, axis=0)


def _unpack(packed, shapes):
    out, r = [], 0
    for s in shapes:
        size = 1
        for e in s:
            size *= e
        rows = size // LANES
        out.append(packed[r:r + rows].reshape(s))
        r += rows
    return out


def kernel(x, c, w_ada, b_ada, norm1_g, w_in, a_ln_g, a_ln_b, a_ws, a_bs, w_pa, b_conv_w, b_conv_b, b_ln_g, b_ln_b, w_pb, w_out, norm2_g, w_ff1, w_ff2, final_g, loss_target, m_w_ada, m_b_ada, m_norm1_g, m_w_in, m_a_ln_g, m_a_ln_b, m_a_ws, m_a_bs, m_w_pa, m_b_conv_w, m_b_conv_b, m_b_ln_g, m_b_ln_b, m_w_pb, m_w_out, m_norm2_g, m_w_ff1, m_w_ff2, m_final_g, v_w_ada, v_b_ada, v_norm1_g, v_w_in, v_a_ln_g, v_a_ln_b, v_a_ws, v_a_bs, v_w_pa, v_b_conv_w, v_b_conv_b, v_b_ln_g, v_b_ln_b, v_w_pb, v_w_out, v_norm2_g, v_w_ff1, v_w_ff2, v_final_g):
    nb, t_len, d = x.shape
    nl = w_in.shape[0]
    n = nb * t_len
    cq = w_ada.shape[-1]
    cc = d // N_CHIP
    mx, my, mc = _my_place()
    myq = (2 * mx + my).astype(jnp.int32).reshape(1)
    me = 4 * mx + 2 * my + mc

    def exchange_start(name, mode, src, after):
        land = lax.empty((N_DEV,) + src.shape[-2:], src.dtype)
        send_sems, recv_sems, srcs, lands, tok_out = _split_start(name + "_start", mode, [src], [land], after)
        return (name, mode, send_sems, recv_sems, srcs, lands), tok_out

    def exchange_wait(handle, after):
        name, mode, send_sems, recv_sems, srcs, lands = handle
        srcs, lands = _split_wait(name + "_wait", mode, send_sems, recv_sems, srcs, lands, after)
        own = lax.dynamic_slice_in_dim(srcs[0], me, 1, axis=0) if mode == "exchange" else srcs[0][None]
        return lax.dynamic_update_slice_in_dim(lands[0], own, me, axis=0)

    taps = jnp.pad(b_conv_w.reshape(nl, CONV_TAPS, cc), ((0, 0), (0, HALO - CONV_TAPS), (0, 0)))
    first = jnp.concatenate([jnp.pad(c, ((0, 8 - nb), (0, 0))), taps.reshape(nl * HALO * cc // d, d)], axis=0)
    first = _all_to_all(jnp.broadcast_to(first[None], (N_DEV,) + first.shape), "gather_c_and_taps")
    c_all = first[:, :nb].reshape(N_DEV * nb, d)
    cwg = first[:, 8:].reshape(N_CHIP, 2, nl, HALO, cc)[:, 0]
    cw = cwg.transpose(1, 2, 0, 3).reshape(nl, HALO, d)
    cw = cw.reshape(nl, HALO, d // LANES, LANES).transpose(0, 2, 1, 3)
    mod_part = _ada_forward(c_all, w_ada, b_ada.reshape(nl, 1, N_CHIP * cq), myq)
    mod_slots = mod_part.reshape(nl, N_DEV, nb, cq).transpose(1, 0, 2, 3).reshape(N_DEV, nl * nb, cq)
    mod_got = _all_to_all(mod_slots, "exchange_mod").reshape(N_CHIP, 2, nl, nb, cq)[:, 0]
    mod6 = mod_got.transpose(1, 2, 0, 3).reshape(nl, nb, 6, d)
    mod = jnp.pad(mod6, ((0, 0), (0, 0), (0, 2), (0, 0)))

    big = ["w_in", "w_pa", "w_pb", "w_out", "w_ff1", "w_ff2"]
    ws_given = dict(w_in=(w_in, m_w_in, v_w_in), w_pa=(w_pa, m_w_pa, v_w_pa), w_pb=(w_pb, m_w_pb, v_w_pb),
                    w_out=(w_out, m_w_out, v_w_out), w_ff1=(w_ff1, m_w_ff1, v_w_ff1), w_ff2=(w_ff2, m_w_ff2, v_w_ff2))

    def own_slot(w_l, after=None):
        if after is not None:
            w_l = w_l - after[0, 0]
        empty = lax.empty((N_CHIP,) + w_l.shape, WIRE_DTYPE)
        return lax.dynamic_update_index_in_dim(empty, w_l.astype(WIRE_DTYPE), myq[0], 0)

    def zero_after(*arrays):
        z = jnp.zeros((8, LANES), F32)
        for a in arrays:
            piece = a.reshape(-1, a.shape[-1])[:8, :LANES]
            z = z + jnp.where(jnp.isfinite(piece), piece, 0.0) * 0.0
        return z

    first_sems = _split_start("gather_start_in_0", "gather", [], [own_slot(w_in[0])], zero_after(cw, mod[:, 0]))
    token = first = first_sems[4]
    gathers = []
    for l in range(nl):
        group = big[1:] if l == 0 else big
        send_sems, recv_sems, _, lands, token = _split_start(
            f"gather_start_{l}", "gather", [], [own_slot(ws_given[k][0][l], first) for k in group], token)
        if l == 0:
            send_sems = list(first_sems[0]) + list(send_sems)
            recv_sems = list(first_sems[1]) + list(recv_sems)
            lands = list(first_sems[3]) + list(lands)
        gathers.append((send_sems, recv_sems, lands))
    mod = mod + token[0, 0]

    def gather_wait(l, part, lo, hi, after):
        send_sems, recv_sems, lands = gathers[l]
        return _split_wait(f"gather_wait_{part}_{l}", "gather", send_sems[lo:hi], recv_sems[lo:hi], [],
                           lands[lo:hi], after)[1]

    vec3 = lambda p: p.reshape(nl, 1, d)
    g1, g2 = vec3(norm1_g), vec3(norm2_g)
    lng, lnb, cb, blg, blb = vec3(a_ln_g), vec3(a_ln_b), vec3(b_conv_b), vec3(b_ln_g), vec3(b_ln_b)
    bst = a_bs.transpose(0, 2, 1)

    xs = x.reshape(n, d)
    saved = []
    weights = []
    for l in range(nl):
        if l == 0:
            send_sems, recv_sems, lands = gathers[0]
            wg_in = lands[0]
            h, proj = _in_proj_quarter("in_proj_0_own", l, xs, mod, g1, wg_in, myq, t_len)
            for k, (px, py) in enumerate(_other_chips(mx, my)):
                (wg_in,) = _split_wait(f"gather_wait_in_0_{k}", "gather", send_sems[:1], recv_sems[:1], [], [wg_in],
                                       proj, peers=(k,))[1]
                quarter = (2 * px + py).astype(jnp.int32).reshape(1)
                proj = _in_proj_quarter(f"in_proj_0_{k}", l, h, None, None, wg_in, quarter, t_len, proj)
        else:
            (wg_in,) = gather_wait(l, "in", 0, 1, xs)
            h, proj = _in_proj(l, xs, mod, g1, wg_in, t_len)
        ya_in, yb_in, zc = _branches_fwd(l, proj, lng, lnb, a_ws, bst, cw, cb, blg, blb, t_len)
        wg_pa, wg_pb, wg_out = gather_wait(l, "mid", 1, 4, ya_in)
        ya, yb, merged, o, x1 = _merge_out(l, xs, mod, proj, ya_in, yb_in, wg_pa, wg_pb, wg_out, t_len)
        wg_ff1, wg_ff2 = gather_wait(l, "ffn", 4, 6, x1)
        h2, f, o2, x2 = _ffn_fwd(l, x1, mod, g2, wg_ff1, wg_ff2, t_len)
        saved.append((xs, h, proj, ya_in, yb_in, zc, ya, yb, merged, o, x1, h2, f, o2))
        weights.append((wg_in, wg_pa, wg_pb, wg_out, wg_ff1, wg_ff2))
        xs = x2

    loss_blk, dx, dfinal = _loss_head(xs, final_g.reshape(1, d), loss_target.reshape(n, d))

    tok = lambda w: (lambda tk: pl.BlockSpec((tk, w), lambda i, j, k: (k, 0)))
    tok_i = lambda w: (lambda tk: pl.BlockSpec((tk, w), lambda i, j, k: (k, i)))
    tok_j = lambda w: (lambda tk: pl.BlockSpec((tk, w), lambda i, j, k: (k, j)))
    qin = weights[0][0].shape[-1]
    hq = weights[0][4].shape[-1]
    rq = d // N_CHIP
    slot_i = lambda r, cdim: pl.BlockSpec((None, r, cdim), lambda i, j, k: (i, 0, 0))
    slot_j = lambda r, cdim: pl.BlockSpec((None, r, cdim), lambda i, j, k: (j, 0, 0))
    all_slots = pl.BlockSpec((N_CHIP, rq, d), lambda i, j, k: (0, 0, 0))
    scatters = []

    def scatter_start(l, part, names, grads, after):
        lands = [lax.empty((3,) + g.shape[1:], g.dtype) for g in grads]
        send_sems, recv_sems, srcs, lands, tok_out = _split_start(f"scatter_start_{part}_{l}", "scatter", grads, lands,
                                                                  after)
        scatters.append((f"scatter_wait_{part}_{l}", l, names, send_sems, recv_sems, srcs, lands))
        return tok_out

    dmods, small = [None] * nl, [None] * nl
    for l in reversed(range(nl)):
        x0, h, proj, ya_in, yb_in, zc, ya, yb, merged, o, x1, h2, f, o2 = saved[l]
        wg_in, wg_pa, wg_pb, wg_out, wg_ff1, wg_ff2 = weights[l]
        do2, df, dx1, dmod_c, dg2 = _ffn_bwd(l, dx, x1, mod, g2, o2, f, wg_ff1, wg_ff2, t_len, nb)
        g_ff2 = _weight_grad(f"grad_w_ff2_{l}", f, do2, tok_i(hq), tok(d), hq, slot_i(hq, d), (hq, d), (N_CHIP, 1),
                             relu2=True)
        g_ff1 = _weight_grad(f"grad_w_ff1_{l}", h2, df, tok(d), tok_j(hq), d, slot_j(d, hq), (d, hq), (1, N_CHIP))
        if l == 0:
            token = scatter_start(l, "ffn", ["w_ff2", "w_ff1"], [g_ff2, g_ff1], token)
        do, dya, dyb, dya_in, dyb_in, dproj, dmod_b = _merge_bwd(l, dx1, mod, o, ya, yb, proj, wg_pa, wg_pb, wg_out,
                                                                 t_len, nb, token)
        g_out = _weight_grad(f"grad_w_out_{l}", merged, do, tok(d), tok(d), rq, all_slots, (d, d), (1, 1))
        g_pa = _weight_grad(f"grad_w_pa_{l}", ya_in, dya, tok(d), tok(d), rq, all_slots, (d, d), (1, 1))
        g_pb = _weight_grad(f"grad_w_pb_{l}", yb_in, dyb, tok(d), tok(d), rq, all_slots, (d, d), (1, 1))
        if l == 0:
            token = scatter_start(l, "mid", ["w_out", "w_pa", "w_pb"], [g_out, g_pa, g_pb], token)
        dproj, dws, dbst, dcw, vecs = _branches_bwd(l, proj, zc, dya_in, dyb_in, dproj, lng, lnb, a_ws, bst, cw,
                                                    blg, blb, t_len, token)
        g_in = _weight_grad(f"grad_w_in_{l}", h, dproj, tok(d), tok_j(qin), d, slot_j(d, qin), (d, qin), (1, N_CHIP))
        if l == 0:
            token = scatter_start(l, "in", ["w_in"], [g_in], token)
        else:
            token = scatter_start(l, "all", ["w_ff2", "w_ff1", "w_out", "w_pa", "w_pb", "w_in"],
                                  [g_ff2, g_ff1, g_out, g_pa, g_pb, g_in], token)
        dx, dmod_a, dg1 = _in_proj_bwd(l, dproj, dx1, x0, mod, g1, wg_in, t_len, nb, token)
        dmods[l] = jnp.concatenate([dmod_a[:, 0:2], dmod_b[:, 2:3], dmod_c[:, 3:6]], axis=1)
        dcw = dcw.transpose(1, 0, 2).reshape(HALO, d)[:CONV_TAPS]
        small[l] = (dg1[0], vecs[0], vecs[1], dws, dbst.T, dcw, vecs[2], vecs[3], vecs[4], dg2[0])
    grad_x = dx.reshape(nb, t_len, d)

    names = ["norm1_g", "a_ln_g", "a_ln_b", "a_ws", "a_bs", "b_conv_w", "b_conv_b", "b_ln_g", "b_ln_b", "norm2_g"]
    stacked = [jnp.stack([small[l][k] for l in range(nl)]) for k in range(len(names))]
    stacked[5] = jnp.pad(stacked[5], ((0, 0), (0, HALO - CONV_TAPS), (0, 0)))
    stacked += [dfinal, loss_blk]
    part_shapes = [s.shape for s in stacked]
    packed = _pack(stacked)
    prow = packed.shape[0]
    pad_rows = (-prow) % (8 * N_DEV)
    packed = jnp.pad(packed, ((0, pad_rows), (0, 0)))
    srow = packed.shape[0] // N_DEV
    half = dict.fromkeys(big)

    def sum_arrived(entries, after):
        last = after
        for name, l, group, send_sems, recv_sems, srcs, lands in entries:
            srcs, lands = _split_wait(name, "scatter", send_sems, recv_sems, srcs, lands, after)
            for k, g_own, g_got in zip(group, srcs, lands):
                last = half[k] = _sum_partials(f"sum_{k}_{l}", g_own, g_got, myq, l, nl, half[k])
        return last

    early = max(1, (nl - 1) * 2 // 3)
    reduce_handle, token = exchange_start("reduce_small", "exchange", packed.reshape(N_DEV, srow, LANES), token)
    sum_arrived(scatters[:early], token)
    mine = _sum_slots(exchange_wait(reduce_handle, [h for h in half.values() if h is not None]), "sum_small")
    dmod_rows = nl * nb * 6 * d // LANES
    second = jnp.concatenate([mine, jnp.stack(dmods).reshape(dmod_rows, LANES)], axis=0)
    gather_handle, token = exchange_start("gather_small_and_dmod", "allgather", second, token)
    sum_arrived(scatters[early:], token)
    sums = [half[k] for k in big]
    swap_send, swap_recv, sums, others, token = _split_start(
        "swap_start", "swap", sums, [lax.empty(s.shape, s.dtype) for s in sums], token)
    second = exchange_wait(gather_handle, token)
    total = second[:, :srow].reshape(N_DEV * srow, LANES)[:prow]
    dmod_all = second[:, srow:].reshape(N_DEV, nl, nb, 6 * d).transpose(1, 0, 2, 3).reshape(nl, N_DEV * nb, 6 * d)
    g_w_ada, g_b_ada = _ada_backward(c_all, dmod_all, myq, cq, token)

    sg = dict(zip(names + ["final_g", "loss"], _unpack(total, part_shapes)))
    loss = sg["loss"][0, 0]
    sg["b_conv_w"] = lax.dynamic_slice_in_dim(sg["b_conv_w"][:, :CONV_TAPS], myq[0] * cc, cc, axis=2).reshape(
        nl, CONV_TAPS, 1, cc)
    sg["final_g"] = sg["final_g"][0]
    sg["b_ada"] = g_b_ada.reshape(nl, N_CHIP * cq)
    small_names = ["b_ada", "norm1_g", "a_ln_g", "a_ln_b", "a_ws", "a_bs", "b_conv_w", "b_conv_b", "b_ln_g",
                   "b_ln_b", "norm2_g", "final_g"]
    given = dict(b_ada=(b_ada, m_b_ada, v_b_ada), norm1_g=(norm1_g, m_norm1_g, v_norm1_g),
                 a_ln_g=(a_ln_g, m_a_ln_g, v_a_ln_g), a_ln_b=(a_ln_b, m_a_ln_b, v_a_ln_b),
                 a_ws=(a_ws, m_a_ws, v_a_ws), a_bs=(a_bs, m_a_bs, v_a_bs),
                 b_conv_w=(b_conv_w, m_b_conv_w, v_b_conv_w), b_conv_b=(b_conv_b, m_b_conv_b, v_b_conv_b),
                 b_ln_g=(b_ln_g, m_b_ln_g, v_b_ln_g), b_ln_b=(b_ln_b, m_b_ln_b, v_b_ln_b),
                 norm2_g=(norm2_g, m_norm2_g, v_norm2_g), final_g=(final_g, m_final_g, v_final_g))

    def padded(a):
        rows = -(-a.size // (8 * LANES)) * 8
        return jnp.pad(a.reshape(-1), (0, rows * LANES - a.size)).reshape(rows, LANES)

    packs = [_pack([padded(given[k][j]) for k in small_names]) for j in range(3)]
    gpack = _pack([padded(sg[k].astype(F32)) for k in small_names])
    res_small = _adamw("adamw_small", packs[0], packs[1], packs[2], gpack)
    out = {}
    for j, kind in enumerate(["grad", "delta", "new_m", "new_v"]):
        r = 0
        for k in small_names:
            a = given[k][0]
            rows = -(-a.size // (8 * LANES)) * 8
            out[(kind, k)] = res_small[j][r:r + rows].reshape(-1)[:a.size].reshape(a.shape)
            r += rows

    res = _adamw("adamw_w_ada", w_ada.reshape(nl * d, cq), m_w_ada.reshape(nl * d, cq), v_w_ada.reshape(nl * d, cq),
                 g_w_ada.reshape(nl * d, cq))
    for kind, r in zip(["grad", "delta", "new_m", "new_v"], res):
        out[(kind, "w_ada")] = r.reshape(w_ada.shape)

    sums, others = _split_wait("swap_wait", "swap", swap_send, swap_recv, sums, others, res[0])
    for k, s_mine, s_other in zip(big, sums, others):
        w, m, v = ws_given[k]
        cols = w.shape[-1]
        res = _adamw(f"adamw_{k}", w.reshape(-1, cols), m.reshape(-1, cols), v.reshape(-1, cols), s_mine, s_other)
        for kind, r in zip(["grad", "delta", "new_m", "new_v"], res):
            out[(kind, k)] = r.reshape(w.shape)

    order = ["w_ada", "b_ada", "norm1_g", "w_in", "a_ln_g", "a_ln_b", "a_ws", "a_bs", "w_pa", "b_conv_w", "b_conv_b",
             "b_ln_g", "b_ln_b", "w_pb", "w_out", "norm2_g", "w_ff1", "w_ff2", "final_g"]
    return (loss, grad_x, *[out[("grad", k)] for k in order], *[out[("delta", k)] for k in order],
            *[out[("new_m", k)] for k in order], *[out[("new_v", k)] for k in order])
```

```python
import jax
import jax.numpy as jnp
from jax import lax
from jax.experimental import pallas as pl
from jax.experimental.pallas import tpu as pltpu

F32 = jnp.float32
MXU_DTYPE = jnp.bfloat16
ACT_DTYPE = jnp.bfloat16
WIRE_DTYPE = jnp.bfloat16

EPS = 1e-6
CHUNK = 128
HEADS = 8
CONV_TAPS = 31
HALO = 32
N_DEV = 8
N_CHIP = 4
ADAM_LR, ADAM_B1, ADAM_B2, ADAM_EPS, ADAM_WD, ADAM_STEP = 0.001, 0.9, 0.999, 1e-08, 0.01, 10

V7X_VMEM_BYTES = 64 * 1024 * 1024
VMEM_LIMIT = V7X_VMEM_BYTES * 7 // 8
TOKEN_TILE = 512
MATMUL_TILE = 2048
FFN_BWD_TILE = 512
CONV_ROWS = 64
TAP_GRAD_ROWS = 32
LANES = 128
MESH_ID = pl.DeviceIdType.MESH


def _params(sem=None):
    return pltpu.CompilerParams(dimension_semantics=sem, vmem_limit_bytes=VMEM_LIMIT)


def _resident(shape):
    return pl.BlockSpec(shape, lambda *_: (0,) * len(shape), pipeline_mode=pl.Buffered(1))


def _dot(a, b):
    return jnp.dot(a.astype(MXU_DTYPE), b.astype(MXU_DTYPE), preferred_element_type=F32)


def _dot_nt(a, b):
    return lax.dot_general(a.astype(MXU_DTYPE), b.astype(MXU_DTYPE), (((1,), (1,)), ((), ())),
                           preferred_element_type=F32)


def _dot_tn(a, b):
    return lax.dot_general(a.astype(MXU_DTYPE), b.astype(MXU_DTYPE), (((0,), (0,)), ((), ())),
                           preferred_element_type=F32)


def _colsum(a):
    return jnp.sum(a, axis=0, keepdims=True)


def _rowmean(a):
    return jnp.mean(a, axis=-1, keepdims=True)


def _sigmoid(a):
    return 1.0 / (1.0 + jnp.exp(-a))


def _modnorm_fwd(x, g, sc, sh):
    r = lax.rsqrt(_rowmean(x * x) + EPS)
    return (x * r) * (g * (1.0 + sc)) + sh


def _modnorm_bwd(x, dh, g, sc):
    r = lax.rsqrt(_rowmean(x * x) + EPS)
    xn = x * r
    dxn = dh * (g * (1.0 + sc))
    dx = r * (dxn - xn * _rowmean(dxn * xn))
    return dx, _colsum(dh), _colsum(dh * xn)


def _ln_stats(v):
    mu = _rowmean(v)
    vc = v - mu
    rstd = lax.rsqrt(_rowmean(vc * vc) + EPS)
    return vc * rstd, rstd


def _ln_bwd(dy, vhat, rstd, g):
    dvh = dy * g
    return rstd * (dvh - _rowmean(dvh) - vhat * _rowmean(dvh * vhat))


def _causal_mask():
    row = lax.broadcasted_iota(jnp.int32, (CHUNK, CHUNK), 0)
    col = lax.broadcasted_iota(jnp.int32, (CHUNK, CHUNK), 1)
    return row >= col


def _my_place():
    return lax.axis_index("x"), lax.axis_index("y"), lax.axis_index("c")


def _other_chips(mx, my):
    return [(1 - mx, my), (mx, 1 - my), (1 - mx, 1 - my)]


def _other_devices(mx, my, mc):
    return [((mx + ((k >> 2) & 1)) % 2, (my + ((k >> 1) & 1)) % 2, (mc + (k & 1)) % 2) for k in range(1, N_DEV)]


def _all_to_all(x, name):
    assert x.shape[0] == N_DEV

    def body(x_ref, o_ref, send_sems, recv_sems):
        mx, my, mc = _my_place()
        me = 4 * mx + 2 * my + mc
        o_ref[me] = x_ref[me]
        copies = []
        for k, (px, py, pc) in enumerate(_other_devices(mx, my, mc)):
            cp = pltpu.make_async_remote_copy(
                src_ref=x_ref.at[4 * px + 2 * py + pc], dst_ref=o_ref.at[me],
                send_sem=send_sems.at[k], recv_sem=recv_sems.at[k],
                device_id=(px, py, pc), device_id_type=MESH_ID)
            cp.start()
            copies.append(cp)
        for cp in copies:
            cp.wait()

    return pl.pallas_call(
        body, name=name, out_shape=jax.ShapeDtypeStruct(x.shape, x.dtype),
        in_specs=[pl.BlockSpec(memory_space=pltpu.VMEM)],
        out_specs=pl.BlockSpec(memory_space=pltpu.VMEM),
        scratch_shapes=[pltpu.SemaphoreType.DMA((N_DEV - 1,)), pltpu.SemaphoreType.DMA((N_DEV - 1,))],
        compiler_params=pltpu.CompilerParams(vmem_limit_bytes=VMEM_LIMIT),
    )(x)


def _sum_slots(x, name):
    def body(x_ref, o_ref):
        acc = x_ref[0]
        for s in range(1, N_DEV):
            acc = acc + x_ref[s]
        o_ref[...] = acc

    return pl.pallas_call(
        body, name=name, out_shape=jax.ShapeDtypeStruct(x.shape[1:], x.dtype),
        in_specs=[pl.BlockSpec(memory_space=pltpu.VMEM)], out_specs=pl.BlockSpec(memory_space=pltpu.VMEM),
        compiler_params=pltpu.CompilerParams(vmem_limit_bytes=VMEM_LIMIT),
    )(x)


HBM_SPEC = pl.BlockSpec(memory_space=pltpu.HBM)
SEM_SPEC = pl.BlockSpec(memory_space=pltpu.SEMAPHORE)
ANY_SPEC = pl.BlockSpec(memory_space=pl.ANY)
SPLIT_EFFECT = pltpu.SideEffectType.DATAFLOW_SIDE_EFFECTING


def _quarter_copies(mode, srcs, lands, send_sems, recv_sems, peers=(0, 1, 2)):
    mx, my, mc = _my_place()
    myq = 2 * mx + my
    if mode == "swap":
        return [pltpu.make_async_remote_copy(
            src_ref=srcs[a], dst_ref=lands[a], send_sem=send_sems[a].at[0], recv_sem=recv_sems[a].at[0],
            device_id=(mx, my, 1 - mc), device_id_type=MESH_ID) for a in range(len(lands))]
    copies = []
    if mode in ("exchange", "allgather"):
        me = 4 * mx + 2 * my + mc
        for a in range(len(lands)):
            for k, (px, py, pc) in enumerate(_other_devices(mx, my, mc)):
                copies.append(pltpu.make_async_remote_copy(
                    src_ref=srcs[a].at[4 * px + 2 * py + pc] if mode == "exchange" else srcs[a],
                    dst_ref=lands[a].at[me],
                    send_sem=send_sems[a].at[k], recv_sem=recv_sems[a].at[k],
                    device_id=(px, py, pc), device_id_type=MESH_ID))
        return copies
    for a in range(len(lands)):
        for k, (px, py) in enumerate(_other_chips(mx, my)):
            if k not in peers:
                continue
            if mode == "gather":
                src, dst = lands[a].at[myq], lands[a].at[myq]
            else:
                src, dst = srcs[a].at[2 * px + py], lands[a].at[k]
            copies.append(pltpu.make_async_remote_copy(
                src_ref=src, dst_ref=dst, send_sem=send_sems[a].at[k], recv_sem=recv_sems[a].at[k],
                device_id=(px, py, mc), device_id_type=MESH_ID))
    return copies


def _split_start(name, mode, srcs, lands, after):
    ns, n = len(srcs), len(lands)

    def body(*refs):
        outs = refs[ns + n + 1:]
        for cp in _quarter_copies(mode, refs[:ns], refs[ns:ns + n], outs[:n], outs[n:2 * n]):
            cp.start()
        token = outs[-1]
        token[...] = jnp.zeros_like(token)

    arrays = list(srcs) + list(lands)
    per_array = {"swap": 1, "exchange": N_DEV - 1, "allgather": N_DEV - 1}.get(mode, 3)
    res = pl.pallas_call(
        body, name=name,
        out_shape=[pltpu.SemaphoreType.DMA((per_array,))] * (2 * n) + [pltpu.HBM(x.shape, x.dtype) for x in arrays]
        + [jax.ShapeDtypeStruct((8, LANES), F32)],
        in_specs=[HBM_SPEC] * (ns + n) + [ANY_SPEC],
        out_specs=[SEM_SPEC] * (2 * n) + [HBM_SPEC] * (ns + n) + [pl.BlockSpec(memory_space=pltpu.VMEM)],
        input_output_aliases={i: 2 * n + i for i in range(ns + n)},
        compiler_params=pltpu.CompilerParams(has_side_effects=SPLIT_EFFECT),
    )(*[pltpu.with_memory_space_constraint(x, pltpu.HBM) for x in arrays], after)
    return res[:n], res[n:2 * n], res[2 * n:2 * n + ns], res[2 * n + ns:2 * n + ns + n], res[-1]


def _split_wait(name, mode, send_sems, recv_sems, srcs, lands, after, peers=(0, 1, 2)):
    ns, n = len(srcs), len(lands)
    afters = list(after) if isinstance(after, (list, tuple)) else [after]

    def body(*refs):
        sems = refs[ns + n:ns + 3 * n]
        for cp in _quarter_copies(mode, refs[:ns], refs[ns:ns + n], sems[:n], sems[n:], peers):
            cp.wait_send()
            cp.wait_recv()

    arrays = list(srcs) + list(lands)
    res = pl.pallas_call(
        body, name=name,
        out_shape=[pltpu.HBM(x.shape, x.dtype) for x in arrays],
        in_specs=[HBM_SPEC] * (ns + n) + [SEM_SPEC] * (2 * n) + [ANY_SPEC] * len(afters),
        out_specs=[HBM_SPEC] * (ns + n),
        input_output_aliases={i: i for i in range(ns + n)},
        compiler_params=pltpu.CompilerParams(has_side_effects=SPLIT_EFFECT),
    )(*arrays, *send_sems, *recv_sems, *afters)
    return res[:ns], res[ns:]


def _ada_forward(c_all, w_ada, b_ada3, myq):
    nl, d, cq = w_ada.shape
    nb = c_all.shape[0]

    def body(q_ref, c_ref, w_ref, b_ref, o_ref):
        c = c_ref[...]
        act = c * _sigmoid(c)
        o_ref[...] = _dot(act, w_ref[...]) + b_ref[...]

    return pl.pallas_call(
        body, name="ada_forward",
        out_shape=jax.ShapeDtypeStruct((nl, nb, cq), F32),
        grid_spec=pltpu.PrefetchScalarGridSpec(
            num_scalar_prefetch=1, grid=(nl,),
            in_specs=[pl.BlockSpec((nb, d), lambda l, q: (0, 0)),
                      pl.BlockSpec((None, d, cq), lambda l, q: (l, 0, 0)),
                      pl.BlockSpec((None, 1, cq), lambda l, q: (l, 0, q[0]))],
            out_specs=pl.BlockSpec((None, nb, cq), lambda l, q: (l, 0, 0))),
        compiler_params=_params(("arbitrary",)),
    )(myq, c_all, w_ada, b_ada3)


def _ada_backward(c_all, dmod_all, myq, cq, after):
    nb, d = c_all.shape
    nl = dmod_all.shape[0]
    full = dmod_all.shape[2]

    def body(q_ref, c_ref, dq_ref, dall_ref, after_ref, gw_ref, gb_ref):
        c = c_ref[...]
        act = c * _sigmoid(c)
        gw_ref[...] = _dot_tn(act, dq_ref[...])
        gb_ref[...] = _colsum(dall_ref[...])

    return pl.pallas_call(
        body, name="ada_backward",
        out_shape=[jax.ShapeDtypeStruct((nl, d, cq), F32), jax.ShapeDtypeStruct((nl, 1, full), F32)],
        grid_spec=pltpu.PrefetchScalarGridSpec(
            num_scalar_prefetch=1, grid=(nl,),
            in_specs=[pl.BlockSpec((nb, d), lambda l, q: (0, 0)),
                      pl.BlockSpec((None, nb, cq), lambda l, q: (l, 0, q[0])),
                      pl.BlockSpec((None, nb, full), lambda l, q: (l, 0, 0)), ANY_SPEC],
            out_specs=[pl.BlockSpec((None, d, cq), lambda l, q: (l, 0, 0)),
                       pl.BlockSpec((None, 1, full), lambda l, q: (l, 0, 0))]),
        compiler_params=_params(("arbitrary",)),
    )(myq, c_all, dmod_all, dmod_all, after)


def _in_proj(l, x, mod, g1, wg_in, t_len):
    n, d = x.shape
    tm = min(TOKEN_TILE, t_len)
    tpb = t_len // tm
    qc = wg_in.shape[-1]

    def body(x_ref, mod_ref, g_ref, w_ref, h_ref, proj_ref):
        h = _modnorm_fwd(x_ref[...], g_ref[...], mod_ref[1:2, :], mod_ref[0:1, :]).astype(MXU_DTYPE)
        h_ref[...] = h.astype(ACT_DTYPE)
        for q in range(N_CHIP):
            proj_ref[:, q * qc:(q + 1) * qc] = jnp.dot(h, w_ref[q], preferred_element_type=F32).astype(ACT_DTYPE)

    return pl.pallas_call(
        body, name=f"in_proj_{l}",
        out_shape=[jax.ShapeDtypeStruct((n, d), ACT_DTYPE), jax.ShapeDtypeStruct((n, N_CHIP * qc), ACT_DTYPE)],
        grid=(n // tm,),
        in_specs=[pl.BlockSpec((tm, d), lambda i: (i, 0)),
                  pl.BlockSpec((None, None, 8, d), lambda i: (l, i // tpb, 0, 0)),
                  pl.BlockSpec((None, 1, d), lambda i: (l, 0, 0)),
                  _resident((N_CHIP, d, qc))],
        out_specs=[pl.BlockSpec((tm, d), lambda i: (i, 0)),
                   pl.BlockSpec((tm, N_CHIP * qc), lambda i: (i, 0))],
        compiler_params=_params(("arbitrary",)),
    )(x, mod, g1, wg_in)


def _in_proj_quarter(name, l, src, mod, g1, wg_in, quarter, t_len, prev=None):
    n, d = src.shape
    tm = min(TOKEN_TILE, t_len)
    tpb = t_len // tm
    qc = wg_in.shape[-1]
    first = prev is None

    def body(q_ref, *refs):
        if first:
            x_ref, mod_ref, g_ref, w_ref, h_ref, proj_ref = refs
            h = _modnorm_fwd(x_ref[...], g_ref[...], mod_ref[1:2, :], mod_ref[0:1, :]).astype(MXU_DTYPE)
            h_ref[...] = h.astype(ACT_DTYPE)
        else:
            h_in_ref, w_ref, _, proj_ref = refs
            h = h_in_ref[...].astype(MXU_DTYPE)
        proj_ref[...] = jnp.dot(h, w_ref[...], preferred_element_type=F32).astype(ACT_DTYPE)

    tile = pl.BlockSpec((tm, d), lambda i, q: (i, 0))
    w_spec = pl.BlockSpec((None, d, qc), lambda i, q: (q[0], 0, 0))
    proj_spec = pl.BlockSpec((tm, qc), lambda i, q: (i, q[0]))
    proj_shape = jax.ShapeDtypeStruct((n, N_CHIP * qc), ACT_DTYPE)
    if first:
        operands = [quarter, src, mod, g1, wg_in]
        in_specs = [tile, pl.BlockSpec((None, None, 8, d), lambda i, q: (l, i // tpb, 0, 0)),
                    pl.BlockSpec((None, 1, d), lambda i, q: (l, 0, 0)), w_spec]
        out_shape, out_specs, aliases = [jax.ShapeDtypeStruct((n, d), ACT_DTYPE), proj_shape], [tile, proj_spec], {}
    else:
        operands = [quarter, src, wg_in, prev]
        in_specs = [tile, w_spec, ANY_SPEC]
        out_shape, out_specs, aliases = proj_shape, proj_spec, {3: 0}
    return pl.pallas_call(
        body, name=name, out_shape=out_shape,
        grid_spec=pltpu.PrefetchScalarGridSpec(num_scalar_prefetch=1, grid=(n // tm,), in_specs=in_specs,
                                               out_specs=out_specs),
        input_output_aliases=aliases,
        compiler_params=_params(("arbitrary",)),
    )(*operands)


def _masked_ws(ws_ref, wm_s):
    mask = _causal_mask()
    for h in range(HEADS):
        wm_s[h] = jnp.where(mask, ws_ref[h], 0.0).astype(MXU_DTYPE)


def _fill_z(i, tpb, ah_ref, gh_ref, zext):
    ah = ah_ref[...].astype(F32)
    gh = gh_ref[...].astype(F32)
    keep = jnp.where(i % tpb == 0, 0.0, 1.0)
    _put_lanes(zext, slice(0, HALO), ah * _sigmoid(gh) * keep)


def _put_lanes(dst3, rows, value):
    for lc in range(value.shape[-1] // LANES):
        dst3[lc, rows, :] = value[:, lc * LANES:(lc + 1) * LANES]


def _tap_windows(src3, lc, base, rows, flip):
    offs = {k: (CONV_TAPS - 1 - k) if flip else (k + 2) for k in range(CONV_TAPS)}
    for r in range(8):
        taps = [k for k in offs if offs[k] % 8 == r]
        lo = min(offs[k] for k in taps)
        hi = max(offs[k] for k in taps)
        win = src3[lc, pl.ds(base + lo, hi - lo + rows), :]
        for k in taps:
            yield k, win[offs[k] - lo:offs[k] - lo + rows]


def _conv_taps(src3, w3_ref, dst3, lc, nrows, flip):
    for b in range(nrows // CONV_ROWS):
        acc = jnp.zeros((CONV_ROWS, LANES), F32)
        for k, win in _tap_windows(src3, lc, b * CONV_ROWS, CONV_ROWS, flip):
            acc = acc + win * w3_ref[lc, k:k + 1, :]
        dst3[lc, b * CONV_ROWS:(b + 1) * CONV_ROWS, :] = acc


def _branches_fwd(l, proj, lng, lnb, ws, bst, cw, cb, blg, blb, t_len):
    n = proj.shape[0]
    d = lng.shape[-1]
    tm = min(TOKEN_TILE, t_len)
    tpb = t_len // tm
    per = tm // HALO
    nchunk = tm // CHUNK

    def body(u_ref, v_ref, a_ref, g_ref, ah_ref, gh_ref, lng_ref, lnb_ref, ws_ref, bst_ref, cw_ref, cb_ref,
             blg_ref, blb_ref, ya_ref, yb_ref, zc_ref, wm_s, zext, zc3):
        i = pl.program_id(0)
        _masked_ws(ws_ref, wm_s)
        _fill_z(i, tpb, ah_ref, gh_ref, zext)

        def chunk(c, carry):
            r0 = pl.multiple_of(c * CHUNK, CHUNK)
            rows = pl.ds(r0, CHUNK)
            vhat, _ = _ln_stats(v_ref[rows, :].astype(F32))
            vn = (vhat * lng_ref[...] + lnb_ref[...]).astype(MXU_DTYPE)
            u = u_ref[rows, :].astype(F32)
            for h in range(HEADS):
                cols = slice(h * CHUNK, (h + 1) * CHUNK)
                s = jnp.dot(wm_s[h], vn[:, cols], preferred_element_type=F32) + bst_ref[:, h:h + 1]
                ya_ref[rows, cols] = (u[:, cols] * s).astype(ACT_DTYPE)
            a = a_ref[rows, :].astype(F32)
            g = g_ref[rows, :].astype(F32)
            _put_lanes(zext, pl.ds(HALO + r0, CHUNK), a * _sigmoid(g))
            return carry

        lax.fori_loop(0, nchunk, chunk, 0)

        def lane_chunk(lc, carry):
            _conv_taps(zext, cw_ref, zc3, lc, tm, flip=False)
            return carry

        lax.fori_loop(0, d // LANES, lane_chunk, 0)

        def chunk2(c, carry):
            r0 = pl.multiple_of(c * CHUNK, CHUNK)
            rows = pl.ds(r0, CHUNK)
            for lc in range(d // LANES):
                lanes = slice(lc * LANES, (lc + 1) * LANES)
                zc_ref[rows, lanes] = (zc3[lc, rows, :] + cb_ref[:, lanes]).astype(ACT_DTYPE)
            zhat, _ = _ln_stats(zc_ref[rows, :].astype(F32))
            zn = zhat * blg_ref[...] + blb_ref[...]
            yb_ref[rows, :] = (zn * _sigmoid(zn)).astype(ACT_DTYPE)
            return carry

        lax.fori_loop(0, nchunk, chunk2, 0)

    col = lambda k: pl.BlockSpec((tm, d), lambda i: (i, k))
    halo = lambda k: pl.BlockSpec((HALO, d), lambda i: (jnp.maximum(i * per - 1, 0), k))
    vec = pl.BlockSpec((None, 1, d), lambda i: (l, 0, 0))
    out = pl.BlockSpec((tm, d), lambda i: (i, 0))
    return pl.pallas_call(
        body, name=f"branches_fwd_{l}",
        out_shape=[jax.ShapeDtypeStruct((n, d), ACT_DTYPE)] * 3,
        grid=(n // tm,),
        in_specs=[col(0), col(1), col(2), col(3), halo(2), halo(3), vec, vec,
                  pl.BlockSpec((None, HEADS, CHUNK, CHUNK), lambda i: (l, 0, 0, 0)),
                  pl.BlockSpec((None, CHUNK, HEADS), lambda i: (l, 0, 0)),
                  pl.BlockSpec((None, d // LANES, HALO, LANES), lambda i: (l, 0, 0, 0)), vec, vec, vec],
        out_specs=[out, out, out],
        scratch_shapes=[pltpu.VMEM((HEADS, CHUNK, CHUNK), MXU_DTYPE), pltpu.VMEM((d // LANES, HALO + tm, LANES), F32),
                        pltpu.VMEM((d // LANES, tm, LANES), F32)],
        compiler_params=_params(("arbitrary",)),
    )(proj, proj, proj, proj, proj, proj, lng, lnb, ws, bst, cw, cb, blg, blb)


def _merge_out(l, x, mod, proj, ya_in, yb_in, wg_pa, wg_pb, wg_out, t_len):
    n, d = x.shape
    tm = min(TOKEN_TILE, t_len)
    tpb = t_len // tm
    rq = d // N_CHIP

    def body(x_ref, mod_ref, ga_ref, gb_ref, yai_ref, ybi_ref, wpa_ref, wpb_ref, wo_ref,
             ya_ref, yb_ref, mg_ref, o_ref, x1_ref):
        wpa = wpa_ref[...].reshape(d, d)
        wpb = wpb_ref[...].reshape(d, d)
        wo = wo_ref[...].reshape(d, d)
        ya = jnp.dot(yai_ref[...].astype(MXU_DTYPE), wpa, preferred_element_type=F32)
        yb = jnp.dot(ybi_ref[...].astype(MXU_DTYPE), wpb, preferred_element_type=F32)
        merged = _sigmoid(ga_ref[...].astype(F32)) * ya + _sigmoid(gb_ref[...].astype(F32)) * yb
        o = _dot(merged, wo)
        ya_ref[...] = ya.astype(ACT_DTYPE)
        yb_ref[...] = yb.astype(ACT_DTYPE)
        mg_ref[...] = merged.astype(ACT_DTYPE)
        o_ref[...] = o.astype(ACT_DTYPE)
        x1_ref[...] = x_ref[...] + mod_ref[2:3, :] * o

    tile = pl.BlockSpec((tm, d), lambda i: (i, 0))
    wspec = pl.BlockSpec((N_CHIP, rq, d), lambda i: (0, 0, 0))
    return pl.pallas_call(
        body, name=f"merge_out_{l}",
        out_shape=[jax.ShapeDtypeStruct((n, d), ACT_DTYPE)] * 4 + [jax.ShapeDtypeStruct((n, d), F32)],
        grid=(n // tm,),
        in_specs=[tile, pl.BlockSpec((None, None, 8, d), lambda i: (l, i // tpb, 0, 0)),
                  pl.BlockSpec((tm, d), lambda i: (i, 4)), pl.BlockSpec((tm, d), lambda i: (i, 5)),
                  tile, tile, wspec, wspec, wspec],
        out_specs=[tile] * 5,
        compiler_params=_params(("arbitrary",)),
    )(x, mod, proj, proj, ya_in, yb_in, wg_pa, wg_pb, wg_out)


def _ffn_fwd(l, x1, mod, g2, wg_ff1, wg_ff2, t_len):
    n, d = x1.shape
    tm = min(TOKEN_TILE, t_len)
    tpb = t_len // tm
    hq = wg_ff1.shape[-1]
    hid = N_CHIP * hq

    def body(x_ref, mod_ref, g_ref, w1_ref, w2_ref, h_ref, f_ref, o2_ref, x2_ref, a2_s):
        h = _modnorm_fwd(x_ref[...], g_ref[...], mod_ref[4:5, :], mod_ref[3:4, :]).astype(MXU_DTYPE)
        h_ref[...] = h.astype(ACT_DTYPE)
        for q in range(N_CHIP):
            cols = slice(q * hq, (q + 1) * hq)
            f = jnp.dot(h, w1_ref[q], preferred_element_type=F32)
            f_ref[:, cols] = f.astype(ACT_DTYPE)
            a2_s[:, cols] = jnp.square(jnp.maximum(f, 0.0)).astype(MXU_DTYPE)
        o2 = jnp.dot(a2_s[...], w2_ref[...].reshape(hid, d), preferred_element_type=F32)
        o2_ref[...] = o2.astype(ACT_DTYPE)
        x2_ref[...] = x_ref[...] + mod_ref[5:6, :] * o2

    tile = pl.BlockSpec((tm, d), lambda i: (i, 0))
    return pl.pallas_call(
        body, name=f"ffn_fwd_{l}",
        out_shape=[jax.ShapeDtypeStruct((n, d), ACT_DTYPE), jax.ShapeDtypeStruct((n, hid), ACT_DTYPE),
                   jax.ShapeDtypeStruct((n, d), ACT_DTYPE), jax.ShapeDtypeStruct((n, d), F32)],
        grid=(n // tm,),
        in_specs=[tile, pl.BlockSpec((None, None, 8, d), lambda i: (l, i // tpb, 0, 0)),
                  pl.BlockSpec((None, 1, d), lambda i: (l, 0, 0)),
                  _resident((N_CHIP, d, hq)), _resident((N_CHIP, hq, d))],
        out_specs=[tile, pl.BlockSpec((tm, hid), lambda i: (i, 0)), tile, tile],
        scratch_shapes=[pltpu.VMEM((tm, hid), MXU_DTYPE)],
        compiler_params=_params(("arbitrary",)),
    )(x1, mod, g2, wg_ff1, wg_ff2)


def _loss_head(x, final_g, target):
    n, d = x.shape
    tm = min(TOKEN_TILE, n)

    def body(x_ref, g_ref, t_ref, loss_ref, dx_ref, dg_ref):
        @pl.when(pl.program_id(0) == 0)
        def _():
            loss_ref[...] = jnp.zeros_like(loss_ref)
            dg_ref[...] = jnp.zeros_like(dg_ref)

        x_t = x_ref[...]
        g = g_ref[...]
        r = lax.rsqrt(_rowmean(x_t * x_t) + EPS)
        xn = x_t * r
        e = xn * g - t_ref[...]
        loss_ref[...] += jnp.sum(e * e) * (0.5 / d)
        dy = e * (1.0 / d)
        dxn = dy * g
        dx_ref[...] = r * (dxn - xn * _rowmean(dxn * xn))
        dg_ref[0:1, :] += _colsum(dy * xn)

    tile = pl.BlockSpec((tm, d), lambda i: (i, 0))
    return pl.pallas_call(
        body, name="loss_head",
        out_shape=[jax.ShapeDtypeStruct((8, LANES), F32), jax.ShapeDtypeStruct((n, d), F32),
                   jax.ShapeDtypeStruct((8, d), F32)],
        grid=(n // tm,),
        in_specs=[tile, pl.BlockSpec((1, d), lambda i: (0, 0)), tile],
        out_specs=[pl.BlockSpec((8, LANES), lambda i: (0, 0)), tile, pl.BlockSpec((8, d), lambda i: (0, 0))],
        compiler_params=_params(("arbitrary",)),
    )(x, final_g, target)


def _norm_tail(x_ref, dxin_ref, dh, g_ref, sc, dx_ref, dmod_ref, dg_ref, row_sh, row_sc):
    dxm, dsh, q = _modnorm_bwd(x_ref[...], dh, g_ref[...], sc)
    dx_ref[...] = dxin_ref[...] + dxm
    dmod_ref[row_sh:row_sh + 1, :] += dsh
    dmod_ref[row_sc:row_sc + 1, :] += g_ref[...] * q
    dg_ref[0:1, :] += (1.0 + sc) * q


def _ffn_bwd(l, dx2, x1, mod, g2, o2, f, wg_ff1, wg_ff2, t_len, nb):
    n, d = dx2.shape
    tm = min(FFN_BWD_TILE, t_len)
    tpb = t_len // tm
    hq = wg_ff1.shape[-1]
    hid = N_CHIP * hq

    def body(dx2_ref, x1_ref, mod_ref, g_ref, o2_ref, f_ref, w1_ref, w2_ref,
             do2_ref, df_ref, dx1_ref, dmod_ref, dg_ref):
        i = pl.program_id(0)

        @pl.when(i == 0)
        def _():
            dg_ref[...] = jnp.zeros_like(dg_ref)

        @pl.when(i % tpb == 0)
        def _():
            dmod_ref[...] = jnp.zeros_like(dmod_ref)

        dx2_t = dx2_ref[...]
        dmod_ref[5:6, :] += _colsum(dx2_t * o2_ref[...].astype(F32))
        do2 = (dx2_t * mod_ref[5:6, :]).astype(MXU_DTYPE)
        do2_ref[...] = do2.astype(ACT_DTYPE)
        dh = jnp.zeros((tm, d), F32)
        for q in range(N_CHIP):
            cols = slice(q * hq, (q + 1) * hq)
            da2 = _dot_nt(do2, w2_ref[q])
            df = (da2 * (2.0 * jnp.maximum(f_ref[:, cols].astype(F32), 0.0))).astype(MXU_DTYPE)
            df_ref[:, cols] = df.astype(ACT_DTYPE)
            dh = dh + _dot_nt(df, w1_ref[q])
        _norm_tail(x1_ref, dx2_ref, dh, g_ref, mod_ref[4:5, :], dx1_ref, dmod_ref, dg_ref, 3, 4)

    tile = pl.BlockSpec((tm, d), lambda i: (i, 0))
    wide = pl.BlockSpec((tm, hid), lambda i: (i, 0))
    return pl.pallas_call(
        body, name=f"ffn_bwd_{l}",
        out_shape=[jax.ShapeDtypeStruct((n, d), ACT_DTYPE), jax.ShapeDtypeStruct((n, hid), ACT_DTYPE),
                   jax.ShapeDtypeStruct((n, d), F32), jax.ShapeDtypeStruct((nb, 8, d), F32),
                   jax.ShapeDtypeStruct((8, d), F32)],
        grid=(n // tm,),
        in_specs=[tile, tile, pl.BlockSpec((None, None, 8, d), lambda i: (l, i // tpb, 0, 0)),
                  pl.BlockSpec((None, 1, d), lambda i: (l, 0, 0)), tile, wide,
                  _resident((N_CHIP, d, hq)), _resident((N_CHIP, hq, d))],
        out_specs=[tile, wide, tile, pl.BlockSpec((None, 8, d), lambda i: (i // tpb, 0, 0)),
                   pl.BlockSpec((8, d), lambda i: (0, 0))],
        compiler_params=_params(("arbitrary",)),
    )(dx2, x1, mod, g2, o2, f, wg_ff1, wg_ff2)


def _merge_bwd(l, dx1, mod, o, ya, yb, proj, wg_pa, wg_pb, wg_out, t_len, nb, after):
    n, d = dx1.shape
    tm = min(TOKEN_TILE, t_len)
    tpb = t_len // tm
    rq = d // N_CHIP

    def body(dx_ref, mod_ref, o_ref, ya_ref, yb_ref, ga_ref, gb_ref, wpa_ref, wpb_ref, wo_ref, after_ref,
             do_ref, dya_ref, dyb_ref, dyai_ref, dybi_ref, dproj_ref, dmod_ref):
        i = pl.program_id(0)

        @pl.when(i % tpb == 0)
        def _():
            dmod_ref[...] = jnp.zeros_like(dmod_ref)

        dx = dx_ref[...]
        dmod_ref[2:3, :] += _colsum(dx * o_ref[...].astype(F32))
        do = (dx * mod_ref[2:3, :]).astype(MXU_DTYPE)
        do_ref[...] = do.astype(ACT_DTYPE)
        dm = _dot_nt(do, wo_ref[...].reshape(d, d))
        sa = _sigmoid(ga_ref[...].astype(F32))
        sb = _sigmoid(gb_ref[...].astype(F32))
        dya = (dm * sa).astype(MXU_DTYPE)
        dyb = (dm * sb).astype(MXU_DTYPE)
        dya_ref[...] = dya.astype(ACT_DTYPE)
        dyb_ref[...] = dyb.astype(ACT_DTYPE)
        dproj_ref[:, 0:d] = (dm * ya_ref[...].astype(F32) * sa * (1.0 - sa)).astype(ACT_DTYPE)
        dproj_ref[:, d:2 * d] = (dm * yb_ref[...].astype(F32) * sb * (1.0 - sb)).astype(ACT_DTYPE)
        dyai_ref[...] = _dot_nt(dya, wpa_ref[...].reshape(d, d)).astype(ACT_DTYPE)
        dybi_ref[...] = _dot_nt(dyb, wpb_ref[...].reshape(d, d)).astype(ACT_DTYPE)

    tile = pl.BlockSpec((tm, d), lambda i: (i, 0))
    wspec = pl.BlockSpec((N_CHIP, rq, d), lambda i: (0, 0, 0))
    return pl.pallas_call(
        body, name=f"merge_bwd_{l}",
        out_shape=[jax.ShapeDtypeStruct((n, d), ACT_DTYPE)] * 5
        + [jax.ShapeDtypeStruct((n, 6 * d), ACT_DTYPE), jax.ShapeDtypeStruct((nb, 8, d), F32)],
        grid=(n // tm,),
        in_specs=[tile, pl.BlockSpec((None, None, 8, d), lambda i: (l, i // tpb, 0, 0)), tile, tile, tile,
                  pl.BlockSpec((tm, d), lambda i: (i, 4)), pl.BlockSpec((tm, d), lambda i: (i, 5)),
                  wspec, wspec, wspec, ANY_SPEC],
        out_specs=[tile] * 5 + [pl.BlockSpec((tm, 2 * d), lambda i: (i, 2)),
                                pl.BlockSpec((None, 8, d), lambda i: (i // tpb, 0, 0))],
        compiler_params=_params(("arbitrary",)),
    )(dx1, mod, o, ya, yb, proj, proj, wg_pa, wg_pb, wg_out, after)


def _branches_bwd(l, proj, zc, dya_in, dyb_in, dproj, lng, lnb, ws, bst, cw, blg, blb, t_len, after):
    n = proj.shape[0]
    d = lng.shape[-1]
    tm = min(TOKEN_TILE, t_len)
    tpb = t_len // tm
    per = tm // HALO
    nchunk = tm // CHUNK
    ntile = n // tm

    def body(u_ref, v_ref, a_ref, g_ref, ah_ref, gh_ref, zc_ref, zcn_ref, dya_ref, dyb_ref, dybn_ref, dproj_in,
             lng_ref, lnb_ref, ws_ref, bst_ref, cw_ref, blg_ref, blb_ref, after_ref,
             dproj_ref, dws_ref, dbst_ref, dcw_ref, vec_ref, wm_s, zext, dzext, dz3, dvn_s):
        i = pl.program_id(0)

        @pl.when(i == 0)
        def _():
            dws_ref[...] = jnp.zeros_like(dws_ref)
            dbst_ref[...] = jnp.zeros_like(dbst_ref)
            dcw_ref[...] = jnp.zeros_like(dcw_ref)
            vec_ref[...] = jnp.zeros_like(vec_ref)

        _masked_ws(ws_ref, wm_s)
        _fill_z(i, tpb, ah_ref, gh_ref, zext)

        def conv_ln_bwd(zc_t, dyb_t):
            zhat, rstd = _ln_stats(zc_t)
            zn = zhat * blg_ref[...] + blb_ref[...]
            sg = _sigmoid(zn)
            dzn = dyb_t * (sg * (1.0 + zn * (1.0 - sg)))
            return _ln_bwd(dzn, zhat, rstd, blg_ref[...]), _colsum(dzn * zhat), _colsum(dzn)

        def chunk(c, carry):
            r0 = pl.multiple_of(c * CHUNK, CHUNK)
            rows = pl.ds(r0, CHUNK)
            vhat, rstd = _ln_stats(v_ref[rows, :].astype(F32))
            vn = (vhat * lng_ref[...] + lnb_ref[...]).astype(MXU_DTYPE)
            u = u_ref[rows, :].astype(F32)
            dya = dya_ref[rows, :].astype(F32)
            for h in range(HEADS):
                cols = slice(h * CHUNK, (h + 1) * CHUNK)
                s = jnp.dot(wm_s[h], vn[:, cols], preferred_element_type=F32) + bst_ref[:, h:h + 1]
                dproj_ref[rows, cols] = (dya[:, cols] * s).astype(ACT_DTYPE)
                ds = dya[:, cols] * u[:, cols]
                dvn_s[:, cols] = _dot_tn(wm_s[h], ds)
                dws_ref[h] += _dot_nt(ds, vn[:, cols])
                dbst_ref[:, h:h + 1] += jnp.sum(ds, axis=1, keepdims=True)
            dvn = dvn_s[...]
            dproj_ref[rows, d:2 * d] = _ln_bwd(dvn, vhat, rstd, lng_ref[...]).astype(ACT_DTYPE)
            vec_ref[0:1, :] += _colsum(dvn * vhat)
            vec_ref[1:2, :] += _colsum(dvn)
            a = a_ref[rows, :].astype(F32)
            g = g_ref[rows, :].astype(F32)
            _put_lanes(zext, pl.ds(HALO + r0, CHUNK), a * _sigmoid(g))
            dzc, dblg, dblb = conv_ln_bwd(zc_ref[rows, :].astype(F32), dyb_ref[rows, :].astype(F32))
            _put_lanes(dzext, rows, dzc)
            vec_ref[2:3, :] += _colsum(dzc)
            vec_ref[3:4, :] += dblg
            vec_ref[4:5, :] += dblb
            return carry

        lax.fori_loop(0, nchunk, chunk, 0)

        dzc_next, _, _ = conv_ln_bwd(zcn_ref[...].astype(F32), dybn_ref[...].astype(F32))
        _put_lanes(dzext, slice(tm, tm + HALO), dzc_next * jnp.where(i % tpb == tpb - 1, 0.0, 1.0))

        def lane_chunk_dz(lc, carry):
            _conv_taps(dzext, cw_ref, dz3, lc, tm, flip=True)
            return carry

        lax.fori_loop(0, d // LANES, lane_chunk_dz, 0)

        def lane_chunk(lc, carry):
            accs = [jnp.zeros((8, LANES), F32) for _ in range(CONV_TAPS)]
            for b in range(tm // TAP_GRAD_ROWS):
                dzc = dzext[lc, b * TAP_GRAD_ROWS:(b + 1) * TAP_GRAD_ROWS, :]
                for k, win in _tap_windows(zext, lc, b * TAP_GRAD_ROWS, TAP_GRAD_ROWS, flip=False):
                    prod = dzc * win
                    part = prod[0:8]
                    for e in range(1, TAP_GRAD_ROWS // 8):
                        part = part + prod[8 * e:8 * e + 8]
                    accs[k] = accs[k] + part
            for k in range(CONV_TAPS):
                dcw_ref[lc, k:k + 1, :] += _colsum(accs[k])
            return carry

        lax.fori_loop(0, d // LANES, lane_chunk, 0)

        def glu_bwd(c, carry):
            r0 = pl.multiple_of(c * CHUNK, CHUNK)
            rows = pl.ds(r0, CHUNK)
            for lc in range(d // LANES):
                lanes = slice(lc * LANES, (lc + 1) * LANES)
                dz = dz3[lc, rows, :]
                a = a_ref[rows, lanes].astype(F32)
                sg = _sigmoid(g_ref[rows, lanes].astype(F32))
                dproj_ref[rows, 2 * d + lc * LANES:2 * d + (lc + 1) * LANES] = (dz * sg).astype(ACT_DTYPE)
                dproj_ref[rows, 3 * d + lc * LANES:3 * d + (lc + 1) * LANES] = (
                    dz * a * sg * (1.0 - sg)).astype(ACT_DTYPE)
            return carry

        lax.fori_loop(0, nchunk, glu_bwd, 0)

        @pl.when(i == ntile - 1)
        def _():
            mask = _causal_mask()
            for h in range(HEADS):
                dws_ref[h] = jnp.where(mask, dws_ref[h], 0.0)

    col = lambda k: pl.BlockSpec((tm, d), lambda i: (i, k))
    tile = pl.BlockSpec((tm, d), lambda i: (i, 0))
    before = lambda k: pl.BlockSpec((HALO, d), lambda i: (jnp.maximum(i * per - 1, 0), k))
    following = pl.BlockSpec((HALO, d), lambda i: (jnp.minimum((i + 1) * per, n // HALO - 1), 0))
    vec = pl.BlockSpec((None, 1, d), lambda i: (l, 0, 0))
    const2 = lambda r, c: pl.BlockSpec((r, c), lambda i: (0, 0))
    return pl.pallas_call(
        body, name=f"branches_bwd_{l}",
        out_shape=[jax.ShapeDtypeStruct((n, 6 * d), ACT_DTYPE), jax.ShapeDtypeStruct((HEADS, CHUNK, CHUNK), F32),
                   jax.ShapeDtypeStruct((CHUNK, HEADS), F32), jax.ShapeDtypeStruct((d // LANES, HALO, LANES), F32),
                   jax.ShapeDtypeStruct((8, d), F32)],
        grid=(ntile,),
        in_specs=[col(0), col(1), col(2), col(3), before(2), before(3), tile, following, tile, tile, following,
                  pl.BlockSpec(memory_space=pl.ANY), vec, vec,
                  pl.BlockSpec((None, HEADS, CHUNK, CHUNK), lambda i: (l, 0, 0, 0)),
                  pl.BlockSpec((None, CHUNK, HEADS), lambda i: (l, 0, 0)),
                  pl.BlockSpec((None, d // LANES, HALO, LANES), lambda i: (l, 0, 0, 0)), vec, vec, ANY_SPEC],
        out_specs=[pl.BlockSpec((tm, 4 * d), lambda i: (i, 0)),
                   pl.BlockSpec((HEADS, CHUNK, CHUNK), lambda i: (0, 0, 0)),
                   const2(CHUNK, HEADS), pl.BlockSpec((d // LANES, HALO, LANES), lambda i: (0, 0, 0)), const2(8, d)],
        scratch_shapes=[pltpu.VMEM((HEADS, CHUNK, CHUNK), MXU_DTYPE),
                        pltpu.VMEM((d // LANES, HALO + tm, LANES), F32),
                        pltpu.VMEM((d // LANES, tm + HALO, LANES), F32),
                        pltpu.VMEM((d // LANES, tm, LANES), F32), pltpu.VMEM((CHUNK, d), F32)],
        input_output_aliases={11: 0},
        compiler_params=_params(("arbitrary",)),
    )(proj, proj, proj, proj, proj, proj, zc, zc, dya_in, dyb_in, dyb_in, dproj, lng, lnb, ws, bst, cw, blg, blb, after)


def _in_proj_bwd(l, dproj, dx1, x, mod, g1, wg_in, t_len, nb, after):
    n, d = x.shape
    tm = min(TOKEN_TILE, t_len)
    tpb = t_len // tm
    qc = wg_in.shape[-1]

    def body(dp_ref, dx1_ref, x_ref, mod_ref, g_ref, w_ref, after_ref, dx_ref, dmod_ref, dg_ref):
        i = pl.program_id(0)

        @pl.when(i == 0)
        def _():
            dg_ref[...] = jnp.zeros_like(dg_ref)

        @pl.when(i % tpb == 0)
        def _():
            dmod_ref[...] = jnp.zeros_like(dmod_ref)

        dh = jnp.zeros((tm, d), F32)
        for q in range(N_CHIP):
            dh = dh + _dot_nt(dp_ref[:, q * qc:(q + 1) * qc], w_ref[q])
        _norm_tail(x_ref, dx1_ref, dh, g_ref, mod_ref[1:2, :], dx_ref, dmod_ref, dg_ref, 0, 1)

    tile = pl.BlockSpec((tm, d), lambda i: (i, 0))
    return pl.pallas_call(
        body, name=f"in_proj_bwd_{l}",
        out_shape=[jax.ShapeDtypeStruct((n, d), F32), jax.ShapeDtypeStruct((nb, 8, d), F32),
                   jax.ShapeDtypeStruct((8, d), F32)],
        grid=(n // tm,),
        in_specs=[pl.BlockSpec((tm, N_CHIP * qc), lambda i: (i, 0)), tile, tile,
                  pl.BlockSpec((None, None, 8, d), lambda i: (l, i // tpb, 0, 0)),
                  pl.BlockSpec((None, 1, d), lambda i: (l, 0, 0)),
                  _resident((N_CHIP, d, qc)), ANY_SPEC],
        out_specs=[tile, pl.BlockSpec((None, 8, d), lambda i: (i // tpb, 0, 0)),
                   pl.BlockSpec((8, d), lambda i: (0, 0))],
        compiler_params=_params(("arbitrary",)),
    )(dproj, dx1, x, mod, g1, wg_in, after)


def _weight_grad(name, a, b, a_spec, b_spec, out_rows, out_spec, acc_shape, grid_ij, relu2=False):
    n = a.shape[0]
    tk = min(MATMUL_TILE, n)
    nk = n // tk
    cols = acc_shape[1]

    def body(a_ref, b_ref, o_ref, acc):
        k = pl.program_id(2)

        @pl.when(k == 0)
        def _():
            acc[...] = jnp.zeros_like(acc)

        a_t = a_ref[...]
        if relu2:
            a_t = jnp.square(jnp.maximum(a_t.astype(F32), 0.0))
        acc[...] += _dot_tn(a_t, b_ref[...])

        @pl.when(k == nk - 1)
        def _():
            o_ref[...] = acc[...].reshape(o_ref.shape).astype(WIRE_DTYPE)

    gi, gj = grid_ij
    return pl.pallas_call(
        body, name=name, out_shape=jax.ShapeDtypeStruct((N_CHIP, out_rows, cols), WIRE_DTYPE),
        grid=(gi, gj, nk),
        in_specs=[a_spec(tk), b_spec(tk)],
        out_specs=out_spec,
        scratch_shapes=[pltpu.VMEM(acc_shape, F32)],
        compiler_params=_params(("arbitrary", "arbitrary", "arbitrary")),
    )(a, b)


def _row_tile(rows, cols, arrays):
    budget = VMEM_LIMIT // 3
    t = budget // (arrays * 2 * cols * 4)
    t = max(8, min(rows, t // 8 * 8))
    while rows % t:
        t -= 8
    return t


def _sum_partials(name, own, got, myq, l, nl, prev):
    _, rows, cols = own.shape
    tr = _row_tile(rows, cols, 3)
    nt = rows // tr

    def body(q_ref, own_ref, got_ref, *rest):
        o_ref = rest[-1]
        acc = own_ref[...].astype(F32)
        for k in range(3):
            acc = acc + got_ref[k].astype(F32)
        o_ref[...] = acc

    operands = [myq, own, got] + ([] if prev is None else [prev])
    return pl.pallas_call(
        body, name=name, out_shape=jax.ShapeDtypeStruct((nl * rows, cols), F32),
        grid_spec=pltpu.PrefetchScalarGridSpec(
            num_scalar_prefetch=1, grid=(nt,),
            in_specs=[pl.BlockSpec((None, tr, cols), lambda i, q: (q[0], i, 0)),
                      pl.BlockSpec((3, tr, cols), lambda i, q: (0, i, 0))]
            + ([] if prev is None else [pl.BlockSpec(memory_space=pl.ANY)]),
            out_specs=pl.BlockSpec((tr, cols), lambda i, q: (l * nt + i, 0))),
        input_output_aliases={} if prev is None else {3: 0},
        compiler_params=_params(("arbitrary",)),
    )(*operands)


def _adamw(name, w, m, v, g_a, g_b=None):
    rows, cols = w.shape
    tr = _row_tile(rows, cols, 9)
    c1 = 1.0 - ADAM_B1 ** ADAM_STEP
    c2 = 1.0 - ADAM_B2 ** ADAM_STEP

    def body(*refs):
        if g_b is None:
            w_ref, m_ref, v_ref, ga_ref, g_ref, d_ref, m2_ref, v2_ref = refs
            g = ga_ref[...]
        else:
            w_ref, m_ref, v_ref, ga_ref, gb_ref, g_ref, d_ref, m2_ref, v2_ref = refs
            g = ga_ref[...] + gb_ref[...]
        m2 = ADAM_B1 * m_ref[...] + (1.0 - ADAM_B1) * g
        v2 = ADAM_B2 * v_ref[...] + (1.0 - ADAM_B2) * (g * g)
        g_ref[...] = g
        m2_ref[...] = m2
        v2_ref[...] = v2
        d_ref[...] = -ADAM_LR * ((m2 / c1) / (jnp.sqrt(v2 / c2) + ADAM_EPS) + ADAM_WD * w_ref[...])

    tile = pl.BlockSpec((tr, cols), lambda i: (i, 0))
    operands = [w, m, v, g_a] + ([] if g_b is None else [g_b])
    return pl.pallas_call(
        body, name=name, out_shape=[jax.ShapeDtypeStruct((rows, cols), F32)] * 4,
        grid=(rows // tr,), in_specs=[tile] * len(operands), out_specs=[tile] * 4,
        compiler_params=_params(("arbitrary",)),
    )(*operands)


def _pack(parts):
    flat = [p.reshape(-1, LANES) for p in parts]
    for f in flat:
        assert f.shape[0] % 8 == 0
    return jnp.concatenate(flat, axis=0)


def _unpack(packed, shapes):
    out, r = [], 0
    for s in shapes:
        size = 1
        for e in s:
            size *= e
        rows = size // LANES
        out.append(packed[r:r + rows].reshape(s))
        r += rows
    return out


def kernel(x, c, w_ada, b_ada, norm1_g, w_in, a_ln_g, a_ln_b, a_ws, a_bs, w_pa, b_conv_w, b_conv_b, b_ln_g, b_ln_b, w_pb, w_out, norm2_g, w_ff1, w_ff2, final_g, loss_target, m_w_ada, m_b_ada, m_norm1_g, m_w_in, m_a_ln_g, m_a_ln_b, m_a_ws, m_a_bs, m_w_pa, m_b_conv_w, m_b_conv_b, m_b_ln_g, m_b_ln_b, m_w_pb, m_w_out, m_norm2_g, m_w_ff1, m_w_ff2, m_final_g, v_w_ada, v_b_ada, v_norm1_g, v_w_in, v_a_ln_g, v_a_ln_b, v_a_ws, v_a_bs, v_w_pa, v_b_conv_w, v_b_conv_b, v_b_ln_g, v_b_ln_b, v_w_pb, v_w_out, v_norm2_g, v_w_ff1, v_w_ff2, v_final_g):
    nb, t_len, d = x.shape
    nl = w_in.shape[0]
    n = nb * t_len
    cq = w_ada.shape[-1]
    cc = d // N_CHIP
    mx, my, mc = _my_place()
    myq = (2 * mx + my).astype(jnp.int32).reshape(1)
    me = 4 * mx + 2 * my + mc

    def exchange_start(name, mode, src, after):
        land = lax.empty((N_DEV,) + src.shape[-2:], src.dtype)
        send_sems, recv_sems, srcs, lands, tok_out = _split_start(name + "_start", mode, [src], [land], after)
        return (name, mode, send_sems, recv_sems, srcs, lands), tok_out

    def exchange_wait(handle, after):
        name, mode, send_sems, recv_sems, srcs, lands = handle
        srcs, lands = _split_wait(name + "_wait", mode, send_sems, recv_sems, srcs, lands, after)
        own = lax.dynamic_slice_in_dim(srcs[0], me, 1, axis=0) if mode == "exchange" else srcs[0][None]
        return lax.dynamic_update_slice_in_dim(lands[0], own, me, axis=0)

    taps = jnp.pad(b_conv_w.reshape(nl, CONV_TAPS, cc), ((0, 0), (0, HALO - CONV_TAPS), (0, 0)))
    first = jnp.concatenate([jnp.pad(c, ((0, 8 - nb), (0, 0))), taps.reshape(nl * HALO * cc // d, d)], axis=0)
    first = _all_to_all(jnp.broadcast_to(first[None], (N_DEV,) + first.shape), "gather_c_and_taps")
    c_all = first[:, :nb].reshape(N_DEV * nb, d)
    cwg = first[:, 8:].reshape(N_CHIP, 2, nl, HALO, cc)[:, 0]
    cw = cwg.transpose(1, 2, 0, 3).reshape(nl, HALO, d)
    cw = cw.reshape(nl, HALO, d // LANES, LANES).transpose(0, 2, 1, 3)
    mod_part = _ada_forward(c_all, w_ada, b_ada.reshape(nl, 1, N_CHIP * cq), myq)
    mod_slots = mod_part.reshape(nl, N_DEV, nb, cq).transpose(1, 0, 2, 3).reshape(N_DEV, nl * nb, cq)
    mod_handle, mod_token = exchange_start("exchange_mod", "exchange", mod_slots, jnp.zeros((8, LANES), F32))
    w_in0 = lax.dynamic_update_index_in_dim(lax.empty((N_CHIP,) + w_in.shape[1:], WIRE_DTYPE),
                                            (w_in[0] - mod_token[0, 0]).astype(WIRE_DTYPE), myq[0], 0)
    mod_got = exchange_wait(mod_handle, w_in0).reshape(N_CHIP, 2, nl, nb, cq)[:, 0]
    mod6 = mod_got.transpose(1, 2, 0, 3).reshape(nl, nb, 6, d)
    mod = jnp.pad(mod6, ((0, 0), (0, 0), (0, 2), (0, 0)))

    big = ["w_in", "w_pa", "w_pb", "w_out", "w_ff1", "w_ff2"]
    ws_given = dict(w_in=(w_in, m_w_in, v_w_in), w_pa=(w_pa, m_w_pa, v_w_pa), w_pb=(w_pb, m_w_pb, v_w_pb),
                    w_out=(w_out, m_w_out, v_w_out), w_ff1=(w_ff1, m_w_ff1, v_w_ff1), w_ff2=(w_ff2, m_w_ff2, v_w_ff2))

    def own_slot(w_l, after=None):
        if after is not None:
            w_l = w_l - after[0, 0]
        empty = lax.empty((N_CHIP,) + w_l.shape, WIRE_DTYPE)
        return lax.dynamic_update_index_in_dim(empty, w_l.astype(WIRE_DTYPE), myq[0], 0)

    def zero_after(*arrays):
        z = jnp.zeros((8, LANES), F32)
        for a in arrays:
            piece = a.reshape(-1, a.shape[-1])[:8, :LANES]
            z = z + jnp.where(jnp.isfinite(piece), piece, 0.0) * 0.0
        return z

    first_sems = _split_start("gather_start_in_0", "gather", [], [w_in0], zero_after(cw, mod[:, 0]))
    token = first = first_sems[4]
    gathers = []
    for l in range(nl):
        group = big[1:] if l == 0 else big
        send_sems, recv_sems, _, lands, token = _split_start(
            f"gather_start_{l}", "gather", [], [own_slot(ws_given[k][0][l], first) for k in group], token)
        if l == 0:
            send_sems = list(first_sems[0]) + list(send_sems)
            recv_sems = list(first_sems[1]) + list(recv_sems)
            lands = list(first_sems[3]) + list(lands)
        gathers.append((send_sems, recv_sems, lands))
    mod = mod + token[0, 0]

    def gather_wait(l, part, lo, hi, after):
        send_sems, recv_sems, lands = gathers[l]
        return _split_wait(f"gather_wait_{part}_{l}", "gather", send_sems[lo:hi], recv_sems[lo:hi], [],
                           lands[lo:hi], after)[1]

    vec3 = lambda p: p.reshape(nl, 1, d)
    g1, g2 = vec3(norm1_g), vec3(norm2_g)
    lng, lnb, cb, blg, blb = vec3(a_ln_g), vec3(a_ln_b), vec3(b_conv_b), vec3(b_ln_g), vec3(b_ln_b)
    bst = a_bs.transpose(0, 2, 1)

    xs = x.reshape(n, d)
    saved = []
    weights = []
    for l in range(nl):
        if l == 0:
            send_sems, recv_sems, lands = gathers[0]
            wg_in = lands[0]
            h, proj = _in_proj_quarter("in_proj_0_own", l, xs, mod, g1, wg_in, myq, t_len)
            for k, (px, py) in enumerate(_other_chips(mx, my)):
                (wg_in,) = _split_wait(f"gather_wait_in_0_{k}", "gather", send_sems[:1], recv_sems[:1], [], [wg_in],
                                       proj, peers=(k,))[1]
                quarter = (2 * px + py).astype(jnp.int32).reshape(1)
                proj = _in_proj_quarter(f"in_proj_0_{k}", l, h, None, None, wg_in, quarter, t_len, proj)
        else:
            (wg_in,) = gather_wait(l, "in", 0, 1, xs)
            h, proj = _in_proj(l, xs, mod, g1, wg_in, t_len)
        ya_in, yb_in, zc = _branches_fwd(l, proj, lng, lnb, a_ws, bst, cw, cb, blg, blb, t_len)
        wg_pa, wg_pb, wg_out = gather_wait(l, "mid", 1, 4, ya_in)
        ya, yb, merged, o, x1 = _merge_out(l, xs, mod, proj, ya_in, yb_in, wg_pa, wg_pb, wg_out, t_len)
        wg_ff1, wg_ff2 = gather_wait(l, "ffn", 4, 6, x1)
        h2, f, o2, x2 = _ffn_fwd(l, x1, mod, g2, wg_ff1, wg_ff2, t_len)
        saved.append((xs, h, proj, ya_in, yb_in, zc, ya, yb, merged, o, x1, h2, f, o2))
        weights.append((wg_in, wg_pa, wg_pb, wg_out, wg_ff1, wg_ff2))
        xs = x2

    loss_blk, dx, dfinal = _loss_head(xs, final_g.reshape(1, d), loss_target.reshape(n, d))

    tok = lambda w: (lambda tk: pl.BlockSpec((tk, w), lambda i, j, k: (k, 0)))
    tok_i = lambda w: (lambda tk: pl.BlockSpec((tk, w), lambda i, j, k: (k, i)))
    tok_j = lambda w: (lambda tk: pl.BlockSpec((tk, w), lambda i, j, k: (k, j)))
    qin = weights[0][0].shape[-1]
    hq = weights[0][4].shape[-1]
    rq = d // N_CHIP
    slot_i = lambda r, cdim: pl.BlockSpec((None, r, cdim), lambda i, j, k: (i, 0, 0))
    slot_j = lambda r, cdim: pl.BlockSpec((None, r, cdim), lambda i, j, k: (j, 0, 0))
    all_slots = pl.BlockSpec((N_CHIP, rq, d), lambda i, j, k: (0, 0, 0))
    scatters = []

    def scatter_start(l, part, names, grads, after):
        lands = [lax.empty((3,) + g.shape[1:], g.dtype) for g in grads]
        send_sems, recv_sems, srcs, lands, tok_out = _split_start(f"scatter_start_{part}_{l}", "scatter", grads, lands,
                                                                  after)
        scatters.append((f"scatter_wait_{part}_{l}", l, names, send_sems, recv_sems, srcs, lands))
        return tok_out

    dmods, small = [None] * nl, [None] * nl
    for l in reversed(range(nl)):
        x0, h, proj, ya_in, yb_in, zc, ya, yb, merged, o, x1, h2, f, o2 = saved[l]
        wg_in, wg_pa, wg_pb, wg_out, wg_ff1, wg_ff2 = weights[l]
        do2, df, dx1, dmod_c, dg2 = _ffn_bwd(l, dx, x1, mod, g2, o2, f, wg_ff1, wg_ff2, t_len, nb)
        g_ff2 = _weight_grad(f"grad_w_ff2_{l}", f, do2, tok_i(hq), tok(d), hq, slot_i(hq, d), (hq, d), (N_CHIP, 1),
                             relu2=True)
        g_ff1 = _weight_grad(f"grad_w_ff1_{l}", h2, df, tok(d), tok_j(hq), d, slot_j(d, hq), (d, hq), (1, N_CHIP))
        if l == 0:
            token = scatter_start(l, "ffn", ["w_ff2", "w_ff1"], [g_ff2, g_ff1], token)
        do, dya, dyb, dya_in, dyb_in, dproj, dmod_b = _merge_bwd(l, dx1, mod, o, ya, yb, proj, wg_pa, wg_pb, wg_out,
                                                                 t_len, nb, token)
        g_out = _weight_grad(f"grad_w_out_{l}", merged, do, tok(d), tok(d), rq, all_slots, (d, d), (1, 1))
        g_pa = _weight_grad(f"grad_w_pa_{l}", ya_in, dya, tok(d), tok(d), rq, all_slots, (d, d), (1, 1))
        g_pb = _weight_grad(f"grad_w_pb_{l}", yb_in, dyb, tok(d), tok(d), rq, all_slots, (d, d), (1, 1))
        if l == 0:
            token = scatter_start(l, "mid", ["w_out", "w_pa", "w_pb"], [g_out, g_pa, g_pb], token)
        dproj, dws, dbst, dcw, vecs = _branches_bwd(l, proj, zc, dya_in, dyb_in, dproj, lng, lnb, a_ws, bst, cw,
                                                    blg, blb, t_len, token)
        g_in = _weight_grad(f"grad_w_in_{l}", h, dproj, tok(d), tok_j(qin), d, slot_j(d, qin), (d, qin), (1, N_CHIP))
        if l == 0:
            token = scatter_start(l, "in", ["w_in"], [g_in], token)
        else:
            token = scatter_start(l, "all", ["w_ff2", "w_ff1", "w_out", "w_pa", "w_pb", "w_in"],
                                  [g_ff2, g_ff1, g_out, g_pa, g_pb, g_in], token)
        dx, dmod_a, dg1 = _in_proj_bwd(l, dproj, dx1, x0, mod, g1, wg_in, t_len, nb, token)
        dmods[l] = jnp.concatenate([dmod_a[:, 0:2], dmod_b[:, 2:3], dmod_c[:, 3:6]], axis=1)
        dcw = dcw.transpose(1, 0, 2).reshape(HALO, d)[:CONV_TAPS]
        small[l] = (dg1[0], vecs[0], vecs[1], dws, dbst.T, dcw, vecs[2], vecs[3], vecs[4], dg2[0])
    grad_x = dx.reshape(nb, t_len, d)

    names = ["norm1_g", "a_ln_g", "a_ln_b", "a_ws", "a_bs", "b_conv_w", "b_conv_b", "b_ln_g", "b_ln_b", "norm2_g"]
    stacked = [jnp.stack([small[l][k] for l in range(nl)]) for k in range(len(names))]
    stacked[5] = jnp.pad(stacked[5], ((0, 0), (0, HALO - CONV_TAPS), (0, 0)))
    stacked += [dfinal, loss_blk]
    part_shapes = [s.shape for s in stacked]
    packed = _pack(stacked)
    prow = packed.shape[0]
    pad_rows = (-prow) % (8 * N_DEV)
    packed = jnp.pad(packed, ((0, pad_rows), (0, 0)))
    srow = packed.shape[0] // N_DEV
    half = dict.fromkeys(big)

    def sum_arrived(entries, after):
        last = after
        for name, l, group, send_sems, recv_sems, srcs, lands in entries:
            srcs, lands = _split_wait(name, "scatter", send_sems, recv_sems, srcs, lands, after)
            for k, g_own, g_got in zip(group, srcs, lands):
                last = half[k] = _sum_partials(f"sum_{k}_{l}", g_own, g_got, myq, l, nl, half[k])
        return last

    early = max(1, (nl - 1) * 2 // 3)
    reduce_handle, token = exchange_start("reduce_small", "exchange", packed.reshape(N_DEV, srow, LANES), token)
    sum_arrived(scatters[:early], token)
    mine = _sum_slots(exchange_wait(reduce_handle, [h for h in half.values() if h is not None]), "sum_small")
    dmod_rows = nl * nb * 6 * d // LANES
    second = jnp.concatenate([mine, jnp.stack(dmods).reshape(dmod_rows, LANES)], axis=0)
    gather_handle, token = exchange_start("gather_small_and_dmod", "allgather", second, token)
    sum_arrived(scatters[early:], token)
    sums = [half[k] for k in big]
    swap_send, swap_recv, sums, others, token = _split_start(
        "swap_start", "swap", sums, [lax.empty(s.shape, s.dtype) for s in sums], token)
    second = exchange_wait(gather_handle, token)
    total = second[:, :srow].reshape(N_DEV * srow, LANES)[:prow]
    dmod_all = second[:, srow:].reshape(N_DEV, nl, nb, 6 * d).transpose(1, 0, 2, 3).reshape(nl, N_DEV * nb, 6 * d)
    g_w_ada, g_b_ada = _ada_backward(c_all, dmod_all, myq, cq, token)

    sg = dict(zip(names + ["final_g", "loss"], _unpack(total, part_shapes)))
    loss = sg["loss"][0, 0]
    sg["b_conv_w"] = lax.dynamic_slice_in_dim(sg["b_conv_w"][:, :CONV_TAPS], myq[0] * cc, cc, axis=2).reshape(
        nl, CONV_TAPS, 1, cc)
    sg["final_g"] = sg["final_g"][0]
    sg["b_ada"] = g_b_ada.reshape(nl, N_CHIP * cq)
    small_names = ["b_ada", "norm1_g", "a_ln_g", "a_ln_b", "a_ws", "a_bs", "b_conv_w", "b_conv_b", "b_ln_g",
                   "b_ln_b", "norm2_g", "final_g"]
    given = dict(b_ada=(b_ada, m_b_ada, v_b_ada), norm1_g=(norm1_g, m_norm1_g, v_norm1_g),
                 a_ln_g=(a_ln_g, m_a_ln_g, v_a_ln_g), a_ln_b=(a_ln_b, m_a_ln_b, v_a_ln_b),
                 a_ws=(a_ws, m_a_ws, v_a_ws), a_bs=(a_bs, m_a_bs, v_a_bs),
                 b_conv_w=(b_conv_w, m_b_conv_w, v_b_conv_w), b_conv_b=(b_conv_b, m_b_conv_b, v_b_conv_b),
                 b_ln_g=(b_ln_g, m_b_ln_g, v_b_ln_g), b_ln_b=(b_ln_b, m_b_ln_b, v_b_ln_b),
                 norm2_g=(norm2_g, m_norm2_g, v_norm2_g), final_g=(final_g, m_final_g, v_final_g))

    def padded(a):
        rows = -(-a.size // (8 * LANES)) * 8
        return jnp.pad(a.reshape(-1), (0, rows * LANES - a.size)).reshape(rows, LANES)

    packs = [_pack([padded(given[k][j]) for k in small_names]) for j in range(3)]
    gpack = _pack([padded(sg[k].astype(F32)) for k in small_names])
    res_small = _adamw("adamw_small", packs[0], packs[1], packs[2], gpack)
    out = {}
    for j, kind in enumerate(["grad", "delta", "new_m", "new_v"]):
        r = 0
        for k in small_names:
            a = given[k][0]
            rows = -(-a.size // (8 * LANES)) * 8
            out[(kind, k)] = res_small[j][r:r + rows].reshape(-1)[:a.size].reshape(a.shape)
            r += rows

    res = _adamw("adamw_w_ada", w_ada.reshape(nl * d, cq), m_w_ada.reshape(nl * d, cq), v_w_ada.reshape(nl * d, cq),
                 g_w_ada.reshape(nl * d, cq))
    for kind, r in zip(["grad", "delta", "new_m", "new_v"], res):
        out[(kind, "w_ada")] = r.reshape(w_ada.shape)

    sums, others = _split_wait("swap_wait", "swap", swap_send, swap_recv, sums, others, res[0])
    for k, s_mine, s_other in zip(big, sums, others):
        w, m, v = ws_given[k]
        cols = w.shape[-1]
        res = _adamw(f"adamw_{k}", w.reshape(-1, cols), m.reshape(-1, cols), v.reshape(-1, cols), s_mine, s_other)
        for kind, r in zip(["grad", "delta", "new_m", "new_v"], res):
            out[(kind, k)] = r.reshape(w.shape)

    order = ["w_ada", "b_ada", "norm1_g", "w_in", "a_ln_g", "a_ln_b", "a_ws", "a_bs", "w_pa", "b_conv_w", "b_conv_b",
             "b_ln_g", "b_ln_b", "w_pb", "w_out", "norm2_g", "w_ff1", "w_ff2", "final_g"]
    return (loss, grad_x, *[out[("grad", k)] for k in order], *[out[("delta", k)] for k in order],
            *[out[("new_m", k)] for k in order], *[out[("new_v", k)] for k in order])
```
